```python
import jax, jax.numpy as jnp
from jax import lax
import numpy as np

D_MODEL = 1024
BATCH = 32
SEQ = 2048
DEPTH = 1

CHUNK = 64
RET_HEADS = 4
RET_DK = 128
RET_DV = 256
RET_QK_WIDTH = RET_HEADS * RET_DK
RET_V_WIDTH = RET_HEADS * RET_DV
LRU_WIDTH = 1024
LRU_BLOCKS = 4
LRU_BLOCK = LRU_WIDTH // LRU_BLOCKS
LRU_CONV = 4
LRU_C = 8.0
LRU_MIN_RAD = 0.9
LRU_MAX_RAD = 0.999
D_FF = 3 * D_MODEL
FFN_CONV = 3
ROPE_BASE = 10000.0
RMS_EPS = 1e-6
GN_EPS = 1e-6

IN_SIZES = (RET_QK_WIDTH, RET_QK_WIDTH, RET_V_WIDTH, RET_V_WIDTH,
            LRU_WIDTH, LRU_WIDTH, D_MODEL, D_MODEL)
D_IN = sum(IN_SIZES)
SPLIT_POINTS = tuple(sum(IN_SIZES[:i + 1]) for i in range(len(IN_SIZES) - 1))

kernel_name = "chunk_causal_retention_rglru_gated_hybrid"


def rms_norm(x, w):
    x32 = x.astype(jnp.float32)
    y = x32 * lax.rsqrt(jnp.mean(x32 * x32, axis=-1, keepdims=True) + RMS_EPS)
    return (y * w.astype(jnp.float32)).astype(x.dtype)


def causal_depthwise_conv(x, w, b):
    k_width, channels = w.shape
    y = lax.conv_general_dilated(
        x, w[:, None, :].astype(x.dtype), window_strides=(1,),
        padding=[(k_width - 1, 0)], dimension_numbers=("NWC", "WIO", "NWC"),
        feature_group_count=channels)
    return y + b.astype(x.dtype)


def rotary(x, positions):
    half = x.shape[-1] // 2
    inv_freq = ROPE_BASE ** (-jnp.arange(half, dtype=jnp.float32) / half)
    ang = positions.astype(jnp.float32)[..., None] * inv_freq
    cos = jnp.cos(ang)[:, :, None, :]
    sin = jnp.sin(ang)[:, :, None, :]
    x32 = x.astype(jnp.float32)
    x1, x2 = x32[..., :half], x32[..., half:]
    return jnp.concatenate([x1 * cos - x2 * sin, x1 * sin + x2 * cos], axis=-1).astype(x.dtype)


def chunkwise_retention(q, k, v):
    bsz, seq, heads, dk = q.shape
    dv = v.shape[-1]
    n_chunks = seq // CHUNK
    log_gamma = jnp.log1p(-jnp.power(2.0, -5.0 - jnp.arange(heads, dtype=jnp.float32)))
    idx = jnp.arange(CHUNK, dtype=jnp.float32)
    dist = jnp.abs(idx[:, None] - idx[None, :])
    intra_decay = jnp.exp(log_gamma[:, None, None] * dist)
    q_decay = jnp.exp(log_gamma[:, None] * (idx + 1.0))
    k_decay = jnp.exp(log_gamma[:, None] * (CHUNK - 1.0 - idx))
    chunk_decay = jnp.exp(log_gamma * CHUNK)

    def to_chunks(t):
        return t.astype(jnp.float32).reshape(bsz, n_chunks, CHUNK, heads, -1).transpose(1, 0, 3, 2, 4)

    qc, kc, vc = to_chunks(q), to_chunks(k), to_chunks(v)

    def step(state, inp):
        qi, ki, vi = inp
        scores = jnp.einsum("bhid,bhjd->bhij", qi, ki) * intra_decay
        out = (jnp.einsum("bhij,bhjv->bhiv", scores, vi)
               + jnp.einsum("bhid,bhdv->bhiv", qi * q_decay[:, :, None], state))
        state = (state * chunk_decay[:, None, None]
                 + jnp.einsum("bhjd,bhjv->bhdv", ki * k_decay[:, :, None], vi))
        return state, out

    state0 = jnp.zeros((bsz, heads, dk, dv), jnp.float32)
    _, out = lax.scan(step, state0, (qc, kc, vc))
    return out.transpose(1, 0, 3, 2, 4).reshape(bsz, seq, heads, dv)


def head_group_norm(o, w):
    bsz, seq, heads, dv = o.shape
    mu = jnp.mean(o, axis=-1, keepdims=True)
    var = jnp.mean(jnp.square(o - mu), axis=-1, keepdims=True)
    y = (o - mu) * lax.rsqrt(var + GN_EPS)
    return y.reshape(bsz, seq, heads * dv) * w.astype(jnp.float32)


def _linear_recurrence_combine(e1, e2):
    a1, b1 = e1
    a2, b2 = e2
    return a1 * a2, a2 * b1 + b2


def rg_lru(x, w_r, b_r, w_i, b_i, lam):
    bsz, seq, width = x.shape
    x32 = x.astype(jnp.float32)
    xb = x32.reshape(bsz, seq, LRU_BLOCKS, LRU_BLOCK)
    r = jax.nn.sigmoid(jnp.einsum("bsni,nij->bsnj", xb, w_r.astype(jnp.float32))
                       + b_r.astype(jnp.float32)).reshape(bsz, seq, width)
    i = jax.nn.sigmoid(jnp.einsum("bsni,nij->bsnj", xb, w_i.astype(jnp.float32))
                       + b_i.astype(jnp.float32)).reshape(bsz, seq, width)
    log_a = -LRU_C * r * jax.nn.softplus(-lam.astype(jnp.float32))
    a = jnp.exp(log_a)
    b = jnp.sqrt(-jnp.expm1(2.0 * log_a)) * (i * x32)
    _, h = lax.associative_scan(_linear_recurrence_combine, (a, b), axis=1)
    return h


def _fwd_setup_inputs(seed: int = 0) -> dict:
    key = jax.random.key(seed)
    ks = jax.random.split(key, 24)

    def nrm(k, shape, scale):
        return jax.random.normal(k, shape, jnp.float32) * scale

    def gain(k, shape):
        return 1.0 + 0.02 * jax.random.normal(k, shape, jnp.float32)

    x = jax.random.normal(ks[0], (BATCH, SEQ, D_MODEL), jnp.float32)
    start = jax.random.randint(ks[1], (BATCH, 1), 0, 64) * CHUNK
    positions = (start + jnp.arange(SEQ)[None, :]).astype(jnp.int32)

    u = jax.random.uniform(ks[12], (DEPTH, LRU_WIDTH), jnp.float32,
                           LRU_MIN_RAD ** 2, LRU_MAX_RAD ** 2)
    a0 = jnp.sqrt(u)
    lru_lambda = jnp.log(a0) - jnp.log1p(-a0)

    return {
        "x": x,
        "positions": positions,
        "norm1_w": gain(ks[2], (DEPTH, D_MODEL)),
        "w_in": nrm(ks[3], (DEPTH, D_MODEL, D_IN), D_MODEL ** -0.5),
        "merge_gate_b": nrm(ks[4], (DEPTH, 2, D_MODEL), 0.02),
        "ret_gn_w": gain(ks[5], (DEPTH, RET_V_WIDTH)),
        "w_ret_o": nrm(ks[6], (DEPTH, RET_V_WIDTH, D_MODEL), RET_V_WIDTH ** -0.5),
        "lru_conv_w": nrm(ks[7], (DEPTH, LRU_CONV, LRU_WIDTH), LRU_CONV ** -0.5),
        "lru_conv_b": nrm(ks[8], (DEPTH, LRU_WIDTH), 0.02),
        "lru_w_r": nrm(ks[9], (DEPTH, LRU_BLOCKS, LRU_BLOCK, LRU_BLOCK), LRU_BLOCK ** -0.5),
        "lru_b_r": nrm(ks[10], (DEPTH, LRU_BLOCKS, LRU_BLOCK), 0.02),
        "lru_w_i": nrm(ks[11], (DEPTH, LRU_BLOCKS, LRU_BLOCK, LRU_BLOCK), LRU_BLOCK ** -0.5),
        "lru_b_i": nrm(ks[13], (DEPTH, LRU_BLOCKS, LRU_BLOCK), 0.02),
        "lru_lambda": lru_lambda,
        "w_lru_o": nrm(ks[14], (DEPTH, LRU_WIDTH, D_MODEL), LRU_WIDTH ** -0.5),
        "w_out": nrm(ks[15], (DEPTH, D_MODEL, D_MODEL), D_MODEL ** -0.5),
        "norm2_w": gain(ks[16], (DEPTH, D_MODEL)),
        "ffn_w_up": nrm(ks[17], (DEPTH, D_MODEL, 2 * D_FF), D_MODEL ** -0.5),
        "ffn_conv_w": nrm(ks[18], (DEPTH, FFN_CONV, D_FF), FFN_CONV ** -0.5),
        "ffn_conv_b": nrm(ks[19], (DEPTH, D_FF), 0.02),
        "ffn_w_down": nrm(ks[20], (DEPTH, D_FF, D_MODEL), D_FF ** -0.5),
        "norm_f_w": gain(ks[21], (D_MODEL,)),
    }


def _fwd_reference(x, positions, norm1_w, w_in, merge_gate_b, ret_gn_w, w_ret_o,
              lru_conv_w, lru_conv_b, lru_w_r, lru_b_r, lru_w_i, lru_b_i, lru_lambda,
              w_lru_o, w_out, norm2_w, ffn_w_up, ffn_conv_w, ffn_conv_b, ffn_w_down,
              norm_f_w):
    bsz, seq, _ = x.shape
    for l in range(DEPTH):
        h = rms_norm(x, norm1_w[l])
        proj = h @ w_in[l]
        q, k, v, g_ret, x_lru, y_lru, gate_ret, gate_lru = jnp.split(proj, SPLIT_POINTS, axis=-1)

        q = rotary(q.reshape(bsz, seq, RET_HEADS, RET_DK), positions)
        k = rotary(k.reshape(bsz, seq, RET_HEADS, RET_DK), positions) * (RET_DK ** -0.5)
        o = chunkwise_retention(q, k, v.reshape(bsz, seq, RET_HEADS, RET_DV))
        o = head_group_norm(o, ret_gn_w[l])
        y_a = (o * jax.nn.silu(g_ret.astype(jnp.float32))).astype(x.dtype) @ w_ret_o[l]

        xc = causal_depthwise_conv(x_lru, lru_conv_w[l], lru_conv_b[l])
        hl = rg_lru(xc, lru_w_r[l], lru_b_r[l], lru_w_i[l], lru_b_i[l], lru_lambda[l])
        y_b = (hl * jax.nn.gelu(y_lru.astype(jnp.float32))).astype(x.dtype) @ w_lru_o[l]

        mix = (jax.nn.sigmoid(gate_ret + merge_gate_b[l, 0]) * y_a
               + jax.nn.sigmoid(gate_lru + merge_gate_b[l, 1]) * y_b)
        x = x + mix @ w_out[l]

        h = rms_norm(x, norm2_w[l])
        up = h @ ffn_w_up[l]
        gate, val = jnp.split(up, [D_FF], axis=-1)
        gate = causal_depthwise_conv(gate, ffn_conv_w[l], ffn_conv_b[l])
        x = x + (jax.nn.gelu(gate) * val) @ ffn_w_down[l]
    return rms_norm(x, norm_f_w)


import jax as _jax
import jax.numpy as _jnp

TWIN_FORMAT = 'train_step'
FWD_PARAMS = ['x', 'positions', 'norm1_w', 'w_in', 'merge_gate_b', 'ret_gn_w', 'w_ret_o', 'lru_conv_w', 'lru_conv_b', 'lru_w_r', 'lru_b_r', 'lru_w_i', 'lru_b_i', 'lru_lambda', 'w_lru_o', 'w_out', 'norm2_w', 'ffn_w_up', 'ffn_conv_w', 'ffn_conv_b', 'ffn_w_down', 'norm_f_w']
TWIN_WEIGHTS = ['norm1_w', 'w_in', 'merge_gate_b', 'ret_gn_w', 'w_ret_o', 'lru_conv_w', 'lru_conv_b', 'lru_w_r', 'lru_b_r', 'lru_w_i', 'lru_b_i', 'lru_lambda', 'w_lru_o', 'w_out', 'norm2_w', 'ffn_w_up', 'ffn_conv_w', 'ffn_conv_b', 'ffn_w_down', 'norm_f_w']
TWIN_DIFF_INPUT = 'x'
TWIN_INPUTS = ['x', 'positions', 'norm1_w', 'w_in', 'merge_gate_b', 'ret_gn_w', 'w_ret_o', 'lru_conv_w', 'lru_conv_b', 'lru_w_r', 'lru_b_r', 'lru_w_i', 'lru_b_i', 'lru_lambda', 'w_lru_o', 'w_out', 'norm2_w', 'ffn_w_up', 'ffn_conv_w', 'ffn_conv_b', 'ffn_w_down', 'norm_f_w', 'loss_target', 'm_norm1_w', 'm_w_in', 'm_merge_gate_b', 'm_ret_gn_w', 'm_w_ret_o', 'm_lru_conv_w', 'm_lru_conv_b', 'm_lru_w_r', 'm_lru_b_r', 'm_lru_w_i', 'm_lru_b_i', 'm_lru_lambda', 'm_w_lru_o', 'm_w_out', 'm_norm2_w', 'm_ffn_w_up', 'm_ffn_conv_w', 'm_ffn_conv_b', 'm_ffn_w_down', 'm_norm_f_w', 'v_norm1_w', 'v_w_in', 'v_merge_gate_b', 'v_ret_gn_w', 'v_w_ret_o', 'v_lru_conv_w', 'v_lru_conv_b', 'v_lru_w_r', 'v_lru_b_r', 'v_lru_w_i', 'v_lru_b_i', 'v_lru_lambda', 'v_w_lru_o', 'v_w_out', 'v_norm2_w', 'v_ffn_w_up', 'v_ffn_conv_w', 'v_ffn_conv_b', 'v_ffn_w_down', 'v_norm_f_w']
TWIN_OUTPUTS = ['loss', 'grad_x', 'grad_norm1_w', 'grad_w_in', 'grad_merge_gate_b', 'grad_ret_gn_w', 'grad_w_ret_o', 'grad_lru_conv_w', 'grad_lru_conv_b', 'grad_lru_w_r', 'grad_lru_b_r', 'grad_lru_w_i', 'grad_lru_b_i', 'grad_lru_lambda', 'grad_w_lru_o', 'grad_w_out', 'grad_norm2_w', 'grad_ffn_w_up', 'grad_ffn_conv_w', 'grad_ffn_conv_b', 'grad_ffn_w_down', 'grad_norm_f_w', 'delta_norm1_w', 'delta_w_in', 'delta_merge_gate_b', 'delta_ret_gn_w', 'delta_w_ret_o', 'delta_lru_conv_w', 'delta_lru_conv_b', 'delta_lru_w_r', 'delta_lru_b_r', 'delta_lru_w_i', 'delta_lru_b_i', 'delta_lru_lambda', 'delta_w_lru_o', 'delta_w_out', 'delta_norm2_w', 'delta_ffn_w_up', 'delta_ffn_conv_w', 'delta_ffn_conv_b', 'delta_ffn_w_down', 'delta_norm_f_w', 'new_m_norm1_w', 'new_m_w_in', 'new_m_merge_gate_b', 'new_m_ret_gn_w', 'new_m_w_ret_o', 'new_m_lru_conv_w', 'new_m_lru_conv_b', 'new_m_lru_w_r', 'new_m_lru_b_r', 'new_m_lru_w_i', 'new_m_lru_b_i', 'new_m_lru_lambda', 'new_m_w_lru_o', 'new_m_w_out', 'new_m_norm2_w', 'new_m_ffn_w_up', 'new_m_ffn_conv_w', 'new_m_ffn_conv_b', 'new_m_ffn_w_down', 'new_m_norm_f_w', 'new_v_norm1_w', 'new_v_w_in', 'new_v_merge_gate_b', 'new_v_ret_gn_w', 'new_v_w_ret_o', 'new_v_lru_conv_w', 'new_v_lru_conv_b', 'new_v_lru_w_r', 'new_v_lru_b_r', 'new_v_lru_w_i', 'new_v_lru_b_i', 'new_v_lru_lambda', 'new_v_w_lru_o', 'new_v_w_out', 'new_v_norm2_w', 'new_v_ffn_w_up', 'new_v_ffn_conv_w', 'new_v_ffn_conv_b', 'new_v_ffn_w_down', 'new_v_norm_f_w']
TWIN_LEAF_KINDS = {'loss': 'loss', 'grad_x': 'grad_x', 'grad_norm1_w': 'grad_w', 'grad_w_in': 'grad_w', 'grad_merge_gate_b': 'grad_w', 'grad_ret_gn_w': 'grad_w', 'grad_w_ret_o': 'grad_w', 'grad_lru_conv_w': 'grad_w', 'grad_lru_conv_b': 'grad_w', 'grad_lru_w_r': 'grad_w', 'grad_lru_b_r': 'grad_w', 'grad_lru_w_i': 'grad_w', 'grad_lru_b_i': 'grad_w', 'grad_lru_lambda': 'grad_w', 'grad_w_lru_o': 'grad_w', 'grad_w_out': 'grad_w', 'grad_norm2_w': 'grad_w', 'grad_ffn_w_up': 'grad_w', 'grad_ffn_conv_w': 'grad_w', 'grad_ffn_conv_b': 'grad_w', 'grad_ffn_w_down': 'grad_w', 'grad_norm_f_w': 'grad_w', 'delta_norm1_w': 'delta_w', 'delta_w_in': 'delta_w', 'delta_merge_gate_b': 'delta_w', 'delta_ret_gn_w': 'delta_w', 'delta_w_ret_o': 'delta_w', 'delta_lru_conv_w': 'delta_w', 'delta_lru_conv_b': 'delta_w', 'delta_lru_w_r': 'delta_w', 'delta_lru_b_r': 'delta_w', 'delta_lru_w_i': 'delta_w', 'delta_lru_b_i': 'delta_w', 'delta_lru_lambda': 'delta_w', 'delta_w_lru_o': 'delta_w', 'delta_w_out': 'delta_w', 'delta_norm2_w': 'delta_w', 'delta_ffn_w_up': 'delta_w', 'delta_ffn_conv_w': 'delta_w', 'delta_ffn_conv_b': 'delta_w', 'delta_ffn_w_down': 'delta_w', 'delta_norm_f_w': 'delta_w', 'new_m_norm1_w': 'new_m', 'new_m_w_in': 'new_m', 'new_m_merge_gate_b': 'new_m', 'new_m_ret_gn_w': 'new_m', 'new_m_w_ret_o': 'new_m', 'new_m_lru_conv_w': 'new_m', 'new_m_lru_conv_b': 'new_m', 'new_m_lru_w_r': 'new_m', 'new_m_lru_b_r': 'new_m', 'new_m_lru_w_i': 'new_m', 'new_m_lru_b_i': 'new_m', 'new_m_lru_lambda': 'new_m', 'new_m_w_lru_o': 'new_m', 'new_m_w_out': 'new_m', 'new_m_norm2_w': 'new_m', 'new_m_ffn_w_up': 'new_m', 'new_m_ffn_conv_w': 'new_m', 'new_m_ffn_conv_b': 'new_m', 'new_m_ffn_w_down': 'new_m', 'new_m_norm_f_w': 'new_m', 'new_v_norm1_w': 'new_v', 'new_v_w_in': 'new_v', 'new_v_merge_gate_b': 'new_v', 'new_v_ret_gn_w': 'new_v', 'new_v_w_ret_o': 'new_v', 'new_v_lru_conv_w': 'new_v', 'new_v_lru_conv_b': 'new_v', 'new_v_lru_w_r': 'new_v', 'new_v_lru_b_r': 'new_v', 'new_v_lru_w_i': 'new_v', 'new_v_lru_b_i': 'new_v', 'new_v_lru_lambda': 'new_v', 'new_v_w_lru_o': 'new_v', 'new_v_w_out': 'new_v', 'new_v_norm2_w': 'new_v', 'new_v_ffn_w_up': 'new_v', 'new_v_ffn_conv_w': 'new_v', 'new_v_ffn_conv_b': 'new_v', 'new_v_ffn_w_down': 'new_v', 'new_v_norm_f_w': 'new_v'}


def _forward(args):
    return _fwd_reference(*[args[k] for k in FWD_PARAMS])


def _output_shape():
    out = _jax.eval_shape(lambda: _forward(_fwd_setup_inputs(0)))
    return out.shape, out.dtype

N_MICROBATCH = 1
ADAM_LR = 0.001
ADAM_B1 = 0.9
ADAM_B2 = 0.999
ADAM_EPS = 1e-08
ADAM_WD = 0.01
ADAM_STEP = 10
PER_EXAMPLE_BATCH_AXIS = {'x': 0, 'positions': 0, 'loss_target': 0}
SHARED_INPUTS = []
_WEIGHT_DTYPES = {'norm1_w': _jnp.float32, 'w_in': _jnp.float32, 'merge_gate_b': _jnp.float32, 'ret_gn_w': _jnp.float32, 'w_ret_o': _jnp.float32, 'lru_conv_w': _jnp.float32, 'lru_conv_b': _jnp.float32, 'lru_w_r': _jnp.float32, 'lru_b_r': _jnp.float32, 'lru_w_i': _jnp.float32, 'lru_b_i': _jnp.float32, 'lru_lambda': _jnp.float32, 'w_lru_o': _jnp.float32, 'w_out': _jnp.float32, 'norm2_w': _jnp.float32, 'ffn_w_up': _jnp.float32, 'ffn_conv_w': _jnp.float32, 'ffn_conv_b': _jnp.float32, 'ffn_w_down': _jnp.float32, 'norm_f_w': _jnp.float32}
MOMENT_SCALE = {'norm1_w': 2.225011e-01, 'w_in': 7.444626e-02, 'merge_gate_b': 2.792776e-02, 'ret_gn_w': 9.732056e-02, 'w_ret_o': 8.644551e-02, 'lru_conv_w': 6.119784e-02, 'lru_conv_b': 2.989096e-01, 'lru_w_r': 1.303420e-02, 'lru_b_r': 1.195048e-02, 'lru_w_i': 2.202353e-02, 'lru_b_i': 2.095139e-02, 'lru_lambda': 2.405759e-02, 'w_lru_o': 5.290262e-02, 'w_out': 1.007924e-01, 'norm2_w': 1.922827e-01, 'ffn_w_up': 7.698685e-02, 'ffn_conv_w': 8.232941e-02, 'ffn_conv_b': 8.099220e-02, 'ffn_w_down': 1.316116e-01, 'norm_f_w': 6.402338e+01}


def _to_microbatches(a, axis):
    t = _jnp.moveaxis(a, axis, 0)
    t = t.reshape((N_MICROBATCH, t.shape[0] // N_MICROBATCH) + t.shape[1:])
    return _jnp.moveaxis(t, 1, axis + 1)


def setup_inputs(seed: int = 0) -> dict:
    inp = _fwd_setup_inputs(seed)
    key = _jax.random.fold_in(_jax.random.key(seed), 7919)
    shape, _ = _output_shape()
    out = dict(inp)
    out["loss_target"] = _jax.random.normal(_jax.random.fold_in(key, 0), shape, _jnp.float32)
    for i, name in enumerate(TWIN_WEIGHTS):
        w = inp[name].astype(_jnp.float32)
        if MOMENT_SCALE is None:
            s = _jnp.sqrt(_jnp.mean(_jnp.square(w)) + 1e-30)
        else:
            s = MOMENT_SCALE[name]
        km, kv = _jax.random.split(_jax.random.fold_in(key, i + 1))
        out[name] = w
        out["m_" + name] = s * _jax.random.normal(km, w.shape, _jnp.float32)
        out["v_" + name] = (s * s) * _jax.random.uniform(kv, w.shape, _jnp.float32, 0.5, 1.5)
    if N_MICROBATCH > 1:
        for name, axis in PER_EXAMPLE_BATCH_AXIS.items():
            out[name] = _to_microbatches(out[name], axis)
    return {'x': out['x'], 'positions': out['positions'], 'norm1_w': out['norm1_w'], 'w_in': out['w_in'], 'merge_gate_b': out['merge_gate_b'], 'ret_gn_w': out['ret_gn_w'], 'w_ret_o': out['w_ret_o'], 'lru_conv_w': out['lru_conv_w'], 'lru_conv_b': out['lru_conv_b'], 'lru_w_r': out['lru_w_r'], 'lru_b_r': out['lru_b_r'], 'lru_w_i': out['lru_w_i'], 'lru_b_i': out['lru_b_i'], 'lru_lambda': out['lru_lambda'], 'w_lru_o': out['w_lru_o'], 'w_out': out['w_out'], 'norm2_w': out['norm2_w'], 'ffn_w_up': out['ffn_w_up'], 'ffn_conv_w': out['ffn_conv_w'], 'ffn_conv_b': out['ffn_conv_b'], 'ffn_w_down': out['ffn_w_down'], 'norm_f_w': out['norm_f_w'], 'loss_target': out['loss_target'], 'm_norm1_w': out['m_norm1_w'], 'm_w_in': out['m_w_in'], 'm_merge_gate_b': out['m_merge_gate_b'], 'm_ret_gn_w': out['m_ret_gn_w'], 'm_w_ret_o': out['m_w_ret_o'], 'm_lru_conv_w': out['m_lru_conv_w'], 'm_lru_conv_b': out['m_lru_conv_b'], 'm_lru_w_r': out['m_lru_w_r'], 'm_lru_b_r': out['m_lru_b_r'], 'm_lru_w_i': out['m_lru_w_i'], 'm_lru_b_i': out['m_lru_b_i'], 'm_lru_lambda': out['m_lru_lambda'], 'm_w_lru_o': out['m_w_lru_o'], 'm_w_out': out['m_w_out'], 'm_norm2_w': out['m_norm2_w'], 'm_ffn_w_up': out['m_ffn_w_up'], 'm_ffn_conv_w': out['m_ffn_conv_w'], 'm_ffn_conv_b': out['m_ffn_conv_b'], 'm_ffn_w_down': out['m_ffn_w_down'], 'm_norm_f_w': out['m_norm_f_w'], 'v_norm1_w': out['v_norm1_w'], 'v_w_in': out['v_w_in'], 'v_merge_gate_b': out['v_merge_gate_b'], 'v_ret_gn_w': out['v_ret_gn_w'], 'v_w_ret_o': out['v_w_ret_o'], 'v_lru_conv_w': out['v_lru_conv_w'], 'v_lru_conv_b': out['v_lru_conv_b'], 'v_lru_w_r': out['v_lru_w_r'], 'v_lru_b_r': out['v_lru_b_r'], 'v_lru_w_i': out['v_lru_w_i'], 'v_lru_b_i': out['v_lru_b_i'], 'v_lru_lambda': out['v_lru_lambda'], 'v_w_lru_o': out['v_w_lru_o'], 'v_w_out': out['v_w_out'], 'v_norm2_w': out['v_norm2_w'], 'v_ffn_w_up': out['v_ffn_w_up'], 'v_ffn_conv_w': out['v_ffn_conv_w'], 'v_ffn_conv_b': out['v_ffn_conv_b'], 'v_ffn_w_down': out['v_ffn_w_down'], 'v_norm_f_w': out['v_norm_f_w']}


def _loss(weights, diff, rest, loss_target):
    with _jax.named_scope("forward"):
        args = {**rest, TWIN_DIFF_INPUT: diff, **{k: w.astype(_WEIGHT_DTYPES[k]) for k, w in weights.items()}}
        y = _forward(args)
    with _jax.named_scope("loss_head"):
        err = _jnp.square(y.astype(_jnp.float32) - loss_target)
        return 0.5 * _jnp.sum(_jnp.mean(err, axis=-1)) if err.ndim else 0.5 * err


def _adamw(w, g, m, v):
    m = ADAM_B1 * m + (1.0 - ADAM_B1) * g
    v = ADAM_B2 * v + (1.0 - ADAM_B2) * _jnp.square(g)
    m_hat = m / (1.0 - ADAM_B1 ** ADAM_STEP)
    v_hat = v / (1.0 - ADAM_B2 ** ADAM_STEP)
    delta = -ADAM_LR * (m_hat / (_jnp.sqrt(v_hat) + ADAM_EPS) + ADAM_WD * w)
    return delta, m, v


def reference(x, positions, norm1_w, w_in, merge_gate_b, ret_gn_w, w_ret_o, lru_conv_w, lru_conv_b, lru_w_r, lru_b_r, lru_w_i, lru_b_i, lru_lambda, w_lru_o, w_out, norm2_w, ffn_w_up, ffn_conv_w, ffn_conv_b, ffn_w_down, norm_f_w, loss_target, m_norm1_w, m_w_in, m_merge_gate_b, m_ret_gn_w, m_w_ret_o, m_lru_conv_w, m_lru_conv_b, m_lru_w_r, m_lru_b_r, m_lru_w_i, m_lru_b_i, m_lru_lambda, m_w_lru_o, m_w_out, m_norm2_w, m_ffn_w_up, m_ffn_conv_w, m_ffn_conv_b, m_ffn_w_down, m_norm_f_w, v_norm1_w, v_w_in, v_merge_gate_b, v_ret_gn_w, v_w_ret_o, v_lru_conv_w, v_lru_conv_b, v_lru_w_r, v_lru_b_r, v_lru_w_i, v_lru_b_i, v_lru_lambda, v_w_lru_o, v_w_out, v_norm2_w, v_ffn_w_up, v_ffn_conv_w, v_ffn_conv_b, v_ffn_w_down, v_norm_f_w):
    given = dict(x=x, positions=positions, norm1_w=norm1_w, w_in=w_in, merge_gate_b=merge_gate_b, ret_gn_w=ret_gn_w, w_ret_o=w_ret_o, lru_conv_w=lru_conv_w, lru_conv_b=lru_conv_b, lru_w_r=lru_w_r, lru_b_r=lru_b_r, lru_w_i=lru_w_i, lru_b_i=lru_b_i, lru_lambda=lru_lambda, w_lru_o=w_lru_o, w_out=w_out, norm2_w=norm2_w, ffn_w_up=ffn_w_up, ffn_conv_w=ffn_conv_w, ffn_conv_b=ffn_conv_b, ffn_w_down=ffn_w_down, norm_f_w=norm_f_w, loss_target=loss_target, m_norm1_w=m_norm1_w, m_w_in=m_w_in, m_merge_gate_b=m_merge_gate_b, m_ret_gn_w=m_ret_gn_w, m_w_ret_o=m_w_ret_o, m_lru_conv_w=m_lru_conv_w, m_lru_conv_b=m_lru_conv_b, m_lru_w_r=m_lru_w_r, m_lru_b_r=m_lru_b_r, m_lru_w_i=m_lru_w_i, m_lru_b_i=m_lru_b_i, m_lru_lambda=m_lru_lambda, m_w_lru_o=m_w_lru_o, m_w_out=m_w_out, m_norm2_w=m_norm2_w, m_ffn_w_up=m_ffn_w_up, m_ffn_conv_w=m_ffn_conv_w, m_ffn_conv_b=m_ffn_conv_b, m_ffn_w_down=m_ffn_w_down, m_norm_f_w=m_norm_f_w, v_norm1_w=v_norm1_w, v_w_in=v_w_in, v_merge_gate_b=v_merge_gate_b, v_ret_gn_w=v_ret_gn_w, v_w_ret_o=v_w_ret_o, v_lru_conv_w=v_lru_conv_w, v_lru_conv_b=v_lru_conv_b, v_lru_w_r=v_lru_w_r, v_lru_b_r=v_lru_b_r, v_lru_w_i=v_lru_w_i, v_lru_b_i=v_lru_b_i, v_lru_lambda=v_lru_lambda, v_w_lru_o=v_w_lru_o, v_w_out=v_w_out, v_norm2_w=v_norm2_w, v_ffn_w_up=v_ffn_w_up, v_ffn_conv_w=v_ffn_conv_w, v_ffn_conv_b=v_ffn_conv_b, v_ffn_w_down=v_ffn_w_down, v_norm_f_w=v_norm_f_w)
    weights = {n: given[n] for n in TWIN_WEIGHTS}
    shared = {n: given[n] for n in SHARED_INPUTS}
    per_example = {n: given[n] for n in ['x', 'positions']}
    grad_fn = _jax.value_and_grad(_loss, argnums=(0, 1))

    def one_microbatch(ex, loss_target):
        ex = dict(ex)
        diff = ex.pop(TWIN_DIFF_INPUT)
        return grad_fn(weights, diff, {**shared, **ex}, loss_target)

    if N_MICROBATCH == 1:
        loss, (grad_w, grad_x) = one_microbatch(per_example, given["loss_target"])
    else:
        def body(carry, xs):
            loss_sum, grad_sum = carry
            l_k, (gw_k, gx_k) = one_microbatch(xs[0], xs[1])
            with _jax.named_scope("update"):
                return (loss_sum + l_k, _jax.tree.map(_jnp.add, grad_sum, gw_k)), gx_k

        init = (_jnp.zeros((), _jnp.float32), _jax.tree.map(_jnp.zeros_like, weights))
        (loss, grad_w), grad_x = _jax.lax.scan(body, init, (per_example, given["loss_target"]))
    with _jax.named_scope("update"):
        delta_w, new_m, new_v = {}, {}, {}
        for n in TWIN_WEIGHTS:
            delta_w[n], new_m[n], new_v[n] = _adamw(weights[n], grad_w[n], given["m_" + n], given["v_" + n])
    return (loss, grad_x, *[grad_w[n] for n in TWIN_WEIGHTS], *[delta_w[n] for n in TWIN_WEIGHTS],
            *[new_m[n] for n in TWIN_WEIGHTS], *[new_v[n] for n in TWIN_WEIGHTS])
```

```python
import functools
import math

import jax
import jax.numpy as jnp
from jax import lax
from jax.experimental import pallas as pl
from jax.experimental.pallas import tpu as pltpu

F32 = jnp.float32
BF16 = jnp.bfloat16
MESH = pl.DeviceIdType.MESH

D_MODEL = 1024
CHUNK = 64
RET_HEADS = 4
RET_DK = 128
RET_DV = 256
LRU_BLOCKS = 4
LRU_BLOCK = 256
LRU_CONV = 4
LRU_C = 8.0
D_FF = 3072
FFN_CONV = 3
ROPE_BASE = 10000.0
RMS_EPS = 1e-6
GN_EPS = 1e-6
D_IN = 7168
ADAM_LR, ADAM_B1, ADAM_B2, ADAM_EPS, ADAM_WD, ADAM_STEP = 0.001, 0.9, 0.999, 1e-08, 0.01, 10

N_DEV = 8
V7X_VMEM_BYTES = 64 * 1024 * 1024
VMEM_LIMIT = 56 * 1024 * 1024
RET_BLOCK = 256
LANES = 128

COL_Q, COL_K = 0, 4
COL_V, COL_G, COL_XL, COL_YL = 4, 8, 12, 16
COL_GR, COL_GL = 5, 6


def _cparams(sem):
    return pltpu.CompilerParams(dimension_semantics=sem, vmem_limit_bytes=VMEM_LIMIT)


def _gelu(x):
    c = math.sqrt(2.0 / math.pi)
    t = jnp.tanh(c * (x + 0.044715 * x * x * x))
    return 0.5 * x * (1.0 + t)


def _gelu_and_grad(x):
    c = math.sqrt(2.0 / math.pi)
    x2 = x * x
    t = jnp.tanh(c * (x + 0.044715 * x2 * x))
    g = 0.5 * x * (1.0 + t)
    dg = 0.5 * (1.0 + t) + 0.5 * x * (1.0 - t * t) * c * (1.0 + 3.0 * 0.044715 * x2)
    return g, dg


def _sigmoid(x):
    return 1.0 / (1.0 + jnp.exp(-x))


def _shift_down(x, s, fill):
    r = pltpu.roll(x, s, 0)
    rows = lax.broadcasted_iota(jnp.int32, x.shape, 0)
    return jnp.where(rows >= s, r, fill)


def _shift_up(x, s, fill):
    n = x.shape[0]
    r = pltpu.roll(x, n - s, 0)
    rows = lax.broadcasted_iota(jnp.int32, x.shape, 0)
    return jnp.where(rows < n - s, r, fill)


def _dot(a, b, dims):
    return lax.dot_general(a, b, (dims, ((), ())), preferred_element_type=F32)


NN = ((1,), (0,))
NT = ((1,), (1,))
TN = ((0,), (0,))


def _matmul(a, b, mode, out_dtype, tm, tn, tk, name, add=None, b_col_off=0, n_out=None):
    if mode == "tn":
        K, M = a.shape
    else:
        M, K = a.shape
    if mode == "nt":
        N = b.shape[0] if n_out is None else n_out
    else:
        N = b.shape[1] if n_out is None else n_out
    tm, tn, tk = min(tm, M), min(tn, N), min(tk, K)
    assert M % tm == 0 and N % tn == 0 and K % tk == 0
    nk = K // tk
    dims = {"nn": NN, "nt": NT, "tn": TN}[mode]

    def body(*refs):
        if add is None:
            a_ref, b_ref, o_ref, acc = refs
            add_ref = None
        else:
            a_ref, b_ref, add_ref, o_ref, acc = refs
        k = pl.program_id(2)
        p = _dot(a_ref[...], b_ref[...], dims)

        def finish(r):
            if add_ref is not None:
                r = r + add_ref[...].astype(F32)
            o_ref[...] = r.astype(out_dtype)

        if nk == 1:
            finish(p)
        else:
            @pl.when(k == 0)
            def _():
                acc[...] = p

            @pl.when(k > 0)
            def _():
                acc[...] += p

            @pl.when(k == nk - 1)
            def _():
                finish(acc[...])

    if mode == "tn":
        a_spec = pl.BlockSpec((tk, tm), lambda i, j, k: (k, i))
    else:
        a_spec = pl.BlockSpec((tm, tk), lambda i, j, k: (i, k))
    if mode == "nt":
        b_spec = pl.BlockSpec((tn, tk), lambda i, j, k: (j + b_col_off, k))
    else:
        b_spec = pl.BlockSpec((tk, tn), lambda i, j, k: (k, j + b_col_off))
    in_specs = [a_spec, b_spec]
    args = [a, b]
    if add is not None:
        in_specs.append(pl.BlockSpec((tm, tn), lambda i, j, k: (i, j)))
        args.append(add)
    return pl.pallas_call(
        body,
        name=name,
        grid=(M // tm, N // tn, nk),
        in_specs=in_specs,
        out_specs=pl.BlockSpec((tm, tn), lambda i, j, k: (i, j)),
        out_shape=jax.ShapeDtypeStruct((M, N), out_dtype),
        scratch_shapes=[pltpu.VMEM((tm, tn) if nk > 1 else (8, LANES), F32)],
        compiler_params=_cparams(("parallel", "parallel", "arbitrary")),
    )(*args)


def _rmsnorm_fwd(x, w, tm, name):
    T, D = x.shape

    def body(x_ref, w_ref, h_ref):
        xv = x_ref[...]
        r = lax.rsqrt(jnp.mean(xv * xv, axis=-1, keepdims=True) + RMS_EPS)
        h_ref[...] = (xv * r * w_ref[...]).astype(BF16)

    return pl.pallas_call(
        body, name=name, grid=(T // tm,),
        in_specs=[pl.BlockSpec((tm, D), lambda i: (i, 0)), pl.BlockSpec((1, D), lambda i: (0, 0))],
        out_specs=pl.BlockSpec((tm, D), lambda i: (i, 0)),
        out_shape=jax.ShapeDtypeStruct((T, D), BF16),
        compiler_params=_cparams(("parallel",)),
    )(x, w)


def _rmsnorm_bwd_add(dres, dh, x, w, tm, name, want_bf16):
    T, D = x.shape

    def body(dres_ref, dh_ref, x_ref, w_ref, *outs):
        if want_bf16:
            dx_ref, dxb_ref, dw_ref = outs
        else:
            dx_ref, dw_ref = outs
        i = pl.program_id(0)
        xv = x_ref[...]
        r = lax.rsqrt(jnp.mean(xv * xv, axis=-1, keepdims=True) + RMS_EPS)
        xh = xv * r
        dh_v = dh_ref[...].astype(F32)
        dxh = dh_v * w_ref[...]
        dx = dres_ref[...] + r * (dxh - xh * jnp.mean(dxh * xh, axis=-1, keepdims=True))
        dx_ref[...] = dx
        if want_bf16:
            dxb_ref[...] = dx.astype(BF16)
        part = jnp.sum(dh_v * xh, axis=0, keepdims=True)

        @pl.when(i == 0)
        def _():
            dw_ref[...] = part

        @pl.when(i > 0)
        def _():
            dw_ref[...] += part

    tile = pl.BlockSpec((tm, D), lambda i: (i, 0))
    row = pl.BlockSpec((1, D), lambda i: (0, 0))
    out_specs = [tile] + ([tile] if want_bf16 else []) + [row]
    out_shape = ([jax.ShapeDtypeStruct((T, D), F32)] + ([jax.ShapeDtypeStruct((T, D), BF16)] if want_bf16 else [])
                 + [jax.ShapeDtypeStruct((1, D), F32)])
    return pl.pallas_call(
        body, name=name, grid=(T // tm,),
        in_specs=[tile, tile, tile, row], out_specs=out_specs, out_shape=out_shape,
        compiler_params=_cparams(("arbitrary",)),
    )(dres, dh, x, w)


def _loss_head(x2, target, wf, tm, name):
    T, D = x2.shape

    def body(x_ref, t_ref, w_ref, dx_ref, loss_ref, dw_ref):
        i = pl.program_id(0)
        xv = x_ref[...]
        r = lax.rsqrt(jnp.mean(xv * xv, axis=-1, keepdims=True) + RMS_EPS)
        xh = xv * r
        wv = w_ref[...]
        e = xh * wv - t_ref[...]
        lpart = 0.5 * jnp.sum(jnp.sum(e * e, axis=-1, keepdims=True), axis=0, keepdims=True) * (1.0 / D)
        dy = e * (1.0 / D)
        dxh = dy * wv
        dx_ref[...] = r * (dxh - xh * jnp.mean(dxh * xh, axis=-1, keepdims=True))
        wpart = jnp.sum(dy * xh, axis=0, keepdims=True)
        lfull = jnp.broadcast_to(lpart, (8, LANES))

        @pl.when(i == 0)
        def _():
            loss_ref[...] = lfull
            dw_ref[...] = wpart

        @pl.when(i > 0)
        def _():
            loss_ref[...] += lfull
            dw_ref[...] += wpart

    tile = pl.BlockSpec((tm, D), lambda i: (i, 0))
    row = pl.BlockSpec((1, D), lambda i: (0, 0))
    return pl.pallas_call(
        body, name=name, grid=(T // tm,),
        in_specs=[tile, tile, row],
        out_specs=[tile, pl.BlockSpec((8, LANES), lambda i: (0, 0)), row],
        out_shape=[jax.ShapeDtypeStruct((T, D), F32), jax.ShapeDtypeStruct((8, LANES), F32),
                   jax.ShapeDtypeStruct((1, D), F32)],
        compiler_params=_cparams(("arbitrary",)),
    )(x2, target, wf)


def _rope_tables(pos_col, inv2, tm, name):
    T = pos_col.shape[0]

    def body(p_ref, f_ref, c_ref, s_ref):
        ang = p_ref[...] * f_ref[...]
        lane = lax.broadcasted_iota(jnp.int32, ang.shape, 1)
        c_ref[...] = jnp.cos(ang)
        s_ref[...] = jnp.where(lane < RET_DK // 2, -1.0, 1.0) * jnp.sin(ang)

    tile = pl.BlockSpec((tm, RET_DK), lambda i: (i, 0))
    return pl.pallas_call(
        body, name=name, grid=(T // tm,),
        in_specs=[pl.BlockSpec((tm, 1), lambda i: (i, 0)), pl.BlockSpec((1, RET_DK), lambda i: (0, 0))],
        out_specs=[tile, tile],
        out_shape=[jax.ShapeDtypeStruct((T, RET_DK), F32)] * 2,
        compiler_params=_cparams(("parallel",)),
    )(pos_col, inv2)


def _mix_fwd(a_in, b_in, proj, x, w_ro, w_lo, w_out, mb, w2, tm, name):
    T, D = x.shape

    def body(a_ref, b_ref, gr_ref, gl_ref, x_ref, wro_ref, wlo_ref, wout_ref, mb_ref, w2_ref,
             x1_ref, mix_ref, h2_ref):
        ya = _dot(a_ref[...], wro_ref[...], NN)
        yb = _dot(b_ref[...], wlo_ref[...], NN)
        sa = _sigmoid(gr_ref[...] + mb_ref[0:1, :])
        sb = _sigmoid(gl_ref[...] + mb_ref[1:2, :])
        mix = (sa * ya + sb * yb).astype(BF16)
        mix_ref[...] = mix
        x1 = x_ref[...] + _dot(mix, wout_ref[...], NN)
        x1_ref[...] = x1
        r = lax.rsqrt(jnp.mean(x1 * x1, axis=-1, keepdims=True) + RMS_EPS)
        h2_ref[...] = (x1 * r * w2_ref[...]).astype(BF16)

    tile = pl.BlockSpec((tm, D), lambda i: (i, 0))
    wspec = pl.BlockSpec((D, D), lambda i: (0, 0))
    return pl.pallas_call(
        body, name=name, grid=(T // tm,),
        in_specs=[tile, tile,
                  pl.BlockSpec((tm, D), lambda i: (i, COL_GR)), pl.BlockSpec((tm, D), lambda i: (i, COL_GL)),
                  tile, wspec, wspec, wspec,
                  pl.BlockSpec((2, D), lambda i: (0, 0)), pl.BlockSpec((1, D), lambda i: (0, 0))],
        out_specs=[tile, tile, tile],
        out_shape=[jax.ShapeDtypeStruct((T, D), F32), jax.ShapeDtypeStruct((T, D), BF16),
                   jax.ShapeDtypeStruct((T, D), BF16)],
        compiler_params=_cparams(("parallel",)),
    )(a_in, b_in, proj, proj, x, w_ro, w_lo, w_out, mb, w2)


def _mix_bwd(dx1b, a_in, b_in, proj, w_ro, w_lo, w_out, mb, tm, name):
    T, D = a_in.shape

    def body(dx_ref, a_ref, b_ref, gr_ref, gl_ref, wro_ref, wlo_ref, wout_ref, mb_ref,
             da_ref, db_ref, dgr_ref, dgl_ref, dya_ref, dyb_ref, dmb_ref):
        i = pl.program_id(0)
        dmix = _dot(dx_ref[...], wout_ref[...], NT)
        ya = _dot(a_ref[...], wro_ref[...], NN)
        yb = _dot(b_ref[...], wlo_ref[...], NN)
        sa = _sigmoid(gr_ref[...] + mb_ref[0:1, :])
        sb = _sigmoid(gl_ref[...] + mb_ref[1:2, :])
        dya = (dmix * sa).astype(BF16)
        dyb = (dmix * sb).astype(BF16)
        dgr = dmix * ya * sa * (1.0 - sa)
        dgl = dmix * yb * sb * (1.0 - sb)
        dya_ref[...] = dya
        dyb_ref[...] = dyb
        dgr_ref[...] = dgr.astype(BF16)
        dgl_ref[...] = dgl.astype(BF16)
        da_ref[...] = _dot(dya, wro_ref[...], NT)
        db_ref[...] = _dot(dyb, wlo_ref[...], NT)

        @pl.when(i == 0)
        def _():
            dmb_ref[...] = jnp.zeros_like(dmb_ref)

        dmb_ref[0:1, :] += jnp.sum(dgr, axis=0, keepdims=True)
        dmb_ref[1:2, :] += jnp.sum(dgl, axis=0, keepdims=True)

    tile = pl.BlockSpec((tm, D), lambda i: (i, 0))
    wspec = pl.BlockSpec((D, D), lambda i: (0, 0))
    two = pl.BlockSpec((2, D), lambda i: (0, 0))
    return pl.pallas_call(
        body, name=name, grid=(T // tm,),
        in_specs=[tile, tile, tile,
                  pl.BlockSpec((tm, D), lambda i: (i, COL_GR)), pl.BlockSpec((tm, D), lambda i: (i, COL_GL)),
                  wspec, wspec, wspec, two],
        out_specs=[tile] * 6 + [two],
        out_shape=[jax.ShapeDtypeStruct((T, D), F32)] * 2 + [jax.ShapeDtypeStruct((T, D), BF16)] * 4
                  + [jax.ShapeDtypeStruct((2, D), F32)],
        compiler_params=_cparams(("arbitrary",)),
    )(dx1b, a_in, b_in, proj, proj, w_ro, w_lo, w_out, mb)


def _ret_decay_consts(lg):
    L = RET_BLOCK
    n = lax.broadcasted_iota(jnp.int32, (L, L), 0)
    m = lax.broadcasted_iota(jnp.int32, (L, L), 1)
    cn, cm = n // CHUNK, m // CHUNK
    expo = jnp.where(cn == cm, jnp.abs(n - m), n - m).astype(F32)
    wm = jnp.where(cm <= cn, jnp.exp(lg * expo), 0.0)
    idx = lax.broadcasted_iota(jnp.int32, (L, 1), 0).astype(F32)
    qd = jnp.exp(lg * (idx + 1.0))
    kd = jnp.exp(lg * (L - 1.0 - idx))
    bd = jnp.exp(lg * float(L))
    return wm, qd, kd, bd


def _rotate(v, cos2, sin2s):
    return v * cos2 + pltpu.roll(v, RET_DK // 2, 1) * sin2s


def _rotate_t(d, cos2, sin2s):
    return d * cos2 - pltpu.roll(d, RET_DK // 2, 1) * sin2s


def _retention_fwd(proj, cos2, sin2s, lgam, gn_w, B, S, name):
    T = B * S
    nb = S // RET_BLOCK
    scale = RET_DK ** -0.5

    def body(q_ref, k_ref, v_ref, g_ref, c_ref, s_ref, lg_ref, gw_ref, o_ref, a_ref, qr, kr, st):
        wm, qd, kd, bd = _ret_decay_consts(lg_ref[0:1, 0:1])
        cos2, sin2s = c_ref[...], s_ref[...]
        qr[...] = _rotate(q_ref[...], cos2, sin2s)
        kr[...] = _rotate(k_ref[...], cos2, sin2s) * scale
        st[...] = jnp.zeros_like(st)
        gw = gw_ref[...]
        for j in range(nb):
            rows = pl.ds(j * RET_BLOCK, RET_BLOCK)
            qb = qr[rows, :]
            kb = kr[rows, :]
            vb = v_ref[rows, :].astype(BF16)
            sc = _dot(qb.astype(BF16), kb.astype(BF16), NT) * wm
            o = _dot(sc.astype(BF16), vb, NN) + _dot((qb * qd).astype(BF16), st[...].astype(BF16), NN)
            st[...] = st[...] * bd + _dot((kb * kd).astype(BF16), vb, TN)
            o_ref[rows, :] = o
            mu = jnp.mean(o, axis=-1, keepdims=True)
            oc = o - mu
            var = jnp.mean(oc * oc, axis=-1, keepdims=True)
            y = oc * lax.rsqrt(var + GN_EPS) * gw
            g = g_ref[rows, :]
            a_ref[rows, :] = (y * (g * _sigmoid(g))).astype(BF16)

    blk = lambda w, off: pl.BlockSpec((S, w), lambda b, h: (b, off + h))
    return pl.pallas_call(
        body, name=name, grid=(B, RET_HEADS),
        in_specs=[blk(RET_DK, COL_Q), blk(RET_DK, COL_K), blk(RET_DV, COL_V), blk(RET_DV, COL_G),
                  pl.BlockSpec((S, RET_DK), lambda b, h: (b, 0)), pl.BlockSpec((S, RET_DK), lambda b, h: (b, 0)),
                  pl.BlockSpec((None, 8, LANES), lambda b, h: (h, 0, 0)),
                  pl.BlockSpec((1, RET_DV), lambda b, h: (0, h))],
        out_specs=[blk(RET_DV, 0), blk(RET_DV, 0)],
        out_shape=[jax.ShapeDtypeStruct((T, RET_HEADS * RET_DV), F32),
                   jax.ShapeDtypeStruct((T, RET_HEADS * RET_DV), BF16)],
        scratch_shapes=[pltpu.VMEM((S, RET_DK), F32), pltpu.VMEM((S, RET_DK), F32),
                        pltpu.VMEM((RET_DK, RET_DV), F32)],
        compiler_params=_cparams(("parallel", "parallel")),
    )(proj, proj, proj, proj, cos2, sin2s, lgam, gn_w)


def _retention_bwd(da_in, o, proj, cos2, sin2s, lgam, gn_w, B, S, name):
    T = B * S
    nb = S // RET_BLOCK
    scale = RET_DK ** -0.5

    def body(da_ref, o_ref, q_ref, k_ref, v_ref, g_ref, c_ref, s_ref, lg_ref, gw_ref,
             dq_ref, dk_ref, dv_ref, dg_ref, dgw_ref, qr, kr, do_s, sts, rst):
        b = pl.program_id(1)
        wm, qd, kd, bd = _ret_decay_consts(lg_ref[0:1, 0:1])
        cos2, sin2s = c_ref[...], s_ref[...]
        qr[...] = _rotate(q_ref[...], cos2, sin2s)
        kr[...] = _rotate(k_ref[...], cos2, sin2s) * scale
        gw = gw_ref[...]
        st = jnp.zeros((RET_DK, RET_DV), F32)
        dgw = jnp.zeros((1, RET_DV), F32)
        for j in range(nb):
            rows = pl.ds(j * RET_BLOCK, RET_BLOCK)
            ov = o_ref[rows, :]
            mu = jnp.mean(ov, axis=-1, keepdims=True)
            oc = ov - mu
            rstd = lax.rsqrt(jnp.mean(oc * oc, axis=-1, keepdims=True) + GN_EPS)
            y = oc * rstd
            g = g_ref[rows, :]
            sg = _sigmoid(g)
            da = da_ref[rows, :]
            dg_ref[rows, :] = (da * (y * gw) * (sg * (1.0 + g * (1.0 - sg)))).astype(BF16)
            dyw = da * (g * sg)
            dgw = dgw + jnp.sum(dyw * y, axis=0, keepdims=True)
            dy = dyw * gw
            do_s[rows, :] = rstd * (dy - jnp.mean(dy, axis=-1, keepdims=True)
                                    - y * jnp.mean(dy * y, axis=-1, keepdims=True))
            sts[j] = st
            st = st * bd + _dot((kr[rows, :] * kd).astype(BF16), v_ref[rows, :].astype(BF16), TN)

        @pl.when(b == 0)
        def _():
            dgw_ref[...] = dgw

        @pl.when(b > 0)
        def _():
            dgw_ref[...] += dgw

        rst[...] = jnp.zeros_like(rst)
        for j in reversed(range(nb)):
            rows = pl.ds(j * RET_BLOCK, RET_BLOCK)
            qb = qr[rows, :]
            kb = kr[rows, :]
            qbb, kbb = qb.astype(BF16), kb.astype(BF16)
            vb = v_ref[rows, :].astype(BF16)
            dob = do_s[rows, :]
            dobb = dob.astype(BF16)
            a_m = (_dot(qbb, kbb, NT) * wm).astype(BF16)
            b_m = (_dot(dobb, vb, NT) * wm).astype(BF16)
            rb = rst[...].astype(BF16)
            dq = _dot(b_m, kbb, NN) + _dot((dob * qd).astype(BF16), sts[j].astype(BF16), NT)
            dk = _dot(b_m, qbb, TN) + kd * _dot(vb, rb, NT)
            dv = _dot(a_m, dobb, TN) + kd * _dot(kbb, rb, NN)
            rst[...] = rst[...] * bd + _dot((qb * qd).astype(BF16), dobb, TN)
            cb, sb = c_ref[rows, :], s_ref[rows, :]
            dq_ref[rows, :] = _rotate_t(dq, cb, sb).astype(BF16)
            dk_ref[rows, :] = _rotate_t(dk * scale, cb, sb).astype(BF16)
            dv_ref[rows, :] = dv.astype(BF16)

    blk = lambda w, off: pl.BlockSpec((S, w), lambda h, b: (b, off + h))
    return pl.pallas_call(
        body, name=name, grid=(RET_HEADS, B),
        in_specs=[blk(RET_DV, 0), blk(RET_DV, 0),
                  blk(RET_DK, COL_Q), blk(RET_DK, COL_K), blk(RET_DV, COL_V), blk(RET_DV, COL_G),
                  pl.BlockSpec((S, RET_DK), lambda h, b: (b, 0)), pl.BlockSpec((S, RET_DK), lambda h, b: (b, 0)),
                  pl.BlockSpec((None, 8, LANES), lambda h, b: (h, 0, 0)),
                  pl.BlockSpec((1, RET_DV), lambda h, b: (0, h))],
        out_specs=[blk(RET_DK, 0), blk(RET_DK, 0), blk(RET_DV, 0), blk(RET_DV, 0),
                   pl.BlockSpec((1, RET_DV), lambda h, b: (0, h))],
        out_shape=[jax.ShapeDtypeStruct((T, RET_HEADS * RET_DK), BF16)] * 2
                  + [jax.ShapeDtypeStruct((T, RET_HEADS * RET_DV), BF16)] * 2
                  + [jax.ShapeDtypeStruct((1, RET_HEADS * RET_DV), F32)],
        scratch_shapes=[pltpu.VMEM((S, RET_DK), F32), pltpu.VMEM((S, RET_DK), F32),
                        pltpu.VMEM((S, RET_DV), F32), pltpu.VMEM((nb, RET_DK, RET_DV), F32),
                        pltpu.VMEM((RET_DK, RET_DV), F32)],
        compiler_params=_cparams(("parallel", "arbitrary")),
    )(da_in, o, proj, proj, proj, proj, cos2, sin2s, lgam, gn_w)


def _lru_gates(x, cw, cb, wr, wi, br, bi, lam):
    xc = cb + cw[LRU_CONV - 1:LRU_CONV, :] * x
    for j in range(LRU_CONV - 1):
        xc = xc + cw[j:j + 1, :] * _shift_down(x, LRU_CONV - 1 - j, 0.0)
    xcb = xc.astype(BF16)
    r = _sigmoid(_dot(xcb, wr, NN) + br)
    ig = _sigmoid(_dot(xcb, wi, NN) + bi)
    z = -lam
    sp = jnp.maximum(z, 0.0) + jnp.log1p(jnp.exp(-jnp.abs(z)))
    log_a = (-LRU_C) * r * sp
    a = jnp.exp(log_a)
    z2 = 2.0 * log_a
    taylor = -z2 * (1.0 + z2 * (0.5 + z2 * (1.0 / 6.0 + z2 * (1.0 / 24.0 + z2 * (1.0 / 120.0)))))
    om = jnp.where(z2 > -0.05, taylor, 1.0 - jnp.exp(z2))
    sq = jnp.sqrt(om)
    return xc, xcb, r, ig, sp, a, sq


def _lru_fwd(proj, cw, cb, wr, wi, br, bi, lam, B, S, name):
    T = B * S
    W = LRU_BLOCKS * LRU_BLOCK

    def body(x_ref, y_ref, cw_ref, cb_ref, wr_ref, wi_ref, br_ref, bi_ref, lam_ref, h_ref, bin_ref):
        xc, _, _, ig, _, a, sq = _lru_gates(x_ref[...], cw_ref[...], cb_ref[...], wr_ref[...], wi_ref[...],
                                           br_ref[...], bi_ref[...], lam_ref[...])
        bv = sq * ig * xc
        s = 1
        while s < S:
            bv = a * _shift_down(bv, s, 0.0) + bv
            if 2 * s < S:
                a = a * _shift_down(a, s, 1.0)
            s *= 2
        h_ref[...] = bv
        bin_ref[...] = (bv * _gelu(y_ref[...])).astype(BF16)

    blk = lambda off: pl.BlockSpec((S, LRU_BLOCK), lambda b, n: (b, off + n))
    vec = lambda rows: pl.BlockSpec((rows, LRU_BLOCK), lambda b, n: (0, n))
    wspec = pl.BlockSpec((None, LRU_BLOCK, LRU_BLOCK), lambda b, n: (n, 0, 0))
    return pl.pallas_call(
        body, name=name, grid=(B, LRU_BLOCKS),
        in_specs=[blk(COL_XL), blk(COL_YL), vec(LRU_CONV), vec(1), wspec, wspec, vec(1), vec(1), vec(1)],
        out_specs=[blk(0), blk(0)],
        out_shape=[jax.ShapeDtypeStruct((T, W), F32), jax.ShapeDtypeStruct((T, W), BF16)],
        compiler_params=_cparams(("parallel", "parallel")),
    )(proj, proj, cw, cb, wr, wi, br, bi, lam)


def _lru_bwd(db_in, h, proj, cw, cb, wr, wi, br, bi, lam, B, S, name):
    T = B * S
    W = LRU_BLOCKS * LRU_BLOCK

    def body(dbin_ref, h_ref, x_ref, y_ref, cw_ref, cb_ref, wr_ref, wi_ref, br_ref, bi_ref, lam_ref,
             dx_ref, dy_ref, dcw_ref, dcb_ref, dwr_ref, dwi_ref, dbr_ref, dbi_ref, dlam_ref):
        b = pl.program_id(1)
        x = x_ref[...]
        cw = cw_ref[...]
        wr, wi = wr_ref[...], wi_ref[...]
        lam = lam_ref[...]
        xc, xcb, r, ig, sp, a, sq = _lru_gates(x, cw, cb_ref[...], wr, wi, br_ref[...], bi_ref[...], lam)
        hv = h_ref[...]
        gel, dgel = _gelu_and_grad(y_ref[...])
        dbin = dbin_ref[...]
        dy_ref[...] = (dbin * hv * dgel).astype(BF16)
        dh = dbin * gel
        an = _shift_up(a, 1, 0.0)
        s = 1
        while s < S:
            dh = dh + an * _shift_up(dh, s, 0.0)
            if 2 * s < S:
                an = an * _shift_up(an, s, 1.0)
            s *= 2
        hprev = _shift_down(hv, 1, 0.0)
        d_ig = dh * sq * xc
        d_xc = dh * sq * ig
        a2 = a * a
        d_loga = dh * hprev * a - dh * ig * xc * a2 / sq
        d_r = d_loga * ((-LRU_C) * sp)
        d_sp = jnp.sum(d_loga * ((-LRU_C) * r), axis=0, keepdims=True)
        dlam = -d_sp * _sigmoid(-lam)
        d_pr = d_r * r * (1.0 - r)
        d_pi = d_ig * ig * (1.0 - ig)
        d_prb, d_pib = d_pr.astype(BF16), d_pi.astype(BF16)
        d_xc = d_xc + _dot(d_prb, wr, NT) + _dot(d_pib, wi, NT)
        dwr = _dot(xcb, d_prb, TN)
        dwi = _dot(xcb, d_pib, TN)
        dbr = jnp.sum(d_pr, axis=0, keepdims=True)
        dbi = jnp.sum(d_pi, axis=0, keepdims=True)
        @pl.when(b == 0)
        def _():
            for ref in (dcw_ref, dcb_ref, dwr_ref, dwi_ref, dbr_ref, dbi_ref, dlam_ref):
                ref[...] = jnp.zeros_like(ref)

        dx = cw[LRU_CONV - 1:LRU_CONV, :] * d_xc
        for j in range(LRU_CONV - 1):
            sft = LRU_CONV - 1 - j
            dx = dx + cw[j:j + 1, :] * _shift_up(d_xc, sft, 0.0)
            dcw_ref[j:j + 1, :] += jnp.sum(d_xc * _shift_down(x, sft, 0.0), axis=0, keepdims=True)
        dcw_ref[LRU_CONV - 1:LRU_CONV, :] += jnp.sum(d_xc * x, axis=0, keepdims=True)
        dx_ref[...] = dx.astype(BF16)
        dcb_ref[...] += jnp.sum(d_xc, axis=0, keepdims=True)
        dwr_ref[...] += dwr
        dwi_ref[...] += dwi
        dbr_ref[...] += dbr
        dbi_ref[...] += dbi
        dlam_ref[...] += dlam

    blk = lambda off: pl.BlockSpec((S, LRU_BLOCK), lambda n, b: (b, off + n))
    vec = lambda rows: pl.BlockSpec((rows, LRU_BLOCK), lambda n, b: (0, n))
    wspec = pl.BlockSpec((None, LRU_BLOCK, LRU_BLOCK), lambda n, b: (n, 0, 0))
    vshape = lambda rows: jax.ShapeDtypeStruct((rows, W), F32)
    wshape = jax.ShapeDtypeStruct((LRU_BLOCKS, LRU_BLOCK, LRU_BLOCK), F32)
    return pl.pallas_call(
        body, name=name, grid=(LRU_BLOCKS, B),
        in_specs=[blk(0), blk(0), blk(COL_XL), blk(COL_YL), vec(LRU_CONV), vec(1), wspec, wspec, vec(1), vec(1),
                  vec(1)],
        out_specs=[blk(0), blk(0), vec(LRU_CONV), vec(1), wspec, wspec, vec(1), vec(1), vec(1)],
        out_shape=[jax.ShapeDtypeStruct((T, W), BF16)] * 2
                  + [vshape(LRU_CONV), vshape(1), wshape, wshape, vshape(1), vshape(1), vshape(1)],
        compiler_params=_cparams(("parallel", "arbitrary")),
    )(db_in, h, proj, proj, cw, cb, wr, wi, br, bi, lam)


FFN_CT = 256


def _ffn_conv(gate, cw, cb):
    gc = cb + cw[FFN_CONV - 1:FFN_CONV, :] * gate
    for j in range(FFN_CONV - 1):
        gc = gc + cw[j:j + 1, :] * _shift_down(gate, FFN_CONV - 1 - j, 0.0)
    return gc


def _ffn_act_fwd(up, cw, cb, B, S, name):
    T = B * S
    nct = D_FF // FFN_CT

    def body(g_ref, v_ref, cw_ref, cb_ref, f_ref):
        gc = _ffn_conv(g_ref[...], cw_ref[...], cb_ref[...])
        f_ref[...] = (_gelu(gc) * v_ref[...]).astype(BF16)

    return pl.pallas_call(
        body, name=name, grid=(B, nct),
        in_specs=[pl.BlockSpec((S, FFN_CT), lambda b, c: (b, c)), pl.BlockSpec((S, FFN_CT), lambda b, c: (b, nct + c)),
                  pl.BlockSpec((FFN_CONV, FFN_CT), lambda b, c: (0, c)), pl.BlockSpec((1, FFN_CT), lambda b, c: (0, c))],
        out_specs=pl.BlockSpec((S, FFN_CT), lambda b, c: (b, c)),
        out_shape=jax.ShapeDtypeStruct((T, D_FF), BF16),
        compiler_params=_cparams(("parallel", "parallel")),
    )(up, up, cw, cb)


def _ffn_act_bwd(df, up, cw, cb, B, S, name):
    T = B * S
    nct = D_FF // FFN_CT

    def body(df_ref, g_ref, v_ref, cw_ref, cb_ref, dg_ref, dv_ref, dcw_ref, dcb_ref):
        b = pl.program_id(1)
        gate = g_ref[...]
        cw = cw_ref[...]
        gc = _ffn_conv(gate, cw, cb_ref[...])
        gel, dgel = _gelu_and_grad(gc)
        dfv = df_ref[...]
        dv_ref[...] = (dfv * gel).astype(BF16)
        dgc = dfv * v_ref[...] * dgel

        @pl.when(b == 0)
        def _():
            dcw_ref[...] = jnp.zeros_like(dcw_ref)
            dcb_ref[...] = jnp.zeros_like(dcb_ref)

        dgate = cw[FFN_CONV - 1:FFN_CONV, :] * dgc
        for j in range(FFN_CONV - 1):
            sft = FFN_CONV - 1 - j
            dgate = dgate + cw[j:j + 1, :] * _shift_up(dgc, sft, 0.0)
            dcw_ref[j:j + 1, :] += jnp.sum(dgc * _shift_down(gate, sft, 0.0), axis=0, keepdims=True)
        dcw_ref[FFN_CONV - 1:FFN_CONV, :] += jnp.sum(dgc * gate, axis=0, keepdims=True)
        dg_ref[...] = dgate.astype(BF16)
        dcb_ref[...] += jnp.sum(dgc, axis=0, keepdims=True)

    blk = pl.BlockSpec((S, FFN_CT), lambda c, b: (b, c))
    return pl.pallas_call(
        body, name=name, grid=(nct, B),
        in_specs=[blk, blk, pl.BlockSpec((S, FFN_CT), lambda c, b: (b, nct + c)),
                  pl.BlockSpec((FFN_CONV, FFN_CT), lambda c, b: (0, c)), pl.BlockSpec((1, FFN_CT), lambda c, b: (0, c))],
        out_specs=[blk, blk, pl.BlockSpec((FFN_CONV, FFN_CT), lambda c, b: (0, c)),
                   pl.BlockSpec((1, FFN_CT), lambda c, b: (0, c))],
        out_shape=[jax.ShapeDtypeStruct((T, D_FF), BF16)] * 2
                  + [jax.ShapeDtypeStruct((FFN_CONV, D_FF), F32), jax.ShapeDtypeStruct((1, D_FF), F32)],
        compiler_params=_cparams(("parallel", "arbitrary")),
    )(df, up, up, cw, cb)


def _local_step(x3, positions, target3, wb, ws):
    B, S, D = x3.shape
    T = B * S
    x = x3.reshape(T, D)
    target = target3.reshape(T, D)
    tm = min(512, T)
    big = min(1024, T)

    half = RET_DK // 2
    inv_freq = ROPE_BASE ** (-jnp.arange(half, dtype=F32) / half)
    inv2 = jnp.concatenate([inv_freq, inv_freq]).reshape(1, RET_DK)
    log_gamma = jnp.log1p(-jnp.power(2.0, -5.0 - jnp.arange(RET_HEADS, dtype=F32)))
    lgam = jnp.broadcast_to(log_gamma[:, None, None], (RET_HEADS, 8, LANES))
    pos_col = positions.astype(F32).reshape(T, 1)
    cos2, sin2s = _rope_tables(pos_col, inv2, tm, "rope_tables")

    h1 = _rmsnorm_fwd(x, ws["norm1_w"], tm, "norm1_fwd")
    proj = _matmul(h1, wb["w_in"], "nn", F32, big, 1024, 1024, "proj_fwd")
    o, a_in = _retention_fwd(proj, cos2, sin2s, lgam, ws["ret_gn_w"], B, S, "retention_fwd")
    hl, b_in = _lru_fwd(proj, ws["lru_conv_w"], ws["lru_conv_b"], wb["lru_w_r"], wb["lru_w_i"],
                        ws["lru_b_r"], ws["lru_b_i"], ws["lru_lambda"], B, S, "lru_fwd")
    x1, mix, h2 = _mix_fwd(a_in, b_in, proj, x, wb["w_ret_o"], wb["w_lru_o"], wb["w_out"],
                           ws["merge_gate_b"], ws["norm2_w"], tm, "mix_fwd")
    up = _matmul(h2, wb["ffn_w_up"], "nn", F32, big, 1024, 1024, "ffn_up_fwd")
    f = _ffn_act_fwd(up, ws["ffn_conv_w"], ws["ffn_conv_b"], B, S, "ffn_act_fwd")
    x2 = _matmul(f, wb["ffn_w_down"], "nn", F32, big, 1024, 1024, "ffn_down_fwd", add=x1)
    dx2, loss_acc, d_norm_f = _loss_head(x2, target, ws["norm_f_w"], tm, "loss_head")

    g = {}
    g["norm_f_w"] = d_norm_f
    dx2b = dx2.astype(BF16)
    df = _matmul(dx2b, wb["ffn_w_down"], "nt", F32, big, 1024, 1024, "ffn_down_bwd_x")
    g["ffn_w_down"] = _matmul(f, dx2b, "tn", F32, 1024, 1024, big, "ffn_down_bwd_w")
    dgate, dval, g["ffn_conv_w"], g["ffn_conv_b"] = _ffn_act_bwd(df, up, ws["ffn_conv_w"], ws["ffn_conv_b"], B, S,
                                                                 "ffn_act_bwd")
    dup = jnp.concatenate([dgate, dval], axis=1)
    dh2 = _matmul(dup, wb["ffn_w_up"], "nt", F32, big, 1024, 1024, "ffn_up_bwd_x")
    g["ffn_w_up"] = _matmul(h2, dup, "tn", F32, 1024, 1024, big, "ffn_up_bwd_w")
    dx1, dx1b, g["norm2_w"] = _rmsnorm_bwd_add(dx2, dh2, x1, ws["norm2_w"], tm, "norm2_bwd", True)
    da_in, db_in, dgr, dgl, dya, dyb, g["merge_gate_b"] = _mix_bwd(
        dx1b, a_in, b_in, proj, wb["w_ret_o"], wb["w_lru_o"], wb["w_out"], ws["merge_gate_b"], tm, "mix_bwd")
    g["w_out"] = _matmul(mix, dx1b, "tn", F32, 1024, 1024, big, "w_out_bwd_w")
    g["w_ret_o"] = _matmul(a_in, dya, "tn", F32, 1024, 1024, big, "w_ret_o_bwd_w")
    g["w_lru_o"] = _matmul(b_in, dyb, "tn", F32, 1024, 1024, big, "w_lru_o_bwd_w")
    (dxl, dyl, g["lru_conv_w"], g["lru_conv_b"], g["lru_w_r"], g["lru_w_i"], g["lru_b_r"], g["lru_b_i"],
     g["lru_lambda"]) = _lru_bwd(db_in, hl, proj, ws["lru_conv_w"], ws["lru_conv_b"], wb["lru_w_r"], wb["lru_w_i"],
                                 ws["lru_b_r"], ws["lru_b_i"], ws["lru_lambda"], B, S, "lru_bwd")
    dq, dk, dv, dg, g["ret_gn_w"] = _retention_bwd(da_in, o, proj, cos2, sin2s, lgam, ws["ret_gn_w"], B, S,
                                                   "retention_bwd")
    dproj = jnp.concatenate([dq, dk, dv, dg, dxl, dyl, dgr, dgl], axis=1)
    dh1 = _matmul(dproj, wb["w_in"], "nt", F32, big, 1024, 1024, "proj_bwd_x")
    g["w_in"] = _matmul(h1, dproj, "tn", F32, 1024, 1024, big, "proj_bwd_w")
    grad_x, g["norm1_w"] = _rmsnorm_bwd_add(dx1, dh1, x, ws["norm1_w"], tm, "norm1_bwd", False)
    return loss_acc, grad_x.reshape(B, S, D), g


HBM_SPEC = pl.BlockSpec(memory_space=pl.ANY)


def _mesh_pos():
    return lax.axis_index("x"), lax.axis_index("y"), lax.axis_index("c")


def _other_chips(x, y):
    return [(1 - x, y), (x, 1 - y), (1 - x, 1 - y)]


def _all_gather(shard, name):
    R, C = shard.shape

    def body(x_ref, out_ref, send_sems, recv_sems, local_sem):
        x, y, c = _mesh_pos()
        me, sibling = (x, y, c), (x, y, 1 - c)
        chips = _other_chips(x, y)

        def slot(px, py, pc):
            return out_ref.at[4 * px + 2 * py + pc]

        def copy(k, block, to, src=None):
            return pltpu.make_async_remote_copy(
                src_ref=slot(*block) if src is None else src, dst_ref=slot(*block),
                send_sem=send_sems.at[k], recv_sem=recv_sems.at[k], device_id=to, device_id_type=MESH)

        mine = pltpu.make_async_copy(x_ref, slot(*me), local_sem)
        mine.start()
        first = [copy(0, me, sibling, src=x_ref)]
        first += [copy(1 + j, me, (*chip, c), src=x_ref) for j, chip in enumerate(chips)]
        for cp in first:
            cp.start()
        passed = [copy(4 + j, (*chip, c), sibling) for j, chip in enumerate(chips)]
        for j, chip in enumerate(chips):
            copy(1 + j, (*chip, c), me).wait_recv()
            passed[j].start()
        copy(0, sibling, me).wait_recv()
        for j, chip in enumerate(chips):
            copy(4 + j, (*chip, 1 - c), me).wait_recv()
        for cp in first + passed:
            cp.wait_send()
        mine.wait()

    return pl.pallas_call(
        body, name=name,
        in_specs=[HBM_SPEC], out_specs=HBM_SPEC,
        out_shape=jax.ShapeDtypeStruct((N_DEV, R, C), shard.dtype),
        scratch_shapes=[pltpu.SemaphoreType.DMA((7,)), pltpu.SemaphoreType.DMA((7,)), pltpu.SemaphoreType.DMA],
    )(shard)


def _rs_sibling_exchange(g, name):
    _, R, C = g.shape

    def body(g_ref, out_ref, send_sems, recv_sems):
        x, y, c = _mesh_pos()
        sibling = (x, y, 1 - c)
        copies = [
            pltpu.make_async_remote_copy(
                src_ref=g_ref.at[2 * k + (1 - c)], dst_ref=out_ref.at[k],
                send_sem=send_sems.at[k], recv_sem=recv_sems.at[k], device_id=sibling, device_id_type=MESH)
            for k in range(4)
        ]
        for cp in copies:
            cp.start()
        for cp in copies:
            cp.wait_recv()
        for cp in copies:
            cp.wait_send()

    return pl.pallas_call(
        body, name=name,
        in_specs=[HBM_SPEC], out_specs=HBM_SPEC,
        out_shape=jax.ShapeDtypeStruct((4, R, C), g.dtype),
        scratch_shapes=[pltpu.SemaphoreType.DMA((4,)), pltpu.SemaphoreType.DMA((4,))],
    )(g)


def _rs_chip_exchange(partial, name):
    _, R, C = partial.shape

    def body(p_ref, out_ref, send_sems, recv_sems):
        x, y, c = _mesh_pos()
        copies = [
            pltpu.make_async_remote_copy(
                src_ref=p_ref.at[2 * px + py], dst_ref=out_ref.at[j],
                send_sem=send_sems.at[j], recv_sem=recv_sems.at[j], device_id=(px, py, c), device_id_type=MESH)
            for j, (px, py) in enumerate(_other_chips(x, y))
        ]
        for cp in copies:
            cp.start()
        for cp in copies:
            cp.wait_recv()
        for cp in copies:
            cp.wait_send()

    return pl.pallas_call(
        body, name=name,
        in_specs=[HBM_SPEC], out_specs=HBM_SPEC,
        out_shape=jax.ShapeDtypeStruct((3, R, C), partial.dtype),
        scratch_shapes=[pltpu.SemaphoreType.DMA((3,)), pltpu.SemaphoreType.DMA((3,))],
    )(partial)


def _row_tile(rows, cap):
    if rows <= cap:
        return rows
    best = None
    for t in range(16, cap + 1, 16):
        if rows % t == 0:
            best = t
    assert best is not None
    return best


def _rs_add_sibling(g, recv, core, name):
    _, R, C = g.shape
    tr = _row_tile(R, 2048)
    g4 = g.reshape(4, 2, R, C)

    def body(c_ref, g_ref, r_ref, o_ref):
        o_ref[...] = g_ref[...] + r_ref[...]

    return pl.pallas_call(
        body, name=name,
        grid_spec=pltpu.PrefetchScalarGridSpec(
            num_scalar_prefetch=1, grid=(4, R // tr),
            in_specs=[pl.BlockSpec((None, None, tr, C), lambda k, i, c_ref: (k, c_ref[0], i, 0)),
                      pl.BlockSpec((None, tr, C), lambda k, i, c_ref: (k, i, 0))],
            out_specs=pl.BlockSpec((None, tr, C), lambda k, i, c_ref: (k, i, 0))),
        out_shape=jax.ShapeDtypeStruct((4, R, C), g.dtype),
        compiler_params=_cparams(("parallel", "parallel")),
    )(core, g4, recv)


def _sum_slots(parts, name):
    n, R, C = parts.shape
    tr = _row_tile(R, 1024)

    def body(p_ref, o_ref):
        acc = p_ref[0]
        for k in range(1, n):
            acc = acc + p_ref[k]
        o_ref[...] = acc

    return pl.pallas_call(
        body, name=name, grid=(R // tr,),
        in_specs=[pl.BlockSpec((n, tr, C), lambda i: (0, i, 0))],
        out_specs=pl.BlockSpec((tr, C), lambda i: (i, 0)),
        out_shape=jax.ShapeDtypeStruct((R, C), parts.dtype),
        compiler_params=_cparams(("parallel",)),
    )(parts)


def _adamw(g, w, m, v, name, partial=None, recv=None, chip=None):
    R, C = w.shape
    tr = _row_tile(R, 1024)
    reduce = partial is not None

    def body(*refs):
        if reduce:
            _, p_ref, r_ref, w_ref, m_ref, v_ref, g_ref, d_ref, nm_ref, nv_ref = refs
            gv = p_ref[...] + r_ref[0] + r_ref[1] + r_ref[2]
        else:
            gi_ref, w_ref, m_ref, v_ref, g_ref, d_ref, nm_ref, nv_ref = refs
            gv = gi_ref[...]
        g_ref[...] = gv
        nm = ADAM_B1 * m_ref[...] + (1.0 - ADAM_B1) * gv
        nv = ADAM_B2 * v_ref[...] + (1.0 - ADAM_B2) * (gv * gv)
        nm_ref[...] = nm
        nv_ref[...] = nv
        m_hat = nm / (1.0 - ADAM_B1 ** ADAM_STEP)
        v_hat = nv / (1.0 - ADAM_B2 ** ADAM_STEP)
        d_ref[...] = -ADAM_LR * (m_hat / (jnp.sqrt(v_hat) + ADAM_EPS) + ADAM_WD * w_ref[...])

    out_shape = [jax.ShapeDtypeStruct((R, C), F32)] * 4
    if reduce:
        tile = pl.BlockSpec((tr, C), lambda i, s: (i, 0))
        grid_spec = pltpu.PrefetchScalarGridSpec(
            num_scalar_prefetch=1, grid=(R // tr,),
            in_specs=[pl.BlockSpec((None, tr, C), lambda i, s: (s[0], i, 0)),
                      pl.BlockSpec((3, tr, C), lambda i, s: (0, i, 0)), tile, tile, tile],
            out_specs=[tile] * 4)
        return pl.pallas_call(body, name=name, grid_spec=grid_spec, out_shape=out_shape,
                              compiler_params=_cparams(("parallel",)))(chip, partial, recv, w, m, v)
    tile = pl.BlockSpec((tr, C), lambda i: (i, 0))
    return pl.pallas_call(body, name=name, grid=(R // tr,), in_specs=[tile] * 4, out_specs=[tile] * 4,
                          out_shape=out_shape, compiler_params=_cparams(("parallel",)))(g, w, m, v)


BIG = [
    ("w_in", (1024, 896), 1), ("w_ret_o", (128, 1024), 0), ("lru_w_r", (4, 32, 256), 1), ("lru_w_i", (4, 32, 256), 1),
    ("w_lru_o", (128, 1024), 0), ("w_out", (128, 1024), 0), ("ffn_w_up", (1024, 768), 1),
    ("ffn_w_down", (384, 1024), 0),
]
SMALL_SHARDED = [("merge_gate_b", (2, 128), 1), ("lru_conv_w", (4, 128), 1), ("lru_b_r", (4, 32), 1),
                 ("lru_b_i", (4, 32), 1), ("ffn_conv_w", (3, 384), 1)]
REPLICATED = [("norm1_w", (1, 1024)), ("ret_gn_w", (1, 1024)), ("lru_conv_b", (1, 1024)), ("lru_lambda", (1, 1024)),
              ("norm2_w", (1, 1024)), ("ffn_conv_b", (1, 3072)), ("norm_f_w", (1, 1024))]


def _pack_rows(flat_parts, row_multiple):
    flat = jnp.concatenate(flat_parts)
    n = flat.shape[0]
    rows = -(-n // LANES)
    rows = -(-rows // row_multiple) * row_multiple
    return jnp.pad(flat, (0, rows * LANES - n)).reshape(rows, LANES)


def _unpack(flat, specs):
    out, off = {}, 0
    for name, shape in specs:
        n = math.prod(shape)
        out[name] = flat[off:off + n].reshape(shape)
        off += n
    return out


def _full_from_gathered(gathered, spec):
    out, off = {}, 0
    for name, shape, axis in spec:
        n = math.prod(shape)
        piece = gathered[:, off:off + n].reshape((N_DEV,) + shape)
        off += n
        piece = jnp.moveaxis(piece, 0, axis)
        full_shape = shape[:axis] + (N_DEV * shape[axis],) + shape[axis + 1:]
        out[name] = piece.reshape(full_shape)
    return out


def _slabs_from_full(full, shape, axis):
    split = full.reshape(shape[:axis] + (N_DEV, shape[axis]) + shape[axis + 1:])
    return jnp.moveaxis(split, axis, 0).reshape(N_DEV, -1)


def _own_shard(full, shape, axis, idx):
    return lax.dynamic_slice_in_dim(full, idx * shape[axis], shape[axis], axis=axis)


def kernel(x, positions, norm1_w, w_in, merge_gate_b, ret_gn_w, w_ret_o, lru_conv_w, lru_conv_b, lru_w_r, lru_b_r, lru_w_i, lru_b_i, lru_lambda, w_lru_o, w_out, norm2_w, ffn_w_up, ffn_conv_w, ffn_conv_b, ffn_w_down, norm_f_w, loss_target, m_norm1_w, m_w_in, m_merge_gate_b, m_ret_gn_w, m_w_ret_o, m_lru_conv_w, m_lru_conv_b, m_lru_w_r, m_lru_b_r, m_lru_w_i, m_lru_b_i, m_lru_lambda, m_w_lru_o, m_w_out, m_norm2_w, m_ffn_w_up, m_ffn_conv_w, m_ffn_conv_b, m_ffn_w_down, m_norm_f_w, v_norm1_w, v_w_in, v_merge_gate_b, v_ret_gn_w, v_w_ret_o, v_lru_conv_w, v_lru_conv_b, v_lru_w_r, v_lru_b_r, v_lru_w_i, v_lru_b_i, v_lru_lambda, v_w_lru_o, v_w_out, v_norm2_w, v_ffn_w_up, v_ffn_conv_w, v_ffn_conv_b, v_ffn_w_down, v_norm_f_w):
    names = ["norm1_w", "w_in", "merge_gate_b", "ret_gn_w", "w_ret_o", "lru_conv_w", "lru_conv_b", "lru_w_r", "lru_b_r",
             "lru_w_i", "lru_b_i", "lru_lambda", "w_lru_o", "w_out", "norm2_w", "ffn_w_up", "ffn_conv_w", "ffn_conv_b",
             "ffn_w_down", "norm_f_w"]
    w_args = [norm1_w, w_in, merge_gate_b, ret_gn_w, w_ret_o, lru_conv_w, lru_conv_b, lru_w_r, lru_b_r, lru_w_i, lru_b_i,
              lru_lambda, w_lru_o, w_out, norm2_w, ffn_w_up, ffn_conv_w, ffn_conv_b, ffn_w_down, norm_f_w]
    m_args = [m_norm1_w, m_w_in, m_merge_gate_b, m_ret_gn_w, m_w_ret_o, m_lru_conv_w, m_lru_conv_b, m_lru_w_r, m_lru_b_r,
              m_lru_w_i, m_lru_b_i, m_lru_lambda, m_w_lru_o, m_w_out, m_norm2_w, m_ffn_w_up, m_ffn_conv_w, m_ffn_conv_b,
              m_ffn_w_down, m_norm_f_w]
    v_args = [v_norm1_w, v_w_in, v_merge_gate_b, v_ret_gn_w, v_w_ret_o, v_lru_conv_w, v_lru_conv_b, v_lru_w_r, v_lru_b_r,
              v_lru_w_i, v_lru_b_i, v_lru_lambda, v_w_lru_o, v_w_out, v_norm2_w, v_ffn_w_up, v_ffn_conv_w, v_ffn_conv_b,
              v_ffn_w_down, v_norm_f_w]
    orig_shape = {n: a.shape for n, a in zip(names, w_args)}
    local_shape = {n: s for n, s, _ in BIG + SMALL_SHARDED}
    local_shape.update({n: s for n, s in REPLICATED})
    W = {n: a.reshape(local_shape[n]) for n, a in zip(names, w_args)}
    M = {n: a.reshape(local_shape[n]) for n, a in zip(names, m_args)}
    V = {n: a.reshape(local_shape[n]) for n, a in zip(names, v_args)}

    xi, yi, ci = _mesh_pos()
    dev = 4 * xi + 2 * yi + ci
    chip = (2 * xi + yi).astype(jnp.int32).reshape(1)
    core = ci.astype(jnp.int32).reshape(1)

    big_pack = _pack_rows([W[n].astype(BF16).reshape(-1) for n, _, _ in BIG], 16)
    small_pack = _pack_rows([W[n].reshape(-1) for n, _, _ in SMALL_SHARDED], 8)
    big_all = _all_gather(big_pack, "gather_weights").reshape(N_DEV, -1)
    small_all = _all_gather(small_pack, "gather_small_weights").reshape(N_DEV, -1)
    wb = _full_from_gathered(big_all, BIG)
    ws = _full_from_gathered(small_all, SMALL_SHARDED)
    ws["lru_b_r"] = ws["lru_b_r"].reshape(1, LRU_BLOCKS * LRU_BLOCK)
    ws["lru_b_i"] = ws["lru_b_i"].reshape(1, LRU_BLOCKS * LRU_BLOCK)
    for n, _ in REPLICATED:
        ws[n] = W[n]

    loss_acc, grad_x, g = _local_step(x, positions, loss_target, wb, ws)

    big_full_shape = {n: s for n, s, _ in BIG}
    slabs = jnp.concatenate([_slabs_from_full(g[n], big_full_shape[n], ax) for n, _, ax in BIG], axis=1)
    rows = slabs.shape[1] // LANES
    slabs = slabs.reshape(N_DEV, rows, LANES)
    from_sibling = _rs_sibling_exchange(slabs, "rs_sibling_exchange")
    partial = _rs_add_sibling(slabs, from_sibling, core, "rs_add_sibling")
    from_chips = _rs_chip_exchange(partial, "rs_chip_exchange")
    pack3 = lambda src: _pack_rows([src[n].reshape(-1) for n, _, _ in BIG], 16)
    gb, db, mb_, vb = _adamw(None, pack3(W), pack3(M), pack3(V), "adamw_big", partial=partial, recv=from_chips, chip=chip)
    big_specs = [(n, s) for n, s, _ in BIG]
    G_out = _unpack(gb.reshape(-1), big_specs)
    D_out = _unpack(db.reshape(-1), big_specs)
    M_out = _unpack(mb_.reshape(-1), big_specs)
    V_out = _unpack(vb.reshape(-1), big_specs)

    g["lru_b_r"] = g["lru_b_r"].reshape(LRU_BLOCKS, LRU_BLOCK)
    g["lru_b_i"] = g["lru_b_i"].reshape(LRU_BLOCKS, LRU_BLOCK)
    small_full = [(n, s[:ax] + (N_DEV * s[ax],) + s[ax + 1:]) for n, s, ax in SMALL_SHARDED]
    red_specs = REPLICATED + small_full + [("loss", (LANES,))]
    parts = [g[n].reshape(-1) for n, _ in REPLICATED + small_full] + [loss_acc[0]]
    red_pack = _pack_rows(parts, 8)
    red_all = _all_gather(red_pack, "gather_small_grads")
    red = _unpack(_sum_slots(red_all, "sum_small_grads").reshape(-1), red_specs)
    loss = red["loss"][0]
    gs = {n: red[n] for n, _ in REPLICATED}
    for n, s, ax in SMALL_SHARDED:
        gs[n] = _own_shard(red[n], s, ax, dev)
    small_specs = REPLICATED + [(n, s) for n, s, _ in SMALL_SHARDED]
    packs = lambda src: _pack_rows([src[n].reshape(-1) for n, _ in small_specs], 8)
    g2, d2, m2, v2 = _adamw(packs(gs), packs(W), packs(M), packs(V), "adamw_small")
    for dst, src in ((G_out, g2), (D_out, d2), (M_out, m2), (V_out, v2)):
        dst.update(_unpack(src.reshape(-1), small_specs))

    outs = [loss, grad_x]
    for group in (G_out, D_out, M_out, V_out):
        outs += [group[n].reshape(orig_shape[n]) for n in names]
    return tuple(outs)
```

```python
import functools
import math

import jax
import jax.numpy as jnp
from jax import lax
from jax.experimental import pallas as pl
from jax.experimental.pallas import tpu as pltpu

F32 = jnp.float32
BF16 = jnp.bfloat16
MESH = pl.DeviceIdType.MESH

D_MODEL = 1024
CHUNK = 64
RET_HEADS = 4
RET_DK = 128
RET_DV = 256
LRU_BLOCKS = 4
LRU_BLOCK = 256
LRU_CONV = 4
LRU_C = 8.0
D_FF = 3072
FFN_CONV = 3
ROPE_BASE = 10000.0
RMS_EPS = 1e-6
GN_EPS = 1e-6
D_IN = 7168
ADAM_LR, ADAM_B1, ADAM_B2, ADAM_EPS, ADAM_WD, ADAM_STEP = 0.001, 0.9, 0.999, 1e-08, 0.01, 10

N_DEV = 8
V7X_VMEM_BYTES = 64 * 1024 * 1024
VMEM_LIMIT = 56 * 1024 * 1024
RET_BLOCK = 256
LANES = 128

COL_Q, COL_K = 0, 4
COL_V, COL_G, COL_XL, COL_YL = 4, 8, 12, 16
COL_GR, COL_GL = 5, 6


def _cparams(sem):
    return pltpu.CompilerParams(dimension_semantics=sem, vmem_limit_bytes=VMEM_LIMIT)


def _gelu(x):
    c = math.sqrt(2.0 / math.pi)
    t = jnp.tanh(c * (x + 0.044715 * x * x * x))
    return 0.5 * x * (1.0 + t)


def _gelu_and_grad(x):
    c = math.sqrt(2.0 / math.pi)
    x2 = x * x
    t = jnp.tanh(c * (x + 0.044715 * x2 * x))
    g = 0.5 * x * (1.0 + t)
    dg = 0.5 * (1.0 + t) + 0.5 * x * (1.0 - t * t) * c * (1.0 + 3.0 * 0.044715 * x2)
    return g, dg


def _sigmoid(x):
    return 1.0 / (1.0 + jnp.exp(-x))


def _shift_down(x, s, fill):
    r = pltpu.roll(x, s, 0)
    rows = lax.broadcasted_iota(jnp.int32, x.shape, 0)
    return jnp.where(rows >= s, r, fill)


def _shift_up(x, s, fill):
    n = x.shape[0]
    r = pltpu.roll(x, n - s, 0)
    rows = lax.broadcasted_iota(jnp.int32, x.shape, 0)
    return jnp.where(rows < n - s, r, fill)


def _dot(a, b, dims):
    return lax.dot_general(a, b, (dims, ((), ())), preferred_element_type=F32)


NN = ((1,), (0,))
NT = ((1,), (1,))
TN = ((0,), (0,))


def _matmul(a, b, mode, out_dtype, tm, tn, tk, name, add=None, b_col_off=0, n_out=None):
    if mode == "tn":
        K, M = a.shape
    else:
        M, K = a.shape
    if mode == "nt":
        N = b.shape[0] if n_out is None else n_out
    else:
        N = b.shape[1] if n_out is None else n_out
    tm, tn, tk = min(tm, M), min(tn, N), min(tk, K)
    assert M % tm == 0 and N % tn == 0 and K % tk == 0
    nk = K // tk
    dims = {"nn": NN, "nt": NT, "tn": TN}[mode]

    def body(*refs):
        if add is None:
            a_ref, b_ref, o_ref, acc = refs
            add_ref = None
        else:
            a_ref, b_ref, add_ref, o_ref, acc = refs
        k = pl.program_id(2)
        p = _dot(a_ref[...], b_ref[...], dims)

        def finish(r):
            if add_ref is not None:
                r = r + add_ref[...].astype(F32)
            o_ref[...] = r.astype(out_dtype)

        if nk == 1:
            finish(p)
        else:
            @pl.when(k == 0)
            def _():
                acc[...] = p

            @pl.when(k > 0)
            def _():
                acc[...] += p

            @pl.when(k == nk - 1)
            def _():
                finish(acc[...])

    if mode == "tn":
        a_spec = pl.BlockSpec((tk, tm), lambda i, j, k: (k, i))
    else:
        a_spec = pl.BlockSpec((tm, tk), lambda i, j, k: (i, k))
    if mode == "nt":
        b_spec = pl.BlockSpec((tn, tk), lambda i, j, k: (j + b_col_off, k))
    else:
        b_spec = pl.BlockSpec((tk, tn), lambda i, j, k: (k, j + b_col_off))
    in_specs = [a_spec, b_spec]
    args = [a, b]
    if add is not None:
        in_specs.append(pl.BlockSpec((tm, tn), lambda i, j, k: (i, j)))
        args.append(add)
    return pl.pallas_call(
        body,
        name=name,
        grid=(M // tm, N // tn, nk),
        in_specs=in_specs,
        out_specs=pl.BlockSpec((tm, tn), lambda i, j, k: (i, j)),
        out_shape=jax.ShapeDtypeStruct((M, N), out_dtype),
        scratch_shapes=[pltpu.VMEM((tm, tn) if nk > 1 else (8, LANES), F32)],
        compiler_params=_cparams(("parallel", "parallel", "arbitrary")),
    )(*args)


def _rmsnorm_fwd(x, w, tm, name):
    T, D = x.shape

    def body(x_ref, w_ref, h_ref):
        xv = x_ref[...]
        r = lax.rsqrt(jnp.mean(xv * xv, axis=-1, keepdims=True) + RMS_EPS)
        h_ref[...] = (xv * r * w_ref[...]).astype(BF16)

    return pl.pallas_call(
        body, name=name, grid=(T // tm,),
        in_specs=[pl.BlockSpec((tm, D), lambda i: (i, 0)), pl.BlockSpec((1, D), lambda i: (0, 0))],
        out_specs=pl.BlockSpec((tm, D), lambda i: (i, 0)),
        out_shape=jax.ShapeDtypeStruct((T, D), BF16),
        compiler_params=_cparams(("parallel",)),
    )(x, w)


def _rmsnorm_bwd_add(dres, dh, x, w, tm, name, want_bf16):
    T, D = x.shape

    def body(dres_ref, dh_ref, x_ref, w_ref, *outs):
        if want_bf16:
            dx_ref, dxb_ref, dw_ref = outs
        else:
            dx_ref, dw_ref = outs
        i = pl.program_id(0)
        xv = x_ref[...]
        r = lax.rsqrt(jnp.mean(xv * xv, axis=-1, keepdims=True) + RMS_EPS)
        xh = xv * r
        dh_v = dh_ref[...].astype(F32)
        dxh = dh_v * w_ref[...]
        dx = dres_ref[...] + r * (dxh - xh * jnp.mean(dxh * xh, axis=-1, keepdims=True))
        dx_ref[...] = dx
        if want_bf16:
            dxb_ref[...] = dx.astype(BF16)
        part = jnp.sum(dh_v * xh, axis=0, keepdims=True)

        @pl.when(i == 0)
        def _():
            dw_ref[...] = part

        @pl.when(i > 0)
        def _():
            dw_ref[...] += part

    tile = pl.BlockSpec((tm, D), lambda i: (i, 0))
    row = pl.BlockSpec((1, D), lambda i: (0, 0))
    out_specs = [tile] + ([tile] if want_bf16 else []) + [row]
    out_shape = ([jax.ShapeDtypeStruct((T, D), F32)] + ([jax.ShapeDtypeStruct((T, D), BF16)] if want_bf16 else [])
                 + [jax.ShapeDtypeStruct((1, D), F32)])
    return pl.pallas_call(
        body, name=name, grid=(T // tm,),
        in_specs=[tile, tile, tile, row], out_specs=out_specs, out_shape=out_shape,
        compiler_params=_cparams(("arbitrary",)),
    )(dres, dh, x, w)


def _loss_head(x2, target, wf, tm, name):
    T, D = x2.shape

    def body(x_ref, t_ref, w_ref, dx_ref, loss_ref, dw_ref):
        i = pl.program_id(0)
        xv = x_ref[...]
        r = lax.rsqrt(jnp.mean(xv * xv, axis=-1, keepdims=True) + RMS_EPS)
        xh = xv * r
        wv = w_ref[...]
        e = xh * wv - t_ref[...]
        lpart = 0.5 * jnp.sum(jnp.sum(e * e, axis=-1, keepdims=True), axis=0, keepdims=True) * (1.0 / D)
        dy = e * (1.0 / D)
        dxh = dy * wv
        dx_ref[...] = r * (dxh - xh * jnp.mean(dxh * xh, axis=-1, keepdims=True))
        wpart = jnp.sum(dy * xh, axis=0, keepdims=True)
        lfull = jnp.broadcast_to(lpart, (8, LANES))

        @pl.when(i == 0)
        def _():
            loss_ref[...] = lfull
            dw_ref[...] = wpart

        @pl.when(i > 0)
        def _():
            loss_ref[...] += lfull
            dw_ref[...] += wpart

    tile = pl.BlockSpec((tm, D), lambda i: (i, 0))
    row = pl.BlockSpec((1, D), lambda i: (0, 0))
    return pl.pallas_call(
        body, name=name, grid=(T // tm,),
        in_specs=[tile, tile, row],
        out_specs=[tile, pl.BlockSpec((8, LANES), lambda i: (0, 0)), row],
        out_shape=[jax.ShapeDtypeStruct((T, D), F32), jax.ShapeDtypeStruct((8, LANES), F32),
                   jax.ShapeDtypeStruct((1, D), F32)],
        compiler_params=_cparams(("arbitrary",)),
    )(x2, target, wf)


def _rope_tables(pos_col, inv2, tm, name):
    T = pos_col.shape[0]

    def body(p_ref, f_ref, c_ref, s_ref):
        ang = p_ref[...] * f_ref[...]
        lane = lax.broadcasted_iota(jnp.int32, ang.shape, 1)
        c_ref[...] = jnp.cos(ang)
        s_ref[...] = jnp.where(lane < RET_DK // 2, -1.0, 1.0) * jnp.sin(ang)

    tile = pl.BlockSpec((tm, RET_DK), lambda i: (i, 0))
    return pl.pallas_call(
        body, name=name, grid=(T // tm,),
        in_specs=[pl.BlockSpec((tm, 1), lambda i: (i, 0)), pl.BlockSpec((1, RET_DK), lambda i: (0, 0))],
        out_specs=[tile, tile],
        out_shape=[jax.ShapeDtypeStruct((T, RET_DK), F32)] * 2,
        compiler_params=_cparams(("parallel",)),
    )(pos_col, inv2)


def _mix_fwd(a_in, b_in, proj, x, w_ro, w_lo, w_out, mb, w2, tm, name):
    T, D = x.shape

    def body(a_ref, b_ref, gr_ref, gl_ref, x_ref, wro_ref, wlo_ref, wout_ref, mb_ref, w2_ref,
             x1_ref, mix_ref, h2_ref):
        ya = _dot(a_ref[...], wro_ref[...], NN)
        yb = _dot(b_ref[...], wlo_ref[...], NN)
        sa = _sigmoid(gr_ref[...] + mb_ref[0:1, :])
        sb = _sigmoid(gl_ref[...] + mb_ref[1:2, :])
        mix = (sa * ya + sb * yb).astype(BF16)
        mix_ref[...] = mix
        x1 = x_ref[...] + _dot(mix, wout_ref[...], NN)
        x1_ref[...] = x1
        r = lax.rsqrt(jnp.mean(x1 * x1, axis=-1, keepdims=True) + RMS_EPS)
        h2_ref[...] = (x1 * r * w2_ref[...]).astype(BF16)

    tile = pl.BlockSpec((tm, D), lambda i: (i, 0))
    wspec = pl.BlockSpec((D, D), lambda i: (0, 0))
    return pl.pallas_call(
        body, name=name, grid=(T // tm,),
        in_specs=[tile, tile,
                  pl.BlockSpec((tm, D), lambda i: (i, COL_GR)), pl.BlockSpec((tm, D), lambda i: (i, COL_GL)),
                  tile, wspec, wspec, wspec,
                  pl.BlockSpec((2, D), lambda i: (0, 0)), pl.BlockSpec((1, D), lambda i: (0, 0))],
        out_specs=[tile, tile, tile],
        out_shape=[jax.ShapeDtypeStruct((T, D), F32), jax.ShapeDtypeStruct((T, D), BF16),
                   jax.ShapeDtypeStruct((T, D), BF16)],
        compiler_params=_cparams(("parallel",)),
    )(a_in, b_in, proj, proj, x, w_ro, w_lo, w_out, mb, w2)


def _mix_bwd(dx1b, a_in, b_in, proj, w_ro, w_lo, w_out, mb, tm, name):
    T, D = a_in.shape

    def body(dx_ref, a_ref, b_ref, gr_ref, gl_ref, wro_ref, wlo_ref, wout_ref, mb_ref,
             da_ref, db_ref, dgr_ref, dgl_ref, dya_ref, dyb_ref, dmb_ref):
        i = pl.program_id(0)
        dmix = _dot(dx_ref[...], wout_ref[...], NT)
        ya = _dot(a_ref[...], wro_ref[...], NN)
        yb = _dot(b_ref[...], wlo_ref[...], NN)
        sa = _sigmoid(gr_ref[...] + mb_ref[0:1, :])
        sb = _sigmoid(gl_ref[...] + mb_ref[1:2, :])
        dya = (dmix * sa).astype(BF16)
        dyb = (dmix * sb).astype(BF16)
        dgr = dmix * ya * sa * (1.0 - sa)
        dgl = dmix * yb * sb * (1.0 - sb)
        dya_ref[...] = dya
        dyb_ref[...] = dyb
        dgr_ref[...] = dgr.astype(BF16)
        dgl_ref[...] = dgl.astype(BF16)
        da_ref[...] = _dot(dya, wro_ref[...], NT)
        db_ref[...] = _dot(dyb, wlo_ref[...], NT)

        @pl.when(i == 0)
        def _():
            dmb_ref[...] = jnp.zeros_like(dmb_ref)

        dmb_ref[0:1, :] += jnp.sum(dgr, axis=0, keepdims=True)
        dmb_ref[1:2, :] += jnp.sum(dgl, axis=0, keepdims=True)

    tile = pl.BlockSpec((tm, D), lambda i: (i, 0))
    wspec = pl.BlockSpec((D, D), lambda i: (0, 0))
    two = pl.BlockSpec((2, D), lambda i: (0, 0))
    return pl.pallas_call(
        body, name=name, grid=(T // tm,),
        in_specs=[tile, tile, tile,
                  pl.BlockSpec((tm, D), lambda i: (i, COL_GR)), pl.BlockSpec((tm, D), lambda i: (i, COL_GL)),
                  wspec, wspec, wspec, two],
        out_specs=[tile] * 6 + [two],
        out_shape=[jax.ShapeDtypeStruct((T, D), F32)] * 2 + [jax.ShapeDtypeStruct((T, D), BF16)] * 4
                  + [jax.ShapeDtypeStruct((2, D), F32)],
        compiler_params=_cparams(("arbitrary",)),
    )(dx1b, a_in, b_in, proj, proj, w_ro, w_lo, w_out, mb)


def _ret_decay_consts(lg):
    L = RET_BLOCK
    n = lax.broadcasted_iota(jnp.int32, (L, L), 0)
    m = lax.broadcasted_iota(jnp.int32, (L, L), 1)
    cn, cm = n // CHUNK, m // CHUNK
    expo = jnp.where(cn == cm, jnp.abs(n - m), n - m).astype(F32)
    wm = jnp.where(cm <= cn, jnp.exp(lg * expo), 0.0)
    idx = lax.broadcasted_iota(jnp.int32, (L, 1), 0).astype(F32)
    qd = jnp.exp(lg * (idx + 1.0))
    kd = jnp.exp(lg * (L - 1.0 - idx))
    bd = jnp.exp(lg * float(L))
    return wm, qd, kd, bd


def _rotate(v, cos2, sin2s):
    return v * cos2 + pltpu.roll(v, RET_DK // 2, 1) * sin2s


def _rotate_t(d, cos2, sin2s):
    return d * cos2 - pltpu.roll(d, RET_DK // 2, 1) * sin2s


def _retention_fwd(proj, cos2, sin2s, lgam, gn_w, B, S, name):
    T = B * S
    nb = S // RET_BLOCK
    scale = RET_DK ** -0.5

    def body(q_ref, k_ref, v_ref, g_ref, c_ref, s_ref, lg_ref, gw_ref, o_ref, a_ref, qr, kr, st):
        wm, qd, kd, bd = _ret_decay_consts(lg_ref[0:1, 0:1])
        cos2, sin2s = c_ref[...], s_ref[...]
        qr[...] = _rotate(q_ref[...], cos2, sin2s)
        kr[...] = _rotate(k_ref[...], cos2, sin2s) * scale
        st[...] = jnp.zeros_like(st)
        gw = gw_ref[...]
        for j in range(nb):
            rows = pl.ds(j * RET_BLOCK, RET_BLOCK)
            qb = qr[rows, :]
            kb = kr[rows, :]
            vb = v_ref[rows, :].astype(BF16)
            sc = _dot(qb.astype(BF16), kb.astype(BF16), NT) * wm
            o = _dot(sc.astype(BF16), vb, NN) + _dot((qb * qd).astype(BF16), st[...].astype(BF16), NN)
            st[...] = st[...] * bd + _dot((kb * kd).astype(BF16), vb, TN)
            o_ref[rows, :] = o
            mu = jnp.mean(o, axis=-1, keepdims=True)
            oc = o - mu
            var = jnp.mean(oc * oc, axis=-1, keepdims=True)
            y = oc * lax.rsqrt(var + GN_EPS) * gw
            g = g_ref[rows, :]
            a_ref[rows, :] = (y * (g * _sigmoid(g))).astype(BF16)

    blk = lambda w, off: pl.BlockSpec((S, w), lambda b, h: (b, off + h))
    return pl.pallas_call(
        body, name=name, grid=(B, RET_HEADS),
        in_specs=[blk(RET_DK, COL_Q), blk(RET_DK, COL_K), blk(RET_DV, COL_V), blk(RET_DV, COL_G),
                  pl.BlockSpec((S, RET_DK), lambda b, h: (b, 0)), pl.BlockSpec((S, RET_DK), lambda b, h: (b, 0)),
                  pl.BlockSpec((None, 8, LANES), lambda b, h: (h, 0, 0)),
                  pl.BlockSpec((1, RET_DV), lambda b, h: (0, h))],
        out_specs=[blk(RET_DV, 0), blk(RET_DV, 0)],
        out_shape=[jax.ShapeDtypeStruct((T, RET_HEADS * RET_DV), F32),
                   jax.ShapeDtypeStruct((T, RET_HEADS * RET_DV), BF16)],
        scratch_shapes=[pltpu.VMEM((S, RET_DK), F32), pltpu.VMEM((S, RET_DK), F32),
                        pltpu.VMEM((RET_DK, RET_DV), F32)],
        compiler_params=_cparams(("parallel", "parallel")),
    )(proj, proj, proj, proj, cos2, sin2s, lgam, gn_w)


def _retention_bwd(da_in, o, proj, cos2, sin2s, lgam, gn_w, B, S, name):
    T = B * S
    nb = S // RET_BLOCK
    scale = RET_DK ** -0.5

    def body(da_ref, o_ref, q_ref, k_ref, v_ref, g_ref, c_ref, s_ref, lg_ref, gw_ref,
             dq_ref, dk_ref, dv_ref, dg_ref, dgw_ref, qr, kr, do_s, sts, rst):
        b = pl.program_id(1)
        wm, qd, kd, bd = _ret_decay_consts(lg_ref[0:1, 0:1])
        cos2, sin2s = c_ref[...], s_ref[...]
        qr[...] = _rotate(q_ref[...], cos2, sin2s)
        kr[...] = _rotate(k_ref[...], cos2, sin2s) * scale
        gw = gw_ref[...]
        st = jnp.zeros((RET_DK, RET_DV), F32)
        dgw = jnp.zeros((1, RET_DV), F32)
        for j in range(nb):
            rows = pl.ds(j * RET_BLOCK, RET_BLOCK)
            ov = o_ref[rows, :]
            mu = jnp.mean(ov, axis=-1, keepdims=True)
            oc = ov - mu
            rstd = lax.rsqrt(jnp.mean(oc * oc, axis=-1, keepdims=True) + GN_EPS)
            y = oc * rstd
            g = g_ref[rows, :]
            sg = _sigmoid(g)
            da = da_ref[rows, :]
            dg_ref[rows, :] = (da * (y * gw) * (sg * (1.0 + g * (1.0 - sg)))).astype(BF16)
            dyw = da * (g * sg)
            dgw = dgw + jnp.sum(dyw * y, axis=0, keepdims=True)
            dy = dyw * gw
            do_s[rows, :] = rstd * (dy - jnp.mean(dy, axis=-1, keepdims=True)
                                    - y * jnp.mean(dy * y, axis=-1, keepdims=True))
            sts[j] = st
            st = st * bd + _dot((kr[rows, :] * kd).astype(BF16), v_ref[rows, :].astype(BF16), TN)

        @pl.when(b == 0)
        def _():
            dgw_ref[...] = dgw

        @pl.when(b > 0)
        def _():
            dgw_ref[...] += dgw

        rst[...] = jnp.zeros_like(rst)
        for j in reversed(range(nb)):
            rows = pl.ds(j * RET_BLOCK, RET_BLOCK)
            qb = qr[rows, :]
            kb = kr[rows, :]
            qbb, kbb = qb.astype(BF16), kb.astype(BF16)
            vb = v_ref[rows, :].astype(BF16)
            dob = do_s[rows, :]
            dobb = dob.astype(BF16)
            a_m = (_dot(qbb, kbb, NT) * wm).astype(BF16)
            b_m = (_dot(dobb, vb, NT) * wm).astype(BF16)
            rb = rst[...].astype(BF16)
            dq = _dot(b_m, kbb, NN) + _dot((dob * qd).astype(BF16), sts[j].astype(BF16), NT)
            dk = _dot(b_m, qbb, TN) + kd * _dot(vb, rb, NT)
            dv = _dot(a_m, dobb, TN) + kd * _dot(kbb, rb, NN)
            rst[...] = rst[...] * bd + _dot((qb * qd).astype(BF16), dobb, TN)
            cb, sb = c_ref[rows, :], s_ref[rows, :]
            dq_ref[rows, :] = _rotate_t(dq, cb, sb).astype(BF16)
            dk_ref[rows, :] = _rotate_t(dk * scale, cb, sb).astype(BF16)
            dv_ref[rows, :] = dv.astype(BF16)

    blk = lambda w, off: pl.BlockSpec((S, w), lambda h, b: (b, off + h))
    return pl.pallas_call(
        body, name=name, grid=(RET_HEADS, B),
        in_specs=[blk(RET_DV, 0), blk(RET_DV, 0),
                  blk(RET_DK, COL_Q), blk(RET_DK, COL_K), blk(RET_DV, COL_V), blk(RET_DV, COL_G),
                  pl.BlockSpec((S, RET_DK), lambda h, b: (b, 0)), pl.BlockSpec((S, RET_DK), lambda h, b: (b, 0)),
                  pl.BlockSpec((None, 8, LANES), lambda h, b: (h, 0, 0)),
                  pl.BlockSpec((1, RET_DV), lambda h, b: (0, h))],
        out_specs=[blk(RET_DK, 0), blk(RET_DK, 0), blk(RET_DV, 0), blk(RET_DV, 0),
                   pl.BlockSpec((1, RET_DV), lambda h, b: (0, h))],
        out_shape=[jax.ShapeDtypeStruct((T, RET_HEADS * RET_DK), BF16)] * 2
                  + [jax.ShapeDtypeStruct((T, RET_HEADS * RET_DV), BF16)] * 2
                  + [jax.ShapeDtypeStruct((1, RET_HEADS * RET_DV), F32)],
        scratch_shapes=[pltpu.VMEM((S, RET_DK), F32), pltpu.VMEM((S, RET_DK), F32),
                        pltpu.VMEM((S, RET_DV), F32), pltpu.VMEM((nb, RET_DK, RET_DV), F32),
                        pltpu.VMEM((RET_DK, RET_DV), F32)],
        compiler_params=_cparams(("parallel", "arbitrary")),
    )(da_in, o, proj, proj, proj, proj, cos2, sin2s, lgam, gn_w)


def _lru_gates(x, cw, cb, wr, wi, br, bi, lam):
    xc = cb + cw[LRU_CONV - 1:LRU_CONV, :] * x
    for j in range(LRU_CONV - 1):
        xc = xc + cw[j:j + 1, :] * _shift_down(x, LRU_CONV - 1 - j, 0.0)
    xcb = xc.astype(BF16)
    r = _sigmoid(_dot(xcb, wr, NN) + br)
    ig = _sigmoid(_dot(xcb, wi, NN) + bi)
    z = -lam
    sp = jnp.maximum(z, 0.0) + jnp.log1p(jnp.exp(-jnp.abs(z)))
    log_a = (-LRU_C) * r * sp
    a = jnp.exp(log_a)
    z2 = 2.0 * log_a
    taylor = -z2 * (1.0 + z2 * (0.5 + z2 * (1.0 / 6.0 + z2 * (1.0 / 24.0 + z2 * (1.0 / 120.0)))))
    om = jnp.where(z2 > -0.05, taylor, 1.0 - jnp.exp(z2))
    sq = jnp.sqrt(om)
    return xc, xcb, r, ig, sp, a, sq


def _lru_fwd(proj, cw, cb, wr, wi, br, bi, lam, B, S, name):
    T = B * S
    W = LRU_BLOCKS * LRU_BLOCK

    def body(x_ref, y_ref, cw_ref, cb_ref, wr_ref, wi_ref, br_ref, bi_ref, lam_ref, h_ref, bin_ref):
        xc, _, _, ig, _, a, sq = _lru_gates(x_ref[...], cw_ref[...], cb_ref[...], wr_ref[...], wi_ref[...],
                                           br_ref[...], bi_ref[...], lam_ref[...])
        bv = sq * ig * xc
        s = 1
        while s < S:
            bv = a * _shift_down(bv, s, 0.0) + bv
            if 2 * s < S:
                a = a * _shift_down(a, s, 1.0)
            s *= 2
        h_ref[...] = bv
        bin_ref[...] = (bv * _gelu(y_ref[...])).astype(BF16)

    blk = lambda off: pl.BlockSpec((S, LRU_BLOCK), lambda b, n: (b, off + n))
    vec = lambda rows: pl.BlockSpec((rows, LRU_BLOCK), lambda b, n: (0, n))
    wspec = pl.BlockSpec((None, LRU_BLOCK, LRU_BLOCK), lambda b, n: (n, 0, 0))
    return pl.pallas_call(
        body, name=name, grid=(B, LRU_BLOCKS),
        in_specs=[blk(COL_XL), blk(COL_YL), vec(LRU_CONV), vec(1), wspec, wspec, vec(1), vec(1), vec(1)],
        out_specs=[blk(0), blk(0)],
        out_shape=[jax.ShapeDtypeStruct((T, W), F32), jax.ShapeDtypeStruct((T, W), BF16)],
        compiler_params=_cparams(("parallel", "parallel")),
    )(proj, proj, cw, cb, wr, wi, br, bi, lam)


def _lru_bwd(db_in, h, proj, cw, cb, wr, wi, br, bi, lam, B, S, name):
    T = B * S
    W = LRU_BLOCKS * LRU_BLOCK

    def body(dbin_ref, h_ref, x_ref, y_ref, cw_ref, cb_ref, wr_ref, wi_ref, br_ref, bi_ref, lam_ref,
             dx_ref, dy_ref, dcw_ref, dcb_ref, dwr_ref, dwi_ref, dbr_ref, dbi_ref, dlam_ref):
        b = pl.program_id(1)
        x = x_ref[...]
        cw = cw_ref[...]
        wr, wi = wr_ref[...], wi_ref[...]
        lam = lam_ref[...]
        xc, xcb, r, ig, sp, a, sq = _lru_gates(x, cw, cb_ref[...], wr, wi, br_ref[...], bi_ref[...], lam)
        hv = h_ref[...]
        gel, dgel = _gelu_and_grad(y_ref[...])
        dbin = dbin_ref[...]
        dy_ref[...] = (dbin * hv * dgel).astype(BF16)
        dh = dbin * gel
        an = _shift_up(a, 1, 0.0)
        s = 1
        while s < S:
            dh = dh + an * _shift_up(dh, s, 0.0)
            if 2 * s < S:
                an = an * _shift_up(an, s, 1.0)
            s *= 2
        hprev = _shift_down(hv, 1, 0.0)
        d_ig = dh * sq * xc
        d_xc = dh * sq * ig
        a2 = a * a
        d_loga = dh * hprev * a - dh * ig * xc * a2 / sq
        d_r = d_loga * ((-LRU_C) * sp)
        d_sp = jnp.sum(d_loga * ((-LRU_C) * r), axis=0, keepdims=True)
        dlam = -d_sp * _sigmoid(-lam)
        d_pr = d_r * r * (1.0 - r)
        d_pi = d_ig * ig * (1.0 - ig)
        d_prb, d_pib = d_pr.astype(BF16), d_pi.astype(BF16)
        d_xc = d_xc + _dot(d_prb, wr, NT) + _dot(d_pib, wi, NT)
        dwr = _dot(xcb, d_prb, TN)
        dwi = _dot(xcb, d_pib, TN)
        dbr = jnp.sum(d_pr, axis=0, keepdims=True)
        dbi = jnp.sum(d_pi, axis=0, keepdims=True)
        @pl.when(b == 0)
        def _():
            for ref in (dcw_ref, dcb_ref, dwr_ref, dwi_ref, dbr_ref, dbi_ref, dlam_ref):
                ref[...] = jnp.zeros_like(ref)

        dx = cw[LRU_CONV - 1:LRU_CONV, :] * d_xc
        for j in range(LRU_CONV - 1):
            sft = LRU_CONV - 1 - j
            dx = dx + cw[j:j + 1, :] * _shift_up(d_xc, sft, 0.0)
            dcw_ref[j:j + 1, :] += jnp.sum(d_xc * _shift_down(x, sft, 0.0), axis=0, keepdims=True)
        dcw_ref[LRU_CONV - 1:LRU_CONV, :] += jnp.sum(d_xc * x, axis=0, keepdims=True)
        dx_ref[...] = dx.astype(BF16)
        dcb_ref[...] += jnp.sum(d_xc, axis=0, keepdims=True)
        dwr_ref[...] += dwr
        dwi_ref[...] += dwi
        dbr_ref[...] += dbr
        dbi_ref[...] += dbi
        dlam_ref[...] += dlam

    blk = lambda off: pl.BlockSpec((S, LRU_BLOCK), lambda n, b: (b, off + n))
    vec = lambda rows: pl.BlockSpec((rows, LRU_BLOCK), lambda n, b: (0, n))
    wspec = pl.BlockSpec((None, LRU_BLOCK, LRU_BLOCK), lambda n, b: (n, 0, 0))
    vshape = lambda rows: jax.ShapeDtypeStruct((rows, W), F32)
    wshape = jax.ShapeDtypeStruct((LRU_BLOCKS, LRU_BLOCK, LRU_BLOCK), F32)
    return pl.pallas_call(
        body, name=name, grid=(LRU_BLOCKS, B),
        in_specs=[blk(0), blk(0), blk(COL_XL), blk(COL_YL), vec(LRU_CONV), vec(1), wspec, wspec, vec(1), vec(1),
                  vec(1)],
        out_specs=[blk(0), blk(0), vec(LRU_CONV), vec(1), wspec, wspec, vec(1), vec(1), vec(1)],
        out_shape=[jax.ShapeDtypeStruct((T, W), BF16)] * 2
                  + [vshape(LRU_CONV), vshape(1), wshape, wshape, vshape(1), vshape(1), vshape(1)],
        compiler_params=_cparams(("parallel", "arbitrary")),
    )(db_in, h, proj, proj, cw, cb, wr, wi, br, bi, lam)


FFN_CT = 256


def _ffn_conv(gate, cw, cb):
    gc = cb + cw[FFN_CONV - 1:FFN_CONV, :] * gate
    for j in range(FFN_CONV - 1):
        gc = gc + cw[j:j + 1, :] * _shift_down(gate, FFN_CONV - 1 - j, 0.0)
    return gc


def _ffn_act_fwd(up, cw, cb, B, S, name):
    T = B * S
    nct = D_FF // FFN_CT

    def body(g_ref, v_ref, cw_ref, cb_ref, f_ref):
        gc = _ffn_conv(g_ref[...], cw_ref[...], cb_ref[...])
        f_ref[...] = (_gelu(gc) * v_ref[...]).astype(BF16)

    return pl.pallas_call(
        body, name=name, grid=(B, nct),
        in_specs=[pl.BlockSpec((S, FFN_CT), lambda b, c: (b, c)), pl.BlockSpec((S, FFN_CT), lambda b, c: (b, nct + c)),
                  pl.BlockSpec((FFN_CONV, FFN_CT), lambda b, c: (0, c)), pl.BlockSpec((1, FFN_CT), lambda b, c: (0, c))],
        out_specs=pl.BlockSpec((S, FFN_CT), lambda b, c: (b, c)),
        out_shape=jax.ShapeDtypeStruct((T, D_FF), BF16),
        compiler_params=_cparams(("parallel", "parallel")),
    )(up, up, cw, cb)


def _ffn_act_bwd(df, up, cw, cb, B, S, name):
    T = B * S
    nct = D_FF // FFN_CT

    def body(df_ref, g_ref, v_ref, cw_ref, cb_ref, dg_ref, dv_ref, dcw_ref, dcb_ref):
        b = pl.program_id(1)
        gate = g_ref[...]
        cw = cw_ref[...]
        gc = _ffn_conv(gate, cw, cb_ref[...])
        gel, dgel = _gelu_and_grad(gc)
        dfv = df_ref[...]
        dv_ref[...] = (dfv * gel).astype(BF16)
        dgc = dfv * v_ref[...] * dgel

        @pl.when(b == 0)
        def _():
            dcw_ref[...] = jnp.zeros_like(dcw_ref)
            dcb_ref[...] = jnp.zeros_like(dcb_ref)

        dgate = cw[FFN_CONV - 1:FFN_CONV, :] * dgc
        for j in range(FFN_CONV - 1):
            sft = FFN_CONV - 1 - j
            dgate = dgate + cw[j:j + 1, :] * _shift_up(dgc, sft, 0.0)
            dcw_ref[j:j + 1, :] += jnp.sum(dgc * _shift_down(gate, sft, 0.0), axis=0, keepdims=True)
        dcw_ref[FFN_CONV - 1:FFN_CONV, :] += jnp.sum(dgc * gate, axis=0, keepdims=True)
        dg_ref[...] = dgate.astype(BF16)
        dcb_ref[...] += jnp.sum(dgc, axis=0, keepdims=True)

    blk = pl.BlockSpec((S, FFN_CT), lambda c, b: (b, c))
    return pl.pallas_call(
        body, name=name, grid=(nct, B),
        in_specs=[blk, blk, pl.BlockSpec((S, FFN_CT), lambda c, b: (b, nct + c)),
                  pl.BlockSpec((FFN_CONV, FFN_CT), lambda c, b: (0, c)), pl.BlockSpec((1, FFN_CT), lambda c, b: (0, c))],
        out_specs=[blk, blk, pl.BlockSpec((FFN_CONV, FFN_CT), lambda c, b: (0, c)),
                   pl.BlockSpec((1, FFN_CT), lambda c, b: (0, c))],
        out_shape=[jax.ShapeDtypeStruct((T, D_FF), BF16)] * 2
                  + [jax.ShapeDtypeStruct((FFN_CONV, D_FF), F32), jax.ShapeDtypeStruct((1, D_FF), F32)],
        compiler_params=_cparams(("parallel", "arbitrary")),
    )(df, up, up, cw, cb)


def _local_step(x3, positions, target3, wb, ws):
    B, S, D = x3.shape
    T = B * S
    x = x3.reshape(T, D)
    target = target3.reshape(T, D)
    tm = min(512, T)
    big = min(1024, T)

    half = RET_DK // 2
    inv_freq = ROPE_BASE ** (-jnp.arange(half, dtype=F32) / half)
    inv2 = jnp.concatenate([inv_freq, inv_freq]).reshape(1, RET_DK)
    log_gamma = jnp.log1p(-jnp.power(2.0, -5.0 - jnp.arange(RET_HEADS, dtype=F32)))
    lgam = jnp.broadcast_to(log_gamma[:, None, None], (RET_HEADS, 8, LANES))
    pos_col = positions.astype(F32).reshape(T, 1)
    cos2, sin2s = _rope_tables(pos_col, inv2, tm, "rope_tables")

    h1 = _rmsnorm_fwd(x, ws["norm1_w"], tm, "norm1_fwd")
    proj = _matmul(h1, wb["w_in"], "nn", F32, big, 1024, 1024, "proj_fwd")
    o, a_in = _retention_fwd(proj, cos2, sin2s, lgam, ws["ret_gn_w"], B, S, "retention_fwd")
    hl, b_in = _lru_fwd(proj, ws["lru_conv_w"], ws["lru_conv_b"], wb["lru_w_r"], wb["lru_w_i"],
                        ws["lru_b_r"], ws["lru_b_i"], ws["lru_lambda"], B, S, "lru_fwd")
    x1, mix, h2 = _mix_fwd(a_in, b_in, proj, x, wb["w_ret_o"], wb["w_lru_o"], wb["w_out"],
                           ws["merge_gate_b"], ws["norm2_w"], tm, "mix_fwd")
    up = _matmul(h2, wb["ffn_w_up"], "nn", F32, big, 1024, 1024, "ffn_up_fwd")
    f = _ffn_act_fwd(up, ws["ffn_conv_w"], ws["ffn_conv_b"], B, S, "ffn_act_fwd")
    x2 = _matmul(f, wb["ffn_w_down"], "nn", F32, big, 1024, 1024, "ffn_down_fwd", add=x1)
    dx2, loss_acc, d_norm_f = _loss_head(x2, target, ws["norm_f_w"], tm, "loss_head")

    g = {}
    g["norm_f_w"] = d_norm_f
    dx2b = dx2.astype(BF16)
    df = _matmul(dx2b, wb["ffn_w_down"], "nt", F32, big, 1024, 1024, "ffn_down_bwd_x")
    g["ffn_w_down"] = _matmul(f, dx2b, "tn", F32, 1024, 1024, big, "ffn_down_bwd_w")
    dgate, dval, g["ffn_conv_w"], g["ffn_conv_b"] = _ffn_act_bwd(df, up, ws["ffn_conv_w"], ws["ffn_conv_b"], B, S,
                                                                 "ffn_act_bwd")
    dup = jnp.concatenate([dgate, dval], axis=1)
    dh2 = _matmul(dup, wb["ffn_w_up"], "nt", F32, big, 1024, 1024, "ffn_up_bwd_x")
    g["ffn_w_up"] = _matmul(h2, dup, "tn", F32, 1024, 1024, big, "ffn_up_bwd_w")
    dx1, dx1b, g["norm2_w"] = _rmsnorm_bwd_add(dx2, dh2, x1, ws["norm2_w"], tm, "norm2_bwd", True)
    da_in, db_in, dgr, dgl, dya, dyb, g["merge_gate_b"] = _mix_bwd(
        dx1b, a_in, b_in, proj, wb["w_ret_o"], wb["w_lru_o"], wb["w_out"], ws["merge_gate_b"], tm, "mix_bwd")
    g["w_out"] = _matmul(mix, dx1b, "tn", F32, 1024, 1024, big, "w_out_bwd_w")
    g["w_ret_o"] = _matmul(a_in, dya, "tn", F32, 1024, 1024, big, "w_ret_o_bwd_w")
    g["w_lru_o"] = _matmul(b_in, dyb, "tn", F32, 1024, 1024, big, "w_lru_o_bwd_w")
    (dxl, dyl, g["lru_conv_w"], g["lru_conv_b"], g["lru_w_r"], g["lru_w_i"], g["lru_b_r"], g["lru_b_i"],
     g["lru_lambda"]) = _lru_bwd(db_in, hl, proj, ws["lru_conv_w"], ws["lru_conv_b"], wb["lru_w_r"], wb["lru_w_i"],
                                 ws["lru_b_r"], ws["lru_b_i"], ws["lru_lambda"], B, S, "lru_bwd")
    dq, dk, dv, dg, g["ret_gn_w"] = _retention_bwd(da_in, o, proj, cos2, sin2s, lgam, ws["ret_gn_w"], B, S,
                                                   "retention_bwd")
    dproj = jnp.concatenate([dq, dk, dv, dg, dxl, dyl, dgr, dgl], axis=1)
    dh1 = _matmul(dproj, wb["w_in"], "nt", F32, big, 1024, 1024, "proj_bwd_x")
    g["w_in"] = _matmul(h1, dproj, "tn", F32, 1024, 1024, big, "proj_bwd_w")
    grad_x, g["norm1_w"] = _rmsnorm_bwd_add(dx1, dh1, x, ws["norm1_w"], tm, "norm1_bwd", False)
    return loss_acc, grad_x.reshape(B, S, D), g


HBM_SPEC = pl.BlockSpec(memory_space=pl.ANY)


def _mesh_pos():
    return lax.axis_index("x"), lax.axis_index("y"), lax.axis_index("c")


def _other_chips(x, y):
    return [(1 - x, y), (x, 1 - y), (1 - x, 1 - y)]


def _full_shape(shard_shape, mode):
    if mode == "rows":
        return (N_DEV * shard_shape[0],) + tuple(shard_shape[1:])
    if mode == "cols":
        return (shard_shape[0], N_DEV * shard_shape[1])
    if mode == "mid":
        return (shard_shape[0], N_DEV * shard_shape[1], shard_shape[2])
    return (N_DEV,) + tuple(shard_shape)


def _extent(shard_shape, mode):
    return {"rows": shard_shape[0], "cols": shard_shape[1], "mid": shard_shape[1], "stack": 1}[mode]


def _window(ref, mode, extent, d):
    if mode == "stack":
        return ref.at[d]
    start = pl.multiple_of(d * extent, extent)
    if mode == "rows":
        return ref.at[pl.ds(start, extent)]
    if mode == "cols":
        return ref.at[:, pl.ds(start, extent)]
    return ref.at[:, pl.ds(start, extent), :]


def _all_gather_multi(shards, modes, name):
    n = len(shards)
    extents = [_extent(s.shape, m) for s, m in zip(shards, modes)]

    def body(*refs):
        x_refs, out_refs = refs[:n], refs[n:2 * n]
        send_sems, recv_sems, local_sems = refs[2 * n:]
        x, y, c = _mesh_pos()
        me, sibling = (x, y, c), (x, y, 1 - c)
        chips = _other_chips(x, y)

        def slot(i, px, py, pc):
            return _window(out_refs[i], modes[i], extents[i], 4 * px + 2 * py + pc)

        def copy(i, k, block, to, src=None):
            return pltpu.make_async_remote_copy(
                src_ref=slot(i, *block) if src is None else src, dst_ref=slot(i, *block),
                send_sem=send_sems.at[i, k], recv_sem=recv_sems.at[i, k], device_id=to, device_id_type=MESH)

        mine = [pltpu.make_async_copy(x_refs[i], slot(i, *me), local_sems.at[i]) for i in range(n)]
        sends = []
        for i in range(n):
            mine[i].start()
            first = [copy(i, 0, me, sibling, src=x_refs[i])]
            first += [copy(i, 1 + j, me, (*chip, c), src=x_refs[i]) for j, chip in enumerate(chips)]
            for cp in first:
                cp.start()
            sends += first
        for i in range(n):
            for j, chip in enumerate(chips):
                copy(i, 1 + j, (*chip, c), me).wait_recv()
                fwd = copy(i, 4 + j, (*chip, c), sibling)
                fwd.start()
                sends.append(fwd)
        for i in range(n):
            copy(i, 0, sibling, me).wait_recv()
            for j, chip in enumerate(chips):
                copy(i, 4 + j, (*chip, 1 - c), me).wait_recv()
        for cp in sends:
            cp.wait_send()
        for cp in mine:
            cp.wait()

    return pl.pallas_call(
        body, name=name,
        in_specs=[HBM_SPEC] * n, out_specs=[HBM_SPEC] * n,
        out_shape=[jax.ShapeDtypeStruct(_full_shape(s.shape, m), s.dtype) for s, m in zip(shards, modes)],
        scratch_shapes=[pltpu.SemaphoreType.DMA((n, 7)), pltpu.SemaphoreType.DMA((n, 7)),
                        pltpu.SemaphoreType.DMA((n,))],
    )(*shards)


def _rs_sibling_multi(grads, modes, shard_shapes, name):
    n = len(grads)
    extents = [_extent(s, m) for s, m in zip(shard_shapes, modes)]

    def body(*refs):
        g_refs, out_refs = refs[:n], refs[n:2 * n]
        send_sems, recv_sems = refs[2 * n:]
        x, y, c = _mesh_pos()
        copies = [
            pltpu.make_async_remote_copy(
                src_ref=_window(g_refs[i], modes[i], extents[i], 2 * k + (1 - c)), dst_ref=out_refs[i].at[k],
                send_sem=send_sems.at[i, k], recv_sem=recv_sems.at[i, k], device_id=(x, y, 1 - c),
                device_id_type=MESH)
            for i in range(n) for k in range(4)
        ]
        for cp in copies:
            cp.start()
        for cp in copies:
            cp.wait_recv()
        for cp in copies:
            cp.wait_send()

    return pl.pallas_call(
        body, name=name,
        in_specs=[HBM_SPEC] * n, out_specs=[HBM_SPEC] * n,
        out_shape=[jax.ShapeDtypeStruct((4,) + tuple(s), g.dtype) for s, g in zip(shard_shapes, grads)],
        scratch_shapes=[pltpu.SemaphoreType.DMA((n, 4)), pltpu.SemaphoreType.DMA((n, 4))],
    )(*grads)


def _rs_chip_multi(partials, name):
    n = len(partials)

    def body(*refs):
        p_refs, out_refs = refs[:n], refs[n:2 * n]
        send_sems, recv_sems = refs[2 * n:]
        x, y, c = _mesh_pos()
        copies = [
            pltpu.make_async_remote_copy(
                src_ref=p_refs[i].at[2 * px + py], dst_ref=out_refs[i].at[j],
                send_sem=send_sems.at[i, j], recv_sem=recv_sems.at[i, j], device_id=(px, py, c), device_id_type=MESH)
            for i in range(n) for j, (px, py) in enumerate(_other_chips(x, y))
        ]
        for cp in copies:
            cp.start()
        for cp in copies:
            cp.wait_recv()
        for cp in copies:
            cp.wait_send()

    return pl.pallas_call(
        body, name=name,
        in_specs=[HBM_SPEC] * n, out_specs=[HBM_SPEC] * n,
        out_shape=[jax.ShapeDtypeStruct((3,) + tuple(p.shape[1:]), p.dtype) for p in partials],
        scratch_shapes=[pltpu.SemaphoreType.DMA((n, 3)), pltpu.SemaphoreType.DMA((n, 3))],
    )(*partials)


def _rs_add(g, recv, mode, core, name):
    shard = tuple(recv.shape[1:])
    if mode == "mid":
        a, e, c2 = shard
        g_in = g.reshape(a, N_DEV, e, c2)
        grid = (4, 1)
        g_spec = pl.BlockSpec((a, None, e, c2), lambda k, i, c_ref: (0, 2 * k + c_ref[0], 0, 0))
        r_spec = pl.BlockSpec((None, a, e, c2), lambda k, i, c_ref: (k, 0, 0, 0))
    else:
        R, C = shard
        tr = _row_tile(R, 512)
        grid = (4, R // tr)
        if mode == "rows":
            g_in = g.reshape(N_DEV, R, C)
            g_spec = pl.BlockSpec((None, tr, C), lambda k, i, c_ref: (2 * k + c_ref[0], i, 0))
        else:
            g_in = g
            g_spec = pl.BlockSpec((tr, C), lambda k, i, c_ref: (i, 2 * k + c_ref[0]))
        r_spec = pl.BlockSpec((None, tr, C), lambda k, i, c_ref: (k, i, 0))

    def body(c_ref, g_ref, r_ref, o_ref):
        o_ref[...] = g_ref[...] + r_ref[...]

    return pl.pallas_call(
        body, name=name,
        grid_spec=pltpu.PrefetchScalarGridSpec(num_scalar_prefetch=1, grid=grid, in_specs=[g_spec, r_spec],
                                               out_specs=r_spec),
        out_shape=jax.ShapeDtypeStruct(recv.shape, recv.dtype),
        compiler_params=_cparams(("parallel", "parallel")),
    )(core, g_in, recv)


def _adam_update(gv, w, m, v):
    nm = ADAM_B1 * m + (1.0 - ADAM_B1) * gv
    nv = ADAM_B2 * v + (1.0 - ADAM_B2) * (gv * gv)
    m_hat = nm / (1.0 - ADAM_B1 ** ADAM_STEP)
    v_hat = nv / (1.0 - ADAM_B2 ** ADAM_STEP)
    delta = -ADAM_LR * (m_hat / (jnp.sqrt(v_hat) + ADAM_EPS) + ADAM_WD * w)
    return delta, nm, nv


def _adamw_shard(partial, recv, w, m, v, chip, name):
    shape = tuple(w.shape)
    tr = _row_tile(shape[0], 256)
    rest = shape[1:]
    zeros = (0,) * len(rest)
    tile = pl.BlockSpec((tr,) + rest, lambda i, s: (i,) + zeros)

    def body(_, p_ref, r_ref, w_ref, m_ref, v_ref, g_ref, d_ref, nm_ref, nv_ref):
        gv = p_ref[...] + r_ref[0] + r_ref[1] + r_ref[2]
        g_ref[...] = gv
        d_ref[...], nm_ref[...], nv_ref[...] = _adam_update(gv, w_ref[...], m_ref[...], v_ref[...])

    grid_spec = pltpu.PrefetchScalarGridSpec(
        num_scalar_prefetch=1, grid=(shape[0] // tr,),
        in_specs=[pl.BlockSpec((None, tr) + rest, lambda i, s: (s[0], i) + zeros),
                  pl.BlockSpec((3, tr) + rest, lambda i, s: (0, i) + zeros), tile, tile, tile],
        out_specs=[tile] * 4)
    return pl.pallas_call(body, name=name, grid_spec=grid_spec, out_shape=[jax.ShapeDtypeStruct(shape, F32)] * 4,
                          compiler_params=_cparams(("parallel",)))(chip, partial, recv, w, m, v)


SMALL_LANES = 1024


def _small_rows(shape):
    r, w = shape
    return r * max(1, w // SMALL_LANES)


def _small_allreduce(parts, name):
    n = len(parts)
    shapes = [tuple(p.shape) for p in parts]
    offs, total = [], 0
    for s in shapes:
        offs.append(total)
        total += _small_rows(s)
    rows = -(-total // 8) * 8

    def body(*refs):
        p_refs, o_refs = refs[:n], refs[n:2 * n]
        buf, tot, send_sems, recv_sems = refs[2 * n:]
        x, y, c = _mesh_pos()
        me, sibling = (x, y, c), (x, y, 1 - c)
        chips = _other_chips(x, y)

        def slot(px, py, pc):
            return buf.at[4 * px + 2 * py + pc]

        def copy(k, block, to):
            return pltpu.make_async_remote_copy(
                src_ref=slot(*block), dst_ref=slot(*block), send_sem=send_sems.at[k], recv_sem=recv_sems.at[k],
                device_id=to, device_id_type=MESH)

        tot[...] = jnp.zeros_like(tot)
        for p_ref, (r, w), off in zip(p_refs, shapes, offs):
            wl = min(w, SMALL_LANES)
            for part in range(max(1, w // SMALL_LANES)):
                tot[pl.ds(off + part * r, r), pl.ds(0, wl)] = p_ref[:, pl.ds(part * SMALL_LANES, wl)]
        slot(*me)[...] = tot[...]
        first = [copy(0, me, sibling)] + [copy(1 + j, me, (*chip, c)) for j, chip in enumerate(chips)]
        for cp in first:
            cp.start()
        passed = [copy(4 + j, (*chip, c), sibling) for j, chip in enumerate(chips)]
        for j, chip in enumerate(chips):
            copy(1 + j, (*chip, c), me).wait_recv()
            passed[j].start()
        copy(0, sibling, me).wait_recv()
        for j, chip in enumerate(chips):
            copy(4 + j, (*chip, 1 - c), me).wait_recv()
        for cp in first + passed:
            cp.wait_send()
        acc = buf[0]
        for d in range(1, N_DEV):
            acc = acc + buf[d]
        tot[...] = acc
        for o_ref, (r, w), off in zip(o_refs, shapes, offs):
            wl = min(w, SMALL_LANES)
            for part in range(max(1, w // SMALL_LANES)):
                o_ref[:, pl.ds(part * SMALL_LANES, wl)] = tot[pl.ds(off + part * r, r), pl.ds(0, wl)]

    vm = pl.BlockSpec(memory_space=pltpu.VMEM)
    return pl.pallas_call(
        body, name=name,
        in_specs=[vm] * n, out_specs=[vm] * n,
        out_shape=[jax.ShapeDtypeStruct(s, F32) for s in shapes],
        scratch_shapes=[pltpu.VMEM((N_DEV, rows, SMALL_LANES), F32), pltpu.VMEM((rows, SMALL_LANES), F32),
                        pltpu.SemaphoreType.DMA((7,)), pltpu.SemaphoreType.DMA((7,))],
    )(*parts)


def _adamw_small(gs, ws, ms, vs, name):
    n = len(gs)

    def body(*refs):
        g_r, w_r, m_r, v_r = refs[:n], refs[n:2 * n], refs[2 * n:3 * n], refs[3 * n:4 * n]
        d_r, nm_r, nv_r = refs[4 * n:5 * n], refs[5 * n:6 * n], refs[6 * n:7 * n]
        for i in range(n):
            d_r[i][...], nm_r[i][...], nv_r[i][...] = _adam_update(g_r[i][...], w_r[i][...], m_r[i][...], v_r[i][...])

    vm = pl.BlockSpec(memory_space=pltpu.VMEM)
    shapes = [jax.ShapeDtypeStruct(w.shape, F32) for w in ws]
    outs = pl.pallas_call(body, name=name, in_specs=[vm] * (4 * n), out_specs=[vm] * (3 * n),
                          out_shape=shapes * 3)(*gs, *ws, *ms, *vs)
    return outs[:n], outs[n:2 * n], outs[2 * n:]


def _row_tile(rows, cap):
    if rows <= cap:
        return rows
    best = None
    for t in range(16, cap + 1, 16):
        if rows % t == 0:
            best = t
    assert best is not None
    return best


BIG = [
    ("w_in", (1024, 896), "cols"), ("lru_w_r", (4, 32, 256), "mid"), ("lru_w_i", (4, 32, 256), "mid"),
    ("w_ret_o", (128, 1024), "rows"), ("w_lru_o", (128, 1024), "rows"), ("w_out", (128, 1024), "rows"),
    ("ffn_w_up", (1024, 768), "cols"), ("ffn_w_down", (384, 1024), "rows"),
]
SMALL_SHARDED = [("merge_gate_b", (2, 128), "cols"), ("lru_conv_w", (4, 128), "cols"), ("lru_b_r", (4, 32), "stack"),
                 ("lru_b_i", (4, 32), "stack"), ("ffn_conv_w", (3, 384), "cols")]
REPLICATED = [("norm1_w", (1, 1024)), ("ret_gn_w", (1, 1024)), ("lru_conv_b", (1, 1024)), ("lru_lambda", (1, 1024)),
              ("norm2_w", (1, 1024)), ("ffn_conv_b", (1, 3072)), ("norm_f_w", (1, 1024))]


def kernel(x, positions, norm1_w, w_in, merge_gate_b, ret_gn_w, w_ret_o, lru_conv_w, lru_conv_b, lru_w_r, lru_b_r, lru_w_i, lru_b_i, lru_lambda, w_lru_o, w_out, norm2_w, ffn_w_up, ffn_conv_w, ffn_conv_b, ffn_w_down, norm_f_w, loss_target, m_norm1_w, m_w_in, m_merge_gate_b, m_ret_gn_w, m_w_ret_o, m_lru_conv_w, m_lru_conv_b, m_lru_w_r, m_lru_b_r, m_lru_w_i, m_lru_b_i, m_lru_lambda, m_w_lru_o, m_w_out, m_norm2_w, m_ffn_w_up, m_ffn_conv_w, m_ffn_conv_b, m_ffn_w_down, m_norm_f_w, v_norm1_w, v_w_in, v_merge_gate_b, v_ret_gn_w, v_w_ret_o, v_lru_conv_w, v_lru_conv_b, v_lru_w_r, v_lru_b_r, v_lru_w_i, v_lru_b_i, v_lru_lambda, v_w_lru_o, v_w_out, v_norm2_w, v_ffn_w_up, v_ffn_conv_w, v_ffn_conv_b, v_ffn_w_down, v_norm_f_w):
    names = ["norm1_w", "w_in", "merge_gate_b", "ret_gn_w", "w_ret_o", "lru_conv_w", "lru_conv_b", "lru_w_r", "lru_b_r",
             "lru_w_i", "lru_b_i", "lru_lambda", "w_lru_o", "w_out", "norm2_w", "ffn_w_up", "ffn_conv_w", "ffn_conv_b",
             "ffn_w_down", "norm_f_w"]
    w_args = [norm1_w, w_in, merge_gate_b, ret_gn_w, w_ret_o, lru_conv_w, lru_conv_b, lru_w_r, lru_b_r, lru_w_i, lru_b_i,
              lru_lambda, w_lru_o, w_out, norm2_w, ffn_w_up, ffn_conv_w, ffn_conv_b, ffn_w_down, norm_f_w]
    m_args = [m_norm1_w, m_w_in, m_merge_gate_b, m_ret_gn_w, m_w_ret_o, m_lru_conv_w, m_lru_conv_b, m_lru_w_r, m_lru_b_r,
              m_lru_w_i, m_lru_b_i, m_lru_lambda, m_w_lru_o, m_w_out, m_norm2_w, m_ffn_w_up, m_ffn_conv_w, m_ffn_conv_b,
              m_ffn_w_down, m_norm_f_w]
    v_args = [v_norm1_w, v_w_in, v_merge_gate_b, v_ret_gn_w, v_w_ret_o, v_lru_conv_w, v_lru_conv_b, v_lru_w_r, v_lru_b_r,
              v_lru_w_i, v_lru_b_i, v_lru_lambda, v_w_lru_o, v_w_out, v_norm2_w, v_ffn_w_up, v_ffn_conv_w, v_ffn_conv_b,
              v_ffn_w_down, v_norm_f_w]
    orig_shape = {n: a.shape for n, a in zip(names, w_args)}
    local_shape = {n: s for n, s, _ in BIG + SMALL_SHARDED}
    local_shape.update({n: s for n, s in REPLICATED})
    W = {n: a.reshape(local_shape[n]) for n, a in zip(names, w_args)}
    M = {n: a.reshape(local_shape[n]) for n, a in zip(names, m_args)}
    V = {n: a.reshape(local_shape[n]) for n, a in zip(names, v_args)}

    xi, yi, ci = _mesh_pos()
    dev = 4 * xi + 2 * yi + ci
    chip = (2 * xi + yi).astype(jnp.int32).reshape(1)
    core = ci.astype(jnp.int32).reshape(1)

    big_names = [n for n, _, _ in BIG]
    big_modes = [m for _, _, m in BIG]
    small_names = [n for n, _, _ in SMALL_SHARDED]
    small_modes = [m for _, _, m in SMALL_SHARDED]
    gathered = _all_gather_multi([W[n].astype(BF16) for n in big_names] + [W[n] for n in small_names],
                                 big_modes + small_modes, "gather_weights")
    wb = dict(zip(big_names, gathered[:len(BIG)]))
    ws = dict(zip(small_names, gathered[len(BIG):]))
    for n in ("lru_b_r", "lru_b_i"):
        ws[n] = jnp.transpose(ws[n], (1, 0, 2)).reshape(1, LRU_BLOCKS * LRU_BLOCK)
    for n, _ in REPLICATED:
        ws[n] = W[n]

    loss_acc, grad_x, g = _local_step(x, positions, loss_target, wb, ws)

    big_shapes = [s for _, s, _ in BIG]
    big_grads = [g[n] for n in big_names]
    from_sibling = _rs_sibling_multi(big_grads, big_modes, big_shapes, "rs_sibling_exchange")
    partials = [_rs_add(g[n], r, m, core, "rs_add_" + n) for n, r, m in zip(big_names, from_sibling, big_modes)]
    from_chips = _rs_chip_multi(partials, "rs_chip_exchange")
    G_out, D_out, M_out, V_out = {}, {}, {}, {}
    for n, p, r in zip(big_names, partials, from_chips):
        G_out[n], D_out[n], M_out[n], V_out[n] = _adamw_shard(p, r, W[n], M[n], V[n], chip, "adamw_" + n)

    rep_names = [n for n, _ in REPLICATED]
    red_names = rep_names + small_names
    red = _small_allreduce([g[n] for n in red_names] + [loss_acc[0:1, :]], "allreduce_small_grads")
    loss = red[-1][0, 0]
    gs = dict(zip(red_names, red[:-1]))
    for n, s, mode in SMALL_SHARDED:
        if mode == "cols":
            gs[n] = lax.dynamic_slice_in_dim(gs[n], dev * s[1], s[1], axis=1)
        else:
            full = gs[n].reshape(LRU_BLOCKS, LRU_BLOCK)
            gs[n] = lax.dynamic_slice_in_dim(full, dev * s[1], s[1], axis=1)
    d2, m2, v2 = _adamw_small([gs[n] for n in red_names], [W[n] for n in red_names], [M[n] for n in red_names],
                              [V[n] for n in red_names], "adamw_small")
    for i, n in enumerate(red_names):
        G_out[n], D_out[n], M_out[n], V_out[n] = gs[n], d2[i], m2[i], v2[i]

    outs = [loss, grad_x]
    for group in (G_out, D_out, M_out, V_out):
        outs += [group[n].reshape(orig_shape[n]) for n in names]
    return tuple(outs)
```

```python
import math

import jax
import jax.numpy as jnp
from jax import lax
from jax.experimental import pallas as pl
from jax.experimental.pallas import tpu as pltpu

F32 = jnp.float32
BF16 = jnp.bfloat16
MESH = pl.DeviceIdType.MESH

D_MODEL = 1024
CHUNK = 64
RET_HEADS = 4
RET_DK = 128
RET_DV = 256
LRU_BLOCKS = 4
LRU_BLOCK = 256
LRU_CONV = 4
LRU_C = 8.0
D_FF = 3072
FFN_CONV = 3
ROPE_BASE = 10000.0
RMS_EPS = 1e-6
GN_EPS = 1e-6
D_IN = 7168
ADAM_LR, ADAM_B1, ADAM_B2, ADAM_EPS, ADAM_WD, ADAM_STEP = 0.001, 0.9, 0.999, 1e-08, 0.01, 10

N_DEV = 8
V7X_VMEM_BYTES = 64 * 1024 * 1024
VMEM_LIMIT = V7X_VMEM_BYTES - 8 * 1024 * 1024
RET_BLOCK = 256
LANES = 128

COL_Q, COL_K = 0, 4
COL_V, COL_G, COL_XL, COL_YL = 4, 8, 12, 16
COL_GR, COL_GL = 5, 6

HBM_SPEC = pl.BlockSpec(memory_space=pl.ANY)


def _gelu(x):
    c = math.sqrt(2.0 / math.pi)
    t = jnp.tanh(c * (x + 0.044715 * x * x * x))
    return 0.5 * x * (1.0 + t)


def _gelu_and_grad(x):
    c = math.sqrt(2.0 / math.pi)
    x2 = x * x
    t = jnp.tanh(c * (x + 0.044715 * x2 * x))
    g = 0.5 * x * (1.0 + t)
    dg = 0.5 * (1.0 + t) + 0.5 * x * (1.0 - t * t) * c * (1.0 + 3.0 * 0.044715 * x2)
    return g, dg


def _sigmoid(x):
    return 1.0 / (1.0 + jnp.exp(-x))


def _shift_down(x, s, fill):
    r = pltpu.roll(x, s, 0)
    rows = lax.broadcasted_iota(jnp.int32, x.shape, 0)
    return jnp.where(rows >= s, r, fill)


def _shift_up(x, s, fill):
    n = x.shape[0]
    r = pltpu.roll(x, n - s, 0)
    rows = lax.broadcasted_iota(jnp.int32, x.shape, 0)
    return jnp.where(rows < n - s, r, fill)


def _dot(a, b, dims):
    return lax.dot_general(a, b, (dims, ((), ())), preferred_element_type=F32)


NN = ((1,), (0,))
NT = ((1,), (1,))
TN = ((0,), (0,))


def _mesh_pos():
    return lax.axis_index("x"), lax.axis_index("y"), lax.axis_index("c")


def _other_chips(x, y):
    return [(1 - x, y), (x, 1 - y), (1 - x, 1 - y)]


def _full_shape(shard_shape, mode):
    if mode == "rows":
        return (N_DEV * shard_shape[0],) + tuple(shard_shape[1:])
    if mode == "cols":
        return (shard_shape[0], N_DEV * shard_shape[1])
    if mode == "mid":
        return (shard_shape[0], N_DEV * shard_shape[1], shard_shape[2])
    return (N_DEV,) + tuple(shard_shape)


def _extent(shard_shape, mode):
    return {"rows": shard_shape[0], "cols": shard_shape[1], "mid": shard_shape[1], "stack": 1}[mode]


def _window(ref, mode, extent, d):
    if mode == "stack":
        return ref.at[d]
    start = pl.multiple_of(d * extent, extent)
    if mode == "rows":
        return ref.at[pl.ds(start, extent)]
    if mode == "cols":
        return ref.at[:, pl.ds(start, extent)]
    return ref.at[:, pl.ds(start, extent), :]


class _Job:
    def __init__(self, inputs, out_shapes, sems, start, finish, aliases=None):
        self.inputs, self.out_shapes, self.sems = list(inputs), list(out_shapes), sems
        self.start, self.finish, self.aliases = start, finish, dict(aliases or {})


def _remote(src, dst, send_sem, recv_sem, to):
    return pltpu.make_async_remote_copy(src_ref=src, dst_ref=dst, send_sem=send_sem, recv_sem=recv_sem,
                                        device_id=to, device_id_type=MESH)


def _ag_first_job(shards, modes):
    n = len(shards)
    extents = [_extent(s.shape, m) for s, m in zip(shards, modes)]

    def copies(x_refs, out_refs, send, recv, local, arriving):
        x, y, c = _mesh_pos()
        peers = [(x, y, 1 - c)] + [(*chip, c) for chip in _other_chips(x, y)]
        win = lambda i, p: _window(out_refs[i], modes[i], extents[i], 4 * p[0] + 2 * p[1] + p[2])
        if arriving:
            return [_remote(x_refs[i], win(i, p), send.at[i, k], recv.at[i, k], p)
                    for i in range(n) for k, p in enumerate(peers)]
        mine = [pltpu.make_async_copy(x_refs[i], win(i, (x, y, c)), local.at[i]) for i in range(n)]
        sends = [_remote(x_refs[i], win(i, (x, y, c)), send.at[i, k], recv.at[i, k], p)
                 for i in range(n) for k, p in enumerate(peers)]
        return mine, sends

    def start(*refs):
        mine, sends = copies(*refs, False)
        for cp in mine + sends:
            cp.start()

    def finish(*refs):
        for cp in copies(*refs, True):
            cp.wait_recv()
        mine, sends = copies(*refs, False)
        for cp in sends:
            cp.wait_send()
        for cp in mine:
            cp.wait()

    out_shapes = [jax.ShapeDtypeStruct(_full_shape(s.shape, m), s.dtype) for s, m in zip(shards, modes)]
    return _Job(shards, out_shapes, ((n, 4), (n, 4), (n,)), start, finish)


def _ag_second_job(fulls, modes, shard_shapes):
    n = len(fulls)
    extents = [_extent(s, m) for s, m in zip(shard_shapes, modes)]

    def copies(_, out_refs, send, recv, local, core_of_block):
        x, y, c = _mesh_pos()
        pc = c if core_of_block == "mine" else 1 - c
        win = lambda i, chip: _window(out_refs[i], modes[i], extents[i], 4 * chip[0] + 2 * chip[1] + pc)
        return [_remote(win(i, chip), win(i, chip), send.at[i, j], recv.at[i, j], (x, y, 1 - c))
                for i in range(n) for j, chip in enumerate(_other_chips(x, y))]

    def start(*refs):
        for cp in copies(*refs, "mine"):
            cp.start()

    def finish(*refs):
        for cp in copies(*refs, "sibling"):
            cp.wait_recv()
        for cp in copies(*refs, "mine"):
            cp.wait_send()

    out_shapes = [jax.ShapeDtypeStruct(f.shape, f.dtype) for f in fulls]
    return _Job(fulls, out_shapes, ((n, 3), (n, 3), (1,)), start, finish, aliases={i: i for i in range(n)})


def _rs_sibling_job(grads, modes, shard_shapes):
    n = len(grads)
    extents = [_extent(s, m) for s, m in zip(shard_shapes, modes)]

    def copies(g_refs, out_refs, send, recv, local):
        x, y, c = _mesh_pos()
        return [_remote(_window(g_refs[i], modes[i], extents[i], 2 * k + (1 - c)), out_refs[i].at[k],
                        send.at[i, k], recv.at[i, k], (x, y, 1 - c))
                for i in range(n) for k in range(4)]

    def start(*refs):
        for cp in copies(*refs):
            cp.start()

    def finish(*refs):
        cps = copies(*refs)
        for cp in cps:
            cp.wait_recv()
        for cp in cps:
            cp.wait_send()

    out_shapes = [jax.ShapeDtypeStruct((4,) + tuple(s), g.dtype) for s, g in zip(shard_shapes, grads)]
    return _Job(grads, out_shapes, ((n, 4), (n, 4), (1,)), start, finish)


def _rs_chip_job(partials):
    n = len(partials)

    def copies(p_refs, out_refs, send, recv, local):
        x, y, c = _mesh_pos()
        return [_remote(p_refs[i].at[2 * px + py], out_refs[i].at[j], send.at[i, j], recv.at[i, j], (px, py, c))
                for i in range(n) for j, (px, py) in enumerate(_other_chips(x, y))]

    def start(*refs):
        for cp in copies(*refs):
            cp.start()

    def finish(*refs):
        cps = copies(*refs)
        for cp in cps:
            cp.wait_recv()
        for cp in cps:
            cp.wait_send()

    out_shapes = [jax.ShapeDtypeStruct((3,) + tuple(p.shape[1:]), p.dtype) for p in partials]
    return _Job(partials, out_shapes, ((n, 3), (n, 3), (1,)), start, finish)


def _all_true(conds):
    out = conds[0]
    for c in conds[1:]:
        out = jnp.logical_and(out, c)
    return out


def _pcall(body, *, name, grid, in_specs, out_specs, out_shape, args, sem, scratch=(), jobs=(), alias_in_out=None):
    n_in, n_out, n_scr = len(args), len(out_shape), len(scratch)
    job_in = [a for j in jobs for a in j.inputs]
    job_out = [s for j in jobs for s in j.out_shapes]
    job_sems = [pltpu.SemaphoreType.DMA(shape) for j in jobs for shape in j.sems]
    aliases, in_off, out_off = dict(alias_in_out or {}), n_in, n_out
    for j in jobs:
        for a, b in j.aliases.items():
            aliases[in_off + a] = out_off + b
        in_off += len(j.inputs)
        out_off += len(j.out_shapes)

    def wrapped(*refs):
        ins = refs[:n_in]
        jins = refs[n_in:n_in + len(job_in)]
        o0 = n_in + len(job_in)
        outs = refs[o0:o0 + n_out]
        jouts = refs[o0 + n_out:o0 + n_out + len(job_out)]
        s0 = o0 + n_out + len(job_out)
        scr = refs[s0:s0 + n_scr]
        jsems = refs[s0 + n_scr:]
        if jobs:
            ids = [pl.program_id(a) for a in range(len(grid))]
            first = _all_true([i == 0 for i in ids])
            last = _all_true([i == g - 1 for i, g in zip(ids, grid)])

            def per_job(which):
                i0 = o0_ = 0
                for k, j in enumerate(jobs):
                    fn = j.start if which == "start" else j.finish
                    fn(jins[i0:i0 + len(j.inputs)], jouts[o0_:o0_ + len(j.out_shapes)], *jsems[3 * k:3 * k + 3])
                    i0 += len(j.inputs)
                    o0_ += len(j.out_shapes)

            @pl.when(first)
            def _():
                per_job("start")

        body(*ins, *outs, *scr)
        if jobs:
            @pl.when(last)
            def _():
                per_job("finish")

    semantics = tuple("arbitrary" for _ in grid) if jobs else sem
    return pl.pallas_call(
        wrapped, name=name, grid=grid,
        in_specs=list(in_specs) + [HBM_SPEC] * len(job_in),
        out_specs=list(out_specs) + [HBM_SPEC] * len(job_out),
        out_shape=list(out_shape) + job_out,
        scratch_shapes=list(scratch) + job_sems,
        input_output_aliases=aliases,
        compiler_params=pltpu.CompilerParams(dimension_semantics=semantics, vmem_limit_bytes=VMEM_LIMIT),
    )(*args, *job_in)


def _row_tile(rows, cap):
    if rows <= cap:
        return rows
    best = None
    for t in range(16, cap + 1, 16):
        if rows % t == 0:
            best = t
    assert best is not None
    return best


def _matmul(a, b, mode, out_dtype, tm, tn, tk, name, add=None, jobs=()):
    if mode == "tn":
        K, M = a.shape
    else:
        M, K = a.shape
    N = b.shape[0] if mode == "nt" else b.shape[1]
    tm, tn, tk = min(tm, M), min(tn, N), min(tk, K)
    assert M % tm == 0 and N % tn == 0 and K % tk == 0
    nk = K // tk
    dims = {"nn": NN, "nt": NT, "tn": TN}[mode]

    def body(*refs):
        if add is None:
            a_ref, b_ref, o_ref, acc = refs
            add_ref = None
        else:
            a_ref, b_ref, add_ref, o_ref, acc = refs
        k = pl.program_id(2)
        p = _dot(a_ref[...], b_ref[...], dims)

        def finish(r):
            if add_ref is not None:
                r = r + add_ref[...].astype(F32)
            o_ref[...] = r.astype(out_dtype)

        if nk == 1:
            finish(p)
        else:
            @pl.when(k == 0)
            def _():
                acc[...] = p

            @pl.when(k > 0)
            def _():
                acc[...] += p

            @pl.when(k == nk - 1)
            def _():
                finish(acc[...])

    if mode == "tn":
        a_spec = pl.BlockSpec((tk, tm), lambda i, j, k: (k, i))
    else:
        a_spec = pl.BlockSpec((tm, tk), lambda i, j, k: (i, k))
    if mode == "nt":
        b_spec = pl.BlockSpec((tn, tk), lambda i, j, k: (j, k))
    else:
        b_spec = pl.BlockSpec((tk, tn), lambda i, j, k: (k, j))
    in_specs = [a_spec, b_spec]
    args = [a, b]
    if add is not None:
        in_specs.append(pl.BlockSpec((tm, tn), lambda i, j, k: (i, j)))
        args.append(add)
    return _pcall(
        body, name=name, grid=(M // tm, N // tn, nk), in_specs=in_specs,
        out_specs=[pl.BlockSpec((tm, tn), lambda i, j, k: (i, j))],
        out_shape=[jax.ShapeDtypeStruct((M, N), out_dtype)], args=args,
        scratch=[pltpu.VMEM((tm, tn) if nk > 1 else (8, LANES), F32)],
        sem=("parallel", "parallel", "arbitrary"), jobs=jobs)


def _rmsnorm_fwd(x, w, tm, name):
    T, D = x.shape

    def body(x_ref, w_ref, h_ref):
        xv = x_ref[...]
        r = lax.rsqrt(jnp.mean(xv * xv, axis=-1, keepdims=True) + RMS_EPS)
        h_ref[...] = (xv * r * w_ref[...]).astype(BF16)

    return _pcall(
        body, name=name, grid=(T // tm,),
        in_specs=[pl.BlockSpec((tm, D), lambda i: (i, 0)), pl.BlockSpec((1, D), lambda i: (0, 0))],
        out_specs=[pl.BlockSpec((tm, D), lambda i: (i, 0))],
        out_shape=[jax.ShapeDtypeStruct((T, D), BF16)], args=[x, w], sem=("parallel",))[0]


def _rmsnorm_bwd_add(dres, dh, x, w, tm, name, want_bf16, jobs=()):
    T, D = x.shape

    def body(dres_ref, dh_ref, x_ref, w_ref, *outs):
        if want_bf16:
            dx_ref, dxb_ref, dw_ref = outs
        else:
            dx_ref, dw_ref = outs
        i = pl.program_id(0)
        xv = x_ref[...]
        r = lax.rsqrt(jnp.mean(xv * xv, axis=-1, keepdims=True) + RMS_EPS)
        xh = xv * r
        dh_v = dh_ref[...].astype(F32)
        dxh = dh_v * w_ref[...]
        dx = dres_ref[...] + r * (dxh - xh * jnp.mean(dxh * xh, axis=-1, keepdims=True))
        dx_ref[...] = dx
        if want_bf16:
            dxb_ref[...] = dx.astype(BF16)
        part = jnp.sum(dh_v * xh, axis=0, keepdims=True)

        @pl.when(i == 0)
        def _():
            dw_ref[...] = part

        @pl.when(i > 0)
        def _():
            dw_ref[...] += part

    tile = pl.BlockSpec((tm, D), lambda i: (i, 0))
    row = pl.BlockSpec((1, D), lambda i: (0, 0))
    out_specs = [tile] + ([tile] if want_bf16 else []) + [row]
    out_shape = ([jax.ShapeDtypeStruct((T, D), F32)] + ([jax.ShapeDtypeStruct((T, D), BF16)] if want_bf16 else [])
                 + [jax.ShapeDtypeStruct((1, D), F32)])
    return _pcall(body, name=name, grid=(T // tm,), in_specs=[tile, tile, tile, row], out_specs=out_specs,
                  out_shape=out_shape, args=[dres, dh, x, w], sem=("arbitrary",), jobs=jobs)


def _loss_head(x2, target, wf, tm, name):
    T, D = x2.shape

    def body(x_ref, t_ref, w_ref, dx_ref, loss_ref, dw_ref):
        i = pl.program_id(0)
        xv = x_ref[...]
        r = lax.rsqrt(jnp.mean(xv * xv, axis=-1, keepdims=True) + RMS_EPS)
        xh = xv * r
        wv = w_ref[...]
        e = xh * wv - t_ref[...]
        lpart = 0.5 * jnp.sum(jnp.sum(e * e, axis=-1, keepdims=True), axis=0, keepdims=True) * (1.0 / D)
        dy = e * (1.0 / D)
        dxh = dy * wv
        dx_ref[...] = r * (dxh - xh * jnp.mean(dxh * xh, axis=-1, keepdims=True))
        wpart = jnp.sum(dy * xh, axis=0, keepdims=True)
        lfull = jnp.broadcast_to(lpart, (8, LANES))

        @pl.when(i == 0)
        def _():
            loss_ref[...] = lfull
            dw_ref[...] = wpart

        @pl.when(i > 0)
        def _():
            loss_ref[...] += lfull
            dw_ref[...] += wpart

    tile = pl.BlockSpec((tm, D), lambda i: (i, 0))
    row = pl.BlockSpec((1, D), lambda i: (0, 0))
    return _pcall(
        body, name=name, grid=(T // tm,), in_specs=[tile, tile, row],
        out_specs=[tile, pl.BlockSpec((8, LANES), lambda i: (0, 0)), row],
        out_shape=[jax.ShapeDtypeStruct((T, D), F32), jax.ShapeDtypeStruct((8, LANES), F32),
                   jax.ShapeDtypeStruct((1, D), F32)],
        args=[x2, target, wf], sem=("arbitrary",))


def _rope_tables(pos_col, inv2, tm, name):
    T = pos_col.shape[0]

    def body(p_ref, f_ref, c_ref, s_ref):
        ang = p_ref[...] * f_ref[...]
        lane = lax.broadcasted_iota(jnp.int32, ang.shape, 1)
        c_ref[...] = jnp.cos(ang)
        s_ref[...] = jnp.where(lane < RET_DK // 2, -1.0, 1.0) * jnp.sin(ang)

    tile = pl.BlockSpec((tm, RET_DK), lambda i: (i, 0))
    return _pcall(
        body, name=name, grid=(T // tm,),
        in_specs=[pl.BlockSpec((tm, 1), lambda i: (i, 0)), pl.BlockSpec((1, RET_DK), lambda i: (0, 0))],
        out_specs=[tile, tile], out_shape=[jax.ShapeDtypeStruct((T, RET_DK), F32)] * 2,
        args=[pos_col, inv2], sem=("parallel",))


def _mix_fwd(a_in, b_in, proj, x, w_ro, w_lo, w_out, mb, w2, tm, name):
    T, D = x.shape

    def body(a_ref, b_ref, gr_ref, gl_ref, x_ref, wro_ref, wlo_ref, wout_ref, mb_ref, w2_ref,
             x1_ref, mix_ref, h2_ref):
        ya = _dot(a_ref[...], wro_ref[...], NN)
        yb = _dot(b_ref[...], wlo_ref[...], NN)
        sa = _sigmoid(gr_ref[...] + mb_ref[0:1, :])
        sb = _sigmoid(gl_ref[...] + mb_ref[1:2, :])
        mix = (sa * ya + sb * yb).astype(BF16)
        mix_ref[...] = mix
        x1 = x_ref[...] + _dot(mix, wout_ref[...], NN)
        x1_ref[...] = x1
        r = lax.rsqrt(jnp.mean(x1 * x1, axis=-1, keepdims=True) + RMS_EPS)
        h2_ref[...] = (x1 * r * w2_ref[...]).astype(BF16)

    tile = pl.BlockSpec((tm, D), lambda i: (i, 0))
    wspec = pl.BlockSpec((D, D), lambda i: (0, 0))
    return _pcall(
        body, name=name, grid=(T // tm,),
        in_specs=[tile, tile,
                  pl.BlockSpec((tm, D), lambda i: (i, COL_GR)), pl.BlockSpec((tm, D), lambda i: (i, COL_GL)),
                  tile, wspec, wspec, wspec,
                  pl.BlockSpec((2, D), lambda i: (0, 0)), pl.BlockSpec((1, D), lambda i: (0, 0))],
        out_specs=[tile, tile, tile],
        out_shape=[jax.ShapeDtypeStruct((T, D), F32), jax.ShapeDtypeStruct((T, D), BF16),
                   jax.ShapeDtypeStruct((T, D), BF16)],
        args=[a_in, b_in, proj, proj, x, w_ro, w_lo, w_out, mb, w2], sem=("parallel",))


def _mix_bwd(dx1b, a_in, b_in, proj, w_ro, w_lo, w_out, mb, tm, name, jobs=()):
    T, D = a_in.shape

    def body(dx_ref, a_ref, b_ref, gr_ref, gl_ref, wro_ref, wlo_ref, wout_ref, mb_ref,
             da_ref, db_ref, dya_ref, dyb_ref, dp_ref, dmb_ref, dgl_s):
        i, j = pl.program_id(0), pl.program_id(1)

        @pl.when(j == 0)
        def _():
            dmix = _dot(dx_ref[...], wout_ref[...], NT)
            ya = _dot(a_ref[...], wro_ref[...], NN)
            yb = _dot(b_ref[...], wlo_ref[...], NN)
            sa = _sigmoid(gr_ref[...] + mb_ref[0:1, :])
            sb = _sigmoid(gl_ref[...] + mb_ref[1:2, :])
            dya = (dmix * sa).astype(BF16)
            dyb = (dmix * sb).astype(BF16)
            dgr = dmix * ya * sa * (1.0 - sa)
            dgl = dmix * yb * sb * (1.0 - sb)
            dya_ref[...] = dya
            dyb_ref[...] = dyb
            dp_ref[...] = dgr.astype(BF16)
            dgl_s[...] = dgl.astype(BF16)
            da_ref[...] = _dot(dya, wro_ref[...], NT)
            db_ref[...] = _dot(dyb, wlo_ref[...], NT)

            @pl.when(i == 0)
            def _():
                dmb_ref[...] = jnp.zeros_like(dmb_ref)

            dmb_ref[0:1, :] += jnp.sum(dgr, axis=0, keepdims=True)
            dmb_ref[1:2, :] += jnp.sum(dgl, axis=0, keepdims=True)

        @pl.when(j == 1)
        def _():
            dp_ref[...] = dgl_s[...]

    tile = pl.BlockSpec((tm, D), lambda i, j: (i, 0))
    wspec = pl.BlockSpec((D, D), lambda i, j: (0, 0))
    two = pl.BlockSpec((2, D), lambda i, j: (0, 0))
    return _pcall(
        body, name=name, grid=(T // tm, 2),
        in_specs=[tile, tile, tile,
                  pl.BlockSpec((tm, D), lambda i, j: (i, COL_GR)), pl.BlockSpec((tm, D), lambda i, j: (i, COL_GL)),
                  wspec, wspec, wspec, two],
        out_specs=[tile] * 4 + [pl.BlockSpec((tm, D), lambda i, j: (i, COL_GR + j)), two],
        out_shape=[jax.ShapeDtypeStruct((T, D), F32)] * 2 + [jax.ShapeDtypeStruct((T, D), BF16)] * 2
                  + [jax.ShapeDtypeStruct((T, D_IN), BF16), jax.ShapeDtypeStruct((2, D), F32)],
        args=[dx1b, a_in, b_in, proj, proj, w_ro, w_lo, w_out, mb],
        scratch=[pltpu.VMEM((tm, D), BF16)], sem=("arbitrary", "arbitrary"), jobs=jobs)


def _ret_decay_consts(lg):
    L = RET_BLOCK
    n = lax.broadcasted_iota(jnp.int32, (L, L), 0)
    m = lax.broadcasted_iota(jnp.int32, (L, L), 1)
    cn, cm = n // CHUNK, m // CHUNK
    expo = jnp.where(cn == cm, jnp.abs(n - m), n - m).astype(F32)
    wm = jnp.where(cm <= cn, jnp.exp(lg * expo), 0.0)
    idx = lax.broadcasted_iota(jnp.int32, (L, 1), 0).astype(F32)
    qd = jnp.exp(lg * (idx + 1.0))
    kd = jnp.exp(lg * (L - 1.0 - idx))
    bd = jnp.exp(lg * float(L))
    return wm, qd, kd, bd


def _rotate(v, cos2, sin2s):
    return v * cos2 + pltpu.roll(v, RET_DK // 2, 1) * sin2s


def _rotate_t(d, cos2, sin2s):
    return d * cos2 - pltpu.roll(d, RET_DK // 2, 1) * sin2s


def _retention_fwd(proj, cos2, sin2s, lgam, gn_w, B, S, name, jobs=()):
    T = B * S
    nb = S // RET_BLOCK
    scale = RET_DK ** -0.5

    def body(q_ref, k_ref, v_ref, g_ref, c_ref, s_ref, lg_ref, gw_ref, o_ref, a_ref, qr, kr, st):
        wm, qd, kd, bd = _ret_decay_consts(lg_ref[0:1, 0:1])
        cos2, sin2s = c_ref[...], s_ref[...]
        qr[...] = _rotate(q_ref[...], cos2, sin2s)
        kr[...] = _rotate(k_ref[...], cos2, sin2s) * scale
        st[...] = jnp.zeros_like(st)
        gw = gw_ref[...]
        for j in range(nb):
            rows = pl.ds(j * RET_BLOCK, RET_BLOCK)
            qb = qr[rows, :]
            kb = kr[rows, :]
            vb = v_ref[rows, :].astype(BF16)
            sc = _dot(qb.astype(BF16), kb.astype(BF16), NT) * wm
            o = _dot(sc.astype(BF16), vb, NN) + _dot((qb * qd).astype(BF16), st[...].astype(BF16), NN)
            st[...] = st[...] * bd + _dot((kb * kd).astype(BF16), vb, TN)
            o_ref[rows, :] = o
            mu = jnp.mean(o, axis=-1, keepdims=True)
            oc = o - mu
            var = jnp.mean(oc * oc, axis=-1, keepdims=True)
            y = oc * lax.rsqrt(var + GN_EPS) * gw
            g = g_ref[rows, :]
            a_ref[rows, :] = (y * (g * _sigmoid(g))).astype(BF16)

    blk = lambda w, off: pl.BlockSpec((S, w), lambda b, h: (b, off + h))
    return _pcall(
        body, name=name, grid=(B, RET_HEADS),
        in_specs=[blk(RET_DK, COL_Q), blk(RET_DK, COL_K), blk(RET_DV, COL_V), blk(RET_DV, COL_G),
                  pl.BlockSpec((S, RET_DK), lambda b, h: (b, 0)), pl.BlockSpec((S, RET_DK), lambda b, h: (b, 0)),
                  pl.BlockSpec((None, 8, LANES), lambda b, h: (h, 0, 0)),
                  pl.BlockSpec((1, RET_DV), lambda b, h: (0, h))],
        out_specs=[blk(RET_DV, 0), blk(RET_DV, 0)],
        out_shape=[jax.ShapeDtypeStruct((T, RET_HEADS * RET_DV), F32),
                   jax.ShapeDtypeStruct((T, RET_HEADS * RET_DV), BF16)],
        args=[proj, proj, proj, proj, cos2, sin2s, lgam, gn_w],
        scratch=[pltpu.VMEM((S, RET_DK), F32), pltpu.VMEM((S, RET_DK), F32), pltpu.VMEM((RET_DK, RET_DV), F32)],
        sem=("parallel", "parallel"), jobs=jobs)


def _retention_bwd(da_in, o, proj, dproj, cos2, sin2s, lgam, gn_w, B, S, name, jobs=()):
    T = B * S
    nb = S // RET_BLOCK
    scale = RET_DK ** -0.5
    half = RET_DV // 2

    def body(da_ref, o_ref, q_ref, k_ref, v_ref, g_ref, c_ref, s_ref, lg_ref, gw_ref, _, dp_ref, dgw_ref,
             qr, kr, do_s, sts, rst, dq_s, dk_s, dv_s, dg_s):
        b, piece = pl.program_id(1), pl.program_id(2)

        @pl.when(piece == 0)
        def _():
            wm, qd, kd, bd = _ret_decay_consts(lg_ref[0:1, 0:1])
            cos2, sin2s = c_ref[...], s_ref[...]
            qr[...] = _rotate(q_ref[...], cos2, sin2s)
            kr[...] = _rotate(k_ref[...], cos2, sin2s) * scale
            gw = gw_ref[...]
            st = jnp.zeros((RET_DK, RET_DV), F32)
            dgw = jnp.zeros((1, RET_DV), F32)
            for j in range(nb):
                rows = pl.ds(j * RET_BLOCK, RET_BLOCK)
                ov = o_ref[rows, :]
                mu = jnp.mean(ov, axis=-1, keepdims=True)
                oc = ov - mu
                rstd = lax.rsqrt(jnp.mean(oc * oc, axis=-1, keepdims=True) + GN_EPS)
                y = oc * rstd
                g = g_ref[rows, :]
                sg = _sigmoid(g)
                da = da_ref[rows, :]
                dg_s[rows, :] = (da * (y * gw) * (sg * (1.0 + g * (1.0 - sg)))).astype(BF16)
                dyw = da * (g * sg)
                dgw = dgw + jnp.sum(dyw * y, axis=0, keepdims=True)
                dy = dyw * gw
                do_s[rows, :] = rstd * (dy - jnp.mean(dy, axis=-1, keepdims=True)
                                        - y * jnp.mean(dy * y, axis=-1, keepdims=True))
                sts[j] = st
                st = st * bd + _dot((kr[rows, :] * kd).astype(BF16), v_ref[rows, :].astype(BF16), TN)

            @pl.when(b == 0)
            def _():
                dgw_ref[...] = dgw

            @pl.when(b > 0)
            def _():
                dgw_ref[...] += dgw

            rst[...] = jnp.zeros_like(rst)
            for j in reversed(range(nb)):
                rows = pl.ds(j * RET_BLOCK, RET_BLOCK)
                qb = qr[rows, :]
                kb = kr[rows, :]
                qbb, kbb = qb.astype(BF16), kb.astype(BF16)
                vb = v_ref[rows, :].astype(BF16)
                dob = do_s[rows, :]
                dobb = dob.astype(BF16)
                a_m = (_dot(qbb, kbb, NT) * wm).astype(BF16)
                b_m = (_dot(dobb, vb, NT) * wm).astype(BF16)
                rb = rst[...].astype(BF16)
                dq = _dot(b_m, kbb, NN) + _dot((dob * qd).astype(BF16), sts[j].astype(BF16), NT)
                dk = _dot(b_m, qbb, TN) + kd * _dot(vb, rb, NT)
                dv = _dot(a_m, dobb, TN) + kd * _dot(kbb, rb, NN)
                rst[...] = rst[...] * bd + _dot((qb * qd).astype(BF16), dobb, TN)
                cb, sb = c_ref[rows, :], s_ref[rows, :]
                dq_s[rows, :] = _rotate_t(dq, cb, sb).astype(BF16)
                dk_s[rows, :] = _rotate_t(dk * scale, cb, sb).astype(BF16)
                dv_s[rows, :] = dv.astype(BF16)

        pieces = [lambda: dq_s[...], lambda: dk_s[...], lambda: dv_s[:, 0:half], lambda: dv_s[:, half:RET_DV],
                  lambda: dg_s[:, 0:half], lambda: dg_s[:, half:RET_DV]]
        for t, get in enumerate(pieces):
            @pl.when(piece == t)
            def _(get=get):
                dp_ref[...] = get()

    def dp_col(h, b, p):
        col = jnp.where(p == 0, h, jnp.where(p == 1, 4 + h, jnp.where(p < 4, 8 + 2 * h + (p - 2),
                                                                        16 + 2 * h + (p - 4))))
        return (b, col)

    blk = lambda w, off: pl.BlockSpec((S, w), lambda h, b, p: (b, off + h))
    return _pcall(
        body, name=name, grid=(RET_HEADS, B, 6),
        in_specs=[blk(RET_DV, 0), blk(RET_DV, 0),
                  blk(RET_DK, COL_Q), blk(RET_DK, COL_K), blk(RET_DV, COL_V), blk(RET_DV, COL_G),
                  pl.BlockSpec((S, RET_DK), lambda h, b, p: (b, 0)), pl.BlockSpec((S, RET_DK), lambda h, b, p: (b, 0)),
                  pl.BlockSpec((None, 8, LANES), lambda h, b, p: (h, 0, 0)),
                  pl.BlockSpec((1, RET_DV), lambda h, b, p: (0, h)), HBM_SPEC],
        out_specs=[pl.BlockSpec((S, half), dp_col), pl.BlockSpec((1, RET_DV), lambda h, b, p: (0, h))],
        out_shape=[jax.ShapeDtypeStruct(dproj.shape, dproj.dtype),
                   jax.ShapeDtypeStruct((1, RET_HEADS * RET_DV), F32)],
        args=[da_in, o, proj, proj, proj, proj, cos2, sin2s, lgam, gn_w, dproj],
        scratch=[pltpu.VMEM((S, RET_DK), F32), pltpu.VMEM((S, RET_DK), F32),
                 pltpu.VMEM((S, RET_DV), F32), pltpu.VMEM((nb, RET_DK, RET_DV), F32),
                 pltpu.VMEM((RET_DK, RET_DV), F32),
                 pltpu.VMEM((S, RET_DK), BF16), pltpu.VMEM((S, RET_DK), BF16),
                 pltpu.VMEM((S, RET_DV), BF16), pltpu.VMEM((S, RET_DV), BF16)],
        sem=("parallel", "arbitrary", "arbitrary"), jobs=jobs, alias_in_out={10: 0})


def _lru_gates(x, cw, cb, wr, wi, br, bi, lam):
    xc = cb + cw[LRU_CONV - 1:LRU_CONV, :] * x
    for j in range(LRU_CONV - 1):
        xc = xc + cw[j:j + 1, :] * _shift_down(x, LRU_CONV - 1 - j, 0.0)
    xcb = xc.astype(BF16)
    r = _sigmoid(_dot(xcb, wr, NN) + br)
    ig = _sigmoid(_dot(xcb, wi, NN) + bi)
    z = -lam
    sp = jnp.maximum(z, 0.0) + jnp.log1p(jnp.exp(-jnp.abs(z)))
    log_a = (-LRU_C) * r * sp
    a = jnp.exp(log_a)
    z2 = 2.0 * log_a
    taylor = -z2 * (1.0 + z2 * (0.5 + z2 * (1.0 / 6.0 + z2 * (1.0 / 24.0 + z2 * (1.0 / 120.0)))))
    om = jnp.where(z2 > -0.05, taylor, 1.0 - jnp.exp(z2))
    sq = jnp.sqrt(om)
    return xc, xcb, r, ig, sp, a, sq


def _lru_fwd(proj, cw, cb, wr, wi, br, bi, lam, B, S, name):
    T = B * S
    W = LRU_BLOCKS * LRU_BLOCK

    def body(x_ref, y_ref, cw_ref, cb_ref, wr_ref, wi_ref, br_ref, bi_ref, lam_ref, h_ref, bin_ref):
        xc, _, _, ig, _, a, sq = _lru_gates(x_ref[...], cw_ref[...], cb_ref[...], wr_ref[...], wi_ref[...],
                                           br_ref[...], bi_ref[...], lam_ref[...])
        bv = sq * ig * xc
        s = 1
        while s < S:
            bv = a * _shift_down(bv, s, 0.0) + bv
            if 2 * s < S:
                a = a * _shift_down(a, s, 1.0)
            s *= 2
        h_ref[...] = bv
        bin_ref[...] = (bv * _gelu(y_ref[...])).astype(BF16)

    blk = lambda off: pl.BlockSpec((S, LRU_BLOCK), lambda b, n: (b, off + n))
    vec = lambda rows: pl.BlockSpec((rows, LRU_BLOCK), lambda b, n: (0, n))
    wspec = pl.BlockSpec((None, LRU_BLOCK, LRU_BLOCK), lambda b, n: (n, 0, 0))
    return _pcall(
        body, name=name, grid=(B, LRU_BLOCKS),
        in_specs=[blk(COL_XL), blk(COL_YL), vec(LRU_CONV), vec(1), wspec, wspec, vec(1), vec(1), vec(1)],
        out_specs=[blk(0), blk(0)],
        out_shape=[jax.ShapeDtypeStruct((T, W), F32), jax.ShapeDtypeStruct((T, W), BF16)],
        args=[proj, proj, cw, cb, wr, wi, br, bi, lam], sem=("parallel", "parallel"))


def _lru_bwd(db_in, h, proj, dproj, cw, cb, wr, wi, br, bi, lam, B, S, name, jobs=()):
    T = B * S
    W = LRU_BLOCKS * LRU_BLOCK

    def body(dbin_ref, h_ref, x_ref, y_ref, cw_ref, cb_ref, wr_ref, wi_ref, br_ref, bi_ref, lam_ref, _,
             dp_ref, dcw_ref, dcb_ref, dwr_ref, dwi_ref, dbr_ref, dbi_ref, dlam_ref, dy_s):
        b, piece = pl.program_id(1), pl.program_id(2)

        @pl.when(piece == 0)
        def _():
            x = x_ref[...]
            cw = cw_ref[...]
            wr, wi = wr_ref[...], wi_ref[...]
            lam = lam_ref[...]
            xc, xcb, r, ig, sp, a, sq = _lru_gates(x, cw, cb_ref[...], wr, wi, br_ref[...], bi_ref[...], lam)
            hv = h_ref[...]
            gel, dgel = _gelu_and_grad(y_ref[...])
            dbin = dbin_ref[...]
            dy_s[...] = (dbin * hv * dgel).astype(BF16)
            dh = dbin * gel
            an = _shift_up(a, 1, 0.0)
            s = 1
            while s < S:
                dh = dh + an * _shift_up(dh, s, 0.0)
                if 2 * s < S:
                    an = an * _shift_up(an, s, 1.0)
                s *= 2
            hprev = _shift_down(hv, 1, 0.0)
            d_ig = dh * sq * xc
            d_xc = dh * sq * ig
            a2 = a * a
            d_loga = dh * hprev * a - dh * ig * xc * a2 / sq
            d_r = d_loga * ((-LRU_C) * sp)
            d_sp = jnp.sum(d_loga * ((-LRU_C) * r), axis=0, keepdims=True)
            dlam = -d_sp * _sigmoid(-lam)
            d_pr = d_r * r * (1.0 - r)
            d_pi = d_ig * ig * (1.0 - ig)
            d_prb, d_pib = d_pr.astype(BF16), d_pi.astype(BF16)
            d_xc = d_xc + _dot(d_prb, wr, NT) + _dot(d_pib, wi, NT)

            @pl.when(b == 0)
            def _():
                for ref in (dcw_ref, dcb_ref, dwr_ref, dwi_ref, dbr_ref, dbi_ref, dlam_ref):
                    ref[...] = jnp.zeros_like(ref)

            dx = cw[LRU_CONV - 1:LRU_CONV, :] * d_xc
            for j in range(LRU_CONV - 1):
                sft = LRU_CONV - 1 - j
                dx = dx + cw[j:j + 1, :] * _shift_up(d_xc, sft, 0.0)
                dcw_ref[j:j + 1, :] += jnp.sum(d_xc * _shift_down(x, sft, 0.0), axis=0, keepdims=True)
            dcw_ref[LRU_CONV - 1:LRU_CONV, :] += jnp.sum(d_xc * x, axis=0, keepdims=True)
            dp_ref[...] = dx.astype(BF16)
            dcb_ref[...] += jnp.sum(d_xc, axis=0, keepdims=True)
            dwr_ref[...] += _dot(xcb, d_prb, TN)
            dwi_ref[...] += _dot(xcb, d_pib, TN)
            dbr_ref[...] += jnp.sum(d_pr, axis=0, keepdims=True)
            dbi_ref[...] += jnp.sum(d_pi, axis=0, keepdims=True)
            dlam_ref[...] += dlam

        @pl.when(piece == 1)
        def _():
            dp_ref[...] = dy_s[...]

    blk = lambda off: pl.BlockSpec((S, LRU_BLOCK), lambda n, b, p: (b, off + n))
    vec = lambda rows: pl.BlockSpec((rows, LRU_BLOCK), lambda n, b, p: (0, n))
    wspec = pl.BlockSpec((None, LRU_BLOCK, LRU_BLOCK), lambda n, b, p: (n, 0, 0))
    vshape = lambda rows: jax.ShapeDtypeStruct((rows, W), F32)
    wshape = jax.ShapeDtypeStruct((LRU_BLOCKS, LRU_BLOCK, LRU_BLOCK), F32)
    return _pcall(
        body, name=name, grid=(LRU_BLOCKS, B, 2),
        in_specs=[blk(0), blk(0), blk(COL_XL), blk(COL_YL), vec(LRU_CONV), vec(1), wspec, wspec, vec(1), vec(1),
                  vec(1), HBM_SPEC],
        out_specs=[pl.BlockSpec((S, LRU_BLOCK), lambda n, b, p: (b, COL_XL + n + (COL_YL - COL_XL) * p)),
                   vec(LRU_CONV), vec(1), wspec, wspec, vec(1), vec(1), vec(1)],
        out_shape=[jax.ShapeDtypeStruct(dproj.shape, dproj.dtype),
                   vshape(LRU_CONV), vshape(1), wshape, wshape, vshape(1), vshape(1), vshape(1)],
        args=[db_in, h, proj, proj, cw, cb, wr, wi, br, bi, lam, dproj],
        scratch=[pltpu.VMEM((S, LRU_BLOCK), BF16)],
        sem=("parallel", "arbitrary", "arbitrary"), jobs=jobs, alias_in_out={11: 0})


FFN_CT = 256


def _ffn_conv(gate, cw, cb):
    gc = cb + cw[FFN_CONV - 1:FFN_CONV, :] * gate
    for j in range(FFN_CONV - 1):
        gc = gc + cw[j:j + 1, :] * _shift_down(gate, FFN_CONV - 1 - j, 0.0)
    return gc


def _ffn_act_fwd(up, cw, cb, B, S, name):
    T = B * S
    nct = D_FF // FFN_CT

    def body(g_ref, v_ref, cw_ref, cb_ref, f_ref):
        gc = _ffn_conv(g_ref[...], cw_ref[...], cb_ref[...])
        f_ref[...] = (_gelu(gc) * v_ref[...]).astype(BF16)

    return _pcall(
        body, name=name, grid=(B, nct),
        in_specs=[pl.BlockSpec((S, FFN_CT), lambda b, c: (b, c)), pl.BlockSpec((S, FFN_CT), lambda b, c: (b, nct + c)),
                  pl.BlockSpec((FFN_CONV, FFN_CT), lambda b, c: (0, c)), pl.BlockSpec((1, FFN_CT), lambda b, c: (0, c))],
        out_specs=[pl.BlockSpec((S, FFN_CT), lambda b, c: (b, c))],
        out_shape=[jax.ShapeDtypeStruct((T, D_FF), BF16)], args=[up, up, cw, cb], sem=("parallel", "parallel"))[0]


def _ffn_act_bwd(df, up, cw, cb, B, S, name, jobs=()):
    T = B * S
    nct = D_FF // FFN_CT

    def body(df_ref, g_ref, v_ref, cw_ref, cb_ref, du_ref, dcw_ref, dcb_ref, dv_s):
        b, piece = pl.program_id(1), pl.program_id(2)

        @pl.when(piece == 0)
        def _():
            gate = g_ref[...]
            cw = cw_ref[...]
            gc = _ffn_conv(gate, cw, cb_ref[...])
            gel, dgel = _gelu_and_grad(gc)
            dfv = df_ref[...]
            dv_s[...] = (dfv * gel).astype(BF16)
            dgc = dfv * v_ref[...] * dgel

            @pl.when(b == 0)
            def _():
                dcw_ref[...] = jnp.zeros_like(dcw_ref)
                dcb_ref[...] = jnp.zeros_like(dcb_ref)

            dgate = cw[FFN_CONV - 1:FFN_CONV, :] * dgc
            for j in range(FFN_CONV - 1):
                sft = FFN_CONV - 1 - j
                dgate = dgate + cw[j:j + 1, :] * _shift_up(dgc, sft, 0.0)
                dcw_ref[j:j + 1, :] += jnp.sum(dgc * _shift_down(gate, sft, 0.0), axis=0, keepdims=True)
            dcw_ref[FFN_CONV - 1:FFN_CONV, :] += jnp.sum(dgc * gate, axis=0, keepdims=True)
            du_ref[...] = dgate.astype(BF16)
            dcb_ref[...] += jnp.sum(dgc, axis=0, keepdims=True)

        @pl.when(piece == 1)
        def _():
            du_ref[...] = dv_s[...]

    blk = pl.BlockSpec((S, FFN_CT), lambda c, b, p: (b, c))
    return _pcall(
        body, name=name, grid=(nct, B, 2),
        in_specs=[blk, blk, pl.BlockSpec((S, FFN_CT), lambda c, b, p: (b, nct + c)),
                  pl.BlockSpec((FFN_CONV, FFN_CT), lambda c, b, p: (0, c)),
                  pl.BlockSpec((1, FFN_CT), lambda c, b, p: (0, c))],
        out_specs=[pl.BlockSpec((S, FFN_CT), lambda c, b, p: (b, c + nct * p)),
                   pl.BlockSpec((FFN_CONV, FFN_CT), lambda c, b, p: (0, c)),
                   pl.BlockSpec((1, FFN_CT), lambda c, b, p: (0, c))],
        out_shape=[jax.ShapeDtypeStruct((T, 2 * D_FF), BF16),
                   jax.ShapeDtypeStruct((FFN_CONV, D_FF), F32), jax.ShapeDtypeStruct((1, D_FF), F32)],
        args=[df, up, up, cw, cb], scratch=[pltpu.VMEM((S, FFN_CT), BF16)],
        sem=("parallel", "arbitrary", "arbitrary"), jobs=jobs)


def _rs_add(g, recv, mode, core, name):
    shard = tuple(recv.shape[1:])
    if mode == "mid":
        a, e, c2 = shard
        g_in = g.reshape(a, N_DEV, e, c2)
        grid = (4, 1)
        g_spec = pl.BlockSpec((a, None, e, c2), lambda k, i, c_ref: (0, 2 * k + c_ref[0], 0, 0))
        r_spec = pl.BlockSpec((None, a, e, c2), lambda k, i, c_ref: (k, 0, 0, 0))
    else:
        R, C = shard
        tr = _row_tile(R, 512)
        grid = (4, R // tr)
        if mode == "rows":
            g_in = g.reshape(N_DEV, R, C)
            g_spec = pl.BlockSpec((None, tr, C), lambda k, i, c_ref: (2 * k + c_ref[0], i, 0))
        else:
            g_in = g
            g_spec = pl.BlockSpec((tr, C), lambda k, i, c_ref: (i, 2 * k + c_ref[0]))
        r_spec = pl.BlockSpec((None, tr, C), lambda k, i, c_ref: (k, i, 0))

    def body(c_ref, g_ref, r_ref, o_ref):
        o_ref[...] = g_ref[...] + r_ref[...]

    return pl.pallas_call(
        body, name=name,
        grid_spec=pltpu.PrefetchScalarGridSpec(num_scalar_prefetch=1, grid=grid, in_specs=[g_spec, r_spec],
                                               out_specs=r_spec),
        out_shape=jax.ShapeDtypeStruct(recv.shape, recv.dtype),
        compiler_params=pltpu.CompilerParams(dimension_semantics=("parallel", "parallel"),
                                             vmem_limit_bytes=VMEM_LIMIT),
    )(core, g_in, recv)


def _adam_update(gv, w, m, v):
    nm = ADAM_B1 * m + (1.0 - ADAM_B1) * gv
    nv = ADAM_B2 * v + (1.0 - ADAM_B2) * (gv * gv)
    m_hat = nm / (1.0 - ADAM_B1 ** ADAM_STEP)
    v_hat = nv / (1.0 - ADAM_B2 ** ADAM_STEP)
    delta = -ADAM_LR * (m_hat / (jnp.sqrt(v_hat) + ADAM_EPS) + ADAM_WD * w)
    return delta, nm, nv


def _adamw_shard(partial, recv, w, m, v, chip, name):
    shape = tuple(w.shape)
    tr = _row_tile(shape[0], 256)
    rest = shape[1:]
    zeros = (0,) * len(rest)
    tile = pl.BlockSpec((tr,) + rest, lambda i, s: (i,) + zeros)

    def body(_, p_ref, r_ref, w_ref, m_ref, v_ref, g_ref, d_ref, nm_ref, nv_ref):
        gv = p_ref[...] + r_ref[0] + r_ref[1] + r_ref[2]
        g_ref[...] = gv
        d_ref[...], nm_ref[...], nv_ref[...] = _adam_update(gv, w_ref[...], m_ref[...], v_ref[...])

    grid_spec = pltpu.PrefetchScalarGridSpec(
        num_scalar_prefetch=1, grid=(shape[0] // tr,),
        in_specs=[pl.BlockSpec((None, tr) + rest, lambda i, s: (s[0], i) + zeros),
                  pl.BlockSpec((3, tr) + rest, lambda i, s: (0, i) + zeros), tile, tile, tile],
        out_specs=[tile] * 4)
    return pl.pallas_call(
        body, name=name, grid_spec=grid_spec, out_shape=[jax.ShapeDtypeStruct(shape, F32)] * 4,
        compiler_params=pltpu.CompilerParams(dimension_semantics=("parallel",), vmem_limit_bytes=VMEM_LIMIT),
    )(chip, partial, recv, w, m, v)


def _all_gather_multi(shards, modes, name):
    n = len(shards)
    extents = [_extent(s.shape, m) for s, m in zip(shards, modes)]

    def body(*refs):
        x_refs, out_refs = refs[:n], refs[n:2 * n]
        send_sems, recv_sems, local_sems = refs[2 * n:]
        x, y, c = _mesh_pos()
        me, sibling = (x, y, c), (x, y, 1 - c)
        chips = _other_chips(x, y)

        def slot(i, px, py, pc):
            return _window(out_refs[i], modes[i], extents[i], 4 * px + 2 * py + pc)

        def copy(i, k, block, to, src=None):
            return _remote(slot(i, *block) if src is None else src, slot(i, *block),
                           send_sems.at[i, k], recv_sems.at[i, k], to)

        mine = [pltpu.make_async_copy(x_refs[i], slot(i, *me), local_sems.at[i]) for i in range(n)]
        sends = []
        for i in range(n):
            mine[i].start()
            first = [copy(i, 0, me, sibling, src=x_refs[i])]
            first += [copy(i, 1 + j, me, (*chip, c), src=x_refs[i]) for j, chip in enumerate(chips)]
            for cp in first:
                cp.start()
            sends += first
        for i in range(n):
            for j, chip in enumerate(chips):
                copy(i, 1 + j, (*chip, c), me).wait_recv()
                fwd = copy(i, 4 + j, (*chip, c), sibling)
                fwd.start()
                sends.append(fwd)
        for i in range(n):
            copy(i, 0, sibling, me).wait_recv()
            for j, chip in enumerate(chips):
                copy(i, 4 + j, (*chip, 1 - c), me).wait_recv()
        for cp in sends:
            cp.wait_send()
        for cp in mine:
            cp.wait()

    return pl.pallas_call(
        body, name=name,
        in_specs=[HBM_SPEC] * n, out_specs=[HBM_SPEC] * n,
        out_shape=[jax.ShapeDtypeStruct(_full_shape(s.shape, m), s.dtype) for s, m in zip(shards, modes)],
        scratch_shapes=[pltpu.SemaphoreType.DMA((n, 7)), pltpu.SemaphoreType.DMA((n, 7)),
                        pltpu.SemaphoreType.DMA((n,))],
    )(*shards)


SMALL_LANES = 1024


def _small_rows(shape):
    r, w = shape
    return r * max(1, w // SMALL_LANES)


def _small_allreduce(parts, name):
    n = len(parts)
    shapes = [tuple(p.shape) for p in parts]
    offs, total = [], 0
    for s in shapes:
        offs.append(total)
        total += _small_rows(s)
    rows = -(-total // 8) * 8

    def body(*refs):
        p_refs, o_refs = refs[:n], refs[n:2 * n]
        buf, tot, send_sems, recv_sems = refs[2 * n:]
        x, y, c = _mesh_pos()
        me, sibling = (x, y, c), (x, y, 1 - c)
        chips = _other_chips(x, y)

        def slot(px, py, pc):
            return buf.at[4 * px + 2 * py + pc]

        def copy(k, block, to):
            return _remote(slot(*block), slot(*block), send_sems.at[k], recv_sems.at[k], to)

        tot[...] = jnp.zeros_like(tot)
        for p_ref, (r, w), off in zip(p_refs, shapes, offs):
            wl = min(w, SMALL_LANES)
            for part in range(max(1, w // SMALL_LANES)):
                tot[pl.ds(off + part * r, r), pl.ds(0, wl)] = p_ref[:, pl.ds(part * SMALL_LANES, wl)]
        buf[4 * x + 2 * y + c] = tot[...]
        first = [copy(0, me, sibling)] + [copy(1 + j, me, (*chip, c)) for j, chip in enumerate(chips)]
        for cp in first:
            cp.start()
        passed = [copy(4 + j, (*chip, c), sibling) for j, chip in enumerate(chips)]
        for j, chip in enumerate(chips):
            copy(1 + j, (*chip, c), me).wait_recv()
            passed[j].start()
        copy(0, sibling, me).wait_recv()
        for j, chip in enumerate(chips):
            copy(4 + j, (*chip, 1 - c), me).wait_recv()
        for cp in first + passed:
            cp.wait_send()
        acc = buf[0]
        for d in range(1, N_DEV):
            acc = acc + buf[d]
        tot[...] = acc
        for o_ref, (r, w), off in zip(o_refs, shapes, offs):
            wl = min(w, SMALL_LANES)
            for part in range(max(1, w // SMALL_LANES)):
                o_ref[:, pl.ds(part * SMALL_LANES, wl)] = tot[pl.ds(off + part * r, r), pl.ds(0, wl)]

    vm = pl.BlockSpec(memory_space=pltpu.VMEM)
    return pl.pallas_call(
        body, name=name,
        in_specs=[vm] * n, out_specs=[vm] * n,
        out_shape=[jax.ShapeDtypeStruct(s, F32) for s in shapes],
        scratch_shapes=[pltpu.VMEM((N_DEV, rows, SMALL_LANES), F32), pltpu.VMEM((rows, SMALL_LANES), F32),
                        pltpu.SemaphoreType.DMA((7,)), pltpu.SemaphoreType.DMA((7,))],
    )(*parts)


def _adamw_small(gs, ws, ms, vs, name):
    n = len(gs)

    def body(*refs):
        g_r, w_r, m_r, v_r = refs[:n], refs[n:2 * n], refs[2 * n:3 * n], refs[3 * n:4 * n]
        d_r, nm_r, nv_r = refs[4 * n:5 * n], refs[5 * n:6 * n], refs[6 * n:7 * n]
        for i in range(n):
            d_r[i][...], nm_r[i][...], nv_r[i][...] = _adam_update(g_r[i][...], w_r[i][...], m_r[i][...], v_r[i][...])

    vm = pl.BlockSpec(memory_space=pltpu.VMEM)
    shapes = [jax.ShapeDtypeStruct(w.shape, F32) for w in ws]
    outs = pl.pallas_call(body, name=name, in_specs=[vm] * (4 * n), out_specs=[vm] * (3 * n),
                          out_shape=shapes * 3)(*gs, *ws, *ms, *vs)
    return outs[:n], outs[n:2 * n], outs[2 * n:]


FIRST = [("w_in", (1024, 896), "cols"), ("lru_w_r", (4, 32, 256), "mid"), ("lru_w_i", (4, 32, 256), "mid")]
LATE = [("w_ret_o", (128, 1024), "rows"), ("w_lru_o", (128, 1024), "rows"), ("w_out", (128, 1024), "rows"),
        ("ffn_w_up", (1024, 768), "cols"), ("ffn_w_down", (384, 1024), "rows")]
BIG = FIRST + LATE
SMALL_SHARDED = [("merge_gate_b", (2, 128), "cols"), ("lru_conv_w", (4, 128), "cols"), ("lru_b_r", (4, 32), "stack"),
                 ("lru_b_i", (4, 32), "stack"), ("ffn_conv_w", (3, 384), "cols")]
REPLICATED = [("norm1_w", (1, 1024)), ("ret_gn_w", (1, 1024)), ("lru_conv_b", (1, 1024)), ("lru_lambda", (1, 1024)),
              ("norm2_w", (1, 1024)), ("ffn_conv_b", (1, 3072)), ("norm_f_w", (1, 1024))]
MODE = {n: m for n, _, m in BIG}
SHARD = {n: s for n, s, _ in BIG}


def _local_step(x3, positions, target3, wb, ws, late_shards, core):
    B, S, D = x3.shape
    T = B * S
    x = x3.reshape(T, D)
    target = target3.reshape(T, D)
    tm = min(512, T)
    big = min(1024, T)

    half = RET_DK // 2
    inv_freq = ROPE_BASE ** (-jnp.arange(half, dtype=F32) / half)
    inv2 = jnp.concatenate([inv_freq, inv_freq]).reshape(1, RET_DK)
    log_gamma = jnp.log1p(-jnp.power(2.0, -5.0 - jnp.arange(RET_HEADS, dtype=F32)))
    lgam = jnp.broadcast_to(log_gamma[:, None, None], (RET_HEADS, 8, LANES))
    pos_col = positions.astype(F32).reshape(T, 1)
    cos2, sin2s = _rope_tables(pos_col, inv2, tm, "rope_tables")

    late_names = [n for n, _, _ in LATE]
    late_modes = [m for _, _, m in LATE]
    late_shapes = [s for _, s, _ in LATE]

    h1 = _rmsnorm_fwd(x, ws["norm1_w"], tm, "norm1_fwd")
    proj, *late_part = _matmul(h1, wb["w_in"], "nn", F32, big, 1024, 1024, "proj_fwd",
                               jobs=[_ag_first_job(late_shards, late_modes)])
    o, a_in, *late_full = _retention_fwd(proj, cos2, sin2s, lgam, ws["ret_gn_w"], B, S, "retention_fwd",
                                         jobs=[_ag_second_job(late_part, late_modes, late_shapes)])
    wb = dict(wb, **dict(zip(late_names, late_full)))
    hl, b_in = _lru_fwd(proj, ws["lru_conv_w"], ws["lru_conv_b"], wb["lru_w_r"], wb["lru_w_i"],
                        ws["lru_b_r"], ws["lru_b_i"], ws["lru_lambda"], B, S, "lru_fwd")
    x1, mix, h2 = _mix_fwd(a_in, b_in, proj, x, wb["w_ret_o"], wb["w_lru_o"], wb["w_out"],
                           ws["merge_gate_b"], ws["norm2_w"], tm, "mix_fwd")
    up = _matmul(h2, wb["ffn_w_up"], "nn", F32, big, 1024, 1024, "ffn_up_fwd")[0]
    f = _ffn_act_fwd(up, ws["ffn_conv_w"], ws["ffn_conv_b"], B, S, "ffn_act_fwd")
    x2 = _matmul(f, wb["ffn_w_down"], "nn", F32, big, 1024, 1024, "ffn_down_fwd", add=x1)[0]
    dx2, loss_acc, d_norm_f = _loss_head(x2, target, ws["norm_f_w"], tm, "loss_head")

    g, rs = {}, {}

    def stage1(names, grads):
        return _rs_sibling_job(grads, [MODE[n] for n in names], [SHARD[n] for n in names])

    def add(names, grads, recvs):
        return [_rs_add(gr, r, MODE[n], core, "rs_add_" + n) for n, gr, r in zip(names, grads, recvs)]

    g["norm_f_w"] = d_norm_f
    dx2b = dx2.astype(BF16)
    g_down = _matmul(f, dx2b, "tn", F32, 1024, 1024, big, "ffn_down_bwd_w")[0]
    df, s1_down = _matmul(dx2b, wb["ffn_w_down"], "nt", F32, big, 1024, 1024, "ffn_down_bwd_x",
                          jobs=[stage1(["ffn_w_down"], [g_down])])
    p_down = add(["ffn_w_down"], [g_down], [s1_down])
    dup, g["ffn_conv_w"], g["ffn_conv_b"], s2_down = _ffn_act_bwd(
        df, up, ws["ffn_conv_w"], ws["ffn_conv_b"], B, S, "ffn_act_bwd", jobs=[_rs_chip_job(p_down)])
    rs["ffn_w_down"] = (p_down[0], s2_down)

    g_up = _matmul(h2, dup, "tn", F32, 1024, 1024, big, "ffn_up_bwd_w")[0]
    dh2, s1_up = _matmul(dup, wb["ffn_w_up"], "nt", F32, big, 1024, 1024, "ffn_up_bwd_x",
                         jobs=[stage1(["ffn_w_up"], [g_up])])
    p_up = add(["ffn_w_up"], [g_up], [s1_up])
    dx1, dx1b, g["norm2_w"] = _rmsnorm_bwd_add(dx2, dh2, x1, ws["norm2_w"], tm, "norm2_bwd", True)
    da_in, db_in, dya, dyb, dproj, g["merge_gate_b"], s2_up = _mix_bwd(
        dx1b, a_in, b_in, proj, wb["w_ret_o"], wb["w_lru_o"], wb["w_out"], ws["merge_gate_b"], tm, "mix_bwd",
        jobs=[_rs_chip_job(p_up)])
    rs["ffn_w_up"] = (p_up[0], s2_up)

    mid_names = ["w_out", "w_ret_o", "w_lru_o"]
    g_mid = [_matmul(mix, dx1b, "tn", F32, 1024, 1024, big, "w_out_bwd_w")[0],
             _matmul(a_in, dya, "tn", F32, 1024, 1024, big, "w_ret_o_bwd_w")[0],
             _matmul(b_in, dyb, "tn", F32, 1024, 1024, big, "w_lru_o_bwd_w")[0]]
    (dproj, g["lru_conv_w"], g["lru_conv_b"], g_wr, g_wi, g["lru_b_r"], g["lru_b_i"], g["lru_lambda"],
     *s1_mid) = _lru_bwd(db_in, hl, proj, dproj, ws["lru_conv_w"], ws["lru_conv_b"], wb["lru_w_r"], wb["lru_w_i"],
                         ws["lru_b_r"], ws["lru_b_i"], ws["lru_lambda"], B, S, "lru_bwd",
                         jobs=[stage1(mid_names, g_mid)])
    p_mid = add(mid_names, g_mid, s1_mid)
    lru_names = ["lru_w_r", "lru_w_i"]
    dproj, g["ret_gn_w"], *rest = _retention_bwd(
        da_in, o, proj, dproj, cos2, sin2s, lgam, ws["ret_gn_w"], B, S, "retention_bwd",
        jobs=[_rs_chip_job(p_mid), stage1(lru_names, [g_wr, g_wi])])
    s2_mid, s1_lru = rest[:3], rest[3:]
    for n, p, r in zip(mid_names, p_mid, s2_mid):
        rs[n] = (p, r)
    p_lru = add(lru_names, [g_wr, g_wi], s1_lru)

    g_in, *s2_lru = _matmul(h1, dproj, "tn", F32, 1024, 1024, big, "proj_bwd_w", jobs=[_rs_chip_job(p_lru)])
    for n, p, r in zip(lru_names, p_lru, s2_lru):
        rs[n] = (p, r)
    dh1, s1_in = _matmul(dproj, wb["w_in"], "nt", F32, big, 1024, 1024, "proj_bwd_x",
                         jobs=[stage1(["w_in"], [g_in])])
    p_in = add(["w_in"], [g_in], [s1_in])
    grad_x, g["norm1_w"], s2_in = _rmsnorm_bwd_add(dx1, dh1, x, ws["norm1_w"], tm, "norm1_bwd", False,
                                                   jobs=[_rs_chip_job(p_in)])
    rs["w_in"] = (p_in[0], s2_in)
    return loss_acc, grad_x.reshape(B, S, D), g, rs


def kernel(x, positions, norm1_w, w_in, merge_gate_b, ret_gn_w, w_ret_o, lru_conv_w, lru_conv_b, lru_w_r, lru_b_r, lru_w_i, lru_b_i, lru_lambda, w_lru_o, w_out, norm2_w, ffn_w_up, ffn_conv_w, ffn_conv_b, ffn_w_down, norm_f_w, loss_target, m_norm1_w, m_w_in, m_merge_gate_b, m_ret_gn_w, m_w_ret_o, m_lru_conv_w, m_lru_conv_b, m_lru_w_r, m_lru_b_r, m_lru_w_i, m_lru_b_i, m_lru_lambda, m_w_lru_o, m_w_out, m_norm2_w, m_ffn_w_up, m_ffn_conv_w, m_ffn_conv_b, m_ffn_w_down, m_norm_f_w, v_norm1_w, v_w_in, v_merge_gate_b, v_ret_gn_w, v_w_ret_o, v_lru_conv_w, v_lru_conv_b, v_lru_w_r, v_lru_b_r, v_lru_w_i, v_lru_b_i, v_lru_lambda, v_w_lru_o, v_w_out, v_norm2_w, v_ffn_w_up, v_ffn_conv_w, v_ffn_conv_b, v_ffn_w_down, v_norm_f_w):
    names = ["norm1_w", "w_in", "merge_gate_b", "ret_gn_w", "w_ret_o", "lru_conv_w", "lru_conv_b", "lru_w_r", "lru_b_r",
             "lru_w_i", "lru_b_i", "lru_lambda", "w_lru_o", "w_out", "norm2_w", "ffn_w_up", "ffn_conv_w", "ffn_conv_b",
             "ffn_w_down", "norm_f_w"]
    w_args = [norm1_w, w_in, merge_gate_b, ret_gn_w, w_ret_o, lru_conv_w, lru_conv_b, lru_w_r, lru_b_r, lru_w_i, lru_b_i,
              lru_lambda, w_lru_o, w_out, norm2_w, ffn_w_up, ffn_conv_w, ffn_conv_b, ffn_w_down, norm_f_w]
    m_args = [m_norm1_w, m_w_in, m_merge_gate_b, m_ret_gn_w, m_w_ret_o, m_lru_conv_w, m_lru_conv_b, m_lru_w_r, m_lru_b_r,
              m_lru_w_i, m_lru_b_i, m_lru_lambda, m_w_lru_o, m_w_out, m_norm2_w, m_ffn_w_up, m_ffn_conv_w, m_ffn_conv_b,
              m_ffn_w_down, m_norm_f_w]
    v_args = [v_norm1_w, v_w_in, v_merge_gate_b, v_ret_gn_w, v_w_ret_o, v_lru_conv_w, v_lru_conv_b, v_lru_w_r, v_lru_b_r,
              v_lru_w_i, v_lru_b_i, v_lru_lambda, v_w_lru_o, v_w_out, v_norm2_w, v_ffn_w_up, v_ffn_conv_w, v_ffn_conv_b,
              v_ffn_w_down, v_norm_f_w]
    orig_shape = {n: a.shape for n, a in zip(names, w_args)}
    local_shape = {n: s for n, s, _ in BIG + SMALL_SHARDED}
    local_shape.update({n: s for n, s in REPLICATED})
    W = {n: a.reshape(local_shape[n]) for n, a in zip(names, w_args)}
    M = {n: a.reshape(local_shape[n]) for n, a in zip(names, m_args)}
    V = {n: a.reshape(local_shape[n]) for n, a in zip(names, v_args)}

    xi, yi, ci = _mesh_pos()
    dev = 4 * xi + 2 * yi + ci
    chip = (2 * xi + yi).astype(jnp.int32).reshape(1)
    core = ci.astype(jnp.int32).reshape(1)

    first_names = [n for n, _, _ in FIRST]
    small_names = [n for n, _, _ in SMALL_SHARDED]
    gathered = _all_gather_multi([W[n].astype(BF16) for n in first_names] + [W[n] for n in small_names],
                                 [m for _, _, m in FIRST + SMALL_SHARDED], "gather_first_weights")
    wb = dict(zip(first_names, gathered[:len(FIRST)]))
    ws = dict(zip(small_names, gathered[len(FIRST):]))
    for n in ("lru_b_r", "lru_b_i"):
        ws[n] = jnp.transpose(ws[n], (1, 0, 2)).reshape(1, LRU_BLOCKS * LRU_BLOCK)
    for n, _ in REPLICATED:
        ws[n] = W[n]

    late_shards = [W[n].astype(BF16) for n, _, _ in LATE]
    loss_acc, grad_x, g, rs = _local_step(x, positions, loss_target, wb, ws, late_shards, core)

    G_out, D_out, M_out, V_out = {}, {}, {}, {}
    for n, _, _ in BIG:
        G_out[n], D_out[n], M_out[n], V_out[n] = _adamw_shard(rs[n][0], rs[n][1], W[n], M[n], V[n], chip, "adamw_" + n)

    rep_names = [n for n, _ in REPLICATED]
    red_names = rep_names + small_names
    red = _small_allreduce([g[n] for n in red_names] + [loss_acc[0:1, :]], "allreduce_small_grads")
    loss = red[-1][0, 0]
    gs = dict(zip(red_names, red[:-1]))
    for n, s, mode in SMALL_SHARDED:
        if mode == "cols":
            gs[n] = lax.dynamic_slice_in_dim(gs[n], dev * s[1], s[1], axis=1)
        else:
            full = gs[n].reshape(LRU_BLOCKS, LRU_BLOCK)
            gs[n] = lax.dynamic_slice_in_dim(full, dev * s[1], s[1], axis=1)
    d2, m2, v2 = _adamw_small([gs[n] for n in red_names], [W[n] for n in red_names], [M[n] for n in red_names],
                              [V[n] for n in red_names], "adamw_small")
    for i, n in enumerate(red_names):
        G_out[n], D_out[n], M_out[n], V_out[n] = gs[n], d2[i], m2[i], v2[i]

    outs = [loss, grad_x]
    for group in (G_out, D_out, M_out, V_out):
        outs += [group[n].reshape(orig_shape[n]) for n in names]
    return tuple(outs)
```

```python
import math

import jax
import jax.numpy as jnp
from jax import lax
from jax.experimental import pallas as pl
from jax.experimental.pallas import tpu as pltpu

F32 = jnp.float32
BF16 = jnp.bfloat16
MESH = pl.DeviceIdType.MESH

D_MODEL = 1024
CHUNK = 64
RET_HEADS = 4
RET_DK = 128
RET_DV = 256
LRU_BLOCKS = 4
LRU_BLOCK = 256
LRU_CONV = 4
LRU_C = 8.0
D_FF = 3072
FFN_CONV = 3
ROPE_BASE = 10000.0
RMS_EPS = 1e-6
GN_EPS = 1e-6
D_IN = 7168
ADAM_LR, ADAM_B1, ADAM_B2, ADAM_EPS, ADAM_WD, ADAM_STEP = 0.001, 0.9, 0.999, 1e-08, 0.01, 10

N_DEV = 8
V7X_VMEM_BYTES = 64 * 1024 * 1024
VMEM_LIMIT = V7X_VMEM_BYTES - 8 * 1024 * 1024
RET_BLOCK = 256
LANES = 128

COL_Q, COL_K = 0, 4
COL_V, COL_G, COL_XL, COL_YL = 4, 8, 12, 16
COL_GR, COL_GL = 5, 6

HBM_SPEC = pl.BlockSpec(memory_space=pl.ANY)


def _gelu(x):
    c = math.sqrt(2.0 / math.pi)
    t = jnp.tanh(c * (x + 0.044715 * x * x * x))
    return 0.5 * x * (1.0 + t)


def _gelu_and_grad(x):
    c = math.sqrt(2.0 / math.pi)
    x2 = x * x
    t = jnp.tanh(c * (x + 0.044715 * x2 * x))
    g = 0.5 * x * (1.0 + t)
    dg = 0.5 * (1.0 + t) + 0.5 * x * (1.0 - t * t) * c * (1.0 + 3.0 * 0.044715 * x2)
    return g, dg


def _sigmoid(x):
    return 1.0 / (1.0 + jnp.exp(-x))


def _shift_down(x, s, fill):
    r = pltpu.roll(x, s, 0)
    rows = lax.broadcasted_iota(jnp.int32, x.shape, 0)
    return jnp.where(rows >= s, r, fill)


def _shift_up(x, s, fill):
    n = x.shape[0]
    r = pltpu.roll(x, n - s, 0)
    rows = lax.broadcasted_iota(jnp.int32, x.shape, 0)
    return jnp.where(rows < n - s, r, fill)


def _dot(a, b, dims):
    return lax.dot_general(a, b, (dims, ((), ())), preferred_element_type=F32)


NN = ((1,), (0,))
NT = ((1,), (1,))
TN = ((0,), (0,))


def _mesh_pos():
    return lax.axis_index("x"), lax.axis_index("y"), lax.axis_index("c")


def _other_chips(x, y):
    return [(1 - x, y), (x, 1 - y), (1 - x, 1 - y)]


def _full_shape(shard_shape, mode):
    if mode == "rows":
        return (N_DEV * shard_shape[0],) + tuple(shard_shape[1:])
    if mode == "cols":
        return (shard_shape[0], N_DEV * shard_shape[1])
    if mode == "mid":
        return (shard_shape[0], N_DEV * shard_shape[1], shard_shape[2])
    return (N_DEV,) + tuple(shard_shape)


def _extent(shard_shape, mode):
    return {"rows": shard_shape[0], "cols": shard_shape[1], "mid": shard_shape[1], "stack": 1}[mode]


def _window(ref, mode, extent, d):
    if mode == "stack":
        return ref.at[d]
    start = pl.multiple_of(d * extent, extent)
    if mode == "rows":
        return ref.at[pl.ds(start, extent)]
    if mode == "cols":
        return ref.at[:, pl.ds(start, extent)]
    return ref.at[:, pl.ds(start, extent), :]


class _Job:
    def __init__(self, inputs, out_shapes, sems, start, finish, aliases=None):
        self.inputs, self.out_shapes, self.sems = list(inputs), list(out_shapes), sems
        self.start, self.finish, self.aliases = start, finish, dict(aliases or {})


def _remote(src, dst, send_sem, recv_sem, to):
    return pltpu.make_async_remote_copy(src_ref=src, dst_ref=dst, send_sem=send_sem, recv_sem=recv_sem,
                                        device_id=to, device_id_type=MESH)


def _ag_first_job(shards, modes):
    n = len(shards)
    extents = [_extent(s.shape, m) for s, m in zip(shards, modes)]

    def copies(x_refs, out_refs, send, recv, local, arriving):
        x, y, c = _mesh_pos()
        peers = [(x, y, 1 - c)] + [(*chip, c) for chip in _other_chips(x, y)]
        win = lambda i, p: _window(out_refs[i], modes[i], extents[i], 4 * p[0] + 2 * p[1] + p[2])
        if arriving:
            return [_remote(x_refs[i], win(i, p), send.at[i, k], recv.at[i, k], p)
                    for i in range(n) for k, p in enumerate(peers)]
        mine = [pltpu.make_async_copy(x_refs[i], win(i, (x, y, c)), local.at[i]) for i in range(n)]
        sends = [_remote(x_refs[i], win(i, (x, y, c)), send.at[i, k], recv.at[i, k], p)
                 for i in range(n) for k, p in enumerate(peers)]
        return mine, sends

    def start(*refs):
        mine, sends = copies(*refs, False)
        for cp in mine + sends:
            cp.start()

    def finish(*refs):
        for cp in copies(*refs, True):
            cp.wait_recv()
        mine, sends = copies(*refs, False)
        for cp in sends:
            cp.wait_send()
        for cp in mine:
            cp.wait()

    out_shapes = [jax.ShapeDtypeStruct(_full_shape(s.shape, m), s.dtype) for s, m in zip(shards, modes)]
    return _Job(shards, out_shapes, ((n, 4), (n, 4), (n,)), start, finish)


def _ag_second_job(fulls, modes, shard_shapes):
    n = len(fulls)
    extents = [_extent(s, m) for s, m in zip(shard_shapes, modes)]

    def copies(_, out_refs, send, recv, local, core_of_block):
        x, y, c = _mesh_pos()
        pc = c if core_of_block == "mine" else 1 - c
        win = lambda i, chip: _window(out_refs[i], modes[i], extents[i], 4 * chip[0] + 2 * chip[1] + pc)
        return [_remote(win(i, chip), win(i, chip), send.at[i, j], recv.at[i, j], (x, y, 1 - c))
                for i in range(n) for j, chip in enumerate(_other_chips(x, y))]

    def start(*refs):
        for cp in copies(*refs, "mine"):
            cp.start()

    def finish(*refs):
        for cp in copies(*refs, "sibling"):
            cp.wait_recv()
        for cp in copies(*refs, "mine"):
            cp.wait_send()

    out_shapes = [jax.ShapeDtypeStruct(f.shape, f.dtype) for f in fulls]
    return _Job(fulls, out_shapes, ((n, 3), (n, 3), (1,)), start, finish, aliases={i: i for i in range(n)})


def _rs_sibling_job(grads, modes, shard_shapes):
    n = len(grads)
    extents = [_extent(s, m) for s, m in zip(shard_shapes, modes)]

    def copies(g_refs, out_refs, send, recv, local):
        x, y, c = _mesh_pos()
        return [_remote(_window(g_refs[i], modes[i], extents[i], 2 * k + (1 - c)), out_refs[i].at[k],
                        send.at[i, k], recv.at[i, k], (x, y, 1 - c))
                for i in range(n) for k in range(4)]

    def start(*refs):
        for cp in copies(*refs):
            cp.start()

    def finish(*refs):
        cps = copies(*refs)
        for cp in cps:
            cp.wait_recv()
        for cp in cps:
            cp.wait_send()

    out_shapes = [jax.ShapeDtypeStruct((4,) + tuple(s), g.dtype) for s, g in zip(shard_shapes, grads)]
    return _Job(grads, out_shapes, ((n, 4), (n, 4), (1,)), start, finish)


def _rs_chip_job(partials):
    n = len(partials)

    def copies(p_refs, out_refs, send, recv, local):
        x, y, c = _mesh_pos()
        return [_remote(p_refs[i].at[2 * px + py], out_refs[i].at[j], send.at[i, j], recv.at[i, j], (px, py, c))
                for i in range(n) for j, (px, py) in enumerate(_other_chips(x, y))]

    def start(*refs):
        for cp in copies(*refs):
            cp.start()

    def finish(*refs):
        cps = copies(*refs)
        for cp in cps:
            cp.wait_recv()
        for cp in cps:
            cp.wait_send()

    out_shapes = [jax.ShapeDtypeStruct((3,) + tuple(p.shape[1:]), p.dtype) for p in partials]
    return _Job(partials, out_shapes, ((n, 3), (n, 3), (1,)), start, finish)


def _all_true(conds):
    out = conds[0]
    for c in conds[1:]:
        out = jnp.logical_and(out, c)
    return out


def _pcall(body, *, name, grid, in_specs, out_specs, out_shape, args, sem, scratch=(), jobs=(), alias_in_out=None):
    n_in, n_out, n_scr = len(args), len(out_shape), len(scratch)
    job_in = [a for j in jobs for a in j.inputs]
    job_out = [s for j in jobs for s in j.out_shapes]
    job_sems = [pltpu.SemaphoreType.DMA(shape) for j in jobs for shape in j.sems]
    aliases, in_off, out_off = dict(alias_in_out or {}), n_in, n_out
    for j in jobs:
        for a, b in j.aliases.items():
            aliases[in_off + a] = out_off + b
        in_off += len(j.inputs)
        out_off += len(j.out_shapes)

    def wrapped(*refs):
        ins = refs[:n_in]
        jins = refs[n_in:n_in + len(job_in)]
        o0 = n_in + len(job_in)
        outs = refs[o0:o0 + n_out]
        jouts = refs[o0 + n_out:o0 + n_out + len(job_out)]
        s0 = o0 + n_out + len(job_out)
        scr = refs[s0:s0 + n_scr]
        jsems = refs[s0 + n_scr:]
        if jobs:
            ids = [pl.program_id(a) for a in range(len(grid))]
            first = _all_true([i == 0 for i in ids])
            last = _all_true([i == g - 1 for i, g in zip(ids, grid)])

            def per_job(which):
                i0 = o0_ = 0
                for k, j in enumerate(jobs):
                    fn = j.start if which == "start" else j.finish
                    fn(jins[i0:i0 + len(j.inputs)], jouts[o0_:o0_ + len(j.out_shapes)], *jsems[3 * k:3 * k + 3])
                    i0 += len(j.inputs)
                    o0_ += len(j.out_shapes)

            @pl.when(first)
            def _():
                per_job("start")

        body(*ins, *outs, *scr)
        if jobs:
            @pl.when(last)
            def _():
                per_job("finish")

    semantics = tuple("arbitrary" for _ in grid) if jobs else sem
    return pl.pallas_call(
        wrapped, name=name, grid=grid,
        in_specs=list(in_specs) + [HBM_SPEC] * len(job_in),
        out_specs=list(out_specs) + [HBM_SPEC] * len(job_out),
        out_shape=list(out_shape) + job_out,
        scratch_shapes=list(scratch) + job_sems,
        input_output_aliases=aliases,
        compiler_params=pltpu.CompilerParams(dimension_semantics=semantics, vmem_limit_bytes=VMEM_LIMIT),
    )(*args, *job_in)


def _row_tile(rows, cap):
    if rows <= cap:
        return rows
    best = None
    for t in range(16, cap + 1, 16):
        if rows % t == 0:
            best = t
    assert best is not None
    return best


def _matmul(a, b, mode, out_dtype, tm, tn, tk, name, add=None, jobs=()):
    if mode == "tn":
        K, M = a.shape
    else:
        M, K = a.shape
    N = b.shape[0] if mode == "nt" else b.shape[1]
    tm, tn, tk = min(tm, M), min(tn, N), min(tk, K)
    assert M % tm == 0 and N % tn == 0 and K % tk == 0
    nk = K // tk
    dims = {"nn": NN, "nt": NT, "tn": TN}[mode]

    def body(*refs):
        if add is None:
            a_ref, b_ref, o_ref, acc = refs
            add_ref = None
        else:
            a_ref, b_ref, add_ref, o_ref, acc = refs
        k = pl.program_id(2)
        p = _dot(a_ref[...], b_ref[...], dims)

        def finish(r):
            if add_ref is not None:
                r = r + add_ref[...].astype(F32)
            o_ref[...] = r.astype(out_dtype)

        if nk == 1:
            finish(p)
        else:
            @pl.when(k == 0)
            def _():
                acc[...] = p

            @pl.when(k > 0)
            def _():
                acc[...] += p

            @pl.when(k == nk - 1)
            def _():
                finish(acc[...])

    if mode == "tn":
        a_spec = pl.BlockSpec((tk, tm), lambda i, j, k: (k, i))
    else:
        a_spec = pl.BlockSpec((tm, tk), lambda i, j, k: (i, k))
    if mode == "nt":
        b_spec = pl.BlockSpec((tn, tk), lambda i, j, k: (j, k))
    else:
        b_spec = pl.BlockSpec((tk, tn), lambda i, j, k: (k, j))
    in_specs = [a_spec, b_spec]
    args = [a, b]
    if add is not None:
        in_specs.append(pl.BlockSpec((tm, tn), lambda i, j, k: (i, j)))
        args.append(add)
    return _pcall(
        body, name=name, grid=(M // tm, N // tn, nk), in_specs=in_specs,
        out_specs=[pl.BlockSpec((tm, tn), lambda i, j, k: (i, j))],
        out_shape=[jax.ShapeDtypeStruct((M, N), out_dtype)], args=args,
        scratch=[pltpu.VMEM((tm, tn) if nk > 1 else (8, LANES), F32)],
        sem=("parallel", "parallel", "arbitrary"), jobs=jobs)


def _rmsnorm_fwd(x, w, tm, name):
    T, D = x.shape

    def body(x_ref, w_ref, h_ref):
        xv = x_ref[...]
        r = lax.rsqrt(jnp.mean(xv * xv, axis=-1, keepdims=True) + RMS_EPS)
        h_ref[...] = (xv * r * w_ref[...]).astype(BF16)

    return _pcall(
        body, name=name, grid=(T // tm,),
        in_specs=[pl.BlockSpec((tm, D), lambda i: (i, 0)), pl.BlockSpec((1, D), lambda i: (0, 0))],
        out_specs=[pl.BlockSpec((tm, D), lambda i: (i, 0))],
        out_shape=[jax.ShapeDtypeStruct((T, D), BF16)], args=[x, w], sem=("parallel",))[0]


def _rmsnorm_bwd_add(dres, dh, x, w, tm, name, want_bf16, jobs=()):
    T, D = x.shape

    def body(dres_ref, dh_ref, x_ref, w_ref, *outs):
        if want_bf16:
            dx_ref, dxb_ref, dw_ref = outs
        else:
            dx_ref, dw_ref = outs
        i = pl.program_id(0)
        xv = x_ref[...]
        r = lax.rsqrt(jnp.mean(xv * xv, axis=-1, keepdims=True) + RMS_EPS)
        xh = xv * r
        dh_v = dh_ref[...].astype(F32)
        dxh = dh_v * w_ref[...]
        dx = dres_ref[...] + r * (dxh - xh * jnp.mean(dxh * xh, axis=-1, keepdims=True))
        dx_ref[...] = dx
        if want_bf16:
            dxb_ref[...] = dx.astype(BF16)
        part = jnp.sum(dh_v * xh, axis=0, keepdims=True)

        @pl.when(i == 0)
        def _():
            dw_ref[...] = part

        @pl.when(i > 0)
        def _():
            dw_ref[...] += part

    tile = pl.BlockSpec((tm, D), lambda i: (i, 0))
    row = pl.BlockSpec((1, D), lambda i: (0, 0))
    out_specs = [tile] + ([tile] if want_bf16 else []) + [row]
    out_shape = ([jax.ShapeDtypeStruct((T, D), F32)] + ([jax.ShapeDtypeStruct((T, D), BF16)] if want_bf16 else [])
                 + [jax.ShapeDtypeStruct((1, D), F32)])
    return _pcall(body, name=name, grid=(T // tm,), in_specs=[tile, tile, tile, row], out_specs=out_specs,
                  out_shape=out_shape, args=[dres, dh, x, w], sem=("arbitrary",), jobs=jobs)


def _loss_head(x2, target, wf, tm, name):
    T, D = x2.shape

    def body(x_ref, t_ref, w_ref, dx_ref, loss_ref, dw_ref):
        i = pl.program_id(0)
        xv = x_ref[...]
        r = lax.rsqrt(jnp.mean(xv * xv, axis=-1, keepdims=True) + RMS_EPS)
        xh = xv * r
        wv = w_ref[...]
        e = xh * wv - t_ref[...]
        lpart = 0.5 * jnp.sum(jnp.sum(e * e, axis=-1, keepdims=True), axis=0, keepdims=True) * (1.0 / D)
        dy = e * (1.0 / D)
        dxh = dy * wv
        dx_ref[...] = r * (dxh - xh * jnp.mean(dxh * xh, axis=-1, keepdims=True))
        wpart = jnp.sum(dy * xh, axis=0, keepdims=True)
        lfull = jnp.broadcast_to(lpart, (8, LANES))

        @pl.when(i == 0)
        def _():
            loss_ref[...] = lfull
            dw_ref[...] = wpart

        @pl.when(i > 0)
        def _():
            loss_ref[...] += lfull
            dw_ref[...] += wpart

    tile = pl.BlockSpec((tm, D), lambda i: (i, 0))
    row = pl.BlockSpec((1, D), lambda i: (0, 0))
    return _pcall(
        body, name=name, grid=(T // tm,), in_specs=[tile, tile, row],
        out_specs=[tile, pl.BlockSpec((8, LANES), lambda i: (0, 0)), row],
        out_shape=[jax.ShapeDtypeStruct((T, D), F32), jax.ShapeDtypeStruct((8, LANES), F32),
                   jax.ShapeDtypeStruct((1, D), F32)],
        args=[x2, target, wf], sem=("arbitrary",))


def _rope_tables(pos_col, inv2, tm, name):
    T = pos_col.shape[0]

    def body(p_ref, f_ref, c_ref, s_ref):
        ang = p_ref[...] * f_ref[...]
        lane = lax.broadcasted_iota(jnp.int32, ang.shape, 1)
        c_ref[...] = jnp.cos(ang)
        s_ref[...] = jnp.where(lane < RET_DK // 2, -1.0, 1.0) * jnp.sin(ang)

    tile = pl.BlockSpec((tm, RET_DK), lambda i: (i, 0))
    return _pcall(
        body, name=name, grid=(T // tm,),
        in_specs=[pl.BlockSpec((tm, 1), lambda i: (i, 0)), pl.BlockSpec((1, RET_DK), lambda i: (0, 0))],
        out_specs=[tile, tile], out_shape=[jax.ShapeDtypeStruct((T, RET_DK), F32)] * 2,
        args=[pos_col, inv2], sem=("parallel",))


def _mix_fwd(a_in, b_in, proj, x, w_ro, w_lo, w_out, mb, w2, tm, name):
    T, D = x.shape

    def body(a_ref, b_ref, gr_ref, gl_ref, x_ref, wro_ref, wlo_ref, wout_ref, mb_ref, w2_ref,
             x1_ref, mix_ref, h2_ref):
        ya = _dot(a_ref[...], wro_ref[...], NN)
        yb = _dot(b_ref[...], wlo_ref[...], NN)
        sa = _sigmoid(gr_ref[...] + mb_ref[0:1, :])
        sb = _sigmoid(gl_ref[...] + mb_ref[1:2, :])
        mix = (sa * ya + sb * yb).astype(BF16)
        mix_ref[...] = mix
        x1 = x_ref[...] + _dot(mix, wout_ref[...], NN)
        x1_ref[...] = x1
        r = lax.rsqrt(jnp.mean(x1 * x1, axis=-1, keepdims=True) + RMS_EPS)
        h2_ref[...] = (x1 * r * w2_ref[...]).astype(BF16)

    tile = pl.BlockSpec((tm, D), lambda i: (i, 0))
    wspec = pl.BlockSpec((D, D), lambda i: (0, 0))
    return _pcall(
        body, name=name, grid=(T // tm,),
        in_specs=[tile, tile,
                  pl.BlockSpec((tm, D), lambda i: (i, COL_GR)), pl.BlockSpec((tm, D), lambda i: (i, COL_GL)),
                  tile, wspec, wspec, wspec,
                  pl.BlockSpec((2, D), lambda i: (0, 0)), pl.BlockSpec((1, D), lambda i: (0, 0))],
        out_specs=[tile, tile, tile],
        out_shape=[jax.ShapeDtypeStruct((T, D), F32), jax.ShapeDtypeStruct((T, D), BF16),
                   jax.ShapeDtypeStruct((T, D), BF16)],
        args=[a_in, b_in, proj, proj, x, w_ro, w_lo, w_out, mb, w2], sem=("parallel",))


def _write_pieces(dst_ref, sems, stashes, row0, col0s, ids, grid, compute):
    def aligned(v, m):
        return v if isinstance(v, int) else pl.multiple_of(v, m)

    def copies():
        return [pltpu.make_async_copy(
                    st, dst_ref.at[pl.ds(aligned(row0, 16), st.shape[0]), pl.ds(aligned(c0, LANES), st.shape[1])],
                    sems.at[k])
                for k, (st, c0) in enumerate(zip(stashes, col0s))]

    first = _all_true([i == 0 for i in ids])
    last = _all_true([i == g - 1 for i, g in zip(ids, grid)])

    @pl.when(jnp.logical_not(first))
    def _():
        for cp in copies():
            cp.wait()

    compute()
    for cp in copies():
        cp.start()

    @pl.when(last)
    def _():
        for cp in copies():
            cp.wait()


def _mix_bwd(dx1b, a_in, b_in, proj, w_ro, w_lo, w_out, mb, tm, name, jobs=()):
    T, D = a_in.shape
    grid = (T // tm,)

    def body(dx_ref, a_ref, b_ref, gr_ref, gl_ref, wro_ref, wlo_ref, wout_ref, mb_ref,
             da_ref, db_ref, dya_ref, dyb_ref, dp_ref, dmb_ref, dgr_s, dgl_s, wsem):
        i = pl.program_id(0)

        def compute():
            dmix = _dot(dx_ref[...], wout_ref[...], NT)
            ya = _dot(a_ref[...], wro_ref[...], NN)
            yb = _dot(b_ref[...], wlo_ref[...], NN)
            sa = _sigmoid(gr_ref[...] + mb_ref[0:1, :])
            sb = _sigmoid(gl_ref[...] + mb_ref[1:2, :])
            dya = (dmix * sa).astype(BF16)
            dyb = (dmix * sb).astype(BF16)
            dgr = dmix * ya * sa * (1.0 - sa)
            dgl = dmix * yb * sb * (1.0 - sb)
            dya_ref[...] = dya
            dyb_ref[...] = dyb
            dgr_s[...] = dgr.astype(BF16)
            dgl_s[...] = dgl.astype(BF16)
            da_ref[...] = _dot(dya, wro_ref[...], NT)
            db_ref[...] = _dot(dyb, wlo_ref[...], NT)

            @pl.when(i == 0)
            def _():
                dmb_ref[...] = jnp.zeros_like(dmb_ref)

            dmb_ref[0:1, :] += jnp.sum(dgr, axis=0, keepdims=True)
            dmb_ref[1:2, :] += jnp.sum(dgl, axis=0, keepdims=True)

        _write_pieces(dp_ref, wsem, [dgr_s, dgl_s], i * tm, [COL_GR * D, COL_GL * D], [i], grid, compute)

    tile = pl.BlockSpec((tm, D), lambda i: (i, 0))
    wspec = pl.BlockSpec((D, D), lambda i: (0, 0))
    two = pl.BlockSpec((2, D), lambda i: (0, 0))
    return _pcall(
        body, name=name, grid=grid,
        in_specs=[tile, tile, tile,
                  pl.BlockSpec((tm, D), lambda i: (i, COL_GR)), pl.BlockSpec((tm, D), lambda i: (i, COL_GL)),
                  wspec, wspec, wspec, two],
        out_specs=[tile] * 4 + [HBM_SPEC, two],
        out_shape=[jax.ShapeDtypeStruct((T, D), F32)] * 2 + [jax.ShapeDtypeStruct((T, D), BF16)] * 2
                  + [jax.ShapeDtypeStruct((T, D_IN), BF16), jax.ShapeDtypeStruct((2, D), F32)],
        args=[dx1b, a_in, b_in, proj, proj, w_ro, w_lo, w_out, mb],
        scratch=[pltpu.VMEM((tm, D), BF16), pltpu.VMEM((tm, D), BF16), pltpu.SemaphoreType.DMA((2,))],
        sem=("arbitrary",), jobs=jobs)


def _ret_decay_consts(lg):
    L = RET_BLOCK
    n = lax.broadcasted_iota(jnp.int32, (L, L), 0)
    m = lax.broadcasted_iota(jnp.int32, (L, L), 1)
    cn, cm = n // CHUNK, m // CHUNK
    expo = jnp.where(cn == cm, jnp.abs(n - m), n - m).astype(F32)
    wm = jnp.where(cm <= cn, jnp.exp(lg * expo), 0.0)
    idx = lax.broadcasted_iota(jnp.int32, (L, 1), 0).astype(F32)
    qd = jnp.exp(lg * (idx + 1.0))
    kd = jnp.exp(lg * (L - 1.0 - idx))
    bd = jnp.exp(lg * float(L))
    return wm, qd, kd, bd


def _rotate(v, cos2, sin2s):
    return v * cos2 + pltpu.roll(v, RET_DK // 2, 1) * sin2s


def _rotate_t(d, cos2, sin2s):
    return d * cos2 - pltpu.roll(d, RET_DK // 2, 1) * sin2s


def _retention_fwd(proj, cos2, sin2s, lgam, gn_w, B, S, name, jobs=()):
    T = B * S
    nb = S // RET_BLOCK
    scale = RET_DK ** -0.5

    def body(q_ref, k_ref, v_ref, g_ref, c_ref, s_ref, lg_ref, gw_ref, o_ref, a_ref, qr, kr, st):
        wm, qd, kd, bd = _ret_decay_consts(lg_ref[0:1, 0:1])
        cos2, sin2s = c_ref[...], s_ref[...]
        qr[...] = _rotate(q_ref[...], cos2, sin2s)
        kr[...] = _rotate(k_ref[...], cos2, sin2s) * scale
        st[...] = jnp.zeros_like(st)
        gw = gw_ref[...]
        for j in range(nb):
            rows = pl.ds(j * RET_BLOCK, RET_BLOCK)
            qb = qr[rows, :]
            kb = kr[rows, :]
            vb = v_ref[rows, :].astype(BF16)
            sc = _dot(qb.astype(BF16), kb.astype(BF16), NT) * wm
            o = _dot(sc.astype(BF16), vb, NN) + _dot((qb * qd).astype(BF16), st[...].astype(BF16), NN)
            st[...] = st[...] * bd + _dot((kb * kd).astype(BF16), vb, TN)
            o_ref[rows, :] = o
            mu = jnp.mean(o, axis=-1, keepdims=True)
            oc = o - mu
            var = jnp.mean(oc * oc, axis=-1, keepdims=True)
            y = oc * lax.rsqrt(var + GN_EPS) * gw
            g = g_ref[rows, :]
            a_ref[rows, :] = (y * (g * _sigmoid(g))).astype(BF16)

    blk = lambda w, off: pl.BlockSpec((S, w), lambda b, h: (b, off + h))
    return _pcall(
        body, name=name, grid=(B, RET_HEADS),
        in_specs=[blk(RET_DK, COL_Q), blk(RET_DK, COL_K), blk(RET_DV, COL_V), blk(RET_DV, COL_G),
                  pl.BlockSpec((S, RET_DK), lambda b, h: (b, 0)), pl.BlockSpec((S, RET_DK), lambda b, h: (b, 0)),
                  pl.BlockSpec((None, 8, LANES), lambda b, h: (h, 0, 0)),
                  pl.BlockSpec((1, RET_DV), lambda b, h: (0, h))],
        out_specs=[blk(RET_DV, 0), blk(RET_DV, 0)],
        out_shape=[jax.ShapeDtypeStruct((T, RET_HEADS * RET_DV), F32),
                   jax.ShapeDtypeStruct((T, RET_HEADS * RET_DV), BF16)],
        args=[proj, proj, proj, proj, cos2, sin2s, lgam, gn_w],
        scratch=[pltpu.VMEM((S, RET_DK), F32), pltpu.VMEM((S, RET_DK), F32), pltpu.VMEM((RET_DK, RET_DV), F32)],
        sem=("parallel", "parallel"), jobs=jobs)


def _retention_bwd(da_in, o, proj, dproj, cos2, sin2s, lgam, gn_w, B, S, name, jobs=()):
    T = B * S
    nb = S // RET_BLOCK
    scale = RET_DK ** -0.5
    grid = (RET_HEADS, B)

    def body(da_ref, o_ref, q_ref, k_ref, v_ref, g_ref, c_ref, s_ref, lg_ref, gw_ref, _, dp_ref, dgw_ref,
             qr, kr, do_s, sts, rst, dq_s, dk_s, dv_s, dg_s, wsem):
        h, b = pl.program_id(0), pl.program_id(1)

        def compute():
            wm, qd, kd, bd = _ret_decay_consts(lg_ref[0:1, 0:1])
            cos2, sin2s = c_ref[...], s_ref[...]
            qr[...] = _rotate(q_ref[...], cos2, sin2s)
            kr[...] = _rotate(k_ref[...], cos2, sin2s) * scale
            gw = gw_ref[...]
            st = jnp.zeros((RET_DK, RET_DV), F32)
            dgw = jnp.zeros((1, RET_DV), F32)
            for j in range(nb):
                rows = pl.ds(j * RET_BLOCK, RET_BLOCK)
                ov = o_ref[rows, :]
                mu = jnp.mean(ov, axis=-1, keepdims=True)
                oc = ov - mu
                rstd = lax.rsqrt(jnp.mean(oc * oc, axis=-1, keepdims=True) + GN_EPS)
                y = oc * rstd
                g = g_ref[rows, :]
                sg = _sigmoid(g)
                da = da_ref[rows, :]
                dg_s[rows, :] = (da * (y * gw) * (sg * (1.0 + g * (1.0 - sg)))).astype(BF16)
                dyw = da * (g * sg)
                dgw = dgw + jnp.sum(dyw * y, axis=0, keepdims=True)
                dy = dyw * gw
                do_s[rows, :] = rstd * (dy - jnp.mean(dy, axis=-1, keepdims=True)
                                        - y * jnp.mean(dy * y, axis=-1, keepdims=True))
                sts[j] = st
                st = st * bd + _dot((kr[rows, :] * kd).astype(BF16), v_ref[rows, :].astype(BF16), TN)

            @pl.when(b == 0)
            def _():
                dgw_ref[...] = dgw

            @pl.when(b > 0)
            def _():
                dgw_ref[...] += dgw

            rst[...] = jnp.zeros_like(rst)
            for j in reversed(range(nb)):
                rows = pl.ds(j * RET_BLOCK, RET_BLOCK)
                qb = qr[rows, :]
                kb = kr[rows, :]
                qbb, kbb = qb.astype(BF16), kb.astype(BF16)
                vb = v_ref[rows, :].astype(BF16)
                dob = do_s[rows, :]
                dobb = dob.astype(BF16)
                a_m = (_dot(qbb, kbb, NT) * wm).astype(BF16)
                b_m = (_dot(dobb, vb, NT) * wm).astype(BF16)
                rb = rst[...].astype(BF16)
                dq = _dot(b_m, kbb, NN) + _dot((dob * qd).astype(BF16), sts[j].astype(BF16), NT)
                dk = _dot(b_m, qbb, TN) + kd * _dot(vb, rb, NT)
                dv = _dot(a_m, dobb, TN) + kd * _dot(kbb, rb, NN)
                rst[...] = rst[...] * bd + _dot((qb * qd).astype(BF16), dobb, TN)
                cb, sb = c_ref[rows, :], s_ref[rows, :]
                dq_s[rows, :] = _rotate_t(dq, cb, sb).astype(BF16)
                dk_s[rows, :] = _rotate_t(dk * scale, cb, sb).astype(BF16)
                dv_s[rows, :] = dv.astype(BF16)

        cols = [(COL_Q + h) * RET_DK, (COL_K + h) * RET_DK, (COL_V + h) * RET_DV, (COL_G + h) * RET_DV]
        _write_pieces(dp_ref, wsem, [dq_s, dk_s, dv_s, dg_s], b * S, cols, [h, b], grid, compute)

    blk = lambda w, off: pl.BlockSpec((S, w), lambda h, b: (b, off + h))
    return _pcall(
        body, name=name, grid=grid,
        in_specs=[blk(RET_DV, 0), blk(RET_DV, 0),
                  blk(RET_DK, COL_Q), blk(RET_DK, COL_K), blk(RET_DV, COL_V), blk(RET_DV, COL_G),
                  pl.BlockSpec((S, RET_DK), lambda h, b: (b, 0)), pl.BlockSpec((S, RET_DK), lambda h, b: (b, 0)),
                  pl.BlockSpec((None, 8, LANES), lambda h, b: (h, 0, 0)),
                  pl.BlockSpec((1, RET_DV), lambda h, b: (0, h)), HBM_SPEC],
        out_specs=[HBM_SPEC, pl.BlockSpec((1, RET_DV), lambda h, b: (0, h))],
        out_shape=[jax.ShapeDtypeStruct(dproj.shape, dproj.dtype),
                   jax.ShapeDtypeStruct((1, RET_HEADS * RET_DV), F32)],
        args=[da_in, o, proj, proj, proj, proj, cos2, sin2s, lgam, gn_w, dproj],
        scratch=[pltpu.VMEM((S, RET_DK), F32), pltpu.VMEM((S, RET_DK), F32),
                 pltpu.VMEM((S, RET_DV), F32), pltpu.VMEM((nb, RET_DK, RET_DV), F32),
                 pltpu.VMEM((RET_DK, RET_DV), F32),
                 pltpu.VMEM((S, RET_DK), BF16), pltpu.VMEM((S, RET_DK), BF16),
                 pltpu.VMEM((S, RET_DV), BF16), pltpu.VMEM((S, RET_DV), BF16), pltpu.SemaphoreType.DMA((4,))],
        sem=("arbitrary", "arbitrary"), jobs=jobs, alias_in_out={10: 0})


def _lru_gates(x, cw, cb, wr, wi, br, bi, lam):
    xc = cb + cw[LRU_CONV - 1:LRU_CONV, :] * x
    for j in range(LRU_CONV - 1):
        xc = xc + cw[j:j + 1, :] * _shift_down(x, LRU_CONV - 1 - j, 0.0)
    xcb = xc.astype(BF16)
    r = _sigmoid(_dot(xcb, wr, NN) + br)
    ig = _sigmoid(_dot(xcb, wi, NN) + bi)
    z = -lam
    sp = jnp.maximum(z, 0.0) + jnp.log1p(jnp.exp(-jnp.abs(z)))
    log_a = (-LRU_C) * r * sp
    a = jnp.exp(log_a)
    z2 = 2.0 * log_a
    taylor = -z2 * (1.0 + z2 * (0.5 + z2 * (1.0 / 6.0 + z2 * (1.0 / 24.0 + z2 * (1.0 / 120.0)))))
    om = jnp.where(z2 > -0.05, taylor, 1.0 - jnp.exp(z2))
    sq = jnp.sqrt(om)
    return xc, xcb, r, ig, sp, a, sq


def _lru_fwd(proj, cw, cb, wr, wi, br, bi, lam, B, S, name):
    T = B * S
    W = LRU_BLOCKS * LRU_BLOCK

    def body(x_ref, y_ref, cw_ref, cb_ref, wr_ref, wi_ref, br_ref, bi_ref, lam_ref, h_ref, bin_ref):
        xc, _, _, ig, _, a, sq = _lru_gates(x_ref[...], cw_ref[...], cb_ref[...], wr_ref[...], wi_ref[...],
                                           br_ref[...], bi_ref[...], lam_ref[...])
        bv = sq * ig * xc
        s = 1
        while s < S:
            bv = a * _shift_down(bv, s, 0.0) + bv
            if 2 * s < S:
                a = a * _shift_down(a, s, 1.0)
            s *= 2
        h_ref[...] = bv
        bin_ref[...] = (bv * _gelu(y_ref[...])).astype(BF16)

    blk = lambda off: pl.BlockSpec((S, LRU_BLOCK), lambda b, n: (b, off + n))
    vec = lambda rows: pl.BlockSpec((rows, LRU_BLOCK), lambda b, n: (0, n))
    wspec = pl.BlockSpec((None, LRU_BLOCK, LRU_BLOCK), lambda b, n: (n, 0, 0))
    return _pcall(
        body, name=name, grid=(B, LRU_BLOCKS),
        in_specs=[blk(COL_XL), blk(COL_YL), vec(LRU_CONV), vec(1), wspec, wspec, vec(1), vec(1), vec(1)],
        out_specs=[blk(0), blk(0)],
        out_shape=[jax.ShapeDtypeStruct((T, W), F32), jax.ShapeDtypeStruct((T, W), BF16)],
        args=[proj, proj, cw, cb, wr, wi, br, bi, lam], sem=("parallel", "parallel"))


def _lru_bwd(db_in, h, proj, dproj, cw, cb, wr, wi, br, bi, lam, B, S, name, jobs=()):
    T = B * S
    W = LRU_BLOCKS * LRU_BLOCK

    grid = (LRU_BLOCKS, B)

    def body(dbin_ref, h_ref, x_ref, y_ref, cw_ref, cb_ref, wr_ref, wi_ref, br_ref, bi_ref, lam_ref, _,
             dp_ref, dcw_ref, dcb_ref, dwr_ref, dwi_ref, dbr_ref, dbi_ref, dlam_ref, dx_s, dy_s, wsem):
        n, b = pl.program_id(0), pl.program_id(1)

        def compute():
            x = x_ref[...]
            cw = cw_ref[...]
            wr, wi = wr_ref[...], wi_ref[...]
            lam = lam_ref[...]
            xc, xcb, r, ig, sp, a, sq = _lru_gates(x, cw, cb_ref[...], wr, wi, br_ref[...], bi_ref[...], lam)
            hv = h_ref[...]
            gel, dgel = _gelu_and_grad(y_ref[...])
            dbin = dbin_ref[...]
            dy_s[...] = (dbin * hv * dgel).astype(BF16)
            dh = dbin * gel
            an = _shift_up(a, 1, 0.0)
            s = 1
            while s < S:
                dh = dh + an * _shift_up(dh, s, 0.0)
                if 2 * s < S:
                    an = an * _shift_up(an, s, 1.0)
                s *= 2
            hprev = _shift_down(hv, 1, 0.0)
            d_ig = dh * sq * xc
            d_xc = dh * sq * ig
            a2 = a * a
            d_loga = dh * hprev * a - dh * ig * xc * a2 / sq
            d_r = d_loga * ((-LRU_C) * sp)
            d_sp = jnp.sum(d_loga * ((-LRU_C) * r), axis=0, keepdims=True)
            dlam = -d_sp * _sigmoid(-lam)
            d_pr = d_r * r * (1.0 - r)
            d_pi = d_ig * ig * (1.0 - ig)
            d_prb, d_pib = d_pr.astype(BF16), d_pi.astype(BF16)
            d_xc = d_xc + _dot(d_prb, wr, NT) + _dot(d_pib, wi, NT)

            @pl.when(b == 0)
            def _():
                for ref in (dcw_ref, dcb_ref, dwr_ref, dwi_ref, dbr_ref, dbi_ref, dlam_ref):
                    ref[...] = jnp.zeros_like(ref)

            dx = cw[LRU_CONV - 1:LRU_CONV, :] * d_xc
            for j in range(LRU_CONV - 1):
                sft = LRU_CONV - 1 - j
                dx = dx + cw[j:j + 1, :] * _shift_up(d_xc, sft, 0.0)
                dcw_ref[j:j + 1, :] += jnp.sum(d_xc * _shift_down(x, sft, 0.0), axis=0, keepdims=True)
            dcw_ref[LRU_CONV - 1:LRU_CONV, :] += jnp.sum(d_xc * x, axis=0, keepdims=True)
            dx_s[...] = dx.astype(BF16)
            dcb_ref[...] += jnp.sum(d_xc, axis=0, keepdims=True)
            dwr_ref[...] += _dot(xcb, d_prb, TN)
            dwi_ref[...] += _dot(xcb, d_pib, TN)
            dbr_ref[...] += jnp.sum(d_pr, axis=0, keepdims=True)
            dbi_ref[...] += jnp.sum(d_pi, axis=0, keepdims=True)
            dlam_ref[...] += dlam

        cols = [(COL_XL + n) * LRU_BLOCK, (COL_YL + n) * LRU_BLOCK]
        _write_pieces(dp_ref, wsem, [dx_s, dy_s], b * S, cols, [n, b], grid, compute)

    blk = lambda off: pl.BlockSpec((S, LRU_BLOCK), lambda n, b: (b, off + n))
    vec = lambda rows: pl.BlockSpec((rows, LRU_BLOCK), lambda n, b: (0, n))
    wspec = pl.BlockSpec((None, LRU_BLOCK, LRU_BLOCK), lambda n, b: (n, 0, 0))
    vshape = lambda rows: jax.ShapeDtypeStruct((rows, W), F32)
    wshape = jax.ShapeDtypeStruct((LRU_BLOCKS, LRU_BLOCK, LRU_BLOCK), F32)
    return _pcall(
        body, name=name, grid=grid,
        in_specs=[blk(0), blk(0), blk(COL_XL), blk(COL_YL), vec(LRU_CONV), vec(1), wspec, wspec, vec(1), vec(1),
                  vec(1), HBM_SPEC],
        out_specs=[HBM_SPEC, vec(LRU_CONV), vec(1), wspec, wspec, vec(1), vec(1), vec(1)],
        out_shape=[jax.ShapeDtypeStruct(dproj.shape, dproj.dtype),
                   vshape(LRU_CONV), vshape(1), wshape, wshape, vshape(1), vshape(1), vshape(1)],
        args=[db_in, h, proj, proj, cw, cb, wr, wi, br, bi, lam, dproj],
        scratch=[pltpu.VMEM((S, LRU_BLOCK), BF16), pltpu.VMEM((S, LRU_BLOCK), BF16), pltpu.SemaphoreType.DMA((2,))],
        sem=("arbitrary", "arbitrary"), jobs=jobs, alias_in_out={11: 0})


FFN_CT = 256


def _ffn_conv(gate, cw, cb):
    gc = cb + cw[FFN_CONV - 1:FFN_CONV, :] * gate
    for j in range(FFN_CONV - 1):
        gc = gc + cw[j:j + 1, :] * _shift_down(gate, FFN_CONV - 1 - j, 0.0)
    return gc


def _ffn_act_fwd(up, cw, cb, B, S, name):
    T = B * S
    nct = D_FF // FFN_CT

    def body(g_ref, v_ref, cw_ref, cb_ref, f_ref):
        gc = _ffn_conv(g_ref[...], cw_ref[...], cb_ref[...])
        f_ref[...] = (_gelu(gc) * v_ref[...]).astype(BF16)

    return _pcall(
        body, name=name, grid=(B, nct),
        in_specs=[pl.BlockSpec((S, FFN_CT), lambda b, c: (b, c)), pl.BlockSpec((S, FFN_CT), lambda b, c: (b, nct + c)),
                  pl.BlockSpec((FFN_CONV, FFN_CT), lambda b, c: (0, c)), pl.BlockSpec((1, FFN_CT), lambda b, c: (0, c))],
        out_specs=[pl.BlockSpec((S, FFN_CT), lambda b, c: (b, c))],
        out_shape=[jax.ShapeDtypeStruct((T, D_FF), BF16)], args=[up, up, cw, cb], sem=("parallel", "parallel"))[0]


def _ffn_act_bwd(df, up, cw, cb, B, S, name, jobs=()):
    T = B * S
    nct = D_FF // FFN_CT

    grid = (nct, B)

    def body(df_ref, g_ref, v_ref, cw_ref, cb_ref, du_ref, dcw_ref, dcb_ref, dg_s, dv_s, wsem):
        c, b = pl.program_id(0), pl.program_id(1)

        def compute():
            gate = g_ref[...]
            cw = cw_ref[...]
            gc = _ffn_conv(gate, cw, cb_ref[...])
            gel, dgel = _gelu_and_grad(gc)
            dfv = df_ref[...]
            dv_s[...] = (dfv * gel).astype(BF16)
            dgc = dfv * v_ref[...] * dgel

            @pl.when(b == 0)
            def _():
                dcw_ref[...] = jnp.zeros_like(dcw_ref)
                dcb_ref[...] = jnp.zeros_like(dcb_ref)

            dgate = cw[FFN_CONV - 1:FFN_CONV, :] * dgc
            for j in range(FFN_CONV - 1):
                sft = FFN_CONV - 1 - j
                dgate = dgate + cw[j:j + 1, :] * _shift_up(dgc, sft, 0.0)
                dcw_ref[j:j + 1, :] += jnp.sum(dgc * _shift_down(gate, sft, 0.0), axis=0, keepdims=True)
            dcw_ref[FFN_CONV - 1:FFN_CONV, :] += jnp.sum(dgc * gate, axis=0, keepdims=True)
            dg_s[...] = dgate.astype(BF16)
            dcb_ref[...] += jnp.sum(dgc, axis=0, keepdims=True)

        _write_pieces(du_ref, wsem, [dg_s, dv_s], b * S, [c * FFN_CT, (nct + c) * FFN_CT], [c, b], grid, compute)

    blk = pl.BlockSpec((S, FFN_CT), lambda c, b: (b, c))
    return _pcall(
        body, name=name, grid=grid,
        in_specs=[blk, blk, pl.BlockSpec((S, FFN_CT), lambda c, b: (b, nct + c)),
                  pl.BlockSpec((FFN_CONV, FFN_CT), lambda c, b: (0, c)),
                  pl.BlockSpec((1, FFN_CT), lambda c, b: (0, c))],
        out_specs=[HBM_SPEC, pl.BlockSpec((FFN_CONV, FFN_CT), lambda c, b: (0, c)),
                   pl.BlockSpec((1, FFN_CT), lambda c, b: (0, c))],
        out_shape=[jax.ShapeDtypeStruct((T, 2 * D_FF), BF16),
                   jax.ShapeDtypeStruct((FFN_CONV, D_FF), F32), jax.ShapeDtypeStruct((1, D_FF), F32)],
        args=[df, up, up, cw, cb],
        scratch=[pltpu.VMEM((S, FFN_CT), BF16), pltpu.VMEM((S, FFN_CT), BF16), pltpu.SemaphoreType.DMA((2,))],
        sem=("arbitrary", "arbitrary"), jobs=jobs)


def _rs_add(g, recv, mode, core, name):
    shard = tuple(recv.shape[1:])
    if mode == "mid":
        a, e, c2 = shard
        g_in = g.reshape(a, N_DEV, e, c2)
        grid = (4, 1)
        g_spec = pl.BlockSpec((a, None, e, c2), lambda k, i, c_ref: (0, 2 * k + c_ref[0], 0, 0))
        r_spec = pl.BlockSpec((None, a, e, c2), lambda k, i, c_ref: (k, 0, 0, 0))
    else:
        R, C = shard
        tr = _row_tile(R, 512)
        grid = (4, R // tr)
        if mode == "rows":
            g_in = g.reshape(N_DEV, R, C)
            g_spec = pl.BlockSpec((None, tr, C), lambda k, i, c_ref: (2 * k + c_ref[0], i, 0))
        else:
            g_in = g
            g_spec = pl.BlockSpec((tr, C), lambda k, i, c_ref: (i, 2 * k + c_ref[0]))
        r_spec = pl.BlockSpec((None, tr, C), lambda k, i, c_ref: (k, i, 0))

    def body(c_ref, g_ref, r_ref, o_ref):
        o_ref[...] = g_ref[...] + r_ref[...]

    return pl.pallas_call(
        body, name=name,
        grid_spec=pltpu.PrefetchScalarGridSpec(num_scalar_prefetch=1, grid=grid, in_specs=[g_spec, r_spec],
                                               out_specs=r_spec),
        out_shape=jax.ShapeDtypeStruct(recv.shape, recv.dtype),
        compiler_params=pltpu.CompilerParams(dimension_semantics=("parallel", "parallel"),
                                             vmem_limit_bytes=VMEM_LIMIT),
    )(core, g_in, recv)


def _adam_update(gv, w, m, v):
    nm = ADAM_B1 * m + (1.0 - ADAM_B1) * gv
    nv = ADAM_B2 * v + (1.0 - ADAM_B2) * (gv * gv)
    m_hat = nm / (1.0 - ADAM_B1 ** ADAM_STEP)
    v_hat = nv / (1.0 - ADAM_B2 ** ADAM_STEP)
    delta = -ADAM_LR * (m_hat / (jnp.sqrt(v_hat) + ADAM_EPS) + ADAM_WD * w)
    return delta, nm, nv


def _adamw_shard(partial, recv, w, m, v, chip, name):
    shape = tuple(w.shape)
    tr = _row_tile(shape[0], 256)
    rest = shape[1:]
    zeros = (0,) * len(rest)
    tile = pl.BlockSpec((tr,) + rest, lambda i, s: (i,) + zeros)

    def body(_, p_ref, r_ref, w_ref, m_ref, v_ref, g_ref, d_ref, nm_ref, nv_ref):
        gv = p_ref[...] + r_ref[0] + r_ref[1] + r_ref[2]
        g_ref[...] = gv
        d_ref[...], nm_ref[...], nv_ref[...] = _adam_update(gv, w_ref[...], m_ref[...], v_ref[...])

    grid_spec = pltpu.PrefetchScalarGridSpec(
        num_scalar_prefetch=1, grid=(shape[0] // tr,),
        in_specs=[pl.BlockSpec((None, tr) + rest, lambda i, s: (s[0], i) + zeros),
                  pl.BlockSpec((3, tr) + rest, lambda i, s: (0, i) + zeros), tile, tile, tile],
        out_specs=[tile] * 4)
    return pl.pallas_call(
        body, name=name, grid_spec=grid_spec, out_shape=[jax.ShapeDtypeStruct(shape, F32)] * 4,
        compiler_params=pltpu.CompilerParams(dimension_semantics=("parallel",), vmem_limit_bytes=VMEM_LIMIT),
    )(chip, partial, recv, w, m, v)


def _all_gather_multi(shards, modes, name):
    n = len(shards)
    extents = [_extent(s.shape, m) for s, m in zip(shards, modes)]

    def body(*refs):
        x_refs, out_refs = refs[:n], refs[n:2 * n]
        send_sems, recv_sems, local_sems = refs[2 * n:]
        x, y, c = _mesh_pos()
        me, sibling = (x, y, c), (x, y, 1 - c)
        chips = _other_chips(x, y)

        def slot(i, px, py, pc):
            return _window(out_refs[i], modes[i], extents[i], 4 * px + 2 * py + pc)

        def copy(i, k, block, to, src=None):
            return _remote(slot(i, *block) if src is None else src, slot(i, *block),
                           send_sems.at[i, k], recv_sems.at[i, k], to)

        mine = [pltpu.make_async_copy(x_refs[i], slot(i, *me), local_sems.at[i]) for i in range(n)]
        sends = []
        for i in range(n):
            mine[i].start()
            first = [copy(i, 0, me, sibling, src=x_refs[i])]
            first += [copy(i, 1 + j, me, (*chip, c), src=x_refs[i]) for j, chip in enumerate(chips)]
            for cp in first:
                cp.start()
            sends += first
        for i in range(n):
            for j, chip in enumerate(chips):
                copy(i, 1 + j, (*chip, c), me).wait_recv()
                fwd = copy(i, 4 + j, (*chip, c), sibling)
                fwd.start()
                sends.append(fwd)
        for i in range(n):
            copy(i, 0, sibling, me).wait_recv()
            for j, chip in enumerate(chips):
                copy(i, 4 + j, (*chip, 1 - c), me).wait_recv()
        for cp in sends:
            cp.wait_send()
        for cp in mine:
            cp.wait()

    return pl.pallas_call(
        body, name=name,
        in_specs=[HBM_SPEC] * n, out_specs=[HBM_SPEC] * n,
        out_shape=[jax.ShapeDtypeStruct(_full_shape(s.shape, m), s.dtype) for s, m in zip(shards, modes)],
        scratch_shapes=[pltpu.SemaphoreType.DMA((n, 7)), pltpu.SemaphoreType.DMA((n, 7)),
                        pltpu.SemaphoreType.DMA((n,))],
    )(*shards)


SMALL_LANES = 1024


def _small_rows(shape):
    r, w = shape
    return r * max(1, w // SMALL_LANES)


def _small_allreduce(parts, name):
    n = len(parts)
    shapes = [tuple(p.shape) for p in parts]
    offs, total = [], 0
    for s in shapes:
        offs.append(total)
        total += _small_rows(s)
    rows = -(-total // 8) * 8

    def body(*refs):
        p_refs, o_refs = refs[:n], refs[n:2 * n]
        buf, tot, send_sems, recv_sems = refs[2 * n:]
        x, y, c = _mesh_pos()
        me, sibling = (x, y, c), (x, y, 1 - c)
        chips = _other_chips(x, y)

        def slot(px, py, pc):
            return buf.at[4 * px + 2 * py + pc]

        def copy(k, block, to):
            return _remote(slot(*block), slot(*block), send_sems.at[k], recv_sems.at[k], to)

        tot[...] = jnp.zeros_like(tot)
        for p_ref, (r, w), off in zip(p_refs, shapes, offs):
            wl = min(w, SMALL_LANES)
            for part in range(max(1, w // SMALL_LANES)):
                tot[pl.ds(off + part * r, r), pl.ds(0, wl)] = p_ref[:, pl.ds(part * SMALL_LANES, wl)]
        buf[4 * x + 2 * y + c] = tot[...]
        first = [copy(0, me, sibling)] + [copy(1 + j, me, (*chip, c)) for j, chip in enumerate(chips)]
        for cp in first:
            cp.start()
        passed = [copy(4 + j, (*chip, c), sibling) for j, chip in enumerate(chips)]
        for j, chip in enumerate(chips):
            copy(1 + j, (*chip, c), me).wait_recv()
            passed[j].start()
        copy(0, sibling, me).wait_recv()
        for j, chip in enumerate(chips):
            copy(4 + j, (*chip, 1 - c), me).wait_recv()
        for cp in first + passed:
            cp.wait_send()
        acc = buf[0]
        for d in range(1, N_DEV):
            acc = acc + buf[d]
        tot[...] = acc
        for o_ref, (r, w), off in zip(o_refs, shapes, offs):
            wl = min(w, SMALL_LANES)
            for part in range(max(1, w // SMALL_LANES)):
                o_ref[:, pl.ds(part * SMALL_LANES, wl)] = tot[pl.ds(off + part * r, r), pl.ds(0, wl)]

    vm = pl.BlockSpec(memory_space=pltpu.VMEM)
    return pl.pallas_call(
        body, name=name,
        in_specs=[vm] * n, out_specs=[vm] * n,
        out_shape=[jax.ShapeDtypeStruct(s, F32) for s in shapes],
        scratch_shapes=[pltpu.VMEM((N_DEV, rows, SMALL_LANES), F32), pltpu.VMEM((rows, SMALL_LANES), F32),
                        pltpu.SemaphoreType.DMA((7,)), pltpu.SemaphoreType.DMA((7,))],
    )(*parts)


def _adamw_small(gs, ws, ms, vs, name):
    n = len(gs)

    def body(*refs):
        g_r, w_r, m_r, v_r = refs[:n], refs[n:2 * n], refs[2 * n:3 * n], refs[3 * n:4 * n]
        d_r, nm_r, nv_r = refs[4 * n:5 * n], refs[5 * n:6 * n], refs[6 * n:7 * n]
        for i in range(n):
            d_r[i][...], nm_r[i][...], nv_r[i][...] = _adam_update(g_r[i][...], w_r[i][...], m_r[i][...], v_r[i][...])

    vm = pl.BlockSpec(memory_space=pltpu.VMEM)
    shapes = [jax.ShapeDtypeStruct(w.shape, F32) for w in ws]
    outs = pl.pallas_call(body, name=name, in_specs=[vm] * (4 * n), out_specs=[vm] * (3 * n),
                          out_shape=shapes * 3)(*gs, *ws, *ms, *vs)
    return outs[:n], outs[n:2 * n], outs[2 * n:]


FIRST = [("w_in", (1024, 896), "cols"), ("lru_w_r", (4, 32, 256), "mid"), ("lru_w_i", (4, 32, 256), "mid")]
LATE = [("w_ret_o", (128, 1024), "rows"), ("w_lru_o", (128, 1024), "rows"), ("w_out", (128, 1024), "rows"),
        ("ffn_w_up", (1024, 768), "cols"), ("ffn_w_down", (384, 1024), "rows")]
BIG = FIRST + LATE
SMALL_SHARDED = [("merge_gate_b", (2, 128), "cols"), ("lru_conv_w", (4, 128), "cols"), ("lru_b_r", (4, 32), "stack"),
                 ("lru_b_i", (4, 32), "stack"), ("ffn_conv_w", (3, 384), "cols")]
REPLICATED = [("norm1_w", (1, 1024)), ("ret_gn_w", (1, 1024)), ("lru_conv_b", (1, 1024)), ("lru_lambda", (1, 1024)),
              ("norm2_w", (1, 1024)), ("ffn_conv_b", (1, 3072)), ("norm_f_w", (1, 1024))]
MODE = {n: m for n, _, m in BIG}
SHARD = {n: s for n, s, _ in BIG}


def _local_step(x3, positions, target3, wb, ws, late_shards, core):
    B, S, D = x3.shape
    T = B * S
    x = x3.reshape(T, D)
    target = target3.reshape(T, D)
    tm = min(512, T)
    big = min(1024, T)

    half = RET_DK // 2
    inv_freq = ROPE_BASE ** (-jnp.arange(half, dtype=F32) / half)
    inv2 = jnp.concatenate([inv_freq, inv_freq]).reshape(1, RET_DK)
    log_gamma = jnp.log1p(-jnp.power(2.0, -5.0 - jnp.arange(RET_HEADS, dtype=F32)))
    lgam = jnp.broadcast_to(log_gamma[:, None, None], (RET_HEADS, 8, LANES))
    pos_col = positions.astype(F32).reshape(T, 1)
    cos2, sin2s = _rope_tables(pos_col, inv2, tm, "rope_tables")

    late_names = [n for n, _, _ in LATE]
    late_modes = [m for _, _, m in LATE]
    late_shapes = [s for _, s, _ in LATE]

    h1 = _rmsnorm_fwd(x, ws["norm1_w"], tm, "norm1_fwd")
    proj, *late_part = _matmul(h1, wb["w_in"], "nn", F32, big, 1024, 1024, "proj_fwd",
                               jobs=[_ag_first_job(late_shards, late_modes)])
    o, a_in, *late_full = _retention_fwd(proj, cos2, sin2s, lgam, ws["ret_gn_w"], B, S, "retention_fwd",
                                         jobs=[_ag_second_job(late_part, late_modes, late_shapes)])
    wb = dict(wb, **dict(zip(late_names, late_full)))
    hl, b_in = _lru_fwd(proj, ws["lru_conv_w"], ws["lru_conv_b"], wb["lru_w_r"], wb["lru_w_i"],
                        ws["lru_b_r"], ws["lru_b_i"], ws["lru_lambda"], B, S, "lru_fwd")
    x1, mix, h2 = _mix_fwd(a_in, b_in, proj, x, wb["w_ret_o"], wb["w_lru_o"], wb["w_out"],
                           ws["merge_gate_b"], ws["norm2_w"], tm, "mix_fwd")
    up = _matmul(h2, wb["ffn_w_up"], "nn", F32, big, 1024, 1024, "ffn_up_fwd")[0]
    f = _ffn_act_fwd(up, ws["ffn_conv_w"], ws["ffn_conv_b"], B, S, "ffn_act_fwd")
    x2 = _matmul(f, wb["ffn_w_down"], "nn", F32, big, 1024, 1024, "ffn_down_fwd", add=x1)[0]
    dx2, loss_acc, d_norm_f = _loss_head(x2, target, ws["norm_f_w"], tm, "loss_head")

    g, rs = {}, {}

    def stage1(names, grads):
        return _rs_sibling_job(grads, [MODE[n] for n in names], [SHARD[n] for n in names])

    def add(names, grads, recvs):
        return [_rs_add(gr, r, MODE[n], core, "rs_add_" + n) for n, gr, r in zip(names, grads, recvs)]

    g["norm_f_w"] = d_norm_f
    dx2b = dx2.astype(BF16)
    g_down = _matmul(f, dx2b, "tn", F32, 1024, 1024, big, "ffn_down_bwd_w")[0]
    df, s1_down = _matmul(dx2b, wb["ffn_w_down"], "nt", F32, big, 1024, 1024, "ffn_down_bwd_x",
                          jobs=[stage1(["ffn_w_down"], [g_down])])
    p_down = add(["ffn_w_down"], [g_down], [s1_down])
    dup, g["ffn_conv_w"], g["ffn_conv_b"], s2_down = _ffn_act_bwd(
        df, up, ws["ffn_conv_w"], ws["ffn_conv_b"], B, S, "ffn_act_bwd", jobs=[_rs_chip_job(p_down)])
    rs["ffn_w_down"] = (p_down[0], s2_down)

    g_up = _matmul(h2, dup, "tn", F32, 1024, 1024, big, "ffn_up_bwd_w")[0]
    dh2, s1_up = _matmul(dup, wb["ffn_w_up"], "nt", F32, big, 1024, 1024, "ffn_up_bwd_x",
                         jobs=[stage1(["ffn_w_up"], [g_up])])
    p_up = add(["ffn_w_up"], [g_up], [s1_up])
    dx1, dx1b, g["norm2_w"] = _rmsnorm_bwd_add(dx2, dh2, x1, ws["norm2_w"], tm, "norm2_bwd", True)
    da_in, db_in, dya, dyb, dproj, g["merge_gate_b"], s2_up = _mix_bwd(
        dx1b, a_in, b_in, proj, wb["w_ret_o"], wb["w_lru_o"], wb["w_out"], ws["merge_gate_b"], tm, "mix_bwd",
        jobs=[_rs_chip_job(p_up)])
    rs["ffn_w_up"] = (p_up[0], s2_up)

    mid_names = ["w_out", "w_ret_o", "w_lru_o"]
    g_mid = [_matmul(mix, dx1b, "tn", F32, 1024, 1024, big, "w_out_bwd_w")[0],
             _matmul(a_in, dya, "tn", F32, 1024, 1024, big, "w_ret_o_bwd_w")[0],
             _matmul(b_in, dyb, "tn", F32, 1024, 1024, big, "w_lru_o_bwd_w")[0]]
    (dproj, g["lru_conv_w"], g["lru_conv_b"], g_wr, g_wi, g["lru_b_r"], g["lru_b_i"], g["lru_lambda"],
     *s1_mid) = _lru_bwd(db_in, hl, proj, dproj, ws["lru_conv_w"], ws["lru_conv_b"], wb["lru_w_r"], wb["lru_w_i"],
                         ws["lru_b_r"], ws["lru_b_i"], ws["lru_lambda"], B, S, "lru_bwd",
                         jobs=[stage1(mid_names, g_mid)])
    p_mid = add(mid_names, g_mid, s1_mid)
    lru_names = ["lru_w_r", "lru_w_i"]
    dproj, g["ret_gn_w"], *rest = _retention_bwd(
        da_in, o, proj, dproj, cos2, sin2s, lgam, ws["ret_gn_w"], B, S, "retention_bwd",
        jobs=[_rs_chip_job(p_mid), stage1(lru_names, [g_wr, g_wi])])
    s2_mid, s1_lru = rest[:3], rest[3:]
    for n, p, r in zip(mid_names, p_mid, s2_mid):
        rs[n] = (p, r)
    p_lru = add(lru_names, [g_wr, g_wi], s1_lru)

    g_in, *s2_lru = _matmul(h1, dproj, "tn", F32, 1024, 1024, big, "proj_bwd_w", jobs=[_rs_chip_job(p_lru)])
    for n, p, r in zip(lru_names, p_lru, s2_lru):
        rs[n] = (p, r)
    s1_in = _pcall(lambda: None, name="rs_sibling_w_in", grid=(1,), in_specs=[], out_specs=[], out_shape=[], args=[],
                   sem=("arbitrary",), jobs=[stage1(["w_in"], [g_in])])
    p_in = add(["w_in"], [g_in], s1_in)
    dh1, s2_in = _matmul(dproj, wb["w_in"], "nt", F32, big, 1024, 1024, "proj_bwd_x", jobs=[_rs_chip_job(p_in)])
    grad_x, g["norm1_w"] = _rmsnorm_bwd_add(dx1, dh1, x, ws["norm1_w"], tm, "norm1_bwd", False)
    rs["w_in"] = (p_in[0], s2_in)
    return loss_acc, grad_x.reshape(B, S, D), g, rs


def kernel(x, positions, norm1_w, w_in, merge_gate_b, ret_gn_w, w_ret_o, lru_conv_w, lru_conv_b, lru_w_r, lru_b_r, lru_w_i, lru_b_i, lru_lambda, w_lru_o, w_out, norm2_w, ffn_w_up, ffn_conv_w, ffn_conv_b, ffn_w_down, norm_f_w, loss_target, m_norm1_w, m_w_in, m_merge_gate_b, m_ret_gn_w, m_w_ret_o, m_lru_conv_w, m_lru_conv_b, m_lru_w_r, m_lru_b_r, m_lru_w_i, m_lru_b_i, m_lru_lambda, m_w_lru_o, m_w_out, m_norm2_w, m_ffn_w_up, m_ffn_conv_w, m_ffn_conv_b, m_ffn_w_down, m_norm_f_w, v_norm1_w, v_w_in, v_merge_gate_b, v_ret_gn_w, v_w_ret_o, v_lru_conv_w, v_lru_conv_b, v_lru_w_r, v_lru_b_r, v_lru_w_i, v_lru_b_i, v_lru_lambda, v_w_lru_o, v_w_out, v_norm2_w, v_ffn_w_up, v_ffn_conv_w, v_ffn_conv_b, v_ffn_w_down, v_norm_f_w):
    names = ["norm1_w", "w_in", "merge_gate_b", "ret_gn_w", "w_ret_o", "lru_conv_w", "lru_conv_b", "lru_w_r", "lru_b_r",
             "lru_w_i", "lru_b_i", "lru_lambda", "w_lru_o", "w_out", "norm2_w", "ffn_w_up", "ffn_conv_w", "ffn_conv_b",
             "ffn_w_down", "norm_f_w"]
    w_args = [norm1_w, w_in, merge_gate_b, ret_gn_w, w_ret_o, lru_conv_w, lru_conv_b, lru_w_r, lru_b_r, lru_w_i, lru_b_i,
              lru_lambda, w_lru_o, w_out, norm2_w, ffn_w_up, ffn_conv_w, ffn_conv_b, ffn_w_down, norm_f_w]
    m_args = [m_norm1_w, m_w_in, m_merge_gate_b, m_ret_gn_w, m_w_ret_o, m_lru_conv_w, m_lru_conv_b, m_lru_w_r, m_lru_b_r,
              m_lru_w_i, m_lru_b_i, m_lru_lambda, m_w_lru_o, m_w_out, m_norm2_w, m_ffn_w_up, m_ffn_conv_w, m_ffn_conv_b,
              m_ffn_w_down, m_norm_f_w]
    v_args = [v_norm1_w, v_w_in, v_merge_gate_b, v_ret_gn_w, v_w_ret_o, v_lru_conv_w, v_lru_conv_b, v_lru_w_r, v_lru_b_r,
              v_lru_w_i, v_lru_b_i, v_lru_lambda, v_w_lru_o, v_w_out, v_norm2_w, v_ffn_w_up, v_ffn_conv_w, v_ffn_conv_b,
              v_ffn_w_down, v_norm_f_w]
    orig_shape = {n: a.shape for n, a in zip(names, w_args)}
    local_shape = {n: s for n, s, _ in BIG + SMALL_SHARDED}
    local_shape.update({n: s for n, s in REPLICATED})
    W = {n: a.reshape(local_shape[n]) for n, a in zip(names, w_args)}
    M = {n: a.reshape(local_shape[n]) for n, a in zip(names, m_args)}
    V = {n: a.reshape(local_shape[n]) for n, a in zip(names, v_args)}

    xi, yi, ci = _mesh_pos()
    dev = 4 * xi + 2 * yi + ci
    chip = (2 * xi + yi).astype(jnp.int32).reshape(1)
    core = ci.astype(jnp.int32).reshape(1)

    first_names = [n for n, _, _ in FIRST]
    small_names = [n for n, _, _ in SMALL_SHARDED]
    gathered = _all_gather_multi([W[n].astype(BF16) for n in first_names] + [W[n] for n in small_names],
                                 [m for _, _, m in FIRST + SMALL_SHARDED], "gather_first_weights")
    wb = dict(zip(first_names, gathered[:len(FIRST)]))
    ws = dict(zip(small_names, gathered[len(FIRST):]))
    for n in ("lru_b_r", "lru_b_i"):
        ws[n] = jnp.transpose(ws[n], (1, 0, 2)).reshape(1, LRU_BLOCKS * LRU_BLOCK)
    for n, _ in REPLICATED:
        ws[n] = W[n]

    late_shards = [W[n].astype(BF16) for n, _, _ in LATE]
    loss_acc, grad_x, g, rs = _local_step(x, positions, loss_target, wb, ws, late_shards, core)

    G_out, D_out, M_out, V_out = {}, {}, {}, {}
    for n, _, _ in BIG:
        G_out[n], D_out[n], M_out[n], V_out[n] = _adamw_shard(rs[n][0], rs[n][1], W[n], M[n], V[n], chip, "adamw_" + n)

    rep_names = [n for n, _ in REPLICATED]
    red_names = rep_names + small_names
    red = _small_allreduce([g[n] for n in red_names] + [loss_acc[0:1, :]], "allreduce_small_grads")
    loss = red[-1][0, 0]
    gs = dict(zip(red_names, red[:-1]))
    for n, s, mode in SMALL_SHARDED:
        if mode == "cols":
            gs[n] = lax.dynamic_slice_in_dim(gs[n], dev * s[1], s[1], axis=1)
        else:
            full = gs[n].reshape(LRU_BLOCKS, LRU_BLOCK)
            gs[n] = lax.dynamic_slice_in_dim(full, dev * s[1], s[1], axis=1)
    d2, m2, v2 = _adamw_small([gs[n] for n in red_names], [W[n] for n in red_names], [M[n] for n in red_names],
                              [V[n] for n in red_names], "adamw_small")
    for i, n in enumerate(red_names):
        G_out[n], D_out[n], M_out[n], V_out[n] = gs[n], d2[i], m2[i], v2[i]

    outs = [loss, grad_x]
    for group in (G_out, D_out, M_out, V_out):
        outs += [group[n].reshape(orig_shape[n]) for n in names]
    return tuple(outs)
```

```python
import math

import jax
import jax.numpy as jnp
from jax import lax
from jax.experimental import pallas as pl
from jax.experimental.pallas import tpu as pltpu

F32 = jnp.float32
BF16 = jnp.bfloat16
MESH = pl.DeviceIdType.MESH

D_MODEL = 1024
CHUNK = 64
RET_HEADS = 4
RET_DK = 128
RET_DV = 256
LRU_BLOCKS = 4
LRU_BLOCK = 256
LRU_CONV = 4
LRU_C = 8.0
D_FF = 3072
FFN_CONV = 3
ROPE_BASE = 10000.0
RMS_EPS = 1e-6
GN_EPS = 1e-6
D_IN = 7168
ADAM_LR, ADAM_B1, ADAM_B2, ADAM_EPS, ADAM_WD, ADAM_STEP = 0.001, 0.9, 0.999, 1e-08, 0.01, 10

N_DEV = 8
V7X_VMEM_BYTES = 64 * 1024 * 1024
VMEM_LIMIT = V7X_VMEM_BYTES - 8 * 1024 * 1024
RET_BLOCK = 256
LANES = 128

COL_Q, COL_K = 0, 4
COL_V, COL_G, COL_XL, COL_YL = 4, 8, 12, 16
COL_GR, COL_GL = 5, 6

HBM_SPEC = pl.BlockSpec(memory_space=pl.ANY)


def _gelu(x):
    c = math.sqrt(2.0 / math.pi)
    t = jnp.tanh(c * (x + 0.044715 * x * x * x))
    return 0.5 * x * (1.0 + t)


def _gelu_and_grad(x):
    c = math.sqrt(2.0 / math.pi)
    x2 = x * x
    t = jnp.tanh(c * (x + 0.044715 * x2 * x))
    g = 0.5 * x * (1.0 + t)
    dg = 0.5 * (1.0 + t) + 0.5 * x * (1.0 - t * t) * c * (1.0 + 3.0 * 0.044715 * x2)
    return g, dg


def _sigmoid(x):
    return 1.0 / (1.0 + jnp.exp(-x))


def _shift_down(x, s, fill):
    r = pltpu.roll(x, s, 0)
    rows = lax.broadcasted_iota(jnp.int32, x.shape, 0)
    return jnp.where(rows >= s, r, fill)


def _shift_up(x, s, fill):
    n = x.shape[0]
    r = pltpu.roll(x, n - s, 0)
    rows = lax.broadcasted_iota(jnp.int32, x.shape, 0)
    return jnp.where(rows < n - s, r, fill)


def _dot(a, b, dims):
    return lax.dot_general(a, b, (dims, ((), ())), preferred_element_type=F32)


NN = ((1,), (0,))
NT = ((1,), (1,))
TN = ((0,), (0,))


def _mesh_pos():
    return lax.axis_index("x"), lax.axis_index("y"), lax.axis_index("c")


def _other_chips(x, y):
    return [(1 - x, y), (x, 1 - y), (1 - x, 1 - y)]


def _full_shape(shard_shape, mode):
    if mode == "rows":
        return (N_DEV * shard_shape[0],) + tuple(shard_shape[1:])
    if mode == "cols":
        return (shard_shape[0], N_DEV * shard_shape[1])
    if mode == "mid":
        return (shard_shape[0], N_DEV * shard_shape[1], shard_shape[2])
    return (N_DEV,) + tuple(shard_shape)


def _extent(shard_shape, mode):
    return {"rows": shard_shape[0], "cols": shard_shape[1], "mid": shard_shape[1], "stack": 1}[mode]


def _window(ref, mode, extent, d):
    if mode == "stack":
        return ref.at[d]
    start = pl.multiple_of(d * extent, extent)
    if mode == "rows":
        return ref.at[pl.ds(start, extent)]
    if mode == "cols":
        return ref.at[:, pl.ds(start, extent)]
    return ref.at[:, pl.ds(start, extent), :]


class _Job:
    def __init__(self, inputs, out_shapes, sems, start, finish, aliases=None):
        self.inputs, self.out_shapes, self.sems = list(inputs), list(out_shapes), sems
        self.start, self.finish, self.aliases = start, finish, dict(aliases or {})


def _remote(src, dst, send_sem, recv_sem, to):
    return pltpu.make_async_remote_copy(src_ref=src, dst_ref=dst, send_sem=send_sem, recv_sem=recv_sem,
                                        device_id=to, device_id_type=MESH)


def _ag_first_job(shards, modes):
    n = len(shards)
    extents = [_extent(s.shape, m) for s, m in zip(shards, modes)]

    def copies(x_refs, out_refs, send, recv, local, arriving):
        x, y, c = _mesh_pos()
        peers = [(x, y, 1 - c)] + [(*chip, c) for chip in _other_chips(x, y)]
        win = lambda i, p: _window(out_refs[i], modes[i], extents[i], 4 * p[0] + 2 * p[1] + p[2])
        if arriving:
            return [_remote(x_refs[i], win(i, p), send.at[i, k], recv.at[i, k], p)
                    for i in range(n) for k, p in enumerate(peers)]
        mine = [pltpu.make_async_copy(x_refs[i], win(i, (x, y, c)), local.at[i]) for i in range(n)]
        sends = [_remote(x_refs[i], win(i, (x, y, c)), send.at[i, k], recv.at[i, k], p)
                 for i in range(n) for k, p in enumerate(peers)]
        return mine, sends

    def start(*refs):
        mine, sends = copies(*refs, False)
        for cp in mine + sends:
            cp.start()

    def finish(*refs):
        for cp in copies(*refs, True):
            cp.wait_recv()
        mine, sends = copies(*refs, False)
        for cp in sends:
            cp.wait_send()
        for cp in mine:
            cp.wait()

    out_shapes = [jax.ShapeDtypeStruct(_full_shape(s.shape, m), s.dtype) for s, m in zip(shards, modes)]
    return _Job(shards, out_shapes, ((n, 4), (n, 4), (n,)), start, finish)


def _ag_second_job(fulls, modes, shard_shapes):
    n = len(fulls)
    extents = [_extent(s, m) for s, m in zip(shard_shapes, modes)]

    def copies(_, out_refs, send, recv, local, core_of_block):
        x, y, c = _mesh_pos()
        pc = c if core_of_block == "mine" else 1 - c
        win = lambda i, chip: _window(out_refs[i], modes[i], extents[i], 4 * chip[0] + 2 * chip[1] + pc)
        return [_remote(win(i, chip), win(i, chip), send.at[i, j], recv.at[i, j], (x, y, 1 - c))
                for i in range(n) for j, chip in enumerate(_other_chips(x, y))]

    def start(*refs):
        for cp in copies(*refs, "mine"):
            cp.start()

    def finish(*refs):
        for cp in copies(*refs, "sibling"):
            cp.wait_recv()
        for cp in copies(*refs, "mine"):
            cp.wait_send()

    out_shapes = [jax.ShapeDtypeStruct(f.shape, f.dtype) for f in fulls]
    return _Job(fulls, out_shapes, ((n, 3), (n, 3), (1,)), start, finish, aliases={i: i for i in range(n)})


def _rs_sibling_job(grads, modes, shard_shapes):
    n = len(grads)
    extents = [_extent(s, m) for s, m in zip(shard_shapes, modes)]

    def copies(g_refs, out_refs, send, recv, local):
        x, y, c = _mesh_pos()
        return [_remote(_window(g_refs[i], modes[i], extents[i], 2 * k + (1 - c)), out_refs[i].at[k],
                        send.at[i, k], recv.at[i, k], (x, y, 1 - c))
                for i in range(n) for k in range(4)]

    def start(*refs):
        for cp in copies(*refs):
            cp.start()

    def finish(*refs):
        cps = copies(*refs)
        for cp in cps:
            cp.wait_recv()
        for cp in cps:
            cp.wait_send()

    out_shapes = [jax.ShapeDtypeStruct((4,) + tuple(s), g.dtype) for s, g in zip(shard_shapes, grads)]
    return _Job(grads, out_shapes, ((n, 4), (n, 4), (1,)), start, finish)


def _rs_chip_job(partials):
    n = len(partials)

    def copies(p_refs, out_refs, send, recv, local):
        x, y, c = _mesh_pos()
        return [_remote(p_refs[i].at[2 * px + py], out_refs[i].at[j], send.at[i, j], recv.at[i, j], (px, py, c))
                for i in range(n) for j, (px, py) in enumerate(_other_chips(x, y))]

    def start(*refs):
        for cp in copies(*refs):
            cp.start()

    def finish(*refs):
        cps = copies(*refs)
        for cp in cps:
            cp.wait_recv()
        for cp in cps:
            cp.wait_send()

    out_shapes = [jax.ShapeDtypeStruct((3,) + tuple(p.shape[1:]), p.dtype) for p in partials]
    return _Job(partials, out_shapes, ((n, 3), (n, 3), (1,)), start, finish)


def _all_true(conds):
    out = conds[0]
    for c in conds[1:]:
        out = jnp.logical_and(out, c)
    return out


def _pcall(body, *, name, grid, in_specs, out_specs, out_shape, args, sem, scratch=(), jobs=(), alias_in_out=None):
    n_in, n_out, n_scr = len(args), len(out_shape), len(scratch)
    job_in = [a for j in jobs for a in j.inputs]
    job_out = [s for j in jobs for s in j.out_shapes]
    job_sems = [pltpu.SemaphoreType.DMA(shape) for j in jobs for shape in j.sems]
    aliases, in_off, out_off = dict(alias_in_out or {}), n_in, n_out
    for j in jobs:
        for a, b in j.aliases.items():
            aliases[in_off + a] = out_off + b
        in_off += len(j.inputs)
        out_off += len(j.out_shapes)

    def wrapped(*refs):
        ins = refs[:n_in]
        jins = refs[n_in:n_in + len(job_in)]
        o0 = n_in + len(job_in)
        outs = refs[o0:o0 + n_out]
        jouts = refs[o0 + n_out:o0 + n_out + len(job_out)]
        s0 = o0 + n_out + len(job_out)
        scr = refs[s0:s0 + n_scr]
        jsems = refs[s0 + n_scr:]
        if jobs:
            ids = [pl.program_id(a) for a in range(len(grid))]
            first = _all_true([i == 0 for i in ids])
            last = _all_true([i == g - 1 for i, g in zip(ids, grid)])

            def per_job(which):
                i0 = o0_ = 0
                for k, j in enumerate(jobs):
                    fn = j.start if which == "start" else j.finish
                    fn(jins[i0:i0 + len(j.inputs)], jouts[o0_:o0_ + len(j.out_shapes)], *jsems[3 * k:3 * k + 3])
                    i0 += len(j.inputs)
                    o0_ += len(j.out_shapes)

            @pl.when(first)
            def _():
                per_job("start")

        body(*ins, *outs, *scr)
        if jobs:
            @pl.when(last)
            def _():
                per_job("finish")

    semantics = tuple("arbitrary" for _ in grid) if jobs else sem
    return pl.pallas_call(
        wrapped, name=name, grid=grid,
        in_specs=list(in_specs) + [HBM_SPEC] * len(job_in),
        out_specs=list(out_specs) + [HBM_SPEC] * len(job_out),
        out_shape=list(out_shape) + job_out,
        scratch_shapes=list(scratch) + job_sems,
        input_output_aliases=aliases,
        compiler_params=pltpu.CompilerParams(dimension_semantics=semantics, vmem_limit_bytes=VMEM_LIMIT),
    )(*args, *job_in)


def _row_tile(rows, cap):
    if rows <= cap:
        return rows
    best = None
    for t in range(16, cap + 1, 16):
        if rows % t == 0:
            best = t
    assert best is not None
    return best


def _matmul(a, b, mode, out_dtype, tm, tn, tk, name, add=None, jobs=()):
    if mode == "tn":
        K, M = a.shape
    else:
        M, K = a.shape
    N = b.shape[0] if mode == "nt" else b.shape[1]
    tm, tn, tk = min(tm, M), min(tn, N), min(tk, K)
    assert M % tm == 0 and N % tn == 0 and K % tk == 0
    nk = K // tk
    dims = {"nn": NN, "nt": NT, "tn": TN}[mode]

    def body(*refs):
        if add is None:
            a_ref, b_ref, o_ref, acc = refs
            add_ref = None
        else:
            a_ref, b_ref, add_ref, o_ref, acc = refs
        k = pl.program_id(2)
        p = _dot(a_ref[...], b_ref[...], dims)

        def finish(r):
            if add_ref is not None:
                r = r + add_ref[...].astype(F32)
            o_ref[...] = r.astype(out_dtype)

        if nk == 1:
            finish(p)
        else:
            @pl.when(k == 0)
            def _():
                acc[...] = p

            @pl.when(k > 0)
            def _():
                acc[...] += p

            @pl.when(k == nk - 1)
            def _():
                finish(acc[...])

    if mode == "tn":
        a_spec = pl.BlockSpec((tk, tm), lambda i, j, k: (k, i))
    else:
        a_spec = pl.BlockSpec((tm, tk), lambda i, j, k: (i, k))
    if mode == "nt":
        b_spec = pl.BlockSpec((tn, tk), lambda i, j, k: (j, k))
    else:
        b_spec = pl.BlockSpec((tk, tn), lambda i, j, k: (k, j))
    in_specs = [a_spec, b_spec]
    args = [a, b]
    if add is not None:
        in_specs.append(pl.BlockSpec((tm, tn), lambda i, j, k: (i, j)))
        args.append(add)
    return _pcall(
        body, name=name, grid=(M // tm, N // tn, nk), in_specs=in_specs,
        out_specs=[pl.BlockSpec((tm, tn), lambda i, j, k: (i, j))],
        out_shape=[jax.ShapeDtypeStruct((M, N), out_dtype)], args=args,
        scratch=[pltpu.VMEM((tm, tn) if nk > 1 else (8, LANES), F32)],
        sem=("parallel", "parallel", "arbitrary"), jobs=jobs)


def _rmsnorm_fwd(x, w, tm, name):
    T, D = x.shape

    def body(x_ref, w_ref, h_ref):
        xv = x_ref[...]
        r = lax.rsqrt(jnp.mean(xv * xv, axis=-1, keepdims=True) + RMS_EPS)
        h_ref[...] = (xv * r * w_ref[...]).astype(BF16)

    return _pcall(
        body, name=name, grid=(T // tm,),
        in_specs=[pl.BlockSpec((tm, D), lambda i: (i, 0)), pl.BlockSpec((1, D), lambda i: (0, 0))],
        out_specs=[pl.BlockSpec((tm, D), lambda i: (i, 0))],
        out_shape=[jax.ShapeDtypeStruct((T, D), BF16)], args=[x, w], sem=("parallel",))[0]


def _rmsnorm_bwd_add(dres, dh, x, w, tm, name, want_bf16, jobs=()):
    T, D = x.shape

    def body(dres_ref, dh_ref, x_ref, w_ref, *outs):
        if want_bf16:
            dx_ref, dxb_ref, dw_ref = outs
        else:
            dx_ref, dw_ref = outs
        i = pl.program_id(0)
        xv = x_ref[...]
        r = lax.rsqrt(jnp.mean(xv * xv, axis=-1, keepdims=True) + RMS_EPS)
        xh = xv * r
        dh_v = dh_ref[...].astype(F32)
        dxh = dh_v * w_ref[...]
        dx = dres_ref[...] + r * (dxh - xh * jnp.mean(dxh * xh, axis=-1, keepdims=True))
        dx_ref[...] = dx
        if want_bf16:
            dxb_ref[...] = dx.astype(BF16)
        part = jnp.sum(dh_v * xh, axis=0, keepdims=True)

        @pl.when(i == 0)
        def _():
            dw_ref[...] = part

        @pl.when(i > 0)
        def _():
            dw_ref[...] += part

    tile = pl.BlockSpec((tm, D), lambda i: (i, 0))
    row = pl.BlockSpec((1, D), lambda i: (0, 0))
    out_specs = [tile] + ([tile] if want_bf16 else []) + [row]
    out_shape = ([jax.ShapeDtypeStruct((T, D), F32)] + ([jax.ShapeDtypeStruct((T, D), BF16)] if want_bf16 else [])
                 + [jax.ShapeDtypeStruct((1, D), F32)])
    return _pcall(body, name=name, grid=(T // tm,), in_specs=[tile, tile, tile, row], out_specs=out_specs,
                  out_shape=out_shape, args=[dres, dh, x, w], sem=("arbitrary",), jobs=jobs)


def _loss_head(x2, target, wf, tm, name):
    T, D = x2.shape

    def body(x_ref, t_ref, w_ref, dx_ref, dxb_ref, loss_ref, dw_ref):
        i = pl.program_id(0)
        xv = x_ref[...]
        r = lax.rsqrt(jnp.mean(xv * xv, axis=-1, keepdims=True) + RMS_EPS)
        xh = xv * r
        wv = w_ref[...]
        e = xh * wv - t_ref[...]
        lpart = 0.5 * jnp.sum(jnp.sum(e * e, axis=-1, keepdims=True), axis=0, keepdims=True) * (1.0 / D)
        dy = e * (1.0 / D)
        dxh = dy * wv
        dx = r * (dxh - xh * jnp.mean(dxh * xh, axis=-1, keepdims=True))
        dx_ref[...] = dx
        dxb_ref[...] = dx.astype(BF16)
        wpart = jnp.sum(dy * xh, axis=0, keepdims=True)
        lfull = jnp.broadcast_to(lpart, (8, LANES))

        @pl.when(i == 0)
        def _():
            loss_ref[...] = lfull
            dw_ref[...] = wpart

        @pl.when(i > 0)
        def _():
            loss_ref[...] += lfull
            dw_ref[...] += wpart

    tile = pl.BlockSpec((tm, D), lambda i: (i, 0))
    row = pl.BlockSpec((1, D), lambda i: (0, 0))
    return _pcall(
        body, name=name, grid=(T // tm,), in_specs=[tile, tile, row],
        out_specs=[tile, tile, pl.BlockSpec((8, LANES), lambda i: (0, 0)), row],
        out_shape=[jax.ShapeDtypeStruct((T, D), F32), jax.ShapeDtypeStruct((T, D), BF16),
                   jax.ShapeDtypeStruct((8, LANES), F32), jax.ShapeDtypeStruct((1, D), F32)],
        args=[x2, target, wf], sem=("arbitrary",))


def _rope_tables(pos_col, inv2, tm, name):
    T = pos_col.shape[0]

    def body(p_ref, f_ref, c_ref, s_ref):
        ang = p_ref[...] * f_ref[...]
        lane = lax.broadcasted_iota(jnp.int32, ang.shape, 1)
        c_ref[...] = jnp.cos(ang)
        s_ref[...] = jnp.where(lane < RET_DK // 2, -1.0, 1.0) * jnp.sin(ang)

    tile = pl.BlockSpec((tm, RET_DK), lambda i: (i, 0))
    return _pcall(
        body, name=name, grid=(T // tm,),
        in_specs=[pl.BlockSpec((tm, 1), lambda i: (i, 0)), pl.BlockSpec((1, RET_DK), lambda i: (0, 0))],
        out_specs=[tile, tile], out_shape=[jax.ShapeDtypeStruct((T, RET_DK), F32)] * 2,
        args=[pos_col, inv2], sem=("parallel",))


def _mix_fwd(a_in, b_in, proj, x, w_ro, w_lo, w_out, mb, w2, tm, name):
    T, D = x.shape

    def body(a_ref, b_ref, gr_ref, gl_ref, x_ref, wro_ref, wlo_ref, wout_ref, mb_ref, w2_ref,
             x1_ref, mix_ref, h2_ref):
        ya = _dot(a_ref[...], wro_ref[...], NN)
        yb = _dot(b_ref[...], wlo_ref[...], NN)
        sa = _sigmoid(gr_ref[...].astype(F32) + mb_ref[0:1, :])
        sb = _sigmoid(gl_ref[...].astype(F32) + mb_ref[1:2, :])
        mix = (sa * ya + sb * yb).astype(BF16)
        mix_ref[...] = mix
        x1 = x_ref[...] + _dot(mix, wout_ref[...], NN)
        x1_ref[...] = x1
        r = lax.rsqrt(jnp.mean(x1 * x1, axis=-1, keepdims=True) + RMS_EPS)
        h2_ref[...] = (x1 * r * w2_ref[...]).astype(BF16)

    tile = pl.BlockSpec((tm, D), lambda i: (i, 0))
    wspec = pl.BlockSpec((D, D), lambda i: (0, 0))
    return _pcall(
        body, name=name, grid=(T // tm,),
        in_specs=[tile, tile,
                  pl.BlockSpec((tm, D), lambda i: (i, COL_GR)), pl.BlockSpec((tm, D), lambda i: (i, COL_GL)),
                  tile, wspec, wspec, wspec,
                  pl.BlockSpec((2, D), lambda i: (0, 0)), pl.BlockSpec((1, D), lambda i: (0, 0))],
        out_specs=[tile, tile, tile],
        out_shape=[jax.ShapeDtypeStruct((T, D), F32), jax.ShapeDtypeStruct((T, D), BF16),
                   jax.ShapeDtypeStruct((T, D), BF16)],
        args=[a_in, b_in, proj, proj, x, w_ro, w_lo, w_out, mb, w2], sem=("parallel",))


def _write_pieces(dst_ref, sems, stashes, row0, col0s, ids, grid, compute):
    def aligned(v, m):
        return v if isinstance(v, int) else pl.multiple_of(v, m)

    def copies():
        return [pltpu.make_async_copy(
                    st, dst_ref.at[pl.ds(aligned(row0, 16), st.shape[0]), pl.ds(aligned(c0, LANES), st.shape[1])],
                    sems.at[k])
                for k, (st, c0) in enumerate(zip(stashes, col0s))]

    first = _all_true([i == 0 for i in ids])
    last = _all_true([i == g - 1 for i, g in zip(ids, grid)])

    @pl.when(jnp.logical_not(first))
    def _():
        for cp in copies():
            cp.wait()

    compute()
    for cp in copies():
        cp.start()

    @pl.when(last)
    def _():
        for cp in copies():
            cp.wait()


def _mix_bwd(dx1b, a_in, b_in, proj, w_ro, w_lo, w_out, mb, tm, name, jobs=()):
    T, D = a_in.shape
    grid = (T // tm,)

    def body(dx_ref, a_ref, b_ref, gr_ref, gl_ref, wro_ref, wlo_ref, wout_ref, mb_ref,
             da_ref, db_ref, dya_ref, dyb_ref, dp_ref, dmb_ref, dgr_s, dgl_s, wsem):
        i = pl.program_id(0)

        def compute():
            dmix = _dot(dx_ref[...], wout_ref[...], NT)
            ya = _dot(a_ref[...], wro_ref[...], NN)
            yb = _dot(b_ref[...], wlo_ref[...], NN)
            sa = _sigmoid(gr_ref[...].astype(F32) + mb_ref[0:1, :])
            sb = _sigmoid(gl_ref[...].astype(F32) + mb_ref[1:2, :])
            dya = (dmix * sa).astype(BF16)
            dyb = (dmix * sb).astype(BF16)
            dgr = dmix * ya * sa * (1.0 - sa)
            dgl = dmix * yb * sb * (1.0 - sb)
            dya_ref[...] = dya
            dyb_ref[...] = dyb
            dgr_s[...] = dgr.astype(BF16)
            dgl_s[...] = dgl.astype(BF16)
            da_ref[...] = _dot(dya, wro_ref[...], NT).astype(BF16)
            db_ref[...] = _dot(dyb, wlo_ref[...], NT).astype(BF16)

            @pl.when(i == 0)
            def _():
                dmb_ref[...] = jnp.zeros_like(dmb_ref)

            dmb_ref[0:1, :] += jnp.sum(dgr, axis=0, keepdims=True)
            dmb_ref[1:2, :] += jnp.sum(dgl, axis=0, keepdims=True)

        _write_pieces(dp_ref, wsem, [dgr_s, dgl_s], i * tm, [COL_GR * D, COL_GL * D], [i], grid, compute)

    tile = pl.BlockSpec((tm, D), lambda i: (i, 0))
    wspec = pl.BlockSpec((D, D), lambda i: (0, 0))
    two = pl.BlockSpec((2, D), lambda i: (0, 0))
    return _pcall(
        body, name=name, grid=grid,
        in_specs=[tile, tile, tile,
                  pl.BlockSpec((tm, D), lambda i: (i, COL_GR)), pl.BlockSpec((tm, D), lambda i: (i, COL_GL)),
                  wspec, wspec, wspec, two],
        out_specs=[tile] * 4 + [HBM_SPEC, two],
        out_shape=[jax.ShapeDtypeStruct((T, D), BF16)] * 4
                  + [jax.ShapeDtypeStruct((T, D_IN), BF16), jax.ShapeDtypeStruct((2, D), F32)],
        args=[dx1b, a_in, b_in, proj, proj, w_ro, w_lo, w_out, mb],
        scratch=[pltpu.VMEM((tm, D), BF16), pltpu.VMEM((tm, D), BF16), pltpu.SemaphoreType.DMA((2,))],
        sem=("arbitrary",), jobs=jobs)


def _ret_decay_consts(lg):
    L = RET_BLOCK
    n = lax.broadcasted_iota(jnp.int32, (L, L), 0)
    m = lax.broadcasted_iota(jnp.int32, (L, L), 1)
    cn, cm = n // CHUNK, m // CHUNK
    expo = jnp.where(cn == cm, jnp.abs(n - m), n - m).astype(F32)
    wm = jnp.where(cm <= cn, jnp.exp(lg * expo), 0.0)
    idx = lax.broadcasted_iota(jnp.int32, (L, 1), 0).astype(F32)
    qd = jnp.exp(lg * (idx + 1.0))
    kd = jnp.exp(lg * (L - 1.0 - idx))
    bd = jnp.exp(lg * float(L))
    return wm, qd, kd, bd


def _rotate(v, cos2, sin2s):
    return v * cos2 + pltpu.roll(v, RET_DK // 2, 1) * sin2s


def _rotate_t(d, cos2, sin2s):
    return d * cos2 - pltpu.roll(d, RET_DK // 2, 1) * sin2s


def _retention_fwd(proj, cos2, sin2s, lgam, gn_w, B, S, name, jobs=()):
    T = B * S
    nb = S // RET_BLOCK
    scale = RET_DK ** -0.5

    def body(q_ref, k_ref, v_ref, g_ref, c_ref, s_ref, lg_ref, gw_ref, o_ref, a_ref, qr, kr, st):
        wm, qd, kd, bd = _ret_decay_consts(lg_ref[0:1, 0:1])
        cos2, sin2s = c_ref[...], s_ref[...]
        qr[...] = _rotate(q_ref[...].astype(F32), cos2, sin2s)
        kr[...] = _rotate(k_ref[...].astype(F32), cos2, sin2s) * scale
        st[...] = jnp.zeros_like(st)
        gw = gw_ref[...]
        for j in range(nb):
            rows = pl.ds(j * RET_BLOCK, RET_BLOCK)
            qb = qr[rows, :]
            kb = kr[rows, :]
            vb = v_ref[rows, :].astype(BF16)
            sc = _dot(qb.astype(BF16), kb.astype(BF16), NT) * wm
            o = _dot(sc.astype(BF16), vb, NN) + _dot((qb * qd).astype(BF16), st[...].astype(BF16), NN)
            st[...] = st[...] * bd + _dot((kb * kd).astype(BF16), vb, TN)
            o_ref[rows, :] = o
            mu = jnp.mean(o, axis=-1, keepdims=True)
            oc = o - mu
            var = jnp.mean(oc * oc, axis=-1, keepdims=True)
            y = oc * lax.rsqrt(var + GN_EPS) * gw
            g = g_ref[rows, :].astype(F32)
            a_ref[rows, :] = (y * (g * _sigmoid(g))).astype(BF16)

    blk = lambda w, off: pl.BlockSpec((S, w), lambda b, h: (b, off + h))
    return _pcall(
        body, name=name, grid=(B, RET_HEADS),
        in_specs=[blk(RET_DK, COL_Q), blk(RET_DK, COL_K), blk(RET_DV, COL_V), blk(RET_DV, COL_G),
                  pl.BlockSpec((S, RET_DK), lambda b, h: (b, 0)), pl.BlockSpec((S, RET_DK), lambda b, h: (b, 0)),
                  pl.BlockSpec((None, 8, LANES), lambda b, h: (h, 0, 0)),
                  pl.BlockSpec((1, RET_DV), lambda b, h: (0, h))],
        out_specs=[blk(RET_DV, 0), blk(RET_DV, 0)],
        out_shape=[jax.ShapeDtypeStruct((T, RET_HEADS * RET_DV), F32),
                   jax.ShapeDtypeStruct((T, RET_HEADS * RET_DV), BF16)],
        args=[proj, proj, proj, proj, cos2, sin2s, lgam, gn_w],
        scratch=[pltpu.VMEM((S, RET_DK), F32), pltpu.VMEM((S, RET_DK), F32), pltpu.VMEM((RET_DK, RET_DV), F32)],
        sem=("parallel", "parallel"), jobs=jobs)


def _retention_bwd(da_in, o, proj, dproj, cos2, sin2s, lgam, gn_w, B, S, name, jobs=()):
    T = B * S
    nb = S // RET_BLOCK
    scale = RET_DK ** -0.5
    grid = (RET_HEADS, B)

    def body(da_ref, o_ref, q_ref, k_ref, v_ref, g_ref, c_ref, s_ref, lg_ref, gw_ref, _, dp_ref, dgw_ref,
             qr, kr, do_s, sts, rst, dq_s, dk_s, dv_s, dg_s, wsem):
        h, b = pl.program_id(0), pl.program_id(1)

        def compute():
            wm, qd, kd, bd = _ret_decay_consts(lg_ref[0:1, 0:1])
            cos2, sin2s = c_ref[...], s_ref[...]
            qr[...] = _rotate(q_ref[...].astype(F32), cos2, sin2s)
            kr[...] = _rotate(k_ref[...].astype(F32), cos2, sin2s) * scale
            gw = gw_ref[...]
            st = jnp.zeros((RET_DK, RET_DV), F32)
            dgw = jnp.zeros((1, RET_DV), F32)
            for j in range(nb):
                rows = pl.ds(j * RET_BLOCK, RET_BLOCK)
                ov = o_ref[rows, :]
                mu = jnp.mean(ov, axis=-1, keepdims=True)
                oc = ov - mu
                rstd = lax.rsqrt(jnp.mean(oc * oc, axis=-1, keepdims=True) + GN_EPS)
                y = oc * rstd
                g = g_ref[rows, :].astype(F32)
                sg = _sigmoid(g)
                da = da_ref[rows, :].astype(F32)
                dg_s[rows, :] = (da * (y * gw) * (sg * (1.0 + g * (1.0 - sg)))).astype(BF16)
                dyw = da * (g * sg)
                dgw = dgw + jnp.sum(dyw * y, axis=0, keepdims=True)
                dy = dyw * gw
                do_s[rows, :] = rstd * (dy - jnp.mean(dy, axis=-1, keepdims=True)
                                        - y * jnp.mean(dy * y, axis=-1, keepdims=True))
                sts[j] = st
                st = st * bd + _dot((kr[rows, :] * kd).astype(BF16), v_ref[rows, :].astype(BF16), TN)

            @pl.when(b == 0)
            def _():
                dgw_ref[...] = dgw

            @pl.when(b > 0)
            def _():
                dgw_ref[...] += dgw

            rst[...] = jnp.zeros_like(rst)
            for j in reversed(range(nb)):
                rows = pl.ds(j * RET_BLOCK, RET_BLOCK)
                qb = qr[rows, :]
                kb = kr[rows, :]
                qbb, kbb = qb.astype(BF16), kb.astype(BF16)
                vb = v_ref[rows, :].astype(BF16)
                dob = do_s[rows, :]
                dobb = dob.astype(BF16)
                a_m = (_dot(qbb, kbb, NT) * wm).astype(BF16)
                b_m = (_dot(dobb, vb, NT) * wm).astype(BF16)
                rb = rst[...].astype(BF16)
                dq = _dot(b_m, kbb, NN) + _dot((dob * qd).astype(BF16), sts[j].astype(BF16), NT)
                dk = _dot(b_m, qbb, TN) + kd * _dot(vb, rb, NT)
                dv = _dot(a_m, dobb, TN) + kd * _dot(kbb, rb, NN)
                rst[...] = rst[...] * bd + _dot((qb * qd).astype(BF16), dobb, TN)
                cb, sb = c_ref[rows, :], s_ref[rows, :]
                dq_s[rows, :] = _rotate_t(dq, cb, sb).astype(BF16)
                dk_s[rows, :] = _rotate_t(dk * scale, cb, sb).astype(BF16)
                dv_s[rows, :] = dv.astype(BF16)

        cols = [(COL_Q + h) * RET_DK, (COL_K + h) * RET_DK, (COL_V + h) * RET_DV, (COL_G + h) * RET_DV]
        _write_pieces(dp_ref, wsem, [dq_s, dk_s, dv_s, dg_s], b * S, cols, [h, b], grid, compute)

    blk = lambda w, off: pl.BlockSpec((S, w), lambda h, b: (b, off + h))
    return _pcall(
        body, name=name, grid=grid,
        in_specs=[blk(RET_DV, 0), blk(RET_DV, 0),
                  blk(RET_DK, COL_Q), blk(RET_DK, COL_K), blk(RET_DV, COL_V), blk(RET_DV, COL_G),
                  pl.BlockSpec((S, RET_DK), lambda h, b: (b, 0)), pl.BlockSpec((S, RET_DK), lambda h, b: (b, 0)),
                  pl.BlockSpec((None, 8, LANES), lambda h, b: (h, 0, 0)),
                  pl.BlockSpec((1, RET_DV), lambda h, b: (0, h)), HBM_SPEC],
        out_specs=[HBM_SPEC, pl.BlockSpec((1, RET_DV), lambda h, b: (0, h))],
        out_shape=[jax.ShapeDtypeStruct(dproj.shape, dproj.dtype),
                   jax.ShapeDtypeStruct((1, RET_HEADS * RET_DV), F32)],
        args=[da_in, o, proj, proj, proj, proj, cos2, sin2s, lgam, gn_w, dproj],
        scratch=[pltpu.VMEM((S, RET_DK), F32), pltpu.VMEM((S, RET_DK), F32),
                 pltpu.VMEM((S, RET_DV), F32), pltpu.VMEM((nb, RET_DK, RET_DV), F32),
                 pltpu.VMEM((RET_DK, RET_DV), F32),
                 pltpu.VMEM((S, RET_DK), BF16), pltpu.VMEM((S, RET_DK), BF16),
                 pltpu.VMEM((S, RET_DV), BF16), pltpu.VMEM((S, RET_DV), BF16), pltpu.SemaphoreType.DMA((4,))],
        sem=("arbitrary", "arbitrary"), jobs=jobs, alias_in_out={10: 0})


def _lru_gates(x, cw, cb, wr, wi, br, bi, lam):
    xc = cb + cw[LRU_CONV - 1:LRU_CONV, :] * x
    for j in range(LRU_CONV - 1):
        xc = xc + cw[j:j + 1, :] * _shift_down(x, LRU_CONV - 1 - j, 0.0)
    xcb = xc.astype(BF16)
    r = _sigmoid(_dot(xcb, wr, NN) + br)
    ig = _sigmoid(_dot(xcb, wi, NN) + bi)
    z = -lam
    sp = jnp.maximum(z, 0.0) + jnp.log1p(jnp.exp(-jnp.abs(z)))
    log_a = (-LRU_C) * r * sp
    a = jnp.exp(log_a)
    z2 = 2.0 * log_a
    taylor = -z2 * (1.0 + z2 * (0.5 + z2 * (1.0 / 6.0 + z2 * (1.0 / 24.0 + z2 * (1.0 / 120.0)))))
    om = jnp.where(z2 > -0.05, taylor, 1.0 - jnp.exp(z2))
    sq = jnp.sqrt(om)
    return xc, xcb, r, ig, sp, a, sq


def _lru_fwd(proj, cw, cb, wr, wi, br, bi, lam, B, S, name):
    T = B * S
    W = LRU_BLOCKS * LRU_BLOCK

    def body(x_ref, y_ref, cw_ref, cb_ref, wr_ref, wi_ref, br_ref, bi_ref, lam_ref, h_ref, bin_ref):
        xc, _, _, ig, _, a, sq = _lru_gates(x_ref[...].astype(F32), cw_ref[...], cb_ref[...], wr_ref[...], wi_ref[...],
                                           br_ref[...], bi_ref[...], lam_ref[...])
        bv = sq * ig * xc
        s = 1
        while s < S:
            bv = a * _shift_down(bv, s, 0.0) + bv
            if 2 * s < S:
                a = a * _shift_down(a, s, 1.0)
            s *= 2
        h_ref[...] = bv
        bin_ref[...] = (bv * _gelu(y_ref[...].astype(F32))).astype(BF16)

    blk = lambda off: pl.BlockSpec((S, LRU_BLOCK), lambda b, n: (b, off + n))
    vec = lambda rows: pl.BlockSpec((rows, LRU_BLOCK), lambda b, n: (0, n))
    wspec = pl.BlockSpec((None, LRU_BLOCK, LRU_BLOCK), lambda b, n: (n, 0, 0))
    return _pcall(
        body, name=name, grid=(B, LRU_BLOCKS),
        in_specs=[blk(COL_XL), blk(COL_YL), vec(LRU_CONV), vec(1), wspec, wspec, vec(1), vec(1), vec(1)],
        out_specs=[blk(0), blk(0)],
        out_shape=[jax.ShapeDtypeStruct((T, W), F32), jax.ShapeDtypeStruct((T, W), BF16)],
        args=[proj, proj, cw, cb, wr, wi, br, bi, lam], sem=("parallel", "parallel"))


def _lru_bwd(db_in, h, proj, dproj, cw, cb, wr, wi, br, bi, lam, B, S, name, jobs=()):
    T = B * S
    W = LRU_BLOCKS * LRU_BLOCK

    grid = (LRU_BLOCKS, B)

    def body(dbin_ref, h_ref, x_ref, y_ref, cw_ref, cb_ref, wr_ref, wi_ref, br_ref, bi_ref, lam_ref, _,
             dp_ref, dcw_ref, dcb_ref, dwr_ref, dwi_ref, dbr_ref, dbi_ref, dlam_ref, dx_s, dy_s, wsem):
        n, b = pl.program_id(0), pl.program_id(1)

        def compute():
            x = x_ref[...].astype(F32)
            cw = cw_ref[...]
            wr, wi = wr_ref[...], wi_ref[...]
            lam = lam_ref[...]
            xc, xcb, r, ig, sp, a, sq = _lru_gates(x, cw, cb_ref[...], wr, wi, br_ref[...], bi_ref[...], lam)
            hv = h_ref[...]
            gel, dgel = _gelu_and_grad(y_ref[...].astype(F32))
            dbin = dbin_ref[...].astype(F32)
            dy_s[...] = (dbin * hv * dgel).astype(BF16)
            dh = dbin * gel
            an = _shift_up(a, 1, 0.0)
            s = 1
            while s < S:
                dh = dh + an * _shift_up(dh, s, 0.0)
                if 2 * s < S:
                    an = an * _shift_up(an, s, 1.0)
                s *= 2
            hprev = _shift_down(hv, 1, 0.0)
            d_ig = dh * sq * xc
            d_xc = dh * sq * ig
            a2 = a * a
            d_loga = dh * hprev * a - dh * ig * xc * a2 / sq
            d_r = d_loga * ((-LRU_C) * sp)
            d_sp = jnp.sum(d_loga * ((-LRU_C) * r), axis=0, keepdims=True)
            dlam = -d_sp * _sigmoid(-lam)
            d_pr = d_r * r * (1.0 - r)
            d_pi = d_ig * ig * (1.0 - ig)
            d_prb, d_pib = d_pr.astype(BF16), d_pi.astype(BF16)
            d_xc = d_xc + _dot(d_prb, wr, NT) + _dot(d_pib, wi, NT)

            @pl.when(b == 0)
            def _():
                for ref in (dcw_ref, dcb_ref, dwr_ref, dwi_ref, dbr_ref, dbi_ref, dlam_ref):
                    ref[...] = jnp.zeros_like(ref)

            dx = cw[LRU_CONV - 1:LRU_CONV, :] * d_xc
            for j in range(LRU_CONV - 1):
                sft = LRU_CONV - 1 - j
                dx = dx + cw[j:j + 1, :] * _shift_up(d_xc, sft, 0.0)
                dcw_ref[j:j + 1, :] += jnp.sum(d_xc * _shift_down(x, sft, 0.0), axis=0, keepdims=True)
            dcw_ref[LRU_CONV - 1:LRU_CONV, :] += jnp.sum(d_xc * x, axis=0, keepdims=True)
            dx_s[...] = dx.astype(BF16)
            dcb_ref[...] += jnp.sum(d_xc, axis=0, keepdims=True)
            dwr_ref[...] += _dot(xcb, d_prb, TN)
            dwi_ref[...] += _dot(xcb, d_pib, TN)
            dbr_ref[...] += jnp.sum(d_pr, axis=0, keepdims=True)
            dbi_ref[...] += jnp.sum(d_pi, axis=0, keepdims=True)
            dlam_ref[...] += dlam

        cols = [(COL_XL + n) * LRU_BLOCK, (COL_YL + n) * LRU_BLOCK]
        _write_pieces(dp_ref, wsem, [dx_s, dy_s], b * S, cols, [n, b], grid, compute)

    blk = lambda off: pl.BlockSpec((S, LRU_BLOCK), lambda n, b: (b, off + n))
    vec = lambda rows: pl.BlockSpec((rows, LRU_BLOCK), lambda n, b: (0, n))
    wspec = pl.BlockSpec((None, LRU_BLOCK, LRU_BLOCK), lambda n, b: (n, 0, 0))
    vshape = lambda rows: jax.ShapeDtypeStruct((rows, W), F32)
    wshape = jax.ShapeDtypeStruct((LRU_BLOCKS, LRU_BLOCK, LRU_BLOCK), F32)
    return _pcall(
        body, name=name, grid=grid,
        in_specs=[blk(0), blk(0), blk(COL_XL), blk(COL_YL), vec(LRU_CONV), vec(1), wspec, wspec, vec(1), vec(1),
                  vec(1), HBM_SPEC],
        out_specs=[HBM_SPEC, vec(LRU_CONV), vec(1), wspec, wspec, vec(1), vec(1), vec(1)],
        out_shape=[jax.ShapeDtypeStruct(dproj.shape, dproj.dtype),
                   vshape(LRU_CONV), vshape(1), wshape, wshape, vshape(1), vshape(1), vshape(1)],
        args=[db_in, h, proj, proj, cw, cb, wr, wi, br, bi, lam, dproj],
        scratch=[pltpu.VMEM((S, LRU_BLOCK), BF16), pltpu.VMEM((S, LRU_BLOCK), BF16), pltpu.SemaphoreType.DMA((2,))],
        sem=("arbitrary", "arbitrary"), jobs=jobs, alias_in_out={11: 0})


FFN_CT = 256


def _ffn_conv(gate, cw, cb):
    gc = cb + cw[FFN_CONV - 1:FFN_CONV, :] * gate
    for j in range(FFN_CONV - 1):
        gc = gc + cw[j:j + 1, :] * _shift_down(gate, FFN_CONV - 1 - j, 0.0)
    return gc


def _ffn_act_fwd(up, cw, cb, B, S, name):
    T = B * S
    nct = D_FF // FFN_CT

    def body(g_ref, v_ref, cw_ref, cb_ref, f_ref):
        gc = _ffn_conv(g_ref[...].astype(F32), cw_ref[...], cb_ref[...])
        f_ref[...] = (_gelu(gc) * v_ref[...].astype(F32)).astype(BF16)

    return _pcall(
        body, name=name, grid=(B, nct),
        in_specs=[pl.BlockSpec((S, FFN_CT), lambda b, c: (b, c)), pl.BlockSpec((S, FFN_CT), lambda b, c: (b, nct + c)),
                  pl.BlockSpec((FFN_CONV, FFN_CT), lambda b, c: (0, c)), pl.BlockSpec((1, FFN_CT), lambda b, c: (0, c))],
        out_specs=[pl.BlockSpec((S, FFN_CT), lambda b, c: (b, c))],
        out_shape=[jax.ShapeDtypeStruct((T, D_FF), BF16)], args=[up, up, cw, cb], sem=("parallel", "parallel"))[0]


def _ffn_act_bwd(df, up, cw, cb, B, S, name, jobs=()):
    T = B * S
    nct = D_FF // FFN_CT

    grid = (nct, B)

    def body(df_ref, g_ref, v_ref, cw_ref, cb_ref, du_ref, dcw_ref, dcb_ref, dg_s, dv_s, wsem):
        c, b = pl.program_id(0), pl.program_id(1)

        def compute():
            gate = g_ref[...].astype(F32)
            cw = cw_ref[...]
            gc = _ffn_conv(gate, cw, cb_ref[...])
            gel, dgel = _gelu_and_grad(gc)
            dfv = df_ref[...].astype(F32)
            dv_s[...] = (dfv * gel).astype(BF16)
            dgc = dfv * v_ref[...].astype(F32) * dgel

            @pl.when(b == 0)
            def _():
                dcw_ref[...] = jnp.zeros_like(dcw_ref)
                dcb_ref[...] = jnp.zeros_like(dcb_ref)

            dgate = cw[FFN_CONV - 1:FFN_CONV, :] * dgc
            for j in range(FFN_CONV - 1):
                sft = FFN_CONV - 1 - j
                dgate = dgate + cw[j:j + 1, :] * _shift_up(dgc, sft, 0.0)
                dcw_ref[j:j + 1, :] += jnp.sum(dgc * _shift_down(gate, sft, 0.0), axis=0, keepdims=True)
            dcw_ref[FFN_CONV - 1:FFN_CONV, :] += jnp.sum(dgc * gate, axis=0, keepdims=True)
            dg_s[...] = dgate.astype(BF16)
            dcb_ref[...] += jnp.sum(dgc, axis=0, keepdims=True)

        _write_pieces(du_ref, wsem, [dg_s, dv_s], b * S, [c * FFN_CT, (nct + c) * FFN_CT], [c, b], grid, compute)

    blk = pl.BlockSpec((S, FFN_CT), lambda c, b: (b, c))
    return _pcall(
        body, name=name, grid=grid,
        in_specs=[blk, blk, pl.BlockSpec((S, FFN_CT), lambda c, b: (b, nct + c)),
                  pl.BlockSpec((FFN_CONV, FFN_CT), lambda c, b: (0, c)),
                  pl.BlockSpec((1, FFN_CT), lambda c, b: (0, c))],
        out_specs=[HBM_SPEC, pl.BlockSpec((FFN_CONV, FFN_CT), lambda c, b: (0, c)),
                   pl.BlockSpec((1, FFN_CT), lambda c, b: (0, c))],
        out_shape=[jax.ShapeDtypeStruct((T, 2 * D_FF), BF16),
                   jax.ShapeDtypeStruct((FFN_CONV, D_FF), F32), jax.ShapeDtypeStruct((1, D_FF), F32)],
        args=[df, up, up, cw, cb],
        scratch=[pltpu.VMEM((S, FFN_CT), BF16), pltpu.VMEM((S, FFN_CT), BF16), pltpu.SemaphoreType.DMA((2,))],
        sem=("arbitrary", "arbitrary"), jobs=jobs)


def _rs_add(g, recv, mode, core, name):
    shard = tuple(recv.shape[1:])
    if mode == "mid":
        a, e, c2 = shard
        g_in = g.reshape(a, N_DEV, e, c2)
        grid = (4, 1)
        g_spec = pl.BlockSpec((a, None, e, c2), lambda k, i, c_ref: (0, 2 * k + c_ref[0], 0, 0))
        r_spec = pl.BlockSpec((None, a, e, c2), lambda k, i, c_ref: (k, 0, 0, 0))
    else:
        R, C = shard
        tr = _row_tile(R, 512)
        grid = (4, R // tr)
        if mode == "rows":
            g_in = g.reshape(N_DEV, R, C)
            g_spec = pl.BlockSpec((None, tr, C), lambda k, i, c_ref: (2 * k + c_ref[0], i, 0))
        else:
            g_in = g
            g_spec = pl.BlockSpec((tr, C), lambda k, i, c_ref: (i, 2 * k + c_ref[0]))
        r_spec = pl.BlockSpec((None, tr, C), lambda k, i, c_ref: (k, i, 0))

    def body(c_ref, g_ref, r_ref, o_ref):
        o_ref[...] = g_ref[...] + r_ref[...]

    return pl.pallas_call(
        body, name=name,
        grid_spec=pltpu.PrefetchScalarGridSpec(num_scalar_prefetch=1, grid=grid, in_specs=[g_spec, r_spec],
                                               out_specs=r_spec),
        out_shape=jax.ShapeDtypeStruct(recv.shape, recv.dtype),
        compiler_params=pltpu.CompilerParams(dimension_semantics=("parallel", "parallel"),
                                             vmem_limit_bytes=VMEM_LIMIT),
    )(core, g_in, recv)


def _adam_update(gv, w, m, v):
    nm = ADAM_B1 * m + (1.0 - ADAM_B1) * gv
    nv = ADAM_B2 * v + (1.0 - ADAM_B2) * (gv * gv)
    m_hat = nm / (1.0 - ADAM_B1 ** ADAM_STEP)
    v_hat = nv / (1.0 - ADAM_B2 ** ADAM_STEP)
    delta = -ADAM_LR * (m_hat / (jnp.sqrt(v_hat) + ADAM_EPS) + ADAM_WD * w)
    return delta, nm, nv


def _adamw_shard(partial, recv, w, m, v, chip, name):
    shape = tuple(w.shape)
    tr = _row_tile(shape[0], 256)
    rest = shape[1:]
    zeros = (0,) * len(rest)
    tile = pl.BlockSpec((tr,) + rest, lambda i, s: (i,) + zeros)

    def body(_, p_ref, r_ref, w_ref, m_ref, v_ref, g_ref, d_ref, nm_ref, nv_ref):
        gv = p_ref[...] + r_ref[0] + r_ref[1] + r_ref[2]
        g_ref[...] = gv
        d_ref[...], nm_ref[...], nv_ref[...] = _adam_update(gv, w_ref[...], m_ref[...], v_ref[...])

    grid_spec = pltpu.PrefetchScalarGridSpec(
        num_scalar_prefetch=1, grid=(shape[0] // tr,),
        in_specs=[pl.BlockSpec((None, tr) + rest, lambda i, s: (s[0], i) + zeros),
                  pl.BlockSpec((3, tr) + rest, lambda i, s: (0, i) + zeros), tile, tile, tile],
        out_specs=[tile] * 4)
    return pl.pallas_call(
        body, name=name, grid_spec=grid_spec, out_shape=[jax.ShapeDtypeStruct(shape, F32)] * 4,
        compiler_params=pltpu.CompilerParams(dimension_semantics=("parallel",), vmem_limit_bytes=VMEM_LIMIT),
    )(chip, partial, recv, w, m, v)


def _all_gather_multi(shards, modes, name):
    n = len(shards)
    extents = [_extent(s.shape, m) for s, m in zip(shards, modes)]

    def body(*refs):
        x_refs, out_refs = refs[:n], refs[n:2 * n]
        send_sems, recv_sems, local_sems = refs[2 * n:]
        x, y, c = _mesh_pos()
        me, sibling = (x, y, c), (x, y, 1 - c)
        chips = _other_chips(x, y)

        def slot(i, px, py, pc):
            return _window(out_refs[i], modes[i], extents[i], 4 * px + 2 * py + pc)

        def copy(i, k, block, to, src=None):
            return _remote(slot(i, *block) if src is None else src, slot(i, *block),
                           send_sems.at[i, k], recv_sems.at[i, k], to)

        mine = [pltpu.make_async_copy(x_refs[i], slot(i, *me), local_sems.at[i]) for i in range(n)]
        sends = []
        for i in range(n):
            mine[i].start()
            first = [copy(i, 0, me, sibling, src=x_refs[i])]
            first += [copy(i, 1 + j, me, (*chip, c), src=x_refs[i]) for j, chip in enumerate(chips)]
            for cp in first:
                cp.start()
            sends += first
        for i in range(n):
            for j, chip in enumerate(chips):
                copy(i, 1 + j, (*chip, c), me).wait_recv()
                fwd = copy(i, 4 + j, (*chip, c), sibling)
                fwd.start()
                sends.append(fwd)
        for i in range(n):
            copy(i, 0, sibling, me).wait_recv()
            for j, chip in enumerate(chips):
                copy(i, 4 + j, (*chip, 1 - c), me).wait_recv()
        for cp in sends:
            cp.wait_send()
        for cp in mine:
            cp.wait()

    return pl.pallas_call(
        body, name=name,
        in_specs=[HBM_SPEC] * n, out_specs=[HBM_SPEC] * n,
        out_shape=[jax.ShapeDtypeStruct(_full_shape(s.shape, m), s.dtype) for s, m in zip(shards, modes)],
        scratch_shapes=[pltpu.SemaphoreType.DMA((n, 7)), pltpu.SemaphoreType.DMA((n, 7)),
                        pltpu.SemaphoreType.DMA((n,))],
    )(*shards)


SMALL_LANES = 1024


def _small_rows(shape):
    r, w = shape
    return r * max(1, w // SMALL_LANES)


def _small_allreduce(parts, name):
    n = len(parts)
    shapes = [tuple(p.shape) for p in parts]
    offs, total = [], 0
    for s in shapes:
        offs.append(total)
        total += _small_rows(s)
    rows = -(-total // 8) * 8

    def body(*refs):
        p_refs, o_refs = refs[:n], refs[n:2 * n]
        buf, tot, send_sems, recv_sems = refs[2 * n:]
        x, y, c = _mesh_pos()
        me, sibling = (x, y, c), (x, y, 1 - c)
        chips = _other_chips(x, y)

        def slot(px, py, pc):
            return buf.at[4 * px + 2 * py + pc]

        def copy(k, block, to):
            return _remote(slot(*block), slot(*block), send_sems.at[k], recv_sems.at[k], to)

        tot[...] = jnp.zeros_like(tot)
        for p_ref, (r, w), off in zip(p_refs, shapes, offs):
            wl = min(w, SMALL_LANES)
            for part in range(max(1, w // SMALL_LANES)):
                tot[pl.ds(off + part * r, r), pl.ds(0, wl)] = p_ref[:, pl.ds(part * SMALL_LANES, wl)]
        buf[4 * x + 2 * y + c] = tot[...]
        first = [copy(0, me, sibling)] + [copy(1 + j, me, (*chip, c)) for j, chip in enumerate(chips)]
        for cp in first:
            cp.start()
        passed = [copy(4 + j, (*chip, c), sibling) for j, chip in enumerate(chips)]
        for j, chip in enumerate(chips):
            copy(1 + j, (*chip, c), me).wait_recv()
            passed[j].start()
        copy(0, sibling, me).wait_recv()
        for j, chip in enumerate(chips):
            copy(4 + j, (*chip, 1 - c), me).wait_recv()
        for cp in first + passed:
            cp.wait_send()
        acc = buf[0]
        for d in range(1, N_DEV):
            acc = acc + buf[d]
        tot[...] = acc
        for o_ref, (r, w), off in zip(o_refs, shapes, offs):
            wl = min(w, SMALL_LANES)
            for part in range(max(1, w // SMALL_LANES)):
                o_ref[:, pl.ds(part * SMALL_LANES, wl)] = tot[pl.ds(off + part * r, r), pl.ds(0, wl)]

    vm = pl.BlockSpec(memory_space=pltpu.VMEM)
    return pl.pallas_call(
        body, name=name,
        in_specs=[vm] * n, out_specs=[vm] * n,
        out_shape=[jax.ShapeDtypeStruct(s, F32) for s in shapes],
        scratch_shapes=[pltpu.VMEM((N_DEV, rows, SMALL_LANES), F32), pltpu.VMEM((rows, SMALL_LANES), F32),
                        pltpu.SemaphoreType.DMA((7,)), pltpu.SemaphoreType.DMA((7,))],
    )(*parts)


def _adamw_small(gs, ws, ms, vs, name):
    n = len(gs)

    def body(*refs):
        g_r, w_r, m_r, v_r = refs[:n], refs[n:2 * n], refs[2 * n:3 * n], refs[3 * n:4 * n]
        d_r, nm_r, nv_r = refs[4 * n:5 * n], refs[5 * n:6 * n], refs[6 * n:7 * n]
        for i in range(n):
            d_r[i][...], nm_r[i][...], nv_r[i][...] = _adam_update(g_r[i][...], w_r[i][...], m_r[i][...], v_r[i][...])

    vm = pl.BlockSpec(memory_space=pltpu.VMEM)
    shapes = [jax.ShapeDtypeStruct(w.shape, F32) for w in ws]
    outs = pl.pallas_call(body, name=name, in_specs=[vm] * (4 * n), out_specs=[vm] * (3 * n),
                          out_shape=shapes * 3)(*gs, *ws, *ms, *vs)
    return outs[:n], outs[n:2 * n], outs[2 * n:]


FIRST = [("w_in", (1024, 896), "cols"), ("lru_w_r", (4, 32, 256), "mid"), ("lru_w_i", (4, 32, 256), "mid")]
LATE = [("w_ret_o", (128, 1024), "rows"), ("w_lru_o", (128, 1024), "rows"), ("w_out", (128, 1024), "rows"),
        ("ffn_w_up", (1024, 768), "cols"), ("ffn_w_down", (384, 1024), "rows")]
BIG = FIRST + LATE
SMALL_SHARDED = [("merge_gate_b", (2, 128), "cols"), ("lru_conv_w", (4, 128), "cols"), ("lru_b_r", (4, 32), "stack"),
                 ("lru_b_i", (4, 32), "stack"), ("ffn_conv_w", (3, 384), "cols")]
REPLICATED = [("norm1_w", (1, 1024)), ("ret_gn_w", (1, 1024)), ("lru_conv_b", (1, 1024)), ("lru_lambda", (1, 1024)),
              ("norm2_w", (1, 1024)), ("ffn_conv_b", (1, 3072)), ("norm_f_w", (1, 1024))]
MODE = {n: m for n, _, m in BIG}
SHARD = {n: s for n, s, _ in BIG}


def _local_step(x3, positions, target3, wb, ws, late_shards, core):
    B, S, D = x3.shape
    T = B * S
    x = x3.reshape(T, D)
    target = target3.reshape(T, D)
    tm = min(512, T)
    big = min(1024, T)
    big2 = min(2048, T)

    half = RET_DK // 2
    inv_freq = ROPE_BASE ** (-jnp.arange(half, dtype=F32) / half)
    inv2 = jnp.concatenate([inv_freq, inv_freq]).reshape(1, RET_DK)
    log_gamma = jnp.log1p(-jnp.power(2.0, -5.0 - jnp.arange(RET_HEADS, dtype=F32)))
    lgam = jnp.broadcast_to(log_gamma[:, None, None], (RET_HEADS, 8, LANES))
    pos_col = positions.astype(F32).reshape(T, 1)
    cos2, sin2s = _rope_tables(pos_col, inv2, tm, "rope_tables")

    late_names = [n for n, _, _ in LATE]
    late_modes = [m for _, _, m in LATE]
    late_shapes = [s for _, s, _ in LATE]

    h1 = _rmsnorm_fwd(x, ws["norm1_w"], tm, "norm1_fwd")
    proj, *late_part = _matmul(h1, wb["w_in"], "nn", BF16, big2, 1024, 1024, "proj_fwd",
                               jobs=[_ag_first_job(late_shards, late_modes)])
    o, a_in, *late_full = _retention_fwd(proj, cos2, sin2s, lgam, ws["ret_gn_w"], B, S, "retention_fwd",
                                         jobs=[_ag_second_job(late_part, late_modes, late_shapes)])
    wb = dict(wb, **dict(zip(late_names, late_full)))
    hl, b_in = _lru_fwd(proj, ws["lru_conv_w"], ws["lru_conv_b"], wb["lru_w_r"], wb["lru_w_i"],
                        ws["lru_b_r"], ws["lru_b_i"], ws["lru_lambda"], B, S, "lru_fwd")
    x1, mix, h2 = _mix_fwd(a_in, b_in, proj, x, wb["w_ret_o"], wb["w_lru_o"], wb["w_out"],
                           ws["merge_gate_b"], ws["norm2_w"], tm, "mix_fwd")
    up = _matmul(h2, wb["ffn_w_up"], "nn", BF16, big2, 1024, 1024, "ffn_up_fwd")[0]
    f = _ffn_act_fwd(up, ws["ffn_conv_w"], ws["ffn_conv_b"], B, S, "ffn_act_fwd")
    x2 = _matmul(f, wb["ffn_w_down"], "nn", F32, big, 1024, D_FF, "ffn_down_fwd", add=x1)[0]
    dx2, dx2b, loss_acc, d_norm_f = _loss_head(x2, target, ws["norm_f_w"], tm, "loss_head")

    g, rs = {}, {}

    def stage1(names, grads):
        return _rs_sibling_job(grads, [MODE[n] for n in names], [SHARD[n] for n in names])

    def add(names, grads, recvs):
        return [_rs_add(gr, r, MODE[n], core, "rs_add_" + n) for n, gr, r in zip(names, grads, recvs)]

    g["norm_f_w"] = d_norm_f
    g_down = _matmul(f, dx2b, "tn", F32, 1024, 1024, big2, "ffn_down_bwd_w")[0]
    df, s1_down = _matmul(dx2b, wb["ffn_w_down"], "nt", BF16, big2, 1024, 1024, "ffn_down_bwd_x",
                          jobs=[stage1(["ffn_w_down"], [g_down])])
    p_down = add(["ffn_w_down"], [g_down], [s1_down])
    dup, g["ffn_conv_w"], g["ffn_conv_b"], s2_down = _ffn_act_bwd(
        df, up, ws["ffn_conv_w"], ws["ffn_conv_b"], B, S, "ffn_act_bwd", jobs=[_rs_chip_job(p_down)])
    rs["ffn_w_down"] = (p_down[0], s2_down)

    g_up = _matmul(h2, dup, "tn", F32, 1024, 1024, big2, "ffn_up_bwd_w")[0]
    dh2, s1_up = _matmul(dup, wb["ffn_w_up"], "nt", BF16, big, 1024, D_FF, "ffn_up_bwd_x",
                         jobs=[stage1(["ffn_w_up"], [g_up])])
    p_up = add(["ffn_w_up"], [g_up], [s1_up])
    dx1, dx1b, g["norm2_w"] = _rmsnorm_bwd_add(dx2, dh2, x1, ws["norm2_w"], tm, "norm2_bwd", True)
    da_in, db_in, dya, dyb, dproj, g["merge_gate_b"], s2_up = _mix_bwd(
        dx1b, a_in, b_in, proj, wb["w_ret_o"], wb["w_lru_o"], wb["w_out"], ws["merge_gate_b"], tm, "mix_bwd",
        jobs=[_rs_chip_job(p_up)])
    rs["ffn_w_up"] = (p_up[0], s2_up)

    mid_names = ["w_out", "w_ret_o", "w_lru_o"]
    g_mid = [_matmul(mix, dx1b, "tn", F32, 1024, 1024, big2, "w_out_bwd_w")[0],
             _matmul(a_in, dya, "tn", F32, 1024, 1024, big2, "w_ret_o_bwd_w")[0],
             _matmul(b_in, dyb, "tn", F32, 1024, 1024, big2, "w_lru_o_bwd_w")[0]]
    (dproj, g["lru_conv_w"], g["lru_conv_b"], g_wr, g_wi, g["lru_b_r"], g["lru_b_i"], g["lru_lambda"],
     *s1_mid) = _lru_bwd(db_in, hl, proj, dproj, ws["lru_conv_w"], ws["lru_conv_b"], wb["lru_w_r"], wb["lru_w_i"],
                         ws["lru_b_r"], ws["lru_b_i"], ws["lru_lambda"], B, S, "lru_bwd",
                         jobs=[stage1(mid_names, g_mid)])
    p_mid = add(mid_names, g_mid, s1_mid)
    lru_names = ["lru_w_r", "lru_w_i"]
    dproj, g["ret_gn_w"], *rest = _retention_bwd(
        da_in, o, proj, dproj, cos2, sin2s, lgam, ws["ret_gn_w"], B, S, "retention_bwd",
        jobs=[_rs_chip_job(p_mid), stage1(lru_names, [g_wr, g_wi])])
    s2_mid, s1_lru = rest[:3], rest[3:]
    for n, p, r in zip(mid_names, p_mid, s2_mid):
        rs[n] = (p, r)
    p_lru = add(lru_names, [g_wr, g_wi], s1_lru)

    g_in, *s2_lru = _matmul(h1, dproj, "tn", F32, 1024, 1024, big2, "proj_bwd_w", jobs=[_rs_chip_job(p_lru)])
    for n, p, r in zip(lru_names, p_lru, s2_lru):
        rs[n] = (p, r)
    s1_in = _pcall(lambda: None, name="rs_sibling_w_in", grid=(1,), in_specs=[], out_specs=[], out_shape=[], args=[],
                   sem=("arbitrary",), jobs=[stage1(["w_in"], [g_in])])
    p_in = add(["w_in"], [g_in], s1_in)
    dh1, s2_in = _matmul(dproj, wb["w_in"], "nt", BF16, big, 1024, D_IN // 2, "proj_bwd_x",
                         jobs=[_rs_chip_job(p_in)])
    grad_x, g["norm1_w"] = _rmsnorm_bwd_add(dx1, dh1, x, ws["norm1_w"], tm, "norm1_bwd", False)
    rs["w_in"] = (p_in[0], s2_in)
    return loss_acc, grad_x.reshape(B, S, D), g, rs


def kernel(x, positions, norm1_w, w_in, merge_gate_b, ret_gn_w, w_ret_o, lru_conv_w, lru_conv_b, lru_w_r, lru_b_r, lru_w_i, lru_b_i, lru_lambda, w_lru_o, w_out, norm2_w, ffn_w_up, ffn_conv_w, ffn_conv_b, ffn_w_down, norm_f_w, loss_target, m_norm1_w, m_w_in, m_merge_gate_b, m_ret_gn_w, m_w_ret_o, m_lru_conv_w, m_lru_conv_b, m_lru_w_r, m_lru_b_r, m_lru_w_i, m_lru_b_i, m_lru_lambda, m_w_lru_o, m_w_out, m_norm2_w, m_ffn_w_up, m_ffn_conv_w, m_ffn_conv_b, m_ffn_w_down, m_norm_f_w, v_norm1_w, v_w_in, v_merge_gate_b, v_ret_gn_w, v_w_ret_o, v_lru_conv_w, v_lru_conv_b, v_lru_w_r, v_lru_b_r, v_lru_w_i, v_lru_b_i, v_lru_lambda, v_w_lru_o, v_w_out, v_norm2_w, v_ffn_w_up, v_ffn_conv_w, v_ffn_conv_b, v_ffn_w_down, v_norm_f_w):
    names = ["norm1_w", "w_in", "merge_gate_b", "ret_gn_w", "w_ret_o", "lru_conv_w", "lru_conv_b", "lru_w_r", "lru_b_r",
             "lru_w_i", "lru_b_i", "lru_lambda", "w_lru_o", "w_out", "norm2_w", "ffn_w_up", "ffn_conv_w", "ffn_conv_b",
             "ffn_w_down", "norm_f_w"]
    w_args = [norm1_w, w_in, merge_gate_b, ret_gn_w, w_ret_o, lru_conv_w, lru_conv_b, lru_w_r, lru_b_r, lru_w_i, lru_b_i,
              lru_lambda, w_lru_o, w_out, norm2_w, ffn_w_up, ffn_conv_w, ffn_conv_b, ffn_w_down, norm_f_w]
    m_args = [m_norm1_w, m_w_in, m_merge_gate_b, m_ret_gn_w, m_w_ret_o, m_lru_conv_w, m_lru_conv_b, m_lru_w_r, m_lru_b_r,
              m_lru_w_i, m_lru_b_i, m_lru_lambda, m_w_lru_o, m_w_out, m_norm2_w, m_ffn_w_up, m_ffn_conv_w, m_ffn_conv_b,
              m_ffn_w_down, m_norm_f_w]
    v_args = [v_norm1_w, v_w_in, v_merge_gate_b, v_ret_gn_w, v_w_ret_o, v_lru_conv_w, v_lru_conv_b, v_lru_w_r, v_lru_b_r,
              v_lru_w_i, v_lru_b_i, v_lru_lambda, v_w_lru_o, v_w_out, v_norm2_w, v_ffn_w_up, v_ffn_conv_w, v_ffn_conv_b,
              v_ffn_w_down, v_norm_f_w]
    orig_shape = {n: a.shape for n, a in zip(names, w_args)}
    local_shape = {n: s for n, s, _ in BIG + SMALL_SHARDED}
    local_shape.update({n: s for n, s in REPLICATED})
    W = {n: a.reshape(local_shape[n]) for n, a in zip(names, w_args)}
    M = {n: a.reshape(local_shape[n]) for n, a in zip(names, m_args)}
    V = {n: a.reshape(local_shape[n]) for n, a in zip(names, v_args)}

    xi, yi, ci = _mesh_pos()
    dev = 4 * xi + 2 * yi + ci
    chip = (2 * xi + yi).astype(jnp.int32).reshape(1)
    core = ci.astype(jnp.int32).reshape(1)

    first_names = [n for n, _, _ in FIRST]
    small_names = [n for n, _, _ in SMALL_SHARDED]
    gathered = _all_gather_multi([W[n].astype(BF16) for n in first_names] + [W[n] for n in small_names],
                                 [m for _, _, m in FIRST + SMALL_SHARDED], "gather_first_weights")
    wb = dict(zip(first_names, gathered[:len(FIRST)]))
    ws = dict(zip(small_names, gathered[len(FIRST):]))
    for n in ("lru_b_r", "lru_b_i"):
        ws[n] = jnp.transpose(ws[n], (1, 0, 2)).reshape(1, LRU_BLOCKS * LRU_BLOCK)
    for n, _ in REPLICATED:
        ws[n] = W[n]

    late_shards = [W[n].astype(BF16) for n, _, _ in LATE]
    loss_acc, grad_x, g, rs = _local_step(x, positions, loss_target, wb, ws, late_shards, core)

    G_out, D_out, M_out, V_out = {}, {}, {}, {}
    for n, _, _ in BIG:
        G_out[n], D_out[n], M_out[n], V_out[n] = _adamw_shard(rs[n][0], rs[n][1], W[n], M[n], V[n], chip, "adamw_" + n)

    rep_names = [n for n, _ in REPLICATED]
    red_names = rep_names + small_names
    red = _small_allreduce([g[n] for n in red_names] + [loss_acc[0:1, :]], "allreduce_small_grads")
    loss = red[-1][0, 0]
    gs = dict(zip(red_names, red[:-1]))
    for n, s, mode in SMALL_SHARDED:
        if mode == "cols":
            gs[n] = lax.dynamic_slice_in_dim(gs[n], dev * s[1], s[1], axis=1)
        else:
            full = gs[n].reshape(LRU_BLOCKS, LRU_BLOCK)
            gs[n] = lax.dynamic_slice_in_dim(full, dev * s[1], s[1], axis=1)
    d2, m2, v2 = _adamw_small([gs[n] for n in red_names], [W[n] for n in red_names], [M[n] for n in red_names],
                              [V[n] for n in red_names], "adamw_small")
    for i, n in enumerate(red_names):
        G_out[n], D_out[n], M_out[n], V_out[n] = gs[n], d2[i], m2[i], v2[i]

    outs = [loss, grad_x]
    for group in (G_out, D_out, M_out, V_out):
        outs += [group[n].reshape(orig_shape[n]) for n in names]
    return tuple(outs)
```

```python
import math

import jax
import jax.numpy as jnp
from jax import lax
from jax.experimental import pallas as pl
from jax.experimental.pallas import tpu as pltpu

F32 = jnp.float32
BF16 = jnp.bfloat16
MESH = pl.DeviceIdType.MESH

D_MODEL = 1024
CHUNK = 64
RET_HEADS = 4
RET_DK = 128
RET_DV = 256
LRU_BLOCKS = 4
LRU_BLOCK = 256
LRU_CONV = 4
LRU_C = 8.0
D_FF = 3072
FFN_CONV = 3
ROPE_BASE = 10000.0
RMS_EPS = 1e-6
GN_EPS = 1e-6
D_IN = 7168
ADAM_LR, ADAM_B1, ADAM_B2, ADAM_EPS, ADAM_WD, ADAM_STEP = 0.001, 0.9, 0.999, 1e-08, 0.01, 10

N_DEV = 8
V7X_VMEM_BYTES = 64 * 1024 * 1024
VMEM_LIMIT = V7X_VMEM_BYTES - 8 * 1024 * 1024
RET_BLOCK = 256
LANES = 128

COL_Q, COL_K = 0, 4
COL_V, COL_G, COL_XL, COL_YL = 4, 8, 12, 16
COL_GR, COL_GL = 5, 6

HBM_SPEC = pl.BlockSpec(memory_space=pl.ANY)


def _gelu(x):
    c = math.sqrt(2.0 / math.pi)
    t = jnp.tanh(c * (x + 0.044715 * x * x * x))
    return 0.5 * x * (1.0 + t)


def _gelu_and_grad(x):
    c = math.sqrt(2.0 / math.pi)
    x2 = x * x
    t = jnp.tanh(c * (x + 0.044715 * x2 * x))
    g = 0.5 * x * (1.0 + t)
    dg = 0.5 * (1.0 + t) + 0.5 * x * (1.0 - t * t) * c * (1.0 + 3.0 * 0.044715 * x2)
    return g, dg


def _sigmoid(x):
    return 1.0 / (1.0 + jnp.exp(-x))


SUBLANES = 8


def _shift_down(x, s, fill):
    r = pltpu.roll(x, s, 0)
    rows = lax.broadcasted_iota(jnp.int32, (SUBLANES,) + x.shape[1:], 0)
    top = jnp.where(rows >= s, r[:SUBLANES], fill)
    return jnp.concatenate([top, r[SUBLANES:]], axis=0)


def _shift_up(x, s, fill):
    n = x.shape[0]
    r = pltpu.roll(x, n - s, 0)
    rows = lax.broadcasted_iota(jnp.int32, (SUBLANES,) + x.shape[1:], 0)
    bottom = jnp.where(rows < SUBLANES - s, r[n - SUBLANES:], fill)
    return jnp.concatenate([r[:n - SUBLANES], bottom], axis=0)


SCAN_CHUNK = 64


def _scan_forward(a, b):
    n = a.shape[0]
    s = 1
    while s < n:
        if s % SUBLANES:
            b = a * _shift_down(b, s, 0.0) + b
            a = a * _shift_down(a, s, 1.0)
        else:
            b = jnp.concatenate([b[:s], a[s:] * b[:n - s] + b[s:]], axis=0)
            a = jnp.concatenate([a[:s], a[s:] * a[:n - s]], axis=0)
        s *= 2
    return a, b


def _scan_backward(a_next, u):
    n = u.shape[0]
    s = 1
    while s < n:
        if s % SUBLANES:
            u = u + a_next * _shift_up(u, s, 0.0)
            a_next = a_next * _shift_up(a_next, s, 1.0)
        else:
            u = jnp.concatenate([u[:n - s] + a_next[:n - s] * u[s:], u[n - s:]], axis=0)
            a_next = jnp.concatenate([a_next[:n - s] * a_next[s:], a_next[n - s:]], axis=0)
        s *= 2
    return a_next, u


def _scan_forward_ref(a_ref, b_ref, h_ref):
    S, W = a_ref.shape
    for strip in range(W // LANES):
        cols = pl.ds(strip * LANES, LANES)

        def body(k, carry, cols=cols):
            rows = pl.ds(pl.multiple_of(k * SCAN_CHUNK, SCAN_CHUNK), SCAN_CHUNK)
            a_cum, h_loc = _scan_forward(a_ref[rows, cols], b_ref[rows, cols])
            h = h_loc + a_cum * carry
            h_ref[rows, cols] = h
            return h[SCAN_CHUNK - 1:, :]

        lax.fori_loop(0, S // SCAN_CHUNK, body, jnp.zeros((1, LANES), F32))


def _scan_backward_ref(an_ref, u_ref, d_ref):
    S, W = an_ref.shape
    n_chunks = S // SCAN_CHUNK
    for strip in range(W // LANES):
        cols = pl.ds(strip * LANES, LANES)

        def body(i, carry, cols=cols):
            rows = pl.ds(pl.multiple_of((n_chunks - 1 - i) * SCAN_CHUNK, SCAN_CHUNK), SCAN_CHUNK)
            an_cum, d_loc = _scan_backward(an_ref[rows, cols], u_ref[rows, cols])
            d = d_loc + an_cum * carry
            d_ref[rows, cols] = d
            return d[:1, :]

        lax.fori_loop(0, n_chunks, body, jnp.zeros((1, LANES), F32))


def _dot(a, b, dims):
    return lax.dot_general(a, b, (dims, ((), ())), preferred_element_type=F32)


NN = ((1,), (0,))
NT = ((1,), (1,))
TN = ((0,), (0,))


def _mesh_pos():
    return lax.axis_index("x"), lax.axis_index("y"), lax.axis_index("c")


def _other_chips(x, y):
    return [(1 - x, y), (x, 1 - y), (1 - x, 1 - y)]


def _full_shape(shard_shape, mode):
    if mode == "rows":
        return (N_DEV * shard_shape[0],) + tuple(shard_shape[1:])
    if mode == "cols":
        return (shard_shape[0], N_DEV * shard_shape[1])
    if mode == "mid":
        return (shard_shape[0], N_DEV * shard_shape[1], shard_shape[2])
    return (N_DEV,) + tuple(shard_shape)


def _extent(shard_shape, mode):
    return {"rows": shard_shape[0], "cols": shard_shape[1], "mid": shard_shape[1], "stack": 1}[mode]


def _window(ref, mode, extent, d):
    if mode == "stack":
        return ref.at[d]
    start = pl.multiple_of(d * extent, extent)
    if mode == "rows":
        return ref.at[pl.ds(start, extent)]
    if mode == "cols":
        return ref.at[:, pl.ds(start, extent)]
    return ref.at[:, pl.ds(start, extent), :]


class _Job:
    def __init__(self, inputs, out_shapes, sems, start, finish, aliases=None):
        self.inputs, self.out_shapes, self.sems = list(inputs), list(out_shapes), sems
        self.start, self.finish, self.aliases = start, finish, dict(aliases or {})


def _remote(src, dst, send_sem, recv_sem, to):
    return pltpu.make_async_remote_copy(src_ref=src, dst_ref=dst, send_sem=send_sem, recv_sem=recv_sem,
                                        device_id=to, device_id_type=MESH)


def _ag_first_job(shards, modes):
    n = len(shards)
    extents = [_extent(s.shape, m) for s, m in zip(shards, modes)]

    def copies(x_refs, out_refs, send, recv, local, arriving):
        x, y, c = _mesh_pos()
        peers = [(x, y, 1 - c)] + [(*chip, c) for chip in _other_chips(x, y)]
        win = lambda i, p: _window(out_refs[i], modes[i], extents[i], 4 * p[0] + 2 * p[1] + p[2])
        if arriving:
            return [_remote(x_refs[i], win(i, p), send.at[i, k], recv.at[i, k], p)
                    for i in range(n) for k, p in enumerate(peers)]
        mine = [pltpu.make_async_copy(x_refs[i], win(i, (x, y, c)), local.at[i]) for i in range(n)]
        sends = [_remote(x_refs[i], win(i, (x, y, c)), send.at[i, k], recv.at[i, k], p)
                 for i in range(n) for k, p in enumerate(peers)]
        return mine, sends

    def start(*refs):
        mine, sends = copies(*refs, False)
        for cp in mine + sends:
            cp.start()

    def finish(*refs):
        for cp in copies(*refs, True):
            cp.wait_recv()
        mine, sends = copies(*refs, False)
        for cp in sends:
            cp.wait_send()
        for cp in mine:
            cp.wait()

    out_shapes = [jax.ShapeDtypeStruct(_full_shape(s.shape, m), s.dtype) for s, m in zip(shards, modes)]
    return _Job(shards, out_shapes, ((n, 4), (n, 4), (n,)), start, finish)


def _ag_second_job(fulls, modes, shard_shapes):
    n = len(fulls)
    extents = [_extent(s, m) for s, m in zip(shard_shapes, modes)]

    def copies(_, out_refs, send, recv, local, core_of_block):
        x, y, c = _mesh_pos()
        pc = c if core_of_block == "mine" else 1 - c
        win = lambda i, chip: _window(out_refs[i], modes[i], extents[i], 4 * chip[0] + 2 * chip[1] + pc)
        return [_remote(win(i, chip), win(i, chip), send.at[i, j], recv.at[i, j], (x, y, 1 - c))
                for i in range(n) for j, chip in enumerate(_other_chips(x, y))]

    def start(*refs):
        for cp in copies(*refs, "mine"):
            cp.start()

    def finish(*refs):
        for cp in copies(*refs, "sibling"):
            cp.wait_recv()
        for cp in copies(*refs, "mine"):
            cp.wait_send()

    out_shapes = [jax.ShapeDtypeStruct(f.shape, f.dtype) for f in fulls]
    return _Job(fulls, out_shapes, ((n, 3), (n, 3), (1,)), start, finish, aliases={i: i for i in range(n)})


def _rs_sibling_job(grads, modes, shard_shapes):
    n = len(grads)
    extents = [_extent(s, m) for s, m in zip(shard_shapes, modes)]

    def copies(g_refs, out_refs, send, recv, local):
        x, y, c = _mesh_pos()
        return [_remote(_window(g_refs[i], modes[i], extents[i], 2 * k + (1 - c)), out_refs[i].at[k],
                        send.at[i, k], recv.at[i, k], (x, y, 1 - c))
                for i in range(n) for k in range(4)]

    def start(*refs):
        for cp in copies(*refs):
            cp.start()

    def finish(*refs):
        cps = copies(*refs)
        for cp in cps:
            cp.wait_recv()
        for cp in cps:
            cp.wait_send()

    out_shapes = [jax.ShapeDtypeStruct((4,) + tuple(s), g.dtype) for s, g in zip(shard_shapes, grads)]
    return _Job(grads, out_shapes, ((n, 4), (n, 4), (1,)), start, finish)


def _rs_chip_job(partials):
    n = len(partials)

    def copies(p_refs, out_refs, send, recv, local):
        x, y, c = _mesh_pos()
        return [_remote(p_refs[i].at[2 * px + py], out_refs[i].at[j], send.at[i, j], recv.at[i, j], (px, py, c))
                for i in range(n) for j, (px, py) in enumerate(_other_chips(x, y))]

    def start(*refs):
        for cp in copies(*refs):
            cp.start()

    def finish(*refs):
        cps = copies(*refs)
        for cp in cps:
            cp.wait_recv()
        for cp in cps:
            cp.wait_send()

    out_shapes = [jax.ShapeDtypeStruct((3,) + tuple(p.shape[1:]), p.dtype) for p in partials]
    return _Job(partials, out_shapes, ((n, 3), (n, 3), (1,)), start, finish)


def _all_true(conds):
    out = conds[0]
    for c in conds[1:]:
        out = jnp.logical_and(out, c)
    return out


def _pcall(body, *, name, grid, in_specs, out_specs, out_shape, args, sem, scratch=(), jobs=(), alias_in_out=None):
    n_in, n_out, n_scr = len(args), len(out_shape), len(scratch)
    job_in = [a for j in jobs for a in j.inputs]
    job_out = [s for j in jobs for s in j.out_shapes]
    job_sems = [pltpu.SemaphoreType.DMA(shape) for j in jobs for shape in j.sems]
    aliases, in_off, out_off = dict(alias_in_out or {}), n_in, n_out
    for j in jobs:
        for a, b in j.aliases.items():
            aliases[in_off + a] = out_off + b
        in_off += len(j.inputs)
        out_off += len(j.out_shapes)

    def wrapped(*refs):
        ins = refs[:n_in]
        jins = refs[n_in:n_in + len(job_in)]
        o0 = n_in + len(job_in)
        outs = refs[o0:o0 + n_out]
        jouts = refs[o0 + n_out:o0 + n_out + len(job_out)]
        s0 = o0 + n_out + len(job_out)
        scr = refs[s0:s0 + n_scr]
        jsems = refs[s0 + n_scr:]
        if jobs:
            ids = [pl.program_id(a) for a in range(len(grid))]
            first = _all_true([i == 0 for i in ids])
            last = _all_true([i == g - 1 for i, g in zip(ids, grid)])

            def per_job(which):
                i0 = o0_ = 0
                for k, j in enumerate(jobs):
                    fn = j.start if which == "start" else j.finish
                    fn(jins[i0:i0 + len(j.inputs)], jouts[o0_:o0_ + len(j.out_shapes)], *jsems[3 * k:3 * k + 3])
                    i0 += len(j.inputs)
                    o0_ += len(j.out_shapes)

            @pl.when(first)
            def _():
                per_job("start")

        body(*ins, *outs, *scr)
        if jobs:
            @pl.when(last)
            def _():
                per_job("finish")

    semantics = tuple("arbitrary" for _ in grid) if jobs else sem
    return pl.pallas_call(
        wrapped, name=name, grid=grid,
        in_specs=list(in_specs) + [HBM_SPEC] * len(job_in),
        out_specs=list(out_specs) + [HBM_SPEC] * len(job_out),
        out_shape=list(out_shape) + job_out,
        scratch_shapes=list(scratch) + job_sems,
        input_output_aliases=aliases,
        compiler_params=pltpu.CompilerParams(dimension_semantics=semantics, vmem_limit_bytes=VMEM_LIMIT),
    )(*args, *job_in)


def _row_tile(rows, cap):
    if rows <= cap:
        return rows
    best = None
    for t in range(16, cap + 1, 16):
        if rows % t == 0:
            best = t
    assert best is not None
    return best


def _matmul(a, b, mode, out_dtype, tm, tn, tk, name, add=None, jobs=()):
    if mode == "tn":
        K, M = a.shape
    else:
        M, K = a.shape
    N = b.shape[0] if mode == "nt" else b.shape[1]
    tm, tn, tk = min(tm, M), min(tn, N), min(tk, K)
    assert M % tm == 0 and N % tn == 0 and K % tk == 0
    nk = K // tk
    dims = {"nn": NN, "nt": NT, "tn": TN}[mode]

    def body(*refs):
        if add is None:
            a_ref, b_ref, o_ref, acc = refs
            add_ref = None
        else:
            a_ref, b_ref, add_ref, o_ref, acc = refs
        k = pl.program_id(2)
        p = _dot(a_ref[...], b_ref[...], dims)

        def finish(r):
            if add_ref is not None:
                r = r + add_ref[...].astype(F32)
            o_ref[...] = r.astype(out_dtype)

        if nk == 1:
            finish(p)
        else:
            @pl.when(k == 0)
            def _():
                acc[...] = p

            @pl.when(k > 0)
            def _():
                acc[...] += p

            @pl.when(k == nk - 1)
            def _():
                finish(acc[...])

    if mode == "tn":
        a_spec = pl.BlockSpec((tk, tm), lambda i, j, k: (k, i))
    else:
        a_spec = pl.BlockSpec((tm, tk), lambda i, j, k: (i, k))
    if mode == "nt":
        b_spec = pl.BlockSpec((tn, tk), lambda i, j, k: (j, k))
    else:
        b_spec = pl.BlockSpec((tk, tn), lambda i, j, k: (k, j))
    in_specs = [a_spec, b_spec]
    args = [a, b]
    if add is not None:
        in_specs.append(pl.BlockSpec((tm, tn), lambda i, j, k: (i, j)))
        args.append(add)
    return _pcall(
        body, name=name, grid=(M // tm, N // tn, nk), in_specs=in_specs,
        out_specs=[pl.BlockSpec((tm, tn), lambda i, j, k: (i, j))],
        out_shape=[jax.ShapeDtypeStruct((M, N), out_dtype)], args=args,
        scratch=[pltpu.VMEM((tm, tn) if nk > 1 else (8, LANES), F32)],
        sem=("parallel", "parallel", "arbitrary"), jobs=jobs)


def _rmsnorm_fwd(x, w, tm, name):
    T, D = x.shape

    def body(x_ref, w_ref, h_ref):
        xv = x_ref[...]
        r = lax.rsqrt(jnp.mean(xv * xv, axis=-1, keepdims=True) + RMS_EPS)
        h_ref[...] = (xv * r * w_ref[...]).astype(BF16)

    return _pcall(
        body, name=name, grid=(T // tm,),
        in_specs=[pl.BlockSpec((tm, D), lambda i: (i, 0)), pl.BlockSpec((1, D), lambda i: (0, 0))],
        out_specs=[pl.BlockSpec((tm, D), lambda i: (i, 0))],
        out_shape=[jax.ShapeDtypeStruct((T, D), BF16)], args=[x, w], sem=("parallel",))[0]


def _rmsnorm_bwd_add(dres, dh, x, w, tm, name, want_bf16, jobs=()):
    T, D = x.shape

    def body(dres_ref, dh_ref, x_ref, w_ref, *outs):
        if want_bf16:
            dx_ref, dxb_ref, dw_ref = outs
        else:
            dx_ref, dw_ref = outs
        i = pl.program_id(0)
        xv = x_ref[...]
        r = lax.rsqrt(jnp.mean(xv * xv, axis=-1, keepdims=True) + RMS_EPS)
        xh = xv * r
        dh_v = dh_ref[...].astype(F32)
        dxh = dh_v * w_ref[...]
        dx = dres_ref[...] + r * (dxh - xh * jnp.mean(dxh * xh, axis=-1, keepdims=True))
        dx_ref[...] = dx
        if want_bf16:
            dxb_ref[...] = dx.astype(BF16)
        part = jnp.sum(dh_v * xh, axis=0, keepdims=True)

        @pl.when(i == 0)
        def _():
            dw_ref[...] = part

        @pl.when(i > 0)
        def _():
            dw_ref[...] += part

    tile = pl.BlockSpec((tm, D), lambda i: (i, 0))
    row = pl.BlockSpec((1, D), lambda i: (0, 0))
    out_specs = [tile] + ([tile] if want_bf16 else []) + [row]
    out_shape = ([jax.ShapeDtypeStruct((T, D), F32)] + ([jax.ShapeDtypeStruct((T, D), BF16)] if want_bf16 else [])
                 + [jax.ShapeDtypeStruct((1, D), F32)])
    return _pcall(body, name=name, grid=(T // tm,), in_specs=[tile, tile, tile, row], out_specs=out_specs,
                  out_shape=out_shape, args=[dres, dh, x, w], sem=("arbitrary",), jobs=jobs)


def _loss_head(x2, target, wf, tm, name):
    T, D = x2.shape

    def body(x_ref, t_ref, w_ref, dx_ref, dxb_ref, loss_ref, dw_ref):
        i = pl.program_id(0)
        xv = x_ref[...]
        r = lax.rsqrt(jnp.mean(xv * xv, axis=-1, keepdims=True) + RMS_EPS)
        xh = xv * r
        wv = w_ref[...]
        e = xh * wv - t_ref[...]
        lpart = 0.5 * jnp.sum(jnp.sum(e * e, axis=-1, keepdims=True), axis=0, keepdims=True) * (1.0 / D)
        dy = e * (1.0 / D)
        dxh = dy * wv
        dx = r * (dxh - xh * jnp.mean(dxh * xh, axis=-1, keepdims=True))
        dx_ref[...] = dx
        dxb_ref[...] = dx.astype(BF16)
        wpart = jnp.sum(dy * xh, axis=0, keepdims=True)
        lfull = jnp.broadcast_to(lpart, (8, LANES))

        @pl.when(i == 0)
        def _():
            loss_ref[...] = lfull
            dw_ref[...] = wpart

        @pl.when(i > 0)
        def _():
            loss_ref[...] += lfull
            dw_ref[...] += wpart

    tile = pl.BlockSpec((tm, D), lambda i: (i, 0))
    row = pl.BlockSpec((1, D), lambda i: (0, 0))
    return _pcall(
        body, name=name, grid=(T // tm,), in_specs=[tile, tile, row],
        out_specs=[tile, tile, pl.BlockSpec((8, LANES), lambda i: (0, 0)), row],
        out_shape=[jax.ShapeDtypeStruct((T, D), F32), jax.ShapeDtypeStruct((T, D), BF16),
                   jax.ShapeDtypeStruct((8, LANES), F32), jax.ShapeDtypeStruct((1, D), F32)],
        args=[x2, target, wf], sem=("arbitrary",))


def _rope_tables(pos_col, inv2, tm, name):
    T = pos_col.shape[0]

    def body(p_ref, f_ref, c_ref, s_ref):
        ang = p_ref[...] * f_ref[...]
        lane = lax.broadcasted_iota(jnp.int32, ang.shape, 1)
        c_ref[...] = jnp.cos(ang)
        s_ref[...] = jnp.where(lane < RET_DK // 2, -1.0, 1.0) * jnp.sin(ang)

    tile = pl.BlockSpec((tm, RET_DK), lambda i: (i, 0))
    return _pcall(
        body, name=name, grid=(T // tm,),
        in_specs=[pl.BlockSpec((tm, 1), lambda i: (i, 0)), pl.BlockSpec((1, RET_DK), lambda i: (0, 0))],
        out_specs=[tile, tile], out_shape=[jax.ShapeDtypeStruct((T, RET_DK), F32)] * 2,
        args=[pos_col, inv2], sem=("parallel",))


def _mix_fwd(a_in, b_in, proj, x, w_ro, w_lo, w_out, mb, w2, tm, name):
    T, D = x.shape

    def body(a_ref, b_ref, gr_ref, gl_ref, x_ref, wro_ref, wlo_ref, wout_ref, mb_ref, w2_ref,
             x1_ref, mix_ref, h2_ref):
        ya = _dot(a_ref[...], wro_ref[...], NN)
        yb = _dot(b_ref[...], wlo_ref[...], NN)
        sa = _sigmoid(gr_ref[...].astype(F32) + mb_ref[0:1, :])
        sb = _sigmoid(gl_ref[...].astype(F32) + mb_ref[1:2, :])
        mix = (sa * ya + sb * yb).astype(BF16)
        mix_ref[...] = mix
        x1 = x_ref[...] + _dot(mix, wout_ref[...], NN)
        x1_ref[...] = x1
        r = lax.rsqrt(jnp.mean(x1 * x1, axis=-1, keepdims=True) + RMS_EPS)
        h2_ref[...] = (x1 * r * w2_ref[...]).astype(BF16)

    tile = pl.BlockSpec((tm, D), lambda i: (i, 0))
    wspec = pl.BlockSpec((D, D), lambda i: (0, 0))
    return _pcall(
        body, name=name, grid=(T // tm,),
        in_specs=[tile, tile,
                  pl.BlockSpec((tm, D), lambda i: (i, COL_GR)), pl.BlockSpec((tm, D), lambda i: (i, COL_GL)),
                  tile, wspec, wspec, wspec,
                  pl.BlockSpec((2, D), lambda i: (0, 0)), pl.BlockSpec((1, D), lambda i: (0, 0))],
        out_specs=[tile, tile, tile],
        out_shape=[jax.ShapeDtypeStruct((T, D), F32), jax.ShapeDtypeStruct((T, D), BF16),
                   jax.ShapeDtypeStruct((T, D), BF16)],
        args=[a_in, b_in, proj, proj, x, w_ro, w_lo, w_out, mb, w2], sem=("parallel",))


def _write_pieces(dst_ref, sems, stashes, row0, col0s, ids, grid, compute):
    def aligned(v, m):
        return v if isinstance(v, int) else pl.multiple_of(v, m)

    def copies():
        return [pltpu.make_async_copy(
                    st, dst_ref.at[pl.ds(aligned(row0, 16), st.shape[0]), pl.ds(aligned(c0, LANES), st.shape[1])],
                    sems.at[k])
                for k, (st, c0) in enumerate(zip(stashes, col0s))]

    first = _all_true([i == 0 for i in ids])
    last = _all_true([i == g - 1 for i, g in zip(ids, grid)])

    @pl.when(jnp.logical_not(first))
    def _():
        for cp in copies():
            cp.wait()

    compute()
    for cp in copies():
        cp.start()

    @pl.when(last)
    def _():
        for cp in copies():
            cp.wait()


def _mix_bwd(dx1b, a_in, b_in, proj, w_ro, w_lo, w_out, mb, tm, name, jobs=()):
    T, D = a_in.shape
    grid = (T // tm,)

    def body(dx_ref, a_ref, b_ref, gr_ref, gl_ref, wro_ref, wlo_ref, wout_ref, mb_ref,
             da_ref, db_ref, dya_ref, dyb_ref, dp_ref, dmb_ref, dgr_s, dgl_s, wsem):
        i = pl.program_id(0)

        def compute():
            dmix = _dot(dx_ref[...], wout_ref[...], NT)
            ya = _dot(a_ref[...], wro_ref[...], NN)
            yb = _dot(b_ref[...], wlo_ref[...], NN)
            sa = _sigmoid(gr_ref[...].astype(F32) + mb_ref[0:1, :])
            sb = _sigmoid(gl_ref[...].astype(F32) + mb_ref[1:2, :])
            dya = (dmix * sa).astype(BF16)
            dyb = (dmix * sb).astype(BF16)
            dgr = dmix * ya * sa * (1.0 - sa)
            dgl = dmix * yb * sb * (1.0 - sb)
            dya_ref[...] = dya
            dyb_ref[...] = dyb
            dgr_s[...] = dgr.astype(BF16)
            dgl_s[...] = dgl.astype(BF16)
            da_ref[...] = _dot(dya, wro_ref[...], NT).astype(BF16)
            db_ref[...] = _dot(dyb, wlo_ref[...], NT).astype(BF16)

            @pl.when(i == 0)
            def _():
                dmb_ref[...] = jnp.zeros_like(dmb_ref)

            dmb_ref[0:1, :] += jnp.sum(dgr, axis=0, keepdims=True)
            dmb_ref[1:2, :] += jnp.sum(dgl, axis=0, keepdims=True)

        _write_pieces(dp_ref, wsem, [dgr_s, dgl_s], i * tm, [COL_GR * D, COL_GL * D], [i], grid, compute)

    tile = pl.BlockSpec((tm, D), lambda i: (i, 0))
    wspec = pl.BlockSpec((D, D), lambda i: (0, 0))
    two = pl.BlockSpec((2, D), lambda i: (0, 0))
    return _pcall(
        body, name=name, grid=grid,
        in_specs=[tile, tile, tile,
                  pl.BlockSpec((tm, D), lambda i: (i, COL_GR)), pl.BlockSpec((tm, D), lambda i: (i, COL_GL)),
                  wspec, wspec, wspec, two],
        out_specs=[tile] * 4 + [HBM_SPEC, two],
        out_shape=[jax.ShapeDtypeStruct((T, D), BF16)] * 4
                  + [jax.ShapeDtypeStruct((T, D_IN), BF16), jax.ShapeDtypeStruct((2, D), F32)],
        args=[dx1b, a_in, b_in, proj, proj, w_ro, w_lo, w_out, mb],
        scratch=[pltpu.VMEM((tm, D), BF16), pltpu.VMEM((tm, D), BF16), pltpu.SemaphoreType.DMA((2,))],
        sem=("arbitrary",), jobs=jobs)


def _ret_decay_consts(lg):
    L = RET_BLOCK
    n = lax.broadcasted_iota(jnp.int32, (L, L), 0)
    m = lax.broadcasted_iota(jnp.int32, (L, L), 1)
    cn, cm = n // CHUNK, m // CHUNK
    expo = jnp.where(cn == cm, jnp.abs(n - m), n - m).astype(F32)
    wm = jnp.where(cm <= cn, jnp.exp(lg * expo), 0.0)
    idx = lax.broadcasted_iota(jnp.int32, (L, 1), 0).astype(F32)
    qd = jnp.exp(lg * (idx + 1.0))
    kd = jnp.exp(lg * (L - 1.0 - idx))
    bd = jnp.exp(lg * float(L))
    return wm, qd, kd, bd


def _rotate(v, cos2, sin2s):
    return v * cos2 + pltpu.roll(v, RET_DK // 2, 1) * sin2s


def _rotate_t(d, cos2, sin2s):
    return d * cos2 - pltpu.roll(d, RET_DK // 2, 1) * sin2s


def _retention_fwd(proj, cos2, sin2s, lgam, gn_w, B, S, name, jobs=()):
    T = B * S
    nb = S // RET_BLOCK
    scale = RET_DK ** -0.5

    def body(q_ref, k_ref, v_ref, g_ref, c_ref, s_ref, lg_ref, gw_ref, o_ref, a_ref, qr, kr, st):
        wm, qd, kd, bd = _ret_decay_consts(lg_ref[0:1, 0:1])
        cos2, sin2s = c_ref[...], s_ref[...]
        qr[...] = _rotate(q_ref[...].astype(F32), cos2, sin2s)
        kr[...] = _rotate(k_ref[...].astype(F32), cos2, sin2s) * scale
        st[...] = jnp.zeros_like(st)
        gw = gw_ref[...]
        for j in range(nb):
            rows = pl.ds(j * RET_BLOCK, RET_BLOCK)
            qb = qr[rows, :]
            kb = kr[rows, :]
            vb = v_ref[rows, :].astype(BF16)
            sc = _dot(qb.astype(BF16), kb.astype(BF16), NT) * wm
            o = _dot(sc.astype(BF16), vb, NN) + _dot((qb * qd).astype(BF16), st[...].astype(BF16), NN)
            st[...] = st[...] * bd + _dot((kb * kd).astype(BF16), vb, TN)
            o_ref[rows, :] = o
            mu = jnp.mean(o, axis=-1, keepdims=True)
            oc = o - mu
            var = jnp.mean(oc * oc, axis=-1, keepdims=True)
            y = oc * lax.rsqrt(var + GN_EPS) * gw
            g = g_ref[rows, :].astype(F32)
            a_ref[rows, :] = (y * (g * _sigmoid(g))).astype(BF16)

    blk = lambda w, off: pl.BlockSpec((S, w), lambda b, h: (b, off + h))
    return _pcall(
        body, name=name, grid=(B, RET_HEADS),
        in_specs=[blk(RET_DK, COL_Q), blk(RET_DK, COL_K), blk(RET_DV, COL_V), blk(RET_DV, COL_G),
                  pl.BlockSpec((S, RET_DK), lambda b, h: (b, 0)), pl.BlockSpec((S, RET_DK), lambda b, h: (b, 0)),
                  pl.BlockSpec((None, 8, LANES), lambda b, h: (h, 0, 0)),
                  pl.BlockSpec((1, RET_DV), lambda b, h: (0, h))],
        out_specs=[blk(RET_DV, 0), blk(RET_DV, 0)],
        out_shape=[jax.ShapeDtypeStruct((T, RET_HEADS * RET_DV), F32),
                   jax.ShapeDtypeStruct((T, RET_HEADS * RET_DV), BF16)],
        args=[proj, proj, proj, proj, cos2, sin2s, lgam, gn_w],
        scratch=[pltpu.VMEM((S, RET_DK), F32), pltpu.VMEM((S, RET_DK), F32), pltpu.VMEM((RET_DK, RET_DV), F32)],
        sem=("parallel", "parallel"), jobs=jobs)


def _retention_bwd(da_in, o, proj, dproj, cos2, sin2s, lgam, gn_w, B, S, name, jobs=()):
    T = B * S
    nb = S // RET_BLOCK
    scale = RET_DK ** -0.5
    grid = (RET_HEADS, B)

    def body(da_ref, o_ref, q_ref, k_ref, v_ref, g_ref, c_ref, s_ref, lg_ref, gw_ref, _, dp_ref, dgw_ref,
             qr, kr, do_s, sts, rst, dq_s, dk_s, dv_s, dg_s, wsem):
        h, b = pl.program_id(0), pl.program_id(1)

        def compute():
            wm, qd, kd, bd = _ret_decay_consts(lg_ref[0:1, 0:1])
            cos2, sin2s = c_ref[...], s_ref[...]
            qr[...] = _rotate(q_ref[...].astype(F32), cos2, sin2s)
            kr[...] = _rotate(k_ref[...].astype(F32), cos2, sin2s) * scale
            gw = gw_ref[...]
            st = jnp.zeros((RET_DK, RET_DV), F32)
            dgw = jnp.zeros((1, RET_DV), F32)
            for j in range(nb):
                rows = pl.ds(j * RET_BLOCK, RET_BLOCK)
                ov = o_ref[rows, :]
                mu = jnp.mean(ov, axis=-1, keepdims=True)
                oc = ov - mu
                rstd = lax.rsqrt(jnp.mean(oc * oc, axis=-1, keepdims=True) + GN_EPS)
                y = oc * rstd
                g = g_ref[rows, :].astype(F32)
                sg = _sigmoid(g)
                da = da_ref[rows, :].astype(F32)
                dg_s[rows, :] = (da * (y * gw) * (sg * (1.0 + g * (1.0 - sg)))).astype(BF16)
                dyw = da * (g * sg)
                dgw = dgw + jnp.sum(dyw * y, axis=0, keepdims=True)
                dy = dyw * gw
                do_s[rows, :] = rstd * (dy - jnp.mean(dy, axis=-1, keepdims=True)
                                        - y * jnp.mean(dy * y, axis=-1, keepdims=True))
                sts[j] = st
                st = st * bd + _dot((kr[rows, :] * kd).astype(BF16), v_ref[rows, :].astype(BF16), TN)

            @pl.when(b == 0)
            def _():
                dgw_ref[...] = dgw

            @pl.when(b > 0)
            def _():
                dgw_ref[...] += dgw

            rst[...] = jnp.zeros_like(rst)
            for j in reversed(range(nb)):
                rows = pl.ds(j * RET_BLOCK, RET_BLOCK)
                qb = qr[rows, :]
                kb = kr[rows, :]
                qbb, kbb = qb.astype(BF16), kb.astype(BF16)
                vb = v_ref[rows, :].astype(BF16)
                dob = do_s[rows, :]
                dobb = dob.astype(BF16)
                a_m = (_dot(qbb, kbb, NT) * wm).astype(BF16)
                b_m = (_dot(dobb, vb, NT) * wm).astype(BF16)
                rb = rst[...].astype(BF16)
                dq = _dot(b_m, kbb, NN) + _dot((dob * qd).astype(BF16), sts[j].astype(BF16), NT)
                dk = _dot(b_m, qbb, TN) + kd * _dot(vb, rb, NT)
                dv = _dot(a_m, dobb, TN) + kd * _dot(kbb, rb, NN)
                rst[...] = rst[...] * bd + _dot((qb * qd).astype(BF16), dobb, TN)
                cb, sb = c_ref[rows, :], s_ref[rows, :]
                dq_s[rows, :] = _rotate_t(dq, cb, sb).astype(BF16)
                dk_s[rows, :] = _rotate_t(dk * scale, cb, sb).astype(BF16)
                dv_s[rows, :] = dv.astype(BF16)

        cols = [(COL_Q + h) * RET_DK, (COL_K + h) * RET_DK, (COL_V + h) * RET_DV, (COL_G + h) * RET_DV]
        _write_pieces(dp_ref, wsem, [dq_s, dk_s, dv_s, dg_s], b * S, cols, [h, b], grid, compute)

    blk = lambda w, off: pl.BlockSpec((S, w), lambda h, b: (b, off + h))
    return _pcall(
        body, name=name, grid=grid,
        in_specs=[blk(RET_DV, 0), blk(RET_DV, 0),
                  blk(RET_DK, COL_Q), blk(RET_DK, COL_K), blk(RET_DV, COL_V), blk(RET_DV, COL_G),
                  pl.BlockSpec((S, RET_DK), lambda h, b: (b, 0)), pl.BlockSpec((S, RET_DK), lambda h, b: (b, 0)),
                  pl.BlockSpec((None, 8, LANES), lambda h, b: (h, 0, 0)),
                  pl.BlockSpec((1, RET_DV), lambda h, b: (0, h)), HBM_SPEC],
        out_specs=[HBM_SPEC, pl.BlockSpec((1, RET_DV), lambda h, b: (0, h))],
        out_shape=[jax.ShapeDtypeStruct(dproj.shape, dproj.dtype),
                   jax.ShapeDtypeStruct((1, RET_HEADS * RET_DV), F32)],
        args=[da_in, o, proj, proj, proj, proj, cos2, sin2s, lgam, gn_w, dproj],
        scratch=[pltpu.VMEM((S, RET_DK), F32), pltpu.VMEM((S, RET_DK), F32),
                 pltpu.VMEM((S, RET_DV), F32), pltpu.VMEM((nb, RET_DK, RET_DV), F32),
                 pltpu.VMEM((RET_DK, RET_DV), F32),
                 pltpu.VMEM((S, RET_DK), BF16), pltpu.VMEM((S, RET_DK), BF16),
                 pltpu.VMEM((S, RET_DV), BF16), pltpu.VMEM((S, RET_DV), BF16), pltpu.SemaphoreType.DMA((4,))],
        sem=("arbitrary", "arbitrary"), jobs=jobs, alias_in_out={10: 0})


def _lru_gates(x, cw, cb, wr, wi, br, bi, lam):
    xc = cb + cw[LRU_CONV - 1:LRU_CONV, :] * x
    for j in range(LRU_CONV - 1):
        xc = xc + cw[j:j + 1, :] * _shift_down(x, LRU_CONV - 1 - j, 0.0)
    xcb = xc.astype(BF16)
    r = _sigmoid(_dot(xcb, wr, NN) + br)
    ig = _sigmoid(_dot(xcb, wi, NN) + bi)
    z = -lam
    sp = jnp.maximum(z, 0.0) + jnp.log1p(jnp.exp(-jnp.abs(z)))
    log_a = (-LRU_C) * r * sp
    a = jnp.exp(log_a)
    z2 = 2.0 * log_a
    taylor = -z2 * (1.0 + z2 * (0.5 + z2 * (1.0 / 6.0 + z2 * (1.0 / 24.0 + z2 * (1.0 / 120.0)))))
    om = jnp.where(z2 > -0.05, taylor, 1.0 - jnp.exp(z2))
    sq = jnp.sqrt(om)
    return xc, xcb, r, ig, sp, a, sq


def _lru_fwd(proj, cw, cb, wr, wi, br, bi, lam, B, S, name):
    T = B * S
    W = LRU_BLOCKS * LRU_BLOCK

    def body(x_ref, y_ref, cw_ref, cb_ref, wr_ref, wi_ref, br_ref, bi_ref, lam_ref, h_ref, bin_ref, a_s, b_s):
        xc, _, _, ig, _, a, sq = _lru_gates(x_ref[...].astype(F32), cw_ref[...], cb_ref[...], wr_ref[...], wi_ref[...],
                                           br_ref[...], bi_ref[...], lam_ref[...])
        a_s[...] = a
        b_s[...] = sq * ig * xc
        _scan_forward_ref(a_s, b_s, h_ref)
        bin_ref[...] = (h_ref[...] * _gelu(y_ref[...].astype(F32))).astype(BF16)

    blk = lambda off: pl.BlockSpec((S, LRU_BLOCK), lambda b, n: (b, off + n))
    vec = lambda rows: pl.BlockSpec((rows, LRU_BLOCK), lambda b, n: (0, n))
    wspec = pl.BlockSpec((None, LRU_BLOCK, LRU_BLOCK), lambda b, n: (n, 0, 0))
    return _pcall(
        body, name=name, grid=(B, LRU_BLOCKS),
        in_specs=[blk(COL_XL), blk(COL_YL), vec(LRU_CONV), vec(1), wspec, wspec, vec(1), vec(1), vec(1)],
        out_specs=[blk(0), blk(0)],
        out_shape=[jax.ShapeDtypeStruct((T, W), F32), jax.ShapeDtypeStruct((T, W), BF16)],
        args=[proj, proj, cw, cb, wr, wi, br, bi, lam],
        scratch=[pltpu.VMEM((S, LRU_BLOCK), F32), pltpu.VMEM((S, LRU_BLOCK), F32)], sem=("parallel", "parallel"))


def _lru_bwd(db_in, h, proj, dproj, cw, cb, wr, wi, br, bi, lam, B, S, name, jobs=()):
    T = B * S
    W = LRU_BLOCKS * LRU_BLOCK

    grid = (LRU_BLOCKS, B)

    def body(dbin_ref, h_ref, x_ref, y_ref, cw_ref, cb_ref, wr_ref, wi_ref, br_ref, bi_ref, lam_ref, _,
             dp_ref, dcw_ref, dcb_ref, dwr_ref, dwi_ref, dbr_ref, dbi_ref, dlam_ref, dx_s, dy_s, wsem,
             an_s, u_s, dh_s):
        n, b = pl.program_id(0), pl.program_id(1)

        def compute():
            x = x_ref[...].astype(F32)
            cw = cw_ref[...]
            wr, wi = wr_ref[...], wi_ref[...]
            lam = lam_ref[...]
            xc, xcb, r, ig, sp, a, sq = _lru_gates(x, cw, cb_ref[...], wr, wi, br_ref[...], bi_ref[...], lam)
            hv = h_ref[...]
            gel, dgel = _gelu_and_grad(y_ref[...].astype(F32))
            dbin = dbin_ref[...].astype(F32)
            dy_s[...] = (dbin * hv * dgel).astype(BF16)
            an_s[...] = _shift_up(a, 1, 0.0)
            u_s[...] = dbin * gel
            _scan_backward_ref(an_s, u_s, dh_s)
            dh = dh_s[...]
            hprev = _shift_down(hv, 1, 0.0)
            d_ig = dh * sq * xc
            d_xc = dh * sq * ig
            a2 = a * a
            d_loga = dh * hprev * a - dh * ig * xc * a2 / sq
            d_r = d_loga * ((-LRU_C) * sp)
            d_sp = jnp.sum(d_loga * ((-LRU_C) * r), axis=0, keepdims=True)
            dlam = -d_sp * _sigmoid(-lam)
            d_pr = d_r * r * (1.0 - r)
            d_pi = d_ig * ig * (1.0 - ig)
            d_prb, d_pib = d_pr.astype(BF16), d_pi.astype(BF16)
            d_xc = d_xc + _dot(d_prb, wr, NT) + _dot(d_pib, wi, NT)

            @pl.when(b == 0)
            def _():
                for ref in (dcw_ref, dcb_ref, dwr_ref, dwi_ref, dbr_ref, dbi_ref, dlam_ref):
                    ref[...] = jnp.zeros_like(ref)

            dx = cw[LRU_CONV - 1:LRU_CONV, :] * d_xc
            for j in range(LRU_CONV - 1):
                sft = LRU_CONV - 1 - j
                dx = dx + cw[j:j + 1, :] * _shift_up(d_xc, sft, 0.0)
                dcw_ref[j:j + 1, :] += jnp.sum(d_xc * _shift_down(x, sft, 0.0), axis=0, keepdims=True)
            dcw_ref[LRU_CONV - 1:LRU_CONV, :] += jnp.sum(d_xc * x, axis=0, keepdims=True)
            dx_s[...] = dx.astype(BF16)
            dcb_ref[...] += jnp.sum(d_xc, axis=0, keepdims=True)
            dwr_ref[...] += _dot(xcb, d_prb, TN)
            dwi_ref[...] += _dot(xcb, d_pib, TN)
            dbr_ref[...] += jnp.sum(d_pr, axis=0, keepdims=True)
            dbi_ref[...] += jnp.sum(d_pi, axis=0, keepdims=True)
            dlam_ref[...] += dlam

        cols = [(COL_XL + n) * LRU_BLOCK, (COL_YL + n) * LRU_BLOCK]
        _write_pieces(dp_ref, wsem, [dx_s, dy_s], b * S, cols, [n, b], grid, compute)

    blk = lambda off: pl.BlockSpec((S, LRU_BLOCK), lambda n, b: (b, off + n))
    vec = lambda rows: pl.BlockSpec((rows, LRU_BLOCK), lambda n, b: (0, n))
    wspec = pl.BlockSpec((None, LRU_BLOCK, LRU_BLOCK), lambda n, b: (n, 0, 0))
    vshape = lambda rows: jax.ShapeDtypeStruct((rows, W), F32)
    wshape = jax.ShapeDtypeStruct((LRU_BLOCKS, LRU_BLOCK, LRU_BLOCK), F32)
    return _pcall(
        body, name=name, grid=grid,
        in_specs=[blk(0), blk(0), blk(COL_XL), blk(COL_YL), vec(LRU_CONV), vec(1), wspec, wspec, vec(1), vec(1),
                  vec(1), HBM_SPEC],
        out_specs=[HBM_SPEC, vec(LRU_CONV), vec(1), wspec, wspec, vec(1), vec(1), vec(1)],
        out_shape=[jax.ShapeDtypeStruct(dproj.shape, dproj.dtype),
                   vshape(LRU_CONV), vshape(1), wshape, wshape, vshape(1), vshape(1), vshape(1)],
        args=[db_in, h, proj, proj, cw, cb, wr, wi, br, bi, lam, dproj],
        scratch=[pltpu.VMEM((S, LRU_BLOCK), BF16), pltpu.VMEM((S, LRU_BLOCK), BF16), pltpu.SemaphoreType.DMA((2,)),
                 pltpu.VMEM((S, LRU_BLOCK), F32), pltpu.VMEM((S, LRU_BLOCK), F32), pltpu.VMEM((S, LRU_BLOCK), F32)],
        sem=("arbitrary", "arbitrary"), jobs=jobs, alias_in_out={11: 0})


FFN_CT = 256


def _ffn_conv(gate, cw, cb):
    gc = cb + cw[FFN_CONV - 1:FFN_CONV, :] * gate
    for j in range(FFN_CONV - 1):
        gc = gc + cw[j:j + 1, :] * _shift_down(gate, FFN_CONV - 1 - j, 0.0)
    return gc


def _ffn_act_fwd(up, cw, cb, B, S, name):
    T = B * S
    nct = D_FF // FFN_CT

    def body(g_ref, v_ref, cw_ref, cb_ref, f_ref):
        gc = _ffn_conv(g_ref[...].astype(F32), cw_ref[...], cb_ref[...])
        f_ref[...] = (_gelu(gc) * v_ref[...].astype(F32)).astype(BF16)

    return _pcall(
        body, name=name, grid=(B, nct),
        in_specs=[pl.BlockSpec((S, FFN_CT), lambda b, c: (b, c)), pl.BlockSpec((S, FFN_CT), lambda b, c: (b, nct + c)),
                  pl.BlockSpec((FFN_CONV, FFN_CT), lambda b, c: (0, c)), pl.BlockSpec((1, FFN_CT), lambda b, c: (0, c))],
        out_specs=[pl.BlockSpec((S, FFN_CT), lambda b, c: (b, c))],
        out_shape=[jax.ShapeDtypeStruct((T, D_FF), BF16)], args=[up, up, cw, cb], sem=("parallel", "parallel"))[0]


def _ffn_act_bwd(df, up, cw, cb, B, S, name, jobs=()):
    T = B * S
    nct = D_FF // FFN_CT

    grid = (nct, B)

    def body(df_ref, g_ref, v_ref, cw_ref, cb_ref, du_ref, dcw_ref, dcb_ref, dg_s, dv_s, wsem):
        c, b = pl.program_id(0), pl.program_id(1)

        def compute():
            gate = g_ref[...].astype(F32)
            cw = cw_ref[...]
            gc = _ffn_conv(gate, cw, cb_ref[...])
            gel, dgel = _gelu_and_grad(gc)
            dfv = df_ref[...].astype(F32)
            dv_s[...] = (dfv * gel).astype(BF16)
            dgc = dfv * v_ref[...].astype(F32) * dgel

            @pl.when(b == 0)
            def _():
                dcw_ref[...] = jnp.zeros_like(dcw_ref)
                dcb_ref[...] = jnp.zeros_like(dcb_ref)

            dgate = cw[FFN_CONV - 1:FFN_CONV, :] * dgc
            for j in range(FFN_CONV - 1):
                sft = FFN_CONV - 1 - j
                dgate = dgate + cw[j:j + 1, :] * _shift_up(dgc, sft, 0.0)
                dcw_ref[j:j + 1, :] += jnp.sum(dgc * _shift_down(gate, sft, 0.0), axis=0, keepdims=True)
            dcw_ref[FFN_CONV - 1:FFN_CONV, :] += jnp.sum(dgc * gate, axis=0, keepdims=True)
            dg_s[...] = dgate.astype(BF16)
            dcb_ref[...] += jnp.sum(dgc, axis=0, keepdims=True)

        _write_pieces(du_ref, wsem, [dg_s, dv_s], b * S, [c * FFN_CT, (nct + c) * FFN_CT], [c, b], grid, compute)

    blk = pl.BlockSpec((S, FFN_CT), lambda c, b: (b, c))
    return _pcall(
        body, name=name, grid=grid,
        in_specs=[blk, blk, pl.BlockSpec((S, FFN_CT), lambda c, b: (b, nct + c)),
                  pl.BlockSpec((FFN_CONV, FFN_CT), lambda c, b: (0, c)),
                  pl.BlockSpec((1, FFN_CT), lambda c, b: (0, c))],
        out_specs=[HBM_SPEC, pl.BlockSpec((FFN_CONV, FFN_CT), lambda c, b: (0, c)),
                   pl.BlockSpec((1, FFN_CT), lambda c, b: (0, c))],
        out_shape=[jax.ShapeDtypeStruct((T, 2 * D_FF), BF16),
                   jax.ShapeDtypeStruct((FFN_CONV, D_FF), F32), jax.ShapeDtypeStruct((1, D_FF), F32)],
        args=[df, up, up, cw, cb],
        scratch=[pltpu.VMEM((S, FFN_CT), BF16), pltpu.VMEM((S, FFN_CT), BF16), pltpu.SemaphoreType.DMA((2,))],
        sem=("arbitrary", "arbitrary"), jobs=jobs)


def _rs_add(g, recv, mode, core, name):
    shard = tuple(recv.shape[1:])
    if mode == "mid":
        a, e, c2 = shard
        g_in = g.reshape(a, N_DEV, e, c2)
        grid = (4, 1)
        g_spec = pl.BlockSpec((a, None, e, c2), lambda k, i, c_ref: (0, 2 * k + c_ref[0], 0, 0))
        r_spec = pl.BlockSpec((None, a, e, c2), lambda k, i, c_ref: (k, 0, 0, 0))
    else:
        R, C = shard
        tr = _row_tile(R, 512)
        grid = (4, R // tr)
        if mode == "rows":
            g_in = g.reshape(N_DEV, R, C)
            g_spec = pl.BlockSpec((None, tr, C), lambda k, i, c_ref: (2 * k + c_ref[0], i, 0))
        else:
            g_in = g
            g_spec = pl.BlockSpec((tr, C), lambda k, i, c_ref: (i, 2 * k + c_ref[0]))
        r_spec = pl.BlockSpec((None, tr, C), lambda k, i, c_ref: (k, i, 0))

    def body(c_ref, g_ref, r_ref, o_ref):
        o_ref[...] = g_ref[...] + r_ref[...]

    return pl.pallas_call(
        body, name=name,
        grid_spec=pltpu.PrefetchScalarGridSpec(num_scalar_prefetch=1, grid=grid, in_specs=[g_spec, r_spec],
                                               out_specs=r_spec),
        out_shape=jax.ShapeDtypeStruct(recv.shape, recv.dtype),
        compiler_params=pltpu.CompilerParams(dimension_semantics=("parallel", "parallel"),
                                             vmem_limit_bytes=VMEM_LIMIT),
    )(core, g_in, recv)


def _adam_update(gv, w, m, v):
    nm = ADAM_B1 * m + (1.0 - ADAM_B1) * gv
    nv = ADAM_B2 * v + (1.0 - ADAM_B2) * (gv * gv)
    m_hat = nm / (1.0 - ADAM_B1 ** ADAM_STEP)
    v_hat = nv / (1.0 - ADAM_B2 ** ADAM_STEP)
    delta = -ADAM_LR * (m_hat / (jnp.sqrt(v_hat) + ADAM_EPS) + ADAM_WD * w)
    return delta, nm, nv


def _adamw_shard(partial, recv, w, m, v, chip, name):
    shape = tuple(w.shape)
    tr = _row_tile(shape[0], 256)
    rest = shape[1:]
    zeros = (0,) * len(rest)
    tile = pl.BlockSpec((tr,) + rest, lambda i, s: (i,) + zeros)

    def body(_, p_ref, r_ref, w_ref, m_ref, v_ref, g_ref, d_ref, nm_ref, nv_ref):
        gv = p_ref[...] + r_ref[0] + r_ref[1] + r_ref[2]
        g_ref[...] = gv
        d_ref[...], nm_ref[...], nv_ref[...] = _adam_update(gv, w_ref[...], m_ref[...], v_ref[...])

    grid_spec = pltpu.PrefetchScalarGridSpec(
        num_scalar_prefetch=1, grid=(shape[0] // tr,),
        in_specs=[pl.BlockSpec((None, tr) + rest, lambda i, s: (s[0], i) + zeros),
                  pl.BlockSpec((3, tr) + rest, lambda i, s: (0, i) + zeros), tile, tile, tile],
        out_specs=[tile] * 4)
    return pl.pallas_call(
        body, name=name, grid_spec=grid_spec, out_shape=[jax.ShapeDtypeStruct(shape, F32)] * 4,
        compiler_params=pltpu.CompilerParams(dimension_semantics=("parallel",), vmem_limit_bytes=VMEM_LIMIT),
    )(chip, partial, recv, w, m, v)


def _all_gather_multi(shards, modes, name):
    n = len(shards)
    extents = [_extent(s.shape, m) for s, m in zip(shards, modes)]

    def body(*refs):
        x_refs, out_refs = refs[:n], refs[n:2 * n]
        send_sems, recv_sems, local_sems = refs[2 * n:]
        x, y, c = _mesh_pos()
        me, sibling = (x, y, c), (x, y, 1 - c)
        chips = _other_chips(x, y)

        def slot(i, px, py, pc):
            return _window(out_refs[i], modes[i], extents[i], 4 * px + 2 * py + pc)

        def copy(i, k, block, to, src=None):
            return _remote(slot(i, *block) if src is None else src, slot(i, *block),
                           send_sems.at[i, k], recv_sems.at[i, k], to)

        mine = [pltpu.make_async_copy(x_refs[i], slot(i, *me), local_sems.at[i]) for i in range(n)]
        sends = []
        for i in range(n):
            mine[i].start()
            first = [copy(i, 0, me, sibling, src=x_refs[i])]
            first += [copy(i, 1 + j, me, (*chip, c), src=x_refs[i]) for j, chip in enumerate(chips)]
            for cp in first:
                cp.start()
            sends += first
        for i in range(n):
            for j, chip in enumerate(chips):
                copy(i, 1 + j, (*chip, c), me).wait_recv()
                fwd = copy(i, 4 + j, (*chip, c), sibling)
                fwd.start()
                sends.append(fwd)
        for i in range(n):
            copy(i, 0, sibling, me).wait_recv()
            for j, chip in enumerate(chips):
                copy(i, 4 + j, (*chip, 1 - c), me).wait_recv()
        for cp in sends:
            cp.wait_send()
        for cp in mine:
            cp.wait()

    return pl.pallas_call(
        body, name=name,
        in_specs=[HBM_SPEC] * n, out_specs=[HBM_SPEC] * n,
        out_shape=[jax.ShapeDtypeStruct(_full_shape(s.shape, m), s.dtype) for s, m in zip(shards, modes)],
        scratch_shapes=[pltpu.SemaphoreType.DMA((n, 7)), pltpu.SemaphoreType.DMA((n, 7)),
                        pltpu.SemaphoreType.DMA((n,))],
    )(*shards)


SMALL_LANES = 1024


def _small_rows(shape):
    r, w = shape
    return r * max(1, w // SMALL_LANES)


def _small_allreduce(parts, name):
    n = len(parts)
    shapes = [tuple(p.shape) for p in parts]
    offs, total = [], 0
    for s in shapes:
        offs.append(total)
        total += _small_rows(s)
    rows = -(-total // 8) * 8

    def body(*refs):
        p_refs, o_refs = refs[:n], refs[n:2 * n]
        buf, tot, send_sems, recv_sems = refs[2 * n:]
        x, y, c = _mesh_pos()
        me, sibling = (x, y, c), (x, y, 1 - c)
        chips = _other_chips(x, y)

        def slot(px, py, pc):
            return buf.at[4 * px + 2 * py + pc]

        def copy(k, block, to):
            return _remote(slot(*block), slot(*block), send_sems.at[k], recv_sems.at[k], to)

        tot[...] = jnp.zeros_like(tot)
        for p_ref, (r, w), off in zip(p_refs, shapes, offs):
            wl = min(w, SMALL_LANES)
            for part in range(max(1, w // SMALL_LANES)):
                tot[pl.ds(off + part * r, r), pl.ds(0, wl)] = p_ref[:, pl.ds(part * SMALL_LANES, wl)]
        buf[4 * x + 2 * y + c] = tot[...]
        first = [copy(0, me, sibling)] + [copy(1 + j, me, (*chip, c)) for j, chip in enumerate(chips)]
        for cp in first:
            cp.start()
        passed = [copy(4 + j, (*chip, c), sibling) for j, chip in enumerate(chips)]
        for j, chip in enumerate(chips):
            copy(1 + j, (*chip, c), me).wait_recv()
            passed[j].start()
        copy(0, sibling, me).wait_recv()
        for j, chip in enumerate(chips):
            copy(4 + j, (*chip, 1 - c), me).wait_recv()
        for cp in first + passed:
            cp.wait_send()
        acc = buf[0]
        for d in range(1, N_DEV):
            acc = acc + buf[d]
        tot[...] = acc
        for o_ref, (r, w), off in zip(o_refs, shapes, offs):
            wl = min(w, SMALL_LANES)
            for part in range(max(1, w // SMALL_LANES)):
                o_ref[:, pl.ds(part * SMALL_LANES, wl)] = tot[pl.ds(off + part * r, r), pl.ds(0, wl)]

    vm = pl.BlockSpec(memory_space=pltpu.VMEM)
    return pl.pallas_call(
        body, name=name,
        in_specs=[vm] * n, out_specs=[vm] * n,
        out_shape=[jax.ShapeDtypeStruct(s, F32) for s in shapes],
        scratch_shapes=[pltpu.VMEM((N_DEV, rows, SMALL_LANES), F32), pltpu.VMEM((rows, SMALL_LANES), F32),
                        pltpu.SemaphoreType.DMA((7,)), pltpu.SemaphoreType.DMA((7,))],
    )(*parts)


def _adamw_small(gs, ws, ms, vs, name):
    n = len(gs)

    def body(*refs):
        g_r, w_r, m_r, v_r = refs[:n], refs[n:2 * n], refs[2 * n:3 * n], refs[3 * n:4 * n]
        d_r, nm_r, nv_r = refs[4 * n:5 * n], refs[5 * n:6 * n], refs[6 * n:7 * n]
        for i in range(n):
            d_r[i][...], nm_r[i][...], nv_r[i][...] = _adam_update(g_r[i][...], w_r[i][...], m_r[i][...], v_r[i][...])

    vm = pl.BlockSpec(memory_space=pltpu.VMEM)
    shapes = [jax.ShapeDtypeStruct(w.shape, F32) for w in ws]
    outs = pl.pallas_call(body, name=name, in_specs=[vm] * (4 * n), out_specs=[vm] * (3 * n),
                          out_shape=shapes * 3)(*gs, *ws, *ms, *vs)
    return outs[:n], outs[n:2 * n], outs[2 * n:]


FIRST = [("w_in", (1024, 896), "cols"), ("lru_w_r", (4, 32, 256), "mid"), ("lru_w_i", (4, 32, 256), "mid")]
LATE = [("w_ret_o", (128, 1024), "rows"), ("w_lru_o", (128, 1024), "rows"), ("w_out", (128, 1024), "rows"),
        ("ffn_w_up", (1024, 768), "cols"), ("ffn_w_down", (384, 1024), "rows")]
BIG = FIRST + LATE
SMALL_SHARDED = [("merge_gate_b", (2, 128), "cols"), ("lru_conv_w", (4, 128), "cols"), ("lru_b_r", (4, 32), "stack"),
                 ("lru_b_i", (4, 32), "stack"), ("ffn_conv_w", (3, 384), "cols")]
REPLICATED = [("norm1_w", (1, 1024)), ("ret_gn_w", (1, 1024)), ("lru_conv_b", (1, 1024)), ("lru_lambda", (1, 1024)),
              ("norm2_w", (1, 1024)), ("ffn_conv_b", (1, 3072)), ("norm_f_w", (1, 1024))]
MODE = {n: m for n, _, m in BIG}
SHARD = {n: s for n, s, _ in BIG}


def _local_step(x3, positions, target3, wb, ws, late_shards, core):
    B, S, D = x3.shape
    T = B * S
    x = x3.reshape(T, D)
    target = target3.reshape(T, D)
    tm = min(512, T)
    big = min(1024, T)
    big2 = min(2048, T)

    half = RET_DK // 2
    inv_freq = ROPE_BASE ** (-jnp.arange(half, dtype=F32) / half)
    inv2 = jnp.concatenate([inv_freq, inv_freq]).reshape(1, RET_DK)
    log_gamma = jnp.log1p(-jnp.power(2.0, -5.0 - jnp.arange(RET_HEADS, dtype=F32)))
    lgam = jnp.broadcast_to(log_gamma[:, None, None], (RET_HEADS, 8, LANES))
    pos_col = positions.astype(F32).reshape(T, 1)
    cos2, sin2s = _rope_tables(pos_col, inv2, tm, "rope_tables")

    late_names = [n for n, _, _ in LATE]
    late_modes = [m for _, _, m in LATE]
    late_shapes = [s for _, s, _ in LATE]

    h1 = _rmsnorm_fwd(x, ws["norm1_w"], tm, "norm1_fwd")
    proj, *late_part = _matmul(h1, wb["w_in"], "nn", BF16, big2, 1024, 1024, "proj_fwd",
                               jobs=[_ag_first_job(late_shards, late_modes)])
    o, a_in, *late_full = _retention_fwd(proj, cos2, sin2s, lgam, ws["ret_gn_w"], B, S, "retention_fwd",
                                         jobs=[_ag_second_job(late_part, late_modes, late_shapes)])
    wb = dict(wb, **dict(zip(late_names, late_full)))
    hl, b_in = _lru_fwd(proj, ws["lru_conv_w"], ws["lru_conv_b"], wb["lru_w_r"], wb["lru_w_i"],
                        ws["lru_b_r"], ws["lru_b_i"], ws["lru_lambda"], B, S, "lru_fwd")
    x1, mix, h2 = _mix_fwd(a_in, b_in, proj, x, wb["w_ret_o"], wb["w_lru_o"], wb["w_out"],
                           ws["merge_gate_b"], ws["norm2_w"], tm, "mix_fwd")
    up = _matmul(h2, wb["ffn_w_up"], "nn", BF16, big2, 1024, 1024, "ffn_up_fwd")[0]
    f = _ffn_act_fwd(up, ws["ffn_conv_w"], ws["ffn_conv_b"], B, S, "ffn_act_fwd")
    x2 = _matmul(f, wb["ffn_w_down"], "nn", F32, big, 1024, D_FF, "ffn_down_fwd", add=x1)[0]
    dx2, dx2b, loss_acc, d_norm_f = _loss_head(x2, target, ws["norm_f_w"], tm, "loss_head")

    g, rs = {}, {}

    def stage1(names, grads):
        return _rs_sibling_job(grads, [MODE[n] for n in names], [SHARD[n] for n in names])

    def add(names, grads, recvs):
        return [_rs_add(gr, r, MODE[n], core, "rs_add_" + n) for n, gr, r in zip(names, grads, recvs)]

    g["norm_f_w"] = d_norm_f
    g_down = _matmul(f, dx2b, "tn", F32, 1024, 1024, big2, "ffn_down_bwd_w")[0]
    df, s1_down = _matmul(dx2b, wb["ffn_w_down"], "nt", BF16, big2, 1024, 1024, "ffn_down_bwd_x",
                          jobs=[stage1(["ffn_w_down"], [g_down])])
    p_down = add(["ffn_w_down"], [g_down], [s1_down])
    dup, g["ffn_conv_w"], g["ffn_conv_b"], s2_down = _ffn_act_bwd(
        df, up, ws["ffn_conv_w"], ws["ffn_conv_b"], B, S, "ffn_act_bwd", jobs=[_rs_chip_job(p_down)])
    rs["ffn_w_down"] = (p_down[0], s2_down)

    g_up = _matmul(h2, dup, "tn", F32, 1024, 1024, big2, "ffn_up_bwd_w")[0]
    dh2, s1_up = _matmul(dup, wb["ffn_w_up"], "nt", BF16, big, 1024, D_FF, "ffn_up_bwd_x",
                         jobs=[stage1(["ffn_w_up"], [g_up])])
    p_up = add(["ffn_w_up"], [g_up], [s1_up])
    dx1, dx1b, g["norm2_w"] = _rmsnorm_bwd_add(dx2, dh2, x1, ws["norm2_w"], tm, "norm2_bwd", True)
    da_in, db_in, dya, dyb, dproj, g["merge_gate_b"] = _mix_bwd(
        dx1b, a_in, b_in, proj, wb["w_ret_o"], wb["w_lru_o"], wb["w_out"], ws["merge_gate_b"], tm, "mix_bwd")

    mid_names = ["w_out", "w_ret_o", "w_lru_o"]
    g_mid = [_matmul(mix, dx1b, "tn", F32, 1024, 1024, big2, "w_out_bwd_w")[0],
             _matmul(a_in, dya, "tn", F32, 1024, 1024, big2, "w_ret_o_bwd_w")[0],
             _matmul(b_in, dyb, "tn", F32, 1024, 1024, big2, "w_lru_o_bwd_w")[0]]
    (dproj, g["lru_conv_w"], g["lru_conv_b"], g_wr, g_wi, g["lru_b_r"], g["lru_b_i"], g["lru_lambda"], s2_up,
     *s1_mid) = _lru_bwd(db_in, hl, proj, dproj, ws["lru_conv_w"], ws["lru_conv_b"], wb["lru_w_r"], wb["lru_w_i"],
                         ws["lru_b_r"], ws["lru_b_i"], ws["lru_lambda"], B, S, "lru_bwd",
                         jobs=[_rs_chip_job(p_up), stage1(mid_names, g_mid)])
    rs["ffn_w_up"] = (p_up[0], s2_up)
    p_mid = add(mid_names, g_mid, s1_mid)
    lru_names = ["lru_w_r", "lru_w_i"]
    dproj, g["ret_gn_w"], *rest = _retention_bwd(
        da_in, o, proj, dproj, cos2, sin2s, lgam, ws["ret_gn_w"], B, S, "retention_bwd",
        jobs=[_rs_chip_job(p_mid), stage1(lru_names, [g_wr, g_wi])])
    s2_mid, s1_lru = rest[:3], rest[3:]
    for n, p, r in zip(mid_names, p_mid, s2_mid):
        rs[n] = (p, r)
    p_lru = add(lru_names, [g_wr, g_wi], s1_lru)

    g_in, *s2_lru = _matmul(h1, dproj, "tn", F32, 1024, 1024, big2, "proj_bwd_w", jobs=[_rs_chip_job(p_lru)])
    for n, p, r in zip(lru_names, p_lru, s2_lru):
        rs[n] = (p, r)
    s1_in = _pcall(lambda: None, name="rs_sibling_w_in", grid=(1,), in_specs=[], out_specs=[], out_shape=[], args=[],
                   sem=("arbitrary",), jobs=[stage1(["w_in"], [g_in])])
    p_in = add(["w_in"], [g_in], s1_in)
    dh1, s2_in = _matmul(dproj, wb["w_in"], "nt", BF16, big, 1024, D_IN // 2, "proj_bwd_x",
                         jobs=[_rs_chip_job(p_in)])
    grad_x, g["norm1_w"] = _rmsnorm_bwd_add(dx1, dh1, x, ws["norm1_w"], tm, "norm1_bwd", False)
    rs["w_in"] = (p_in[0], s2_in)
    return loss_acc, grad_x.reshape(B, S, D), g, rs


def kernel(x, positions, norm1_w, w_in, merge_gate_b, ret_gn_w, w_ret_o, lru_conv_w, lru_conv_b, lru_w_r, lru_b_r, lru_w_i, lru_b_i, lru_lambda, w_lru_o, w_out, norm2_w, ffn_w_up, ffn_conv_w, ffn_conv_b, ffn_w_down, norm_f_w, loss_target, m_norm1_w, m_w_in, m_merge_gate_b, m_ret_gn_w, m_w_ret_o, m_lru_conv_w, m_lru_conv_b, m_lru_w_r, m_lru_b_r, m_lru_w_i, m_lru_b_i, m_lru_lambda, m_w_lru_o, m_w_out, m_norm2_w, m_ffn_w_up, m_ffn_conv_w, m_ffn_conv_b, m_ffn_w_down, m_norm_f_w, v_norm1_w, v_w_in, v_merge_gate_b, v_ret_gn_w, v_w_ret_o, v_lru_conv_w, v_lru_conv_b, v_lru_w_r, v_lru_b_r, v_lru_w_i, v_lru_b_i, v_lru_lambda, v_w_lru_o, v_w_out, v_norm2_w, v_ffn_w_up, v_ffn_conv_w, v_ffn_conv_b, v_ffn_w_down, v_norm_f_w):
    names = ["norm1_w", "w_in", "merge_gate_b", "ret_gn_w", "w_ret_o", "lru_conv_w", "lru_conv_b", "lru_w_r", "lru_b_r",
             "lru_w_i", "lru_b_i", "lru_lambda", "w_lru_o", "w_out", "norm2_w", "ffn_w_up", "ffn_conv_w", "ffn_conv_b",
             "ffn_w_down", "norm_f_w"]
    w_args = [norm1_w, w_in, merge_gate_b, ret_gn_w, w_ret_o, lru_conv_w, lru_conv_b, lru_w_r, lru_b_r, lru_w_i, lru_b_i,
              lru_lambda, w_lru_o, w_out, norm2_w, ffn_w_up, ffn_conv_w, ffn_conv_b, ffn_w_down, norm_f_w]
    m_args = [m_norm1_w, m_w_in, m_merge_gate_b, m_ret_gn_w, m_w_ret_o, m_lru_conv_w, m_lru_conv_b, m_lru_w_r, m_lru_b_r,
              m_lru_w_i, m_lru_b_i, m_lru_lambda, m_w_lru_o, m_w_out, m_norm2_w, m_ffn_w_up, m_ffn_conv_w, m_ffn_conv_b,
              m_ffn_w_down, m_norm_f_w]
    v_args = [v_norm1_w, v_w_in, v_merge_gate_b, v_ret_gn_w, v_w_ret_o, v_lru_conv_w, v_lru_conv_b, v_lru_w_r, v_lru_b_r,
              v_lru_w_i, v_lru_b_i, v_lru_lambda, v_w_lru_o, v_w_out, v_norm2_w, v_ffn_w_up, v_ffn_conv_w, v_ffn_conv_b,
              v_ffn_w_down, v_norm_f_w]
    orig_shape = {n: a.shape for n, a in zip(names, w_args)}
    local_shape = {n: s for n, s, _ in BIG + SMALL_SHARDED}
    local_shape.update({n: s for n, s in REPLICATED})
    W = {n: a.reshape(local_shape[n]) for n, a in zip(names, w_args)}
    M = {n: a.reshape(local_shape[n]) for n, a in zip(names, m_args)}
    V = {n: a.reshape(local_shape[n]) for n, a in zip(names, v_args)}

    xi, yi, ci = _mesh_pos()
    dev = 4 * xi + 2 * yi + ci
    chip = (2 * xi + yi).astype(jnp.int32).reshape(1)
    core = ci.astype(jnp.int32).reshape(1)

    first_names = [n for n, _, _ in FIRST]
    small_names = [n for n, _, _ in SMALL_SHARDED]
    gathered = _all_gather_multi([W[n].astype(BF16) for n in first_names] + [W[n] for n in small_names],
                                 [m for _, _, m in FIRST + SMALL_SHARDED], "gather_first_weights")
    wb = dict(zip(first_names, gathered[:len(FIRST)]))
    ws = dict(zip(small_names, gathered[len(FIRST):]))
    for n in ("lru_b_r", "lru_b_i"):
        ws[n] = jnp.transpose(ws[n], (1, 0, 2)).reshape(1, LRU_BLOCKS * LRU_BLOCK)
    for n, _ in REPLICATED:
        ws[n] = W[n]

    late_shards = [W[n].astype(BF16) for n, _, _ in LATE]
    loss_acc, grad_x, g, rs = _local_step(x, positions, loss_target, wb, ws, late_shards, core)

    G_out, D_out, M_out, V_out = {}, {}, {}, {}
    for n, _, _ in BIG:
        G_out[n], D_out[n], M_out[n], V_out[n] = _adamw_shard(rs[n][0], rs[n][1], W[n], M[n], V[n], chip, "adamw_" + n)

    rep_names = [n for n, _ in REPLICATED]
    red_names = rep_names + small_names
    red = _small_allreduce([g[n] for n in red_names] + [loss_acc[0:1, :]], "allreduce_small_grads")
    loss = red[-1][0, 0]
    gs = dict(zip(red_names, red[:-1]))
    for n, s, mode in SMALL_SHARDED:
        if mode == "cols":
            gs[n] = lax.dynamic_slice_in_dim(gs[n], dev * s[1], s[1], axis=1)
        else:
            full = gs[n].reshape(LRU_BLOCKS, LRU_BLOCK)
            gs[n] = lax.dynamic_slice_in_dim(full, dev * s[1], s[1], axis=1)
    d2, m2, v2 = _adamw_small([gs[n] for n in red_names], [W[n] for n in red_names], [M[n] for n in red_names],
                              [V[n] for n in red_names], "adamw_small")
    for i, n in enumerate(red_names):
        G_out[n], D_out[n], M_out[n], V_out[n] = gs[n], d2[i], m2[i], v2[i]

    outs = [loss, grad_x]
    for group in (G_out, D_out, M_out, V_out):
        outs += [group[n].reshape(orig_shape[n]) for n in names]
    return tuple(outs)
```

```python
import math

import jax
import jax.numpy as jnp
from jax import lax
from jax.experimental import pallas as pl
from jax.experimental.pallas import tpu as pltpu

F32 = jnp.float32
BF16 = jnp.bfloat16
MESH = pl.DeviceIdType.MESH

D_MODEL = 1024
CHUNK = 64
RET_HEADS = 4
RET_DK = 128
RET_DV = 256
LRU_BLOCKS = 4
LRU_BLOCK = 256
LRU_CONV = 4
LRU_C = 8.0
D_FF = 3072
FFN_CONV = 3
ROPE_BASE = 10000.0
RMS_EPS = 1e-6
GN_EPS = 1e-6
D_IN = 7168
ADAM_LR, ADAM_B1, ADAM_B2, ADAM_EPS, ADAM_WD, ADAM_STEP = 0.001, 0.9, 0.999, 1e-08, 0.01, 10

N_DEV = 8
V7X_VMEM_BYTES = 64 * 1024 * 1024
VMEM_LIMIT = V7X_VMEM_BYTES - 8 * 1024 * 1024
RET_BLOCK = 256
LANES = 128

COL_Q, COL_K = 0, 4
COL_V, COL_G, COL_XL, COL_YL = 4, 8, 12, 16
COL_GR, COL_GL = 5, 6

HBM_SPEC = pl.BlockSpec(memory_space=pl.ANY)


def _gelu(x):
    c = math.sqrt(2.0 / math.pi)
    t = jnp.tanh(c * (x + 0.044715 * x * x * x))
    return 0.5 * x * (1.0 + t)


def _gelu_and_grad(x):
    c = math.sqrt(2.0 / math.pi)
    x2 = x * x
    t = jnp.tanh(c * (x + 0.044715 * x2 * x))
    g = 0.5 * x * (1.0 + t)
    dg = 0.5 * (1.0 + t) + 0.5 * x * (1.0 - t * t) * c * (1.0 + 3.0 * 0.044715 * x2)
    return g, dg


def _sigmoid(x):
    return 1.0 / (1.0 + jnp.exp(-x))


SUBLANES = 8


def _shift_down(x, s, fill):
    r = pltpu.roll(x, s, 0)
    rows = lax.broadcasted_iota(jnp.int32, (SUBLANES,) + x.shape[1:], 0)
    top = jnp.where(rows >= s, r[:SUBLANES], fill)
    return jnp.concatenate([top, r[SUBLANES:]], axis=0)


def _shift_up(x, s, fill):
    n = x.shape[0]
    r = pltpu.roll(x, n - s, 0)
    rows = lax.broadcasted_iota(jnp.int32, (SUBLANES,) + x.shape[1:], 0)
    bottom = jnp.where(rows < SUBLANES - s, r[n - SUBLANES:], fill)
    return jnp.concatenate([r[:n - SUBLANES], bottom], axis=0)


SCAN_CHUNK = 64


def _scan_forward(a, b):
    n = a.shape[0]
    s = 1
    while s < n:
        if s % SUBLANES:
            b = a * _shift_down(b, s, 0.0) + b
            a = a * _shift_down(a, s, 1.0)
        else:
            b = jnp.concatenate([b[:s], a[s:] * b[:n - s] + b[s:]], axis=0)
            a = jnp.concatenate([a[:s], a[s:] * a[:n - s]], axis=0)
        s *= 2
    return a, b


def _scan_backward(a_next, u):
    n = u.shape[0]
    s = 1
    while s < n:
        if s % SUBLANES:
            u = u + a_next * _shift_up(u, s, 0.0)
            a_next = a_next * _shift_up(a_next, s, 1.0)
        else:
            u = jnp.concatenate([u[:n - s] + a_next[:n - s] * u[s:], u[n - s:]], axis=0)
            a_next = jnp.concatenate([a_next[:n - s] * a_next[s:], a_next[n - s:]], axis=0)
        s *= 2
    return a_next, u


def _scan_forward_ref(a_ref, b_ref, h_ref):
    S, W = a_ref.shape
    for strip in range(W // LANES):
        cols = pl.ds(strip * LANES, LANES)

        def body(k, carry, cols=cols):
            rows = pl.ds(pl.multiple_of(k * SCAN_CHUNK, SCAN_CHUNK), SCAN_CHUNK)
            a_cum, h_loc = _scan_forward(a_ref[rows, cols], b_ref[rows, cols])
            h = h_loc + a_cum * carry
            h_ref[rows, cols] = h
            return h[SCAN_CHUNK - 1:, :]

        lax.fori_loop(0, S // SCAN_CHUNK, body, jnp.zeros((1, LANES), F32))


def _scan_backward_ref(an_ref, u_ref, d_ref):
    S, W = an_ref.shape
    n_chunks = S // SCAN_CHUNK
    for strip in range(W // LANES):
        cols = pl.ds(strip * LANES, LANES)

        def body(i, carry, cols=cols):
            rows = pl.ds(pl.multiple_of((n_chunks - 1 - i) * SCAN_CHUNK, SCAN_CHUNK), SCAN_CHUNK)
            an_cum, d_loc = _scan_backward(an_ref[rows, cols], u_ref[rows, cols])
            d = d_loc + an_cum * carry
            d_ref[rows, cols] = d
            return d[:1, :]

        lax.fori_loop(0, n_chunks, body, jnp.zeros((1, LANES), F32))


def _dot(a, b, dims):
    return lax.dot_general(a, b, (dims, ((), ())), preferred_element_type=F32)


NN = ((1,), (0,))
NT = ((1,), (1,))
TN = ((0,), (0,))


def _mesh_pos():
    return lax.axis_index("x"), lax.axis_index("y"), lax.axis_index("c")


def _other_chips(x, y):
    return [(1 - x, y), (x, 1 - y), (1 - x, 1 - y)]


def _full_shape(shard_shape, mode):
    if mode == "rows":
        return (N_DEV * shard_shape[0],) + tuple(shard_shape[1:])
    if mode == "cols":
        return (shard_shape[0], N_DEV * shard_shape[1])
    if mode == "mid":
        return (shard_shape[0], N_DEV * shard_shape[1], shard_shape[2])
    return (N_DEV,) + tuple(shard_shape)


def _extent(shard_shape, mode):
    return {"rows": shard_shape[0], "cols": shard_shape[1], "mid": shard_shape[1], "stack": 1}[mode]


def _window(ref, mode, extent, d):
    if mode == "stack":
        return ref.at[d]
    start = pl.multiple_of(d * extent, extent)
    if mode == "rows":
        return ref.at[pl.ds(start, extent)]
    if mode == "cols":
        return ref.at[:, pl.ds(start, extent)]
    return ref.at[:, pl.ds(start, extent), :]


class _Job:
    def __init__(self, inputs, out_shapes, sems, start, finish, aliases=None):
        self.inputs, self.out_shapes, self.sems = list(inputs), list(out_shapes), sems
        self.start, self.finish, self.aliases = start, finish, dict(aliases or {})


def _remote(src, dst, send_sem, recv_sem, to):
    return pltpu.make_async_remote_copy(src_ref=src, dst_ref=dst, send_sem=send_sem, recv_sem=recv_sem,
                                        device_id=to, device_id_type=MESH)


def _ag_first_job(shards, modes):
    n = len(shards)
    extents = [_extent(s.shape, m) for s, m in zip(shards, modes)]

    def copies(x_refs, out_refs, send, recv, local, arriving):
        x, y, c = _mesh_pos()
        peers = [(x, y, 1 - c)] + [(*chip, c) for chip in _other_chips(x, y)]
        win = lambda i, p: _window(out_refs[i], modes[i], extents[i], 4 * p[0] + 2 * p[1] + p[2])
        if arriving:
            return [_remote(x_refs[i], win(i, p), send.at[i, k], recv.at[i, k], p)
                    for i in range(n) for k, p in enumerate(peers)]
        mine = [pltpu.make_async_copy(x_refs[i], win(i, (x, y, c)), local.at[i]) for i in range(n)]
        sends = [_remote(x_refs[i], win(i, (x, y, c)), send.at[i, k], recv.at[i, k], p)
                 for i in range(n) for k, p in enumerate(peers)]
        return mine, sends

    def start(*refs):
        mine, sends = copies(*refs, False)
        for cp in mine + sends:
            cp.start()

    def finish(*refs):
        for cp in copies(*refs, True):
            cp.wait_recv()
        mine, sends = copies(*refs, False)
        for cp in sends:
            cp.wait_send()
        for cp in mine:
            cp.wait()

    out_shapes = [jax.ShapeDtypeStruct(_full_shape(s.shape, m), s.dtype) for s, m in zip(shards, modes)]
    return _Job(shards, out_shapes, ((n, 4), (n, 4), (n,)), start, finish)


def _ag_second_job(fulls, modes, shard_shapes):
    n = len(fulls)
    extents = [_extent(s, m) for s, m in zip(shard_shapes, modes)]

    def copies(_, out_refs, send, recv, local, core_of_block):
        x, y, c = _mesh_pos()
        pc = c if core_of_block == "mine" else 1 - c
        win = lambda i, chip: _window(out_refs[i], modes[i], extents[i], 4 * chip[0] + 2 * chip[1] + pc)
        return [_remote(win(i, chip), win(i, chip), send.at[i, j], recv.at[i, j], (x, y, 1 - c))
                for i in range(n) for j, chip in enumerate(_other_chips(x, y))]

    def start(*refs):
        for cp in copies(*refs, "mine"):
            cp.start()

    def finish(*refs):
        for cp in copies(*refs, "sibling"):
            cp.wait_recv()
        for cp in copies(*refs, "mine"):
            cp.wait_send()

    out_shapes = [jax.ShapeDtypeStruct(f.shape, f.dtype) for f in fulls]
    return _Job(fulls, out_shapes, ((n, 3), (n, 3), (1,)), start, finish, aliases={i: i for i in range(n)})


def _rs_sibling_job(grads, modes, shard_shapes):
    n = len(grads)
    extents = [_extent(s, m) for s, m in zip(shard_shapes, modes)]

    def copies(g_refs, out_refs, send, recv, local):
        x, y, c = _mesh_pos()
        return [_remote(_window(g_refs[i], modes[i], extents[i], 2 * k + (1 - c)), out_refs[i].at[k],
                        send.at[i, k], recv.at[i, k], (x, y, 1 - c))
                for i in range(n) for k in range(4)]

    def start(*refs):
        for cp in copies(*refs):
            cp.start()

    def finish(*refs):
        cps = copies(*refs)
        for cp in cps:
            cp.wait_recv()
        for cp in cps:
            cp.wait_send()

    out_shapes = [jax.ShapeDtypeStruct((4,) + tuple(s), g.dtype) for s, g in zip(shard_shapes, grads)]
    return _Job(grads, out_shapes, ((n, 4), (n, 4), (1,)), start, finish)


def _rs_chip_job(partials):
    n = len(partials)

    def copies(p_refs, out_refs, send, recv, local):
        x, y, c = _mesh_pos()
        return [_remote(p_refs[i].at[2 * px + py], out_refs[i].at[j], send.at[i, j], recv.at[i, j], (px, py, c))
                for i in range(n) for j, (px, py) in enumerate(_other_chips(x, y))]

    def start(*refs):
        for cp in copies(*refs):
            cp.start()

    def finish(*refs):
        cps = copies(*refs)
        for cp in cps:
            cp.wait_recv()
        for cp in cps:
            cp.wait_send()

    out_shapes = [jax.ShapeDtypeStruct((3,) + tuple(p.shape[1:]), p.dtype) for p in partials]
    return _Job(partials, out_shapes, ((n, 3), (n, 3), (1,)), start, finish)


def _all_true(conds):
    out = conds[0]
    for c in conds[1:]:
        out = jnp.logical_and(out, c)
    return out


def _pcall(body, *, name, grid, in_specs, out_specs, out_shape, args, sem, scratch=(), jobs=(), alias_in_out=None):
    n_in, n_out, n_scr = len(args), len(out_shape), len(scratch)
    job_in = [a for j in jobs for a in j.inputs]
    job_out = [s for j in jobs for s in j.out_shapes]
    job_sems = [pltpu.SemaphoreType.DMA(shape) for j in jobs for shape in j.sems]
    aliases, in_off, out_off = dict(alias_in_out or {}), n_in, n_out
    for j in jobs:
        for a, b in j.aliases.items():
            aliases[in_off + a] = out_off + b
        in_off += len(j.inputs)
        out_off += len(j.out_shapes)

    def wrapped(*refs):
        ins = refs[:n_in]
        jins = refs[n_in:n_in + len(job_in)]
        o0 = n_in + len(job_in)
        outs = refs[o0:o0 + n_out]
        jouts = refs[o0 + n_out:o0 + n_out + len(job_out)]
        s0 = o0 + n_out + len(job_out)
        scr = refs[s0:s0 + n_scr]
        jsems = refs[s0 + n_scr:]
        if jobs:
            ids = [pl.program_id(a) for a in range(len(grid))]
            first = _all_true([i == 0 for i in ids])
            last = _all_true([i == g - 1 for i, g in zip(ids, grid)])

            def per_job(which):
                i0 = o0_ = 0
                for k, j in enumerate(jobs):
                    fn = j.start if which == "start" else j.finish
                    fn(jins[i0:i0 + len(j.inputs)], jouts[o0_:o0_ + len(j.out_shapes)], *jsems[3 * k:3 * k + 3])
                    i0 += len(j.inputs)
                    o0_ += len(j.out_shapes)

            @pl.when(first)
            def _():
                per_job("start")

        body(*ins, *outs, *scr)
        if jobs:
            @pl.when(last)
            def _():
                per_job("finish")

    semantics = tuple("arbitrary" for _ in grid) if jobs else sem
    return pl.pallas_call(
        wrapped, name=name, grid=grid,
        in_specs=list(in_specs) + [HBM_SPEC] * len(job_in),
        out_specs=list(out_specs) + [HBM_SPEC] * len(job_out),
        out_shape=list(out_shape) + job_out,
        scratch_shapes=list(scratch) + job_sems,
        input_output_aliases=aliases,
        compiler_params=pltpu.CompilerParams(dimension_semantics=semantics, vmem_limit_bytes=VMEM_LIMIT),
    )(*args, *job_in)


def _row_tile(rows, cap):
    if rows <= cap:
        return rows
    best = None
    for t in range(16, cap + 1, 16):
        if rows % t == 0:
            best = t
    assert best is not None
    return best


def _matmul(a, b, mode, out_dtype, tm, tn, tk, name, add=None, jobs=()):
    if mode == "tn":
        K, M = a.shape
    else:
        M, K = a.shape
    N = b.shape[0] if mode == "nt" else b.shape[1]
    tm, tn, tk = min(tm, M), min(tn, N), min(tk, K)
    assert M % tm == 0 and N % tn == 0 and K % tk == 0
    nk = K // tk
    dims = {"nn": NN, "nt": NT, "tn": TN}[mode]

    def body(*refs):
        if add is None:
            a_ref, b_ref, o_ref, acc = refs
            add_ref = None
        else:
            a_ref, b_ref, add_ref, o_ref, acc = refs
        k = pl.program_id(2)
        p = _dot(a_ref[...], b_ref[...], dims)

        def finish(r):
            if add_ref is not None:
                r = r + add_ref[...].astype(F32)
            o_ref[...] = r.astype(out_dtype)

        if nk == 1:
            finish(p)
        else:
            @pl.when(k == 0)
            def _():
                acc[...] = p

            @pl.when(k > 0)
            def _():
                acc[...] += p

            @pl.when(k == nk - 1)
            def _():
                finish(acc[...])

    if mode == "tn":
        a_spec = pl.BlockSpec((tk, tm), lambda i, j, k: (k, i))
    else:
        a_spec = pl.BlockSpec((tm, tk), lambda i, j, k: (i, k))
    if mode == "nt":
        b_spec = pl.BlockSpec((tn, tk), lambda i, j, k: (j, k))
    else:
        b_spec = pl.BlockSpec((tk, tn), lambda i, j, k: (k, j))
    in_specs = [a_spec, b_spec]
    args = [a, b]
    if add is not None:
        in_specs.append(pl.BlockSpec((tm, tn), lambda i, j, k: (i, j)))
        args.append(add)
    return _pcall(
        body, name=name, grid=(M // tm, N // tn, nk), in_specs=in_specs,
        out_specs=[pl.BlockSpec((tm, tn), lambda i, j, k: (i, j))],
        out_shape=[jax.ShapeDtypeStruct((M, N), out_dtype)], args=args,
        scratch=[pltpu.VMEM((tm, tn) if nk > 1 else (8, LANES), F32)],
        sem=("parallel", "parallel", "arbitrary"), jobs=jobs)


def _rmsnorm_fwd(x, w, tm, name):
    T, D = x.shape

    def body(x_ref, w_ref, h_ref):
        xv = x_ref[...]
        r = lax.rsqrt(jnp.mean(xv * xv, axis=-1, keepdims=True) + RMS_EPS)
        h_ref[...] = (xv * r * w_ref[...]).astype(BF16)

    return _pcall(
        body, name=name, grid=(T // tm,),
        in_specs=[pl.BlockSpec((tm, D), lambda i: (i, 0)), pl.BlockSpec((1, D), lambda i: (0, 0))],
        out_specs=[pl.BlockSpec((tm, D), lambda i: (i, 0))],
        out_shape=[jax.ShapeDtypeStruct((T, D), BF16)], args=[x, w], sem=("parallel",))[0]


def _rmsnorm_bwd_add(dres, dh, x, w, tm, name, want_bf16, jobs=()):
    T, D = x.shape

    def body(dres_ref, dh_ref, x_ref, w_ref, *outs):
        if want_bf16:
            dx_ref, dxb_ref, dw_ref = outs
        else:
            dx_ref, dw_ref = outs
        i = pl.program_id(0)
        xv = x_ref[...]
        r = lax.rsqrt(jnp.mean(xv * xv, axis=-1, keepdims=True) + RMS_EPS)
        xh = xv * r
        dh_v = dh_ref[...].astype(F32)
        dxh = dh_v * w_ref[...]
        dx = dres_ref[...] + r * (dxh - xh * jnp.mean(dxh * xh, axis=-1, keepdims=True))
        dx_ref[...] = dx
        if want_bf16:
            dxb_ref[...] = dx.astype(BF16)
        part = jnp.sum(dh_v * xh, axis=0, keepdims=True)

        @pl.when(i == 0)
        def _():
            dw_ref[...] = part

        @pl.when(i > 0)
        def _():
            dw_ref[...] += part

    tile = pl.BlockSpec((tm, D), lambda i: (i, 0))
    row = pl.BlockSpec((1, D), lambda i: (0, 0))
    out_specs = [tile] + ([tile] if want_bf16 else []) + [row]
    out_shape = ([jax.ShapeDtypeStruct((T, D), F32)] + ([jax.ShapeDtypeStruct((T, D), BF16)] if want_bf16 else [])
                 + [jax.ShapeDtypeStruct((1, D), F32)])
    return _pcall(body, name=name, grid=(T // tm,), in_specs=[tile, tile, tile, row], out_specs=out_specs,
                  out_shape=out_shape, args=[dres, dh, x, w], sem=("arbitrary",), jobs=jobs)


def _loss_head(x2, target, wf, tm, name):
    T, D = x2.shape

    def body(x_ref, t_ref, w_ref, dx_ref, dxb_ref, loss_ref, dw_ref):
        i = pl.program_id(0)
        xv = x_ref[...]
        r = lax.rsqrt(jnp.mean(xv * xv, axis=-1, keepdims=True) + RMS_EPS)
        xh = xv * r
        wv = w_ref[...]
        e = xh * wv - t_ref[...]
        lpart = 0.5 * jnp.sum(jnp.sum(e * e, axis=-1, keepdims=True), axis=0, keepdims=True) * (1.0 / D)
        dy = e * (1.0 / D)
        dxh = dy * wv
        dx = r * (dxh - xh * jnp.mean(dxh * xh, axis=-1, keepdims=True))
        dx_ref[...] = dx
        dxb_ref[...] = dx.astype(BF16)
        wpart = jnp.sum(dy * xh, axis=0, keepdims=True)
        lfull = jnp.broadcast_to(lpart, (8, LANES))

        @pl.when(i == 0)
        def _():
            loss_ref[...] = lfull
            dw_ref[...] = wpart

        @pl.when(i > 0)
        def _():
            loss_ref[...] += lfull
            dw_ref[...] += wpart

    tile = pl.BlockSpec((tm, D), lambda i: (i, 0))
    row = pl.BlockSpec((1, D), lambda i: (0, 0))
    return _pcall(
        body, name=name, grid=(T // tm,), in_specs=[tile, tile, row],
        out_specs=[tile, tile, pl.BlockSpec((8, LANES), lambda i: (0, 0)), row],
        out_shape=[jax.ShapeDtypeStruct((T, D), F32), jax.ShapeDtypeStruct((T, D), BF16),
                   jax.ShapeDtypeStruct((8, LANES), F32), jax.ShapeDtypeStruct((1, D), F32)],
        args=[x2, target, wf], sem=("arbitrary",))


def _rope_tables(pos_col, inv2, tm, name):
    T = pos_col.shape[0]

    def body(p_ref, f_ref, c_ref, s_ref):
        ang = p_ref[...] * f_ref[...]
        lane = lax.broadcasted_iota(jnp.int32, ang.shape, 1)
        c_ref[...] = jnp.cos(ang)
        s_ref[...] = jnp.where(lane < RET_DK // 2, -1.0, 1.0) * jnp.sin(ang)

    tile = pl.BlockSpec((tm, RET_DK), lambda i: (i, 0))
    return _pcall(
        body, name=name, grid=(T // tm,),
        in_specs=[pl.BlockSpec((tm, 1), lambda i: (i, 0)), pl.BlockSpec((1, RET_DK), lambda i: (0, 0))],
        out_specs=[tile, tile], out_shape=[jax.ShapeDtypeStruct((T, RET_DK), F32)] * 2,
        args=[pos_col, inv2], sem=("parallel",))


def _mix_fwd(a_in, b_in, proj, x, w_ro, w_lo, w_out, mb, w2, tm, name):
    T, D = x.shape

    def body(a_ref, b_ref, gr_ref, gl_ref, x_ref, wro_ref, wlo_ref, wout_ref, mb_ref, w2_ref,
             x1_ref, mix_ref, h2_ref):
        ya = _dot(a_ref[...], wro_ref[...], NN)
        yb = _dot(b_ref[...], wlo_ref[...], NN)
        sa = _sigmoid(gr_ref[...].astype(F32) + mb_ref[0:1, :])
        sb = _sigmoid(gl_ref[...].astype(F32) + mb_ref[1:2, :])
        mix = (sa * ya + sb * yb).astype(BF16)
        mix_ref[...] = mix
        x1 = x_ref[...] + _dot(mix, wout_ref[...], NN)
        x1_ref[...] = x1
        r = lax.rsqrt(jnp.mean(x1 * x1, axis=-1, keepdims=True) + RMS_EPS)
        h2_ref[...] = (x1 * r * w2_ref[...]).astype(BF16)

    tile = pl.BlockSpec((tm, D), lambda i: (i, 0))
    wspec = pl.BlockSpec((D, D), lambda i: (0, 0))
    return _pcall(
        body, name=name, grid=(T // tm,),
        in_specs=[tile, tile,
                  pl.BlockSpec((tm, D), lambda i: (i, COL_GR)), pl.BlockSpec((tm, D), lambda i: (i, COL_GL)),
                  tile, wspec, wspec, wspec,
                  pl.BlockSpec((2, D), lambda i: (0, 0)), pl.BlockSpec((1, D), lambda i: (0, 0))],
        out_specs=[tile, tile, tile],
        out_shape=[jax.ShapeDtypeStruct((T, D), F32), jax.ShapeDtypeStruct((T, D), BF16),
                   jax.ShapeDtypeStruct((T, D), BF16)],
        args=[a_in, b_in, proj, proj, x, w_ro, w_lo, w_out, mb, w2], sem=("parallel",))


def _write_pieces(dst_ref, sems, stashes, row0, col0s, ids, grid, compute):
    def aligned(v, m):
        return v if isinstance(v, int) else pl.multiple_of(v, m)

    def copies(slot):
        return [pltpu.make_async_copy(
                    st.at[slot],
                    dst_ref.at[pl.ds(aligned(row0, 16), st.shape[1]), pl.ds(aligned(c0, LANES), st.shape[2])],
                    sems.at[slot, k])
                for k, (st, c0) in enumerate(zip(stashes, col0s))]

    step = ids[0]
    for i, g in zip(ids[1:], grid[1:]):
        step = step * g + i
    slot = step % 2
    last = _all_true([i == g - 1 for i, g in zip(ids, grid)])
    compute(slot)

    @pl.when(step > 0)
    def _():
        for cp in copies(1 - slot):
            cp.wait()

    for cp in copies(slot):
        cp.start()

    @pl.when(last)
    def _():
        for cp in copies(slot):
            cp.wait()


def _mix_bwd(dx1b, a_in, b_in, proj, w_ro, w_lo, w_out, mb, tm, name, jobs=()):
    T, D = a_in.shape
    grid = (T // tm,)

    def body(dx_ref, a_ref, b_ref, gr_ref, gl_ref, wro_ref, wlo_ref, wout_ref, mb_ref,
             da_ref, db_ref, dya_ref, dyb_ref, dp_ref, dmb_ref, dgr_s, dgl_s, wsem):
        i = pl.program_id(0)

        def compute(slot):
            dmix = _dot(dx_ref[...], wout_ref[...], NT)
            ya = _dot(a_ref[...], wro_ref[...], NN)
            yb = _dot(b_ref[...], wlo_ref[...], NN)
            sa = _sigmoid(gr_ref[...].astype(F32) + mb_ref[0:1, :])
            sb = _sigmoid(gl_ref[...].astype(F32) + mb_ref[1:2, :])
            dya = (dmix * sa).astype(BF16)
            dyb = (dmix * sb).astype(BF16)
            dgr = dmix * ya * sa * (1.0 - sa)
            dgl = dmix * yb * sb * (1.0 - sb)
            dya_ref[...] = dya
            dyb_ref[...] = dyb
            dgr_s[slot] = dgr.astype(BF16)
            dgl_s[slot] = dgl.astype(BF16)
            da_ref[...] = _dot(dya, wro_ref[...], NT).astype(BF16)
            db_ref[...] = _dot(dyb, wlo_ref[...], NT).astype(BF16)

            @pl.when(i == 0)
            def _():
                dmb_ref[...] = jnp.zeros_like(dmb_ref)

            dmb_ref[0:1, :] += jnp.sum(dgr, axis=0, keepdims=True)
            dmb_ref[1:2, :] += jnp.sum(dgl, axis=0, keepdims=True)

        _write_pieces(dp_ref, wsem, [dgr_s, dgl_s], i * tm, [COL_GR * D, COL_GL * D], [i], grid, compute)

    tile = pl.BlockSpec((tm, D), lambda i: (i, 0))
    wspec = pl.BlockSpec((D, D), lambda i: (0, 0))
    two = pl.BlockSpec((2, D), lambda i: (0, 0))
    return _pcall(
        body, name=name, grid=grid,
        in_specs=[tile, tile, tile,
                  pl.BlockSpec((tm, D), lambda i: (i, COL_GR)), pl.BlockSpec((tm, D), lambda i: (i, COL_GL)),
                  wspec, wspec, wspec, two],
        out_specs=[tile] * 4 + [HBM_SPEC, two],
        out_shape=[jax.ShapeDtypeStruct((T, D), BF16)] * 4
                  + [jax.ShapeDtypeStruct((T, D_IN), BF16), jax.ShapeDtypeStruct((2, D), F32)],
        args=[dx1b, a_in, b_in, proj, proj, w_ro, w_lo, w_out, mb],
        scratch=[pltpu.VMEM((2, tm, D), BF16), pltpu.VMEM((2, tm, D), BF16), pltpu.SemaphoreType.DMA((2, 2))],
        sem=("arbitrary",), jobs=jobs)


def _ret_decay_consts(lg):
    L = RET_BLOCK
    n = lax.broadcasted_iota(jnp.int32, (L, L), 0)
    m = lax.broadcasted_iota(jnp.int32, (L, L), 1)
    cn, cm = n // CHUNK, m // CHUNK
    expo = jnp.where(cn == cm, jnp.abs(n - m), n - m).astype(F32)
    wm = jnp.where(cm <= cn, jnp.exp(lg * expo), 0.0)
    idx = lax.broadcasted_iota(jnp.int32, (L, 1), 0).astype(F32)
    qd = jnp.exp(lg * (idx + 1.0))
    kd = jnp.exp(lg * (L - 1.0 - idx))
    bd = jnp.exp(lg * float(L))
    return wm, qd, kd, bd


def _rotate(v, cos2, sin2s):
    return v * cos2 + pltpu.roll(v, RET_DK // 2, 1) * sin2s


def _rotate_t(d, cos2, sin2s):
    return d * cos2 - pltpu.roll(d, RET_DK // 2, 1) * sin2s


def _retention_fwd(proj, cos2, sin2s, lgam, gn_w, B, S, name, jobs=()):
    T = B * S
    nb = S // RET_BLOCK
    scale = RET_DK ** -0.5

    def body(q_ref, k_ref, v_ref, g_ref, c_ref, s_ref, lg_ref, gw_ref, o_ref, a_ref, qr, kr, st):
        wm, qd, kd, bd = _ret_decay_consts(lg_ref[0:1, 0:1])
        cos2, sin2s = c_ref[...], s_ref[...]
        qr[...] = _rotate(q_ref[...].astype(F32), cos2, sin2s)
        kr[...] = _rotate(k_ref[...].astype(F32), cos2, sin2s) * scale
        st[...] = jnp.zeros_like(st)
        gw = gw_ref[...]
        for j in range(nb):
            rows = pl.ds(j * RET_BLOCK, RET_BLOCK)
            qb = qr[rows, :]
            kb = kr[rows, :]
            vb = v_ref[rows, :].astype(BF16)
            sc = _dot(qb.astype(BF16), kb.astype(BF16), NT) * wm
            o = _dot(sc.astype(BF16), vb, NN) + _dot((qb * qd).astype(BF16), st[...].astype(BF16), NN)
            st[...] = st[...] * bd + _dot((kb * kd).astype(BF16), vb, TN)
            o_ref[rows, :] = o
            mu = jnp.mean(o, axis=-1, keepdims=True)
            oc = o - mu
            var = jnp.mean(oc * oc, axis=-1, keepdims=True)
            y = oc * lax.rsqrt(var + GN_EPS) * gw
            g = g_ref[rows, :].astype(F32)
            a_ref[rows, :] = (y * (g * _sigmoid(g))).astype(BF16)

    blk = lambda w, off: pl.BlockSpec((S, w), lambda b, h: (b, off + h))
    return _pcall(
        body, name=name, grid=(B, RET_HEADS),
        in_specs=[blk(RET_DK, COL_Q), blk(RET_DK, COL_K), blk(RET_DV, COL_V), blk(RET_DV, COL_G),
                  pl.BlockSpec((S, RET_DK), lambda b, h: (b, 0)), pl.BlockSpec((S, RET_DK), lambda b, h: (b, 0)),
                  pl.BlockSpec((None, 8, LANES), lambda b, h: (h, 0, 0)),
                  pl.BlockSpec((1, RET_DV), lambda b, h: (0, h))],
        out_specs=[blk(RET_DV, 0), blk(RET_DV, 0)],
        out_shape=[jax.ShapeDtypeStruct((T, RET_HEADS * RET_DV), F32),
                   jax.ShapeDtypeStruct((T, RET_HEADS * RET_DV), BF16)],
        args=[proj, proj, proj, proj, cos2, sin2s, lgam, gn_w],
        scratch=[pltpu.VMEM((S, RET_DK), F32), pltpu.VMEM((S, RET_DK), F32), pltpu.VMEM((RET_DK, RET_DV), F32)],
        sem=("parallel", "parallel"), jobs=jobs)


def _retention_bwd(da_in, o, proj, dproj, cos2, sin2s, lgam, gn_w, B, S, name, jobs=()):
    T = B * S
    nb = S // RET_BLOCK
    scale = RET_DK ** -0.5
    grid = (RET_HEADS, B)

    def body(da_ref, o_ref, q_ref, k_ref, v_ref, g_ref, c_ref, s_ref, lg_ref, gw_ref, _, dp_ref, dgw_ref,
             qr, kr, do_s, sts, rst, dq_s, dk_s, dv_s, dg_s, wsem):
        h, b = pl.program_id(0), pl.program_id(1)

        def compute(slot):
            wm, qd, kd, bd = _ret_decay_consts(lg_ref[0:1, 0:1])
            cos2, sin2s = c_ref[...], s_ref[...]
            qr[...] = _rotate(q_ref[...].astype(F32), cos2, sin2s)
            kr[...] = _rotate(k_ref[...].astype(F32), cos2, sin2s) * scale
            gw = gw_ref[...]
            st = jnp.zeros((RET_DK, RET_DV), F32)
            dgw = jnp.zeros((1, RET_DV), F32)
            for j in range(nb):
                rows = pl.ds(j * RET_BLOCK, RET_BLOCK)
                ov = o_ref[rows, :]
                mu = jnp.mean(ov, axis=-1, keepdims=True)
                oc = ov - mu
                rstd = lax.rsqrt(jnp.mean(oc * oc, axis=-1, keepdims=True) + GN_EPS)
                y = oc * rstd
                g = g_ref[rows, :].astype(F32)
                sg = _sigmoid(g)
                da = da_ref[rows, :].astype(F32)
                dg_s[slot, rows, :] = (da * (y * gw) * (sg * (1.0 + g * (1.0 - sg)))).astype(BF16)
                dyw = da * (g * sg)
                dgw = dgw + jnp.sum(dyw * y, axis=0, keepdims=True)
                dy = dyw * gw
                do_s[rows, :] = rstd * (dy - jnp.mean(dy, axis=-1, keepdims=True)
                                        - y * jnp.mean(dy * y, axis=-1, keepdims=True))
                sts[j] = st
                st = st * bd + _dot((kr[rows, :] * kd).astype(BF16), v_ref[rows, :].astype(BF16), TN)

            @pl.when(b == 0)
            def _():
                dgw_ref[...] = dgw

            @pl.when(b > 0)
            def _():
                dgw_ref[...] += dgw

            rst[...] = jnp.zeros_like(rst)
            for j in reversed(range(nb)):
                rows = pl.ds(j * RET_BLOCK, RET_BLOCK)
                qb = qr[rows, :]
                kb = kr[rows, :]
                qbb, kbb = qb.astype(BF16), kb.astype(BF16)
                vb = v_ref[rows, :].astype(BF16)
                dob = do_s[rows, :]
                dobb = dob.astype(BF16)
                a_m = (_dot(qbb, kbb, NT) * wm).astype(BF16)
                b_m = (_dot(dobb, vb, NT) * wm).astype(BF16)
                rb = rst[...].astype(BF16)
                dq = _dot(b_m, kbb, NN) + _dot((dob * qd).astype(BF16), sts[j].astype(BF16), NT)
                dk = _dot(b_m, qbb, TN) + kd * _dot(vb, rb, NT)
                dv = _dot(a_m, dobb, TN) + kd * _dot(kbb, rb, NN)
                rst[...] = rst[...] * bd + _dot((qb * qd).astype(BF16), dobb, TN)
                cb, sb = c_ref[rows, :], s_ref[rows, :]
                dq_s[slot, rows, :] = _rotate_t(dq, cb, sb).astype(BF16)
                dk_s[slot, rows, :] = _rotate_t(dk * scale, cb, sb).astype(BF16)
                dv_s[slot, rows, :] = dv.astype(BF16)

        cols = [(COL_Q + h) * RET_DK, (COL_K + h) * RET_DK, (COL_V + h) * RET_DV, (COL_G + h) * RET_DV]
        _write_pieces(dp_ref, wsem, [dq_s, dk_s, dv_s, dg_s], b * S, cols, [h, b], grid, compute)

    blk = lambda w, off: pl.BlockSpec((S, w), lambda h, b: (b, off + h))
    return _pcall(
        body, name=name, grid=grid,
        in_specs=[blk(RET_DV, 0), blk(RET_DV, 0),
                  blk(RET_DK, COL_Q), blk(RET_DK, COL_K), blk(RET_DV, COL_V), blk(RET_DV, COL_G),
                  pl.BlockSpec((S, RET_DK), lambda h, b: (b, 0)), pl.BlockSpec((S, RET_DK), lambda h, b: (b, 0)),
                  pl.BlockSpec((None, 8, LANES), lambda h, b: (h, 0, 0)),
                  pl.BlockSpec((1, RET_DV), lambda h, b: (0, h)), HBM_SPEC],
        out_specs=[HBM_SPEC, pl.BlockSpec((1, RET_DV), lambda h, b: (0, h))],
        out_shape=[jax.ShapeDtypeStruct(dproj.shape, dproj.dtype),
                   jax.ShapeDtypeStruct((1, RET_HEADS * RET_DV), F32)],
        args=[da_in, o, proj, proj, proj, proj, cos2, sin2s, lgam, gn_w, dproj],
        scratch=[pltpu.VMEM((S, RET_DK), F32), pltpu.VMEM((S, RET_DK), F32),
                 pltpu.VMEM((S, RET_DV), F32), pltpu.VMEM((nb, RET_DK, RET_DV), F32),
                 pltpu.VMEM((RET_DK, RET_DV), F32),
                 pltpu.VMEM((2, S, RET_DK), BF16), pltpu.VMEM((2, S, RET_DK), BF16),
                 pltpu.VMEM((2, S, RET_DV), BF16), pltpu.VMEM((2, S, RET_DV), BF16), pltpu.SemaphoreType.DMA((2, 4))],
        sem=("arbitrary", "arbitrary"), jobs=jobs, alias_in_out={10: 0})


def _lru_gates(x, cw, cb, wr, wi, br, bi, lam):
    xc = cb + cw[LRU_CONV - 1:LRU_CONV, :] * x
    for j in range(LRU_CONV - 1):
        xc = xc + cw[j:j + 1, :] * _shift_down(x, LRU_CONV - 1 - j, 0.0)
    xcb = xc.astype(BF16)
    r = _sigmoid(_dot(xcb, wr, NN) + br)
    ig = _sigmoid(_dot(xcb, wi, NN) + bi)
    z = -lam
    sp = jnp.maximum(z, 0.0) + jnp.log1p(jnp.exp(-jnp.abs(z)))
    log_a = (-LRU_C) * r * sp
    a = jnp.exp(log_a)
    z2 = 2.0 * log_a
    taylor = -z2 * (1.0 + z2 * (0.5 + z2 * (1.0 / 6.0 + z2 * (1.0 / 24.0 + z2 * (1.0 / 120.0)))))
    om = jnp.where(z2 > -0.05, taylor, 1.0 - jnp.exp(z2))
    sq = jnp.sqrt(om)
    return xc, xcb, r, ig, sp, a, sq


def _lru_fwd(proj, cw, cb, wr, wi, br, bi, lam, B, S, name):
    T = B * S
    W = LRU_BLOCKS * LRU_BLOCK

    def body(x_ref, y_ref, cw_ref, cb_ref, wr_ref, wi_ref, br_ref, bi_ref, lam_ref, h_ref, bin_ref, a_s, b_s):
        xc, _, _, ig, _, a, sq = _lru_gates(x_ref[...].astype(F32), cw_ref[...], cb_ref[...], wr_ref[...], wi_ref[...],
                                           br_ref[...], bi_ref[...], lam_ref[...])
        a_s[...] = a
        b_s[...] = sq * ig * xc
        _scan_forward_ref(a_s, b_s, h_ref)
        bin_ref[...] = (h_ref[...] * _gelu(y_ref[...].astype(F32))).astype(BF16)

    blk = lambda off: pl.BlockSpec((S, LRU_BLOCK), lambda b, n: (b, off + n))
    vec = lambda rows: pl.BlockSpec((rows, LRU_BLOCK), lambda b, n: (0, n))
    wspec = pl.BlockSpec((None, LRU_BLOCK, LRU_BLOCK), lambda b, n: (n, 0, 0))
    return _pcall(
        body, name=name, grid=(B, LRU_BLOCKS),
        in_specs=[blk(COL_XL), blk(COL_YL), vec(LRU_CONV), vec(1), wspec, wspec, vec(1), vec(1), vec(1)],
        out_specs=[blk(0), blk(0)],
        out_shape=[jax.ShapeDtypeStruct((T, W), F32), jax.ShapeDtypeStruct((T, W), BF16)],
        args=[proj, proj, cw, cb, wr, wi, br, bi, lam],
        scratch=[pltpu.VMEM((S, LRU_BLOCK), F32), pltpu.VMEM((S, LRU_BLOCK), F32)], sem=("parallel", "parallel"))


def _lru_bwd(db_in, h, proj, dproj, cw, cb, wr, wi, br, bi, lam, B, S, name, jobs=()):
    T = B * S
    W = LRU_BLOCKS * LRU_BLOCK

    grid = (LRU_BLOCKS, B)

    def body(dbin_ref, h_ref, x_ref, y_ref, cw_ref, cb_ref, wr_ref, wi_ref, br_ref, bi_ref, lam_ref, _,
             dp_ref, dcw_ref, dcb_ref, dwr_ref, dwi_ref, dbr_ref, dbi_ref, dlam_ref, dx_s, dy_s, wsem,
             an_s, u_s, dh_s):
        n, b = pl.program_id(0), pl.program_id(1)

        def compute(slot):
            x = x_ref[...].astype(F32)
            cw = cw_ref[...]
            wr, wi = wr_ref[...], wi_ref[...]
            lam = lam_ref[...]
            xc, xcb, r, ig, sp, a, sq = _lru_gates(x, cw, cb_ref[...], wr, wi, br_ref[...], bi_ref[...], lam)
            hv = h_ref[...]
            gel, dgel = _gelu_and_grad(y_ref[...].astype(F32))
            dbin = dbin_ref[...].astype(F32)
            dy_s[slot] = (dbin * hv * dgel).astype(BF16)
            an_s[...] = _shift_up(a, 1, 0.0)
            u_s[...] = dbin * gel
            _scan_backward_ref(an_s, u_s, dh_s)
            dh = dh_s[...]
            hprev = _shift_down(hv, 1, 0.0)
            d_ig = dh * sq * xc
            d_xc = dh * sq * ig
            a2 = a * a
            d_loga = dh * hprev * a - dh * ig * xc * a2 / sq
            d_r = d_loga * ((-LRU_C) * sp)
            d_sp = jnp.sum(d_loga * ((-LRU_C) * r), axis=0, keepdims=True)
            dlam = -d_sp * _sigmoid(-lam)
            d_pr = d_r * r * (1.0 - r)
            d_pi = d_ig * ig * (1.0 - ig)
            d_prb, d_pib = d_pr.astype(BF16), d_pi.astype(BF16)
            d_xc = d_xc + _dot(d_prb, wr, NT) + _dot(d_pib, wi, NT)

            @pl.when(b == 0)
            def _():
                for ref in (dcw_ref, dcb_ref, dwr_ref, dwi_ref, dbr_ref, dbi_ref, dlam_ref):
                    ref[...] = jnp.zeros_like(ref)

            dx = cw[LRU_CONV - 1:LRU_CONV, :] * d_xc
            for j in range(LRU_CONV - 1):
                sft = LRU_CONV - 1 - j
                dx = dx + cw[j:j + 1, :] * _shift_up(d_xc, sft, 0.0)
                dcw_ref[j:j + 1, :] += jnp.sum(d_xc * _shift_down(x, sft, 0.0), axis=0, keepdims=True)
            dcw_ref[LRU_CONV - 1:LRU_CONV, :] += jnp.sum(d_xc * x, axis=0, keepdims=True)
            dx_s[slot] = dx.astype(BF16)
            dcb_ref[...] += jnp.sum(d_xc, axis=0, keepdims=True)
            dwr_ref[...] += _dot(xcb, d_prb, TN)
            dwi_ref[...] += _dot(xcb, d_pib, TN)
            dbr_ref[...] += jnp.sum(d_pr, axis=0, keepdims=True)
            dbi_ref[...] += jnp.sum(d_pi, axis=0, keepdims=True)
            dlam_ref[...] += dlam

        cols = [(COL_XL + n) * LRU_BLOCK, (COL_YL + n) * LRU_BLOCK]
        _write_pieces(dp_ref, wsem, [dx_s, dy_s], b * S, cols, [n, b], grid, compute)

    blk = lambda off: pl.BlockSpec((S, LRU_BLOCK), lambda n, b: (b, off + n))
    vec = lambda rows: pl.BlockSpec((rows, LRU_BLOCK), lambda n, b: (0, n))
    wspec = pl.BlockSpec((None, LRU_BLOCK, LRU_BLOCK), lambda n, b: (n, 0, 0))
    vshape = lambda rows: jax.ShapeDtypeStruct((rows, W), F32)
    wshape = jax.ShapeDtypeStruct((LRU_BLOCKS, LRU_BLOCK, LRU_BLOCK), F32)
    return _pcall(
        body, name=name, grid=grid,
        in_specs=[blk(0), blk(0), blk(COL_XL), blk(COL_YL), vec(LRU_CONV), vec(1), wspec, wspec, vec(1), vec(1),
                  vec(1), HBM_SPEC],
        out_specs=[HBM_SPEC, vec(LRU_CONV), vec(1), wspec, wspec, vec(1), vec(1), vec(1)],
        out_shape=[jax.ShapeDtypeStruct(dproj.shape, dproj.dtype),
                   vshape(LRU_CONV), vshape(1), wshape, wshape, vshape(1), vshape(1), vshape(1)],
        args=[db_in, h, proj, proj, cw, cb, wr, wi, br, bi, lam, dproj],
        scratch=[pltpu.VMEM((2, S, LRU_BLOCK), BF16), pltpu.VMEM((2, S, LRU_BLOCK), BF16), pltpu.SemaphoreType.DMA((2, 2)),
                 pltpu.VMEM((S, LRU_BLOCK), F32), pltpu.VMEM((S, LRU_BLOCK), F32), pltpu.VMEM((S, LRU_BLOCK), F32)],
        sem=("arbitrary", "arbitrary"), jobs=jobs, alias_in_out={11: 0})


FFN_CT = 256


def _ffn_conv(gate, cw, cb):
    gc = cb + cw[FFN_CONV - 1:FFN_CONV, :] * gate
    for j in range(FFN_CONV - 1):
        gc = gc + cw[j:j + 1, :] * _shift_down(gate, FFN_CONV - 1 - j, 0.0)
    return gc


def _ffn_act_fwd(up, cw, cb, B, S, name):
    T = B * S
    nct = D_FF // FFN_CT

    def body(g_ref, v_ref, cw_ref, cb_ref, f_ref):
        gc = _ffn_conv(g_ref[...].astype(F32), cw_ref[...], cb_ref[...])
        f_ref[...] = (_gelu(gc) * v_ref[...].astype(F32)).astype(BF16)

    return _pcall(
        body, name=name, grid=(B, nct),
        in_specs=[pl.BlockSpec((S, FFN_CT), lambda b, c: (b, c)), pl.BlockSpec((S, FFN_CT), lambda b, c: (b, nct + c)),
                  pl.BlockSpec((FFN_CONV, FFN_CT), lambda b, c: (0, c)), pl.BlockSpec((1, FFN_CT), lambda b, c: (0, c))],
        out_specs=[pl.BlockSpec((S, FFN_CT), lambda b, c: (b, c))],
        out_shape=[jax.ShapeDtypeStruct((T, D_FF), BF16)], args=[up, up, cw, cb], sem=("parallel", "parallel"))[0]


def _ffn_act_bwd(df, up, cw, cb, B, S, name, jobs=()):
    T = B * S
    nct = D_FF // FFN_CT

    grid = (nct, B)

    def body(df_ref, g_ref, v_ref, cw_ref, cb_ref, du_ref, dcw_ref, dcb_ref, dg_s, dv_s, wsem):
        c, b = pl.program_id(0), pl.program_id(1)

        def compute(slot):
            gate = g_ref[...].astype(F32)
            cw = cw_ref[...]
            gc = _ffn_conv(gate, cw, cb_ref[...])
            gel, dgel = _gelu_and_grad(gc)
            dfv = df_ref[...].astype(F32)
            dv_s[slot] = (dfv * gel).astype(BF16)
            dgc = dfv * v_ref[...].astype(F32) * dgel

            @pl.when(b == 0)
            def _():
                dcw_ref[...] = jnp.zeros_like(dcw_ref)
                dcb_ref[...] = jnp.zeros_like(dcb_ref)

            dgate = cw[FFN_CONV - 1:FFN_CONV, :] * dgc
            for j in range(FFN_CONV - 1):
                sft = FFN_CONV - 1 - j
                dgate = dgate + cw[j:j + 1, :] * _shift_up(dgc, sft, 0.0)
                dcw_ref[j:j + 1, :] += jnp.sum(dgc * _shift_down(gate, sft, 0.0), axis=0, keepdims=True)
            dcw_ref[FFN_CONV - 1:FFN_CONV, :] += jnp.sum(dgc * gate, axis=0, keepdims=True)
            dg_s[slot] = dgate.astype(BF16)
            dcb_ref[...] += jnp.sum(dgc, axis=0, keepdims=True)

        _write_pieces(du_ref, wsem, [dg_s, dv_s], b * S, [c * FFN_CT, (nct + c) * FFN_CT], [c, b], grid, compute)

    blk = pl.BlockSpec((S, FFN_CT), lambda c, b: (b, c))
    return _pcall(
        body, name=name, grid=grid,
        in_specs=[blk, blk, pl.BlockSpec((S, FFN_CT), lambda c, b: (b, nct + c)),
                  pl.BlockSpec((FFN_CONV, FFN_CT), lambda c, b: (0, c)),
                  pl.BlockSpec((1, FFN_CT), lambda c, b: (0, c))],
        out_specs=[HBM_SPEC, pl.BlockSpec((FFN_CONV, FFN_CT), lambda c, b: (0, c)),
                   pl.BlockSpec((1, FFN_CT), lambda c, b: (0, c))],
        out_shape=[jax.ShapeDtypeStruct((T, 2 * D_FF), BF16),
                   jax.ShapeDtypeStruct((FFN_CONV, D_FF), F32), jax.ShapeDtypeStruct((1, D_FF), F32)],
        args=[df, up, up, cw, cb],
        scratch=[pltpu.VMEM((2, S, FFN_CT), BF16), pltpu.VMEM((2, S, FFN_CT), BF16), pltpu.SemaphoreType.DMA((2, 2))],
        sem=("arbitrary", "arbitrary"), jobs=jobs)


def _rs_add(g, recv, mode, core, name):
    shard = tuple(recv.shape[1:])
    if mode == "mid":
        a, e, c2 = shard
        g_in = g.reshape(a, N_DEV, e, c2)
        grid = (4, 1)
        g_spec = pl.BlockSpec((a, None, e, c2), lambda k, i, c_ref: (0, 2 * k + c_ref[0], 0, 0))
        r_spec = pl.BlockSpec((None, a, e, c2), lambda k, i, c_ref: (k, 0, 0, 0))
    else:
        R, C = shard
        tr = _row_tile(R, 512)
        grid = (4, R // tr)
        if mode == "rows":
            g_in = g.reshape(N_DEV, R, C)
            g_spec = pl.BlockSpec((None, tr, C), lambda k, i, c_ref: (2 * k + c_ref[0], i, 0))
        else:
            g_in = g
            g_spec = pl.BlockSpec((tr, C), lambda k, i, c_ref: (i, 2 * k + c_ref[0]))
        r_spec = pl.BlockSpec((None, tr, C), lambda k, i, c_ref: (k, i, 0))

    def body(c_ref, g_ref, r_ref, o_ref):
        o_ref[...] = g_ref[...] + r_ref[...]

    return pl.pallas_call(
        body, name=name,
        grid_spec=pltpu.PrefetchScalarGridSpec(num_scalar_prefetch=1, grid=grid, in_specs=[g_spec, r_spec],
                                               out_specs=r_spec),
        out_shape=jax.ShapeDtypeStruct(recv.shape, recv.dtype),
        compiler_params=pltpu.CompilerParams(dimension_semantics=("parallel", "parallel"),
                                             vmem_limit_bytes=VMEM_LIMIT),
    )(core, g_in, recv)


def _adam_update(gv, w, m, v):
    nm = ADAM_B1 * m + (1.0 - ADAM_B1) * gv
    nv = ADAM_B2 * v + (1.0 - ADAM_B2) * (gv * gv)
    m_hat = nm / (1.0 - ADAM_B1 ** ADAM_STEP)
    v_hat = nv / (1.0 - ADAM_B2 ** ADAM_STEP)
    delta = -ADAM_LR * (m_hat / (jnp.sqrt(v_hat) + ADAM_EPS) + ADAM_WD * w)
    return delta, nm, nv


def _adamw_shard(partial, recv, w, m, v, chip, name):
    shape = tuple(w.shape)
    tr = _row_tile(shape[0], 256)
    rest = shape[1:]
    zeros = (0,) * len(rest)
    tile = pl.BlockSpec((tr,) + rest, lambda i, s: (i,) + zeros)

    def body(_, p_ref, r_ref, w_ref, m_ref, v_ref, g_ref, d_ref, nm_ref, nv_ref):
        gv = p_ref[...] + r_ref[0] + r_ref[1] + r_ref[2]
        g_ref[...] = gv
        d_ref[...], nm_ref[...], nv_ref[...] = _adam_update(gv, w_ref[...], m_ref[...], v_ref[...])

    grid_spec = pltpu.PrefetchScalarGridSpec(
        num_scalar_prefetch=1, grid=(shape[0] // tr,),
        in_specs=[pl.BlockSpec((None, tr) + rest, lambda i, s: (s[0], i) + zeros),
                  pl.BlockSpec((3, tr) + rest, lambda i, s: (0, i) + zeros), tile, tile, tile],
        out_specs=[tile] * 4)
    return pl.pallas_call(
        body, name=name, grid_spec=grid_spec, out_shape=[jax.ShapeDtypeStruct(shape, F32)] * 4,
        compiler_params=pltpu.CompilerParams(dimension_semantics=("parallel",), vmem_limit_bytes=VMEM_LIMIT),
    )(chip, partial, recv, w, m, v)


def _all_gather_multi(shards, modes, name):
    n = len(shards)
    extents = [_extent(s.shape, m) for s, m in zip(shards, modes)]

    def body(*refs):
        x_refs, out_refs = refs[:n], refs[n:2 * n]
        send_sems, recv_sems, local_sems = refs[2 * n:]
        x, y, c = _mesh_pos()
        me, sibling = (x, y, c), (x, y, 1 - c)
        chips = _other_chips(x, y)

        def slot(i, px, py, pc):
            return _window(out_refs[i], modes[i], extents[i], 4 * px + 2 * py + pc)

        def copy(i, k, block, to, src=None):
            return _remote(slot(i, *block) if src is None else src, slot(i, *block),
                           send_sems.at[i, k], recv_sems.at[i, k], to)

        mine = [pltpu.make_async_copy(x_refs[i], slot(i, *me), local_sems.at[i]) for i in range(n)]
        sends = []
        for i in range(n):
            mine[i].start()
            first = [copy(i, 0, me, sibling, src=x_refs[i])]
            first += [copy(i, 1 + j, me, (*chip, c), src=x_refs[i]) for j, chip in enumerate(chips)]
            for cp in first:
                cp.start()
            sends += first
        for i in range(n):
            for j, chip in enumerate(chips):
                copy(i, 1 + j, (*chip, c), me).wait_recv()
                fwd = copy(i, 4 + j, (*chip, c), sibling)
                fwd.start()
                sends.append(fwd)
        for i in range(n):
            copy(i, 0, sibling, me).wait_recv()
            for j, chip in enumerate(chips):
                copy(i, 4 + j, (*chip, 1 - c), me).wait_recv()
        for cp in sends:
            cp.wait_send()
        for cp in mine:
            cp.wait()

    return pl.pallas_call(
        body, name=name,
        in_specs=[HBM_SPEC] * n, out_specs=[HBM_SPEC] * n,
        out_shape=[jax.ShapeDtypeStruct(_full_shape(s.shape, m), s.dtype) for s, m in zip(shards, modes)],
        scratch_shapes=[pltpu.SemaphoreType.DMA((n, 7)), pltpu.SemaphoreType.DMA((n, 7)),
                        pltpu.SemaphoreType.DMA((n,))],
    )(*shards)


SMALL_LANES = 1024


def _small_rows(shape):
    r, w = shape
    return r * max(1, w // SMALL_LANES)


def _small_allreduce(parts, name):
    n = len(parts)
    shapes = [tuple(p.shape) for p in parts]
    offs, total = [], 0
    for s in shapes:
        offs.append(total)
        total += _small_rows(s)
    rows = -(-total // 8) * 8

    def body(*refs):
        p_refs, o_refs = refs[:n], refs[n:2 * n]
        buf, tot, send_sems, recv_sems = refs[2 * n:]
        x, y, c = _mesh_pos()
        me, sibling = (x, y, c), (x, y, 1 - c)
        chips = _other_chips(x, y)

        def slot(px, py, pc):
            return buf.at[4 * px + 2 * py + pc]

        def copy(k, block, to):
            return _remote(slot(*block), slot(*block), send_sems.at[k], recv_sems.at[k], to)

        tot[...] = jnp.zeros_like(tot)
        for p_ref, (r, w), off in zip(p_refs, shapes, offs):
            wl = min(w, SMALL_LANES)
            for part in range(max(1, w // SMALL_LANES)):
                tot[pl.ds(off + part * r, r), pl.ds(0, wl)] = p_ref[:, pl.ds(part * SMALL_LANES, wl)]
        buf[4 * x + 2 * y + c] = tot[...]
        first = [copy(0, me, sibling)] + [copy(1 + j, me, (*chip, c)) for j, chip in enumerate(chips)]
        for cp in first:
            cp.start()
        passed = [copy(4 + j, (*chip, c), sibling) for j, chip in enumerate(chips)]
        for j, chip in enumerate(chips):
            copy(1 + j, (*chip, c), me).wait_recv()
            passed[j].start()
        copy(0, sibling, me).wait_recv()
        for j, chip in enumerate(chips):
            copy(4 + j, (*chip, 1 - c), me).wait_recv()
        for cp in first + passed:
            cp.wait_send()
        acc = buf[0]
        for d in range(1, N_DEV):
            acc = acc + buf[d]
        tot[...] = acc
        for o_ref, (r, w), off in zip(o_refs, shapes, offs):
            wl = min(w, SMALL_LANES)
            for part in range(max(1, w // SMALL_LANES)):
                o_ref[:, pl.ds(part * SMALL_LANES, wl)] = tot[pl.ds(off + part * r, r), pl.ds(0, wl)]

    vm = pl.BlockSpec(memory_space=pltpu.VMEM)
    return pl.pallas_call(
        body, name=name,
        in_specs=[vm] * n, out_specs=[vm] * n,
        out_shape=[jax.ShapeDtypeStruct(s, F32) for s in shapes],
        scratch_shapes=[pltpu.VMEM((N_DEV, rows, SMALL_LANES), F32), pltpu.VMEM((rows, SMALL_LANES), F32),
                        pltpu.SemaphoreType.DMA((7,)), pltpu.SemaphoreType.DMA((7,))],
    )(*parts)


def _adamw_small(gs, ws, ms, vs, name):
    n = len(gs)

    def body(*refs):
        g_r, w_r, m_r, v_r = refs[:n], refs[n:2 * n], refs[2 * n:3 * n], refs[3 * n:4 * n]
        d_r, nm_r, nv_r = refs[4 * n:5 * n], refs[5 * n:6 * n], refs[6 * n:7 * n]
        for i in range(n):
            d_r[i][...], nm_r[i][...], nv_r[i][...] = _adam_update(g_r[i][...], w_r[i][...], m_r[i][...], v_r[i][...])

    vm = pl.BlockSpec(memory_space=pltpu.VMEM)
    shapes = [jax.ShapeDtypeStruct(w.shape, F32) for w in ws]
    outs = pl.pallas_call(body, name=name, in_specs=[vm] * (4 * n), out_specs=[vm] * (3 * n),
                          out_shape=shapes * 3)(*gs, *ws, *ms, *vs)
    return outs[:n], outs[n:2 * n], outs[2 * n:]


FIRST = [("w_in", (1024, 896), "cols"), ("lru_w_r", (4, 32, 256), "mid"), ("lru_w_i", (4, 32, 256), "mid")]
LATE = [("w_ret_o", (128, 1024), "rows"), ("w_lru_o", (128, 1024), "rows"), ("w_out", (128, 1024), "rows"),
        ("ffn_w_up", (1024, 768), "cols"), ("ffn_w_down", (384, 1024), "rows")]
BIG = FIRST + LATE
SMALL_SHARDED = [("merge_gate_b", (2, 128), "cols"), ("lru_conv_w", (4, 128), "cols"), ("lru_b_r", (4, 32), "stack"),
                 ("lru_b_i", (4, 32), "stack"), ("ffn_conv_w", (3, 384), "cols")]
REPLICATED = [("norm1_w", (1, 1024)), ("ret_gn_w", (1, 1024)), ("lru_conv_b", (1, 1024)), ("lru_lambda", (1, 1024)),
              ("norm2_w", (1, 1024)), ("ffn_conv_b", (1, 3072)), ("norm_f_w", (1, 1024))]
MODE = {n: m for n, _, m in BIG}
SHARD = {n: s for n, s, _ in BIG}


def _local_step(x3, positions, target3, wb, ws, late_shards, core):
    B, S, D = x3.shape
    T = B * S
    x = x3.reshape(T, D)
    target = target3.reshape(T, D)
    tm = min(512, T)
    big = min(1024, T)
    big2 = min(2048, T)

    half = RET_DK // 2
    inv_freq = ROPE_BASE ** (-jnp.arange(half, dtype=F32) / half)
    inv2 = jnp.concatenate([inv_freq, inv_freq]).reshape(1, RET_DK)
    log_gamma = jnp.log1p(-jnp.power(2.0, -5.0 - jnp.arange(RET_HEADS, dtype=F32)))
    lgam = jnp.broadcast_to(log_gamma[:, None, None], (RET_HEADS, 8, LANES))
    pos_col = positions.astype(F32).reshape(T, 1)
    cos2, sin2s = _rope_tables(pos_col, inv2, tm, "rope_tables")

    late_names = [n for n, _, _ in LATE]
    late_modes = [m for _, _, m in LATE]
    late_shapes = [s for _, s, _ in LATE]

    h1 = _rmsnorm_fwd(x, ws["norm1_w"], tm, "norm1_fwd")
    proj, *late_part = _matmul(h1, wb["w_in"], "nn", BF16, big2, 1024, 1024, "proj_fwd",
                               jobs=[_ag_first_job(late_shards, late_modes)])
    o, a_in, *late_full = _retention_fwd(proj, cos2, sin2s, lgam, ws["ret_gn_w"], B, S, "retention_fwd",
                                         jobs=[_ag_second_job(late_part, late_modes, late_shapes)])
    wb = dict(wb, **dict(zip(late_names, late_full)))
    hl, b_in = _lru_fwd(proj, ws["lru_conv_w"], ws["lru_conv_b"], wb["lru_w_r"], wb["lru_w_i"],
                        ws["lru_b_r"], ws["lru_b_i"], ws["lru_lambda"], B, S, "lru_fwd")
    x1, mix, h2 = _mix_fwd(a_in, b_in, proj, x, wb["w_ret_o"], wb["w_lru_o"], wb["w_out"],
                           ws["merge_gate_b"], ws["norm2_w"], tm, "mix_fwd")
    up = _matmul(h2, wb["ffn_w_up"], "nn", BF16, big2, 1024, 1024, "ffn_up_fwd")[0]
    f = _ffn_act_fwd(up, ws["ffn_conv_w"], ws["ffn_conv_b"], B, S, "ffn_act_fwd")
    x2 = _matmul(f, wb["ffn_w_down"], "nn", F32, big, 1024, D_FF, "ffn_down_fwd", add=x1)[0]
    dx2, dx2b, loss_acc, d_norm_f = _loss_head(x2, target, ws["norm_f_w"], tm, "loss_head")

    g, rs = {}, {}

    def stage1(names, grads):
        return _rs_sibling_job(grads, [MODE[n] for n in names], [SHARD[n] for n in names])

    def add(names, grads, recvs):
        return [_rs_add(gr, r, MODE[n], core, "rs_add_" + n) for n, gr, r in zip(names, grads, recvs)]

    g["norm_f_w"] = d_norm_f
    g_down = _matmul(f, dx2b, "tn", F32, 1024, 1024, big2, "ffn_down_bwd_w")[0]
    df, s1_down = _matmul(dx2b, wb["ffn_w_down"], "nt", BF16, big2, 1024, 1024, "ffn_down_bwd_x",
                          jobs=[stage1(["ffn_w_down"], [g_down])])
    p_down = add(["ffn_w_down"], [g_down], [s1_down])
    dup, g["ffn_conv_w"], g["ffn_conv_b"], s2_down = _ffn_act_bwd(
        df, up, ws["ffn_conv_w"], ws["ffn_conv_b"], B, S, "ffn_act_bwd", jobs=[_rs_chip_job(p_down)])
    rs["ffn_w_down"] = (p_down[0], s2_down)

    g_up = _matmul(h2, dup, "tn", F32, 1024, 1024, big2, "ffn_up_bwd_w")[0]
    dh2, s1_up = _matmul(dup, wb["ffn_w_up"], "nt", BF16, big, 1024, D_FF, "ffn_up_bwd_x",
                         jobs=[stage1(["ffn_w_up"], [g_up])])
    p_up = add(["ffn_w_up"], [g_up], [s1_up])
    dx1, dx1b, g["norm2_w"] = _rmsnorm_bwd_add(dx2, dh2, x1, ws["norm2_w"], tm, "norm2_bwd", True)
    da_in, db_in, dya, dyb, dproj, g["merge_gate_b"] = _mix_bwd(
        dx1b, a_in, b_in, proj, wb["w_ret_o"], wb["w_lru_o"], wb["w_out"], ws["merge_gate_b"], tm, "mix_bwd")

    mid_names = ["w_out", "w_ret_o", "w_lru_o"]
    g_mid = [_matmul(mix, dx1b, "tn", F32, 1024, 1024, big2, "w_out_bwd_w")[0],
             _matmul(a_in, dya, "tn", F32, 1024, 1024, big2, "w_ret_o_bwd_w")[0],
             _matmul(b_in, dyb, "tn", F32, 1024, 1024, big2, "w_lru_o_bwd_w")[0]]
    (dproj, g["lru_conv_w"], g["lru_conv_b"], g_wr, g_wi, g["lru_b_r"], g["lru_b_i"], g["lru_lambda"], s2_up,
     *s1_mid) = _lru_bwd(db_in, hl, proj, dproj, ws["lru_conv_w"], ws["lru_conv_b"], wb["lru_w_r"], wb["lru_w_i"],
                         ws["lru_b_r"], ws["lru_b_i"], ws["lru_lambda"], B, S, "lru_bwd",
                         jobs=[_rs_chip_job(p_up), stage1(mid_names, g_mid)])
    rs["ffn_w_up"] = (p_up[0], s2_up)
    p_mid = add(mid_names, g_mid, s1_mid)
    lru_names = ["lru_w_r", "lru_w_i"]
    dproj, g["ret_gn_w"], *rest = _retention_bwd(
        da_in, o, proj, dproj, cos2, sin2s, lgam, ws["ret_gn_w"], B, S, "retention_bwd",
        jobs=[_rs_chip_job(p_mid), stage1(lru_names, [g_wr, g_wi])])
    s2_mid, s1_lru = rest[:3], rest[3:]
    for n, p, r in zip(mid_names, p_mid, s2_mid):
        rs[n] = (p, r)
    p_lru = add(lru_names, [g_wr, g_wi], s1_lru)

    g_in, *s2_lru = _matmul(h1, dproj, "tn", F32, 1024, 1024, big2, "proj_bwd_w", jobs=[_rs_chip_job(p_lru)])
    for n, p, r in zip(lru_names, p_lru, s2_lru):
        rs[n] = (p, r)
    s1_in = _pcall(lambda: None, name="rs_sibling_w_in", grid=(1,), in_specs=[], out_specs=[], out_shape=[], args=[],
                   sem=("arbitrary",), jobs=[stage1(["w_in"], [g_in])])
    p_in = add(["w_in"], [g_in], s1_in)
    dh1, s2_in = _matmul(dproj, wb["w_in"], "nt", BF16, big, 1024, D_IN // 2, "proj_bwd_x",
                         jobs=[_rs_chip_job(p_in)])
    grad_x, g["norm1_w"] = _rmsnorm_bwd_add(dx1, dh1, x, ws["norm1_w"], tm, "norm1_bwd", False)
    rs["w_in"] = (p_in[0], s2_in)
    return loss_acc, grad_x.reshape(B, S, D), g, rs


def kernel(x, positions, norm1_w, w_in, merge_gate_b, ret_gn_w, w_ret_o, lru_conv_w, lru_conv_b, lru_w_r, lru_b_r, lru_w_i, lru_b_i, lru_lambda, w_lru_o, w_out, norm2_w, ffn_w_up, ffn_conv_w, ffn_conv_b, ffn_w_down, norm_f_w, loss_target, m_norm1_w, m_w_in, m_merge_gate_b, m_ret_gn_w, m_w_ret_o, m_lru_conv_w, m_lru_conv_b, m_lru_w_r, m_lru_b_r, m_lru_w_i, m_lru_b_i, m_lru_lambda, m_w_lru_o, m_w_out, m_norm2_w, m_ffn_w_up, m_ffn_conv_w, m_ffn_conv_b, m_ffn_w_down, m_norm_f_w, v_norm1_w, v_w_in, v_merge_gate_b, v_ret_gn_w, v_w_ret_o, v_lru_conv_w, v_lru_conv_b, v_lru_w_r, v_lru_b_r, v_lru_w_i, v_lru_b_i, v_lru_lambda, v_w_lru_o, v_w_out, v_norm2_w, v_ffn_w_up, v_ffn_conv_w, v_ffn_conv_b, v_ffn_w_down, v_norm_f_w):
    names = ["norm1_w", "w_in", "merge_gate_b", "ret_gn_w", "w_ret_o", "lru_conv_w", "lru_conv_b", "lru_w_r", "lru_b_r",
             "lru_w_i", "lru_b_i", "lru_lambda", "w_lru_o", "w_out", "norm2_w", "ffn_w_up", "ffn_conv_w", "ffn_conv_b",
             "ffn_w_down", "norm_f_w"]
    w_args = [norm1_w, w_in, merge_gate_b, ret_gn_w, w_ret_o, lru_conv_w, lru_conv_b, lru_w_r, lru_b_r, lru_w_i, lru_b_i,
              lru_lambda, w_lru_o, w_out, norm2_w, ffn_w_up, ffn_conv_w, ffn_conv_b, ffn_w_down, norm_f_w]
    m_args = [m_norm1_w, m_w_in, m_merge_gate_b, m_ret_gn_w, m_w_ret_o, m_lru_conv_w, m_lru_conv_b, m_lru_w_r, m_lru_b_r,
              m_lru_w_i, m_lru_b_i, m_lru_lambda, m_w_lru_o, m_w_out, m_norm2_w, m_ffn_w_up, m_ffn_conv_w, m_ffn_conv_b,
              m_ffn_w_down, m_norm_f_w]
    v_args = [v_norm1_w, v_w_in, v_merge_gate_b, v_ret_gn_w, v_w_ret_o, v_lru_conv_w, v_lru_conv_b, v_lru_w_r, v_lru_b_r,
              v_lru_w_i, v_lru_b_i, v_lru_lambda, v_w_lru_o, v_w_out, v_norm2_w, v_ffn_w_up, v_ffn_conv_w, v_ffn_conv_b,
              v_ffn_w_down, v_norm_f_w]
    orig_shape = {n: a.shape for n, a in zip(names, w_args)}
    local_shape = {n: s for n, s, _ in BIG + SMALL_SHARDED}
    local_shape.update({n: s for n, s in REPLICATED})
    W = {n: a.reshape(local_shape[n]) for n, a in zip(names, w_args)}
    M = {n: a.reshape(local_shape[n]) for n, a in zip(names, m_args)}
    V = {n: a.reshape(local_shape[n]) for n, a in zip(names, v_args)}

    xi, yi, ci = _mesh_pos()
    dev = 4 * xi + 2 * yi + ci
    chip = (2 * xi + yi).astype(jnp.int32).reshape(1)
    core = ci.astype(jnp.int32).reshape(1)

    first_names = [n for n, _, _ in FIRST]
    small_names = [n for n, _, _ in SMALL_SHARDED]
    gathered = _all_gather_multi([W[n].astype(BF16) for n in first_names] + [W[n] for n in small_names],
                                 [m for _, _, m in FIRST + SMALL_SHARDED], "gather_first_weights")
    wb = dict(zip(first_names, gathered[:len(FIRST)]))
    ws = dict(zip(small_names, gathered[len(FIRST):]))
    for n in ("lru_b_r", "lru_b_i"):
        ws[n] = jnp.transpose(ws[n], (1, 0, 2)).reshape(1, LRU_BLOCKS * LRU_BLOCK)
    for n, _ in REPLICATED:
        ws[n] = W[n]

    late_shards = [W[n].astype(BF16) for n, _, _ in LATE]
    loss_acc, grad_x, g, rs = _local_step(x, positions, loss_target, wb, ws, late_shards, core)

    G_out, D_out, M_out, V_out = {}, {}, {}, {}
    for n, _, _ in BIG:
        G_out[n], D_out[n], M_out[n], V_out[n] = _adamw_shard(rs[n][0], rs[n][1], W[n], M[n], V[n], chip, "adamw_" + n)

    rep_names = [n for n, _ in REPLICATED]
    red_names = rep_names + small_names
    red = _small_allreduce([g[n] for n in red_names] + [loss_acc[0:1, :]], "allreduce_small_grads")
    loss = red[-1][0, 0]
    gs = dict(zip(red_names, red[:-1]))
    for n, s, mode in SMALL_SHARDED:
        if mode == "cols":
            gs[n] = lax.dynamic_slice_in_dim(gs[n], dev * s[1], s[1], axis=1)
        else:
            full = gs[n].reshape(LRU_BLOCKS, LRU_BLOCK)
            gs[n] = lax.dynamic_slice_in_dim(full, dev * s[1], s[1], axis=1)
    d2, m2, v2 = _adamw_small([gs[n] for n in red_names], [W[n] for n in red_names], [M[n] for n in red_names],
                              [V[n] for n in red_names], "adamw_small")
    for i, n in enumerate(red_names):
        G_out[n], D_out[n], M_out[n], V_out[n] = gs[n], d2[i], m2[i], v2[i]

    outs = [loss, grad_x]
    for group in (G_out, D_out, M_out, V_out):
        outs += [group[n].reshape(orig_shape[n]) for n in names]
    return tuple(outs)
```

```python
import math

import jax
import jax.numpy as jnp
from jax import lax
from jax.experimental import pallas as pl
from jax.experimental.pallas import tpu as pltpu

F32 = jnp.float32
BF16 = jnp.bfloat16
MESH = pl.DeviceIdType.MESH

D_MODEL = 1024
CHUNK = 64
RET_HEADS = 4
RET_DK = 128
RET_DV = 256
LRU_BLOCKS = 4
LRU_BLOCK = 256
LRU_CONV = 4
LRU_C = 8.0
D_FF = 3072
FFN_CONV = 3
ROPE_BASE = 10000.0
RMS_EPS = 1e-6
GN_EPS = 1e-6
D_IN = 7168
ADAM_LR, ADAM_B1, ADAM_B2, ADAM_EPS, ADAM_WD, ADAM_STEP = 0.001, 0.9, 0.999, 1e-08, 0.01, 10

N_DEV = 8
V7X_VMEM_BYTES = 64 * 1024 * 1024
VMEM_LIMIT = V7X_VMEM_BYTES - 8 * 1024 * 1024
RET_BLOCK = 256
LANES = 128

COL_Q, COL_K = 0, 4
COL_V, COL_G, COL_XL, COL_YL = 4, 8, 12, 16
COL_GR, COL_GL = 5, 6

HBM_SPEC = pl.BlockSpec(memory_space=pl.ANY)


def _gelu(x):
    c = math.sqrt(2.0 / math.pi)
    t = jnp.tanh(c * (x + 0.044715 * x * x * x))
    return 0.5 * x * (1.0 + t)


def _gelu_and_grad(x):
    c = math.sqrt(2.0 / math.pi)
    x2 = x * x
    t = jnp.tanh(c * (x + 0.044715 * x2 * x))
    g = 0.5 * x * (1.0 + t)
    dg = 0.5 * (1.0 + t) + 0.5 * x * (1.0 - t * t) * c * (1.0 + 3.0 * 0.044715 * x2)
    return g, dg


def _sigmoid(x):
    return 1.0 / (1.0 + jnp.exp(-x))


SUBLANES = 8


def _shift_down(x, s, fill):
    r = pltpu.roll(x, s, 0)
    rows = lax.broadcasted_iota(jnp.int32, (SUBLANES,) + x.shape[1:], 0)
    top = jnp.where(rows >= s, r[:SUBLANES], fill)
    return jnp.concatenate([top, r[SUBLANES:]], axis=0)


def _shift_up(x, s, fill):
    n = x.shape[0]
    r = pltpu.roll(x, n - s, 0)
    rows = lax.broadcasted_iota(jnp.int32, (SUBLANES,) + x.shape[1:], 0)
    bottom = jnp.where(rows < SUBLANES - s, r[n - SUBLANES:], fill)
    return jnp.concatenate([r[:n - SUBLANES], bottom], axis=0)


SCAN_CHUNK = 64


def _scan_forward(a, b):
    n = a.shape[0]
    s = 1
    while s < n:
        if s % SUBLANES:
            b = a * _shift_down(b, s, 0.0) + b
            a = a * _shift_down(a, s, 1.0)
        else:
            b = jnp.concatenate([b[:s], a[s:] * b[:n - s] + b[s:]], axis=0)
            a = jnp.concatenate([a[:s], a[s:] * a[:n - s]], axis=0)
        s *= 2
    return a, b


def _scan_backward(a_next, u):
    n = u.shape[0]
    s = 1
    while s < n:
        if s % SUBLANES:
            u = u + a_next * _shift_up(u, s, 0.0)
            a_next = a_next * _shift_up(a_next, s, 1.0)
        else:
            u = jnp.concatenate([u[:n - s] + a_next[:n - s] * u[s:], u[n - s:]], axis=0)
            a_next = jnp.concatenate([a_next[:n - s] * a_next[s:], a_next[n - s:]], axis=0)
        s *= 2
    return a_next, u


def _scan_forward_ref(a_ref, b_ref, h_ref):
    S, W = a_ref.shape
    for strip in range(W // LANES):
        cols = pl.ds(strip * LANES, LANES)

        def body(k, carry, cols=cols):
            rows = pl.ds(pl.multiple_of(k * SCAN_CHUNK, SCAN_CHUNK), SCAN_CHUNK)
            a_cum, h_loc = _scan_forward(a_ref[rows, cols], b_ref[rows, cols])
            h = h_loc + a_cum * carry
            h_ref[rows, cols] = h
            return h[SCAN_CHUNK - 1:, :]

        lax.fori_loop(0, S // SCAN_CHUNK, body, jnp.zeros((1, LANES), F32))


def _scan_backward_ref(an_ref, u_ref, d_ref):
    S, W = an_ref.shape
    n_chunks = S // SCAN_CHUNK
    for strip in range(W // LANES):
        cols = pl.ds(strip * LANES, LANES)

        def body(i, carry, cols=cols):
            rows = pl.ds(pl.multiple_of((n_chunks - 1 - i) * SCAN_CHUNK, SCAN_CHUNK), SCAN_CHUNK)
            an_cum, d_loc = _scan_backward(an_ref[rows, cols], u_ref[rows, cols])
            d = d_loc + an_cum * carry
            d_ref[rows, cols] = d
            return d[:1, :]

        lax.fori_loop(0, n_chunks, body, jnp.zeros((1, LANES), F32))


def _dot(a, b, dims):
    return lax.dot_general(a, b, (dims, ((), ())), preferred_element_type=F32)


NN = ((1,), (0,))
NT = ((1,), (1,))
TN = ((0,), (0,))


def _mesh_pos():
    return lax.axis_index("x"), lax.axis_index("y"), lax.axis_index("c")


def _other_chips(x, y):
    return [(1 - x, y), (x, 1 - y), (1 - x, 1 - y)]


def _full_shape(shard_shape, mode):
    if mode == "rows":
        return (N_DEV * shard_shape[0],) + tuple(shard_shape[1:])
    if mode == "cols":
        return (shard_shape[0], N_DEV * shard_shape[1])
    if mode == "mid":
        return (shard_shape[0], N_DEV * shard_shape[1], shard_shape[2])
    return (N_DEV,) + tuple(shard_shape)


def _extent(shard_shape, mode):
    return {"rows": shard_shape[0], "cols": shard_shape[1], "mid": shard_shape[1], "stack": 1}[mode]


def _window(ref, mode, extent, d):
    if mode == "stack":
        return ref.at[d]
    start = pl.multiple_of(d * extent, extent)
    if mode == "rows":
        return ref.at[pl.ds(start, extent)]
    if mode == "cols":
        return ref.at[:, pl.ds(start, extent)]
    return ref.at[:, pl.ds(start, extent), :]


class _Job:
    def __init__(self, inputs, out_shapes, sems, start, finish, aliases=None):
        self.inputs, self.out_shapes, self.sems = list(inputs), list(out_shapes), sems
        self.start, self.finish, self.aliases = start, finish, dict(aliases or {})


def _remote(src, dst, send_sem, recv_sem, to):
    return pltpu.make_async_remote_copy(src_ref=src, dst_ref=dst, send_sem=send_sem, recv_sem=recv_sem,
                                        device_id=to, device_id_type=MESH)


def _ag_first_job(shards, modes):
    n = len(shards)
    extents = [_extent(s.shape, m) for s, m in zip(shards, modes)]

    def copies(x_refs, out_refs, send, recv, local, arriving):
        x, y, c = _mesh_pos()
        peers = [(x, y, 1 - c)] + [(*chip, c) for chip in _other_chips(x, y)]
        win = lambda i, p: _window(out_refs[i], modes[i], extents[i], 4 * p[0] + 2 * p[1] + p[2])
        if arriving:
            return [_remote(x_refs[i], win(i, p), send.at[i, k], recv.at[i, k], p)
                    for i in range(n) for k, p in enumerate(peers)]
        mine = [pltpu.make_async_copy(x_refs[i], win(i, (x, y, c)), local.at[i]) for i in range(n)]
        sends = [_remote(x_refs[i], win(i, (x, y, c)), send.at[i, k], recv.at[i, k], p)
                 for i in range(n) for k, p in enumerate(peers)]
        return mine, sends

    def start(*refs):
        mine, sends = copies(*refs, False)
        for cp in mine + sends:
            cp.start()

    def finish(*refs):
        for cp in copies(*refs, True):
            cp.wait_recv()
        mine, sends = copies(*refs, False)
        for cp in sends:
            cp.wait_send()
        for cp in mine:
            cp.wait()

    out_shapes = [jax.ShapeDtypeStruct(_full_shape(s.shape, m), s.dtype) for s, m in zip(shards, modes)]
    return _Job(shards, out_shapes, ((n, 4), (n, 4), (n,)), start, finish)


def _ag_second_job(fulls, modes, shard_shapes):
    n = len(fulls)
    extents = [_extent(s, m) for s, m in zip(shard_shapes, modes)]

    def copies(_, out_refs, send, recv, local, core_of_block):
        x, y, c = _mesh_pos()
        pc = c if core_of_block == "mine" else 1 - c
        win = lambda i, chip: _window(out_refs[i], modes[i], extents[i], 4 * chip[0] + 2 * chip[1] + pc)
        return [_remote(win(i, chip), win(i, chip), send.at[i, j], recv.at[i, j], (x, y, 1 - c))
                for i in range(n) for j, chip in enumerate(_other_chips(x, y))]

    def start(*refs):
        for cp in copies(*refs, "mine"):
            cp.start()

    def finish(*refs):
        for cp in copies(*refs, "sibling"):
            cp.wait_recv()
        for cp in copies(*refs, "mine"):
            cp.wait_send()

    out_shapes = [jax.ShapeDtypeStruct(f.shape, f.dtype) for f in fulls]
    return _Job(fulls, out_shapes, ((n, 3), (n, 3), (1,)), start, finish, aliases={i: i for i in range(n)})


def _rs_sibling_job(grads, modes, shard_shapes):
    n = len(grads)
    extents = [_extent(s, m) for s, m in zip(shard_shapes, modes)]

    def copies(g_refs, out_refs, send, recv, local):
        x, y, c = _mesh_pos()
        return [_remote(_window(g_refs[i], modes[i], extents[i], 2 * k + (1 - c)), out_refs[i].at[k],
                        send.at[i, k], recv.at[i, k], (x, y, 1 - c))
                for i in range(n) for k in range(4)]

    def start(*refs):
        for cp in copies(*refs):
            cp.start()

    def finish(*refs):
        cps = copies(*refs)
        for cp in cps:
            cp.wait_recv()
        for cp in cps:
            cp.wait_send()

    out_shapes = [jax.ShapeDtypeStruct((4,) + tuple(s), g.dtype) for s, g in zip(shard_shapes, grads)]
    return _Job(grads, out_shapes, ((n, 4), (n, 4), (1,)), start, finish)


def _rs_chip_job(partials):
    n = len(partials)

    def copies(p_refs, out_refs, send, recv, local):
        x, y, c = _mesh_pos()
        return [_remote(p_refs[i].at[2 * px + py], out_refs[i].at[j], send.at[i, j], recv.at[i, j], (px, py, c))
                for i in range(n) for j, (px, py) in enumerate(_other_chips(x, y))]

    def start(*refs):
        for cp in copies(*refs):
            cp.start()

    def finish(*refs):
        cps = copies(*refs)
        for cp in cps:
            cp.wait_recv()
        for cp in cps:
            cp.wait_send()

    out_shapes = [jax.ShapeDtypeStruct((3,) + tuple(p.shape[1:]), p.dtype) for p in partials]
    return _Job(partials, out_shapes, ((n, 3), (n, 3), (1,)), start, finish)


def _all_true(conds):
    out = conds[0]
    for c in conds[1:]:
        out = jnp.logical_and(out, c)
    return out


def _pcall(body, *, name, grid, in_specs, out_specs, out_shape, args, sem, scratch=(), jobs=(), alias_in_out=None):
    n_in, n_out, n_scr = len(args), len(out_shape), len(scratch)
    job_in = [a for j in jobs for a in j.inputs]
    job_out = [s for j in jobs for s in j.out_shapes]
    job_sems = [pltpu.SemaphoreType.DMA(shape) for j in jobs for shape in j.sems]
    aliases, in_off, out_off = dict(alias_in_out or {}), n_in, n_out
    for j in jobs:
        for a, b in j.aliases.items():
            aliases[in_off + a] = out_off + b
        in_off += len(j.inputs)
        out_off += len(j.out_shapes)

    def wrapped(*refs):
        ins = refs[:n_in]
        jins = refs[n_in:n_in + len(job_in)]
        o0 = n_in + len(job_in)
        outs = refs[o0:o0 + n_out]
        jouts = refs[o0 + n_out:o0 + n_out + len(job_out)]
        s0 = o0 + n_out + len(job_out)
        scr = refs[s0:s0 + n_scr]
        jsems = refs[s0 + n_scr:]
        if jobs:
            ids = [pl.program_id(a) for a in range(len(grid))]
            first = _all_true([i == 0 for i in ids])
            last = _all_true([i == g - 1 for i, g in zip(ids, grid)])

            def per_job(which):
                i0 = o0_ = 0
                for k, j in enumerate(jobs):
                    fn = j.start if which == "start" else j.finish
                    fn(jins[i0:i0 + len(j.inputs)], jouts[o0_:o0_ + len(j.out_shapes)], *jsems[3 * k:3 * k + 3])
                    i0 += len(j.inputs)
                    o0_ += len(j.out_shapes)

            @pl.when(first)
            def _():
                per_job("start")

        body(*ins, *outs, *scr)
        if jobs:
            @pl.when(last)
            def _():
                per_job("finish")

    semantics = tuple("arbitrary" for _ in grid) if jobs else sem
    return pl.pallas_call(
        wrapped, name=name, grid=grid,
        in_specs=list(in_specs) + [HBM_SPEC] * len(job_in),
        out_specs=list(out_specs) + [HBM_SPEC] * len(job_out),
        out_shape=list(out_shape) + job_out,
        scratch_shapes=list(scratch) + job_sems,
        input_output_aliases=aliases,
        compiler_params=pltpu.CompilerParams(dimension_semantics=semantics, vmem_limit_bytes=VMEM_LIMIT),
    )(*args, *job_in)


def _row_tile(rows, cap):
    if rows <= cap:
        return rows
    best = None
    for t in range(16, cap + 1, 16):
        if rows % t == 0:
            best = t
    assert best is not None
    return best


def _matmul(a, b, mode, out_dtype, tm, tn, tk, name, add=None, jobs=()):
    if mode == "tn":
        K, M = a.shape
    else:
        M, K = a.shape
    N = b.shape[0] if mode == "nt" else b.shape[1]
    tm, tn, tk = min(tm, M), min(tn, N), min(tk, K)
    assert M % tm == 0 and N % tn == 0 and K % tk == 0
    nk = K // tk
    dims = {"nn": NN, "nt": NT, "tn": TN}[mode]

    def body(*refs):
        if add is None:
            a_ref, b_ref, o_ref, acc = refs
            add_ref = None
        else:
            a_ref, b_ref, add_ref, o_ref, acc = refs
        k = pl.program_id(2)
        p = _dot(a_ref[...], b_ref[...], dims)

        def finish(r):
            if add_ref is not None:
                r = r + add_ref[...].astype(F32)
            o_ref[...] = r.astype(out_dtype)

        if nk == 1:
            finish(p)
        else:
            @pl.when(k == 0)
            def _():
                acc[...] = p

            @pl.when(k > 0)
            def _():
                acc[...] += p

            @pl.when(k == nk - 1)
            def _():
                finish(acc[...])

    if mode == "tn":
        a_spec = pl.BlockSpec((tk, tm), lambda i, j, k: (k, i))
    else:
        a_spec = pl.BlockSpec((tm, tk), lambda i, j, k: (i, k))
    if mode == "nt":
        b_spec = pl.BlockSpec((tn, tk), lambda i, j, k: (j, k))
    else:
        b_spec = pl.BlockSpec((tk, tn), lambda i, j, k: (k, j))
    in_specs = [a_spec, b_spec]
    args = [a, b]
    if add is not None:
        in_specs.append(pl.BlockSpec((tm, tn), lambda i, j, k: (i, j)))
        args.append(add)
    return _pcall(
        body, name=name, grid=(M // tm, N // tn, nk), in_specs=in_specs,
        out_specs=[pl.BlockSpec((tm, tn), lambda i, j, k: (i, j))],
        out_shape=[jax.ShapeDtypeStruct((M, N), out_dtype)], args=args,
        scratch=[pltpu.VMEM((tm, tn) if nk > 1 else (8, LANES), F32)],
        sem=("parallel", "parallel", "arbitrary"), jobs=jobs)


def _rmsnorm_fwd(x, w, tm, name):
    T, D = x.shape

    def body(x_ref, w_ref, h_ref):
        xv = x_ref[...]
        r = lax.rsqrt(jnp.mean(xv * xv, axis=-1, keepdims=True) + RMS_EPS)
        h_ref[...] = (xv * r * w_ref[...]).astype(BF16)

    return _pcall(
        body, name=name, grid=(T // tm,),
        in_specs=[pl.BlockSpec((tm, D), lambda i: (i, 0)), pl.BlockSpec((1, D), lambda i: (0, 0))],
        out_specs=[pl.BlockSpec((tm, D), lambda i: (i, 0))],
        out_shape=[jax.ShapeDtypeStruct((T, D), BF16)], args=[x, w], sem=("parallel",))[0]


def _rmsnorm_bwd_add(dres, dh, x, w, tm, name, want_bf16, jobs=()):
    T, D = x.shape

    def body(dres_ref, dh_ref, x_ref, w_ref, *outs):
        if want_bf16:
            dx_ref, dxb_ref, dw_ref = outs
        else:
            dx_ref, dw_ref = outs
        i = pl.program_id(0)
        xv = x_ref[...]
        r = lax.rsqrt(jnp.mean(xv * xv, axis=-1, keepdims=True) + RMS_EPS)
        xh = xv * r
        dh_v = dh_ref[...].astype(F32)
        dxh = dh_v * w_ref[...]
        dx = dres_ref[...] + r * (dxh - xh * jnp.mean(dxh * xh, axis=-1, keepdims=True))
        dx_ref[...] = dx
        if want_bf16:
            dxb_ref[...] = dx.astype(BF16)
        part = jnp.sum(dh_v * xh, axis=0, keepdims=True)

        @pl.when(i == 0)
        def _():
            dw_ref[...] = part

        @pl.when(i > 0)
        def _():
            dw_ref[...] += part

    tile = pl.BlockSpec((tm, D), lambda i: (i, 0))
    row = pl.BlockSpec((1, D), lambda i: (0, 0))
    out_specs = [tile] + ([tile] if want_bf16 else []) + [row]
    out_shape = ([jax.ShapeDtypeStruct((T, D), F32)] + ([jax.ShapeDtypeStruct((T, D), BF16)] if want_bf16 else [])
                 + [jax.ShapeDtypeStruct((1, D), F32)])
    return _pcall(body, name=name, grid=(T // tm,), in_specs=[tile, tile, tile, row], out_specs=out_specs,
                  out_shape=out_shape, args=[dres, dh, x, w], sem=("arbitrary",), jobs=jobs)


def _loss_head(x2, target, wf, tm, name):
    T, D = x2.shape

    def body(x_ref, t_ref, w_ref, dx_ref, dxb_ref, loss_ref, dw_ref):
        i = pl.program_id(0)
        xv = x_ref[...]
        r = lax.rsqrt(jnp.mean(xv * xv, axis=-1, keepdims=True) + RMS_EPS)
        xh = xv * r
        wv = w_ref[...]
        e = xh * wv - t_ref[...]
        lpart = 0.5 * jnp.sum(jnp.sum(e * e, axis=-1, keepdims=True), axis=0, keepdims=True) * (1.0 / D)
        dy = e * (1.0 / D)
        dxh = dy * wv
        dx = r * (dxh - xh * jnp.mean(dxh * xh, axis=-1, keepdims=True))
        dx_ref[...] = dx
        dxb_ref[...] = dx.astype(BF16)
        wpart = jnp.sum(dy * xh, axis=0, keepdims=True)
        lfull = jnp.broadcast_to(lpart, (8, LANES))

        @pl.when(i == 0)
        def _():
            loss_ref[...] = lfull
            dw_ref[...] = wpart

        @pl.when(i > 0)
        def _():
            loss_ref[...] += lfull
            dw_ref[...] += wpart

    tile = pl.BlockSpec((tm, D), lambda i: (i, 0))
    row = pl.BlockSpec((1, D), lambda i: (0, 0))
    return _pcall(
        body, name=name, grid=(T // tm,), in_specs=[tile, tile, row],
        out_specs=[tile, tile, pl.BlockSpec((8, LANES), lambda i: (0, 0)), row],
        out_shape=[jax.ShapeDtypeStruct((T, D), F32), jax.ShapeDtypeStruct((T, D), BF16),
                   jax.ShapeDtypeStruct((8, LANES), F32), jax.ShapeDtypeStruct((1, D), F32)],
        args=[x2, target, wf], sem=("arbitrary",))


def _rope_tables(pos_col, inv2, tm, name):
    T = pos_col.shape[0]

    def body(p_ref, f_ref, c_ref, s_ref):
        ang = p_ref[...] * f_ref[...]
        lane = lax.broadcasted_iota(jnp.int32, ang.shape, 1)
        c_ref[...] = jnp.cos(ang)
        s_ref[...] = jnp.where(lane < RET_DK // 2, -1.0, 1.0) * jnp.sin(ang)

    tile = pl.BlockSpec((tm, RET_DK), lambda i: (i, 0))
    return _pcall(
        body, name=name, grid=(T // tm,),
        in_specs=[pl.BlockSpec((tm, 1), lambda i: (i, 0)), pl.BlockSpec((1, RET_DK), lambda i: (0, 0))],
        out_specs=[tile, tile], out_shape=[jax.ShapeDtypeStruct((T, RET_DK), F32)] * 2,
        args=[pos_col, inv2], sem=("parallel",))


def _mix_fwd(a_in, b_in, proj, x, w_ro, w_lo, w_out, mb, w2, tm, name):
    T, D = x.shape

    def body(a_ref, b_ref, gr_ref, gl_ref, x_ref, wro_ref, wlo_ref, wout_ref, mb_ref, w2_ref,
             x1_ref, mix_ref, h2_ref):
        ya = _dot(a_ref[...], wro_ref[...], NN)
        yb = _dot(b_ref[...], wlo_ref[...], NN)
        sa = _sigmoid(gr_ref[...].astype(F32) + mb_ref[0:1, :])
        sb = _sigmoid(gl_ref[...].astype(F32) + mb_ref[1:2, :])
        mix = (sa * ya + sb * yb).astype(BF16)
        mix_ref[...] = mix
        x1 = x_ref[...] + _dot(mix, wout_ref[...], NN)
        x1_ref[...] = x1
        r = lax.rsqrt(jnp.mean(x1 * x1, axis=-1, keepdims=True) + RMS_EPS)
        h2_ref[...] = (x1 * r * w2_ref[...]).astype(BF16)

    tile = pl.BlockSpec((tm, D), lambda i: (i, 0))
    wspec = pl.BlockSpec((D, D), lambda i: (0, 0))
    return _pcall(
        body, name=name, grid=(T // tm,),
        in_specs=[tile, tile,
                  pl.BlockSpec((tm, D), lambda i: (i, COL_GR)), pl.BlockSpec((tm, D), lambda i: (i, COL_GL)),
                  tile, wspec, wspec, wspec,
                  pl.BlockSpec((2, D), lambda i: (0, 0)), pl.BlockSpec((1, D), lambda i: (0, 0))],
        out_specs=[tile, tile, tile],
        out_shape=[jax.ShapeDtypeStruct((T, D), F32), jax.ShapeDtypeStruct((T, D), BF16),
                   jax.ShapeDtypeStruct((T, D), BF16)],
        args=[a_in, b_in, proj, proj, x, w_ro, w_lo, w_out, mb, w2], sem=("parallel",))


def _write_pieces(dst_ref, sems, stashes, row0, col0s, ids, grid, compute):
    def aligned(v, m):
        return v if isinstance(v, int) else pl.multiple_of(v, m)

    def copies(slot):
        return [pltpu.make_async_copy(
                    st.at[slot],
                    dst_ref.at[pl.ds(aligned(row0, 16), st.shape[1]), pl.ds(aligned(c0, LANES), st.shape[2])],
                    sems.at[slot, k])
                for k, (st, c0) in enumerate(zip(stashes, col0s))]

    step = ids[0]
    for i, g in zip(ids[1:], grid[1:]):
        step = step * g + i
    slot = step % 2
    last = _all_true([i == g - 1 for i, g in zip(ids, grid)])
    compute(slot)

    @pl.when(step > 0)
    def _():
        for cp in copies(1 - slot):
            cp.wait()

    for cp in copies(slot):
        cp.start()

    @pl.when(last)
    def _():
        for cp in copies(slot):
            cp.wait()


def _mix_bwd(dx1b, a_in, b_in, proj, w_ro, w_lo, w_out, mb, tm, name, jobs=()):
    T, D = a_in.shape
    grid = (T // tm,)

    def body(dx_ref, a_ref, b_ref, gr_ref, gl_ref, wro_ref, wlo_ref, wout_ref, mb_ref,
             da_ref, db_ref, dya_ref, dyb_ref, dp_ref, dmb_ref, dgr_s, dgl_s, wsem):
        i = pl.program_id(0)

        def compute(slot):
            dmix = _dot(dx_ref[...], wout_ref[...], NT)
            ya = _dot(a_ref[...], wro_ref[...], NN)
            yb = _dot(b_ref[...], wlo_ref[...], NN)
            sa = _sigmoid(gr_ref[...].astype(F32) + mb_ref[0:1, :])
            sb = _sigmoid(gl_ref[...].astype(F32) + mb_ref[1:2, :])
            dya = (dmix * sa).astype(BF16)
            dyb = (dmix * sb).astype(BF16)
            dgr = dmix * ya * sa * (1.0 - sa)
            dgl = dmix * yb * sb * (1.0 - sb)
            dya_ref[...] = dya
            dyb_ref[...] = dyb
            dgr_s[slot] = dgr.astype(BF16)
            dgl_s[slot] = dgl.astype(BF16)
            da_ref[...] = _dot(dya, wro_ref[...], NT).astype(BF16)
            db_ref[...] = _dot(dyb, wlo_ref[...], NT).astype(BF16)

            @pl.when(i == 0)
            def _():
                dmb_ref[...] = jnp.zeros_like(dmb_ref)

            dmb_ref[0:1, :] += jnp.sum(dgr, axis=0, keepdims=True)
            dmb_ref[1:2, :] += jnp.sum(dgl, axis=0, keepdims=True)

        _write_pieces(dp_ref, wsem, [dgr_s, dgl_s], i * tm, [COL_GR * D, COL_GL * D], [i], grid, compute)

    tile = pl.BlockSpec((tm, D), lambda i: (i, 0))
    wspec = pl.BlockSpec((D, D), lambda i: (0, 0))
    two = pl.BlockSpec((2, D), lambda i: (0, 0))
    return _pcall(
        body, name=name, grid=grid,
        in_specs=[tile, tile, tile,
                  pl.BlockSpec((tm, D), lambda i: (i, COL_GR)), pl.BlockSpec((tm, D), lambda i: (i, COL_GL)),
                  wspec, wspec, wspec, two],
        out_specs=[tile] * 4 + [HBM_SPEC, two],
        out_shape=[jax.ShapeDtypeStruct((T, D), BF16)] * 4
                  + [jax.ShapeDtypeStruct((T, D_IN), BF16), jax.ShapeDtypeStruct((2, D), F32)],
        args=[dx1b, a_in, b_in, proj, proj, w_ro, w_lo, w_out, mb],
        scratch=[pltpu.VMEM((2, tm, D), BF16), pltpu.VMEM((2, tm, D), BF16), pltpu.SemaphoreType.DMA((2, 2))],
        sem=("arbitrary",), jobs=jobs)


def _ret_decay_consts(lg):
    L = RET_BLOCK
    n = lax.broadcasted_iota(jnp.int32, (L, L), 0)
    m = lax.broadcasted_iota(jnp.int32, (L, L), 1)
    cn, cm = n // CHUNK, m // CHUNK
    expo = jnp.where(cn == cm, jnp.abs(n - m), n - m).astype(F32)
    wm = jnp.where(cm <= cn, jnp.exp(lg * expo), 0.0)
    idx = lax.broadcasted_iota(jnp.int32, (L, 1), 0).astype(F32)
    qd = jnp.exp(lg * (idx + 1.0))
    kd = jnp.exp(lg * (L - 1.0 - idx))
    bd = jnp.exp(lg * float(L))
    return wm, qd, kd, bd


def _rotate(v, cos2, sin2s):
    return v * cos2 + pltpu.roll(v, RET_DK // 2, 1) * sin2s


def _rotate_t(d, cos2, sin2s):
    return d * cos2 - pltpu.roll(d, RET_DK // 2, 1) * sin2s


def _retention_fwd(proj, cos2, sin2s, lgam, gn_w, B, S, name, jobs=()):
    T = B * S
    nb = S // RET_BLOCK
    scale = RET_DK ** -0.5

    def body(q_ref, k_ref, v_ref, g_ref, c_ref, s_ref, lg_ref, gw_ref, o_ref, a_ref, qr, kr, st):
        wm, qd, kd, bd = _ret_decay_consts(lg_ref[0:1, 0:1])
        cos2, sin2s = c_ref[...], s_ref[...]
        qr[...] = _rotate(q_ref[...].astype(F32), cos2, sin2s)
        kr[...] = _rotate(k_ref[...].astype(F32), cos2, sin2s) * scale
        st[...] = jnp.zeros_like(st)
        gw = gw_ref[...]
        for j in range(nb):
            rows = pl.ds(j * RET_BLOCK, RET_BLOCK)
            qb = qr[rows, :]
            kb = kr[rows, :]
            vb = v_ref[rows, :].astype(BF16)
            sc = _dot(qb.astype(BF16), kb.astype(BF16), NT) * wm
            o = _dot(sc.astype(BF16), vb, NN) + _dot((qb * qd).astype(BF16), st[...].astype(BF16), NN)
            st[...] = st[...] * bd + _dot((kb * kd).astype(BF16), vb, TN)
            o_ref[rows, :] = o
            mu = jnp.mean(o, axis=-1, keepdims=True)
            oc = o - mu
            var = jnp.mean(oc * oc, axis=-1, keepdims=True)
            y = oc * lax.rsqrt(var + GN_EPS) * gw
            g = g_ref[rows, :].astype(F32)
            a_ref[rows, :] = (y * (g * _sigmoid(g))).astype(BF16)

    blk = lambda w, off: pl.BlockSpec((S, w), lambda b, h: (b, off + h))
    return _pcall(
        body, name=name, grid=(B, RET_HEADS),
        in_specs=[blk(RET_DK, COL_Q), blk(RET_DK, COL_K), blk(RET_DV, COL_V), blk(RET_DV, COL_G),
                  pl.BlockSpec((S, RET_DK), lambda b, h: (b, 0)), pl.BlockSpec((S, RET_DK), lambda b, h: (b, 0)),
                  pl.BlockSpec((None, 8, LANES), lambda b, h: (h, 0, 0)),
                  pl.BlockSpec((1, RET_DV), lambda b, h: (0, h))],
        out_specs=[blk(RET_DV, 0), blk(RET_DV, 0)],
        out_shape=[jax.ShapeDtypeStruct((T, RET_HEADS * RET_DV), F32),
                   jax.ShapeDtypeStruct((T, RET_HEADS * RET_DV), BF16)],
        args=[proj, proj, proj, proj, cos2, sin2s, lgam, gn_w],
        scratch=[pltpu.VMEM((S, RET_DK), F32), pltpu.VMEM((S, RET_DK), F32), pltpu.VMEM((RET_DK, RET_DV), F32)],
        sem=("parallel", "parallel"), jobs=jobs)


def _retention_bwd(da_in, o, proj, dproj, cos2, sin2s, lgam, gn_w, B, S, name, jobs=()):
    T = B * S
    nb = S // RET_BLOCK
    scale = RET_DK ** -0.5
    grid = (RET_HEADS, B)

    def body(da_ref, o_ref, q_ref, k_ref, v_ref, g_ref, c_ref, s_ref, lg_ref, gw_ref, _, dp_ref, dgw_ref,
             qr, kr, do_s, sts, rst, dq_s, dk_s, dv_s, dg_s, wsem):
        h, b = pl.program_id(0), pl.program_id(1)

        def compute(slot):
            wm, qd, kd, bd = _ret_decay_consts(lg_ref[0:1, 0:1])
            cos2, sin2s = c_ref[...], s_ref[...]
            qr[...] = _rotate(q_ref[...].astype(F32), cos2, sin2s)
            kr[...] = _rotate(k_ref[...].astype(F32), cos2, sin2s) * scale
            gw = gw_ref[...]
            st = jnp.zeros((RET_DK, RET_DV), F32)
            dgw = jnp.zeros((1, RET_DV), F32)
            for j in range(nb):
                rows = pl.ds(j * RET_BLOCK, RET_BLOCK)
                ov = o_ref[rows, :]
                mu = jnp.mean(ov, axis=-1, keepdims=True)
                oc = ov - mu
                rstd = lax.rsqrt(jnp.mean(oc * oc, axis=-1, keepdims=True) + GN_EPS)
                y = oc * rstd
                g = g_ref[rows, :].astype(F32)
                sg = _sigmoid(g)
                da = da_ref[rows, :].astype(F32)
                dg_s[slot, rows, :] = (da * (y * gw) * (sg * (1.0 + g * (1.0 - sg)))).astype(BF16)
                dyw = da * (g * sg)
                dgw = dgw + jnp.sum(dyw * y, axis=0, keepdims=True)
                dy = dyw * gw
                do_s[rows, :] = rstd * (dy - jnp.mean(dy, axis=-1, keepdims=True)
                                        - y * jnp.mean(dy * y, axis=-1, keepdims=True))
                sts[j] = st
                st = st * bd + _dot((kr[rows, :] * kd).astype(BF16), v_ref[rows, :].astype(BF16), TN)

            @pl.when(b == 0)
            def _():
                dgw_ref[...] = dgw

            @pl.when(b > 0)
            def _():
                dgw_ref[...] += dgw

            rst[...] = jnp.zeros_like(rst)
            for j in reversed(range(nb)):
                rows = pl.ds(j * RET_BLOCK, RET_BLOCK)
                qb = qr[rows, :]
                kb = kr[rows, :]
                qbb, kbb = qb.astype(BF16), kb.astype(BF16)
                vb = v_ref[rows, :].astype(BF16)
                dob = do_s[rows, :]
                dobb = dob.astype(BF16)
                a_m = (_dot(qbb, kbb, NT) * wm).astype(BF16)
                b_m = (_dot(dobb, vb, NT) * wm).astype(BF16)
                rb = rst[...].astype(BF16)
                dq = _dot(b_m, kbb, NN) + _dot((dob * qd).astype(BF16), sts[j].astype(BF16), NT)
                dk = _dot(b_m, qbb, TN) + kd * _dot(vb, rb, NT)
                dv = _dot(a_m, dobb, TN) + kd * _dot(kbb, rb, NN)
                rst[...] = rst[...] * bd + _dot((qb * qd).astype(BF16), dobb, TN)
                cb, sb = c_ref[rows, :], s_ref[rows, :]
                dq_s[slot, rows, :] = _rotate_t(dq, cb, sb).astype(BF16)
                dk_s[slot, rows, :] = _rotate_t(dk * scale, cb, sb).astype(BF16)
                dv_s[slot, rows, :] = dv.astype(BF16)

        cols = [(COL_Q + h) * RET_DK, (COL_K + h) * RET_DK, (COL_V + h) * RET_DV, (COL_G + h) * RET_DV]
        _write_pieces(dp_ref, wsem, [dq_s, dk_s, dv_s, dg_s], b * S, cols, [h, b], grid, compute)

    blk = lambda w, off: pl.BlockSpec((S, w), lambda h, b: (b, off + h))
    return _pcall(
        body, name=name, grid=grid,
        in_specs=[blk(RET_DV, 0), blk(RET_DV, 0),
                  blk(RET_DK, COL_Q), blk(RET_DK, COL_K), blk(RET_DV, COL_V), blk(RET_DV, COL_G),
                  pl.BlockSpec((S, RET_DK), lambda h, b: (b, 0)), pl.BlockSpec((S, RET_DK), lambda h, b: (b, 0)),
                  pl.BlockSpec((None, 8, LANES), lambda h, b: (h, 0, 0)),
                  pl.BlockSpec((1, RET_DV), lambda h, b: (0, h)), HBM_SPEC],
        out_specs=[HBM_SPEC, pl.BlockSpec((1, RET_DV), lambda h, b: (0, h))],
        out_shape=[jax.ShapeDtypeStruct(dproj.shape, dproj.dtype),
                   jax.ShapeDtypeStruct((1, RET_HEADS * RET_DV), F32)],
        args=[da_in, o, proj, proj, proj, proj, cos2, sin2s, lgam, gn_w, dproj],
        scratch=[pltpu.VMEM((S, RET_DK), F32), pltpu.VMEM((S, RET_DK), F32),
                 pltpu.VMEM((S, RET_DV), F32), pltpu.VMEM((nb, RET_DK, RET_DV), F32),
                 pltpu.VMEM((RET_DK, RET_DV), F32),
                 pltpu.VMEM((2, S, RET_DK), BF16), pltpu.VMEM((2, S, RET_DK), BF16),
                 pltpu.VMEM((2, S, RET_DV), BF16), pltpu.VMEM((2, S, RET_DV), BF16), pltpu.SemaphoreType.DMA((2, 4))],
        sem=("arbitrary", "arbitrary"), jobs=jobs, alias_in_out={10: 0})


def _lru_gates(x, cw, cb, wr, wi, br, bi, lam):
    xc = cb + cw[LRU_CONV - 1:LRU_CONV, :] * x
    for j in range(LRU_CONV - 1):
        xc = xc + cw[j:j + 1, :] * _shift_down(x, LRU_CONV - 1 - j, 0.0)
    xcb = xc.astype(BF16)
    r = _sigmoid(_dot(xcb, wr, NN) + br)
    ig = _sigmoid(_dot(xcb, wi, NN) + bi)
    z = -lam
    sp = jnp.maximum(z, 0.0) + jnp.log1p(jnp.exp(-jnp.abs(z)))
    log_a = (-LRU_C) * r * sp
    a = jnp.exp(log_a)
    z2 = 2.0 * log_a
    taylor = -z2 * (1.0 + z2 * (0.5 + z2 * (1.0 / 6.0 + z2 * (1.0 / 24.0 + z2 * (1.0 / 120.0)))))
    om = jnp.where(z2 > -0.05, taylor, 1.0 - jnp.exp(z2))
    sq = jnp.sqrt(om)
    return xc, xcb, r, ig, sp, a, sq


def _lru_fwd(proj, cw, cb, wr, wi, br, bi, lam, B, S, name, jobs=()):
    T = B * S
    W = LRU_BLOCKS * LRU_BLOCK

    def body(x_ref, y_ref, cw_ref, cb_ref, wr_ref, wi_ref, br_ref, bi_ref, lam_ref, h_ref, bin_ref, a_s, b_s):
        xc, _, _, ig, _, a, sq = _lru_gates(x_ref[...].astype(F32), cw_ref[...], cb_ref[...], wr_ref[...], wi_ref[...],
                                           br_ref[...], bi_ref[...], lam_ref[...])
        a_s[...] = a
        b_s[...] = sq * ig * xc
        _scan_forward_ref(a_s, b_s, h_ref)
        bin_ref[...] = (h_ref[...] * _gelu(y_ref[...].astype(F32))).astype(BF16)

    blk = lambda off: pl.BlockSpec((S, LRU_BLOCK), lambda b, n: (b, off + n))
    vec = lambda rows: pl.BlockSpec((rows, LRU_BLOCK), lambda b, n: (0, n))
    wspec = pl.BlockSpec((None, LRU_BLOCK, LRU_BLOCK), lambda b, n: (n, 0, 0))
    return _pcall(
        body, name=name, grid=(B, LRU_BLOCKS),
        in_specs=[blk(COL_XL), blk(COL_YL), vec(LRU_CONV), vec(1), wspec, wspec, vec(1), vec(1), vec(1)],
        out_specs=[blk(0), blk(0)],
        out_shape=[jax.ShapeDtypeStruct((T, W), F32), jax.ShapeDtypeStruct((T, W), BF16)],
        args=[proj, proj, cw, cb, wr, wi, br, bi, lam],
        scratch=[pltpu.VMEM((S, LRU_BLOCK), F32), pltpu.VMEM((S, LRU_BLOCK), F32)], sem=("parallel", "parallel"),
        jobs=jobs)


def _lru_bwd(db_in, h, proj, dproj, cw, cb, wr, wi, br, bi, lam, B, S, name, jobs=()):
    T = B * S
    W = LRU_BLOCKS * LRU_BLOCK

    grid = (LRU_BLOCKS, B)

    def body(dbin_ref, h_ref, x_ref, y_ref, cw_ref, cb_ref, wr_ref, wi_ref, br_ref, bi_ref, lam_ref, _,
             dp_ref, dcw_ref, dcb_ref, dwr_ref, dwi_ref, dbr_ref, dbi_ref, dlam_ref, dx_s, dy_s, wsem,
             an_s, u_s, dh_s):
        n, b = pl.program_id(0), pl.program_id(1)

        def compute(slot):
            x = x_ref[...].astype(F32)
            cw = cw_ref[...]
            wr, wi = wr_ref[...], wi_ref[...]
            lam = lam_ref[...]
            xc, xcb, r, ig, sp, a, sq = _lru_gates(x, cw, cb_ref[...], wr, wi, br_ref[...], bi_ref[...], lam)
            hv = h_ref[...]
            gel, dgel = _gelu_and_grad(y_ref[...].astype(F32))
            dbin = dbin_ref[...].astype(F32)
            dy_s[slot] = (dbin * hv * dgel).astype(BF16)
            an_s[...] = _shift_up(a, 1, 0.0)
            u_s[...] = dbin * gel
            _scan_backward_ref(an_s, u_s, dh_s)
            dh = dh_s[...]
            hprev = _shift_down(hv, 1, 0.0)
            d_ig = dh * sq * xc
            d_xc = dh * sq * ig
            a2 = a * a
            d_loga = dh * hprev * a - dh * ig * xc * a2 / sq
            d_r = d_loga * ((-LRU_C) * sp)
            d_sp = jnp.sum(d_loga * ((-LRU_C) * r), axis=0, keepdims=True)
            dlam = -d_sp * _sigmoid(-lam)
            d_pr = d_r * r * (1.0 - r)
            d_pi = d_ig * ig * (1.0 - ig)
            d_prb, d_pib = d_pr.astype(BF16), d_pi.astype(BF16)
            d_xc = d_xc + _dot(d_prb, wr, NT) + _dot(d_pib, wi, NT)

            @pl.when(b == 0)
            def _():
                for ref in (dcw_ref, dcb_ref, dwr_ref, dwi_ref, dbr_ref, dbi_ref, dlam_ref):
                    ref[...] = jnp.zeros_like(ref)

            dx = cw[LRU_CONV - 1:LRU_CONV, :] * d_xc
            for j in range(LRU_CONV - 1):
                sft = LRU_CONV - 1 - j
                dx = dx + cw[j:j + 1, :] * _shift_up(d_xc, sft, 0.0)
                dcw_ref[j:j + 1, :] += jnp.sum(d_xc * _shift_down(x, sft, 0.0), axis=0, keepdims=True)
            dcw_ref[LRU_CONV - 1:LRU_CONV, :] += jnp.sum(d_xc * x, axis=0, keepdims=True)
            dx_s[slot] = dx.astype(BF16)
            dcb_ref[...] += jnp.sum(d_xc, axis=0, keepdims=True)
            dwr_ref[...] += _dot(xcb, d_prb, TN)
            dwi_ref[...] += _dot(xcb, d_pib, TN)
            dbr_ref[...] += jnp.sum(d_pr, axis=0, keepdims=True)
            dbi_ref[...] += jnp.sum(d_pi, axis=0, keepdims=True)
            dlam_ref[...] += dlam

        cols = [(COL_XL + n) * LRU_BLOCK, (COL_YL + n) * LRU_BLOCK]
        _write_pieces(dp_ref, wsem, [dx_s, dy_s], b * S, cols, [n, b], grid, compute)

    blk = lambda off: pl.BlockSpec((S, LRU_BLOCK), lambda n, b: (b, off + n))
    vec = lambda rows: pl.BlockSpec((rows, LRU_BLOCK), lambda n, b: (0, n))
    wspec = pl.BlockSpec((None, LRU_BLOCK, LRU_BLOCK), lambda n, b: (n, 0, 0))
    vshape = lambda rows: jax.ShapeDtypeStruct((rows, W), F32)
    wshape = jax.ShapeDtypeStruct((LRU_BLOCKS, LRU_BLOCK, LRU_BLOCK), F32)
    return _pcall(
        body, name=name, grid=grid,
        in_specs=[blk(0), blk(0), blk(COL_XL), blk(COL_YL), vec(LRU_CONV), vec(1), wspec, wspec, vec(1), vec(1),
                  vec(1), HBM_SPEC],
        out_specs=[HBM_SPEC, vec(LRU_CONV), vec(1), wspec, wspec, vec(1), vec(1), vec(1)],
        out_shape=[jax.ShapeDtypeStruct(dproj.shape, dproj.dtype),
                   vshape(LRU_CONV), vshape(1), wshape, wshape, vshape(1), vshape(1), vshape(1)],
        args=[db_in, h, proj, proj, cw, cb, wr, wi, br, bi, lam, dproj],
        scratch=[pltpu.VMEM((2, S, LRU_BLOCK), BF16), pltpu.VMEM((2, S, LRU_BLOCK), BF16), pltpu.SemaphoreType.DMA((2, 2)),
                 pltpu.VMEM((S, LRU_BLOCK), F32), pltpu.VMEM((S, LRU_BLOCK), F32), pltpu.VMEM((S, LRU_BLOCK), F32)],
        sem=("arbitrary", "arbitrary"), jobs=jobs, alias_in_out={11: 0})


FFN_CT = 256


def _ffn_conv(gate, cw, cb):
    gc = cb + cw[FFN_CONV - 1:FFN_CONV, :] * gate
    for j in range(FFN_CONV - 1):
        gc = gc + cw[j:j + 1, :] * _shift_down(gate, FFN_CONV - 1 - j, 0.0)
    return gc


def _ffn_act_fwd(up, cw, cb, B, S, name):
    T = B * S
    nct = D_FF // FFN_CT

    def body(g_ref, v_ref, cw_ref, cb_ref, f_ref):
        gc = _ffn_conv(g_ref[...].astype(F32), cw_ref[...], cb_ref[...])
        f_ref[...] = (_gelu(gc) * v_ref[...].astype(F32)).astype(BF16)

    return _pcall(
        body, name=name, grid=(B, nct),
        in_specs=[pl.BlockSpec((S, FFN_CT), lambda b, c: (b, c)), pl.BlockSpec((S, FFN_CT), lambda b, c: (b, nct + c)),
                  pl.BlockSpec((FFN_CONV, FFN_CT), lambda b, c: (0, c)), pl.BlockSpec((1, FFN_CT), lambda b, c: (0, c))],
        out_specs=[pl.BlockSpec((S, FFN_CT), lambda b, c: (b, c))],
        out_shape=[jax.ShapeDtypeStruct((T, D_FF), BF16)], args=[up, up, cw, cb], sem=("parallel", "parallel"))[0]


def _ffn_act_bwd(df, up, cw, cb, B, S, name, jobs=()):
    T = B * S
    nct = D_FF // FFN_CT

    grid = (nct, B)

    def body(df_ref, g_ref, v_ref, cw_ref, cb_ref, du_ref, dcw_ref, dcb_ref, dg_s, dv_s, wsem):
        c, b = pl.program_id(0), pl.program_id(1)

        def compute(slot):
            gate = g_ref[...].astype(F32)
            cw = cw_ref[...]
            gc = _ffn_conv(gate, cw, cb_ref[...])
            gel, dgel = _gelu_and_grad(gc)
            dfv = df_ref[...].astype(F32)
            dv_s[slot] = (dfv * gel).astype(BF16)
            dgc = dfv * v_ref[...].astype(F32) * dgel

            @pl.when(b == 0)
            def _():
                dcw_ref[...] = jnp.zeros_like(dcw_ref)
                dcb_ref[...] = jnp.zeros_like(dcb_ref)

            dgate = cw[FFN_CONV - 1:FFN_CONV, :] * dgc
            for j in range(FFN_CONV - 1):
                sft = FFN_CONV - 1 - j
                dgate = dgate + cw[j:j + 1, :] * _shift_up(dgc, sft, 0.0)
                dcw_ref[j:j + 1, :] += jnp.sum(dgc * _shift_down(gate, sft, 0.0), axis=0, keepdims=True)
            dcw_ref[FFN_CONV - 1:FFN_CONV, :] += jnp.sum(dgc * gate, axis=0, keepdims=True)
            dg_s[slot] = dgate.astype(BF16)
            dcb_ref[...] += jnp.sum(dgc, axis=0, keepdims=True)

        _write_pieces(du_ref, wsem, [dg_s, dv_s], b * S, [c * FFN_CT, (nct + c) * FFN_CT], [c, b], grid, compute)

    blk = pl.BlockSpec((S, FFN_CT), lambda c, b: (b, c))
    return _pcall(
        body, name=name, grid=grid,
        in_specs=[blk, blk, pl.BlockSpec((S, FFN_CT), lambda c, b: (b, nct + c)),
                  pl.BlockSpec((FFN_CONV, FFN_CT), lambda c, b: (0, c)),
                  pl.BlockSpec((1, FFN_CT), lambda c, b: (0, c))],
        out_specs=[HBM_SPEC, pl.BlockSpec((FFN_CONV, FFN_CT), lambda c, b: (0, c)),
                   pl.BlockSpec((1, FFN_CT), lambda c, b: (0, c))],
        out_shape=[jax.ShapeDtypeStruct((T, 2 * D_FF), BF16),
                   jax.ShapeDtypeStruct((FFN_CONV, D_FF), F32), jax.ShapeDtypeStruct((1, D_FF), F32)],
        args=[df, up, up, cw, cb],
        scratch=[pltpu.VMEM((2, S, FFN_CT), BF16), pltpu.VMEM((2, S, FFN_CT), BF16), pltpu.SemaphoreType.DMA((2, 2))],
        sem=("arbitrary", "arbitrary"), jobs=jobs)


def _rs_add(g, recv, mode, core, name):
    shard = tuple(recv.shape[1:])
    if mode == "mid":
        a, e, c2 = shard
        g_in = g.reshape(a, N_DEV, e, c2)
        grid = (4, 1)
        g_spec = pl.BlockSpec((a, None, e, c2), lambda k, i, c_ref: (0, 2 * k + c_ref[0], 0, 0))
        r_spec = pl.BlockSpec((None, a, e, c2), lambda k, i, c_ref: (k, 0, 0, 0))
    else:
        R, C = shard
        tr = _row_tile(R, 512)
        grid = (4, R // tr)
        if mode == "rows":
            g_in = g.reshape(N_DEV, R, C)
            g_spec = pl.BlockSpec((None, tr, C), lambda k, i, c_ref: (2 * k + c_ref[0], i, 0))
        else:
            g_in = g
            g_spec = pl.BlockSpec((tr, C), lambda k, i, c_ref: (i, 2 * k + c_ref[0]))
        r_spec = pl.BlockSpec((None, tr, C), lambda k, i, c_ref: (k, i, 0))

    def body(c_ref, g_ref, r_ref, o_ref):
        o_ref[...] = g_ref[...] + r_ref[...]

    return pl.pallas_call(
        body, name=name,
        grid_spec=pltpu.PrefetchScalarGridSpec(num_scalar_prefetch=1, grid=grid, in_specs=[g_spec, r_spec],
                                               out_specs=r_spec),
        out_shape=jax.ShapeDtypeStruct(recv.shape, recv.dtype),
        compiler_params=pltpu.CompilerParams(dimension_semantics=("parallel", "parallel"),
                                             vmem_limit_bytes=VMEM_LIMIT),
    )(core, g_in, recv)


def _adam_update(gv, w, m, v):
    nm = ADAM_B1 * m + (1.0 - ADAM_B1) * gv
    nv = ADAM_B2 * v + (1.0 - ADAM_B2) * (gv * gv)
    m_hat = nm / (1.0 - ADAM_B1 ** ADAM_STEP)
    v_hat = nv / (1.0 - ADAM_B2 ** ADAM_STEP)
    delta = -ADAM_LR * (m_hat / (jnp.sqrt(v_hat) + ADAM_EPS) + ADAM_WD * w)
    return delta, nm, nv


def _adamw_shard(partial, recv, w, m, v, chip, name):
    shape = tuple(w.shape)
    tr = _row_tile(shape[0], 256)
    rest = shape[1:]
    zeros = (0,) * len(rest)
    tile = pl.BlockSpec((tr,) + rest, lambda i, s: (i,) + zeros)

    def body(_, p_ref, r_ref, w_ref, m_ref, v_ref, g_ref, d_ref, nm_ref, nv_ref):
        gv = p_ref[...] + r_ref[0] + r_ref[1] + r_ref[2]
        g_ref[...] = gv
        d_ref[...], nm_ref[...], nv_ref[...] = _adam_update(gv, w_ref[...], m_ref[...], v_ref[...])

    grid_spec = pltpu.PrefetchScalarGridSpec(
        num_scalar_prefetch=1, grid=(shape[0] // tr,),
        in_specs=[pl.BlockSpec((None, tr) + rest, lambda i, s: (s[0], i) + zeros),
                  pl.BlockSpec((3, tr) + rest, lambda i, s: (0, i) + zeros), tile, tile, tile],
        out_specs=[tile] * 4)
    return pl.pallas_call(
        body, name=name, grid_spec=grid_spec, out_shape=[jax.ShapeDtypeStruct(shape, F32)] * 4,
        compiler_params=pltpu.CompilerParams(dimension_semantics=("parallel",), vmem_limit_bytes=VMEM_LIMIT),
    )(chip, partial, recv, w, m, v)


def _proj_fwd_gather(h1, shards, modes, chip_order, name):
    n = len(shards)
    extents = [_extent(s.shape, m) for s, m in zip(shards, modes)]
    T, K = h1.shape
    tn = 2 * shards[0].shape[1]
    tm = min(1024, T)
    n_i = T // tm

    def body(order_ref, a_ref, *refs):
        x_refs, o_ref, full_refs = refs[:n], refs[n], refs[n + 1:2 * n + 1]
        wtile, send, recv, local = refs[2 * n + 1:]
        j, i = pl.program_id(0), pl.program_id(1)
        x, y, c = _mesh_pos()
        me, sibling = (x, y, c), (x, y, 1 - c)
        chips = _other_chips(x, y)

        def slot(a, p):
            return _window(full_refs[a], modes[a], extents[a], 4 * p[0] + 2 * p[1] + p[2])

        def copy(a, k, block, to, src=None):
            return _remote(slot(a, block) if src is None else src, slot(a, block), send.at[a, k], recv.at[a, k], to)

        def own_sends(a):
            return [copy(a, 0, me, sibling, src=x_refs[a])] + [copy(a, 1 + r, me, (*chip, c), src=x_refs[a])
                                                                for r, chip in enumerate(chips)]

        def load_tile(chip):
            col = pl.multiple_of((2 * chip[0] + chip[1]) * tn, LANES)
            cp = pltpu.make_async_copy(full_refs[0].at[:, pl.ds(col, tn)], wtile, local.at[n])
            cp.start()
            cp.wait()

        @pl.when(jnp.logical_and(j == 0, i == 0))
        def _():
            mine = [pltpu.make_async_copy(x_refs[a], slot(a, me), local.at[a]) for a in range(n)]
            for a in range(n):
                mine[a].start()
                for cp in own_sends(a):
                    cp.start()
            for a in range(n):
                copy(a, 0, sibling, me).wait_recv()
                mine[a].wait()
            load_tile((x, y))

        for r, chip in enumerate(chips):
            @pl.when(jnp.logical_and(j == r + 1, i == 0))
            def _(r=r, chip=chip):
                for a in range(n):
                    copy(a, 1 + r, (*chip, c), me).wait_recv()
                    copy(a, 4 + r, (*chip, c), sibling).start()
                for a in range(n):
                    copy(a, 4 + r, (*chip, 1 - c), me).wait_recv()
                load_tile(chip)

        o_ref[...] = _dot(a_ref[...], wtile[...], NN).astype(BF16)

        @pl.when(jnp.logical_and(j == 3, i == n_i - 1))
        def _():
            for a in range(n):
                for cp in own_sends(a):
                    cp.wait_send()
                for r, chip in enumerate(chips):
                    copy(a, 4 + r, (*chip, c), sibling).wait_send()

    grid_spec = pltpu.PrefetchScalarGridSpec(
        num_scalar_prefetch=1, grid=(4, n_i),
        in_specs=[pl.BlockSpec((tm, K), lambda j, i, order: (i, 0))] + [HBM_SPEC] * n,
        out_specs=[pl.BlockSpec((tm, tn), lambda j, i, order: (i, order[j]))] + [HBM_SPEC] * n,
        scratch_shapes=[pltpu.VMEM((K, tn), BF16), pltpu.SemaphoreType.DMA((n, 7)), pltpu.SemaphoreType.DMA((n, 7)),
                        pltpu.SemaphoreType.DMA((n + 1,))])
    return pl.pallas_call(
        body, name=name, grid_spec=grid_spec,
        out_shape=[jax.ShapeDtypeStruct((T, 4 * tn), BF16)]
                  + [jax.ShapeDtypeStruct(_full_shape(s.shape, m), s.dtype) for s, m in zip(shards, modes)],
        compiler_params=pltpu.CompilerParams(dimension_semantics=("arbitrary", "arbitrary"),
                                             vmem_limit_bytes=VMEM_LIMIT),
    )(chip_order, h1, *shards)


SMALL_LANES = 1024


def _small_rows(shape):
    r, w = shape
    return r * max(1, w // SMALL_LANES)


def _small_allreduce(parts, name):
    n = len(parts)
    shapes = [tuple(p.shape) for p in parts]
    offs, total = [], 0
    for s in shapes:
        offs.append(total)
        total += _small_rows(s)
    rows = -(-total // 8) * 8

    def body(*refs):
        p_refs, o_refs = refs[:n], refs[n:2 * n]
        buf, tot, send_sems, recv_sems = refs[2 * n:]
        x, y, c = _mesh_pos()
        me, sibling = (x, y, c), (x, y, 1 - c)
        chips = _other_chips(x, y)

        def slot(px, py, pc):
            return buf.at[4 * px + 2 * py + pc]

        def copy(k, block, to):
            return _remote(slot(*block), slot(*block), send_sems.at[k], recv_sems.at[k], to)

        tot[...] = jnp.zeros_like(tot)
        for p_ref, (r, w), off in zip(p_refs, shapes, offs):
            wl = min(w, SMALL_LANES)
            for part in range(max(1, w // SMALL_LANES)):
                tot[pl.ds(off + part * r, r), pl.ds(0, wl)] = p_ref[:, pl.ds(part * SMALL_LANES, wl)]
        buf[4 * x + 2 * y + c] = tot[...]
        first = [copy(0, me, sibling)] + [copy(1 + j, me, (*chip, c)) for j, chip in enumerate(chips)]
        for cp in first:
            cp.start()
        passed = [copy(4 + j, (*chip, c), sibling) for j, chip in enumerate(chips)]
        for j, chip in enumerate(chips):
            copy(1 + j, (*chip, c), me).wait_recv()
            passed[j].start()
        copy(0, sibling, me).wait_recv()
        for j, chip in enumerate(chips):
            copy(4 + j, (*chip, 1 - c), me).wait_recv()
        for cp in first + passed:
            cp.wait_send()
        acc = buf[0]
        for d in range(1, N_DEV):
            acc = acc + buf[d]
        tot[...] = acc
        for o_ref, (r, w), off in zip(o_refs, shapes, offs):
            wl = min(w, SMALL_LANES)
            for part in range(max(1, w // SMALL_LANES)):
                o_ref[:, pl.ds(part * SMALL_LANES, wl)] = tot[pl.ds(off + part * r, r), pl.ds(0, wl)]

    vm = pl.BlockSpec(memory_space=pltpu.VMEM)
    return pl.pallas_call(
        body, name=name,
        in_specs=[vm] * n, out_specs=[vm] * n,
        out_shape=[jax.ShapeDtypeStruct(s, F32) for s in shapes],
        scratch_shapes=[pltpu.VMEM((N_DEV, rows, SMALL_LANES), F32), pltpu.VMEM((rows, SMALL_LANES), F32),
                        pltpu.SemaphoreType.DMA((7,)), pltpu.SemaphoreType.DMA((7,))],
    )(*parts)


def _adamw_small(gs, ws, ms, vs, name):
    n = len(gs)

    def body(*refs):
        g_r, w_r, m_r, v_r = refs[:n], refs[n:2 * n], refs[2 * n:3 * n], refs[3 * n:4 * n]
        d_r, nm_r, nv_r = refs[4 * n:5 * n], refs[5 * n:6 * n], refs[6 * n:7 * n]
        for i in range(n):
            d_r[i][...], nm_r[i][...], nv_r[i][...] = _adam_update(g_r[i][...], w_r[i][...], m_r[i][...], v_r[i][...])

    vm = pl.BlockSpec(memory_space=pltpu.VMEM)
    shapes = [jax.ShapeDtypeStruct(w.shape, F32) for w in ws]
    outs = pl.pallas_call(body, name=name, in_specs=[vm] * (4 * n), out_specs=[vm] * (3 * n),
                          out_shape=shapes * 3)(*gs, *ws, *ms, *vs)
    return outs[:n], outs[n:2 * n], outs[2 * n:]


FIRST = [("w_in", (1024, 896), "cols"), ("lru_w_r", (4, 32, 256), "mid"), ("lru_w_i", (4, 32, 256), "mid")]
LATE = [("w_ret_o", (128, 1024), "rows"), ("w_lru_o", (128, 1024), "rows"), ("w_out", (128, 1024), "rows"),
        ("ffn_w_up", (1024, 768), "cols"), ("ffn_w_down", (384, 1024), "rows")]
BIG = FIRST + LATE
SMALL_SHARDED = [("merge_gate_b", (2, 128), "cols"), ("lru_conv_w", (4, 128), "cols"), ("lru_b_r", (4, 32), "stack"),
                 ("lru_b_i", (4, 32), "stack"), ("ffn_conv_w", (3, 384), "cols")]
REPLICATED = [("norm1_w", (1, 1024)), ("ret_gn_w", (1, 1024)), ("lru_conv_b", (1, 1024)), ("lru_lambda", (1, 1024)),
              ("norm2_w", (1, 1024)), ("ffn_conv_b", (1, 3072)), ("norm_f_w", (1, 1024))]
MODE = {n: m for n, _, m in BIG}
SHARD = {n: s for n, s, _ in BIG}


def _local_step(x3, positions, target3, first_shards, ws, late_shards, core, chip_order):
    B, S, D = x3.shape
    T = B * S
    x = x3.reshape(T, D)
    target = target3.reshape(T, D)
    tm = min(512, T)
    big = min(1024, T)
    big2 = min(2048, T)

    half = RET_DK // 2
    inv_freq = ROPE_BASE ** (-jnp.arange(half, dtype=F32) / half)
    inv2 = jnp.concatenate([inv_freq, inv_freq]).reshape(1, RET_DK)
    log_gamma = jnp.log1p(-jnp.power(2.0, -5.0 - jnp.arange(RET_HEADS, dtype=F32)))
    lgam = jnp.broadcast_to(log_gamma[:, None, None], (RET_HEADS, 8, LANES))
    pos_col = positions.astype(F32).reshape(T, 1)
    cos2, sin2s = _rope_tables(pos_col, inv2, tm, "rope_tables")

    late_names = [n for n, _, _ in LATE]
    late_modes = [m for _, _, m in LATE]
    late_shapes = [s for _, s, _ in LATE]

    h1 = _rmsnorm_fwd(x, ws["norm1_w"], tm, "norm1_fwd")
    first_names = [n for n, _, _ in FIRST + SMALL_SHARDED]
    proj, *first_full = _proj_fwd_gather(h1, first_shards, [m for _, _, m in FIRST + SMALL_SHARDED], chip_order,
                                         "proj_fwd")
    gathered = dict(zip(first_names, first_full))
    wb = {n: gathered[n] for n, _, _ in FIRST}
    ws = dict(ws, **{n: gathered[n] for n, _, _ in SMALL_SHARDED})
    for n in ("lru_b_r", "lru_b_i"):
        ws[n] = jnp.transpose(ws[n], (1, 0, 2)).reshape(1, LRU_BLOCKS * LRU_BLOCK)
    hl, b_in, *late_part = _lru_fwd(proj, ws["lru_conv_w"], ws["lru_conv_b"], wb["lru_w_r"], wb["lru_w_i"],
                                    ws["lru_b_r"], ws["lru_b_i"], ws["lru_lambda"], B, S, "lru_fwd",
                                    jobs=[_ag_first_job(late_shards, late_modes)])
    o, a_in, *late_full = _retention_fwd(proj, cos2, sin2s, lgam, ws["ret_gn_w"], B, S, "retention_fwd",
                                         jobs=[_ag_second_job(late_part, late_modes, late_shapes)])
    wb = dict(wb, **dict(zip(late_names, late_full)))
    x1, mix, h2 = _mix_fwd(a_in, b_in, proj, x, wb["w_ret_o"], wb["w_lru_o"], wb["w_out"],
                           ws["merge_gate_b"], ws["norm2_w"], tm, "mix_fwd")
    up = _matmul(h2, wb["ffn_w_up"], "nn", BF16, big2, 1024, 1024, "ffn_up_fwd")[0]
    f = _ffn_act_fwd(up, ws["ffn_conv_w"], ws["ffn_conv_b"], B, S, "ffn_act_fwd")
    x2 = _matmul(f, wb["ffn_w_down"], "nn", F32, big, 1024, D_FF, "ffn_down_fwd", add=x1)[0]
    dx2, dx2b, loss_acc, d_norm_f = _loss_head(x2, target, ws["norm_f_w"], tm, "loss_head")

    g, rs = {}, {}

    def stage1(names, grads):
        return _rs_sibling_job(grads, [MODE[n] for n in names], [SHARD[n] for n in names])

    def add(names, grads, recvs):
        return [_rs_add(gr, r, MODE[n], core, "rs_add_" + n) for n, gr, r in zip(names, grads, recvs)]

    g["norm_f_w"] = d_norm_f
    g_down = _matmul(f, dx2b, "tn", F32, 1024, 1024, big2, "ffn_down_bwd_w")[0]
    df, s1_down = _matmul(dx2b, wb["ffn_w_down"], "nt", BF16, big2, 1024, 1024, "ffn_down_bwd_x",
                          jobs=[stage1(["ffn_w_down"], [g_down])])
    p_down = add(["ffn_w_down"], [g_down], [s1_down])
    dup, g["ffn_conv_w"], g["ffn_conv_b"], s2_down = _ffn_act_bwd(
        df, up, ws["ffn_conv_w"], ws["ffn_conv_b"], B, S, "ffn_act_bwd", jobs=[_rs_chip_job(p_down)])
    rs["ffn_w_down"] = (p_down[0], s2_down)

    g_up = _matmul(h2, dup, "tn", F32, 1024, 1024, big2, "ffn_up_bwd_w")[0]
    dh2, s1_up = _matmul(dup, wb["ffn_w_up"], "nt", BF16, big, 1024, D_FF, "ffn_up_bwd_x",
                         jobs=[stage1(["ffn_w_up"], [g_up])])
    p_up = add(["ffn_w_up"], [g_up], [s1_up])
    dx1, dx1b, g["norm2_w"] = _rmsnorm_bwd_add(dx2, dh2, x1, ws["norm2_w"], tm, "norm2_bwd", True)
    da_in, db_in, dya, dyb, dproj, g["merge_gate_b"] = _mix_bwd(
        dx1b, a_in, b_in, proj, wb["w_ret_o"], wb["w_lru_o"], wb["w_out"], ws["merge_gate_b"], tm, "mix_bwd")

    mid_names = ["w_out", "w_ret_o", "w_lru_o"]
    g_mid = [_matmul(mix, dx1b, "tn", F32, 1024, 1024, big2, "w_out_bwd_w")[0],
             _matmul(a_in, dya, "tn", F32, 1024, 1024, big2, "w_ret_o_bwd_w")[0],
             _matmul(b_in, dyb, "tn", F32, 1024, 1024, big2, "w_lru_o_bwd_w")[0]]
    (dproj, g["lru_conv_w"], g["lru_conv_b"], g_wr, g_wi, g["lru_b_r"], g["lru_b_i"], g["lru_lambda"], s2_up,
     *s1_mid) = _lru_bwd(db_in, hl, proj, dproj, ws["lru_conv_w"], ws["lru_conv_b"], wb["lru_w_r"], wb["lru_w_i"],
                         ws["lru_b_r"], ws["lru_b_i"], ws["lru_lambda"], B, S, "lru_bwd",
                         jobs=[_rs_chip_job(p_up), stage1(mid_names, g_mid)])
    rs["ffn_w_up"] = (p_up[0], s2_up)
    p_mid = add(mid_names, g_mid, s1_mid)
    lru_names = ["lru_w_r", "lru_w_i"]
    dproj, g["ret_gn_w"], *rest = _retention_bwd(
        da_in, o, proj, dproj, cos2, sin2s, lgam, ws["ret_gn_w"], B, S, "retention_bwd",
        jobs=[_rs_chip_job(p_mid), stage1(lru_names, [g_wr, g_wi])])
    s2_mid, s1_lru = rest[:3], rest[3:]
    for n, p, r in zip(mid_names, p_mid, s2_mid):
        rs[n] = (p, r)
    p_lru = add(lru_names, [g_wr, g_wi], s1_lru)

    g_in, *s2_lru = _matmul(h1, dproj, "tn", F32, 1024, 1024, big2, "proj_bwd_w", jobs=[_rs_chip_job(p_lru)])
    for n, p, r in zip(lru_names, p_lru, s2_lru):
        rs[n] = (p, r)
    s1_in = _pcall(lambda: None, name="rs_sibling_w_in", grid=(1,), in_specs=[], out_specs=[], out_shape=[], args=[],
                   sem=("arbitrary",), jobs=[stage1(["w_in"], [g_in])])
    p_in = add(["w_in"], [g_in], s1_in)
    dh1, s2_in = _matmul(dproj, wb["w_in"], "nt", BF16, big, 1024, D_IN // 2, "proj_bwd_x",
                         jobs=[_rs_chip_job(p_in)])
    grad_x, g["norm1_w"] = _rmsnorm_bwd_add(dx1, dh1, x, ws["norm1_w"], tm, "norm1_bwd", False)
    rs["w_in"] = (p_in[0], s2_in)
    return loss_acc, grad_x.reshape(B, S, D), g, rs


def kernel(x, positions, norm1_w, w_in, merge_gate_b, ret_gn_w, w_ret_o, lru_conv_w, lru_conv_b, lru_w_r, lru_b_r, lru_w_i, lru_b_i, lru_lambda, w_lru_o, w_out, norm2_w, ffn_w_up, ffn_conv_w, ffn_conv_b, ffn_w_down, norm_f_w, loss_target, m_norm1_w, m_w_in, m_merge_gate_b, m_ret_gn_w, m_w_ret_o, m_lru_conv_w, m_lru_conv_b, m_lru_w_r, m_lru_b_r, m_lru_w_i, m_lru_b_i, m_lru_lambda, m_w_lru_o, m_w_out, m_norm2_w, m_ffn_w_up, m_ffn_conv_w, m_ffn_conv_b, m_ffn_w_down, m_norm_f_w, v_norm1_w, v_w_in, v_merge_gate_b, v_ret_gn_w, v_w_ret_o, v_lru_conv_w, v_lru_conv_b, v_lru_w_r, v_lru_b_r, v_lru_w_i, v_lru_b_i, v_lru_lambda, v_w_lru_o, v_w_out, v_norm2_w, v_ffn_w_up, v_ffn_conv_w, v_ffn_conv_b, v_ffn_w_down, v_norm_f_w):
    names = ["norm1_w", "w_in", "merge_gate_b", "ret_gn_w", "w_ret_o", "lru_conv_w", "lru_conv_b", "lru_w_r", "lru_b_r",
             "lru_w_i", "lru_b_i", "lru_lambda", "w_lru_o", "w_out", "norm2_w", "ffn_w_up", "ffn_conv_w", "ffn_conv_b",
             "ffn_w_down", "norm_f_w"]
    w_args = [norm1_w, w_in, merge_gate_b, ret_gn_w, w_ret_o, lru_conv_w, lru_conv_b, lru_w_r, lru_b_r, lru_w_i, lru_b_i,
              lru_lambda, w_lru_o, w_out, norm2_w, ffn_w_up, ffn_conv_w, ffn_conv_b, ffn_w_down, norm_f_w]
    m_args = [m_norm1_w, m_w_in, m_merge_gate_b, m_ret_gn_w, m_w_ret_o, m_lru_conv_w, m_lru_conv_b, m_lru_w_r, m_lru_b_r,
              m_lru_w_i, m_lru_b_i, m_lru_lambda, m_w_lru_o, m_w_out, m_norm2_w, m_ffn_w_up, m_ffn_conv_w, m_ffn_conv_b,
              m_ffn_w_down, m_norm_f_w]
    v_args = [v_norm1_w, v_w_in, v_merge_gate_b, v_ret_gn_w, v_w_ret_o, v_lru_conv_w, v_lru_conv_b, v_lru_w_r, v_lru_b_r,
              v_lru_w_i, v_lru_b_i, v_lru_lambda, v_w_lru_o, v_w_out, v_norm2_w, v_ffn_w_up, v_ffn_conv_w, v_ffn_conv_b,
              v_ffn_w_down, v_norm_f_w]
    orig_shape = {n: a.shape for n, a in zip(names, w_args)}
    local_shape = {n: s for n, s, _ in BIG + SMALL_SHARDED}
    local_shape.update({n: s for n, s in REPLICATED})
    W = {n: a.reshape(local_shape[n]) for n, a in zip(names, w_args)}
    M = {n: a.reshape(local_shape[n]) for n, a in zip(names, m_args)}
    V = {n: a.reshape(local_shape[n]) for n, a in zip(names, v_args)}

    xi, yi, ci = _mesh_pos()
    dev = 4 * xi + 2 * yi + ci
    chip = (2 * xi + yi).astype(jnp.int32).reshape(1)
    core = ci.astype(jnp.int32).reshape(1)

    chip_order = jnp.stack([2 * xi + yi, 2 * (1 - xi) + yi, 2 * xi + (1 - yi), 2 * (1 - xi) + (1 - yi)]).astype(jnp.int32)

    small_names = [n for n, _, _ in SMALL_SHARDED]
    first_shards = [W[n].astype(BF16) for n, _, _ in FIRST] + [W[n] for n in small_names]
    late_shards = [W[n].astype(BF16) for n, _, _ in LATE]
    rep = {n: W[n] for n, _ in REPLICATED}
    loss_acc, grad_x, g, rs = _local_step(x, positions, loss_target, first_shards, rep, late_shards, core, chip_order)

    G_out, D_out, M_out, V_out = {}, {}, {}, {}
    for n, _, _ in BIG:
        G_out[n], D_out[n], M_out[n], V_out[n] = _adamw_shard(rs[n][0], rs[n][1], W[n], M[n], V[n], chip, "adamw_" + n)

    rep_names = [n for n, _ in REPLICATED]
    red_names = rep_names + small_names
    red = _small_allreduce([g[n] for n in red_names] + [loss_acc[0:1, :]], "allreduce_small_grads")
    loss = red[-1][0, 0]
    gs = dict(zip(red_names, red[:-1]))
    for n, s, mode in SMALL_SHARDED:
        if mode == "cols":
            gs[n] = lax.dynamic_slice_in_dim(gs[n], dev * s[1], s[1], axis=1)
        else:
            full = gs[n].reshape(LRU_BLOCKS, LRU_BLOCK)
            gs[n] = lax.dynamic_slice_in_dim(full, dev * s[1], s[1], axis=1)
    d2, m2, v2 = _adamw_small([gs[n] for n in red_names], [W[n] for n in red_names], [M[n] for n in red_names],
                              [V[n] for n in red_names], "adamw_small")
    for i, n in enumerate(red_names):
        G_out[n], D_out[n], M_out[n], V_out[n] = gs[n], d2[i], m2[i], v2[i]

    outs = [loss, grad_x]
    for group in (G_out, D_out, M_out, V_out):
        outs += [group[n].reshape(orig_shape[n]) for n in names]
    return tuple(outs)
```

```python
import math

import jax
import jax.numpy as jnp
from jax import lax
from jax.experimental import pallas as pl
from jax.experimental.pallas import tpu as pltpu

F32 = jnp.float32
BF16 = jnp.bfloat16
MESH = pl.DeviceIdType.MESH

D_MODEL = 1024
CHUNK = 64
RET_HEADS = 4
RET_DK = 128
RET_DV = 256
LRU_BLOCKS = 4
LRU_BLOCK = 256
LRU_CONV = 4
LRU_C = 8.0
D_FF = 3072
FFN_CONV = 3
ROPE_BASE = 10000.0
RMS_EPS = 1e-6
GN_EPS = 1e-6
D_IN = 7168
ADAM_LR, ADAM_B1, ADAM_B2, ADAM_EPS, ADAM_WD, ADAM_STEP = 0.001, 0.9, 0.999, 1e-08, 0.01, 10

N_DEV = 8
V7X_VMEM_BYTES = 64 * 1024 * 1024
VMEM_LIMIT = V7X_VMEM_BYTES - 8 * 1024 * 1024
RET_BLOCK = 256
LANES = 128

COL_Q, COL_K = 0, 4
COL_V, COL_G, COL_XL, COL_YL = 4, 8, 12, 16
COL_GR, COL_GL = 5, 6

HBM_SPEC = pl.BlockSpec(memory_space=pl.ANY)


def _gelu(x):
    c = math.sqrt(2.0 / math.pi)
    t = jnp.tanh(x * (c + (c * 0.044715) * (x * x)))
    return x * (0.5 * t + 0.5)


def _gelu_and_grad(x):
    c = math.sqrt(2.0 / math.pi)
    x2 = x * x
    t = jnp.tanh(x * (c + (c * 0.044715) * x2))
    h = 0.5 * t + 0.5
    g = x * h
    dg = h + g * (1.0 - h) * ((2.0 * c) + (6.0 * c * 0.044715) * x2)
    return g, dg


def _sigmoid(x):
    return 1.0 / (1.0 + jnp.exp(-x))


SUBLANES = 8


def _shift_down(x, s, fill):
    r = pltpu.roll(x, s, 0)
    rows = lax.broadcasted_iota(jnp.int32, (SUBLANES,) + x.shape[1:], 0)
    top = jnp.where(rows >= s, r[:SUBLANES], fill)
    return jnp.concatenate([top, r[SUBLANES:]], axis=0)


def _shift_up(x, s, fill):
    n = x.shape[0]
    r = pltpu.roll(x, n - s, 0)
    rows = lax.broadcasted_iota(jnp.int32, (SUBLANES,) + x.shape[1:], 0)
    bottom = jnp.where(rows < SUBLANES - s, r[n - SUBLANES:], fill)
    return jnp.concatenate([r[:n - SUBLANES], bottom], axis=0)


SCAN_CHUNK = 64


def _scan_forward(a, b):
    n = a.shape[0]
    s = 1
    while s < n:
        if s % SUBLANES:
            b = a * _shift_down(b, s, 0.0) + b
            a = a * _shift_down(a, s, 1.0)
        else:
            b = jnp.concatenate([b[:s], a[s:] * b[:n - s] + b[s:]], axis=0)
            a = jnp.concatenate([a[:s], a[s:] * a[:n - s]], axis=0)
        s *= 2
    return a, b


def _scan_backward(a_next, u):
    n = u.shape[0]
    s = 1
    while s < n:
        if s % SUBLANES:
            u = u + a_next * _shift_up(u, s, 0.0)
            a_next = a_next * _shift_up(a_next, s, 1.0)
        else:
            u = jnp.concatenate([u[:n - s] + a_next[:n - s] * u[s:], u[n - s:]], axis=0)
            a_next = jnp.concatenate([a_next[:n - s] * a_next[s:], a_next[n - s:]], axis=0)
        s *= 2
    return a_next, u


def _scan_forward_ref(a_ref, b_ref, h_ref):
    S, W = a_ref.shape
    for strip in range(W // LANES):
        cols = pl.ds(strip * LANES, LANES)

        def body(k, carry, cols=cols):
            rows = pl.ds(pl.multiple_of(k * SCAN_CHUNK, SCAN_CHUNK), SCAN_CHUNK)
            a_cum, h_loc = _scan_forward(a_ref[rows, cols], b_ref[rows, cols])
            h = h_loc + a_cum * carry
            h_ref[rows, cols] = h
            return h[SCAN_CHUNK - 1:, :]

        lax.fori_loop(0, S // SCAN_CHUNK, body, jnp.zeros((1, LANES), F32))


def _scan_backward_ref(an_ref, u_ref, d_ref):
    S, W = an_ref.shape
    n_chunks = S // SCAN_CHUNK
    for strip in range(W // LANES):
        cols = pl.ds(strip * LANES, LANES)

        def body(i, carry, cols=cols):
            rows = pl.ds(pl.multiple_of((n_chunks - 1 - i) * SCAN_CHUNK, SCAN_CHUNK), SCAN_CHUNK)
            an_cum, d_loc = _scan_backward(an_ref[rows, cols], u_ref[rows, cols])
            d = d_loc + an_cum * carry
            d_ref[rows, cols] = d
            return d[:1, :]

        lax.fori_loop(0, n_chunks, body, jnp.zeros((1, LANES), F32))


def _dot(a, b, dims):
    return lax.dot_general(a, b, (dims, ((), ())), preferred_element_type=F32)


NN = ((1,), (0,))
NT = ((1,), (1,))
TN = ((0,), (0,))


def _mesh_pos():
    return lax.axis_index("x"), lax.axis_index("y"), lax.axis_index("c")


def _other_chips(x, y):
    return [(1 - x, y), (x, 1 - y), (1 - x, 1 - y)]


def _full_shape(shard_shape, mode):
    if mode == "rows":
        return (N_DEV * shard_shape[0],) + tuple(shard_shape[1:])
    if mode == "cols":
        return (shard_shape[0], N_DEV * shard_shape[1])
    if mode == "mid":
        return (shard_shape[0], N_DEV * shard_shape[1], shard_shape[2])
    return (N_DEV,) + tuple(shard_shape)


def _extent(shard_shape, mode):
    return {"rows": shard_shape[0], "cols": shard_shape[1], "mid": shard_shape[1], "stack": 1}[mode]


def _window(ref, mode, extent, d):
    if mode == "stack":
        return ref.at[d]
    start = pl.multiple_of(d * extent, extent)
    if mode == "rows":
        return ref.at[pl.ds(start, extent)]
    if mode == "cols":
        return ref.at[:, pl.ds(start, extent)]
    return ref.at[:, pl.ds(start, extent), :]


class _Job:
    def __init__(self, inputs, out_shapes, sems, start, finish, aliases=None):
        self.inputs, self.out_shapes, self.sems = list(inputs), list(out_shapes), sems
        self.start, self.finish, self.aliases = start, finish, dict(aliases or {})


def _remote(src, dst, send_sem, recv_sem, to):
    return pltpu.make_async_remote_copy(src_ref=src, dst_ref=dst, send_sem=send_sem, recv_sem=recv_sem,
                                        device_id=to, device_id_type=MESH)


def _ag_first_job(shards, modes):
    n = len(shards)
    extents = [_extent(s.shape, m) for s, m in zip(shards, modes)]

    def copies(x_refs, out_refs, send, recv, local, arriving):
        x, y, c = _mesh_pos()
        peers = [(x, y, 1 - c)] + [(*chip, c) for chip in _other_chips(x, y)]
        win = lambda i, p: _window(out_refs[i], modes[i], extents[i], 4 * p[0] + 2 * p[1] + p[2])
        if arriving:
            return [_remote(x_refs[i], win(i, p), send.at[i, k], recv.at[i, k], p)
                    for i in range(n) for k, p in enumerate(peers)]
        mine = [pltpu.make_async_copy(x_refs[i], win(i, (x, y, c)), local.at[i]) for i in range(n)]
        sends = [_remote(x_refs[i], win(i, (x, y, c)), send.at[i, k], recv.at[i, k], p)
                 for i in range(n) for k, p in enumerate(peers)]
        return mine, sends

    def start(*refs):
        mine, sends = copies(*refs, False)
        for cp in mine + sends:
            cp.start()

    def finish(*refs):
        for cp in copies(*refs, True):
            cp.wait_recv()
        mine, sends = copies(*refs, False)
        for cp in sends:
            cp.wait_send()
        for cp in mine:
            cp.wait()

    out_shapes = [jax.ShapeDtypeStruct(_full_shape(s.shape, m), s.dtype) for s, m in zip(shards, modes)]
    return _Job(shards, out_shapes, ((n, 4), (n, 4), (n,)), start, finish)


def _ag_second_job(fulls, modes, shard_shapes):
    n = len(fulls)
    extents = [_extent(s, m) for s, m in zip(shard_shapes, modes)]

    def copies(_, out_refs, send, recv, local, core_of_block):
        x, y, c = _mesh_pos()
        pc = c if core_of_block == "mine" else 1 - c
        win = lambda i, chip: _window(out_refs[i], modes[i], extents[i], 4 * chip[0] + 2 * chip[1] + pc)
        return [_remote(win(i, chip), win(i, chip), send.at[i, j], recv.at[i, j], (x, y, 1 - c))
                for i in range(n) for j, chip in enumerate(_other_chips(x, y))]

    def start(*refs):
        for cp in copies(*refs, "mine"):
            cp.start()

    def finish(*refs):
        for cp in copies(*refs, "sibling"):
            cp.wait_recv()
        for cp in copies(*refs, "mine"):
            cp.wait_send()

    out_shapes = [jax.ShapeDtypeStruct(f.shape, f.dtype) for f in fulls]
    return _Job(fulls, out_shapes, ((n, 3), (n, 3), (1,)), start, finish, aliases={i: i for i in range(n)})


def _rs_sibling_job(grads, modes, shard_shapes):
    n = len(grads)
    extents = [_extent(s, m) for s, m in zip(shard_shapes, modes)]

    def copies(g_refs, out_refs, send, recv, local):
        x, y, c = _mesh_pos()
        return [_remote(_window(g_refs[i], modes[i], extents[i], 2 * k + (1 - c)), out_refs[i].at[k],
                        send.at[i, k], recv.at[i, k], (x, y, 1 - c))
                for i in range(n) for k in range(4)]

    def start(*refs):
        for cp in copies(*refs):
            cp.start()

    def finish(*refs):
        cps = copies(*refs)
        for cp in cps:
            cp.wait_recv()
        for cp in cps:
            cp.wait_send()

    out_shapes = [jax.ShapeDtypeStruct((4,) + tuple(s), g.dtype) for s, g in zip(shard_shapes, grads)]
    return _Job(grads, out_shapes, ((n, 4), (n, 4), (1,)), start, finish)


def _rs_chip_job(partials):
    n = len(partials)

    def copies(p_refs, out_refs, send, recv, local):
        x, y, c = _mesh_pos()
        return [_remote(p_refs[i].at[2 * px + py], out_refs[i].at[j], send.at[i, j], recv.at[i, j], (px, py, c))
                for i in range(n) for j, (px, py) in enumerate(_other_chips(x, y))]

    def start(*refs):
        for cp in copies(*refs):
            cp.start()

    def finish(*refs):
        cps = copies(*refs)
        for cp in cps:
            cp.wait_recv()
        for cp in cps:
            cp.wait_send()

    out_shapes = [jax.ShapeDtypeStruct((3,) + tuple(p.shape[1:]), p.dtype) for p in partials]
    return _Job(partials, out_shapes, ((n, 3), (n, 3), (1,)), start, finish)


def _all_true(conds):
    out = conds[0]
    for c in conds[1:]:
        out = jnp.logical_and(out, c)
    return out


def _pcall(body, *, name, grid, in_specs, out_specs, out_shape, args, sem, scratch=(), jobs=(), alias_in_out=None):
    n_in, n_out, n_scr = len(args), len(out_shape), len(scratch)
    job_in = [a for j in jobs for a in j.inputs]
    job_out = [s for j in jobs for s in j.out_shapes]
    job_sems = [pltpu.SemaphoreType.DMA(shape) for j in jobs for shape in j.sems]
    aliases, in_off, out_off = dict(alias_in_out or {}), n_in, n_out
    for j in jobs:
        for a, b in j.aliases.items():
            aliases[in_off + a] = out_off + b
        in_off += len(j.inputs)
        out_off += len(j.out_shapes)

    def wrapped(*refs):
        ins = refs[:n_in]
        jins = refs[n_in:n_in + len(job_in)]
        o0 = n_in + len(job_in)
        outs = refs[o0:o0 + n_out]
        jouts = refs[o0 + n_out:o0 + n_out + len(job_out)]
        s0 = o0 + n_out + len(job_out)
        scr = refs[s0:s0 + n_scr]
        jsems = refs[s0 + n_scr:]
        if jobs:
            ids = [pl.program_id(a) for a in range(len(grid))]
            first = _all_true([i == 0 for i in ids])
            last = _all_true([i == g - 1 for i, g in zip(ids, grid)])

            def per_job(which):
                i0 = o0_ = 0
                for k, j in enumerate(jobs):
                    fn = j.start if which == "start" else j.finish
                    fn(jins[i0:i0 + len(j.inputs)], jouts[o0_:o0_ + len(j.out_shapes)], *jsems[3 * k:3 * k + 3])
                    i0 += len(j.inputs)
                    o0_ += len(j.out_shapes)

            @pl.when(first)
            def _():
                per_job("start")

        body(*ins, *outs, *scr)
        if jobs:
            @pl.when(last)
            def _():
                per_job("finish")

    semantics = tuple("arbitrary" for _ in grid) if jobs else sem
    return pl.pallas_call(
        wrapped, name=name, grid=grid,
        in_specs=list(in_specs) + [HBM_SPEC] * len(job_in),
        out_specs=list(out_specs) + [HBM_SPEC] * len(job_out),
        out_shape=list(out_shape) + job_out,
        scratch_shapes=list(scratch) + job_sems,
        input_output_aliases=aliases,
        compiler_params=pltpu.CompilerParams(dimension_semantics=semantics, vmem_limit_bytes=VMEM_LIMIT),
    )(*args, *job_in)


def _row_tile(rows, cap):
    if rows <= cap:
        return rows
    best = None
    for t in range(16, cap + 1, 16):
        if rows % t == 0:
            best = t
    assert best is not None
    return best


def _matmul(a, b, mode, out_dtype, tm, tn, tk, name, add=None, jobs=()):
    if mode == "tn":
        K, M = a.shape
    else:
        M, K = a.shape
    N = b.shape[0] if mode == "nt" else b.shape[1]
    tm, tn, tk = min(tm, M), min(tn, N), min(tk, K)
    assert M % tm == 0 and N % tn == 0 and K % tk == 0
    nk = K // tk
    dims = {"nn": NN, "nt": NT, "tn": TN}[mode]

    def body(*refs):
        if add is None:
            a_ref, b_ref, o_ref, acc = refs
            add_ref = None
        else:
            a_ref, b_ref, add_ref, o_ref, acc = refs
        k = pl.program_id(2)
        p = _dot(a_ref[...], b_ref[...], dims)

        def finish(r):
            if add_ref is not None:
                r = r + add_ref[...].astype(F32)
            o_ref[...] = r.astype(out_dtype)

        if nk == 1:
            finish(p)
        else:
            @pl.when(k == 0)
            def _():
                acc[...] = p

            @pl.when(k > 0)
            def _():
                acc[...] += p

            @pl.when(k == nk - 1)
            def _():
                finish(acc[...])

    if mode == "tn":
        a_spec = pl.BlockSpec((tk, tm), lambda i, j, k: (k, i))
    else:
        a_spec = pl.BlockSpec((tm, tk), lambda i, j, k: (i, k))
    if mode == "nt":
        b_spec = pl.BlockSpec((tn, tk), lambda i, j, k: (j, k))
    else:
        b_spec = pl.BlockSpec((tk, tn), lambda i, j, k: (k, j))
    in_specs = [a_spec, b_spec]
    args = [a, b]
    if add is not None:
        in_specs.append(pl.BlockSpec((tm, tn), lambda i, j, k: (i, j)))
        args.append(add)
    return _pcall(
        body, name=name, grid=(M // tm, N // tn, nk), in_specs=in_specs,
        out_specs=[pl.BlockSpec((tm, tn), lambda i, j, k: (i, j))],
        out_shape=[jax.ShapeDtypeStruct((M, N), out_dtype)], args=args,
        scratch=[pltpu.VMEM((tm, tn) if nk > 1 else (8, LANES), F32)],
        sem=("parallel", "parallel", "arbitrary"), jobs=jobs)


def _rmsnorm_fwd(x, w, tm, name):
    T, D = x.shape

    def body(x_ref, w_ref, h_ref):
        xv = x_ref[...]
        r = lax.rsqrt(jnp.mean(xv * xv, axis=-1, keepdims=True) + RMS_EPS)
        h_ref[...] = (xv * r * w_ref[...]).astype(BF16)

    return _pcall(
        body, name=name, grid=(T // tm,),
        in_specs=[pl.BlockSpec((tm, D), lambda i: (i, 0)), pl.BlockSpec((1, D), lambda i: (0, 0))],
        out_specs=[pl.BlockSpec((tm, D), lambda i: (i, 0))],
        out_shape=[jax.ShapeDtypeStruct((T, D), BF16)], args=[x, w], sem=("parallel",))[0]


def _rmsnorm_bwd_add(dres, dh, x, w, tm, name, want_bf16, jobs=()):
    T, D = x.shape

    def body(dres_ref, dh_ref, x_ref, w_ref, *outs):
        if want_bf16:
            dx_ref, dxb_ref, dw_ref = outs
        else:
            dx_ref, dw_ref = outs
        i = pl.program_id(0)
        xv = x_ref[...]
        r = lax.rsqrt(jnp.mean(xv * xv, axis=-1, keepdims=True) + RMS_EPS)
        xh = xv * r
        dh_v = dh_ref[...].astype(F32)
        dxh = dh_v * w_ref[...]
        dx = dres_ref[...] + r * (dxh - xh * jnp.mean(dxh * xh, axis=-1, keepdims=True))
        dx_ref[...] = dx
        if want_bf16:
            dxb_ref[...] = dx.astype(BF16)
        part = jnp.sum(dh_v * xh, axis=0, keepdims=True)

        @pl.when(i == 0)
        def _():
            dw_ref[...] = part

        @pl.when(i > 0)
        def _():
            dw_ref[...] += part

    tile = pl.BlockSpec((tm, D), lambda i: (i, 0))
    row = pl.BlockSpec((1, D), lambda i: (0, 0))
    out_specs = [tile] + ([tile] if want_bf16 else []) + [row]
    out_shape = ([jax.ShapeDtypeStruct((T, D), F32)] + ([jax.ShapeDtypeStruct((T, D), BF16)] if want_bf16 else [])
                 + [jax.ShapeDtypeStruct((1, D), F32)])
    return _pcall(body, name=name, grid=(T // tm,), in_specs=[tile, tile, tile, row], out_specs=out_specs,
                  out_shape=out_shape, args=[dres, dh, x, w], sem=("arbitrary",), jobs=jobs)


def _loss_head(x2, target, wf, tm, name):
    T, D = x2.shape

    def body(x_ref, t_ref, w_ref, dx_ref, dxb_ref, loss_ref, dw_ref):
        i = pl.program_id(0)
        xv = x_ref[...]
        r = lax.rsqrt(jnp.mean(xv * xv, axis=-1, keepdims=True) + RMS_EPS)
        xh = xv * r
        wv = w_ref[...]
        e = xh * wv - t_ref[...]
        lpart = 0.5 * jnp.sum(jnp.sum(e * e, axis=-1, keepdims=True), axis=0, keepdims=True) * (1.0 / D)
        dy = e * (1.0 / D)
        dxh = dy * wv
        dx = r * (dxh - xh * jnp.mean(dxh * xh, axis=-1, keepdims=True))
        dx_ref[...] = dx
        dxb_ref[...] = dx.astype(BF16)
        wpart = jnp.sum(dy * xh, axis=0, keepdims=True)
        lfull = jnp.broadcast_to(lpart, (8, LANES))

        @pl.when(i == 0)
        def _():
            loss_ref[...] = lfull
            dw_ref[...] = wpart

        @pl.when(i > 0)
        def _():
            loss_ref[...] += lfull
            dw_ref[...] += wpart

    tile = pl.BlockSpec((tm, D), lambda i: (i, 0))
    row = pl.BlockSpec((1, D), lambda i: (0, 0))
    return _pcall(
        body, name=name, grid=(T // tm,), in_specs=[tile, tile, row],
        out_specs=[tile, tile, pl.BlockSpec((8, LANES), lambda i: (0, 0)), row],
        out_shape=[jax.ShapeDtypeStruct((T, D), F32), jax.ShapeDtypeStruct((T, D), BF16),
                   jax.ShapeDtypeStruct((8, LANES), F32), jax.ShapeDtypeStruct((1, D), F32)],
        args=[x2, target, wf], sem=("arbitrary",))


def _rope_tables(pos_col, inv2, tm, name):
    T = pos_col.shape[0]

    def body(p_ref, f_ref, c_ref, s_ref):
        ang = p_ref[...] * f_ref[...]
        lane = lax.broadcasted_iota(jnp.int32, ang.shape, 1)
        c_ref[...] = jnp.cos(ang)
        s_ref[...] = jnp.where(lane < RET_DK // 2, -1.0, 1.0) * jnp.sin(ang)

    tile = pl.BlockSpec((tm, RET_DK), lambda i: (i, 0))
    return _pcall(
        body, name=name, grid=(T // tm,),
        in_specs=[pl.BlockSpec((tm, 1), lambda i: (i, 0)), pl.BlockSpec((1, RET_DK), lambda i: (0, 0))],
        out_specs=[tile, tile], out_shape=[jax.ShapeDtypeStruct((T, RET_DK), F32)] * 2,
        args=[pos_col, inv2], sem=("parallel",))


def _mix_fwd(a_in, b_in, proj, x, w_ro, w_lo, w_out, mb, w2, tm, name):
    T, D = x.shape

    def body(a_ref, b_ref, gr_ref, gl_ref, x_ref, wro_ref, wlo_ref, wout_ref, mb_ref, w2_ref,
             x1_ref, mix_ref, h2_ref, ya_ref, yb_ref):
        ya = _dot(a_ref[...], wro_ref[...], NN)
        yb = _dot(b_ref[...], wlo_ref[...], NN)
        ya_ref[...] = ya.astype(BF16)
        yb_ref[...] = yb.astype(BF16)
        sa = _sigmoid(gr_ref[...].astype(F32) + mb_ref[0:1, :])
        sb = _sigmoid(gl_ref[...].astype(F32) + mb_ref[1:2, :])
        mix = (sa * ya + sb * yb).astype(BF16)
        mix_ref[...] = mix
        x1 = x_ref[...] + _dot(mix, wout_ref[...], NN)
        x1_ref[...] = x1
        r = lax.rsqrt(jnp.mean(x1 * x1, axis=-1, keepdims=True) + RMS_EPS)
        h2_ref[...] = (x1 * r * w2_ref[...]).astype(BF16)

    tile = pl.BlockSpec((tm, D), lambda i: (i, 0))
    wspec = pl.BlockSpec((D, D), lambda i: (0, 0))
    return _pcall(
        body, name=name, grid=(T // tm,),
        in_specs=[tile, tile,
                  pl.BlockSpec((tm, D), lambda i: (i, COL_GR)), pl.BlockSpec((tm, D), lambda i: (i, COL_GL)),
                  tile, wspec, wspec, wspec,
                  pl.BlockSpec((2, D), lambda i: (0, 0)), pl.BlockSpec((1, D), lambda i: (0, 0))],
        out_specs=[tile] * 5,
        out_shape=[jax.ShapeDtypeStruct((T, D), F32)] + [jax.ShapeDtypeStruct((T, D), BF16)] * 4,
        args=[a_in, b_in, proj, proj, x, w_ro, w_lo, w_out, mb, w2], sem=("parallel",))


def _write_pieces(dst_ref, sems, stashes, row0, col0s, ids, grid, compute):
    def aligned(v, m):
        return v if isinstance(v, int) else pl.multiple_of(v, m)

    def copies(slot):
        return [pltpu.make_async_copy(
                    st.at[slot],
                    dst_ref.at[pl.ds(aligned(row0, 16), st.shape[1]), pl.ds(aligned(c0, LANES), st.shape[2])],
                    sems.at[slot, k])
                for k, (st, c0) in enumerate(zip(stashes, col0s))]

    step = ids[0]
    for i, g in zip(ids[1:], grid[1:]):
        step = step * g + i
    slot = step % 2
    last = _all_true([i == g - 1 for i, g in zip(ids, grid)])
    compute(slot)

    @pl.when(step > 0)
    def _():
        for cp in copies(1 - slot):
            cp.wait()

    for cp in copies(slot):
        cp.start()

    @pl.when(last)
    def _():
        for cp in copies(slot):
            cp.wait()


def _mix_bwd(dx1b, ya, yb, proj, w_ro, w_lo, w_out, mb, tm, name, jobs=()):
    T, D = ya.shape
    grid = (T // tm,)

    def body(dx_ref, ya_ref, yb_ref, gr_ref, gl_ref, wro_ref, wlo_ref, wout_ref, mb_ref,
             da_ref, db_ref, dya_ref, dyb_ref, dp_ref, dmb_ref, dgr_s, dgl_s, wsem):
        i = pl.program_id(0)

        def compute(slot):
            dmix = _dot(dx_ref[...], wout_ref[...], NT)
            ya = ya_ref[...].astype(F32)
            yb = yb_ref[...].astype(F32)
            sa = _sigmoid(gr_ref[...].astype(F32) + mb_ref[0:1, :])
            sb = _sigmoid(gl_ref[...].astype(F32) + mb_ref[1:2, :])
            dya = (dmix * sa).astype(BF16)
            dyb = (dmix * sb).astype(BF16)
            dgr = dmix * ya * sa * (1.0 - sa)
            dgl = dmix * yb * sb * (1.0 - sb)
            dya_ref[...] = dya
            dyb_ref[...] = dyb
            dgr_s[slot] = dgr.astype(BF16)
            dgl_s[slot] = dgl.astype(BF16)
            da_ref[...] = _dot(dya, wro_ref[...], NT).astype(BF16)
            db_ref[...] = _dot(dyb, wlo_ref[...], NT).astype(BF16)

            @pl.when(i == 0)
            def _():
                dmb_ref[...] = jnp.zeros_like(dmb_ref)

            dmb_ref[0:1, :] += jnp.sum(dgr, axis=0, keepdims=True)
            dmb_ref[1:2, :] += jnp.sum(dgl, axis=0, keepdims=True)

        _write_pieces(dp_ref, wsem, [dgr_s, dgl_s], i * tm, [COL_GR * D, COL_GL * D], [i], grid, compute)

    tile = pl.BlockSpec((tm, D), lambda i: (i, 0))
    wspec = pl.BlockSpec((D, D), lambda i: (0, 0))
    two = pl.BlockSpec((2, D), lambda i: (0, 0))
    return _pcall(
        body, name=name, grid=grid,
        in_specs=[tile, tile, tile,
                  pl.BlockSpec((tm, D), lambda i: (i, COL_GR)), pl.BlockSpec((tm, D), lambda i: (i, COL_GL)),
                  wspec, wspec, wspec, two],
        out_specs=[tile] * 4 + [HBM_SPEC, two],
        out_shape=[jax.ShapeDtypeStruct((T, D), BF16)] * 4
                  + [jax.ShapeDtypeStruct((T, D_IN), BF16), jax.ShapeDtypeStruct((2, D), F32)],
        args=[dx1b, ya, yb, proj, proj, w_ro, w_lo, w_out, mb],
        scratch=[pltpu.VMEM((2, tm, D), BF16), pltpu.VMEM((2, tm, D), BF16), pltpu.SemaphoreType.DMA((2, 2))],
        sem=("arbitrary",), jobs=jobs)


def _ret_decay_consts(lg):
    L = RET_BLOCK
    n = lax.broadcasted_iota(jnp.int32, (L, L), 0)
    m = lax.broadcasted_iota(jnp.int32, (L, L), 1)
    cn, cm = n // CHUNK, m // CHUNK
    expo = jnp.where(cn == cm, jnp.abs(n - m), n - m).astype(F32)
    wm = jnp.where(cm <= cn, jnp.exp(lg * expo), 0.0)
    idx = lax.broadcasted_iota(jnp.int32, (L, 1), 0).astype(F32)
    qd = jnp.exp(lg * (idx + 1.0))
    kd = jnp.exp(lg * (L - 1.0 - idx))
    bd = jnp.exp(lg * float(L))
    return wm, qd, kd, bd


def _rotate(v, cos2, sin2s):
    return v * cos2 + pltpu.roll(v, RET_DK // 2, 1) * sin2s


def _rotate_t(d, cos2, sin2s):
    return d * cos2 - pltpu.roll(d, RET_DK // 2, 1) * sin2s


def _retention_fwd(proj, cos2, sin2s, lgam, gn_w, B, S, name, jobs=()):
    T = B * S
    nb = S // RET_BLOCK
    scale = RET_DK ** -0.5

    def body(q_ref, k_ref, v_ref, g_ref, c_ref, s_ref, lg_ref, gw_ref, o_ref, a_ref, qr, kr, st):
        wm, qd, kd, bd = _ret_decay_consts(lg_ref[0:1, 0:1])
        cos2, sin2s = c_ref[...], s_ref[...]
        qr[...] = _rotate(q_ref[...].astype(F32), cos2, sin2s)
        kr[...] = _rotate(k_ref[...].astype(F32), cos2, sin2s) * scale
        st[...] = jnp.zeros_like(st)
        gw = gw_ref[...]
        for j in range(nb):
            rows = pl.ds(j * RET_BLOCK, RET_BLOCK)
            qb = qr[rows, :]
            kb = kr[rows, :]
            vb = v_ref[rows, :].astype(BF16)
            sc = _dot(qb.astype(BF16), kb.astype(BF16), NT) * wm
            o = _dot(sc.astype(BF16), vb, NN) + _dot((qb * qd).astype(BF16), st[...].astype(BF16), NN)
            st[...] = st[...] * bd + _dot((kb * kd).astype(BF16), vb, TN)
            o_ref[rows, :] = o
            mu = jnp.mean(o, axis=-1, keepdims=True)
            oc = o - mu
            var = jnp.mean(oc * oc, axis=-1, keepdims=True)
            y = oc * lax.rsqrt(var + GN_EPS) * gw
            g = g_ref[rows, :].astype(F32)
            a_ref[rows, :] = (y * (g * _sigmoid(g))).astype(BF16)

    blk = lambda w, off: pl.BlockSpec((S, w), lambda b, h: (b, off + h))
    return _pcall(
        body, name=name, grid=(B, RET_HEADS),
        in_specs=[blk(RET_DK, COL_Q), blk(RET_DK, COL_K), blk(RET_DV, COL_V), blk(RET_DV, COL_G),
                  pl.BlockSpec((S, RET_DK), lambda b, h: (b, 0)), pl.BlockSpec((S, RET_DK), lambda b, h: (b, 0)),
                  pl.BlockSpec((None, 8, LANES), lambda b, h: (h, 0, 0)),
                  pl.BlockSpec((1, RET_DV), lambda b, h: (0, h))],
        out_specs=[blk(RET_DV, 0), blk(RET_DV, 0)],
        out_shape=[jax.ShapeDtypeStruct((T, RET_HEADS * RET_DV), F32),
                   jax.ShapeDtypeStruct((T, RET_HEADS * RET_DV), BF16)],
        args=[proj, proj, proj, proj, cos2, sin2s, lgam, gn_w],
        scratch=[pltpu.VMEM((S, RET_DK), F32), pltpu.VMEM((S, RET_DK), F32), pltpu.VMEM((RET_DK, RET_DV), F32)],
        sem=("parallel", "parallel"), jobs=jobs)


def _retention_bwd(da_in, o, proj, dproj, cos2, sin2s, lgam, gn_w, B, S, name, jobs=()):
    T = B * S
    nb = S // RET_BLOCK
    scale = RET_DK ** -0.5
    grid = (RET_HEADS, B)

    def body(da_ref, o_ref, q_ref, k_ref, v_ref, g_ref, c_ref, s_ref, lg_ref, gw_ref, _, dp_ref, dgw_ref,
             qr, kr, do_s, sts, rst, dq_s, dk_s, dv_s, dg_s, wsem):
        h, b = pl.program_id(0), pl.program_id(1)

        def compute(slot):
            wm, qd, kd, bd = _ret_decay_consts(lg_ref[0:1, 0:1])
            cos2, sin2s = c_ref[...], s_ref[...]
            qr[...] = _rotate(q_ref[...].astype(F32), cos2, sin2s)
            kr[...] = _rotate(k_ref[...].astype(F32), cos2, sin2s) * scale
            gw = gw_ref[...]
            st = jnp.zeros((RET_DK, RET_DV), F32)
            dgw = jnp.zeros((1, RET_DV), F32)
            for j in range(nb):
                rows = pl.ds(j * RET_BLOCK, RET_BLOCK)
                ov = o_ref[rows, :]
                mu = jnp.mean(ov, axis=-1, keepdims=True)
                oc = ov - mu
                rstd = lax.rsqrt(jnp.mean(oc * oc, axis=-1, keepdims=True) + GN_EPS)
                y = oc * rstd
                g = g_ref[rows, :].astype(F32)
                sg = _sigmoid(g)
                da = da_ref[rows, :].astype(F32)
                dg_s[slot, rows, :] = (da * (y * gw) * (sg * (1.0 + g * (1.0 - sg)))).astype(BF16)
                dyw = da * (g * sg)
                dgw = dgw + jnp.sum(dyw * y, axis=0, keepdims=True)
                dy = dyw * gw
                do_s[rows, :] = rstd * (dy - jnp.mean(dy, axis=-1, keepdims=True)
                                        - y * jnp.mean(dy * y, axis=-1, keepdims=True))
                sts[j] = st
                st = st * bd + _dot((kr[rows, :] * kd).astype(BF16), v_ref[rows, :].astype(BF16), TN)

            @pl.when(b == 0)
            def _():
                dgw_ref[...] = dgw

            @pl.when(b > 0)
            def _():
                dgw_ref[...] += dgw

            rst[...] = jnp.zeros_like(rst)
            for j in reversed(range(nb)):
                rows = pl.ds(j * RET_BLOCK, RET_BLOCK)
                qb = qr[rows, :]
                kb = kr[rows, :]
                qbb, kbb = qb.astype(BF16), kb.astype(BF16)
                vb = v_ref[rows, :].astype(BF16)
                dob = do_s[rows, :]
                dobb = dob.astype(BF16)
                a_m = (_dot(qbb, kbb, NT) * wm).astype(BF16)
                b_m = (_dot(dobb, vb, NT) * wm).astype(BF16)
                rb = rst[...].astype(BF16)
                dq = _dot(b_m, kbb, NN) + _dot((dob * qd).astype(BF16), sts[j].astype(BF16), NT)
                dk = _dot(b_m, qbb, TN) + kd * _dot(vb, rb, NT)
                dv = _dot(a_m, dobb, TN) + kd * _dot(kbb, rb, NN)
                rst[...] = rst[...] * bd + _dot((qb * qd).astype(BF16), dobb, TN)
                cb, sb = c_ref[rows, :], s_ref[rows, :]
                dq_s[slot, rows, :] = _rotate_t(dq, cb, sb).astype(BF16)
                dk_s[slot, rows, :] = _rotate_t(dk * scale, cb, sb).astype(BF16)
                dv_s[slot, rows, :] = dv.astype(BF16)

        cols = [(COL_Q + h) * RET_DK, (COL_K + h) * RET_DK, (COL_V + h) * RET_DV, (COL_G + h) * RET_DV]
        _write_pieces(dp_ref, wsem, [dq_s, dk_s, dv_s, dg_s], b * S, cols, [h, b], grid, compute)

    blk = lambda w, off: pl.BlockSpec((S, w), lambda h, b: (b, off + h))
    return _pcall(
        body, name=name, grid=grid,
        in_specs=[blk(RET_DV, 0), blk(RET_DV, 0),
                  blk(RET_DK, COL_Q), blk(RET_DK, COL_K), blk(RET_DV, COL_V), blk(RET_DV, COL_G),
                  pl.BlockSpec((S, RET_DK), lambda h, b: (b, 0)), pl.BlockSpec((S, RET_DK), lambda h, b: (b, 0)),
                  pl.BlockSpec((None, 8, LANES), lambda h, b: (h, 0, 0)),
                  pl.BlockSpec((1, RET_DV), lambda h, b: (0, h)), HBM_SPEC],
        out_specs=[HBM_SPEC, pl.BlockSpec((1, RET_DV), lambda h, b: (0, h))],
        out_shape=[jax.ShapeDtypeStruct(dproj.shape, dproj.dtype),
                   jax.ShapeDtypeStruct((1, RET_HEADS * RET_DV), F32)],
        args=[da_in, o, proj, proj, proj, proj, cos2, sin2s, lgam, gn_w, dproj],
        scratch=[pltpu.VMEM((S, RET_DK), F32), pltpu.VMEM((S, RET_DK), F32),
                 pltpu.VMEM((S, RET_DV), F32), pltpu.VMEM((nb, RET_DK, RET_DV), F32),
                 pltpu.VMEM((RET_DK, RET_DV), F32),
                 pltpu.VMEM((2, S, RET_DK), BF16), pltpu.VMEM((2, S, RET_DK), BF16),
                 pltpu.VMEM((2, S, RET_DV), BF16), pltpu.VMEM((2, S, RET_DV), BF16), pltpu.SemaphoreType.DMA((2, 4))],
        sem=("arbitrary", "arbitrary"), jobs=jobs, alias_in_out={10: 0})


def _lru_gates(x, cw, cb, wr, wi, br, bi, lam):
    xc = cb + cw[LRU_CONV - 1:LRU_CONV, :] * x
    for j in range(LRU_CONV - 1):
        xc = xc + cw[j:j + 1, :] * _shift_down(x, LRU_CONV - 1 - j, 0.0)
    xcb = xc.astype(BF16)
    r = _sigmoid(_dot(xcb, wr, NN) + br)
    ig = _sigmoid(_dot(xcb, wi, NN) + bi)
    z = -lam
    sp = jnp.maximum(z, 0.0) + jnp.log1p(jnp.exp(-jnp.abs(z)))
    log_a = (-LRU_C) * r * sp
    a = jnp.exp(log_a)
    z2 = 2.0 * log_a
    taylor = -z2 * (1.0 + z2 * (0.5 + z2 * (1.0 / 6.0 + z2 * (1.0 / 24.0 + z2 * (1.0 / 120.0)))))
    om = jnp.where(z2 > -0.05, taylor, 1.0 - jnp.exp(z2))
    sq = jnp.sqrt(om)
    return xc, xcb, r, ig, sp, a, sq


def _lru_fwd(proj, cw, cb, wr, wi, br, bi, lam, B, S, name):
    T = B * S
    W = LRU_BLOCKS * LRU_BLOCK

    def body(x_ref, y_ref, cw_ref, cb_ref, wr_ref, wi_ref, br_ref, bi_ref, lam_ref, h_ref, bin_ref, a_s, b_s):
        xc, _, _, ig, _, a, sq = _lru_gates(x_ref[...].astype(F32), cw_ref[...], cb_ref[...], wr_ref[...], wi_ref[...],
                                           br_ref[...], bi_ref[...], lam_ref[...])
        a_s[...] = a
        b_s[...] = sq * ig * xc
        _scan_forward_ref(a_s, b_s, h_ref)
        bin_ref[...] = (h_ref[...] * _gelu(y_ref[...].astype(F32))).astype(BF16)

    blk = lambda off: pl.BlockSpec((S, LRU_BLOCK), lambda b, n: (b, off + n))
    vec = lambda rows: pl.BlockSpec((rows, LRU_BLOCK), lambda b, n: (0, n))
    wspec = pl.BlockSpec((None, LRU_BLOCK, LRU_BLOCK), lambda b, n: (n, 0, 0))
    return _pcall(
        body, name=name, grid=(B, LRU_BLOCKS),
        in_specs=[blk(COL_XL), blk(COL_YL), vec(LRU_CONV), vec(1), wspec, wspec, vec(1), vec(1), vec(1)],
        out_specs=[blk(0), blk(0)],
        out_shape=[jax.ShapeDtypeStruct((T, W), F32), jax.ShapeDtypeStruct((T, W), BF16)],
        args=[proj, proj, cw, cb, wr, wi, br, bi, lam],
        scratch=[pltpu.VMEM((S, LRU_BLOCK), F32), pltpu.VMEM((S, LRU_BLOCK), F32)], sem=("parallel", "parallel"))


def _lru_bwd(db_in, h, proj, dproj, cw, cb, wr, wi, br, bi, lam, B, S, name, jobs=()):
    T = B * S
    W = LRU_BLOCKS * LRU_BLOCK

    grid = (LRU_BLOCKS, B)

    def body(dbin_ref, h_ref, x_ref, y_ref, cw_ref, cb_ref, wr_ref, wi_ref, br_ref, bi_ref, lam_ref, _,
             dp_ref, dcw_ref, dcb_ref, dwr_ref, dwi_ref, dbr_ref, dbi_ref, dlam_ref, dx_s, dy_s, wsem,
             an_s, u_s, dh_s):
        n, b = pl.program_id(0), pl.program_id(1)

        def compute(slot):
            x = x_ref[...].astype(F32)
            cw = cw_ref[...]
            wr, wi = wr_ref[...], wi_ref[...]
            lam = lam_ref[...]
            xc, xcb, r, ig, sp, a, sq = _lru_gates(x, cw, cb_ref[...], wr, wi, br_ref[...], bi_ref[...], lam)
            hv = h_ref[...]
            gel, dgel = _gelu_and_grad(y_ref[...].astype(F32))
            dbin = dbin_ref[...].astype(F32)
            dy_s[slot] = (dbin * hv * dgel).astype(BF16)
            an_s[...] = _shift_up(a, 1, 0.0)
            u_s[...] = dbin * gel
            _scan_backward_ref(an_s, u_s, dh_s)
            dh = dh_s[...]
            hprev = _shift_down(hv, 1, 0.0)
            d_ig = dh * sq * xc
            d_xc = dh * sq * ig
            a2 = a * a
            d_loga = dh * hprev * a - dh * ig * xc * a2 / sq
            d_r = d_loga * ((-LRU_C) * sp)
            d_sp = jnp.sum(d_loga * ((-LRU_C) * r), axis=0, keepdims=True)
            dlam = -d_sp * _sigmoid(-lam)
            d_pr = d_r * r * (1.0 - r)
            d_pi = d_ig * ig * (1.0 - ig)
            d_prb, d_pib = d_pr.astype(BF16), d_pi.astype(BF16)
            d_xc = d_xc + _dot(d_prb, wr, NT) + _dot(d_pib, wi, NT)

            @pl.when(b == 0)
            def _():
                for ref in (dcw_ref, dcb_ref, dwr_ref, dwi_ref, dbr_ref, dbi_ref, dlam_ref):
                    ref[...] = jnp.zeros_like(ref)

            dx = cw[LRU_CONV - 1:LRU_CONV, :] * d_xc
            for j in range(LRU_CONV - 1):
                sft = LRU_CONV - 1 - j
                dx = dx + cw[j:j + 1, :] * _shift_up(d_xc, sft, 0.0)
                dcw_ref[j:j + 1, :] += jnp.sum(d_xc * _shift_down(x, sft, 0.0), axis=0, keepdims=True)
            dcw_ref[LRU_CONV - 1:LRU_CONV, :] += jnp.sum(d_xc * x, axis=0, keepdims=True)
            dx_s[slot] = dx.astype(BF16)
            dcb_ref[...] += jnp.sum(d_xc, axis=0, keepdims=True)
            dwr_ref[...] += _dot(xcb, d_prb, TN)
            dwi_ref[...] += _dot(xcb, d_pib, TN)
            dbr_ref[...] += jnp.sum(d_pr, axis=0, keepdims=True)
            dbi_ref[...] += jnp.sum(d_pi, axis=0, keepdims=True)
            dlam_ref[...] += dlam

        cols = [(COL_XL + n) * LRU_BLOCK, (COL_YL + n) * LRU_BLOCK]
        _write_pieces(dp_ref, wsem, [dx_s, dy_s], b * S, cols, [n, b], grid, compute)

    blk = lambda off: pl.BlockSpec((S, LRU_BLOCK), lambda n, b: (b, off + n))
    vec = lambda rows: pl.BlockSpec((rows, LRU_BLOCK), lambda n, b: (0, n))
    wspec = pl.BlockSpec((None, LRU_BLOCK, LRU_BLOCK), lambda n, b: (n, 0, 0))
    vshape = lambda rows: jax.ShapeDtypeStruct((rows, W), F32)
    wshape = jax.ShapeDtypeStruct((LRU_BLOCKS, LRU_BLOCK, LRU_BLOCK), F32)
    return _pcall(
        body, name=name, grid=grid,
        in_specs=[blk(0), blk(0), blk(COL_XL), blk(COL_YL), vec(LRU_CONV), vec(1), wspec, wspec, vec(1), vec(1),
                  vec(1), HBM_SPEC],
        out_specs=[HBM_SPEC, vec(LRU_CONV), vec(1), wspec, wspec, vec(1), vec(1), vec(1)],
        out_shape=[jax.ShapeDtypeStruct(dproj.shape, dproj.dtype),
                   vshape(LRU_CONV), vshape(1), wshape, wshape, vshape(1), vshape(1), vshape(1)],
        args=[db_in, h, proj, proj, cw, cb, wr, wi, br, bi, lam, dproj],
        scratch=[pltpu.VMEM((2, S, LRU_BLOCK), BF16), pltpu.VMEM((2, S, LRU_BLOCK), BF16), pltpu.SemaphoreType.DMA((2, 2)),
                 pltpu.VMEM((S, LRU_BLOCK), F32), pltpu.VMEM((S, LRU_BLOCK), F32), pltpu.VMEM((S, LRU_BLOCK), F32)],
        sem=("arbitrary", "arbitrary"), jobs=jobs, alias_in_out={11: 0})


FFN_CT = 256


def _ffn_conv(gate, cw, cb):
    gc = cb + cw[FFN_CONV - 1:FFN_CONV, :] * gate
    for j in range(FFN_CONV - 1):
        gc = gc + cw[j:j + 1, :] * _shift_down(gate, FFN_CONV - 1 - j, 0.0)
    return gc


def _ffn_act_fwd(up, cw, cb, B, S, name):
    T = B * S
    nct = D_FF // FFN_CT

    def body(g_ref, v_ref, cw_ref, cb_ref, f_ref):
        gc = _ffn_conv(g_ref[...].astype(F32), cw_ref[...], cb_ref[...])
        f_ref[...] = (_gelu(gc) * v_ref[...].astype(F32)).astype(BF16)

    return _pcall(
        body, name=name, grid=(B, nct),
        in_specs=[pl.BlockSpec((S, FFN_CT), lambda b, c: (b, c)), pl.BlockSpec((S, FFN_CT), lambda b, c: (b, nct + c)),
                  pl.BlockSpec((FFN_CONV, FFN_CT), lambda b, c: (0, c)), pl.BlockSpec((1, FFN_CT), lambda b, c: (0, c))],
        out_specs=[pl.BlockSpec((S, FFN_CT), lambda b, c: (b, c))],
        out_shape=[jax.ShapeDtypeStruct((T, D_FF), BF16)], args=[up, up, cw, cb], sem=("parallel", "parallel"))[0]


def _ffn_act_bwd(df, up, cw, cb, B, S, name, jobs=()):
    T = B * S
    nct = D_FF // FFN_CT

    grid = (nct, B)

    def body(df_ref, g_ref, v_ref, cw_ref, cb_ref, du_ref, dcw_ref, dcb_ref, dg_s, dv_s, wsem):
        c, b = pl.program_id(0), pl.program_id(1)

        def compute(slot):
            gate = g_ref[...].astype(F32)
            cw = cw_ref[...]
            gc = _ffn_conv(gate, cw, cb_ref[...])
            gel, dgel = _gelu_and_grad(gc)
            dfv = df_ref[...].astype(F32)
            dv_s[slot] = (dfv * gel).astype(BF16)
            dgc = dfv * v_ref[...].astype(F32) * dgel

            @pl.when(b == 0)
            def _():
                dcw_ref[...] = jnp.zeros_like(dcw_ref)
                dcb_ref[...] = jnp.zeros_like(dcb_ref)

            dgate = cw[FFN_CONV - 1:FFN_CONV, :] * dgc
            for j in range(FFN_CONV - 1):
                sft = FFN_CONV - 1 - j
                dgate = dgate + cw[j:j + 1, :] * _shift_up(dgc, sft, 0.0)
                dcw_ref[j:j + 1, :] += jnp.sum(dgc * _shift_down(gate, sft, 0.0), axis=0, keepdims=True)
            dcw_ref[FFN_CONV - 1:FFN_CONV, :] += jnp.sum(dgc * gate, axis=0, keepdims=True)
            dg_s[slot] = dgate.astype(BF16)
            dcb_ref[...] += jnp.sum(dgc, axis=0, keepdims=True)

        _write_pieces(du_ref, wsem, [dg_s, dv_s], b * S, [c * FFN_CT, (nct + c) * FFN_CT], [c, b], grid, compute)

    blk = pl.BlockSpec((S, FFN_CT), lambda c, b: (b, c))
    return _pcall(
        body, name=name, grid=grid,
        in_specs=[blk, blk, pl.BlockSpec((S, FFN_CT), lambda c, b: (b, nct + c)),
                  pl.BlockSpec((FFN_CONV, FFN_CT), lambda c, b: (0, c)),
                  pl.BlockSpec((1, FFN_CT), lambda c, b: (0, c))],
        out_specs=[HBM_SPEC, pl.BlockSpec((FFN_CONV, FFN_CT), lambda c, b: (0, c)),
                   pl.BlockSpec((1, FFN_CT), lambda c, b: (0, c))],
        out_shape=[jax.ShapeDtypeStruct((T, 2 * D_FF), BF16),
                   jax.ShapeDtypeStruct((FFN_CONV, D_FF), F32), jax.ShapeDtypeStruct((1, D_FF), F32)],
        args=[df, up, up, cw, cb],
        scratch=[pltpu.VMEM((2, S, FFN_CT), BF16), pltpu.VMEM((2, S, FFN_CT), BF16), pltpu.SemaphoreType.DMA((2, 2))],
        sem=("arbitrary", "arbitrary"), jobs=jobs)


def _rs_add(g, recv, mode, core, name, also_bf16=False):
    shard = tuple(recv.shape[1:])
    if mode == "mid":
        a, e, c2 = shard
        g_in = g.reshape(a, N_DEV, e, c2)
        grid = (4, 1)
        g_spec = pl.BlockSpec((a, None, e, c2), lambda k, i, c_ref: (0, 2 * k + c_ref[0], 0, 0))
        r_spec = pl.BlockSpec((None, a, e, c2), lambda k, i, c_ref: (k, 0, 0, 0))
    else:
        R, C = shard
        tr = _row_tile(R, 512)
        grid = (4, R // tr)
        if mode == "rows":
            g_in = g.reshape(N_DEV, R, C)
            g_spec = pl.BlockSpec((None, tr, C), lambda k, i, c_ref: (2 * k + c_ref[0], i, 0))
        else:
            g_in = g
            g_spec = pl.BlockSpec((tr, C), lambda k, i, c_ref: (i, 2 * k + c_ref[0]))
        r_spec = pl.BlockSpec((None, tr, C), lambda k, i, c_ref: (k, i, 0))

    def body(c_ref, g_ref, r_ref, o_ref, *ob_ref):
        s = g_ref[...] + r_ref[...]
        o_ref[...] = s
        if also_bf16:
            ob_ref[0][...] = s.astype(BF16)

    out_shape = jax.ShapeDtypeStruct(recv.shape, recv.dtype)
    return pl.pallas_call(
        body, name=name,
        grid_spec=pltpu.PrefetchScalarGridSpec(
            num_scalar_prefetch=1, grid=grid, in_specs=[g_spec, r_spec],
            out_specs=[r_spec, r_spec] if also_bf16 else r_spec),
        out_shape=[out_shape, jax.ShapeDtypeStruct(recv.shape, BF16)] if also_bf16 else out_shape,
        compiler_params=pltpu.CompilerParams(dimension_semantics=("parallel", "parallel"),
                                             vmem_limit_bytes=VMEM_LIMIT),
    )(core, g_in, recv)


def _adam_update(gv, w, m, v):
    nm = ADAM_B1 * m + (1.0 - ADAM_B1) * gv
    nv = ADAM_B2 * v + (1.0 - ADAM_B2) * (gv * gv)
    m_hat = nm / (1.0 - ADAM_B1 ** ADAM_STEP)
    v_hat = nv / (1.0 - ADAM_B2 ** ADAM_STEP)
    delta = -ADAM_LR * (m_hat / (jnp.sqrt(v_hat) + ADAM_EPS) + ADAM_WD * w)
    return delta, nm, nv


def _adamw_shard(partial, recv, w, m, v, chip, name):
    shape = tuple(w.shape)
    tr = _row_tile(shape[0], 256)
    rest = shape[1:]
    zeros = (0,) * len(rest)
    tile = pl.BlockSpec((tr,) + rest, lambda i, s: (i,) + zeros)

    def body(_, p_ref, r_ref, w_ref, m_ref, v_ref, g_ref, d_ref, nm_ref, nv_ref):
        gv = p_ref[...] + r_ref[0].astype(F32) + r_ref[1].astype(F32) + r_ref[2].astype(F32)
        g_ref[...] = gv
        d_ref[...], nm_ref[...], nv_ref[...] = _adam_update(gv, w_ref[...], m_ref[...], v_ref[...])

    grid_spec = pltpu.PrefetchScalarGridSpec(
        num_scalar_prefetch=1, grid=(shape[0] // tr,),
        in_specs=[pl.BlockSpec((None, tr) + rest, lambda i, s: (s[0], i) + zeros),
                  pl.BlockSpec((3, tr) + rest, lambda i, s: (0, i) + zeros), tile, tile, tile],
        out_specs=[tile] * 4)
    return pl.pallas_call(
        body, name=name, grid_spec=grid_spec, out_shape=[jax.ShapeDtypeStruct(shape, F32)] * 4,
        compiler_params=pltpu.CompilerParams(dimension_semantics=("parallel",), vmem_limit_bytes=VMEM_LIMIT),
    )(chip, partial, recv, w, m, v)


def _all_gather_multi(shards, modes, name):
    n = len(shards)
    extents = [_extent(s.shape, m) for s, m in zip(shards, modes)]

    def body(*refs):
        x_refs, out_refs = refs[:n], refs[n:2 * n]
        send_sems, recv_sems, local_sems = refs[2 * n:]
        x, y, c = _mesh_pos()
        me, sibling = (x, y, c), (x, y, 1 - c)
        chips = _other_chips(x, y)

        def slot(i, px, py, pc):
            return _window(out_refs[i], modes[i], extents[i], 4 * px + 2 * py + pc)

        def copy(i, k, block, to, src=None):
            return _remote(slot(i, *block) if src is None else src, slot(i, *block),
                           send_sems.at[i, k], recv_sems.at[i, k], to)

        mine = [pltpu.make_async_copy(x_refs[i], slot(i, *me), local_sems.at[i]) for i in range(n)]
        sends = []
        for i in range(n):
            mine[i].start()
            first = [copy(i, 0, me, sibling, src=x_refs[i])]
            first += [copy(i, 1 + j, me, (*chip, c), src=x_refs[i]) for j, chip in enumerate(chips)]
            for cp in first:
                cp.start()
            sends += first
        for i in range(n):
            for j, chip in enumerate(chips):
                copy(i, 1 + j, (*chip, c), me).wait_recv()
                fwd = copy(i, 4 + j, (*chip, c), sibling)
                fwd.start()
                sends.append(fwd)
        for i in range(n):
            copy(i, 0, sibling, me).wait_recv()
            for j, chip in enumerate(chips):
                copy(i, 4 + j, (*chip, 1 - c), me).wait_recv()
        for cp in sends:
            cp.wait_send()
        for cp in mine:
            cp.wait()

    return pl.pallas_call(
        body, name=name,
        in_specs=[HBM_SPEC] * n, out_specs=[HBM_SPEC] * n,
        out_shape=[jax.ShapeDtypeStruct(_full_shape(s.shape, m), s.dtype) for s, m in zip(shards, modes)],
        scratch_shapes=[pltpu.SemaphoreType.DMA((n, 7)), pltpu.SemaphoreType.DMA((n, 7)),
                        pltpu.SemaphoreType.DMA((n,))],
    )(*shards)


SMALL_LANES = 1024


def _small_rows(shape):
    r, w = shape
    return r * max(1, w // SMALL_LANES)


def _small_allreduce(parts, name):
    n = len(parts)
    shapes = [tuple(p.shape) for p in parts]
    offs, total = [], 0
    for s in shapes:
        offs.append(total)
        total += _small_rows(s)
    rows = -(-total // 8) * 8

    def body(*refs):
        p_refs, o_refs = refs[:n], refs[n:2 * n]
        buf, tot, send_sems, recv_sems = refs[2 * n:]
        x, y, c = _mesh_pos()
        me, sibling = (x, y, c), (x, y, 1 - c)
        chips = _other_chips(x, y)

        def slot(px, py, pc):
            return buf.at[4 * px + 2 * py + pc]

        def copy(k, block, to):
            return _remote(slot(*block), slot(*block), send_sems.at[k], recv_sems.at[k], to)

        tot[...] = jnp.zeros_like(tot)
        for p_ref, (r, w), off in zip(p_refs, shapes, offs):
            wl = min(w, SMALL_LANES)
            for part in range(max(1, w // SMALL_LANES)):
                tot[pl.ds(off + part * r, r), pl.ds(0, wl)] = p_ref[:, pl.ds(part * SMALL_LANES, wl)]
        buf[4 * x + 2 * y + c] = tot[...]
        first = [copy(0, me, sibling)] + [copy(1 + j, me, (*chip, c)) for j, chip in enumerate(chips)]
        for cp in first:
            cp.start()
        passed = [copy(4 + j, (*chip, c), sibling) for j, chip in enumerate(chips)]
        for j, chip in enumerate(chips):
            copy(1 + j, (*chip, c), me).wait_recv()
            passed[j].start()
        copy(0, sibling, me).wait_recv()
        for j, chip in enumerate(chips):
            copy(4 + j, (*chip, 1 - c), me).wait_recv()
        for cp in first + passed:
            cp.wait_send()
        acc = buf[0]
        for d in range(1, N_DEV):
            acc = acc + buf[d]
        tot[...] = acc
        for o_ref, (r, w), off in zip(o_refs, shapes, offs):
            wl = min(w, SMALL_LANES)
            for part in range(max(1, w // SMALL_LANES)):
                o_ref[:, pl.ds(part * SMALL_LANES, wl)] = tot[pl.ds(off + part * r, r), pl.ds(0, wl)]

    vm = pl.BlockSpec(memory_space=pltpu.VMEM)
    return pl.pallas_call(
        body, name=name,
        in_specs=[vm] * n, out_specs=[vm] * n,
        out_shape=[jax.ShapeDtypeStruct(s, F32) for s in shapes],
        scratch_shapes=[pltpu.VMEM((N_DEV, rows, SMALL_LANES), F32), pltpu.VMEM((rows, SMALL_LANES), F32),
                        pltpu.SemaphoreType.DMA((7,)), pltpu.SemaphoreType.DMA((7,))],
    )(*parts)


def _adamw_small(gs, ws, ms, vs, name):
    n = len(gs)

    def body(*refs):
        g_r, w_r, m_r, v_r = refs[:n], refs[n:2 * n], refs[2 * n:3 * n], refs[3 * n:4 * n]
        d_r, nm_r, nv_r = refs[4 * n:5 * n], refs[5 * n:6 * n], refs[6 * n:7 * n]
        for i in range(n):
            d_r[i][...], nm_r[i][...], nv_r[i][...] = _adam_update(g_r[i][...], w_r[i][...], m_r[i][...], v_r[i][...])

    vm = pl.BlockSpec(memory_space=pltpu.VMEM)
    shapes = [jax.ShapeDtypeStruct(w.shape, F32) for w in ws]
    outs = pl.pallas_call(body, name=name, in_specs=[vm] * (4 * n), out_specs=[vm] * (3 * n),
                          out_shape=shapes * 3)(*gs, *ws, *ms, *vs)
    return outs[:n], outs[n:2 * n], outs[2 * n:]


FIRST = [("w_in", (1024, 896), "cols"), ("lru_w_r", (4, 32, 256), "mid"), ("lru_w_i", (4, 32, 256), "mid")]
LATE = [("w_ret_o", (128, 1024), "rows"), ("w_lru_o", (128, 1024), "rows"), ("w_out", (128, 1024), "rows"),
        ("ffn_w_up", (1024, 768), "cols"), ("ffn_w_down", (384, 1024), "rows")]
BIG = FIRST + LATE
SMALL_SHARDED = [("merge_gate_b", (2, 128), "cols"), ("lru_conv_w", (4, 128), "cols"), ("lru_b_r", (4, 32), "stack"),
                 ("lru_b_i", (4, 32), "stack"), ("ffn_conv_w", (3, 384), "cols")]
REPLICATED = [("norm1_w", (1, 1024)), ("ret_gn_w", (1, 1024)), ("lru_conv_b", (1, 1024)), ("lru_lambda", (1, 1024)),
              ("norm2_w", (1, 1024)), ("ffn_conv_b", (1, 3072)), ("norm_f_w", (1, 1024))]
MODE = {n: m for n, _, m in BIG}
SHARD = {n: s for n, s, _ in BIG}


def _local_step(x3, positions, target3, wb, ws, late_shards, core):
    B, S, D = x3.shape
    T = B * S
    x = x3.reshape(T, D)
    target = target3.reshape(T, D)
    tm = min(512, T)
    big = min(1024, T)
    big2 = min(2048, T)

    half = RET_DK // 2
    inv_freq = ROPE_BASE ** (-jnp.arange(half, dtype=F32) / half)
    inv2 = jnp.concatenate([inv_freq, inv_freq]).reshape(1, RET_DK)
    log_gamma = jnp.log1p(-jnp.power(2.0, -5.0 - jnp.arange(RET_HEADS, dtype=F32)))
    lgam = jnp.broadcast_to(log_gamma[:, None, None], (RET_HEADS, 8, LANES))
    pos_col = positions.astype(F32).reshape(T, 1)
    cos2, sin2s = _rope_tables(pos_col, inv2, tm, "rope_tables")

    late_names = [n for n, _, _ in LATE]
    late_modes = [m for _, _, m in LATE]
    late_shapes = [s for _, s, _ in LATE]

    h1 = _rmsnorm_fwd(x, ws["norm1_w"], tm, "norm1_fwd")
    proj, *late_part = _matmul(h1, wb["w_in"], "nn", BF16, big2, 1024, 1024, "proj_fwd",
                               jobs=[_ag_first_job(late_shards, late_modes)])
    o, a_in, *late_full = _retention_fwd(proj, cos2, sin2s, lgam, ws["ret_gn_w"], B, S, "retention_fwd",
                                         jobs=[_ag_second_job(late_part, late_modes, late_shapes)])
    wb = dict(wb, **dict(zip(late_names, late_full)))
    hl, b_in = _lru_fwd(proj, ws["lru_conv_w"], ws["lru_conv_b"], wb["lru_w_r"], wb["lru_w_i"],
                        ws["lru_b_r"], ws["lru_b_i"], ws["lru_lambda"], B, S, "lru_fwd")
    x1, mix, h2, ya, yb = _mix_fwd(a_in, b_in, proj, x, wb["w_ret_o"], wb["w_lru_o"], wb["w_out"],
                                   ws["merge_gate_b"], ws["norm2_w"], tm, "mix_fwd")
    up = _matmul(h2, wb["ffn_w_up"], "nn", BF16, big2, 1024, 1024, "ffn_up_fwd")[0]
    f = _ffn_act_fwd(up, ws["ffn_conv_w"], ws["ffn_conv_b"], B, S, "ffn_act_fwd")
    x2 = _matmul(f, wb["ffn_w_down"], "nn", F32, big, 1024, D_FF, "ffn_down_fwd", add=x1)[0]
    dx2, dx2b, loss_acc, d_norm_f = _loss_head(x2, target, ws["norm_f_w"], tm, "loss_head")

    g, rs = {}, {}

    def stage1(names, grads):
        return _rs_sibling_job(grads, [MODE[n] for n in names], [SHARD[n] for n in names])

    def add(names, grads, recvs):
        return [_rs_add(gr, r, MODE[n], core, "rs_add_" + n) for n, gr, r in zip(names, grads, recvs)]

    g["norm_f_w"] = d_norm_f
    g_down = _matmul(f, dx2b, "tn", F32, 1024, 1024, big2, "ffn_down_bwd_w")[0]
    df, s1_down = _matmul(dx2b, wb["ffn_w_down"], "nt", BF16, big2, 1024, 1024, "ffn_down_bwd_x",
                          jobs=[stage1(["ffn_w_down"], [g_down])])
    p_down = add(["ffn_w_down"], [g_down], [s1_down])
    dup, g["ffn_conv_w"], g["ffn_conv_b"], s2_down = _ffn_act_bwd(
        df, up, ws["ffn_conv_w"], ws["ffn_conv_b"], B, S, "ffn_act_bwd", jobs=[_rs_chip_job(p_down)])
    rs["ffn_w_down"] = (p_down[0], s2_down)

    g_up = _matmul(h2, dup, "tn", F32, 1024, 1024, big2, "ffn_up_bwd_w")[0]
    dh2, s1_up = _matmul(dup, wb["ffn_w_up"], "nt", BF16, big, 1024, D_FF, "ffn_up_bwd_x",
                         jobs=[stage1(["ffn_w_up"], [g_up])])
    p_up = add(["ffn_w_up"], [g_up], [s1_up])
    dx1, dx1b, g["norm2_w"] = _rmsnorm_bwd_add(dx2, dh2, x1, ws["norm2_w"], tm, "norm2_bwd", True)
    da_in, db_in, dya, dyb, dproj, g["merge_gate_b"] = _mix_bwd(
        dx1b, ya, yb, proj, wb["w_ret_o"], wb["w_lru_o"], wb["w_out"], ws["merge_gate_b"], tm, "mix_bwd")

    mid_names = ["w_out", "w_ret_o", "w_lru_o"]
    g_mid = [_matmul(mix, dx1b, "tn", F32, 1024, 1024, big2, "w_out_bwd_w")[0],
             _matmul(a_in, dya, "tn", F32, 1024, 1024, big2, "w_ret_o_bwd_w")[0],
             _matmul(b_in, dyb, "tn", F32, 1024, 1024, big2, "w_lru_o_bwd_w")[0]]
    (dproj, g["lru_conv_w"], g["lru_conv_b"], g_wr, g_wi, g["lru_b_r"], g["lru_b_i"], g["lru_lambda"], s2_up,
     *s1_mid) = _lru_bwd(db_in, hl, proj, dproj, ws["lru_conv_w"], ws["lru_conv_b"], wb["lru_w_r"], wb["lru_w_i"],
                         ws["lru_b_r"], ws["lru_b_i"], ws["lru_lambda"], B, S, "lru_bwd",
                         jobs=[_rs_chip_job(p_up), stage1(mid_names, g_mid)])
    rs["ffn_w_up"] = (p_up[0], s2_up)
    p_mid = add(mid_names, g_mid, s1_mid)
    lru_names = ["lru_w_r", "lru_w_i"]
    dproj, g["ret_gn_w"], *rest = _retention_bwd(
        da_in, o, proj, dproj, cos2, sin2s, lgam, ws["ret_gn_w"], B, S, "retention_bwd",
        jobs=[_rs_chip_job(p_mid), stage1(lru_names, [g_wr, g_wi])])
    s2_mid, s1_lru = rest[:3], rest[3:]
    for n, p, r in zip(mid_names, p_mid, s2_mid):
        rs[n] = (p, r)
    p_lru = add(lru_names, [g_wr, g_wi], s1_lru)

    g_in, *s2_lru = _matmul(h1, dproj, "tn", F32, 1024, 1024, big2, "proj_bwd_w", jobs=[_rs_chip_job(p_lru)])
    for n, p, r in zip(lru_names, p_lru, s2_lru):
        rs[n] = (p, r)
    s1_in = _pcall(lambda: None, name="rs_sibling_w_in", grid=(1,), in_specs=[], out_specs=[], out_shape=[], args=[],
                   sem=("arbitrary",), jobs=[stage1(["w_in"], [g_in])])
    p_in, p_in_bf16 = _rs_add(g_in, s1_in[0], MODE["w_in"], core, "rs_add_w_in", also_bf16=True)
    dh1, s2_in = _matmul(dproj, wb["w_in"], "nt", BF16, big, 1024, D_IN // 2, "proj_bwd_x",
                         jobs=[_rs_chip_job([p_in_bf16])])
    grad_x, g["norm1_w"] = _rmsnorm_bwd_add(dx1, dh1, x, ws["norm1_w"], tm, "norm1_bwd", False)
    rs["w_in"] = (p_in, s2_in)
    return loss_acc, grad_x.reshape(B, S, D), g, rs


def kernel(x, positions, norm1_w, w_in, merge_gate_b, ret_gn_w, w_ret_o, lru_conv_w, lru_conv_b, lru_w_r, lru_b_r, lru_w_i, lru_b_i, lru_lambda, w_lru_o, w_out, norm2_w, ffn_w_up, ffn_conv_w, ffn_conv_b, ffn_w_down, norm_f_w, loss_target, m_norm1_w, m_w_in, m_merge_gate_b, m_ret_gn_w, m_w_ret_o, m_lru_conv_w, m_lru_conv_b, m_lru_w_r, m_lru_b_r, m_lru_w_i, m_lru_b_i, m_lru_lambda, m_w_lru_o, m_w_out, m_norm2_w, m_ffn_w_up, m_ffn_conv_w, m_ffn_conv_b, m_ffn_w_down, m_norm_f_w, v_norm1_w, v_w_in, v_merge_gate_b, v_ret_gn_w, v_w_ret_o, v_lru_conv_w, v_lru_conv_b, v_lru_w_r, v_lru_b_r, v_lru_w_i, v_lru_b_i, v_lru_lambda, v_w_lru_o, v_w_out, v_norm2_w, v_ffn_w_up, v_ffn_conv_w, v_ffn_conv_b, v_ffn_w_down, v_norm_f_w):
    names = ["norm1_w", "w_in", "merge_gate_b", "ret_gn_w", "w_ret_o", "lru_conv_w", "lru_conv_b", "lru_w_r", "lru_b_r",
             "lru_w_i", "lru_b_i", "lru_lambda", "w_lru_o", "w_out", "norm2_w", "ffn_w_up", "ffn_conv_w", "ffn_conv_b",
             "ffn_w_down", "norm_f_w"]
    w_args = [norm1_w, w_in, merge_gate_b, ret_gn_w, w_ret_o, lru_conv_w, lru_conv_b, lru_w_r, lru_b_r, lru_w_i, lru_b_i,
              lru_lambda, w_lru_o, w_out, norm2_w, ffn_w_up, ffn_conv_w, ffn_conv_b, ffn_w_down, norm_f_w]
    m_args = [m_norm1_w, m_w_in, m_merge_gate_b, m_ret_gn_w, m_w_ret_o, m_lru_conv_w, m_lru_conv_b, m_lru_w_r, m_lru_b_r,
              m_lru_w_i, m_lru_b_i, m_lru_lambda, m_w_lru_o, m_w_out, m_norm2_w, m_ffn_w_up, m_ffn_conv_w, m_ffn_conv_b,
              m_ffn_w_down, m_norm_f_w]
    v_args = [v_norm1_w, v_w_in, v_merge_gate_b, v_ret_gn_w, v_w_ret_o, v_lru_conv_w, v_lru_conv_b, v_lru_w_r, v_lru_b_r,
              v_lru_w_i, v_lru_b_i, v_lru_lambda, v_w_lru_o, v_w_out, v_norm2_w, v_ffn_w_up, v_ffn_conv_w, v_ffn_conv_b,
              v_ffn_w_down, v_norm_f_w]
    orig_shape = {n: a.shape for n, a in zip(names, w_args)}
    local_shape = {n: s for n, s, _ in BIG + SMALL_SHARDED}
    local_shape.update({n: s for n, s in REPLICATED})
    W = {n: a.reshape(local_shape[n]) for n, a in zip(names, w_args)}
    M = {n: a.reshape(local_shape[n]) for n, a in zip(names, m_args)}
    V = {n: a.reshape(local_shape[n]) for n, a in zip(names, v_args)}

    xi, yi, ci = _mesh_pos()
    dev = 4 * xi + 2 * yi + ci
    chip = (2 * xi + yi).astype(jnp.int32).reshape(1)
    core = ci.astype(jnp.int32).reshape(1)

    first_names = [n for n, _, _ in FIRST]
    small_names = [n for n, _, _ in SMALL_SHARDED]
    gathered = _all_gather_multi([W[n].astype(BF16) for n in first_names] + [W[n] for n in small_names],
                                 [m for _, _, m in FIRST + SMALL_SHARDED], "gather_first_weights")
    wb = dict(zip(first_names, gathered[:len(FIRST)]))
    ws = dict(zip(small_names, gathered[len(FIRST):]))
    for n in ("lru_b_r", "lru_b_i"):
        ws[n] = jnp.transpose(ws[n], (1, 0, 2)).reshape(1, LRU_BLOCKS * LRU_BLOCK)
    for n, _ in REPLICATED:
        ws[n] = W[n]

    late_shards = [W[n].astype(BF16) for n, _, _ in LATE]
    loss_acc, grad_x, g, rs = _local_step(x, positions, loss_target, wb, ws, late_shards, core)

    G_out, D_out, M_out, V_out = {}, {}, {}, {}
    for n, _, _ in BIG:
        G_out[n], D_out[n], M_out[n], V_out[n] = _adamw_shard(rs[n][0], rs[n][1], W[n], M[n], V[n], chip, "adamw_" + n)

    rep_names = [n for n, _ in REPLICATED]
    red_names = rep_names + small_names
    red = _small_allreduce([g[n] for n in red_names] + [loss_acc[0:1, :]], "allreduce_small_grads")
    loss = red[-1][0, 0]
    gs = dict(zip(red_names, red[:-1]))
    for n, s, mode in SMALL_SHARDED:
        if mode == "cols":
            gs[n] = lax.dynamic_slice_in_dim(gs[n], dev * s[1], s[1], axis=1)
        else:
            full = gs[n].reshape(LRU_BLOCKS, LRU_BLOCK)
            gs[n] = lax.dynamic_slice_in_dim(full, dev * s[1], s[1], axis=1)
    d2, m2, v2 = _adamw_small([gs[n] for n in red_names], [W[n] for n in red_names], [M[n] for n in red_names],
                              [V[n] for n in red_names], "adamw_small")
    for i, n in enumerate(red_names):
        G_out[n], D_out[n], M_out[n], V_out[n] = gs[n], d2[i], m2[i], v2[i]

    outs = [loss, grad_x]
    for group in (G_out, D_out, M_out, V_out):
        outs += [group[n].reshape(orig_shape[n]) for n in names]
    return tuple(outs)
```

```python
import math

import jax
import jax.numpy as jnp
from jax import lax
from jax.experimental import pallas as pl
from jax.experimental.pallas import tpu as pltpu

F32 = jnp.float32
BF16 = jnp.bfloat16
MESH = pl.DeviceIdType.MESH

D_MODEL = 1024
CHUNK = 64
RET_HEADS = 4
RET_DK = 128
RET_DV = 256
LRU_BLOCKS = 4
LRU_BLOCK = 256
LRU_CONV = 4
LRU_C = 8.0
D_FF = 3072
FFN_CONV = 3
ROPE_BASE = 10000.0
RMS_EPS = 1e-6
GN_EPS = 1e-6
D_IN = 7168
ADAM_LR, ADAM_B1, ADAM_B2, ADAM_EPS, ADAM_WD, ADAM_STEP = 0.001, 0.9, 0.999, 1e-08, 0.01, 10

N_DEV = 8
V7X_VMEM_BYTES = 64 * 1024 * 1024
VMEM_LIMIT = V7X_VMEM_BYTES - 8 * 1024 * 1024
RET_BLOCK = 256
LANES = 128

COL_Q, COL_K = 0, 4
COL_V, COL_G, COL_XL, COL_YL = 4, 8, 12, 16
COL_GR, COL_GL = 5, 6

HBM_SPEC = pl.BlockSpec(memory_space=pl.ANY)


def _gelu(x):
    c = math.sqrt(2.0 / math.pi)
    t = jnp.tanh(x * (c + (c * 0.044715) * (x * x)))
    return x * (0.5 * t + 0.5)


def _gelu_and_grad(x):
    c = math.sqrt(2.0 / math.pi)
    x2 = x * x
    t = jnp.tanh(x * (c + (c * 0.044715) * x2))
    h = 0.5 * t + 0.5
    g = x * h
    dg = h + g * (1.0 - h) * ((2.0 * c) + (6.0 * c * 0.044715) * x2)
    return g, dg


def _sigmoid(x):
    return 1.0 / (1.0 + jnp.exp(-x))


SUBLANES = 8


def _shift_down(x, s, fill):
    r = pltpu.roll(x, s, 0)
    rows = lax.broadcasted_iota(jnp.int32, (SUBLANES,) + x.shape[1:], 0)
    top = jnp.where(rows >= s, r[:SUBLANES], fill)
    return jnp.concatenate([top, r[SUBLANES:]], axis=0)


def _shift_up(x, s, fill):
    n = x.shape[0]
    r = pltpu.roll(x, n - s, 0)
    rows = lax.broadcasted_iota(jnp.int32, (SUBLANES,) + x.shape[1:], 0)
    bottom = jnp.where(rows < SUBLANES - s, r[n - SUBLANES:], fill)
    return jnp.concatenate([r[:n - SUBLANES], bottom], axis=0)


SCAN_CHUNK = 64


def _scan_forward(a, b):
    n = a.shape[0]
    s = 1
    while s < n:
        if s % SUBLANES:
            b = a * _shift_down(b, s, 0.0) + b
            a = a * _shift_down(a, s, 1.0)
        else:
            b = jnp.concatenate([b[:s], a[s:] * b[:n - s] + b[s:]], axis=0)
            a = jnp.concatenate([a[:s], a[s:] * a[:n - s]], axis=0)
        s *= 2
    return a, b


def _scan_backward(a_next, u):
    n = u.shape[0]
    s = 1
    while s < n:
        if s % SUBLANES:
            u = u + a_next * _shift_up(u, s, 0.0)
            a_next = a_next * _shift_up(a_next, s, 1.0)
        else:
            u = jnp.concatenate([u[:n - s] + a_next[:n - s] * u[s:], u[n - s:]], axis=0)
            a_next = jnp.concatenate([a_next[:n - s] * a_next[s:], a_next[n - s:]], axis=0)
        s *= 2
    return a_next, u


def _scan_forward_ref(a_ref, b_ref, h_ref):
    S, W = a_ref.shape
    for strip in range(W // LANES):
        cols = pl.ds(strip * LANES, LANES)

        def body(k, carry, cols=cols):
            rows = pl.ds(pl.multiple_of(k * SCAN_CHUNK, SCAN_CHUNK), SCAN_CHUNK)
            a_cum, h_loc = _scan_forward(a_ref[rows, cols], b_ref[rows, cols])
            h = h_loc + a_cum * carry
            h_ref[rows, cols] = h
            return h[SCAN_CHUNK - 1:, :]

        lax.fori_loop(0, S // SCAN_CHUNK, body, jnp.zeros((1, LANES), F32))


def _scan_backward_ref(an_ref, u_ref, d_ref):
    S, W = an_ref.shape
    n_chunks = S // SCAN_CHUNK
    for strip in range(W // LANES):
        cols = pl.ds(strip * LANES, LANES)

        def body(i, carry, cols=cols):
            rows = pl.ds(pl.multiple_of((n_chunks - 1 - i) * SCAN_CHUNK, SCAN_CHUNK), SCAN_CHUNK)
            an_cum, d_loc = _scan_backward(an_ref[rows, cols], u_ref[rows, cols])
            d = d_loc + an_cum * carry
            d_ref[rows, cols] = d
            return d[:1, :]

        lax.fori_loop(0, n_chunks, body, jnp.zeros((1, LANES), F32))


def _dot(a, b, dims):
    return lax.dot_general(a, b, (dims, ((), ())), preferred_element_type=F32)


NN = ((1,), (0,))
NT = ((1,), (1,))
TN = ((0,), (0,))


def _mesh_pos():
    return lax.axis_index("x"), lax.axis_index("y"), lax.axis_index("c")


def _other_chips(x, y):
    return [(1 - x, y), (x, 1 - y), (1 - x, 1 - y)]


def _full_shape(shard_shape, mode):
    if mode == "rows":
        return (N_DEV * shard_shape[0],) + tuple(shard_shape[1:])
    if mode == "cols":
        return (shard_shape[0], N_DEV * shard_shape[1])
    if mode == "mid":
        return (shard_shape[0], N_DEV * shard_shape[1], shard_shape[2])
    return (N_DEV,) + tuple(shard_shape)


def _extent(shard_shape, mode):
    return {"rows": shard_shape[0], "cols": shard_shape[1], "mid": shard_shape[1], "stack": 1}[mode]


def _window(ref, mode, extent, d):
    if mode == "stack":
        return ref.at[d]
    start = pl.multiple_of(d * extent, extent)
    if mode == "rows":
        return ref.at[pl.ds(start, extent)]
    if mode == "cols":
        return ref.at[:, pl.ds(start, extent)]
    return ref.at[:, pl.ds(start, extent), :]


class _Job:
    def __init__(self, inputs, out_shapes, sems, start, finish, aliases=None):
        self.inputs, self.out_shapes, self.sems = list(inputs), list(out_shapes), sems
        self.start, self.finish, self.aliases = start, finish, dict(aliases or {})


def _remote(src, dst, send_sem, recv_sem, to):
    return pltpu.make_async_remote_copy(src_ref=src, dst_ref=dst, send_sem=send_sem, recv_sem=recv_sem,
                                        device_id=to, device_id_type=MESH)


def _ag_first_job(shards, modes):
    n = len(shards)
    extents = [_extent(s.shape, m) for s, m in zip(shards, modes)]

    def copies(x_refs, out_refs, send, recv, local, arriving):
        x, y, c = _mesh_pos()
        peers = [(x, y, 1 - c)] + [(*chip, c) for chip in _other_chips(x, y)]
        win = lambda i, p: _window(out_refs[i], modes[i], extents[i], 4 * p[0] + 2 * p[1] + p[2])
        if arriving:
            return [_remote(x_refs[i], win(i, p), send.at[i, k], recv.at[i, k], p)
                    for i in range(n) for k, p in enumerate(peers)]
        mine = [pltpu.make_async_copy(x_refs[i], win(i, (x, y, c)), local.at[i]) for i in range(n)]
        sends = [_remote(x_refs[i], win(i, (x, y, c)), send.at[i, k], recv.at[i, k], p)
                 for i in range(n) for k, p in enumerate(peers)]
        return mine, sends

    def start(*refs):
        mine, sends = copies(*refs, False)
        for cp in mine + sends:
            cp.start()

    def finish(*refs):
        for cp in copies(*refs, True):
            cp.wait_recv()
        mine, sends = copies(*refs, False)
        for cp in sends:
            cp.wait_send()
        for cp in mine:
            cp.wait()

    out_shapes = [jax.ShapeDtypeStruct(_full_shape(s.shape, m), s.dtype) for s, m in zip(shards, modes)]
    return _Job(shards, out_shapes, ((n, 4), (n, 4), (n,)), start, finish)


def _ag_second_job(fulls, modes, shard_shapes):
    n = len(fulls)
    extents = [_extent(s, m) for s, m in zip(shard_shapes, modes)]

    def copies(_, out_refs, send, recv, local, core_of_block):
        x, y, c = _mesh_pos()
        pc = c if core_of_block == "mine" else 1 - c
        win = lambda i, chip: _window(out_refs[i], modes[i], extents[i], 4 * chip[0] + 2 * chip[1] + pc)
        return [_remote(win(i, chip), win(i, chip), send.at[i, j], recv.at[i, j], (x, y, 1 - c))
                for i in range(n) for j, chip in enumerate(_other_chips(x, y))]

    def start(*refs):
        for cp in copies(*refs, "mine"):
            cp.start()

    def finish(*refs):
        for cp in copies(*refs, "sibling"):
            cp.wait_recv()
        for cp in copies(*refs, "mine"):
            cp.wait_send()

    out_shapes = [jax.ShapeDtypeStruct(f.shape, f.dtype) for f in fulls]
    return _Job(fulls, out_shapes, ((n, 3), (n, 3), (1,)), start, finish, aliases={i: i for i in range(n)})


def _rs_sibling_job(grads, modes, shard_shapes):
    n = len(grads)
    extents = [_extent(s, m) for s, m in zip(shard_shapes, modes)]

    def copies(g_refs, out_refs, send, recv, local):
        x, y, c = _mesh_pos()
        return [_remote(_window(g_refs[i], modes[i], extents[i], 2 * k + (1 - c)), out_refs[i].at[k],
                        send.at[i, k], recv.at[i, k], (x, y, 1 - c))
                for i in range(n) for k in range(4)]

    def start(*refs):
        for cp in copies(*refs):
            cp.start()

    def finish(*refs):
        cps = copies(*refs)
        for cp in cps:
            cp.wait_recv()
        for cp in cps:
            cp.wait_send()

    out_shapes = [jax.ShapeDtypeStruct((4,) + tuple(s), g.dtype) for s, g in zip(shard_shapes, grads)]
    return _Job(grads, out_shapes, ((n, 4), (n, 4), (1,)), start, finish)


def _rs_chip_job(partials):
    n = len(partials)

    def copies(p_refs, out_refs, send, recv, local):
        x, y, c = _mesh_pos()
        return [_remote(p_refs[i].at[2 * px + py], out_refs[i].at[j], send.at[i, j], recv.at[i, j], (px, py, c))
                for i in range(n) for j, (px, py) in enumerate(_other_chips(x, y))]

    def start(*refs):
        for cp in copies(*refs):
            cp.start()

    def finish(*refs):
        cps = copies(*refs)
        for cp in cps:
            cp.wait_recv()
        for cp in cps:
            cp.wait_send()

    out_shapes = [jax.ShapeDtypeStruct((3,) + tuple(p.shape[1:]), p.dtype) for p in partials]
    return _Job(partials, out_shapes, ((n, 3), (n, 3), (1,)), start, finish)


def _all_true(conds):
    out = conds[0]
    for c in conds[1:]:
        out = jnp.logical_and(out, c)
    return out


def _pcall(body, *, name, grid, in_specs, out_specs, out_shape, args, sem, scratch=(), jobs=(), alias_in_out=None):
    n_in, n_out, n_scr = len(args), len(out_shape), len(scratch)
    job_in = [a for j in jobs for a in j.inputs]
    job_out = [s for j in jobs for s in j.out_shapes]
    job_sems = [pltpu.SemaphoreType.DMA(shape) for j in jobs for shape in j.sems]
    aliases, in_off, out_off = dict(alias_in_out or {}), n_in, n_out
    for j in jobs:
        for a, b in j.aliases.items():
            aliases[in_off + a] = out_off + b
        in_off += len(j.inputs)
        out_off += len(j.out_shapes)

    def wrapped(*refs):
        ins = refs[:n_in]
        jins = refs[n_in:n_in + len(job_in)]
        o0 = n_in + len(job_in)
        outs = refs[o0:o0 + n_out]
        jouts = refs[o0 + n_out:o0 + n_out + len(job_out)]
        s0 = o0 + n_out + len(job_out)
        scr = refs[s0:s0 + n_scr]
        jsems = refs[s0 + n_scr:]
        if jobs:
            ids = [pl.program_id(a) for a in range(len(grid))]
            first = _all_true([i == 0 for i in ids])
            last = _all_true([i == g - 1 for i, g in zip(ids, grid)])

            def per_job(which):
                i0 = o0_ = 0
                for k, j in enumerate(jobs):
                    fn = j.start if which == "start" else j.finish
                    fn(jins[i0:i0 + len(j.inputs)], jouts[o0_:o0_ + len(j.out_shapes)], *jsems[3 * k:3 * k + 3])
                    i0 += len(j.inputs)
                    o0_ += len(j.out_shapes)

            @pl.when(first)
            def _():
                per_job("start")

        body(*ins, *outs, *scr)
        if jobs:
            @pl.when(last)
            def _():
                per_job("finish")

    semantics = tuple("arbitrary" for _ in grid) if jobs else sem
    return pl.pallas_call(
        wrapped, name=name, grid=grid,
        in_specs=list(in_specs) + [HBM_SPEC] * len(job_in),
        out_specs=list(out_specs) + [HBM_SPEC] * len(job_out),
        out_shape=list(out_shape) + job_out,
        scratch_shapes=list(scratch) + job_sems,
        input_output_aliases=aliases,
        compiler_params=pltpu.CompilerParams(dimension_semantics=semantics, vmem_limit_bytes=VMEM_LIMIT),
    )(*args, *job_in)


def _row_tile(rows, cap):
    if rows <= cap:
        return rows
    best = None
    for t in range(16, cap + 1, 16):
        if rows % t == 0:
            best = t
    assert best is not None
    return best


def _matmul(a, b, mode, out_dtype, tm, tn, tk, name, add=None, jobs=()):
    if mode == "tn":
        K, M = a.shape
    else:
        M, K = a.shape
    N = b.shape[0] if mode == "nt" else b.shape[1]
    tm, tn, tk = min(tm, M), min(tn, N), min(tk, K)
    assert M % tm == 0 and N % tn == 0 and K % tk == 0
    nk = K // tk
    dims = {"nn": NN, "nt": NT, "tn": TN}[mode]

    def body(*refs):
        if add is None:
            a_ref, b_ref, o_ref, acc = refs
            add_ref = None
        else:
            a_ref, b_ref, add_ref, o_ref, acc = refs
        k = pl.program_id(2)
        p = _dot(a_ref[...], b_ref[...], dims)

        def finish(r):
            if add_ref is not None:
                r = r + add_ref[...].astype(F32)
            o_ref[...] = r.astype(out_dtype)

        if nk == 1:
            finish(p)
        else:
            @pl.when(k == 0)
            def _():
                acc[...] = p

            @pl.when(k > 0)
            def _():
                acc[...] += p

            @pl.when(k == nk - 1)
            def _():
                finish(acc[...])

    if mode == "tn":
        a_spec = pl.BlockSpec((tk, tm), lambda i, j, k: (k, i))
    else:
        a_spec = pl.BlockSpec((tm, tk), lambda i, j, k: (i, k))
    if mode == "nt":
        b_spec = pl.BlockSpec((tn, tk), lambda i, j, k: (j, k))
    else:
        b_spec = pl.BlockSpec((tk, tn), lambda i, j, k: (k, j))
    in_specs = [a_spec, b_spec]
    args = [a, b]
    if add is not None:
        in_specs.append(pl.BlockSpec((tm, tn), lambda i, j, k: (i, j)))
        args.append(add)
    return _pcall(
        body, name=name, grid=(M // tm, N // tn, nk), in_specs=in_specs,
        out_specs=[pl.BlockSpec((tm, tn), lambda i, j, k: (i, j))],
        out_shape=[jax.ShapeDtypeStruct((M, N), out_dtype)], args=args,
        scratch=[pltpu.VMEM((tm, tn) if nk > 1 else (8, LANES), F32)],
        sem=("parallel", "parallel", "arbitrary"), jobs=jobs)


def _rmsnorm_fwd(x, w, tm, name, jobs=()):
    T, D = x.shape

    def body(x_ref, w_ref, h_ref):
        xv = x_ref[...]
        r = lax.rsqrt(jnp.mean(xv * xv, axis=-1, keepdims=True) + RMS_EPS)
        h_ref[...] = (xv * r * w_ref[...]).astype(BF16)

    return _pcall(
        body, name=name, grid=(T // tm,),
        in_specs=[pl.BlockSpec((tm, D), lambda i: (i, 0)), pl.BlockSpec((1, D), lambda i: (0, 0))],
        out_specs=[pl.BlockSpec((tm, D), lambda i: (i, 0))],
        out_shape=[jax.ShapeDtypeStruct((T, D), BF16)], args=[x, w], sem=("parallel",), jobs=jobs)


def _rmsnorm_bwd_add(dres, dh, x, w, tm, name, want_bf16, jobs=()):
    T, D = x.shape

    def body(dres_ref, dh_ref, x_ref, w_ref, *outs):
        if want_bf16:
            dx_ref, dxb_ref, dw_ref = outs
        else:
            dx_ref, dw_ref = outs
        i = pl.program_id(0)
        xv = x_ref[...]
        r = lax.rsqrt(jnp.mean(xv * xv, axis=-1, keepdims=True) + RMS_EPS)
        xh = xv * r
        dh_v = dh_ref[...].astype(F32)
        dxh = dh_v * w_ref[...]
        dx = dres_ref[...] + r * (dxh - xh * jnp.mean(dxh * xh, axis=-1, keepdims=True))
        dx_ref[...] = dx
        if want_bf16:
            dxb_ref[...] = dx.astype(BF16)
        part = jnp.sum(dh_v * xh, axis=0, keepdims=True)

        @pl.when(i == 0)
        def _():
            dw_ref[...] = part

        @pl.when(i > 0)
        def _():
            dw_ref[...] += part

    tile = pl.BlockSpec((tm, D), lambda i: (i, 0))
    row = pl.BlockSpec((1, D), lambda i: (0, 0))
    out_specs = [tile] + ([tile] if want_bf16 else []) + [row]
    out_shape = ([jax.ShapeDtypeStruct((T, D), F32)] + ([jax.ShapeDtypeStruct((T, D), BF16)] if want_bf16 else [])
                 + [jax.ShapeDtypeStruct((1, D), F32)])
    return _pcall(body, name=name, grid=(T // tm,), in_specs=[tile, tile, tile, row], out_specs=out_specs,
                  out_shape=out_shape, args=[dres, dh, x, w], sem=("arbitrary",), jobs=jobs)


def _loss_head(x2, target, wf, tm, name):
    T, D = x2.shape

    def body(x_ref, t_ref, w_ref, dx_ref, dxb_ref, loss_ref, dw_ref):
        i = pl.program_id(0)
        xv = x_ref[...]
        r = lax.rsqrt(jnp.mean(xv * xv, axis=-1, keepdims=True) + RMS_EPS)
        xh = xv * r
        wv = w_ref[...]
        e = xh * wv - t_ref[...]
        lpart = 0.5 * jnp.sum(jnp.sum(e * e, axis=-1, keepdims=True), axis=0, keepdims=True) * (1.0 / D)
        dy = e * (1.0 / D)
        dxh = dy * wv
        dx = r * (dxh - xh * jnp.mean(dxh * xh, axis=-1, keepdims=True))
        dx_ref[...] = dx
        dxb_ref[...] = dx.astype(BF16)
        wpart = jnp.sum(dy * xh, axis=0, keepdims=True)
        lfull = jnp.broadcast_to(lpart, (8, LANES))

        @pl.when(i == 0)
        def _():
            loss_ref[...] = lfull
            dw_ref[...] = wpart

        @pl.when(i > 0)
        def _():
            loss_ref[...] += lfull
            dw_ref[...] += wpart

    tile = pl.BlockSpec((tm, D), lambda i: (i, 0))
    row = pl.BlockSpec((1, D), lambda i: (0, 0))
    return _pcall(
        body, name=name, grid=(T // tm,), in_specs=[tile, tile, row],
        out_specs=[tile, tile, pl.BlockSpec((8, LANES), lambda i: (0, 0)), row],
        out_shape=[jax.ShapeDtypeStruct((T, D), F32), jax.ShapeDtypeStruct((T, D), BF16),
                   jax.ShapeDtypeStruct((8, LANES), F32), jax.ShapeDtypeStruct((1, D), F32)],
        args=[x2, target, wf], sem=("arbitrary",))


def _rope_tables(pos_col, inv2, tm, name, jobs=()):
    T = pos_col.shape[0]

    def body(p_ref, f_ref, c_ref, s_ref):
        ang = p_ref[...] * f_ref[...]
        lane = lax.broadcasted_iota(jnp.int32, ang.shape, 1)
        c_ref[...] = jnp.cos(ang)
        s_ref[...] = jnp.where(lane < RET_DK // 2, -1.0, 1.0) * jnp.sin(ang)

    tile = pl.BlockSpec((tm, RET_DK), lambda i: (i, 0))
    return _pcall(
        body, name=name, grid=(T // tm,),
        in_specs=[pl.BlockSpec((tm, 1), lambda i: (i, 0)), pl.BlockSpec((1, RET_DK), lambda i: (0, 0))],
        out_specs=[tile, tile], out_shape=[jax.ShapeDtypeStruct((T, RET_DK), F32)] * 2,
        args=[pos_col, inv2], sem=("parallel",), jobs=jobs)


def _mix_fwd(a_in, b_in, proj, x, w_ro, w_lo, w_out, mb, w2, tm, name):
    T, D = x.shape

    def body(a_ref, b_ref, gr_ref, gl_ref, x_ref, wro_ref, wlo_ref, wout_ref, mb_ref, w2_ref,
             x1_ref, mix_ref, h2_ref, ya_ref, yb_ref):
        ya = _dot(a_ref[...], wro_ref[...], NN)
        yb = _dot(b_ref[...], wlo_ref[...], NN)
        ya_ref[...] = ya.astype(BF16)
        yb_ref[...] = yb.astype(BF16)
        sa = _sigmoid(gr_ref[...].astype(F32) + mb_ref[0:1, :])
        sb = _sigmoid(gl_ref[...].astype(F32) + mb_ref[1:2, :])
        mix = (sa * ya + sb * yb).astype(BF16)
        mix_ref[...] = mix
        x1 = x_ref[...] + _dot(mix, wout_ref[...], NN)
        x1_ref[...] = x1
        r = lax.rsqrt(jnp.mean(x1 * x1, axis=-1, keepdims=True) + RMS_EPS)
        h2_ref[...] = (x1 * r * w2_ref[...]).astype(BF16)

    tile = pl.BlockSpec((tm, D), lambda i: (i, 0))
    wspec = pl.BlockSpec((D, D), lambda i: (0, 0))
    return _pcall(
        body, name=name, grid=(T // tm,),
        in_specs=[tile, tile,
                  pl.BlockSpec((tm, D), lambda i: (i, COL_GR)), pl.BlockSpec((tm, D), lambda i: (i, COL_GL)),
                  tile, wspec, wspec, wspec,
                  pl.BlockSpec((2, D), lambda i: (0, 0)), pl.BlockSpec((1, D), lambda i: (0, 0))],
        out_specs=[tile] * 5,
        out_shape=[jax.ShapeDtypeStruct((T, D), F32)] + [jax.ShapeDtypeStruct((T, D), BF16)] * 4,
        args=[a_in, b_in, proj, proj, x, w_ro, w_lo, w_out, mb, w2], sem=("parallel",))


def _write_pieces(dst_ref, sems, stashes, row0, col0s, ids, grid, compute):
    def aligned(v, m):
        return v if isinstance(v, int) else pl.multiple_of(v, m)

    def copies(slot):
        return [pltpu.make_async_copy(
                    st.at[slot],
                    dst_ref.at[pl.ds(aligned(row0, 16), st.shape[1]), pl.ds(aligned(c0, LANES), st.shape[2])],
                    sems.at[slot, k])
                for k, (st, c0) in enumerate(zip(stashes, col0s))]

    step = ids[0]
    for i, g in zip(ids[1:], grid[1:]):
        step = step * g + i
    slot = step % 2
    last = _all_true([i == g - 1 for i, g in zip(ids, grid)])
    compute(slot)

    @pl.when(step > 0)
    def _():
        for cp in copies(1 - slot):
            cp.wait()

    for cp in copies(slot):
        cp.start()

    @pl.when(last)
    def _():
        for cp in copies(slot):
            cp.wait()


def _mix_bwd(dx1b, ya, yb, proj, w_ro, w_lo, w_out, mb, tm, name, jobs=()):
    T, D = ya.shape
    grid = (T // tm,)

    def body(dx_ref, ya_ref, yb_ref, gr_ref, gl_ref, wro_ref, wlo_ref, wout_ref, mb_ref,
             da_ref, db_ref, dya_ref, dyb_ref, dp_ref, dmb_ref, dgr_s, dgl_s, wsem):
        i = pl.program_id(0)

        def compute(slot):
            dmix = _dot(dx_ref[...], wout_ref[...], NT)
            ya = ya_ref[...].astype(F32)
            yb = yb_ref[...].astype(F32)
            sa = _sigmoid(gr_ref[...].astype(F32) + mb_ref[0:1, :])
            sb = _sigmoid(gl_ref[...].astype(F32) + mb_ref[1:2, :])
            dya = (dmix * sa).astype(BF16)
            dyb = (dmix * sb).astype(BF16)
            dgr = dmix * ya * sa * (1.0 - sa)
            dgl = dmix * yb * sb * (1.0 - sb)
            dya_ref[...] = dya
            dyb_ref[...] = dyb
            dgr_s[slot] = dgr.astype(BF16)
            dgl_s[slot] = dgl.astype(BF16)
            da_ref[...] = _dot(dya, wro_ref[...], NT).astype(BF16)
            db_ref[...] = _dot(dyb, wlo_ref[...], NT).astype(BF16)

            @pl.when(i == 0)
            def _():
                dmb_ref[...] = jnp.zeros_like(dmb_ref)

            dmb_ref[0:1, :] += jnp.sum(dgr, axis=0, keepdims=True)
            dmb_ref[1:2, :] += jnp.sum(dgl, axis=0, keepdims=True)

        _write_pieces(dp_ref, wsem, [dgr_s, dgl_s], i * tm, [COL_GR * D, COL_GL * D], [i], grid, compute)

    tile = pl.BlockSpec((tm, D), lambda i: (i, 0))
    wspec = pl.BlockSpec((D, D), lambda i: (0, 0))
    two = pl.BlockSpec((2, D), lambda i: (0, 0))
    return _pcall(
        body, name=name, grid=grid,
        in_specs=[tile, tile, tile,
                  pl.BlockSpec((tm, D), lambda i: (i, COL_GR)), pl.BlockSpec((tm, D), lambda i: (i, COL_GL)),
                  wspec, wspec, wspec, two],
        out_specs=[tile] * 4 + [HBM_SPEC, two],
        out_shape=[jax.ShapeDtypeStruct((T, D), BF16)] * 4
                  + [jax.ShapeDtypeStruct((T, D_IN), BF16), jax.ShapeDtypeStruct((2, D), F32)],
        args=[dx1b, ya, yb, proj, proj, w_ro, w_lo, w_out, mb],
        scratch=[pltpu.VMEM((2, tm, D), BF16), pltpu.VMEM((2, tm, D), BF16), pltpu.SemaphoreType.DMA((2, 2))],
        sem=("arbitrary",), jobs=jobs)


def _ret_decay_consts(lg):
    L = RET_BLOCK
    n = lax.broadcasted_iota(jnp.int32, (L, L), 0)
    m = lax.broadcasted_iota(jnp.int32, (L, L), 1)
    cn, cm = n // CHUNK, m // CHUNK
    expo = jnp.where(cn == cm, jnp.abs(n - m), n - m).astype(F32)
    wm = jnp.where(cm <= cn, jnp.exp(lg * expo), 0.0)
    idx = lax.broadcasted_iota(jnp.int32, (L, 1), 0).astype(F32)
    qd = jnp.exp(lg * (idx + 1.0))
    kd = jnp.exp(lg * (L - 1.0 - idx))
    bd = jnp.exp(lg * float(L))
    return wm, qd, kd, bd


def _rotate(v, cos2, sin2s):
    return v * cos2 + pltpu.roll(v, RET_DK // 2, 1) * sin2s


def _rotate_t(d, cos2, sin2s):
    return d * cos2 - pltpu.roll(d, RET_DK // 2, 1) * sin2s


def _retention_fwd(proj, cos2, sin2s, lgam, gn_w, B, S, name, jobs=()):
    T = B * S
    nb = S // RET_BLOCK
    scale = RET_DK ** -0.5

    def body(q_ref, k_ref, v_ref, g_ref, c_ref, s_ref, lg_ref, gw_ref, o_ref, a_ref, qr, kr, st):
        wm, qd, kd, bd = _ret_decay_consts(lg_ref[0:1, 0:1])
        cos2, sin2s = c_ref[...], s_ref[...]
        qr[...] = _rotate(q_ref[...].astype(F32), cos2, sin2s)
        kr[...] = _rotate(k_ref[...].astype(F32), cos2, sin2s) * scale
        st[...] = jnp.zeros_like(st)
        gw = gw_ref[...]
        for j in range(nb):
            rows = pl.ds(j * RET_BLOCK, RET_BLOCK)
            qb = qr[rows, :]
            kb = kr[rows, :]
            vb = v_ref[rows, :].astype(BF16)
            sc = _dot(qb.astype(BF16), kb.astype(BF16), NT) * wm
            o = _dot(sc.astype(BF16), vb, NN) + _dot((qb * qd).astype(BF16), st[...].astype(BF16), NN)
            st[...] = st[...] * bd + _dot((kb * kd).astype(BF16), vb, TN)
            o_ref[rows, :] = o
            mu = jnp.mean(o, axis=-1, keepdims=True)
            oc = o - mu
            var = jnp.mean(oc * oc, axis=-1, keepdims=True)
            y = oc * lax.rsqrt(var + GN_EPS) * gw
            g = g_ref[rows, :].astype(F32)
            a_ref[rows, :] = (y * (g * _sigmoid(g))).astype(BF16)

    blk = lambda w, off: pl.BlockSpec((S, w), lambda b, h: (b, off + h))
    return _pcall(
        body, name=name, grid=(B, RET_HEADS),
        in_specs=[blk(RET_DK, COL_Q), blk(RET_DK, COL_K), blk(RET_DV, COL_V), blk(RET_DV, COL_G),
                  pl.BlockSpec((S, RET_DK), lambda b, h: (b, 0)), pl.BlockSpec((S, RET_DK), lambda b, h: (b, 0)),
                  pl.BlockSpec((None, 8, LANES), lambda b, h: (h, 0, 0)),
                  pl.BlockSpec((1, RET_DV), lambda b, h: (0, h))],
        out_specs=[blk(RET_DV, 0), blk(RET_DV, 0)],
        out_shape=[jax.ShapeDtypeStruct((T, RET_HEADS * RET_DV), F32),
                   jax.ShapeDtypeStruct((T, RET_HEADS * RET_DV), BF16)],
        args=[proj, proj, proj, proj, cos2, sin2s, lgam, gn_w],
        scratch=[pltpu.VMEM((S, RET_DK), F32), pltpu.VMEM((S, RET_DK), F32), pltpu.VMEM((RET_DK, RET_DV), F32)],
        sem=("parallel", "parallel"), jobs=jobs)


def _retention_bwd(da_in, o, proj, dproj, cos2, sin2s, lgam, gn_w, B, S, name, jobs=()):
    T = B * S
    nb = S // RET_BLOCK
    scale = RET_DK ** -0.5
    grid = (RET_HEADS, B)

    def body(da_ref, o_ref, q_ref, k_ref, v_ref, g_ref, c_ref, s_ref, lg_ref, gw_ref, _, dp_ref, dgw_ref,
             qr, kr, do_s, sts, rst, dq_s, dk_s, dv_s, dg_s, wsem):
        h, b = pl.program_id(0), pl.program_id(1)

        def compute(slot):
            wm, qd, kd, bd = _ret_decay_consts(lg_ref[0:1, 0:1])
            cos2, sin2s = c_ref[...], s_ref[...]
            qr[...] = _rotate(q_ref[...].astype(F32), cos2, sin2s)
            kr[...] = _rotate(k_ref[...].astype(F32), cos2, sin2s) * scale
            gw = gw_ref[...]
            st = jnp.zeros((RET_DK, RET_DV), F32)
            dgw = jnp.zeros((1, RET_DV), F32)
            for j in range(nb):
                rows = pl.ds(j * RET_BLOCK, RET_BLOCK)
                ov = o_ref[rows, :]
                mu = jnp.mean(ov, axis=-1, keepdims=True)
                oc = ov - mu
                rstd = lax.rsqrt(jnp.mean(oc * oc, axis=-1, keepdims=True) + GN_EPS)
                y = oc * rstd
                g = g_ref[rows, :].astype(F32)
                sg = _sigmoid(g)
                da = da_ref[rows, :].astype(F32)
                dg_s[slot, rows, :] = (da * (y * gw) * (sg * (1.0 + g * (1.0 - sg)))).astype(BF16)
                dyw = da * (g * sg)
                dgw = dgw + jnp.sum(dyw * y, axis=0, keepdims=True)
                dy = dyw * gw
                do_s[rows, :] = rstd * (dy - jnp.mean(dy, axis=-1, keepdims=True)
                                        - y * jnp.mean(dy * y, axis=-1, keepdims=True))
                sts[j] = st
                st = st * bd + _dot((kr[rows, :] * kd).astype(BF16), v_ref[rows, :].astype(BF16), TN)

            @pl.when(b == 0)
            def _():
                dgw_ref[...] = dgw

            @pl.when(b > 0)
            def _():
                dgw_ref[...] += dgw

            rst[...] = jnp.zeros_like(rst)
            for j in reversed(range(nb)):
                rows = pl.ds(j * RET_BLOCK, RET_BLOCK)
                qb = qr[rows, :]
                kb = kr[rows, :]
                qbb, kbb = qb.astype(BF16), kb.astype(BF16)
                vb = v_ref[rows, :].astype(BF16)
                dob = do_s[rows, :]
                dobb = dob.astype(BF16)
                a_m = (_dot(qbb, kbb, NT) * wm).astype(BF16)
                b_m = (_dot(dobb, vb, NT) * wm).astype(BF16)
                rb = rst[...].astype(BF16)
                dq = _dot(b_m, kbb, NN) + _dot((dob * qd).astype(BF16), sts[j].astype(BF16), NT)
                dk = _dot(b_m, qbb, TN) + kd * _dot(vb, rb, NT)
                dv = _dot(a_m, dobb, TN) + kd * _dot(kbb, rb, NN)
                rst[...] = rst[...] * bd + _dot((qb * qd).astype(BF16), dobb, TN)
                cb, sb = c_ref[rows, :], s_ref[rows, :]
                dq_s[slot, rows, :] = _rotate_t(dq, cb, sb).astype(BF16)
                dk_s[slot, rows, :] = _rotate_t(dk * scale, cb, sb).astype(BF16)
                dv_s[slot, rows, :] = dv.astype(BF16)

        cols = [(COL_Q + h) * RET_DK, (COL_K + h) * RET_DK, (COL_V + h) * RET_DV, (COL_G + h) * RET_DV]
        _write_pieces(dp_ref, wsem, [dq_s, dk_s, dv_s, dg_s], b * S, cols, [h, b], grid, compute)

    blk = lambda w, off: pl.BlockSpec((S, w), lambda h, b: (b, off + h))
    return _pcall(
        body, name=name, grid=grid,
        in_specs=[blk(RET_DV, 0), blk(RET_DV, 0),
                  blk(RET_DK, COL_Q), blk(RET_DK, COL_K), blk(RET_DV, COL_V), blk(RET_DV, COL_G),
                  pl.BlockSpec((S, RET_DK), lambda h, b: (b, 0)), pl.BlockSpec((S, RET_DK), lambda h, b: (b, 0)),
                  pl.BlockSpec((None, 8, LANES), lambda h, b: (h, 0, 0)),
                  pl.BlockSpec((1, RET_DV), lambda h, b: (0, h)), HBM_SPEC],
        out_specs=[HBM_SPEC, pl.BlockSpec((1, RET_DV), lambda h, b: (0, h))],
        out_shape=[jax.ShapeDtypeStruct(dproj.shape, dproj.dtype),
                   jax.ShapeDtypeStruct((1, RET_HEADS * RET_DV), F32)],
        args=[da_in, o, proj, proj, proj, proj, cos2, sin2s, lgam, gn_w, dproj],
        scratch=[pltpu.VMEM((S, RET_DK), F32), pltpu.VMEM((S, RET_DK), F32),
                 pltpu.VMEM((S, RET_DV), F32), pltpu.VMEM((nb, RET_DK, RET_DV), F32),
                 pltpu.VMEM((RET_DK, RET_DV), F32),
                 pltpu.VMEM((2, S, RET_DK), BF16), pltpu.VMEM((2, S, RET_DK), BF16),
                 pltpu.VMEM((2, S, RET_DV), BF16), pltpu.VMEM((2, S, RET_DV), BF16), pltpu.SemaphoreType.DMA((2, 4))],
        sem=("arbitrary", "arbitrary"), jobs=jobs, alias_in_out={10: 0})


def _lru_gates(x, cw, cb, wr, wi, br, bi, lam):
    xc = cb + cw[LRU_CONV - 1:LRU_CONV, :] * x
    for j in range(LRU_CONV - 1):
        xc = xc + cw[j:j + 1, :] * _shift_down(x, LRU_CONV - 1 - j, 0.0)
    xcb = xc.astype(BF16)
    r = _sigmoid(_dot(xcb, wr, NN) + br)
    ig = _sigmoid(_dot(xcb, wi, NN) + bi)
    z = -lam
    sp = jnp.maximum(z, 0.0) + jnp.log1p(jnp.exp(-jnp.abs(z)))
    log_a = (-LRU_C) * r * sp
    a = jnp.exp(log_a)
    om = -jnp.tanh(log_a) * (a * a + 1.0)
    sq = jnp.sqrt(om)
    return xc, xcb, r, ig, sp, a, sq


def _lru_fwd(proj, cw, cb, wr, wi, br, bi, lam, B, S, name):
    T = B * S
    W = LRU_BLOCKS * LRU_BLOCK

    def body(x_ref, y_ref, cw_ref, cb_ref, wr_ref, wi_ref, br_ref, bi_ref, lam_ref, h_ref, bin_ref, a_s, b_s):
        xc, _, _, ig, _, a, sq = _lru_gates(x_ref[...].astype(F32), cw_ref[...], cb_ref[...], wr_ref[...], wi_ref[...],
                                           br_ref[...], bi_ref[...], lam_ref[...])
        a_s[...] = a
        b_s[...] = sq * ig * xc
        _scan_forward_ref(a_s, b_s, h_ref)
        bin_ref[...] = (h_ref[...] * _gelu(y_ref[...].astype(F32))).astype(BF16)

    blk = lambda off: pl.BlockSpec((S, LRU_BLOCK), lambda b, n: (b, off + n))
    vec = lambda rows: pl.BlockSpec((rows, LRU_BLOCK), lambda b, n: (0, n))
    wspec = pl.BlockSpec((None, LRU_BLOCK, LRU_BLOCK), lambda b, n: (n, 0, 0))
    return _pcall(
        body, name=name, grid=(B, LRU_BLOCKS),
        in_specs=[blk(COL_XL), blk(COL_YL), vec(LRU_CONV), vec(1), wspec, wspec, vec(1), vec(1), vec(1)],
        out_specs=[blk(0), blk(0)],
        out_shape=[jax.ShapeDtypeStruct((T, W), F32), jax.ShapeDtypeStruct((T, W), BF16)],
        args=[proj, proj, cw, cb, wr, wi, br, bi, lam],
        scratch=[pltpu.VMEM((S, LRU_BLOCK), F32), pltpu.VMEM((S, LRU_BLOCK), F32)], sem=("parallel", "parallel"))


def _lru_bwd(db_in, h, proj, dproj, cw, cb, wr, wi, br, bi, lam, B, S, name, jobs=()):
    T = B * S
    W = LRU_BLOCKS * LRU_BLOCK

    grid = (LRU_BLOCKS, B)

    def body(dbin_ref, h_ref, x_ref, y_ref, cw_ref, cb_ref, wr_ref, wi_ref, br_ref, bi_ref, lam_ref, _,
             dp_ref, dcw_ref, dcb_ref, dwr_ref, dwi_ref, dbr_ref, dbi_ref, dlam_ref, dx_s, dy_s, wsem,
             an_s, u_s, dh_s):
        n, b = pl.program_id(0), pl.program_id(1)

        def compute(slot):
            x = x_ref[...].astype(F32)
            cw = cw_ref[...]
            wr, wi = wr_ref[...], wi_ref[...]
            lam = lam_ref[...]
            xc, xcb, r, ig, sp, a, sq = _lru_gates(x, cw, cb_ref[...], wr, wi, br_ref[...], bi_ref[...], lam)
            hv = h_ref[...]
            gel, dgel = _gelu_and_grad(y_ref[...].astype(F32))
            dbin = dbin_ref[...].astype(F32)
            dy_s[slot] = (dbin * hv * dgel).astype(BF16)
            an_s[...] = _shift_up(a, 1, 0.0)
            u_s[...] = dbin * gel
            _scan_backward_ref(an_s, u_s, dh_s)
            dh = dh_s[...]
            hprev = _shift_down(hv, 1, 0.0)
            d_ig = dh * sq * xc
            d_xc = dh * sq * ig
            a2 = a * a
            d_loga = dh * hprev * a - dh * ig * xc * a2 / sq
            d_r = d_loga * ((-LRU_C) * sp)
            d_sp = jnp.sum(d_loga * ((-LRU_C) * r), axis=0, keepdims=True)
            dlam = -d_sp * _sigmoid(-lam)
            d_pr = d_r * r * (1.0 - r)
            d_pi = d_ig * ig * (1.0 - ig)
            d_prb, d_pib = d_pr.astype(BF16), d_pi.astype(BF16)
            d_xc = d_xc + _dot(d_prb, wr, NT) + _dot(d_pib, wi, NT)

            @pl.when(b == 0)
            def _():
                for ref in (dcw_ref, dcb_ref, dwr_ref, dwi_ref, dbr_ref, dbi_ref, dlam_ref):
                    ref[...] = jnp.zeros_like(ref)

            dx = cw[LRU_CONV - 1:LRU_CONV, :] * d_xc
            for j in range(LRU_CONV - 1):
                sft = LRU_CONV - 1 - j
                dx = dx + cw[j:j + 1, :] * _shift_up(d_xc, sft, 0.0)
                dcw_ref[j:j + 1, :] += jnp.sum(d_xc * _shift_down(x, sft, 0.0), axis=0, keepdims=True)
            dcw_ref[LRU_CONV - 1:LRU_CONV, :] += jnp.sum(d_xc * x, axis=0, keepdims=True)
            dx_s[slot] = dx.astype(BF16)
            dcb_ref[...] += jnp.sum(d_xc, axis=0, keepdims=True)
            dwr_ref[...] += _dot(xcb, d_prb, TN)
            dwi_ref[...] += _dot(xcb, d_pib, TN)
            dbr_ref[...] += jnp.sum(d_pr, axis=0, keepdims=True)
            dbi_ref[...] += jnp.sum(d_pi, axis=0, keepdims=True)
            dlam_ref[...] += dlam

        cols = [(COL_XL + n) * LRU_BLOCK, (COL_YL + n) * LRU_BLOCK]
        _write_pieces(dp_ref, wsem, [dx_s, dy_s], b * S, cols, [n, b], grid, compute)

    blk = lambda off: pl.BlockSpec((S, LRU_BLOCK), lambda n, b: (b, off + n))
    vec = lambda rows: pl.BlockSpec((rows, LRU_BLOCK), lambda n, b: (0, n))
    wspec = pl.BlockSpec((None, LRU_BLOCK, LRU_BLOCK), lambda n, b: (n, 0, 0))
    vshape = lambda rows: jax.ShapeDtypeStruct((rows, W), F32)
    wshape = jax.ShapeDtypeStruct((LRU_BLOCKS, LRU_BLOCK, LRU_BLOCK), F32)
    return _pcall(
        body, name=name, grid=grid,
        in_specs=[blk(0), blk(0), blk(COL_XL), blk(COL_YL), vec(LRU_CONV), vec(1), wspec, wspec, vec(1), vec(1),
                  vec(1), HBM_SPEC],
        out_specs=[HBM_SPEC, vec(LRU_CONV), vec(1), wspec, wspec, vec(1), vec(1), vec(1)],
        out_shape=[jax.ShapeDtypeStruct(dproj.shape, dproj.dtype),
                   vshape(LRU_CONV), vshape(1), wshape, wshape, vshape(1), vshape(1), vshape(1)],
        args=[db_in, h, proj, proj, cw, cb, wr, wi, br, bi, lam, dproj],
        scratch=[pltpu.VMEM((2, S, LRU_BLOCK), BF16), pltpu.VMEM((2, S, LRU_BLOCK), BF16), pltpu.SemaphoreType.DMA((2, 2)),
                 pltpu.VMEM((S, LRU_BLOCK), F32), pltpu.VMEM((S, LRU_BLOCK), F32), pltpu.VMEM((S, LRU_BLOCK), F32)],
        sem=("arbitrary", "arbitrary"), jobs=jobs, alias_in_out={11: 0})


FFN_CT = 256


def _ffn_conv(gate, cw, cb):
    gc = cb + cw[FFN_CONV - 1:FFN_CONV, :] * gate
    for j in range(FFN_CONV - 1):
        gc = gc + cw[j:j + 1, :] * _shift_down(gate, FFN_CONV - 1 - j, 0.0)
    return gc


def _ffn_act_fwd(up, cw, cb, B, S, name):
    T = B * S
    nct = D_FF // FFN_CT

    def body(g_ref, v_ref, cw_ref, cb_ref, f_ref):
        gc = _ffn_conv(g_ref[...].astype(F32), cw_ref[...], cb_ref[...])
        f_ref[...] = (_gelu(gc) * v_ref[...].astype(F32)).astype(BF16)

    return _pcall(
        body, name=name, grid=(B, nct),
        in_specs=[pl.BlockSpec((S, FFN_CT), lambda b, c: (b, c)), pl.BlockSpec((S, FFN_CT), lambda b, c: (b, nct + c)),
                  pl.BlockSpec((FFN_CONV, FFN_CT), lambda b, c: (0, c)), pl.BlockSpec((1, FFN_CT), lambda b, c: (0, c))],
        out_specs=[pl.BlockSpec((S, FFN_CT), lambda b, c: (b, c))],
        out_shape=[jax.ShapeDtypeStruct((T, D_FF), BF16)], args=[up, up, cw, cb], sem=("parallel", "parallel"))[0]


def _ffn_act_bwd(df, up, cw, cb, B, S, name, jobs=()):
    T = B * S
    nct = D_FF // FFN_CT

    grid = (nct, B)

    def body(df_ref, g_ref, v_ref, cw_ref, cb_ref, du_ref, dcw_ref, dcb_ref, dg_s, dv_s, wsem):
        c, b = pl.program_id(0), pl.program_id(1)

        def compute(slot):
            gate = g_ref[...].astype(F32)
            cw = cw_ref[...]
            gc = _ffn_conv(gate, cw, cb_ref[...])
            gel, dgel = _gelu_and_grad(gc)
            dfv = df_ref[...].astype(F32)
            dv_s[slot] = (dfv * gel).astype(BF16)
            dgc = dfv * v_ref[...].astype(F32) * dgel

            @pl.when(b == 0)
            def _():
                dcw_ref[...] = jnp.zeros_like(dcw_ref)
                dcb_ref[...] = jnp.zeros_like(dcb_ref)

            dgate = cw[FFN_CONV - 1:FFN_CONV, :] * dgc
            for j in range(FFN_CONV - 1):
                sft = FFN_CONV - 1 - j
                dgate = dgate + cw[j:j + 1, :] * _shift_up(dgc, sft, 0.0)
                dcw_ref[j:j + 1, :] += jnp.sum(dgc * _shift_down(gate, sft, 0.0), axis=0, keepdims=True)
            dcw_ref[FFN_CONV - 1:FFN_CONV, :] += jnp.sum(dgc * gate, axis=0, keepdims=True)
            dg_s[slot] = dgate.astype(BF16)
            dcb_ref[...] += jnp.sum(dgc, axis=0, keepdims=True)

        _write_pieces(du_ref, wsem, [dg_s, dv_s], b * S, [c * FFN_CT, (nct + c) * FFN_CT], [c, b], grid, compute)

    blk = pl.BlockSpec((S, FFN_CT), lambda c, b: (b, c))
    return _pcall(
        body, name=name, grid=grid,
        in_specs=[blk, blk, pl.BlockSpec((S, FFN_CT), lambda c, b: (b, nct + c)),
                  pl.BlockSpec((FFN_CONV, FFN_CT), lambda c, b: (0, c)),
                  pl.BlockSpec((1, FFN_CT), lambda c, b: (0, c))],
        out_specs=[HBM_SPEC, pl.BlockSpec((FFN_CONV, FFN_CT), lambda c, b: (0, c)),
                   pl.BlockSpec((1, FFN_CT), lambda c, b: (0, c))],
        out_shape=[jax.ShapeDtypeStruct((T, 2 * D_FF), BF16),
                   jax.ShapeDtypeStruct((FFN_CONV, D_FF), F32), jax.ShapeDtypeStruct((1, D_FF), F32)],
        args=[df, up, up, cw, cb],
        scratch=[pltpu.VMEM((2, S, FFN_CT), BF16), pltpu.VMEM((2, S, FFN_CT), BF16), pltpu.SemaphoreType.DMA((2, 2))],
        sem=("arbitrary", "arbitrary"), jobs=jobs)


def _rs_add(g, recv, mode, core, name, also_bf16=False):
    shard = tuple(recv.shape[1:])
    if mode == "mid":
        a, e, c2 = shard
        g_in = g.reshape(a, N_DEV, e, c2)
        grid = (4, 1)
        g_spec = pl.BlockSpec((a, None, e, c2), lambda k, i, c_ref: (0, 2 * k + c_ref[0], 0, 0))
        r_spec = pl.BlockSpec((None, a, e, c2), lambda k, i, c_ref: (k, 0, 0, 0))
    else:
        R, C = shard
        tr = _row_tile(R, 512)
        grid = (4, R // tr)
        if mode == "rows":
            g_in = g.reshape(N_DEV, R, C)
            g_spec = pl.BlockSpec((None, tr, C), lambda k, i, c_ref: (2 * k + c_ref[0], i, 0))
        else:
            g_in = g
            g_spec = pl.BlockSpec((tr, C), lambda k, i, c_ref: (i, 2 * k + c_ref[0]))
        r_spec = pl.BlockSpec((None, tr, C), lambda k, i, c_ref: (k, i, 0))

    def body(c_ref, g_ref, r_ref, o_ref, *ob_ref):
        s = g_ref[...] + r_ref[...]
        o_ref[...] = s
        if also_bf16:
            ob_ref[0][...] = s.astype(BF16)

    out_shape = jax.ShapeDtypeStruct(recv.shape, recv.dtype)
    return pl.pallas_call(
        body, name=name,
        grid_spec=pltpu.PrefetchScalarGridSpec(
            num_scalar_prefetch=1, grid=grid, in_specs=[g_spec, r_spec],
            out_specs=[r_spec, r_spec] if also_bf16 else r_spec),
        out_shape=[out_shape, jax.ShapeDtypeStruct(recv.shape, BF16)] if also_bf16 else out_shape,
        compiler_params=pltpu.CompilerParams(dimension_semantics=("parallel", "parallel"),
                                             vmem_limit_bytes=VMEM_LIMIT),
    )(core, g_in, recv)


def _adam_update(gv, w, m, v):
    nm = ADAM_B1 * m + (1.0 - ADAM_B1) * gv
    nv = ADAM_B2 * v + (1.0 - ADAM_B2) * (gv * gv)
    m_hat = nm / (1.0 - ADAM_B1 ** ADAM_STEP)
    v_hat = nv / (1.0 - ADAM_B2 ** ADAM_STEP)
    delta = -ADAM_LR * (m_hat / (jnp.sqrt(v_hat) + ADAM_EPS) + ADAM_WD * w)
    return delta, nm, nv


def _adamw_shard(partial, recv, w, m, v, chip, name):
    shape = tuple(w.shape)
    tr = _row_tile(shape[0], 256)
    rest = shape[1:]
    zeros = (0,) * len(rest)
    tile = pl.BlockSpec((tr,) + rest, lambda i, s: (i,) + zeros)

    def body(_, p_ref, r_ref, w_ref, m_ref, v_ref, g_ref, d_ref, nm_ref, nv_ref):
        gv = p_ref[...] + r_ref[0].astype(F32) + r_ref[1].astype(F32) + r_ref[2].astype(F32)
        g_ref[...] = gv
        d_ref[...], nm_ref[...], nv_ref[...] = _adam_update(gv, w_ref[...], m_ref[...], v_ref[...])

    grid_spec = pltpu.PrefetchScalarGridSpec(
        num_scalar_prefetch=1, grid=(shape[0] // tr,),
        in_specs=[pl.BlockSpec((None, tr) + rest, lambda i, s: (s[0], i) + zeros),
                  pl.BlockSpec((3, tr) + rest, lambda i, s: (0, i) + zeros), tile, tile, tile],
        out_specs=[tile] * 4)
    return pl.pallas_call(
        body, name=name, grid_spec=grid_spec, out_shape=[jax.ShapeDtypeStruct(shape, F32)] * 4,
        compiler_params=pltpu.CompilerParams(dimension_semantics=("parallel",), vmem_limit_bytes=VMEM_LIMIT),
    )(chip, partial, recv, w, m, v)


SMALL_LANES = 1024


def _small_rows(shape):
    r, w = shape
    return r * max(1, w // SMALL_LANES)


def _small_allreduce(parts, name):
    n = len(parts)
    shapes = [tuple(p.shape) for p in parts]
    offs, total = [], 0
    for s in shapes:
        offs.append(total)
        total += _small_rows(s)
    rows = -(-total // 8) * 8

    def body(*refs):
        p_refs, o_refs = refs[:n], refs[n:2 * n]
        buf, tot, send_sems, recv_sems = refs[2 * n:]
        x, y, c = _mesh_pos()
        me, sibling = (x, y, c), (x, y, 1 - c)
        chips = _other_chips(x, y)

        def slot(px, py, pc):
            return buf.at[4 * px + 2 * py + pc]

        def copy(k, block, to):
            return _remote(slot(*block), slot(*block), send_sems.at[k], recv_sems.at[k], to)

        tot[...] = jnp.zeros_like(tot)
        for p_ref, (r, w), off in zip(p_refs, shapes, offs):
            wl = min(w, SMALL_LANES)
            for part in range(max(1, w // SMALL_LANES)):
                tot[pl.ds(off + part * r, r), pl.ds(0, wl)] = p_ref[:, pl.ds(part * SMALL_LANES, wl)]
        buf[4 * x + 2 * y + c] = tot[...]
        first = [copy(0, me, sibling)] + [copy(1 + j, me, (*chip, c)) for j, chip in enumerate(chips)]
        for cp in first:
            cp.start()
        passed = [copy(4 + j, (*chip, c), sibling) for j, chip in enumerate(chips)]
        for j, chip in enumerate(chips):
            copy(1 + j, (*chip, c), me).wait_recv()
            passed[j].start()
        copy(0, sibling, me).wait_recv()
        for j, chip in enumerate(chips):
            copy(4 + j, (*chip, 1 - c), me).wait_recv()
        for cp in first + passed:
            cp.wait_send()
        acc = buf[0]
        for d in range(1, N_DEV):
            acc = acc + buf[d]
        tot[...] = acc
        for o_ref, (r, w), off in zip(o_refs, shapes, offs):
            wl = min(w, SMALL_LANES)
            for part in range(max(1, w // SMALL_LANES)):
                o_ref[:, pl.ds(part * SMALL_LANES, wl)] = tot[pl.ds(off + part * r, r), pl.ds(0, wl)]

    vm = pl.BlockSpec(memory_space=pltpu.VMEM)
    return pl.pallas_call(
        body, name=name,
        in_specs=[vm] * n, out_specs=[vm] * n,
        out_shape=[jax.ShapeDtypeStruct(s, F32) for s in shapes],
        scratch_shapes=[pltpu.VMEM((N_DEV, rows, SMALL_LANES), F32), pltpu.VMEM((rows, SMALL_LANES), F32),
                        pltpu.SemaphoreType.DMA((7,)), pltpu.SemaphoreType.DMA((7,))],
    )(*parts)


def _adamw_small(gs, ws, ms, vs, name):
    n = len(gs)

    def body(*refs):
        g_r, w_r, m_r, v_r = refs[:n], refs[n:2 * n], refs[2 * n:3 * n], refs[3 * n:4 * n]
        d_r, nm_r, nv_r = refs[4 * n:5 * n], refs[5 * n:6 * n], refs[6 * n:7 * n]
        for i in range(n):
            d_r[i][...], nm_r[i][...], nv_r[i][...] = _adam_update(g_r[i][...], w_r[i][...], m_r[i][...], v_r[i][...])

    vm = pl.BlockSpec(memory_space=pltpu.VMEM)
    shapes = [jax.ShapeDtypeStruct(w.shape, F32) for w in ws]
    outs = pl.pallas_call(body, name=name, in_specs=[vm] * (4 * n), out_specs=[vm] * (3 * n),
                          out_shape=shapes * 3)(*gs, *ws, *ms, *vs)
    return outs[:n], outs[n:2 * n], outs[2 * n:]


FIRST = [("w_in", (1024, 896), "cols"), ("lru_w_r", (4, 32, 256), "mid"), ("lru_w_i", (4, 32, 256), "mid")]
LATE = [("w_ret_o", (128, 1024), "rows"), ("w_lru_o", (128, 1024), "rows"), ("w_out", (128, 1024), "rows"),
        ("ffn_w_up", (1024, 768), "cols"), ("ffn_w_down", (384, 1024), "rows")]
BIG = FIRST + LATE
SMALL_SHARDED = [("merge_gate_b", (2, 128), "cols"), ("lru_conv_w", (4, 128), "cols"), ("lru_b_r", (4, 32), "stack"),
                 ("lru_b_i", (4, 32), "stack"), ("ffn_conv_w", (3, 384), "cols")]
REPLICATED = [("norm1_w", (1, 1024)), ("ret_gn_w", (1, 1024)), ("lru_conv_b", (1, 1024)), ("lru_lambda", (1, 1024)),
              ("norm2_w", (1, 1024)), ("ffn_conv_b", (1, 3072)), ("norm_f_w", (1, 1024))]
MODE = {n: m for n, _, m in BIG}
SHARD = {n: s for n, s, _ in BIG}


def _local_step(x3, positions, target3, first_shards, ws, late_shards, core):
    B, S, D = x3.shape
    T = B * S
    x = x3.reshape(T, D)
    target = target3.reshape(T, D)
    tm = min(512, T)
    big = min(1024, T)
    big2 = min(2048, T)

    half = RET_DK // 2
    inv_freq = ROPE_BASE ** (-jnp.arange(half, dtype=F32) / half)
    inv2 = jnp.concatenate([inv_freq, inv_freq]).reshape(1, RET_DK)
    log_gamma = jnp.log1p(-jnp.power(2.0, -5.0 - jnp.arange(RET_HEADS, dtype=F32)))
    lgam = jnp.broadcast_to(log_gamma[:, None, None], (RET_HEADS, 8, LANES))
    pos_col = positions.astype(F32).reshape(T, 1)
    late_names = [n for n, _, _ in LATE]
    late_modes = [m for _, _, m in LATE]
    late_shapes = [s for _, s, _ in LATE]
    first = FIRST + SMALL_SHARDED
    first_modes = [m for _, _, m in first]

    cos2, sin2s, *first_part = _rope_tables(pos_col, inv2, tm, "rope_tables",
                                            jobs=[_ag_first_job(first_shards, first_modes)])
    h1, *first_full = _rmsnorm_fwd(x, ws["norm1_w"], tm, "norm1_fwd",
                                   jobs=[_ag_second_job(first_part, first_modes, [s for _, s, _ in first])])
    gathered = dict(zip([n for n, _, _ in first], first_full))
    wb = {n: gathered[n] for n, _, _ in FIRST}
    ws = dict(ws, **{n: gathered[n] for n, _, _ in SMALL_SHARDED})
    for n in ("lru_b_r", "lru_b_i"):
        ws[n] = jnp.transpose(ws[n], (1, 0, 2)).reshape(1, LRU_BLOCKS * LRU_BLOCK)
    proj, *late_part = _matmul(h1, wb["w_in"], "nn", BF16, big2, 1024, 1024, "proj_fwd",
                               jobs=[_ag_first_job(late_shards, late_modes)])
    o, a_in, *late_full = _retention_fwd(proj, cos2, sin2s, lgam, ws["ret_gn_w"], B, S, "retention_fwd",
                                         jobs=[_ag_second_job(late_part, late_modes, late_shapes)])
    wb = dict(wb, **dict(zip(late_names, late_full)))
    hl, b_in = _lru_fwd(proj, ws["lru_conv_w"], ws["lru_conv_b"], wb["lru_w_r"], wb["lru_w_i"],
                        ws["lru_b_r"], ws["lru_b_i"], ws["lru_lambda"], B, S, "lru_fwd")
    x1, mix, h2, ya, yb = _mix_fwd(a_in, b_in, proj, x, wb["w_ret_o"], wb["w_lru_o"], wb["w_out"],
                                   ws["merge_gate_b"], ws["norm2_w"], tm, "mix_fwd")
    up = _matmul(h2, wb["ffn_w_up"], "nn", BF16, big2, 1024, 1024, "ffn_up_fwd")[0]
    f = _ffn_act_fwd(up, ws["ffn_conv_w"], ws["ffn_conv_b"], B, S, "ffn_act_fwd")
    x2 = _matmul(f, wb["ffn_w_down"], "nn", F32, big, 1024, D_FF, "ffn_down_fwd", add=x1)[0]
    dx2, dx2b, loss_acc, d_norm_f = _loss_head(x2, target, ws["norm_f_w"], tm, "loss_head")

    g, rs = {}, {}

    def stage1(names, grads):
        return _rs_sibling_job(grads, [MODE[n] for n in names], [SHARD[n] for n in names])

    def add(names, grads, recvs):
        return [_rs_add(gr, r, MODE[n], core, "rs_add_" + n) for n, gr, r in zip(names, grads, recvs)]

    g["norm_f_w"] = d_norm_f
    g_down = _matmul(f, dx2b, "tn", F32, 1024, 1024, big2, "ffn_down_bwd_w")[0]
    df, s1_down = _matmul(dx2b, wb["ffn_w_down"], "nt", BF16, big2, 1024, 1024, "ffn_down_bwd_x",
                          jobs=[stage1(["ffn_w_down"], [g_down])])
    p_down = add(["ffn_w_down"], [g_down], [s1_down])
    dup, g["ffn_conv_w"], g["ffn_conv_b"], s2_down = _ffn_act_bwd(
        df, up, ws["ffn_conv_w"], ws["ffn_conv_b"], B, S, "ffn_act_bwd", jobs=[_rs_chip_job(p_down)])
    rs["ffn_w_down"] = (p_down[0], s2_down)

    g_up = _matmul(h2, dup, "tn", F32, 1024, 1024, big2, "ffn_up_bwd_w")[0]
    dh2, s1_up = _matmul(dup, wb["ffn_w_up"], "nt", BF16, big, 1024, D_FF, "ffn_up_bwd_x",
                         jobs=[stage1(["ffn_w_up"], [g_up])])
    p_up = add(["ffn_w_up"], [g_up], [s1_up])
    dx1, dx1b, g["norm2_w"] = _rmsnorm_bwd_add(dx2, dh2, x1, ws["norm2_w"], tm, "norm2_bwd", True)
    da_in, db_in, dya, dyb, dproj, g["merge_gate_b"] = _mix_bwd(
        dx1b, ya, yb, proj, wb["w_ret_o"], wb["w_lru_o"], wb["w_out"], ws["merge_gate_b"], tm, "mix_bwd")

    mid_names = ["w_out", "w_ret_o", "w_lru_o"]
    g_mid = [_matmul(mix, dx1b, "tn", F32, 1024, 1024, big2, "w_out_bwd_w")[0],
             _matmul(a_in, dya, "tn", F32, 1024, 1024, big2, "w_ret_o_bwd_w")[0],
             _matmul(b_in, dyb, "tn", F32, 1024, 1024, big2, "w_lru_o_bwd_w")[0]]
    (dproj, g["lru_conv_w"], g["lru_conv_b"], g_wr, g_wi, g["lru_b_r"], g["lru_b_i"], g["lru_lambda"], s2_up,
     *s1_mid) = _lru_bwd(db_in, hl, proj, dproj, ws["lru_conv_w"], ws["lru_conv_b"], wb["lru_w_r"], wb["lru_w_i"],
                         ws["lru_b_r"], ws["lru_b_i"], ws["lru_lambda"], B, S, "lru_bwd",
                         jobs=[_rs_chip_job(p_up), stage1(mid_names, g_mid)])
    rs["ffn_w_up"] = (p_up[0], s2_up)
    p_mid = add(mid_names, g_mid, s1_mid)
    lru_names = ["lru_w_r", "lru_w_i"]
    dproj, g["ret_gn_w"], *rest = _retention_bwd(
        da_in, o, proj, dproj, cos2, sin2s, lgam, ws["ret_gn_w"], B, S, "retention_bwd",
        jobs=[_rs_chip_job(p_mid), stage1(lru_names, [g_wr, g_wi])])
    s2_mid, s1_lru = rest[:3], rest[3:]
    for n, p, r in zip(mid_names, p_mid, s2_mid):
        rs[n] = (p, r)
    p_lru = add(lru_names, [g_wr, g_wi], s1_lru)

    g_in, *s2_lru = _matmul(h1, dproj, "tn", F32, 1024, 1024, big2, "proj_bwd_w", jobs=[_rs_chip_job(p_lru)])
    for n, p, r in zip(lru_names, p_lru, s2_lru):
        rs[n] = (p, r)
    s1_in = _pcall(lambda: None, name="rs_sibling_w_in", grid=(1,), in_specs=[], out_specs=[], out_shape=[], args=[],
                   sem=("arbitrary",), jobs=[stage1(["w_in"], [g_in])])
    p_in, p_in_bf16 = _rs_add(g_in, s1_in[0], MODE["w_in"], core, "rs_add_w_in", also_bf16=True)
    dh1, s2_in = _matmul(dproj, wb["w_in"], "nt", BF16, big, 1024, D_IN // 2, "proj_bwd_x",
                         jobs=[_rs_chip_job([p_in_bf16])])
    grad_x, g["norm1_w"] = _rmsnorm_bwd_add(dx1, dh1, x, ws["norm1_w"], tm, "norm1_bwd", False)
    rs["w_in"] = (p_in, s2_in)
    return loss_acc, grad_x.reshape(B, S, D), g, rs


def kernel(x, positions, norm1_w, w_in, merge_gate_b, ret_gn_w, w_ret_o, lru_conv_w, lru_conv_b, lru_w_r, lru_b_r, lru_w_i, lru_b_i, lru_lambda, w_lru_o, w_out, norm2_w, ffn_w_up, ffn_conv_w, ffn_conv_b, ffn_w_down, norm_f_w, loss_target, m_norm1_w, m_w_in, m_merge_gate_b, m_ret_gn_w, m_w_ret_o, m_lru_conv_w, m_lru_conv_b, m_lru_w_r, m_lru_b_r, m_lru_w_i, m_lru_b_i, m_lru_lambda, m_w_lru_o, m_w_out, m_norm2_w, m_ffn_w_up, m_ffn_conv_w, m_ffn_conv_b, m_ffn_w_down, m_norm_f_w, v_norm1_w, v_w_in, v_merge_gate_b, v_ret_gn_w, v_w_ret_o, v_lru_conv_w, v_lru_conv_b, v_lru_w_r, v_lru_b_r, v_lru_w_i, v_lru_b_i, v_lru_lambda, v_w_lru_o, v_w_out, v_norm2_w, v_ffn_w_up, v_ffn_conv_w, v_ffn_conv_b, v_ffn_w_down, v_norm_f_w):
    names = ["norm1_w", "w_in", "merge_gate_b", "ret_gn_w", "w_ret_o", "lru_conv_w", "lru_conv_b", "lru_w_r", "lru_b_r",
             "lru_w_i", "lru_b_i", "lru_lambda", "w_lru_o", "w_out", "norm2_w", "ffn_w_up", "ffn_conv_w", "ffn_conv_b",
             "ffn_w_down", "norm_f_w"]
    w_args = [norm1_w, w_in, merge_gate_b, ret_gn_w, w_ret_o, lru_conv_w, lru_conv_b, lru_w_r, lru_b_r, lru_w_i, lru_b_i,
              lru_lambda, w_lru_o, w_out, norm2_w, ffn_w_up, ffn_conv_w, ffn_conv_b, ffn_w_down, norm_f_w]
    m_args = [m_norm1_w, m_w_in, m_merge_gate_b, m_ret_gn_w, m_w_ret_o, m_lru_conv_w, m_lru_conv_b, m_lru_w_r, m_lru_b_r,
              m_lru_w_i, m_lru_b_i, m_lru_lambda, m_w_lru_o, m_w_out, m_norm2_w, m_ffn_w_up, m_ffn_conv_w, m_ffn_conv_b,
              m_ffn_w_down, m_norm_f_w]
    v_args = [v_norm1_w, v_w_in, v_merge_gate_b, v_ret_gn_w, v_w_ret_o, v_lru_conv_w, v_lru_conv_b, v_lru_w_r, v_lru_b_r,
              v_lru_w_i, v_lru_b_i, v_lru_lambda, v_w_lru_o, v_w_out, v_norm2_w, v_ffn_w_up, v_ffn_conv_w, v_ffn_conv_b,
              v_ffn_w_down, v_norm_f_w]
    orig_shape = {n: a.shape for n, a in zip(names, w_args)}
    local_shape = {n: s for n, s, _ in BIG + SMALL_SHARDED}
    local_shape.update({n: s for n, s in REPLICATED})
    W = {n: a.reshape(local_shape[n]) for n, a in zip(names, w_args)}
    M = {n: a.reshape(local_shape[n]) for n, a in zip(names, m_args)}
    V = {n: a.reshape(local_shape[n]) for n, a in zip(names, v_args)}

    xi, yi, ci = _mesh_pos()
    dev = 4 * xi + 2 * yi + ci
    chip = (2 * xi + yi).astype(jnp.int32).reshape(1)
    core = ci.astype(jnp.int32).reshape(1)

    small_names = [n for n, _, _ in SMALL_SHARDED]
    first_shards = [W[n].astype(BF16) for n, _, _ in FIRST] + [W[n] for n in small_names]
    late_shards = [W[n].astype(BF16) for n, _, _ in LATE]
    rep = {n: W[n] for n, _ in REPLICATED}
    loss_acc, grad_x, g, rs = _local_step(x, positions, loss_target, first_shards, rep, late_shards, core)

    G_out, D_out, M_out, V_out = {}, {}, {}, {}
    for n, _, _ in BIG:
        G_out[n], D_out[n], M_out[n], V_out[n] = _adamw_shard(rs[n][0], rs[n][1], W[n], M[n], V[n], chip, "adamw_" + n)

    rep_names = [n for n, _ in REPLICATED]
    red_names = rep_names + small_names
    red = _small_allreduce([g[n] for n in red_names] + [loss_acc[0:1, :]], "allreduce_small_grads")
    loss = red[-1][0, 0]
    gs = dict(zip(red_names, red[:-1]))
    for n, s, mode in SMALL_SHARDED:
        if mode == "cols":
            gs[n] = lax.dynamic_slice_in_dim(gs[n], dev * s[1], s[1], axis=1)
        else:
            full = gs[n].reshape(LRU_BLOCKS, LRU_BLOCK)
            gs[n] = lax.dynamic_slice_in_dim(full, dev * s[1], s[1], axis=1)
    d2, m2, v2 = _adamw_small([gs[n] for n in red_names], [W[n] for n in red_names], [M[n] for n in red_names],
                              [V[n] for n in red_names], "adamw_small")
    for i, n in enumerate(red_names):
        G_out[n], D_out[n], M_out[n], V_out[n] = gs[n], d2[i], m2[i], v2[i]

    outs = [loss, grad_x]
    for group in (G_out, D_out, M_out, V_out):
        outs += [group[n].reshape(orig_shape[n]) for n in names]
    return tuple(outs)
```

```python
import math

import jax
import jax.numpy as jnp
from jax import lax
from jax.experimental import pallas as pl
from jax.experimental.pallas import tpu as pltpu

F32 = jnp.float32
BF16 = jnp.bfloat16
MESH = pl.DeviceIdType.MESH

D_MODEL = 1024
CHUNK = 64
RET_HEADS = 4
RET_DK = 128
RET_DV = 256
LRU_BLOCKS = 4
LRU_BLOCK = 256
LRU_CONV = 4
LRU_C = 8.0
D_FF = 3072
FFN_CONV = 3
ROPE_BASE = 10000.0
RMS_EPS = 1e-6
GN_EPS = 1e-6
D_IN = 7168
ADAM_LR, ADAM_B1, ADAM_B2, ADAM_EPS, ADAM_WD, ADAM_STEP = 0.001, 0.9, 0.999, 1e-08, 0.01, 10

N_DEV = 8
V7X_VMEM_BYTES = 64 * 1024 * 1024
VMEM_LIMIT = V7X_VMEM_BYTES - 8 * 1024 * 1024
RET_BLOCK = 256
LANES = 128

COL_Q, COL_K = 0, 4
COL_V, COL_G, COL_XL, COL_YL = 4, 8, 12, 16
COL_GR, COL_GL = 5, 6

HBM_SPEC = pl.BlockSpec(memory_space=pl.ANY)


def _gelu(x):
    c = math.sqrt(2.0 / math.pi)
    t = jnp.tanh(x * (c + (c * 0.044715) * (x * x)))
    return x * (0.5 * t + 0.5)


def _gelu_and_grad(x):
    c = math.sqrt(2.0 / math.pi)
    x2 = x * x
    t = jnp.tanh(x * (c + (c * 0.044715) * x2))
    h = 0.5 * t + 0.5
    g = x * h
    dg = h + g * (1.0 - h) * ((2.0 * c) + (6.0 * c * 0.044715) * x2)
    return g, dg


def _sigmoid(x):
    return 0.5 * jnp.tanh(0.5 * x) + 0.5


SUBLANES = 8


def _shift_down(x, s, fill):
    r = pltpu.roll(x, s, 0)
    rows = lax.broadcasted_iota(jnp.int32, (SUBLANES,) + x.shape[1:], 0)
    top = jnp.where(rows >= s, r[:SUBLANES], fill)
    return jnp.concatenate([top, r[SUBLANES:]], axis=0)


def _shift_up(x, s, fill):
    n = x.shape[0]
    r = pltpu.roll(x, n - s, 0)
    rows = lax.broadcasted_iota(jnp.int32, (SUBLANES,) + x.shape[1:], 0)
    bottom = jnp.where(rows < SUBLANES - s, r[n - SUBLANES:], fill)
    return jnp.concatenate([r[:n - SUBLANES], bottom], axis=0)


SCAN_CHUNK = 64


def _scan_forward(a, b):
    n = a.shape[0]
    s = 1
    while s < n:
        if s % SUBLANES:
            b = a * _shift_down(b, s, 0.0) + b
            a = a * _shift_down(a, s, 1.0)
        else:
            b = jnp.concatenate([b[:s], a[s:] * b[:n - s] + b[s:]], axis=0)
            a = jnp.concatenate([a[:s], a[s:] * a[:n - s]], axis=0)
        s *= 2
    return a, b


def _scan_backward(a_next, u):
    n = u.shape[0]
    s = 1
    while s < n:
        if s % SUBLANES:
            u = u + a_next * _shift_up(u, s, 0.0)
            a_next = a_next * _shift_up(a_next, s, 1.0)
        else:
            u = jnp.concatenate([u[:n - s] + a_next[:n - s] * u[s:], u[n - s:]], axis=0)
            a_next = jnp.concatenate([a_next[:n - s] * a_next[s:], a_next[n - s:]], axis=0)
        s *= 2
    return a_next, u


def _scan_forward_ref(a_ref, b_ref, h_ref):
    S, W = a_ref.shape
    for strip in range(W // LANES):
        cols = pl.ds(strip * LANES, LANES)

        def body(k, carry, cols=cols):
            rows = pl.ds(pl.multiple_of(k * SCAN_CHUNK, SCAN_CHUNK), SCAN_CHUNK)
            a_cum, h_loc = _scan_forward(a_ref[rows, cols], b_ref[rows, cols])
            h = h_loc + a_cum * carry
            h_ref[rows, cols] = h
            return h[SCAN_CHUNK - 1:, :]

        lax.fori_loop(0, S // SCAN_CHUNK, body, jnp.zeros((1, LANES), F32))


def _scan_backward_ref(an_ref, u_ref, d_ref):
    S, W = an_ref.shape
    n_chunks = S // SCAN_CHUNK
    for strip in range(W // LANES):
        cols = pl.ds(strip * LANES, LANES)

        def body(i, carry, cols=cols):
            rows = pl.ds(pl.multiple_of((n_chunks - 1 - i) * SCAN_CHUNK, SCAN_CHUNK), SCAN_CHUNK)
            an_cum, d_loc = _scan_backward(an_ref[rows, cols], u_ref[rows, cols])
            d = d_loc + an_cum * carry
            d_ref[rows, cols] = d
            return d[:1, :]

        lax.fori_loop(0, n_chunks, body, jnp.zeros((1, LANES), F32))


def _dot(a, b, dims):
    return lax.dot_general(a, b, (dims, ((), ())), preferred_element_type=F32)


NN = ((1,), (0,))
NT = ((1,), (1,))
TN = ((0,), (0,))


def _mesh_pos():
    return lax.axis_index("x"), lax.axis_index("y"), lax.axis_index("c")


def _other_chips(x, y):
    return [(1 - x, y), (x, 1 - y), (1 - x, 1 - y)]


def _full_shape(shard_shape, mode):
    if mode == "rows":
        return (N_DEV * shard_shape[0],) + tuple(shard_shape[1:])
    if mode == "cols":
        return (shard_shape[0], N_DEV * shard_shape[1])
    if mode == "mid":
        return (shard_shape[0], N_DEV * shard_shape[1], shard_shape[2])
    return (N_DEV,) + tuple(shard_shape)


def _extent(shard_shape, mode):
    return {"rows": shard_shape[0], "cols": shard_shape[1], "mid": shard_shape[1], "stack": 1}[mode]


def _window(ref, mode, extent, d):
    if mode == "stack":
        return ref.at[d]
    start = pl.multiple_of(d * extent, extent)
    if mode == "rows":
        return ref.at[pl.ds(start, extent)]
    if mode == "cols":
        return ref.at[:, pl.ds(start, extent)]
    return ref.at[:, pl.ds(start, extent), :]


class _Job:
    def __init__(self, inputs, out_shapes, sems, start, finish, aliases=None):
        self.inputs, self.out_shapes, self.sems = list(inputs), list(out_shapes), sems
        self.start, self.finish, self.aliases = start, finish, dict(aliases or {})


def _remote(src, dst, send_sem, recv_sem, to):
    return pltpu.make_async_remote_copy(src_ref=src, dst_ref=dst, send_sem=send_sem, recv_sem=recv_sem,
                                        device_id=to, device_id_type=MESH)


def _ag_first_job(shards, modes):
    n = len(shards)
    extents = [_extent(s.shape, m) for s, m in zip(shards, modes)]

    def copies(x_refs, out_refs, send, recv, local, arriving):
        x, y, c = _mesh_pos()
        peers = [(x, y, 1 - c)] + [(*chip, c) for chip in _other_chips(x, y)]
        win = lambda i, p: _window(out_refs[i], modes[i], extents[i], 4 * p[0] + 2 * p[1] + p[2])
        if arriving:
            return [_remote(x_refs[i], win(i, p), send.at[i, k], recv.at[i, k], p)
                    for i in range(n) for k, p in enumerate(peers)]
        mine = [pltpu.make_async_copy(x_refs[i], win(i, (x, y, c)), local.at[i]) for i in range(n)]
        sends = [_remote(x_refs[i], win(i, (x, y, c)), send.at[i, k], recv.at[i, k], p)
                 for i in range(n) for k, p in enumerate(peers)]
        return mine, sends

    def start(*refs):
        mine, sends = copies(*refs, False)
        for cp in mine + sends:
            cp.start()

    def finish(*refs):
        for cp in copies(*refs, True):
            cp.wait_recv()
        mine, sends = copies(*refs, False)
        for cp in sends:
            cp.wait_send()
        for cp in mine:
            cp.wait()

    out_shapes = [jax.ShapeDtypeStruct(_full_shape(s.shape, m), s.dtype) for s, m in zip(shards, modes)]
    return _Job(shards, out_shapes, ((n, 4), (n, 4), (n,)), start, finish)


def _ag_second_job(fulls, modes, shard_shapes):
    n = len(fulls)
    extents = [_extent(s, m) for s, m in zip(shard_shapes, modes)]

    def copies(_, out_refs, send, recv, local, core_of_block):
        x, y, c = _mesh_pos()
        pc = c if core_of_block == "mine" else 1 - c
        win = lambda i, chip: _window(out_refs[i], modes[i], extents[i], 4 * chip[0] + 2 * chip[1] + pc)
        return [_remote(win(i, chip), win(i, chip), send.at[i, j], recv.at[i, j], (x, y, 1 - c))
                for i in range(n) for j, chip in enumerate(_other_chips(x, y))]

    def start(*refs):
        for cp in copies(*refs, "mine"):
            cp.start()

    def finish(*refs):
        for cp in copies(*refs, "sibling"):
            cp.wait_recv()
        for cp in copies(*refs, "mine"):
            cp.wait_send()

    out_shapes = [jax.ShapeDtypeStruct(f.shape, f.dtype) for f in fulls]
    return _Job(fulls, out_shapes, ((n, 3), (n, 3), (1,)), start, finish, aliases={i: i for i in range(n)})


def _rs_sibling_job(grads, modes, shard_shapes):
    n = len(grads)
    extents = [_extent(s, m) for s, m in zip(shard_shapes, modes)]

    def copies(g_refs, out_refs, send, recv, local):
        x, y, c = _mesh_pos()
        return [_remote(_window(g_refs[i], modes[i], extents[i], 2 * k + (1 - c)), out_refs[i].at[k],
                        send.at[i, k], recv.at[i, k], (x, y, 1 - c))
                for i in range(n) for k in range(4)]

    def start(*refs):
        for cp in copies(*refs):
            cp.start()

    def finish(*refs):
        cps = copies(*refs)
        for cp in cps:
            cp.wait_recv()
        for cp in cps:
            cp.wait_send()

    out_shapes = [jax.ShapeDtypeStruct((4,) + tuple(s), g.dtype) for s, g in zip(shard_shapes, grads)]
    return _Job(grads, out_shapes, ((n, 4), (n, 4), (1,)), start, finish)


def _rs_chip_job(partials):
    n = len(partials)

    def copies(p_refs, out_refs, send, recv, local):
        x, y, c = _mesh_pos()
        return [_remote(p_refs[i].at[2 * px + py], out_refs[i].at[j], send.at[i, j], recv.at[i, j], (px, py, c))
                for i in range(n) for j, (px, py) in enumerate(_other_chips(x, y))]

    def start(*refs):
        for cp in copies(*refs):
            cp.start()

    def finish(*refs):
        cps = copies(*refs)
        for cp in cps:
            cp.wait_recv()
        for cp in cps:
            cp.wait_send()

    out_shapes = [jax.ShapeDtypeStruct((3,) + tuple(p.shape[1:]), p.dtype) for p in partials]
    return _Job(partials, out_shapes, ((n, 3), (n, 3), (1,)), start, finish)


def _all_true(conds):
    out = conds[0]
    for c in conds[1:]:
        out = jnp.logical_and(out, c)
    return out


def _pcall(body, *, name, grid, in_specs, out_specs, out_shape, args, sem, scratch=(), jobs=(), alias_in_out=None):
    n_in, n_out, n_scr = len(args), len(out_shape), len(scratch)
    job_in = [a for j in jobs for a in j.inputs]
    job_out = [s for j in jobs for s in j.out_shapes]
    job_sems = [pltpu.SemaphoreType.DMA(shape) for j in jobs for shape in j.sems]
    aliases, in_off, out_off = dict(alias_in_out or {}), n_in, n_out
    for j in jobs:
        for a, b in j.aliases.items():
            aliases[in_off + a] = out_off + b
        in_off += len(j.inputs)
        out_off += len(j.out_shapes)

    def wrapped(*refs):
        ins = refs[:n_in]
        jins = refs[n_in:n_in + len(job_in)]
        o0 = n_in + len(job_in)
        outs = refs[o0:o0 + n_out]
        jouts = refs[o0 + n_out:o0 + n_out + len(job_out)]
        s0 = o0 + n_out + len(job_out)
        scr = refs[s0:s0 + n_scr]
        jsems = refs[s0 + n_scr:]
        if jobs:
            ids = [pl.program_id(a) for a in range(len(grid))]
            first = _all_true([i == 0 for i in ids])
            last = _all_true([i == g - 1 for i, g in zip(ids, grid)])

            def per_job(which):
                i0 = o0_ = 0
                for k, j in enumerate(jobs):
                    fn = j.start if which == "start" else j.finish
                    fn(jins[i0:i0 + len(j.inputs)], jouts[o0_:o0_ + len(j.out_shapes)], *jsems[3 * k:3 * k + 3])
                    i0 += len(j.inputs)
                    o0_ += len(j.out_shapes)

            @pl.when(first)
            def _():
                per_job("start")

        body(*ins, *outs, *scr)
        if jobs:
            @pl.when(last)
            def _():
                per_job("finish")

    semantics = tuple("arbitrary" for _ in grid) if jobs else sem
    return pl.pallas_call(
        wrapped, name=name, grid=grid,
        in_specs=list(in_specs) + [HBM_SPEC] * len(job_in),
        out_specs=list(out_specs) + [HBM_SPEC] * len(job_out),
        out_shape=list(out_shape) + job_out,
        scratch_shapes=list(scratch) + job_sems,
        input_output_aliases=aliases,
        compiler_params=pltpu.CompilerParams(dimension_semantics=semantics, vmem_limit_bytes=VMEM_LIMIT),
    )(*args, *job_in)


def _row_tile(rows, cap):
    if rows <= cap:
        return rows
    best = None
    for t in range(16, cap + 1, 16):
        if rows % t == 0:
            best = t
    assert best is not None
    return best


def _matmul(a, b, mode, out_dtype, tm, tn, tk, name, add=None, jobs=()):
    if mode == "tn":
        K, M = a.shape
    else:
        M, K = a.shape
    N = b.shape[0] if mode == "nt" else b.shape[1]
    tm, tn, tk = min(tm, M), min(tn, N), min(tk, K)
    assert M % tm == 0 and N % tn == 0 and K % tk == 0
    nk = K // tk
    dims = {"nn": NN, "nt": NT, "tn": TN}[mode]

    def body(*refs):
        if add is None:
            a_ref, b_ref, o_ref, acc = refs
            add_ref = None
        else:
            a_ref, b_ref, add_ref, o_ref, acc = refs
        k = pl.program_id(2)
        p = _dot(a_ref[...], b_ref[...], dims)

        def finish(r):
            if add_ref is not None:
                r = r + add_ref[...].astype(F32)
            o_ref[...] = r.astype(out_dtype)

        if nk == 1:
            finish(p)
        else:
            @pl.when(k == 0)
            def _():
                acc[...] = p

            @pl.when(k > 0)
            def _():
                acc[...] += p

            @pl.when(k == nk - 1)
            def _():
                finish(acc[...])

    if mode == "tn":
        a_spec = pl.BlockSpec((tk, tm), lambda i, j, k: (k, i))
    else:
        a_spec = pl.BlockSpec((tm, tk), lambda i, j, k: (i, k))
    if mode == "nt":
        b_spec = pl.BlockSpec((tn, tk), lambda i, j, k: (j, k))
    else:
        b_spec = pl.BlockSpec((tk, tn), lambda i, j, k: (k, j))
    in_specs = [a_spec, b_spec]
    args = [a, b]
    if add is not None:
        in_specs.append(pl.BlockSpec((tm, tn), lambda i, j, k: (i, j)))
        args.append(add)
    return _pcall(
        body, name=name, grid=(M // tm, N // tn, nk), in_specs=in_specs,
        out_specs=[pl.BlockSpec((tm, tn), lambda i, j, k: (i, j))],
        out_shape=[jax.ShapeDtypeStruct((M, N), out_dtype)], args=args,
        scratch=[pltpu.VMEM((tm, tn) if nk > 1 else (8, LANES), F32)],
        sem=("parallel", "parallel", "arbitrary"), jobs=jobs)


def _rmsnorm_fwd(x, w, tm, name, jobs=()):
    T, D = x.shape

    def body(x_ref, w_ref, h_ref):
        xv = x_ref[...]
        r = lax.rsqrt(jnp.mean(xv * xv, axis=-1, keepdims=True) + RMS_EPS)
        h_ref[...] = (xv * r * w_ref[...]).astype(BF16)

    return _pcall(
        body, name=name, grid=(T // tm,),
        in_specs=[pl.BlockSpec((tm, D), lambda i: (i, 0)), pl.BlockSpec((1, D), lambda i: (0, 0))],
        out_specs=[pl.BlockSpec((tm, D), lambda i: (i, 0))],
        out_shape=[jax.ShapeDtypeStruct((T, D), BF16)], args=[x, w], sem=("parallel",), jobs=jobs)


def _rmsnorm_bwd_add(dres, dh, x, w, tm, name, want_bf16, jobs=()):
    T, D = x.shape

    def body(dres_ref, dh_ref, x_ref, w_ref, *outs):
        if want_bf16:
            dx_ref, dxb_ref, dw_ref = outs
        else:
            dx_ref, dw_ref = outs
        i = pl.program_id(0)
        xv = x_ref[...]
        r = lax.rsqrt(jnp.mean(xv * xv, axis=-1, keepdims=True) + RMS_EPS)
        xh = xv * r
        dh_v = dh_ref[...].astype(F32)
        dxh = dh_v * w_ref[...]
        dx = dres_ref[...] + r * (dxh - xh * jnp.mean(dxh * xh, axis=-1, keepdims=True))
        dx_ref[...] = dx
        if want_bf16:
            dxb_ref[...] = dx.astype(BF16)
        part = jnp.sum(dh_v * xh, axis=0, keepdims=True)

        @pl.when(i == 0)
        def _():
            dw_ref[...] = part

        @pl.when(i > 0)
        def _():
            dw_ref[...] += part

    tile = pl.BlockSpec((tm, D), lambda i: (i, 0))
    row = pl.BlockSpec((1, D), lambda i: (0, 0))
    out_specs = [tile] + ([tile] if want_bf16 else []) + [row]
    out_shape = ([jax.ShapeDtypeStruct((T, D), F32)] + ([jax.ShapeDtypeStruct((T, D), BF16)] if want_bf16 else [])
                 + [jax.ShapeDtypeStruct((1, D), F32)])
    return _pcall(body, name=name, grid=(T // tm,), in_specs=[tile, tile, tile, row], out_specs=out_specs,
                  out_shape=out_shape, args=[dres, dh, x, w], sem=("arbitrary",), jobs=jobs)


def _loss_head(x2, target, wf, tm, name):
    T, D = x2.shape

    def body(x_ref, t_ref, w_ref, dx_ref, dxb_ref, loss_ref, dw_ref):
        i = pl.program_id(0)
        xv = x_ref[...]
        r = lax.rsqrt(jnp.mean(xv * xv, axis=-1, keepdims=True) + RMS_EPS)
        xh = xv * r
        wv = w_ref[...]
        e = xh * wv - t_ref[...]
        lpart = 0.5 * jnp.sum(jnp.sum(e * e, axis=-1, keepdims=True), axis=0, keepdims=True) * (1.0 / D)
        dy = e * (1.0 / D)
        dxh = dy * wv
        dx = r * (dxh - xh * jnp.mean(dxh * xh, axis=-1, keepdims=True))
        dx_ref[...] = dx
        dxb_ref[...] = dx.astype(BF16)
        wpart = jnp.sum(dy * xh, axis=0, keepdims=True)
        lfull = jnp.broadcast_to(lpart, (8, LANES))

        @pl.when(i == 0)
        def _():
            loss_ref[...] = lfull
            dw_ref[...] = wpart

        @pl.when(i > 0)
        def _():
            loss_ref[...] += lfull
            dw_ref[...] += wpart

    tile = pl.BlockSpec((tm, D), lambda i: (i, 0))
    row = pl.BlockSpec((1, D), lambda i: (0, 0))
    return _pcall(
        body, name=name, grid=(T // tm,), in_specs=[tile, tile, row],
        out_specs=[tile, tile, pl.BlockSpec((8, LANES), lambda i: (0, 0)), row],
        out_shape=[jax.ShapeDtypeStruct((T, D), F32), jax.ShapeDtypeStruct((T, D), BF16),
                   jax.ShapeDtypeStruct((8, LANES), F32), jax.ShapeDtypeStruct((1, D), F32)],
        args=[x2, target, wf], sem=("arbitrary",))


def _rope_tables(pos_col, inv2, tm, name, jobs=()):
    T = pos_col.shape[0]

    def body(p_ref, f_ref, c_ref, s_ref):
        ang = p_ref[...] * f_ref[...]
        lane = lax.broadcasted_iota(jnp.int32, ang.shape, 1)
        c_ref[...] = jnp.cos(ang)
        s_ref[...] = jnp.where(lane < RET_DK // 2, -1.0, 1.0) * jnp.sin(ang)

    tile = pl.BlockSpec((tm, RET_DK), lambda i: (i, 0))
    return _pcall(
        body, name=name, grid=(T // tm,),
        in_specs=[pl.BlockSpec((tm, 1), lambda i: (i, 0)), pl.BlockSpec((1, RET_DK), lambda i: (0, 0))],
        out_specs=[tile, tile], out_shape=[jax.ShapeDtypeStruct((T, RET_DK), F32)] * 2,
        args=[pos_col, inv2], sem=("parallel",), jobs=jobs)


def _mix_fwd(a_in, b_in, proj, x, w_ro, w_lo, w_out, mb, w2, tm, name):
    T, D = x.shape

    def body(a_ref, b_ref, gr_ref, gl_ref, x_ref, wro_ref, wlo_ref, wout_ref, mb_ref, w2_ref,
             x1_ref, mix_ref, h2_ref, ya_ref, yb_ref):
        ya = _dot(a_ref[...], wro_ref[...], NN)
        yb = _dot(b_ref[...], wlo_ref[...], NN)
        ya_ref[...] = ya.astype(BF16)
        yb_ref[...] = yb.astype(BF16)
        sa = _sigmoid(gr_ref[...].astype(F32) + mb_ref[0:1, :])
        sb = _sigmoid(gl_ref[...].astype(F32) + mb_ref[1:2, :])
        mix = (sa * ya + sb * yb).astype(BF16)
        mix_ref[...] = mix
        x1 = x_ref[...] + _dot(mix, wout_ref[...], NN)
        x1_ref[...] = x1
        r = lax.rsqrt(jnp.mean(x1 * x1, axis=-1, keepdims=True) + RMS_EPS)
        h2_ref[...] = (x1 * r * w2_ref[...]).astype(BF16)

    tile = pl.BlockSpec((tm, D), lambda i: (i, 0))
    wspec = pl.BlockSpec((D, D), lambda i: (0, 0))
    return _pcall(
        body, name=name, grid=(T // tm,),
        in_specs=[tile, tile,
                  pl.BlockSpec((tm, D), lambda i: (i, COL_GR)), pl.BlockSpec((tm, D), lambda i: (i, COL_GL)),
                  tile, wspec, wspec, wspec,
                  pl.BlockSpec((2, D), lambda i: (0, 0)), pl.BlockSpec((1, D), lambda i: (0, 0))],
        out_specs=[tile] * 5,
        out_shape=[jax.ShapeDtypeStruct((T, D), F32)] + [jax.ShapeDtypeStruct((T, D), BF16)] * 4,
        args=[a_in, b_in, proj, proj, x, w_ro, w_lo, w_out, mb, w2], sem=("parallel",))


def _write_pieces(dst_ref, sems, stashes, row0, col0s, ids, grid, compute):
    def aligned(v, m):
        return v if isinstance(v, int) else pl.multiple_of(v, m)

    def copies(slot):
        return [pltpu.make_async_copy(
                    st.at[slot],
                    dst_ref.at[pl.ds(aligned(row0, 16), st.shape[1]), pl.ds(aligned(c0, LANES), st.shape[2])],
                    sems.at[slot, k])
                for k, (st, c0) in enumerate(zip(stashes, col0s))]

    step = ids[0]
    for i, g in zip(ids[1:], grid[1:]):
        step = step * g + i
    slot = step % 2
    last = _all_true([i == g - 1 for i, g in zip(ids, grid)])
    compute(slot)

    @pl.when(step > 0)
    def _():
        for cp in copies(1 - slot):
            cp.wait()

    for cp in copies(slot):
        cp.start()

    @pl.when(last)
    def _():
        for cp in copies(slot):
            cp.wait()


def _mix_bwd(dx1b, ya, yb, proj, w_ro, w_lo, w_out, mb, tm, name, jobs=()):
    T, D = ya.shape
    grid = (T // tm,)

    def body(dx_ref, ya_ref, yb_ref, gr_ref, gl_ref, wro_ref, wlo_ref, wout_ref, mb_ref,
             da_ref, db_ref, dya_ref, dyb_ref, dp_ref, dmb_ref, dgr_s, dgl_s, wsem):
        i = pl.program_id(0)

        def compute(slot):
            dmix = _dot(dx_ref[...], wout_ref[...], NT)
            ya = ya_ref[...].astype(F32)
            yb = yb_ref[...].astype(F32)
            sa = _sigmoid(gr_ref[...].astype(F32) + mb_ref[0:1, :])
            sb = _sigmoid(gl_ref[...].astype(F32) + mb_ref[1:2, :])
            dya = (dmix * sa).astype(BF16)
            dyb = (dmix * sb).astype(BF16)
            dgr = dmix * ya * sa * (1.0 - sa)
            dgl = dmix * yb * sb * (1.0 - sb)
            dya_ref[...] = dya
            dyb_ref[...] = dyb
            dgr_s[slot] = dgr.astype(BF16)
            dgl_s[slot] = dgl.astype(BF16)
            da_ref[...] = _dot(dya, wro_ref[...], NT).astype(BF16)
            db_ref[...] = _dot(dyb, wlo_ref[...], NT).astype(BF16)

            @pl.when(i == 0)
            def _():
                dmb_ref[...] = jnp.zeros_like(dmb_ref)

            dmb_ref[0:1, :] += jnp.sum(dgr, axis=0, keepdims=True)
            dmb_ref[1:2, :] += jnp.sum(dgl, axis=0, keepdims=True)

        _write_pieces(dp_ref, wsem, [dgr_s, dgl_s], i * tm, [COL_GR * D, COL_GL * D], [i], grid, compute)

    tile = pl.BlockSpec((tm, D), lambda i: (i, 0))
    wspec = pl.BlockSpec((D, D), lambda i: (0, 0))
    two = pl.BlockSpec((2, D), lambda i: (0, 0))
    return _pcall(
        body, name=name, grid=grid,
        in_specs=[tile, tile, tile,
                  pl.BlockSpec((tm, D), lambda i: (i, COL_GR)), pl.BlockSpec((tm, D), lambda i: (i, COL_GL)),
                  wspec, wspec, wspec, two],
        out_specs=[tile] * 4 + [HBM_SPEC, two],
        out_shape=[jax.ShapeDtypeStruct((T, D), BF16)] * 4
                  + [jax.ShapeDtypeStruct((T, D_IN), BF16), jax.ShapeDtypeStruct((2, D), F32)],
        args=[dx1b, ya, yb, proj, proj, w_ro, w_lo, w_out, mb],
        scratch=[pltpu.VMEM((2, tm, D), BF16), pltpu.VMEM((2, tm, D), BF16), pltpu.SemaphoreType.DMA((2, 2))],
        sem=("arbitrary",), jobs=jobs)


def _ret_decay_consts(lg):
    L = RET_BLOCK
    n = lax.broadcasted_iota(jnp.int32, (L, L), 0)
    m = lax.broadcasted_iota(jnp.int32, (L, L), 1)
    cn, cm = n // CHUNK, m // CHUNK
    expo = jnp.where(cn == cm, jnp.abs(n - m), n - m).astype(F32)
    wm = jnp.where(cm <= cn, jnp.exp(lg * expo), 0.0)
    idx = lax.broadcasted_iota(jnp.int32, (L, 1), 0).astype(F32)
    qd = jnp.exp(lg * (idx + 1.0))
    kd = jnp.exp(lg * (L - 1.0 - idx))
    bd = jnp.exp(lg * float(L))
    return wm, qd, kd, bd


def _rotate(v, cos2, sin2s):
    return v * cos2 + pltpu.roll(v, RET_DK // 2, 1) * sin2s


def _rotate_t(d, cos2, sin2s):
    return d * cos2 - pltpu.roll(d, RET_DK // 2, 1) * sin2s


def _retention_fwd(proj, cos2, sin2s, lgam, gn_w, B, S, name, jobs=()):
    T = B * S
    nb = S // RET_BLOCK
    scale = RET_DK ** -0.5

    def body(q_ref, k_ref, v_ref, g_ref, c_ref, s_ref, lg_ref, gw_ref, o_ref, a_ref, qr, kr, st):
        wm, qd, kd, bd = _ret_decay_consts(lg_ref[0:1, 0:1])
        cos2, sin2s = c_ref[...], s_ref[...]
        qr[...] = _rotate(q_ref[...].astype(F32), cos2, sin2s)
        kr[...] = _rotate(k_ref[...].astype(F32), cos2, sin2s) * scale
        st[...] = jnp.zeros_like(st)
        gw = gw_ref[...]
        for j in range(nb):
            rows = pl.ds(j * RET_BLOCK, RET_BLOCK)
            qb = qr[rows, :]
            kb = kr[rows, :]
            vb = v_ref[rows, :].astype(BF16)
            sc = _dot(qb.astype(BF16), kb.astype(BF16), NT) * wm
            o = _dot(sc.astype(BF16), vb, NN) + _dot((qb * qd).astype(BF16), st[...].astype(BF16), NN)
            st[...] = st[...] * bd + _dot((kb * kd).astype(BF16), vb, TN)
            o_ref[rows, :] = o
            mu = jnp.mean(o, axis=-1, keepdims=True)
            oc = o - mu
            var = jnp.mean(oc * oc, axis=-1, keepdims=True)
            y = oc * lax.rsqrt(var + GN_EPS) * gw
            g = g_ref[rows, :].astype(F32)
            a_ref[rows, :] = (y * (g * _sigmoid(g))).astype(BF16)

    blk = lambda w, off: pl.BlockSpec((S, w), lambda b, h: (b, off + h))
    return _pcall(
        body, name=name, grid=(B, RET_HEADS),
        in_specs=[blk(RET_DK, COL_Q), blk(RET_DK, COL_K), blk(RET_DV, COL_V), blk(RET_DV, COL_G),
                  pl.BlockSpec((S, RET_DK), lambda b, h: (b, 0)), pl.BlockSpec((S, RET_DK), lambda b, h: (b, 0)),
                  pl.BlockSpec((None, 8, LANES), lambda b, h: (h, 0, 0)),
                  pl.BlockSpec((1, RET_DV), lambda b, h: (0, h))],
        out_specs=[blk(RET_DV, 0), blk(RET_DV, 0)],
        out_shape=[jax.ShapeDtypeStruct((T, RET_HEADS * RET_DV), F32),
                   jax.ShapeDtypeStruct((T, RET_HEADS * RET_DV), BF16)],
        args=[proj, proj, proj, proj, cos2, sin2s, lgam, gn_w],
        scratch=[pltpu.VMEM((S, RET_DK), F32), pltpu.VMEM((S, RET_DK), F32), pltpu.VMEM((RET_DK, RET_DV), F32)],
        sem=("parallel", "parallel"), jobs=jobs)


def _retention_bwd(da_in, o, proj, dproj, cos2, sin2s, lgam, gn_w, B, S, name, jobs=()):
    T = B * S
    nb = S // RET_BLOCK
    scale = RET_DK ** -0.5
    grid = (RET_HEADS, B)

    def body(da_ref, o_ref, q_ref, k_ref, v_ref, g_ref, c_ref, s_ref, lg_ref, gw_ref, _, dp_ref, dgw_ref,
             qr, kr, do_s, sts, rst, dq_s, dk_s, dv_s, dg_s, wsem):
        h, b = pl.program_id(0), pl.program_id(1)

        def compute(slot):
            wm, qd, kd, bd = _ret_decay_consts(lg_ref[0:1, 0:1])
            cos2, sin2s = c_ref[...], s_ref[...]
            qr[...] = _rotate(q_ref[...].astype(F32), cos2, sin2s)
            kr[...] = _rotate(k_ref[...].astype(F32), cos2, sin2s) * scale
            gw = gw_ref[...]
            st = jnp.zeros((RET_DK, RET_DV), F32)
            dgw = jnp.zeros((1, RET_DV), F32)
            for j in range(nb):
                rows = pl.ds(j * RET_BLOCK, RET_BLOCK)
                ov = o_ref[rows, :]
                mu = jnp.mean(ov, axis=-1, keepdims=True)
                oc = ov - mu
                rstd = lax.rsqrt(jnp.mean(oc * oc, axis=-1, keepdims=True) + GN_EPS)
                y = oc * rstd
                g = g_ref[rows, :].astype(F32)
                sg = _sigmoid(g)
                da = da_ref[rows, :].astype(F32)
                dg_s[slot, rows, :] = (da * (y * gw) * (sg * (1.0 + g * (1.0 - sg)))).astype(BF16)
                dyw = da * (g * sg)
                dgw = dgw + jnp.sum(dyw * y, axis=0, keepdims=True)
                dy = dyw * gw
                do_s[rows, :] = rstd * (dy - jnp.mean(dy, axis=-1, keepdims=True)
                                        - y * jnp.mean(dy * y, axis=-1, keepdims=True))
                sts[j] = st
                st = st * bd + _dot((kr[rows, :] * kd).astype(BF16), v_ref[rows, :].astype(BF16), TN)

            @pl.when(b == 0)
            def _():
                dgw_ref[...] = dgw

            @pl.when(b > 0)
            def _():
                dgw_ref[...] += dgw

            rst[...] = jnp.zeros_like(rst)
            for j in reversed(range(nb)):
                rows = pl.ds(j * RET_BLOCK, RET_BLOCK)
                qb = qr[rows, :]
                kb = kr[rows, :]
                qbb, kbb = qb.astype(BF16), kb.astype(BF16)
                vb = v_ref[rows, :].astype(BF16)
                dob = do_s[rows, :]
                dobb = dob.astype(BF16)
                a_m = (_dot(qbb, kbb, NT) * wm).astype(BF16)
                b_m = (_dot(dobb, vb, NT) * wm).astype(BF16)
                rb = rst[...].astype(BF16)
                dq = _dot(b_m, kbb, NN) + _dot((dob * qd).astype(BF16), sts[j].astype(BF16), NT)
                dk = _dot(b_m, qbb, TN) + kd * _dot(vb, rb, NT)
                dv = _dot(a_m, dobb, TN) + kd * _dot(kbb, rb, NN)
                rst[...] = rst[...] * bd + _dot((qb * qd).astype(BF16), dobb, TN)
                cb, sb = c_ref[rows, :], s_ref[rows, :]
                dq_s[slot, rows, :] = _rotate_t(dq, cb, sb).astype(BF16)
                dk_s[slot, rows, :] = _rotate_t(dk * scale, cb, sb).astype(BF16)
                dv_s[slot, rows, :] = dv.astype(BF16)

        cols = [(COL_Q + h) * RET_DK, (COL_K + h) * RET_DK, (COL_V + h) * RET_DV, (COL_G + h) * RET_DV]
        _write_pieces(dp_ref, wsem, [dq_s, dk_s, dv_s, dg_s], b * S, cols, [h, b], grid, compute)

    blk = lambda w, off: pl.BlockSpec((S, w), lambda h, b: (b, off + h))
    return _pcall(
        body, name=name, grid=grid,
        in_specs=[blk(RET_DV, 0), blk(RET_DV, 0),
                  blk(RET_DK, COL_Q), blk(RET_DK, COL_K), blk(RET_DV, COL_V), blk(RET_DV, COL_G),
                  pl.BlockSpec((S, RET_DK), lambda h, b: (b, 0)), pl.BlockSpec((S, RET_DK), lambda h, b: (b, 0)),
                  pl.BlockSpec((None, 8, LANES), lambda h, b: (h, 0, 0)),
                  pl.BlockSpec((1, RET_DV), lambda h, b: (0, h)), HBM_SPEC],
        out_specs=[HBM_SPEC, pl.BlockSpec((1, RET_DV), lambda h, b: (0, h))],
        out_shape=[jax.ShapeDtypeStruct(dproj.shape, dproj.dtype),
                   jax.ShapeDtypeStruct((1, RET_HEADS * RET_DV), F32)],
        args=[da_in, o, proj, proj, proj, proj, cos2, sin2s, lgam, gn_w, dproj],
        scratch=[pltpu.VMEM((S, RET_DK), F32), pltpu.VMEM((S, RET_DK), F32),
                 pltpu.VMEM((S, RET_DV), F32), pltpu.VMEM((nb, RET_DK, RET_DV), F32),
                 pltpu.VMEM((RET_DK, RET_DV), F32),
                 pltpu.VMEM((2, S, RET_DK), BF16), pltpu.VMEM((2, S, RET_DK), BF16),
                 pltpu.VMEM((2, S, RET_DV), BF16), pltpu.VMEM((2, S, RET_DV), BF16), pltpu.SemaphoreType.DMA((2, 4))],
        sem=("arbitrary", "arbitrary"), jobs=jobs, alias_in_out={10: 0})


def _lru_gates(x, cw, cb, wr, wi, br, bi, lam):
    xc = cb + cw[LRU_CONV - 1:LRU_CONV, :] * x
    for j in range(LRU_CONV - 1):
        xc = xc + cw[j:j + 1, :] * _shift_down(x, LRU_CONV - 1 - j, 0.0)
    xcb = xc.astype(BF16)
    r = _sigmoid(_dot(xcb, wr, NN) + br)
    ig = _sigmoid(_dot(xcb, wi, NN) + bi)
    z = -lam
    sp = jnp.maximum(z, 0.0) + jnp.log1p(jnp.exp(-jnp.abs(z)))
    log_a = (-LRU_C) * r * sp
    a = jnp.exp(log_a)
    om = -jnp.tanh(log_a) * (a * a + 1.0)
    sq = jnp.sqrt(om)
    return xc, xcb, r, ig, sp, a, sq


def _lru_fwd(proj, cw, cb, wr, wi, br, bi, lam, B, S, name):
    T = B * S
    W = LRU_BLOCKS * LRU_BLOCK

    def body(x_ref, y_ref, cw_ref, cb_ref, wr_ref, wi_ref, br_ref, bi_ref, lam_ref, h_ref, bin_ref, a_s, b_s):
        xc, _, _, ig, _, a, sq = _lru_gates(x_ref[...].astype(F32), cw_ref[...], cb_ref[...], wr_ref[...], wi_ref[...],
                                           br_ref[...], bi_ref[...], lam_ref[...])
        a_s[...] = a
        b_s[...] = sq * ig * xc
        _scan_forward_ref(a_s, b_s, h_ref)
        bin_ref[...] = (h_ref[...] * _gelu(y_ref[...].astype(F32))).astype(BF16)

    blk = lambda off: pl.BlockSpec((S, LRU_BLOCK), lambda b, n: (b, off + n))
    vec = lambda rows: pl.BlockSpec((rows, LRU_BLOCK), lambda b, n: (0, n))
    wspec = pl.BlockSpec((None, LRU_BLOCK, LRU_BLOCK), lambda b, n: (n, 0, 0))
    return _pcall(
        body, name=name, grid=(B, LRU_BLOCKS),
        in_specs=[blk(COL_XL), blk(COL_YL), vec(LRU_CONV), vec(1), wspec, wspec, vec(1), vec(1), vec(1)],
        out_specs=[blk(0), blk(0)],
        out_shape=[jax.ShapeDtypeStruct((T, W), F32), jax.ShapeDtypeStruct((T, W), BF16)],
        args=[proj, proj, cw, cb, wr, wi, br, bi, lam],
        scratch=[pltpu.VMEM((S, LRU_BLOCK), F32), pltpu.VMEM((S, LRU_BLOCK), F32)], sem=("parallel", "parallel"))


def _lru_bwd(db_in, h, proj, dproj, cw, cb, wr, wi, br, bi, lam, B, S, name, jobs=()):
    T = B * S
    W = LRU_BLOCKS * LRU_BLOCK

    grid = (LRU_BLOCKS, B)

    def body(dbin_ref, h_ref, x_ref, y_ref, cw_ref, cb_ref, wr_ref, wi_ref, br_ref, bi_ref, lam_ref, _,
             dp_ref, dcw_ref, dcb_ref, dwr_ref, dwi_ref, dbr_ref, dbi_ref, dlam_ref, dx_s, dy_s, wsem,
             an_s, u_s, dh_s):
        n, b = pl.program_id(0), pl.program_id(1)

        def compute(slot):
            x = x_ref[...].astype(F32)
            cw = cw_ref[...]
            wr, wi = wr_ref[...], wi_ref[...]
            lam = lam_ref[...]
            xc, xcb, r, ig, sp, a, sq = _lru_gates(x, cw, cb_ref[...], wr, wi, br_ref[...], bi_ref[...], lam)
            hv = h_ref[...]
            gel, dgel = _gelu_and_grad(y_ref[...].astype(F32))
            dbin = dbin_ref[...].astype(F32)
            dy_s[slot] = (dbin * hv * dgel).astype(BF16)
            an_s[...] = _shift_up(a, 1, 0.0)
            u_s[...] = dbin * gel
            _scan_backward_ref(an_s, u_s, dh_s)
            dh = dh_s[...]
            hprev = _shift_down(hv, 1, 0.0)
            dhs = dh * sq
            d_ig = dhs * xc
            d_xc = dhs * ig
            d_loga = (dh * a) * (hprev - (ig * xc) * (a / sq))
            d_r = d_loga * ((-LRU_C) * sp)
            d_sp = jnp.sum(d_loga * ((-LRU_C) * r), axis=0, keepdims=True)
            dlam = -d_sp * _sigmoid(-lam)
            d_pr = d_r * r * (1.0 - r)
            d_pi = d_ig * ig * (1.0 - ig)
            d_prb, d_pib = d_pr.astype(BF16), d_pi.astype(BF16)
            d_xc = d_xc + _dot(d_prb, wr, NT) + _dot(d_pib, wi, NT)

            @pl.when(b == 0)
            def _():
                for ref in (dcw_ref, dcb_ref, dwr_ref, dwi_ref, dbr_ref, dbi_ref, dlam_ref):
                    ref[...] = jnp.zeros_like(ref)

            dx = cw[LRU_CONV - 1:LRU_CONV, :] * d_xc
            for j in range(LRU_CONV - 1):
                sft = LRU_CONV - 1 - j
                dx = dx + cw[j:j + 1, :] * _shift_up(d_xc, sft, 0.0)
                dcw_ref[j:j + 1, :] += jnp.sum(d_xc * _shift_down(x, sft, 0.0), axis=0, keepdims=True)
            dcw_ref[LRU_CONV - 1:LRU_CONV, :] += jnp.sum(d_xc * x, axis=0, keepdims=True)
            dx_s[slot] = dx.astype(BF16)
            dcb_ref[...] += jnp.sum(d_xc, axis=0, keepdims=True)
            dwr_ref[...] += _dot(xcb, d_prb, TN)
            dwi_ref[...] += _dot(xcb, d_pib, TN)
            dbr_ref[...] += jnp.sum(d_pr, axis=0, keepdims=True)
            dbi_ref[...] += jnp.sum(d_pi, axis=0, keepdims=True)
            dlam_ref[...] += dlam

        cols = [(COL_XL + n) * LRU_BLOCK, (COL_YL + n) * LRU_BLOCK]
        _write_pieces(dp_ref, wsem, [dx_s, dy_s], b * S, cols, [n, b], grid, compute)

    blk = lambda off: pl.BlockSpec((S, LRU_BLOCK), lambda n, b: (b, off + n))
    vec = lambda rows: pl.BlockSpec((rows, LRU_BLOCK), lambda n, b: (0, n))
    wspec = pl.BlockSpec((None, LRU_BLOCK, LRU_BLOCK), lambda n, b: (n, 0, 0))
    vshape = lambda rows: jax.ShapeDtypeStruct((rows, W), F32)
    wshape = jax.ShapeDtypeStruct((LRU_BLOCKS, LRU_BLOCK, LRU_BLOCK), F32)
    return _pcall(
        body, name=name, grid=grid,
        in_specs=[blk(0), blk(0), blk(COL_XL), blk(COL_YL), vec(LRU_CONV), vec(1), wspec, wspec, vec(1), vec(1),
                  vec(1), HBM_SPEC],
        out_specs=[HBM_SPEC, vec(LRU_CONV), vec(1), wspec, wspec, vec(1), vec(1), vec(1)],
        out_shape=[jax.ShapeDtypeStruct(dproj.shape, dproj.dtype),
                   vshape(LRU_CONV), vshape(1), wshape, wshape, vshape(1), vshape(1), vshape(1)],
        args=[db_in, h, proj, proj, cw, cb, wr, wi, br, bi, lam, dproj],
        scratch=[pltpu.VMEM((2, S, LRU_BLOCK), BF16), pltpu.VMEM((2, S, LRU_BLOCK), BF16), pltpu.SemaphoreType.DMA((2, 2)),
                 pltpu.VMEM((S, LRU_BLOCK), F32), pltpu.VMEM((S, LRU_BLOCK), F32), pltpu.VMEM((S, LRU_BLOCK), F32)],
        sem=("arbitrary", "arbitrary"), jobs=jobs, alias_in_out={11: 0})


FFN_CT = 256


def _ffn_conv(gate, cw, cb):
    gc = cb + cw[FFN_CONV - 1:FFN_CONV, :] * gate
    for j in range(FFN_CONV - 1):
        gc = gc + cw[j:j + 1, :] * _shift_down(gate, FFN_CONV - 1 - j, 0.0)
    return gc


def _ffn_act_fwd(up, cw, cb, B, S, name):
    T = B * S
    nct = D_FF // FFN_CT

    def body(g_ref, v_ref, cw_ref, cb_ref, f_ref):
        gc = _ffn_conv(g_ref[...].astype(F32), cw_ref[...], cb_ref[...])
        f_ref[...] = (_gelu(gc) * v_ref[...].astype(F32)).astype(BF16)

    return _pcall(
        body, name=name, grid=(B, nct),
        in_specs=[pl.BlockSpec((S, FFN_CT), lambda b, c: (b, c)), pl.BlockSpec((S, FFN_CT), lambda b, c: (b, nct + c)),
                  pl.BlockSpec((FFN_CONV, FFN_CT), lambda b, c: (0, c)), pl.BlockSpec((1, FFN_CT), lambda b, c: (0, c))],
        out_specs=[pl.BlockSpec((S, FFN_CT), lambda b, c: (b, c))],
        out_shape=[jax.ShapeDtypeStruct((T, D_FF), BF16)], args=[up, up, cw, cb], sem=("parallel", "parallel"))[0]


def _ffn_act_bwd(df, up, cw, cb, B, S, name, jobs=()):
    T = B * S
    nct = D_FF // FFN_CT

    grid = (nct, B)

    def body(df_ref, g_ref, v_ref, cw_ref, cb_ref, du_ref, dcw_ref, dcb_ref, dg_s, dv_s, wsem):
        c, b = pl.program_id(0), pl.program_id(1)

        def compute(slot):
            gate = g_ref[...].astype(F32)
            cw = cw_ref[...]
            gc = _ffn_conv(gate, cw, cb_ref[...])
            gel, dgel = _gelu_and_grad(gc)
            dfv = df_ref[...].astype(F32)
            dv_s[slot] = (dfv * gel).astype(BF16)
            dgc = dfv * v_ref[...].astype(F32) * dgel

            @pl.when(b == 0)
            def _():
                dcw_ref[...] = jnp.zeros_like(dcw_ref)
                dcb_ref[...] = jnp.zeros_like(dcb_ref)

            dgate = cw[FFN_CONV - 1:FFN_CONV, :] * dgc
            for j in range(FFN_CONV - 1):
                sft = FFN_CONV - 1 - j
                dgate = dgate + cw[j:j + 1, :] * _shift_up(dgc, sft, 0.0)
                dcw_ref[j:j + 1, :] += jnp.sum(dgc * _shift_down(gate, sft, 0.0), axis=0, keepdims=True)
            dcw_ref[FFN_CONV - 1:FFN_CONV, :] += jnp.sum(dgc * gate, axis=0, keepdims=True)
            dg_s[slot] = dgate.astype(BF16)
            dcb_ref[...] += jnp.sum(dgc, axis=0, keepdims=True)

        _write_pieces(du_ref, wsem, [dg_s, dv_s], b * S, [c * FFN_CT, (nct + c) * FFN_CT], [c, b], grid, compute)

    blk = pl.BlockSpec((S, FFN_CT), lambda c, b: (b, c))
    return _pcall(
        body, name=name, grid=grid,
        in_specs=[blk, blk, pl.BlockSpec((S, FFN_CT), lambda c, b: (b, nct + c)),
                  pl.BlockSpec((FFN_CONV, FFN_CT), lambda c, b: (0, c)),
                  pl.BlockSpec((1, FFN_CT), lambda c, b: (0, c))],
        out_specs=[HBM_SPEC, pl.BlockSpec((FFN_CONV, FFN_CT), lambda c, b: (0, c)),
                   pl.BlockSpec((1, FFN_CT), lambda c, b: (0, c))],
        out_shape=[jax.ShapeDtypeStruct((T, 2 * D_FF), BF16),
                   jax.ShapeDtypeStruct((FFN_CONV, D_FF), F32), jax.ShapeDtypeStruct((1, D_FF), F32)],
        args=[df, up, up, cw, cb],
        scratch=[pltpu.VMEM((2, S, FFN_CT), BF16), pltpu.VMEM((2, S, FFN_CT), BF16), pltpu.SemaphoreType.DMA((2, 2))],
        sem=("arbitrary", "arbitrary"), jobs=jobs)


def _rs_add(g, recv, mode, core, name, also_bf16=False):
    shard = tuple(recv.shape[1:])
    if mode == "mid":
        a, e, c2 = shard
        g_in = g.reshape(a, N_DEV, e, c2)
        grid = (4, 1)
        g_spec = pl.BlockSpec((a, None, e, c2), lambda k, i, c_ref: (0, 2 * k + c_ref[0], 0, 0))
        r_spec = pl.BlockSpec((None, a, e, c2), lambda k, i, c_ref: (k, 0, 0, 0))
    else:
        R, C = shard
        tr = _row_tile(R, 512)
        grid = (4, R // tr)
        if mode == "rows":
            g_in = g.reshape(N_DEV, R, C)
            g_spec = pl.BlockSpec((None, tr, C), lambda k, i, c_ref: (2 * k + c_ref[0], i, 0))
        else:
            g_in = g
            g_spec = pl.BlockSpec((tr, C), lambda k, i, c_ref: (i, 2 * k + c_ref[0]))
        r_spec = pl.BlockSpec((None, tr, C), lambda k, i, c_ref: (k, i, 0))

    def body(c_ref, g_ref, r_ref, o_ref, *ob_ref):
        s = g_ref[...] + r_ref[...]
        o_ref[...] = s
        if also_bf16:
            ob_ref[0][...] = s.astype(BF16)

    out_shape = jax.ShapeDtypeStruct(recv.shape, recv.dtype)
    return pl.pallas_call(
        body, name=name,
        grid_spec=pltpu.PrefetchScalarGridSpec(
            num_scalar_prefetch=1, grid=grid, in_specs=[g_spec, r_spec],
            out_specs=[r_spec, r_spec] if also_bf16 else r_spec),
        out_shape=[out_shape, jax.ShapeDtypeStruct(recv.shape, BF16)] if also_bf16 else out_shape,
        compiler_params=pltpu.CompilerParams(dimension_semantics=("parallel", "parallel"),
                                             vmem_limit_bytes=VMEM_LIMIT),
    )(core, g_in, recv)


def _adam_update(gv, w, m, v):
    nm = ADAM_B1 * m + (1.0 - ADAM_B1) * gv
    nv = ADAM_B2 * v + (1.0 - ADAM_B2) * (gv * gv)
    m_hat = nm / (1.0 - ADAM_B1 ** ADAM_STEP)
    v_hat = nv / (1.0 - ADAM_B2 ** ADAM_STEP)
    delta = -ADAM_LR * (m_hat / (jnp.sqrt(v_hat) + ADAM_EPS) + ADAM_WD * w)
    return delta, nm, nv


def _adamw_shard(partial, recv, w, m, v, chip, name):
    shape = tuple(w.shape)
    tr = _row_tile(shape[0], 256)
    rest = shape[1:]
    zeros = (0,) * len(rest)
    tile = pl.BlockSpec((tr,) + rest, lambda i, s: (i,) + zeros)

    def body(_, p_ref, r_ref, w_ref, m_ref, v_ref, g_ref, d_ref, nm_ref, nv_ref):
        gv = p_ref[...] + r_ref[0].astype(F32) + r_ref[1].astype(F32) + r_ref[2].astype(F32)
        g_ref[...] = gv
        d_ref[...], nm_ref[...], nv_ref[...] = _adam_update(gv, w_ref[...], m_ref[...], v_ref[...])

    grid_spec = pltpu.PrefetchScalarGridSpec(
        num_scalar_prefetch=1, grid=(shape[0] // tr,),
        in_specs=[pl.BlockSpec((None, tr) + rest, lambda i, s: (s[0], i) + zeros),
                  pl.BlockSpec((3, tr) + rest, lambda i, s: (0, i) + zeros), tile, tile, tile],
        out_specs=[tile] * 4)
    return pl.pallas_call(
        body, name=name, grid_spec=grid_spec, out_shape=[jax.ShapeDtypeStruct(shape, F32)] * 4,
        compiler_params=pltpu.CompilerParams(dimension_semantics=("parallel",), vmem_limit_bytes=VMEM_LIMIT),
    )(chip, partial, recv, w, m, v)


SMALL_LANES = 1024


def _small_rows(shape):
    r, w = shape
    return r * max(1, w // SMALL_LANES)


def _small_allreduce(parts, name):
    n = len(parts)
    shapes = [tuple(p.shape) for p in parts]
    offs, total = [], 0
    for s in shapes:
        offs.append(total)
        total += _small_rows(s)
    rows = -(-total // 8) * 8

    def body(*refs):
        p_refs, o_refs = refs[:n], refs[n:2 * n]
        buf, tot, send_sems, recv_sems = refs[2 * n:]
        x, y, c = _mesh_pos()
        me, sibling = (x, y, c), (x, y, 1 - c)
        chips = _other_chips(x, y)

        def slot(px, py, pc):
            return buf.at[4 * px + 2 * py + pc]

        def copy(k, block, to):
            return _remote(slot(*block), slot(*block), send_sems.at[k], recv_sems.at[k], to)

        tot[...] = jnp.zeros_like(tot)
        for p_ref, (r, w), off in zip(p_refs, shapes, offs):
            wl = min(w, SMALL_LANES)
            for part in range(max(1, w // SMALL_LANES)):
                tot[pl.ds(off + part * r, r), pl.ds(0, wl)] = p_ref[:, pl.ds(part * SMALL_LANES, wl)]
        buf[4 * x + 2 * y + c] = tot[...]
        first = [copy(0, me, sibling)] + [copy(1 + j, me, (*chip, c)) for j, chip in enumerate(chips)]
        for cp in first:
            cp.start()
        passed = [copy(4 + j, (*chip, c), sibling) for j, chip in enumerate(chips)]
        for j, chip in enumerate(chips):
            copy(1 + j, (*chip, c), me).wait_recv()
            passed[j].start()
        copy(0, sibling, me).wait_recv()
        for j, chip in enumerate(chips):
            copy(4 + j, (*chip, 1 - c), me).wait_recv()
        for cp in first + passed:
            cp.wait_send()
        acc = buf[0]
        for d in range(1, N_DEV):
            acc = acc + buf[d]
        tot[...] = acc
        for o_ref, (r, w), off in zip(o_refs, shapes, offs):
            wl = min(w, SMALL_LANES)
            for part in range(max(1, w // SMALL_LANES)):
                o_ref[:, pl.ds(part * SMALL_LANES, wl)] = tot[pl.ds(off + part * r, r), pl.ds(0, wl)]

    vm = pl.BlockSpec(memory_space=pltpu.VMEM)
    return pl.pallas_call(
        body, name=name,
        in_specs=[vm] * n, out_specs=[vm] * n,
        out_shape=[jax.ShapeDtypeStruct(s, F32) for s in shapes],
        scratch_shapes=[pltpu.VMEM((N_DEV, rows, SMALL_LANES), F32), pltpu.VMEM((rows, SMALL_LANES), F32),
                        pltpu.SemaphoreType.DMA((7,)), pltpu.SemaphoreType.DMA((7,))],
    )(*parts)


def _adamw_small(gs, ws, ms, vs, name):
    n = len(gs)

    def body(*refs):
        g_r, w_r, m_r, v_r = refs[:n], refs[n:2 * n], refs[2 * n:3 * n], refs[3 * n:4 * n]
        d_r, nm_r, nv_r = refs[4 * n:5 * n], refs[5 * n:6 * n], refs[6 * n:7 * n]
        for i in range(n):
            d_r[i][...], nm_r[i][...], nv_r[i][...] = _adam_update(g_r[i][...], w_r[i][...], m_r[i][...], v_r[i][...])

    vm = pl.BlockSpec(memory_space=pltpu.VMEM)
    shapes = [jax.ShapeDtypeStruct(w.shape, F32) for w in ws]
    outs = pl.pallas_call(body, name=name, in_specs=[vm] * (4 * n), out_specs=[vm] * (3 * n),
                          out_shape=shapes * 3)(*gs, *ws, *ms, *vs)
    return outs[:n], outs[n:2 * n], outs[2 * n:]


FIRST = [("w_in", (1024, 896), "cols"), ("lru_w_r", (4, 32, 256), "mid"), ("lru_w_i", (4, 32, 256), "mid")]
LATE = [("w_ret_o", (128, 1024), "rows"), ("w_lru_o", (128, 1024), "rows"), ("w_out", (128, 1024), "rows"),
        ("ffn_w_up", (1024, 768), "cols"), ("ffn_w_down", (384, 1024), "rows")]
BIG = FIRST + LATE
SMALL_SHARDED = [("merge_gate_b", (2, 128), "cols"), ("lru_conv_w", (4, 128), "cols"), ("lru_b_r", (4, 32), "stack"),
                 ("lru_b_i", (4, 32), "stack"), ("ffn_conv_w", (3, 384), "cols")]
REPLICATED = [("norm1_w", (1, 1024)), ("ret_gn_w", (1, 1024)), ("lru_conv_b", (1, 1024)), ("lru_lambda", (1, 1024)),
              ("norm2_w", (1, 1024)), ("ffn_conv_b", (1, 3072)), ("norm_f_w", (1, 1024))]
MODE = {n: m for n, _, m in BIG}
SHARD = {n: s for n, s, _ in BIG}


def _local_step(x3, positions, target3, first_shards, ws, late_shards, core):
    B, S, D = x3.shape
    T = B * S
    x = x3.reshape(T, D)
    target = target3.reshape(T, D)
    tm = min(512, T)
    big = min(1024, T)
    big2 = min(2048, T)
    big4 = min(4096, T)

    half = RET_DK // 2
    inv_freq = ROPE_BASE ** (-jnp.arange(half, dtype=F32) / half)
    inv2 = jnp.concatenate([inv_freq, inv_freq]).reshape(1, RET_DK)
    log_gamma = jnp.log1p(-jnp.power(2.0, -5.0 - jnp.arange(RET_HEADS, dtype=F32)))
    lgam = jnp.broadcast_to(log_gamma[:, None, None], (RET_HEADS, 8, LANES))
    pos_col = positions.astype(F32).reshape(T, 1)
    late_names = [n for n, _, _ in LATE]
    late_modes = [m for _, _, m in LATE]
    late_shapes = [s for _, s, _ in LATE]
    first = FIRST + SMALL_SHARDED
    first_modes = [m for _, _, m in first]

    cos2, sin2s, *first_part = _rope_tables(pos_col, inv2, tm, "rope_tables",
                                            jobs=[_ag_first_job(first_shards, first_modes)])
    h1, *first_full = _rmsnorm_fwd(x, ws["norm1_w"], tm, "norm1_fwd",
                                   jobs=[_ag_second_job(first_part, first_modes, [s for _, s, _ in first])])
    gathered = dict(zip([n for n, _, _ in first], first_full))
    wb = {n: gathered[n] for n, _, _ in FIRST}
    ws = dict(ws, **{n: gathered[n] for n, _, _ in SMALL_SHARDED})
    for n in ("lru_b_r", "lru_b_i"):
        ws[n] = jnp.transpose(ws[n], (1, 0, 2)).reshape(1, LRU_BLOCKS * LRU_BLOCK)
    proj, *late_part = _matmul(h1, wb["w_in"], "nn", BF16, big2, 1024, 1024, "proj_fwd",
                               jobs=[_ag_first_job(late_shards, late_modes)])
    o, a_in, *late_full = _retention_fwd(proj, cos2, sin2s, lgam, ws["ret_gn_w"], B, S, "retention_fwd",
                                         jobs=[_ag_second_job(late_part, late_modes, late_shapes)])
    wb = dict(wb, **dict(zip(late_names, late_full)))
    hl, b_in = _lru_fwd(proj, ws["lru_conv_w"], ws["lru_conv_b"], wb["lru_w_r"], wb["lru_w_i"],
                        ws["lru_b_r"], ws["lru_b_i"], ws["lru_lambda"], B, S, "lru_fwd")
    x1, mix, h2, ya, yb = _mix_fwd(a_in, b_in, proj, x, wb["w_ret_o"], wb["w_lru_o"], wb["w_out"],
                                   ws["merge_gate_b"], ws["norm2_w"], tm, "mix_fwd")
    up = _matmul(h2, wb["ffn_w_up"], "nn", BF16, big2, 1024, 1024, "ffn_up_fwd")[0]
    f = _ffn_act_fwd(up, ws["ffn_conv_w"], ws["ffn_conv_b"], B, S, "ffn_act_fwd")
    x2 = _matmul(f, wb["ffn_w_down"], "nn", F32, big, 1024, D_FF, "ffn_down_fwd", add=x1)[0]
    dx2, dx2b, loss_acc, d_norm_f = _loss_head(x2, target, ws["norm_f_w"], tm, "loss_head")

    g, rs = {}, {}

    def stage1(names, grads):
        return _rs_sibling_job(grads, [MODE[n] for n in names], [SHARD[n] for n in names])

    def add(names, grads, recvs):
        return [_rs_add(gr, r, MODE[n], core, "rs_add_" + n) for n, gr, r in zip(names, grads, recvs)]

    g["norm_f_w"] = d_norm_f
    g_down = _matmul(f, dx2b, "tn", F32, 1024, 1024, big4, "ffn_down_bwd_w")[0]
    df, s1_down = _matmul(dx2b, wb["ffn_w_down"], "nt", BF16, big2, 1024, 1024, "ffn_down_bwd_x",
                          jobs=[stage1(["ffn_w_down"], [g_down])])
    p_down = add(["ffn_w_down"], [g_down], [s1_down])
    dup, g["ffn_conv_w"], g["ffn_conv_b"], s2_down = _ffn_act_bwd(
        df, up, ws["ffn_conv_w"], ws["ffn_conv_b"], B, S, "ffn_act_bwd", jobs=[_rs_chip_job(p_down)])
    rs["ffn_w_down"] = (p_down[0], s2_down)

    g_up = _matmul(h2, dup, "tn", F32, 1024, 1024, big4, "ffn_up_bwd_w")[0]
    dh2, s1_up = _matmul(dup, wb["ffn_w_up"], "nt", BF16, big, 1024, D_FF, "ffn_up_bwd_x",
                         jobs=[stage1(["ffn_w_up"], [g_up])])
    p_up = add(["ffn_w_up"], [g_up], [s1_up])
    dx1, dx1b, g["norm2_w"] = _rmsnorm_bwd_add(dx2, dh2, x1, ws["norm2_w"], tm, "norm2_bwd", True)
    da_in, db_in, dya, dyb, dproj, g["merge_gate_b"] = _mix_bwd(
        dx1b, ya, yb, proj, wb["w_ret_o"], wb["w_lru_o"], wb["w_out"], ws["merge_gate_b"], tm, "mix_bwd")

    mid_names = ["w_out", "w_ret_o", "w_lru_o"]
    g_mid = [_matmul(mix, dx1b, "tn", F32, 1024, 1024, big4, "w_out_bwd_w")[0],
             _matmul(a_in, dya, "tn", F32, 1024, 1024, big4, "w_ret_o_bwd_w")[0],
             _matmul(b_in, dyb, "tn", F32, 1024, 1024, big4, "w_lru_o_bwd_w")[0]]
    (dproj, g["lru_conv_w"], g["lru_conv_b"], g_wr, g_wi, g["lru_b_r"], g["lru_b_i"], g["lru_lambda"], s2_up,
     *s1_mid) = _lru_bwd(db_in, hl, proj, dproj, ws["lru_conv_w"], ws["lru_conv_b"], wb["lru_w_r"], wb["lru_w_i"],
                         ws["lru_b_r"], ws["lru_b_i"], ws["lru_lambda"], B, S, "lru_bwd",
                         jobs=[_rs_chip_job(p_up), stage1(mid_names, g_mid)])
    rs["ffn_w_up"] = (p_up[0], s2_up)
    p_mid = add(mid_names, g_mid, s1_mid)
    lru_names = ["lru_w_r", "lru_w_i"]
    dproj, g["ret_gn_w"], *rest = _retention_bwd(
        da_in, o, proj, dproj, cos2, sin2s, lgam, ws["ret_gn_w"], B, S, "retention_bwd",
        jobs=[_rs_chip_job(p_mid), stage1(lru_names, [g_wr, g_wi])])
    s2_mid, s1_lru = rest[:3], rest[3:]
    for n, p, r in zip(mid_names, p_mid, s2_mid):
        rs[n] = (p, r)
    p_lru = add(lru_names, [g_wr, g_wi], s1_lru)

    g_in, *s2_lru = _matmul(h1, dproj, "tn", F32, 1024, 1024, big4, "proj_bwd_w", jobs=[_rs_chip_job(p_lru)])
    for n, p, r in zip(lru_names, p_lru, s2_lru):
        rs[n] = (p, r)
    s1_in = _pcall(lambda: None, name="rs_sibling_w_in", grid=(1,), in_specs=[], out_specs=[], out_shape=[], args=[],
                   sem=("arbitrary",), jobs=[stage1(["w_in"], [g_in])])
    p_in, p_in_bf16 = _rs_add(g_in, s1_in[0], MODE["w_in"], core, "rs_add_w_in", also_bf16=True)
    dh1, s2_in = _matmul(dproj, wb["w_in"], "nt", BF16, big, 1024, D_IN // 2, "proj_bwd_x",
                         jobs=[_rs_chip_job([p_in_bf16])])
    grad_x, g["norm1_w"] = _rmsnorm_bwd_add(dx1, dh1, x, ws["norm1_w"], tm, "norm1_bwd", False)
    rs["w_in"] = (p_in, s2_in)
    return loss_acc, grad_x.reshape(B, S, D), g, rs


def kernel(x, positions, norm1_w, w_in, merge_gate_b, ret_gn_w, w_ret_o, lru_conv_w, lru_conv_b, lru_w_r, lru_b_r, lru_w_i, lru_b_i, lru_lambda, w_lru_o, w_out, norm2_w, ffn_w_up, ffn_conv_w, ffn_conv_b, ffn_w_down, norm_f_w, loss_target, m_norm1_w, m_w_in, m_merge_gate_b, m_ret_gn_w, m_w_ret_o, m_lru_conv_w, m_lru_conv_b, m_lru_w_r, m_lru_b_r, m_lru_w_i, m_lru_b_i, m_lru_lambda, m_w_lru_o, m_w_out, m_norm2_w, m_ffn_w_up, m_ffn_conv_w, m_ffn_conv_b, m_ffn_w_down, m_norm_f_w, v_norm1_w, v_w_in, v_merge_gate_b, v_ret_gn_w, v_w_ret_o, v_lru_conv_w, v_lru_conv_b, v_lru_w_r, v_lru_b_r, v_lru_w_i, v_lru_b_i, v_lru_lambda, v_w_lru_o, v_w_out, v_norm2_w, v_ffn_w_up, v_ffn_conv_w, v_ffn_conv_b, v_ffn_w_down, v_norm_f_w):
    names = ["norm1_w", "w_in", "merge_gate_b", "ret_gn_w", "w_ret_o", "lru_conv_w", "lru_conv_b", "lru_w_r", "lru_b_r",
             "lru_w_i", "lru_b_i", "lru_lambda", "w_lru_o", "w_out", "norm2_w", "ffn_w_up", "ffn_conv_w", "ffn_conv_b",
             "ffn_w_down", "norm_f_w"]
    w_args = [norm1_w, w_in, merge_gate_b, ret_gn_w, w_ret_o, lru_conv_w, lru_conv_b, lru_w_r, lru_b_r, lru_w_i, lru_b_i,
              lru_lambda, w_lru_o, w_out, norm2_w, ffn_w_up, ffn_conv_w, ffn_conv_b, ffn_w_down, norm_f_w]
    m_args = [m_norm1_w, m_w_in, m_merge_gate_b, m_ret_gn_w, m_w_ret_o, m_lru_conv_w, m_lru_conv_b, m_lru_w_r, m_lru_b_r,
              m_lru_w_i, m_lru_b_i, m_lru_lambda, m_w_lru_o, m_w_out, m_norm2_w, m_ffn_w_up, m_ffn_conv_w, m_ffn_conv_b,
              m_ffn_w_down, m_norm_f_w]
    v_args = [v_norm1_w, v_w_in, v_merge_gate_b, v_ret_gn_w, v_w_ret_o, v_lru_conv_w, v_lru_conv_b, v_lru_w_r, v_lru_b_r,
              v_lru_w_i, v_lru_b_i, v_lru_lambda, v_w_lru_o, v_w_out, v_norm2_w, v_ffn_w_up, v_ffn_conv_w, v_ffn_conv_b,
              v_ffn_w_down, v_norm_f_w]
    orig_shape = {n: a.shape for n, a in zip(names, w_args)}
    local_shape = {n: s for n, s, _ in BIG + SMALL_SHARDED}
    local_shape.update({n: s for n, s in REPLICATED})
    W = {n: a.reshape(local_shape[n]) for n, a in zip(names, w_args)}
    M = {n: a.reshape(local_shape[n]) for n, a in zip(names, m_args)}
    V = {n: a.reshape(local_shape[n]) for n, a in zip(names, v_args)}

    xi, yi, ci = _mesh_pos()
    dev = 4 * xi + 2 * yi + ci
    chip = (2 * xi + yi).astype(jnp.int32).reshape(1)
    core = ci.astype(jnp.int32).reshape(1)

    small_names = [n for n, _, _ in SMALL_SHARDED]
    first_shards = [W[n].astype(BF16) for n, _, _ in FIRST] + [W[n] for n in small_names]
    late_shards = [W[n].astype(BF16) for n, _, _ in LATE]
    rep = {n: W[n] for n, _ in REPLICATED}
    loss_acc, grad_x, g, rs = _local_step(x, positions, loss_target, first_shards, rep, late_shards, core)

    G_out, D_out, M_out, V_out = {}, {}, {}, {}
    for n, _, _ in BIG:
        G_out[n], D_out[n], M_out[n], V_out[n] = _adamw_shard(rs[n][0], rs[n][1], W[n], M[n], V[n], chip, "adamw_" + n)

    rep_names = [n for n, _ in REPLICATED]
    red_names = rep_names + small_names
    red = _small_allreduce([g[n] for n in red_names] + [loss_acc[0:1, :]], "allreduce_small_grads")
    loss = red[-1][0, 0]
    gs = dict(zip(red_names, red[:-1]))
    for n, s, mode in SMALL_SHARDED:
        if mode == "cols":
            gs[n] = lax.dynamic_slice_in_dim(gs[n], dev * s[1], s[1], axis=1)
        else:
            full = gs[n].reshape(LRU_BLOCKS, LRU_BLOCK)
            gs[n] = lax.dynamic_slice_in_dim(full, dev * s[1], s[1], axis=1)
    d2, m2, v2 = _adamw_small([gs[n] for n in red_names], [W[n] for n in red_names], [M[n] for n in red_names],
                              [V[n] for n in red_names], "adamw_small")
    for i, n in enumerate(red_names):
        G_out[n], D_out[n], M_out[n], V_out[n] = gs[n], d2[i], m2[i], v2[i]

    outs = [loss, grad_x]
    for group in (G_out, D_out, M_out, V_out):
        outs += [group[n].reshape(orig_shape[n]) for n in names]
    return tuple(outs)
```

```python
import math

import jax
import jax.numpy as jnp
from jax import lax
from jax.experimental import pallas as pl
from jax.experimental.pallas import tpu as pltpu

F32 = jnp.float32
BF16 = jnp.bfloat16
MESH = pl.DeviceIdType.MESH

D_MODEL = 1024
CHUNK = 64
RET_HEADS = 4
RET_DK = 128
RET_DV = 256
LRU_BLOCKS = 4
LRU_BLOCK = 256
LRU_CONV = 4
LRU_C = 8.0
D_FF = 3072
FFN_CONV = 3
ROPE_BASE = 10000.0
RMS_EPS = 1e-6
GN_EPS = 1e-6
D_IN = 7168
ADAM_LR, ADAM_B1, ADAM_B2, ADAM_EPS, ADAM_WD, ADAM_STEP = 0.001, 0.9, 0.999, 1e-08, 0.01, 10

N_DEV = 8
V7X_VMEM_BYTES = 64 * 1024 * 1024
VMEM_LIMIT = V7X_VMEM_BYTES - 8 * 1024 * 1024
RET_BLOCK = 256
LANES = 128

COL_Q, COL_K = 0, 4
COL_V, COL_G, COL_XL, COL_YL = 4, 8, 12, 16
COL_GR, COL_GL = 5, 6

HBM_SPEC = pl.BlockSpec(memory_space=pl.ANY)


def _gelu(x):
    c = math.sqrt(2.0 / math.pi)
    t = jnp.tanh(x * (c + (c * 0.044715) * (x * x)))
    return x * (0.5 * t + 0.5)


def _gelu_and_grad(x):
    c = math.sqrt(2.0 / math.pi)
    x2 = x * x
    t = jnp.tanh(x * (c + (c * 0.044715) * x2))
    h = 0.5 * t + 0.5
    g = x * h
    dg = h + g * (1.0 - h) * ((2.0 * c) + (6.0 * c * 0.044715) * x2)
    return g, dg


def _sigmoid(x):
    return 0.5 * jnp.tanh(0.5 * x) + 0.5


SUBLANES = 8


def _shift_down(x, s, fill):
    r = pltpu.roll(x, s, 0)
    rows = lax.broadcasted_iota(jnp.int32, (SUBLANES,) + x.shape[1:], 0)
    top = jnp.where(rows >= s, r[:SUBLANES], fill)
    return jnp.concatenate([top, r[SUBLANES:]], axis=0)


def _shift_up(x, s, fill):
    n = x.shape[0]
    r = pltpu.roll(x, n - s, 0)
    rows = lax.broadcasted_iota(jnp.int32, (SUBLANES,) + x.shape[1:], 0)
    bottom = jnp.where(rows < SUBLANES - s, r[n - SUBLANES:], fill)
    return jnp.concatenate([r[:n - SUBLANES], bottom], axis=0)


SCAN_CHUNK = 64


def _scan_forward(a, b):
    n = a.shape[0]
    s = 1
    while s < n:
        if s % SUBLANES:
            b = a * _shift_down(b, s, 0.0) + b
            a = a * _shift_down(a, s, 1.0)
        else:
            b = jnp.concatenate([b[:s], a[s:] * b[:n - s] + b[s:]], axis=0)
            a = jnp.concatenate([a[:s], a[s:] * a[:n - s]], axis=0)
        s *= 2
    return a, b


def _scan_backward(a_next, u):
    n = u.shape[0]
    s = 1
    while s < n:
        if s % SUBLANES:
            u = u + a_next * _shift_up(u, s, 0.0)
            a_next = a_next * _shift_up(a_next, s, 1.0)
        else:
            u = jnp.concatenate([u[:n - s] + a_next[:n - s] * u[s:], u[n - s:]], axis=0)
            a_next = jnp.concatenate([a_next[:n - s] * a_next[s:], a_next[n - s:]], axis=0)
        s *= 2
    return a_next, u


def _scan_forward_ref(a_ref, b_ref, h_ref):
    S, W = a_ref.shape
    for strip in range(W // LANES):
        cols = pl.ds(strip * LANES, LANES)

        def body(k, carry, cols=cols):
            rows = pl.ds(pl.multiple_of(k * SCAN_CHUNK, SCAN_CHUNK), SCAN_CHUNK)
            a_cum, h_loc = _scan_forward(a_ref[rows, cols], b_ref[rows, cols])
            h = h_loc + a_cum * carry
            h_ref[rows, cols] = h
            return h[SCAN_CHUNK - 1:, :]

        lax.fori_loop(0, S // SCAN_CHUNK, body, jnp.zeros((1, LANES), F32))


def _scan_backward_ref(an_ref, u_ref, d_ref):
    S, W = an_ref.shape
    n_chunks = S // SCAN_CHUNK
    for strip in range(W // LANES):
        cols = pl.ds(strip * LANES, LANES)

        def body(i, carry, cols=cols):
            rows = pl.ds(pl.multiple_of((n_chunks - 1 - i) * SCAN_CHUNK, SCAN_CHUNK), SCAN_CHUNK)
            an_cum, d_loc = _scan_backward(an_ref[rows, cols], u_ref[rows, cols])
            d = d_loc + an_cum * carry
            d_ref[rows, cols] = d
            return d[:1, :]

        lax.fori_loop(0, n_chunks, body, jnp.zeros((1, LANES), F32))


def _dot(a, b, dims):
    return lax.dot_general(a, b, (dims, ((), ())), preferred_element_type=F32)


NN = ((1,), (0,))
NT = ((1,), (1,))
TN = ((0,), (0,))


def _mesh_pos():
    return lax.axis_index("x"), lax.axis_index("y"), lax.axis_index("c")


def _other_chips(x, y):
    return [(1 - x, y), (x, 1 - y), (1 - x, 1 - y)]


def _full_shape(shard_shape, mode):
    if mode == "rows":
        return (N_DEV * shard_shape[0],) + tuple(shard_shape[1:])
    if mode == "cols":
        return (shard_shape[0], N_DEV * shard_shape[1])
    if mode == "mid":
        return (shard_shape[0], N_DEV * shard_shape[1], shard_shape[2])
    return (N_DEV,) + tuple(shard_shape)


def _extent(shard_shape, mode):
    return {"rows": shard_shape[0], "cols": shard_shape[1], "mid": shard_shape[1], "stack": 1}[mode]


def _window(ref, mode, extent, d):
    if mode == "stack":
        return ref.at[d]
    start = pl.multiple_of(d * extent, extent)
    if mode == "rows":
        return ref.at[pl.ds(start, extent)]
    if mode == "cols":
        return ref.at[:, pl.ds(start, extent)]
    return ref.at[:, pl.ds(start, extent), :]


class _Job:
    def __init__(self, inputs, out_shapes, sems, start, finish, aliases=None):
        self.inputs, self.out_shapes, self.sems = list(inputs), list(out_shapes), sems
        self.start, self.finish, self.aliases = start, finish, dict(aliases or {})


def _remote(src, dst, send_sem, recv_sem, to):
    return pltpu.make_async_remote_copy(src_ref=src, dst_ref=dst, send_sem=send_sem, recv_sem=recv_sem,
                                        device_id=to, device_id_type=MESH)


def _ag_first_job(shards, modes):
    n = len(shards)
    extents = [_extent(s.shape, m) for s, m in zip(shards, modes)]

    def copies(x_refs, out_refs, send, recv, local, arriving):
        x, y, c = _mesh_pos()
        peers = [(x, y, 1 - c)] + [(*chip, c) for chip in _other_chips(x, y)]
        win = lambda i, p: _window(out_refs[i], modes[i], extents[i], 4 * p[0] + 2 * p[1] + p[2])
        if arriving:
            return [_remote(x_refs[i], win(i, p), send.at[i, k], recv.at[i, k], p)
                    for i in range(n) for k, p in enumerate(peers)]
        mine = [pltpu.make_async_copy(x_refs[i], win(i, (x, y, c)), local.at[i]) for i in range(n)]
        sends = [_remote(x_refs[i], win(i, (x, y, c)), send.at[i, k], recv.at[i, k], p)
                 for i in range(n) for k, p in enumerate(peers)]
        return mine, sends

    def start(*refs):
        mine, sends = copies(*refs, False)
        for cp in mine + sends:
            cp.start()

    def finish(*refs):
        for cp in copies(*refs, True):
            cp.wait_recv()
        mine, sends = copies(*refs, False)
        for cp in sends:
            cp.wait_send()
        for cp in mine:
            cp.wait()

    out_shapes = [jax.ShapeDtypeStruct(_full_shape(s.shape, m), s.dtype) for s, m in zip(shards, modes)]
    return _Job(shards, out_shapes, ((n, 4), (n, 4), (n,)), start, finish)


def _ag_second_job(fulls, modes, shard_shapes):
    n = len(fulls)
    extents = [_extent(s, m) for s, m in zip(shard_shapes, modes)]

    def copies(_, out_refs, send, recv, local, core_of_block):
        x, y, c = _mesh_pos()
        pc = c if core_of_block == "mine" else 1 - c
        win = lambda i, chip: _window(out_refs[i], modes[i], extents[i], 4 * chip[0] + 2 * chip[1] + pc)
        return [_remote(win(i, chip), win(i, chip), send.at[i, j], recv.at[i, j], (x, y, 1 - c))
                for i in range(n) for j, chip in enumerate(_other_chips(x, y))]

    def start(*refs):
        for cp in copies(*refs, "mine"):
            cp.start()

    def finish(*refs):
        for cp in copies(*refs, "sibling"):
            cp.wait_recv()
        for cp in copies(*refs, "mine"):
            cp.wait_send()

    out_shapes = [jax.ShapeDtypeStruct(f.shape, f.dtype) for f in fulls]
    return _Job(fulls, out_shapes, ((n, 3), (n, 3), (1,)), start, finish, aliases={i: i for i in range(n)})


def _rs_sibling_job(grads, modes, shard_shapes):
    n = len(grads)
    extents = [_extent(s, m) for s, m in zip(shard_shapes, modes)]

    def copies(g_refs, out_refs, send, recv, local):
        x, y, c = _mesh_pos()
        return [_remote(_window(g_refs[i], modes[i], extents[i], 2 * k + (1 - c)), out_refs[i].at[k],
                        send.at[i, k], recv.at[i, k], (x, y, 1 - c))
                for i in range(n) for k in range(4)]

    def start(*refs):
        for cp in copies(*refs):
            cp.start()

    def finish(*refs):
        cps = copies(*refs)
        for cp in cps:
            cp.wait_recv()
        for cp in cps:
            cp.wait_send()

    out_shapes = [jax.ShapeDtypeStruct((4,) + tuple(s), g.dtype) for s, g in zip(shard_shapes, grads)]
    return _Job(grads, out_shapes, ((n, 4), (n, 4), (1,)), start, finish)


def _rs_chip_job(partials):
    n = len(partials)

    def copies(p_refs, out_refs, send, recv, local):
        x, y, c = _mesh_pos()
        return [_remote(p_refs[i].at[2 * px + py], out_refs[i].at[j], send.at[i, j], recv.at[i, j], (px, py, c))
                for i in range(n) for j, (px, py) in enumerate(_other_chips(x, y))]

    def start(*refs):
        for cp in copies(*refs):
            cp.start()

    def finish(*refs):
        cps = copies(*refs)
        for cp in cps:
            cp.wait_recv()
        for cp in cps:
            cp.wait_send()

    out_shapes = [jax.ShapeDtypeStruct((3,) + tuple(p.shape[1:]), p.dtype) for p in partials]
    return _Job(partials, out_shapes, ((n, 3), (n, 3), (1,)), start, finish)


def _all_true(conds):
    out = conds[0]
    for c in conds[1:]:
        out = jnp.logical_and(out, c)
    return out


def _pcall(body, *, name, grid, in_specs, out_specs, out_shape, args, sem, scratch=(), jobs=(), alias_in_out=None):
    n_in, n_out, n_scr = len(args), len(out_shape), len(scratch)
    job_in = [a for j in jobs for a in j.inputs]
    job_out = [s for j in jobs for s in j.out_shapes]
    job_sems = [pltpu.SemaphoreType.DMA(shape) for j in jobs for shape in j.sems]
    aliases, in_off, out_off = dict(alias_in_out or {}), n_in, n_out
    for j in jobs:
        for a, b in j.aliases.items():
            aliases[in_off + a] = out_off + b
        in_off += len(j.inputs)
        out_off += len(j.out_shapes)

    def wrapped(*refs):
        ins = refs[:n_in]
        jins = refs[n_in:n_in + len(job_in)]
        o0 = n_in + len(job_in)
        outs = refs[o0:o0 + n_out]
        jouts = refs[o0 + n_out:o0 + n_out + len(job_out)]
        s0 = o0 + n_out + len(job_out)
        scr = refs[s0:s0 + n_scr]
        jsems = refs[s0 + n_scr:]
        if jobs:
            ids = [pl.program_id(a) for a in range(len(grid))]
            first = _all_true([i == 0 for i in ids])
            last = _all_true([i == g - 1 for i, g in zip(ids, grid)])

            def per_job(which):
                i0 = o0_ = 0
                for k, j in enumerate(jobs):
                    fn = j.start if which == "start" else j.finish
                    fn(jins[i0:i0 + len(j.inputs)], jouts[o0_:o0_ + len(j.out_shapes)], *jsems[3 * k:3 * k + 3])
                    i0 += len(j.inputs)
                    o0_ += len(j.out_shapes)

            @pl.when(first)
            def _():
                per_job("start")

        body(*ins, *outs, *scr)
        if jobs:
            @pl.when(last)
            def _():
                per_job("finish")

    semantics = tuple("arbitrary" for _ in grid) if jobs else sem
    return pl.pallas_call(
        wrapped, name=name, grid=grid,
        in_specs=list(in_specs) + [HBM_SPEC] * len(job_in),
        out_specs=list(out_specs) + [HBM_SPEC] * len(job_out),
        out_shape=list(out_shape) + job_out,
        scratch_shapes=list(scratch) + job_sems,
        input_output_aliases=aliases,
        compiler_params=pltpu.CompilerParams(dimension_semantics=semantics, vmem_limit_bytes=VMEM_LIMIT),
    )(*args, *job_in)


def _row_tile(rows, cap):
    if rows <= cap:
        return rows
    best = None
    for t in range(16, cap + 1, 16):
        if rows % t == 0:
            best = t
    assert best is not None
    return best


def _matmul(a, b, mode, out_dtype, tm, tn, tk, name, add=None, jobs=()):
    if mode == "tn":
        K, M = a.shape
    else:
        M, K = a.shape
    N = b.shape[0] if mode == "nt" else b.shape[1]
    tm, tn, tk = min(tm, M), min(tn, N), min(tk, K)
    assert M % tm == 0 and N % tn == 0 and K % tk == 0
    nk = K // tk
    dims = {"nn": NN, "nt": NT, "tn": TN}[mode]

    def body(*refs):
        if add is None:
            a_ref, b_ref, o_ref, acc = refs
            add_ref = None
        else:
            a_ref, b_ref, add_ref, o_ref, acc = refs
        k = pl.program_id(2)
        p = _dot(a_ref[...], b_ref[...], dims)

        def finish(r):
            if add_ref is not None:
                r = r + add_ref[...].astype(F32)
            o_ref[...] = r.astype(out_dtype)

        if nk == 1:
            finish(p)
        else:
            @pl.when(k == 0)
            def _():
                acc[...] = p

            @pl.when(k > 0)
            def _():
                acc[...] += p

            @pl.when(k == nk - 1)
            def _():
                finish(acc[...])

    if mode == "tn":
        a_spec = pl.BlockSpec((tk, tm), lambda i, j, k: (k, i))
    else:
        a_spec = pl.BlockSpec((tm, tk), lambda i, j, k: (i, k))
    if mode == "nt":
        b_spec = pl.BlockSpec((tn, tk), lambda i, j, k: (j, k))
    else:
        b_spec = pl.BlockSpec((tk, tn), lambda i, j, k: (k, j))
    in_specs = [a_spec, b_spec]
    args = [a, b]
    if add is not None:
        in_specs.append(pl.BlockSpec((tm, tn), lambda i, j, k: (i, j)))
        args.append(add)
    return _pcall(
        body, name=name, grid=(M // tm, N // tn, nk), in_specs=in_specs,
        out_specs=[pl.BlockSpec((tm, tn), lambda i, j, k: (i, j))],
        out_shape=[jax.ShapeDtypeStruct((M, N), out_dtype)], args=args,
        scratch=[pltpu.VMEM((tm, tn) if nk > 1 else (8, LANES), F32)],
        sem=("parallel", "parallel", "arbitrary"), jobs=jobs)


def _rmsnorm_fwd(x, w, tm, name, jobs=()):
    T, D = x.shape

    def body(x_ref, w_ref, h_ref):
        xv = x_ref[...]
        r = lax.rsqrt(jnp.mean(xv * xv, axis=-1, keepdims=True) + RMS_EPS)
        h_ref[...] = (xv * r * w_ref[...]).astype(BF16)

    return _pcall(
        body, name=name, grid=(T // tm,),
        in_specs=[pl.BlockSpec((tm, D), lambda i: (i, 0)), pl.BlockSpec((1, D), lambda i: (0, 0))],
        out_specs=[pl.BlockSpec((tm, D), lambda i: (i, 0))],
        out_shape=[jax.ShapeDtypeStruct((T, D), BF16)], args=[x, w], sem=("parallel",), jobs=jobs)


def _rmsnorm_bwd_add(dres, dh, x, w, tm, name, want_bf16, jobs=()):
    T, D = x.shape

    def body(dres_ref, dh_ref, x_ref, w_ref, *outs):
        if want_bf16:
            dx_ref, dxb_ref, dw_ref = outs
        else:
            dx_ref, dw_ref = outs
        i = pl.program_id(0)
        xv = x_ref[...]
        r = lax.rsqrt(jnp.mean(xv * xv, axis=-1, keepdims=True) + RMS_EPS)
        xh = xv * r
        dh_v = dh_ref[...].astype(F32)
        dxh = dh_v * w_ref[...]
        dx = dres_ref[...] + r * (dxh - xh * jnp.mean(dxh * xh, axis=-1, keepdims=True))
        dx_ref[...] = dx
        if want_bf16:
            dxb_ref[...] = dx.astype(BF16)
        part = jnp.sum(dh_v * xh, axis=0, keepdims=True)

        @pl.when(i == 0)
        def _():
            dw_ref[...] = part

        @pl.when(i > 0)
        def _():
            dw_ref[...] += part

    tile = pl.BlockSpec((tm, D), lambda i: (i, 0))
    row = pl.BlockSpec((1, D), lambda i: (0, 0))
    out_specs = [tile] + ([tile] if want_bf16 else []) + [row]
    out_shape = ([jax.ShapeDtypeStruct((T, D), F32)] + ([jax.ShapeDtypeStruct((T, D), BF16)] if want_bf16 else [])
                 + [jax.ShapeDtypeStruct((1, D), F32)])
    return _pcall(body, name=name, grid=(T // tm,), in_specs=[tile, tile, tile, row], out_specs=out_specs,
                  out_shape=out_shape, args=[dres, dh, x, w], sem=("arbitrary",), jobs=jobs)


def _loss_head(x2, target, wf, tm, name):
    T, D = x2.shape

    def body(x_ref, t_ref, w_ref, dx_ref, dxb_ref, loss_ref, dw_ref):
        i = pl.program_id(0)
        xv = x_ref[...]
        r = lax.rsqrt(jnp.mean(xv * xv, axis=-1, keepdims=True) + RMS_EPS)
        xh = xv * r
        wv = w_ref[...]
        e = xh * wv - t_ref[...]
        lpart = 0.5 * jnp.sum(jnp.sum(e * e, axis=-1, keepdims=True), axis=0, keepdims=True) * (1.0 / D)
        dy = e * (1.0 / D)
        dxh = dy * wv
        dx = r * (dxh - xh * jnp.mean(dxh * xh, axis=-1, keepdims=True))
        dx_ref[...] = dx
        dxb_ref[...] = dx.astype(BF16)
        wpart = jnp.sum(dy * xh, axis=0, keepdims=True)
        lfull = jnp.broadcast_to(lpart, (8, LANES))

        @pl.when(i == 0)
        def _():
            loss_ref[...] = lfull
            dw_ref[...] = wpart

        @pl.when(i > 0)
        def _():
            loss_ref[...] += lfull
            dw_ref[...] += wpart

    tile = pl.BlockSpec((tm, D), lambda i: (i, 0))
    row = pl.BlockSpec((1, D), lambda i: (0, 0))
    return _pcall(
        body, name=name, grid=(T // tm,), in_specs=[tile, tile, row],
        out_specs=[tile, tile, pl.BlockSpec((8, LANES), lambda i: (0, 0)), row],
        out_shape=[jax.ShapeDtypeStruct((T, D), F32), jax.ShapeDtypeStruct((T, D), BF16),
                   jax.ShapeDtypeStruct((8, LANES), F32), jax.ShapeDtypeStruct((1, D), F32)],
        args=[x2, target, wf], sem=("arbitrary",))


def _rope_tables(pos_col, inv2, tm, name, jobs=()):
    T = pos_col.shape[0]

    def body(p_ref, f_ref, c_ref, s_ref):
        ang = p_ref[...] * f_ref[...]
        lane = lax.broadcasted_iota(jnp.int32, ang.shape, 1)
        c_ref[...] = jnp.cos(ang)
        s_ref[...] = jnp.where(lane < RET_DK // 2, -1.0, 1.0) * jnp.sin(ang)

    tile = pl.BlockSpec((tm, RET_DK), lambda i: (i, 0))
    return _pcall(
        body, name=name, grid=(T // tm,),
        in_specs=[pl.BlockSpec((tm, 1), lambda i: (i, 0)), pl.BlockSpec((1, RET_DK), lambda i: (0, 0))],
        out_specs=[tile, tile], out_shape=[jax.ShapeDtypeStruct((T, RET_DK), F32)] * 2,
        args=[pos_col, inv2], sem=("parallel",), jobs=jobs)


def _mix_fwd(a_in, b_in, proj, x, w_ro, w_lo, w_out, mb, w2, tm, name):
    T, D = x.shape

    def body(a_ref, b_ref, gr_ref, gl_ref, x_ref, wro_ref, wlo_ref, wout_ref, mb_ref, w2_ref,
             x1_ref, mix_ref, h2_ref, ya_ref, yb_ref):
        ya = _dot(a_ref[...], wro_ref[...], NN)
        yb = _dot(b_ref[...], wlo_ref[...], NN)
        ya_ref[...] = ya.astype(BF16)
        yb_ref[...] = yb.astype(BF16)
        sa = _sigmoid(gr_ref[...].astype(F32) + mb_ref[0:1, :])
        sb = _sigmoid(gl_ref[...].astype(F32) + mb_ref[1:2, :])
        mix = (sa * ya + sb * yb).astype(BF16)
        mix_ref[...] = mix
        x1 = x_ref[...] + _dot(mix, wout_ref[...], NN)
        x1_ref[...] = x1
        r = lax.rsqrt(jnp.mean(x1 * x1, axis=-1, keepdims=True) + RMS_EPS)
        h2_ref[...] = (x1 * r * w2_ref[...]).astype(BF16)

    tile = pl.BlockSpec((tm, D), lambda i: (i, 0))
    wspec = pl.BlockSpec((D, D), lambda i: (0, 0))
    return _pcall(
        body, name=name, grid=(T // tm,),
        in_specs=[tile, tile,
                  pl.BlockSpec((tm, D), lambda i: (i, COL_GR)), pl.BlockSpec((tm, D), lambda i: (i, COL_GL)),
                  tile, wspec, wspec, wspec,
                  pl.BlockSpec((2, D), lambda i: (0, 0)), pl.BlockSpec((1, D), lambda i: (0, 0))],
        out_specs=[tile] * 5,
        out_shape=[jax.ShapeDtypeStruct((T, D), F32)] + [jax.ShapeDtypeStruct((T, D), BF16)] * 4,
        args=[a_in, b_in, proj, proj, x, w_ro, w_lo, w_out, mb, w2], sem=("parallel",))


def _write_pieces(dst_ref, sems, stashes, row0, col0s, ids, grid, compute):
    def aligned(v, m):
        return v if isinstance(v, int) else pl.multiple_of(v, m)

    def copies(slot):
        return [pltpu.make_async_copy(
                    st.at[slot],
                    dst_ref.at[pl.ds(aligned(row0, 16), st.shape[1]), pl.ds(aligned(c0, LANES), st.shape[2])],
                    sems.at[slot, k])
                for k, (st, c0) in enumerate(zip(stashes, col0s))]

    step = ids[0]
    for i, g in zip(ids[1:], grid[1:]):
        step = step * g + i
    slot = step % 2
    last = _all_true([i == g - 1 for i, g in zip(ids, grid)])
    compute(slot)

    @pl.when(step > 0)
    def _():
        for cp in copies(1 - slot):
            cp.wait()

    for cp in copies(slot):
        cp.start()

    @pl.when(last)
    def _():
        for cp in copies(slot):
            cp.wait()


def _mix_bwd(dx1b, ya, yb, proj, w_ro, w_lo, w_out, mb, tm, name, jobs=()):
    T, D = ya.shape
    grid = (T // tm,)

    def body(dx_ref, ya_ref, yb_ref, gr_ref, gl_ref, wro_ref, wlo_ref, wout_ref, mb_ref,
             da_ref, db_ref, dya_ref, dyb_ref, dp_ref, dmb_ref, dgr_s, dgl_s, wsem):
        i = pl.program_id(0)

        def compute(slot):
            dmix = _dot(dx_ref[...], wout_ref[...], NT)
            ya = ya_ref[...].astype(F32)
            yb = yb_ref[...].astype(F32)
            sa = _sigmoid(gr_ref[...].astype(F32) + mb_ref[0:1, :])
            sb = _sigmoid(gl_ref[...].astype(F32) + mb_ref[1:2, :])
            dya = (dmix * sa).astype(BF16)
            dyb = (dmix * sb).astype(BF16)
            dgr = dmix * ya * sa * (1.0 - sa)
            dgl = dmix * yb * sb * (1.0 - sb)
            dya_ref[...] = dya
            dyb_ref[...] = dyb
            dgr_s[slot] = dgr.astype(BF16)
            dgl_s[slot] = dgl.astype(BF16)
            da_ref[...] = _dot(dya, wro_ref[...], NT).astype(BF16)
            db_ref[...] = _dot(dyb, wlo_ref[...], NT).astype(BF16)

            @pl.when(i == 0)
            def _():
                dmb_ref[...] = jnp.zeros_like(dmb_ref)

            dmb_ref[0:1, :] += jnp.sum(dgr, axis=0, keepdims=True)
            dmb_ref[1:2, :] += jnp.sum(dgl, axis=0, keepdims=True)

        _write_pieces(dp_ref, wsem, [dgr_s, dgl_s], i * tm, [COL_GR * D, COL_GL * D], [i], grid, compute)

    tile = pl.BlockSpec((tm, D), lambda i: (i, 0))
    wspec = pl.BlockSpec((D, D), lambda i: (0, 0))
    two = pl.BlockSpec((2, D), lambda i: (0, 0))
    return _pcall(
        body, name=name, grid=grid,
        in_specs=[tile, tile, tile,
                  pl.BlockSpec((tm, D), lambda i: (i, COL_GR)), pl.BlockSpec((tm, D), lambda i: (i, COL_GL)),
                  wspec, wspec, wspec, two],
        out_specs=[tile] * 4 + [HBM_SPEC, two],
        out_shape=[jax.ShapeDtypeStruct((T, D), BF16)] * 4
                  + [jax.ShapeDtypeStruct((T, D_IN), BF16), jax.ShapeDtypeStruct((2, D), F32)],
        args=[dx1b, ya, yb, proj, proj, w_ro, w_lo, w_out, mb],
        scratch=[pltpu.VMEM((2, tm, D), BF16), pltpu.VMEM((2, tm, D), BF16), pltpu.SemaphoreType.DMA((2, 2))],
        sem=("arbitrary",), jobs=jobs)


def _ret_decay_consts(lg):
    L = RET_BLOCK
    n = lax.broadcasted_iota(jnp.int32, (L, L), 0)
    m = lax.broadcasted_iota(jnp.int32, (L, L), 1)
    cn, cm = n // CHUNK, m // CHUNK
    expo = jnp.where(cn == cm, jnp.abs(n - m), n - m).astype(F32)
    wm = jnp.where(cm <= cn, jnp.exp(lg * expo), 0.0)
    idx = lax.broadcasted_iota(jnp.int32, (L, 1), 0).astype(F32)
    qd = jnp.exp(lg * (idx + 1.0))
    kd = jnp.exp(lg * (L - 1.0 - idx))
    bd = jnp.exp(lg * float(L))
    return wm, qd, kd, bd


def _rotate(v, cos2, sin2s):
    return v * cos2 + pltpu.roll(v, RET_DK // 2, 1) * sin2s


def _rotate_t(d, cos2, sin2s):
    return d * cos2 - pltpu.roll(d, RET_DK // 2, 1) * sin2s


def _retention_fwd(proj, cos2, sin2s, lgam, gn_w, B, S, name, jobs=()):
    T = B * S
    nb = S // RET_BLOCK
    scale = RET_DK ** -0.5

    def body(q_ref, k_ref, v_ref, g_ref, c_ref, s_ref, lg_ref, gw_ref, o_ref, a_ref, qr, kr, st):
        wm, qd, kd, bd = _ret_decay_consts(lg_ref[0:1, 0:1])
        cos2, sin2s = c_ref[...], s_ref[...]
        qr[...] = _rotate(q_ref[...].astype(F32), cos2, sin2s)
        kr[...] = _rotate(k_ref[...].astype(F32), cos2, sin2s) * scale
        st[...] = jnp.zeros_like(st)
        gw = gw_ref[...]
        for j in range(nb):
            rows = pl.ds(j * RET_BLOCK, RET_BLOCK)
            qb = qr[rows, :]
            kb = kr[rows, :]
            vb = v_ref[rows, :].astype(BF16)
            sc = _dot(qb.astype(BF16), kb.astype(BF16), NT) * wm
            o = _dot(sc.astype(BF16), vb, NN) + _dot((qb * qd).astype(BF16), st[...].astype(BF16), NN)
            st[...] = st[...] * bd + _dot((kb * kd).astype(BF16), vb, TN)
            o_ref[rows, :] = o
            mu = jnp.mean(o, axis=-1, keepdims=True)
            oc = o - mu
            var = jnp.mean(oc * oc, axis=-1, keepdims=True)
            y = oc * lax.rsqrt(var + GN_EPS) * gw
            g = g_ref[rows, :].astype(F32)
            a_ref[rows, :] = (y * (g * _sigmoid(g))).astype(BF16)

    blk = lambda w, off: pl.BlockSpec((S, w), lambda b, h: (b, off + h))
    return _pcall(
        body, name=name, grid=(B, RET_HEADS),
        in_specs=[blk(RET_DK, COL_Q), blk(RET_DK, COL_K), blk(RET_DV, COL_V), blk(RET_DV, COL_G),
                  pl.BlockSpec((S, RET_DK), lambda b, h: (b, 0)), pl.BlockSpec((S, RET_DK), lambda b, h: (b, 0)),
                  pl.BlockSpec((None, 8, LANES), lambda b, h: (h, 0, 0)),
                  pl.BlockSpec((1, RET_DV), lambda b, h: (0, h))],
        out_specs=[blk(RET_DV, 0), blk(RET_DV, 0)],
        out_shape=[jax.ShapeDtypeStruct((T, RET_HEADS * RET_DV), F32),
                   jax.ShapeDtypeStruct((T, RET_HEADS * RET_DV), BF16)],
        args=[proj, proj, proj, proj, cos2, sin2s, lgam, gn_w],
        scratch=[pltpu.VMEM((S, RET_DK), F32), pltpu.VMEM((S, RET_DK), F32), pltpu.VMEM((RET_DK, RET_DV), F32)],
        sem=("parallel", "parallel"), jobs=jobs)


def _retention_bwd(da_in, o, proj, dproj, cos2, sin2s, lgam, gn_w, B, S, name, jobs=()):
    T = B * S
    nb = S // RET_BLOCK
    scale = RET_DK ** -0.5
    grid = (RET_HEADS, B)

    def body(da_ref, o_ref, q_ref, k_ref, v_ref, g_ref, c_ref, s_ref, lg_ref, gw_ref, _, dp_ref, dgw_ref,
             qr, kr, do_s, sts, rst, dq_s, dk_s, dv_s, dg_s, wsem):
        h, b = pl.program_id(0), pl.program_id(1)

        def compute(slot):
            wm, qd, kd, bd = _ret_decay_consts(lg_ref[0:1, 0:1])
            cos2, sin2s = c_ref[...], s_ref[...]
            qr[...] = _rotate(q_ref[...].astype(F32), cos2, sin2s)
            kr[...] = _rotate(k_ref[...].astype(F32), cos2, sin2s) * scale
            gw = gw_ref[...]
            st = jnp.zeros((RET_DK, RET_DV), F32)
            dgw = jnp.zeros((1, RET_DV), F32)
            for j in range(nb):
                rows = pl.ds(j * RET_BLOCK, RET_BLOCK)
                ov = o_ref[rows, :]
                mu = jnp.mean(ov, axis=-1, keepdims=True)
                oc = ov - mu
                rstd = lax.rsqrt(jnp.mean(oc * oc, axis=-1, keepdims=True) + GN_EPS)
                y = oc * rstd
                g = g_ref[rows, :].astype(F32)
                sg = _sigmoid(g)
                da = da_ref[rows, :].astype(F32)
                dg_s[slot, rows, :] = (da * (y * gw) * (sg * (1.0 + g * (1.0 - sg)))).astype(BF16)
                dyw = da * (g * sg)
                dgw = dgw + jnp.sum(dyw * y, axis=0, keepdims=True)
                dy = dyw * gw
                do_s[rows, :] = rstd * (dy - jnp.mean(dy, axis=-1, keepdims=True)
                                        - y * jnp.mean(dy * y, axis=-1, keepdims=True))
                sts[j] = st
                st = st * bd + _dot((kr[rows, :] * kd).astype(BF16), v_ref[rows, :].astype(BF16), TN)

            @pl.when(b == 0)
            def _():
                dgw_ref[...] = dgw

            @pl.when(b > 0)
            def _():
                dgw_ref[...] += dgw

            rst[...] = jnp.zeros_like(rst)
            for j in reversed(range(nb)):
                rows = pl.ds(j * RET_BLOCK, RET_BLOCK)
                qb = qr[rows, :]
                kb = kr[rows, :]
                qbb, kbb = qb.astype(BF16), kb.astype(BF16)
                vb = v_ref[rows, :].astype(BF16)
                dob = do_s[rows, :]
                dobb = dob.astype(BF16)
                a_m = (_dot(qbb, kbb, NT) * wm).astype(BF16)
                b_m = (_dot(dobb, vb, NT) * wm).astype(BF16)
                rb = rst[...].astype(BF16)
                dq = _dot(b_m, kbb, NN) + _dot((dob * qd).astype(BF16), sts[j].astype(BF16), NT)
                dk = _dot(b_m, qbb, TN) + kd * _dot(vb, rb, NT)
                dv = _dot(a_m, dobb, TN) + kd * _dot(kbb, rb, NN)
                rst[...] = rst[...] * bd + _dot((qb * qd).astype(BF16), dobb, TN)
                cb, sb = c_ref[rows, :], s_ref[rows, :]
                dq_s[slot, rows, :] = _rotate_t(dq, cb, sb).astype(BF16)
                dk_s[slot, rows, :] = _rotate_t(dk * scale, cb, sb).astype(BF16)
                dv_s[slot, rows, :] = dv.astype(BF16)

        cols = [(COL_Q + h) * RET_DK, (COL_K + h) * RET_DK, (COL_V + h) * RET_DV, (COL_G + h) * RET_DV]
        _write_pieces(dp_ref, wsem, [dq_s, dk_s, dv_s, dg_s], b * S, cols, [h, b], grid, compute)

    blk = lambda w, off: pl.BlockSpec((S, w), lambda h, b: (b, off + h))
    return _pcall(
        body, name=name, grid=grid,
        in_specs=[blk(RET_DV, 0), blk(RET_DV, 0),
                  blk(RET_DK, COL_Q), blk(RET_DK, COL_K), blk(RET_DV, COL_V), blk(RET_DV, COL_G),
                  pl.BlockSpec((S, RET_DK), lambda h, b: (b, 0)), pl.BlockSpec((S, RET_DK), lambda h, b: (b, 0)),
                  pl.BlockSpec((None, 8, LANES), lambda h, b: (h, 0, 0)),
                  pl.BlockSpec((1, RET_DV), lambda h, b: (0, h)), HBM_SPEC],
        out_specs=[HBM_SPEC, pl.BlockSpec((1, RET_DV), lambda h, b: (0, h))],
        out_shape=[jax.ShapeDtypeStruct(dproj.shape, dproj.dtype),
                   jax.ShapeDtypeStruct((1, RET_HEADS * RET_DV), F32)],
        args=[da_in, o, proj, proj, proj, proj, cos2, sin2s, lgam, gn_w, dproj],
        scratch=[pltpu.VMEM((S, RET_DK), F32), pltpu.VMEM((S, RET_DK), F32),
                 pltpu.VMEM((S, RET_DV), F32), pltpu.VMEM((nb, RET_DK, RET_DV), F32),
                 pltpu.VMEM((RET_DK, RET_DV), F32),
                 pltpu.VMEM((2, S, RET_DK), BF16), pltpu.VMEM((2, S, RET_DK), BF16),
                 pltpu.VMEM((2, S, RET_DV), BF16), pltpu.VMEM((2, S, RET_DV), BF16), pltpu.SemaphoreType.DMA((2, 4))],
        sem=("arbitrary", "arbitrary"), jobs=jobs, alias_in_out={10: 0})


def _lru_gates(x, cw, cb, wr, wi, br, bi, lam):
    xc = cb + cw[LRU_CONV - 1:LRU_CONV, :] * x
    for j in range(LRU_CONV - 1):
        xc = xc + cw[j:j + 1, :] * _shift_down(x, LRU_CONV - 1 - j, 0.0)
    xcb = xc.astype(BF16)
    r = 1.0 / (1.0 + jnp.exp(-(_dot(xcb, wr, NN) + br)))
    ig = _sigmoid(_dot(xcb, wi, NN) + bi)
    z = -lam
    sp = jnp.maximum(z, 0.0) + jnp.log1p(jnp.exp(-jnp.abs(z)))
    log_a = (-LRU_C) * r * sp
    a = jnp.exp(log_a)
    om = -jnp.tanh(log_a) * (a * a + 1.0)
    sq = jnp.sqrt(om)
    return xc, xcb, r, ig, sp, a, sq


def _lru_fwd(proj, cw, cb, wr, wi, br, bi, lam, B, S, name):
    T = B * S
    W = LRU_BLOCKS * LRU_BLOCK

    def body(x_ref, y_ref, cw_ref, cb_ref, wr_ref, wi_ref, br_ref, bi_ref, lam_ref, h_ref, bin_ref, a_s, b_s):
        xc, _, _, ig, _, a, sq = _lru_gates(x_ref[...].astype(F32), cw_ref[...], cb_ref[...], wr_ref[...], wi_ref[...],
                                           br_ref[...], bi_ref[...], lam_ref[...])
        a_s[...] = a
        b_s[...] = sq * ig * xc
        _scan_forward_ref(a_s, b_s, h_ref)
        bin_ref[...] = (h_ref[...] * _gelu(y_ref[...].astype(F32))).astype(BF16)

    blk = lambda off: pl.BlockSpec((S, LRU_BLOCK), lambda b, n: (b, off + n))
    vec = lambda rows: pl.BlockSpec((rows, LRU_BLOCK), lambda b, n: (0, n))
    wspec = pl.BlockSpec((None, LRU_BLOCK, LRU_BLOCK), lambda b, n: (n, 0, 0))
    return _pcall(
        body, name=name, grid=(B, LRU_BLOCKS),
        in_specs=[blk(COL_XL), blk(COL_YL), vec(LRU_CONV), vec(1), wspec, wspec, vec(1), vec(1), vec(1)],
        out_specs=[blk(0), blk(0)],
        out_shape=[jax.ShapeDtypeStruct((T, W), F32), jax.ShapeDtypeStruct((T, W), BF16)],
        args=[proj, proj, cw, cb, wr, wi, br, bi, lam],
        scratch=[pltpu.VMEM((S, LRU_BLOCK), F32), pltpu.VMEM((S, LRU_BLOCK), F32)], sem=("parallel", "parallel"))


def _lru_bwd(db_in, h, proj, dproj, cw, cb, wr, wi, br, bi, lam, B, S, name, jobs=()):
    T = B * S
    W = LRU_BLOCKS * LRU_BLOCK

    grid = (LRU_BLOCKS, B)

    def body(dbin_ref, h_ref, x_ref, y_ref, cw_ref, cb_ref, wr_ref, wi_ref, br_ref, bi_ref, lam_ref, _,
             dp_ref, dcw_ref, dcb_ref, dwr_ref, dwi_ref, dbr_ref, dbi_ref, dlam_ref, dx_s, dy_s, wsem,
             an_s, u_s, dh_s):
        n, b = pl.program_id(0), pl.program_id(1)

        def compute(slot):
            x = x_ref[...].astype(F32)
            cw = cw_ref[...]
            wr, wi = wr_ref[...], wi_ref[...]
            lam = lam_ref[...]
            xc, xcb, r, ig, sp, a, sq = _lru_gates(x, cw, cb_ref[...], wr, wi, br_ref[...], bi_ref[...], lam)
            hv = h_ref[...]
            gel, dgel = _gelu_and_grad(y_ref[...].astype(F32))
            dbin = dbin_ref[...].astype(F32)
            dy_s[slot] = (dbin * hv * dgel).astype(BF16)
            an_s[...] = _shift_up(a, 1, 0.0)
            u_s[...] = dbin * gel
            _scan_backward_ref(an_s, u_s, dh_s)
            dh = dh_s[...]
            hprev = _shift_down(hv, 1, 0.0)
            dhs = dh * sq
            d_ig = dhs * xc
            d_xc = dhs * ig
            d_loga = (dh * a) * (hprev - (ig * xc) * (a / sq))
            d_r = d_loga * ((-LRU_C) * sp)
            d_sp = jnp.sum(d_loga * ((-LRU_C) * r), axis=0, keepdims=True)
            dlam = -d_sp * _sigmoid(-lam)
            d_pr = d_r * r * (1.0 - r)
            d_pi = d_ig * ig * (1.0 - ig)
            d_prb, d_pib = d_pr.astype(BF16), d_pi.astype(BF16)
            d_xc = d_xc + _dot(d_prb, wr, NT) + _dot(d_pib, wi, NT)

            @pl.when(b == 0)
            def _():
                for ref in (dcw_ref, dcb_ref, dwr_ref, dwi_ref, dbr_ref, dbi_ref, dlam_ref):
                    ref[...] = jnp.zeros_like(ref)

            dx = cw[LRU_CONV - 1:LRU_CONV, :] * d_xc
            for j in range(LRU_CONV - 1):
                sft = LRU_CONV - 1 - j
                dx = dx + cw[j:j + 1, :] * _shift_up(d_xc, sft, 0.0)
                dcw_ref[j:j + 1, :] += jnp.sum(d_xc * _shift_down(x, sft, 0.0), axis=0, keepdims=True)
            dcw_ref[LRU_CONV - 1:LRU_CONV, :] += jnp.sum(d_xc * x, axis=0, keepdims=True)
            dx_s[slot] = dx.astype(BF16)
            dcb_ref[...] += jnp.sum(d_xc, axis=0, keepdims=True)
            dwr_ref[...] += _dot(xcb, d_prb, TN)
            dwi_ref[...] += _dot(xcb, d_pib, TN)
            dbr_ref[...] += jnp.sum(d_pr, axis=0, keepdims=True)
            dbi_ref[...] += jnp.sum(d_pi, axis=0, keepdims=True)
            dlam_ref[...] += dlam

        cols = [(COL_XL + n) * LRU_BLOCK, (COL_YL + n) * LRU_BLOCK]
        _write_pieces(dp_ref, wsem, [dx_s, dy_s], b * S, cols, [n, b], grid, compute)

    blk = lambda off: pl.BlockSpec((S, LRU_BLOCK), lambda n, b: (b, off + n))
    vec = lambda rows: pl.BlockSpec((rows, LRU_BLOCK), lambda n, b: (0, n))
    wspec = pl.BlockSpec((None, LRU_BLOCK, LRU_BLOCK), lambda n, b: (n, 0, 0))
    vshape = lambda rows: jax.ShapeDtypeStruct((rows, W), F32)
    wshape = jax.ShapeDtypeStruct((LRU_BLOCKS, LRU_BLOCK, LRU_BLOCK), F32)
    return _pcall(
        body, name=name, grid=grid,
        in_specs=[blk(0), blk(0), blk(COL_XL), blk(COL_YL), vec(LRU_CONV), vec(1), wspec, wspec, vec(1), vec(1),
                  vec(1), HBM_SPEC],
        out_specs=[HBM_SPEC, vec(LRU_CONV), vec(1), wspec, wspec, vec(1), vec(1), vec(1)],
        out_shape=[jax.ShapeDtypeStruct(dproj.shape, dproj.dtype),
                   vshape(LRU_CONV), vshape(1), wshape, wshape, vshape(1), vshape(1), vshape(1)],
        args=[db_in, h, proj, proj, cw, cb, wr, wi, br, bi, lam, dproj],
        scratch=[pltpu.VMEM((2, S, LRU_BLOCK), BF16), pltpu.VMEM((2, S, LRU_BLOCK), BF16), pltpu.SemaphoreType.DMA((2, 2)),
                 pltpu.VMEM((S, LRU_BLOCK), F32), pltpu.VMEM((S, LRU_BLOCK), F32), pltpu.VMEM((S, LRU_BLOCK), F32)],
        sem=("arbitrary", "arbitrary"), jobs=jobs, alias_in_out={11: 0})


FFN_CT = 256


def _ffn_conv(gate, cw, cb):
    gc = cb + cw[FFN_CONV - 1:FFN_CONV, :] * gate
    for j in range(FFN_CONV - 1):
        gc = gc + cw[j:j + 1, :] * _shift_down(gate, FFN_CONV - 1 - j, 0.0)
    return gc


def _ffn_act_fwd(up, cw, cb, B, S, name):
    T = B * S
    nct = D_FF // FFN_CT

    def body(g_ref, v_ref, cw_ref, cb_ref, f_ref):
        gc = _ffn_conv(g_ref[...].astype(F32), cw_ref[...], cb_ref[...])
        f_ref[...] = (_gelu(gc) * v_ref[...].astype(F32)).astype(BF16)

    return _pcall(
        body, name=name, grid=(B, nct),
        in_specs=[pl.BlockSpec((S, FFN_CT), lambda b, c: (b, c)), pl.BlockSpec((S, FFN_CT), lambda b, c: (b, nct + c)),
                  pl.BlockSpec((FFN_CONV, FFN_CT), lambda b, c: (0, c)), pl.BlockSpec((1, FFN_CT), lambda b, c: (0, c))],
        out_specs=[pl.BlockSpec((S, FFN_CT), lambda b, c: (b, c))],
        out_shape=[jax.ShapeDtypeStruct((T, D_FF), BF16)], args=[up, up, cw, cb], sem=("parallel", "parallel"))[0]


def _ffn_act_bwd(df, up, cw, cb, B, S, name, jobs=()):
    T = B * S
    nct = D_FF // FFN_CT

    grid = (nct, B)

    def body(df_ref, g_ref, v_ref, cw_ref, cb_ref, du_ref, dcw_ref, dcb_ref, dg_s, dv_s, wsem):
        c, b = pl.program_id(0), pl.program_id(1)

        def compute(slot):
            gate = g_ref[...].astype(F32)
            cw = cw_ref[...]
            gc = _ffn_conv(gate, cw, cb_ref[...])
            gel, dgel = _gelu_and_grad(gc)
            dfv = df_ref[...].astype(F32)
            dv_s[slot] = (dfv * gel).astype(BF16)
            dgc = dfv * v_ref[...].astype(F32) * dgel

            @pl.when(b == 0)
            def _():
                dcw_ref[...] = jnp.zeros_like(dcw_ref)
                dcb_ref[...] = jnp.zeros_like(dcb_ref)

            dgate = cw[FFN_CONV - 1:FFN_CONV, :] * dgc
            for j in range(FFN_CONV - 1):
                sft = FFN_CONV - 1 - j
                dgate = dgate + cw[j:j + 1, :] * _shift_up(dgc, sft, 0.0)
                dcw_ref[j:j + 1, :] += jnp.sum(dgc * _shift_down(gate, sft, 0.0), axis=0, keepdims=True)
            dcw_ref[FFN_CONV - 1:FFN_CONV, :] += jnp.sum(dgc * gate, axis=0, keepdims=True)
            dg_s[slot] = dgate.astype(BF16)
            dcb_ref[...] += jnp.sum(dgc, axis=0, keepdims=True)

        _write_pieces(du_ref, wsem, [dg_s, dv_s], b * S, [c * FFN_CT, (nct + c) * FFN_CT], [c, b], grid, compute)

    blk = pl.BlockSpec((S, FFN_CT), lambda c, b: (b, c))
    return _pcall(
        body, name=name, grid=grid,
        in_specs=[blk, blk, pl.BlockSpec((S, FFN_CT), lambda c, b: (b, nct + c)),
                  pl.BlockSpec((FFN_CONV, FFN_CT), lambda c, b: (0, c)),
                  pl.BlockSpec((1, FFN_CT), lambda c, b: (0, c))],
        out_specs=[HBM_SPEC, pl.BlockSpec((FFN_CONV, FFN_CT), lambda c, b: (0, c)),
                   pl.BlockSpec((1, FFN_CT), lambda c, b: (0, c))],
        out_shape=[jax.ShapeDtypeStruct((T, 2 * D_FF), BF16),
                   jax.ShapeDtypeStruct((FFN_CONV, D_FF), F32), jax.ShapeDtypeStruct((1, D_FF), F32)],
        args=[df, up, up, cw, cb],
        scratch=[pltpu.VMEM((2, S, FFN_CT), BF16), pltpu.VMEM((2, S, FFN_CT), BF16), pltpu.SemaphoreType.DMA((2, 2))],
        sem=("arbitrary", "arbitrary"), jobs=jobs)


def _rs_add(g, recv, mode, core, name, also_bf16=False):
    shard = tuple(recv.shape[1:])
    if mode == "mid":
        a, e, c2 = shard
        g_in = g.reshape(a, N_DEV, e, c2)
        grid = (4, 1)
        g_spec = pl.BlockSpec((a, None, e, c2), lambda k, i, c_ref: (0, 2 * k + c_ref[0], 0, 0))
        r_spec = pl.BlockSpec((None, a, e, c2), lambda k, i, c_ref: (k, 0, 0, 0))
    else:
        R, C = shard
        tr = _row_tile(R, 512)
        grid = (4, R // tr)
        if mode == "rows":
            g_in = g.reshape(N_DEV, R, C)
            g_spec = pl.BlockSpec((None, tr, C), lambda k, i, c_ref: (2 * k + c_ref[0], i, 0))
        else:
            g_in = g
            g_spec = pl.BlockSpec((tr, C), lambda k, i, c_ref: (i, 2 * k + c_ref[0]))
        r_spec = pl.BlockSpec((None, tr, C), lambda k, i, c_ref: (k, i, 0))

    def body(c_ref, g_ref, r_ref, o_ref, *ob_ref):
        s = g_ref[...] + r_ref[...]
        o_ref[...] = s
        if also_bf16:
            ob_ref[0][...] = s.astype(BF16)

    out_shape = jax.ShapeDtypeStruct(recv.shape, recv.dtype)
    return pl.pallas_call(
        body, name=name,
        grid_spec=pltpu.PrefetchScalarGridSpec(
            num_scalar_prefetch=1, grid=grid, in_specs=[g_spec, r_spec],
            out_specs=[r_spec, r_spec] if also_bf16 else r_spec),
        out_shape=[out_shape, jax.ShapeDtypeStruct(recv.shape, BF16)] if also_bf16 else out_shape,
        compiler_params=pltpu.CompilerParams(dimension_semantics=("parallel", "parallel"),
                                             vmem_limit_bytes=VMEM_LIMIT),
    )(core, g_in, recv)


def _adam_update(gv, w, m, v):
    nm = ADAM_B1 * m + (1.0 - ADAM_B1) * gv
    nv = ADAM_B2 * v + (1.0 - ADAM_B2) * (gv * gv)
    m_hat = nm / (1.0 - ADAM_B1 ** ADAM_STEP)
    v_hat = nv / (1.0 - ADAM_B2 ** ADAM_STEP)
    delta = -ADAM_LR * (m_hat / (jnp.sqrt(v_hat) + ADAM_EPS) + ADAM_WD * w)
    return delta, nm, nv


def _adamw_shard(partial, recv, w, m, v, chip, name):
    shape = tuple(w.shape)
    tr = _row_tile(shape[0], 256)
    rest = shape[1:]
    zeros = (0,) * len(rest)
    tile = pl.BlockSpec((tr,) + rest, lambda i, s: (i,) + zeros)

    def body(_, p_ref, r_ref, w_ref, m_ref, v_ref, g_ref, d_ref, nm_ref, nv_ref):
        gv = p_ref[...] + r_ref[0].astype(F32) + r_ref[1].astype(F32) + r_ref[2].astype(F32)
        g_ref[...] = gv
        d_ref[...], nm_ref[...], nv_ref[...] = _adam_update(gv, w_ref[...], m_ref[...], v_ref[...])

    grid_spec = pltpu.PrefetchScalarGridSpec(
        num_scalar_prefetch=1, grid=(shape[0] // tr,),
        in_specs=[pl.BlockSpec((None, tr) + rest, lambda i, s: (s[0], i) + zeros),
                  pl.BlockSpec((3, tr) + rest, lambda i, s: (0, i) + zeros), tile, tile, tile],
        out_specs=[tile] * 4)
    return pl.pallas_call(
        body, name=name, grid_spec=grid_spec, out_shape=[jax.ShapeDtypeStruct(shape, F32)] * 4,
        compiler_params=pltpu.CompilerParams(dimension_semantics=("parallel",), vmem_limit_bytes=VMEM_LIMIT),
    )(chip, partial, recv, w, m, v)


SMALL_LANES = 1024


def _small_rows(shape):
    r, w = shape
    return r * max(1, w // SMALL_LANES)


def _small_allreduce(parts, name):
    n = len(parts)
    shapes = [tuple(p.shape) for p in parts]
    offs, total = [], 0
    for s in shapes:
        offs.append(total)
        total += _small_rows(s)
    rows = -(-total // 8) * 8

    def body(*refs):
        p_refs, o_refs = refs[:n], refs[n:2 * n]
        buf, tot, send_sems, recv_sems = refs[2 * n:]
        x, y, c = _mesh_pos()
        me, sibling = (x, y, c), (x, y, 1 - c)
        chips = _other_chips(x, y)

        def slot(px, py, pc):
            return buf.at[4 * px + 2 * py + pc]

        def copy(k, block, to):
            return _remote(slot(*block), slot(*block), send_sems.at[k], recv_sems.at[k], to)

        tot[...] = jnp.zeros_like(tot)
        for p_ref, (r, w), off in zip(p_refs, shapes, offs):
            wl = min(w, SMALL_LANES)
            for part in range(max(1, w // SMALL_LANES)):
                tot[pl.ds(off + part * r, r), pl.ds(0, wl)] = p_ref[:, pl.ds(part * SMALL_LANES, wl)]
        buf[4 * x + 2 * y + c] = tot[...]
        first = [copy(0, me, sibling)] + [copy(1 + j, me, (*chip, c)) for j, chip in enumerate(chips)]
        for cp in first:
            cp.start()
        passed = [copy(4 + j, (*chip, c), sibling) for j, chip in enumerate(chips)]
        for j, chip in enumerate(chips):
            copy(1 + j, (*chip, c), me).wait_recv()
            passed[j].start()
        copy(0, sibling, me).wait_recv()
        for j, chip in enumerate(chips):
            copy(4 + j, (*chip, 1 - c), me).wait_recv()
        for cp in first + passed:
            cp.wait_send()
        acc = buf[0]
        for d in range(1, N_DEV):
            acc = acc + buf[d]
        tot[...] = acc
        for o_ref, (r, w), off in zip(o_refs, shapes, offs):
            wl = min(w, SMALL_LANES)
            for part in range(max(1, w // SMALL_LANES)):
                o_ref[:, pl.ds(part * SMALL_LANES, wl)] = tot[pl.ds(off + part * r, r), pl.ds(0, wl)]

    vm = pl.BlockSpec(memory_space=pltpu.VMEM)
    return pl.pallas_call(
        body, name=name,
        in_specs=[vm] * n, out_specs=[vm] * n,
        out_shape=[jax.ShapeDtypeStruct(s, F32) for s in shapes],
        scratch_shapes=[pltpu.VMEM((N_DEV, rows, SMALL_LANES), F32), pltpu.VMEM((rows, SMALL_LANES), F32),
                        pltpu.SemaphoreType.DMA((7,)), pltpu.SemaphoreType.DMA((7,))],
    )(*parts)


def _adamw_small(gs, ws, ms, vs, name):
    n = len(gs)

    def body(*refs):
        g_r, w_r, m_r, v_r = refs[:n], refs[n:2 * n], refs[2 * n:3 * n], refs[3 * n:4 * n]
        d_r, nm_r, nv_r = refs[4 * n:5 * n], refs[5 * n:6 * n], refs[6 * n:7 * n]
        for i in range(n):
            d_r[i][...], nm_r[i][...], nv_r[i][...] = _adam_update(g_r[i][...], w_r[i][...], m_r[i][...], v_r[i][...])

    vm = pl.BlockSpec(memory_space=pltpu.VMEM)
    shapes = [jax.ShapeDtypeStruct(w.shape, F32) for w in ws]
    outs = pl.pallas_call(body, name=name, in_specs=[vm] * (4 * n), out_specs=[vm] * (3 * n),
                          out_shape=shapes * 3)(*gs, *ws, *ms, *vs)
    return outs[:n], outs[n:2 * n], outs[2 * n:]


FIRST = [("w_in", (1024, 896), "cols"), ("lru_w_r", (4, 32, 256), "mid"), ("lru_w_i", (4, 32, 256), "mid")]
LATE = [("w_ret_o", (128, 1024), "rows"), ("w_lru_o", (128, 1024), "rows"), ("w_out", (128, 1024), "rows"),
        ("ffn_w_up", (1024, 768), "cols"), ("ffn_w_down", (384, 1024), "rows")]
BIG = FIRST + LATE
SMALL_SHARDED = [("merge_gate_b", (2, 128), "cols"), ("lru_conv_w", (4, 128), "cols"), ("lru_b_r", (4, 32), "stack"),
                 ("lru_b_i", (4, 32), "stack"), ("ffn_conv_w", (3, 384), "cols")]
REPLICATED = [("norm1_w", (1, 1024)), ("ret_gn_w", (1, 1024)), ("lru_conv_b", (1, 1024)), ("lru_lambda", (1, 1024)),
              ("norm2_w", (1, 1024)), ("ffn_conv_b", (1, 3072)), ("norm_f_w", (1, 1024))]
MODE = {n: m for n, _, m in BIG}
SHARD = {n: s for n, s, _ in BIG}


def _local_step(x3, positions, target3, first_shards, ws, late_shards, core):
    B, S, D = x3.shape
    T = B * S
    x = x3.reshape(T, D)
    target = target3.reshape(T, D)
    tm = min(512, T)
    big = min(1024, T)
    big2 = min(2048, T)
    big4 = min(4096, T)

    half = RET_DK // 2
    inv_freq = ROPE_BASE ** (-jnp.arange(half, dtype=F32) / half)
    inv2 = jnp.concatenate([inv_freq, inv_freq]).reshape(1, RET_DK)
    log_gamma = jnp.log1p(-jnp.power(2.0, -5.0 - jnp.arange(RET_HEADS, dtype=F32)))
    lgam = jnp.broadcast_to(log_gamma[:, None, None], (RET_HEADS, 8, LANES))
    pos_col = positions.astype(F32).reshape(T, 1)
    late_names = [n for n, _, _ in LATE]
    late_modes = [m for _, _, m in LATE]
    late_shapes = [s for _, s, _ in LATE]
    first = FIRST + SMALL_SHARDED
    first_modes = [m for _, _, m in first]

    cos2, sin2s, *first_part = _rope_tables(pos_col, inv2, tm, "rope_tables",
                                            jobs=[_ag_first_job(first_shards, first_modes)])
    h1, *first_full = _rmsnorm_fwd(x, ws["norm1_w"], tm, "norm1_fwd",
                                   jobs=[_ag_second_job(first_part, first_modes, [s for _, s, _ in first])])
    gathered = dict(zip([n for n, _, _ in first], first_full))
    wb = {n: gathered[n] for n, _, _ in FIRST}
    ws = dict(ws, **{n: gathered[n] for n, _, _ in SMALL_SHARDED})
    for n in ("lru_b_r", "lru_b_i"):
        ws[n] = jnp.transpose(ws[n], (1, 0, 2)).reshape(1, LRU_BLOCKS * LRU_BLOCK)
    proj, *late_part = _matmul(h1, wb["w_in"], "nn", BF16, big2, 1024, 1024, "proj_fwd",
                               jobs=[_ag_first_job(late_shards, late_modes)])
    o, a_in, *late_full = _retention_fwd(proj, cos2, sin2s, lgam, ws["ret_gn_w"], B, S, "retention_fwd",
                                         jobs=[_ag_second_job(late_part, late_modes, late_shapes)])
    wb = dict(wb, **dict(zip(late_names, late_full)))
    hl, b_in = _lru_fwd(proj, ws["lru_conv_w"], ws["lru_conv_b"], wb["lru_w_r"], wb["lru_w_i"],
                        ws["lru_b_r"], ws["lru_b_i"], ws["lru_lambda"], B, S, "lru_fwd")
    x1, mix, h2, ya, yb = _mix_fwd(a_in, b_in, proj, x, wb["w_ret_o"], wb["w_lru_o"], wb["w_out"],
                                   ws["merge_gate_b"], ws["norm2_w"], tm, "mix_fwd")
    up = _matmul(h2, wb["ffn_w_up"], "nn", BF16, big2, 1024, 1024, "ffn_up_fwd")[0]
    f = _ffn_act_fwd(up, ws["ffn_conv_w"], ws["ffn_conv_b"], B, S, "ffn_act_fwd")
    x2 = _matmul(f, wb["ffn_w_down"], "nn", F32, big, 1024, D_FF, "ffn_down_fwd", add=x1)[0]
    dx2, dx2b, loss_acc, d_norm_f = _loss_head(x2, target, ws["norm_f_w"], tm, "loss_head")

    g, rs = {}, {}

    def stage1(names, grads):
        return _rs_sibling_job(grads, [MODE[n] for n in names], [SHARD[n] for n in names])

    def add(names, grads, recvs):
        return [_rs_add(gr, r, MODE[n], core, "rs_add_" + n) for n, gr, r in zip(names, grads, recvs)]

    g["norm_f_w"] = d_norm_f
    g_down = _matmul(f, dx2b, "tn", F32, 1024, 1024, big4, "ffn_down_bwd_w")[0]
    df, s1_down = _matmul(dx2b, wb["ffn_w_down"], "nt", BF16, big2, 1024, 1024, "ffn_down_bwd_x",
                          jobs=[stage1(["ffn_w_down"], [g_down])])
    p_down = add(["ffn_w_down"], [g_down], [s1_down])
    dup, g["ffn_conv_w"], g["ffn_conv_b"], s2_down = _ffn_act_bwd(
        df, up, ws["ffn_conv_w"], ws["ffn_conv_b"], B, S, "ffn_act_bwd", jobs=[_rs_chip_job(p_down)])
    rs["ffn_w_down"] = (p_down[0], s2_down)

    g_up = _matmul(h2, dup, "tn", F32, 1024, 1024, big4, "ffn_up_bwd_w")[0]
    dh2, s1_up = _matmul(dup, wb["ffn_w_up"], "nt", BF16, big, 1024, D_FF, "ffn_up_bwd_x",
                         jobs=[stage1(["ffn_w_up"], [g_up])])
    p_up = add(["ffn_w_up"], [g_up], [s1_up])
    dx1, dx1b, g["norm2_w"] = _rmsnorm_bwd_add(dx2, dh2, x1, ws["norm2_w"], tm, "norm2_bwd", True)
    da_in, db_in, dya, dyb, dproj, g["merge_gate_b"] = _mix_bwd(
        dx1b, ya, yb, proj, wb["w_ret_o"], wb["w_lru_o"], wb["w_out"], ws["merge_gate_b"], tm, "mix_bwd")

    mid_names = ["w_out", "w_ret_o", "w_lru_o"]
    g_mid = [_matmul(mix, dx1b, "tn", F32, 1024, 1024, big4, "w_out_bwd_w")[0],
             _matmul(a_in, dya, "tn", F32, 1024, 1024, big4, "w_ret_o_bwd_w")[0],
             _matmul(b_in, dyb, "tn", F32, 1024, 1024, big4, "w_lru_o_bwd_w")[0]]
    (dproj, g["lru_conv_w"], g["lru_conv_b"], g_wr, g_wi, g["lru_b_r"], g["lru_b_i"], g["lru_lambda"], s2_up,
     *s1_mid) = _lru_bwd(db_in, hl, proj, dproj, ws["lru_conv_w"], ws["lru_conv_b"], wb["lru_w_r"], wb["lru_w_i"],
                         ws["lru_b_r"], ws["lru_b_i"], ws["lru_lambda"], B, S, "lru_bwd",
                         jobs=[_rs_chip_job(p_up), stage1(mid_names, g_mid)])
    rs["ffn_w_up"] = (p_up[0], s2_up)
    p_mid = add(mid_names, g_mid, s1_mid)
    lru_names = ["lru_w_r", "lru_w_i"]
    dproj, g["ret_gn_w"], *rest = _retention_bwd(
        da_in, o, proj, dproj, cos2, sin2s, lgam, ws["ret_gn_w"], B, S, "retention_bwd",
        jobs=[_rs_chip_job(p_mid), stage1(lru_names, [g_wr, g_wi])])
    s2_mid, s1_lru = rest[:3], rest[3:]
    for n, p, r in zip(mid_names, p_mid, s2_mid):
        rs[n] = (p, r)
    p_lru = add(lru_names, [g_wr, g_wi], s1_lru)

    g_in, *s2_lru = _matmul(h1, dproj, "tn", F32, 1024, 1024, big4, "proj_bwd_w", jobs=[_rs_chip_job(p_lru)])
    for n, p, r in zip(lru_names, p_lru, s2_lru):
        rs[n] = (p, r)
    s1_in = _pcall(lambda: None, name="rs_sibling_w_in", grid=(1,), in_specs=[], out_specs=[], out_shape=[], args=[],
                   sem=("arbitrary",), jobs=[stage1(["w_in"], [g_in])])
    p_in, p_in_bf16 = _rs_add(g_in, s1_in[0], MODE["w_in"], core, "rs_add_w_in", also_bf16=True)
    dh1, s2_in = _matmul(dproj, wb["w_in"], "nt", BF16, big, 1024, D_IN // 2, "proj_bwd_x",
                         jobs=[_rs_chip_job([p_in_bf16])])
    grad_x, g["norm1_w"] = _rmsnorm_bwd_add(dx1, dh1, x, ws["norm1_w"], tm, "norm1_bwd", False)
    rs["w_in"] = (p_in, s2_in)
    return loss_acc, grad_x.reshape(B, S, D), g, rs


def kernel(x, positions, norm1_w, w_in, merge_gate_b, ret_gn_w, w_ret_o, lru_conv_w, lru_conv_b, lru_w_r, lru_b_r, lru_w_i, lru_b_i, lru_lambda, w_lru_o, w_out, norm2_w, ffn_w_up, ffn_conv_w, ffn_conv_b, ffn_w_down, norm_f_w, loss_target, m_norm1_w, m_w_in, m_merge_gate_b, m_ret_gn_w, m_w_ret_o, m_lru_conv_w, m_lru_conv_b, m_lru_w_r, m_lru_b_r, m_lru_w_i, m_lru_b_i, m_lru_lambda, m_w_lru_o, m_w_out, m_norm2_w, m_ffn_w_up, m_ffn_conv_w, m_ffn_conv_b, m_ffn_w_down, m_norm_f_w, v_norm1_w, v_w_in, v_merge_gate_b, v_ret_gn_w, v_w_ret_o, v_lru_conv_w, v_lru_conv_b, v_lru_w_r, v_lru_b_r, v_lru_w_i, v_lru_b_i, v_lru_lambda, v_w_lru_o, v_w_out, v_norm2_w, v_ffn_w_up, v_ffn_conv_w, v_ffn_conv_b, v_ffn_w_down, v_norm_f_w):
    names = ["norm1_w", "w_in", "merge_gate_b", "ret_gn_w", "w_ret_o", "lru_conv_w", "lru_conv_b", "lru_w_r", "lru_b_r",
             "lru_w_i", "lru_b_i", "lru_lambda", "w_lru_o", "w_out", "norm2_w", "ffn_w_up", "ffn_conv_w", "ffn_conv_b",
             "ffn_w_down", "norm_f_w"]
    w_args = [norm1_w, w_in, merge_gate_b, ret_gn_w, w_ret_o, lru_conv_w, lru_conv_b, lru_w_r, lru_b_r, lru_w_i, lru_b_i,
              lru_lambda, w_lru_o, w_out, norm2_w, ffn_w_up, ffn_conv_w, ffn_conv_b, ffn_w_down, norm_f_w]
    m_args = [m_norm1_w, m_w_in, m_merge_gate_b, m_ret_gn_w, m_w_ret_o, m_lru_conv_w, m_lru_conv_b, m_lru_w_r, m_lru_b_r,
              m_lru_w_i, m_lru_b_i, m_lru_lambda, m_w_lru_o, m_w_out, m_norm2_w, m_ffn_w_up, m_ffn_conv_w, m_ffn_conv_b,
              m_ffn_w_down, m_norm_f_w]
    v_args = [v_norm1_w, v_w_in, v_merge_gate_b, v_ret_gn_w, v_w_ret_o, v_lru_conv_w, v_lru_conv_b, v_lru_w_r, v_lru_b_r,
              v_lru_w_i, v_lru_b_i, v_lru_lambda, v_w_lru_o, v_w_out, v_norm2_w, v_ffn_w_up, v_ffn_conv_w, v_ffn_conv_b,
              v_ffn_w_down, v_norm_f_w]
    orig_shape = {n: a.shape for n, a in zip(names, w_args)}
    local_shape = {n: s for n, s, _ in BIG + SMALL_SHARDED}
    local_shape.update({n: s for n, s in REPLICATED})
    W = {n: a.reshape(local_shape[n]) for n, a in zip(names, w_args)}
    M = {n: a.reshape(local_shape[n]) for n, a in zip(names, m_args)}
    V = {n: a.reshape(local_shape[n]) for n, a in zip(names, v_args)}

    xi, yi, ci = _mesh_pos()
    dev = 4 * xi + 2 * yi + ci
    chip = (2 * xi + yi).astype(jnp.int32).reshape(1)
    core = ci.astype(jnp.int32).reshape(1)

    small_names = [n for n, _, _ in SMALL_SHARDED]
    first_shards = [W[n].astype(BF16) for n, _, _ in FIRST] + [W[n] for n in small_names]
    late_shards = [W[n].astype(BF16) for n, _, _ in LATE]
    rep = {n: W[n] for n, _ in REPLICATED}
    loss_acc, grad_x, g, rs = _local_step(x, positions, loss_target, first_shards, rep, late_shards, core)

    G_out, D_out, M_out, V_out = {}, {}, {}, {}
    for n, _, _ in BIG:
        G_out[n], D_out[n], M_out[n], V_out[n] = _adamw_shard(rs[n][0], rs[n][1], W[n], M[n], V[n], chip, "adamw_" + n)

    rep_names = [n for n, _ in REPLICATED]
    red_names = rep_names + small_names
    red = _small_allreduce([g[n] for n in red_names] + [loss_acc[0:1, :]], "allreduce_small_grads")
    loss = red[-1][0, 0]
    gs = dict(zip(red_names, red[:-1]))
    for n, s, mode in SMALL_SHARDED:
        if mode == "cols":
            gs[n] = lax.dynamic_slice_in_dim(gs[n], dev * s[1], s[1], axis=1)
        else:
            full = gs[n].reshape(LRU_BLOCKS, LRU_BLOCK)
            gs[n] = lax.dynamic_slice_in_dim(full, dev * s[1], s[1], axis=1)
    d2, m2, v2 = _adamw_small([gs[n] for n in red_names], [W[n] for n in red_names], [M[n] for n in red_names],
                              [V[n] for n in red_names], "adamw_small")
    for i, n in enumerate(red_names):
        G_out[n], D_out[n], M_out[n], V_out[n] = gs[n], d2[i], m2[i], v2[i]

    outs = [loss, grad_x]
    for group in (G_out, D_out, M_out, V_out):
        outs += [group[n].reshape(orig_shape[n]) for n in names]
    return tuple(outs)
```

```python
import math

import jax
import jax.numpy as jnp
from jax import lax
from jax.experimental import pallas as pl
from jax.experimental.pallas import tpu as pltpu

F32 = jnp.float32
BF16 = jnp.bfloat16
MESH = pl.DeviceIdType.MESH

D_MODEL = 1024
CHUNK = 64
RET_HEADS = 4
RET_DK = 128
RET_DV = 256
LRU_BLOCKS = 4
LRU_BLOCK = 256
LRU_CONV = 4
LRU_C = 8.0
D_FF = 3072
FFN_CONV = 3
ROPE_BASE = 10000.0
RMS_EPS = 1e-6
GN_EPS = 1e-6
D_IN = 7168
ADAM_LR, ADAM_B1, ADAM_B2, ADAM_EPS, ADAM_WD, ADAM_STEP = 0.001, 0.9, 0.999, 1e-08, 0.01, 10

N_DEV = 8
V7X_VMEM_BYTES = 64 * 1024 * 1024
VMEM_LIMIT = V7X_VMEM_BYTES - 8 * 1024 * 1024
RET_BLOCK = 256
LANES = 128

COL_Q, COL_K = 0, 4
COL_V, COL_G, COL_XL, COL_YL = 4, 8, 12, 16
COL_GR, COL_GL = 5, 6

HBM_SPEC = pl.BlockSpec(memory_space=pl.ANY)


def _gelu(x):
    c = math.sqrt(2.0 / math.pi)
    t = jnp.tanh(x * (c + (c * 0.044715) * (x * x)))
    return x * (0.5 * t + 0.5)


def _gelu_and_grad(x):
    c = math.sqrt(2.0 / math.pi)
    x2 = x * x
    t = jnp.tanh(x * (c + (c * 0.044715) * x2))
    h = 0.5 * t + 0.5
    g = x * h
    dg = h + g * (1.0 - h) * ((2.0 * c) + (6.0 * c * 0.044715) * x2)
    return g, dg


def _sigmoid(x):
    return 0.5 * jnp.tanh(0.5 * x) + 0.5


SUBLANES = 8


def _shift_down(x, s, fill):
    r = pltpu.roll(x, s, 0)
    rows = lax.broadcasted_iota(jnp.int32, (SUBLANES,) + x.shape[1:], 0)
    top = jnp.where(rows >= s, r[:SUBLANES], fill)
    return jnp.concatenate([top, r[SUBLANES:]], axis=0)


def _shift_up(x, s, fill):
    n = x.shape[0]
    r = pltpu.roll(x, n - s, 0)
    rows = lax.broadcasted_iota(jnp.int32, (SUBLANES,) + x.shape[1:], 0)
    bottom = jnp.where(rows < SUBLANES - s, r[n - SUBLANES:], fill)
    return jnp.concatenate([r[:n - SUBLANES], bottom], axis=0)


SCAN_CHUNK = 64


def _scan_forward(a, b):
    n = a.shape[0]
    s = 1
    while s < n:
        if s % SUBLANES:
            b = a * _shift_down(b, s, 0.0) + b
            a = a * _shift_down(a, s, 1.0)
        else:
            b = jnp.concatenate([b[:s], a[s:] * b[:n - s] + b[s:]], axis=0)
            a = jnp.concatenate([a[:s], a[s:] * a[:n - s]], axis=0)
        s *= 2
    return a, b


def _scan_backward(a_next, u):
    n = u.shape[0]
    s = 1
    while s < n:
        if s % SUBLANES:
            u = u + a_next * _shift_up(u, s, 0.0)
            a_next = a_next * _shift_up(a_next, s, 1.0)
        else:
            u = jnp.concatenate([u[:n - s] + a_next[:n - s] * u[s:], u[n - s:]], axis=0)
            a_next = jnp.concatenate([a_next[:n - s] * a_next[s:], a_next[n - s:]], axis=0)
        s *= 2
    return a_next, u


def _scan_forward_ref(a_ref, b_ref, h_ref):
    S, W = a_ref.shape
    for strip in range(W // LANES):
        cols = pl.ds(strip * LANES, LANES)

        def body(k, carry, cols=cols):
            rows = pl.ds(pl.multiple_of(k * SCAN_CHUNK, SCAN_CHUNK), SCAN_CHUNK)
            a_cum, h_loc = _scan_forward(a_ref[rows, cols], b_ref[rows, cols])
            h = h_loc + a_cum * carry
            h_ref[rows, cols] = h
            return h[SCAN_CHUNK - 1:, :]

        lax.fori_loop(0, S // SCAN_CHUNK, body, jnp.zeros((1, LANES), F32))


def _scan_backward_ref(an_ref, u_ref, d_ref):
    S, W = an_ref.shape
    n_chunks = S // SCAN_CHUNK
    for strip in range(W // LANES):
        cols = pl.ds(strip * LANES, LANES)

        def body(i, carry, cols=cols):
            rows = pl.ds(pl.multiple_of((n_chunks - 1 - i) * SCAN_CHUNK, SCAN_CHUNK), SCAN_CHUNK)
            an_cum, d_loc = _scan_backward(an_ref[rows, cols], u_ref[rows, cols])
            d = d_loc + an_cum * carry
            d_ref[rows, cols] = d
            return d[:1, :]

        lax.fori_loop(0, n_chunks, body, jnp.zeros((1, LANES), F32))


def _dot(a, b, dims):
    return lax.dot_general(a, b, (dims, ((), ())), preferred_element_type=F32)


NN = ((1,), (0,))
NT = ((1,), (1,))
TN = ((0,), (0,))


def _mesh_pos():
    return lax.axis_index("x"), lax.axis_index("y"), lax.axis_index("c")


def _other_chips(x, y):
    return [(1 - x, y), (x, 1 - y), (1 - x, 1 - y)]


def _full_shape(shard_shape, mode):
    if mode == "rows":
        return (N_DEV * shard_shape[0],) + tuple(shard_shape[1:])
    if mode == "cols":
        return (shard_shape[0], N_DEV * shard_shape[1])
    if mode == "mid":
        return (shard_shape[0], N_DEV * shard_shape[1], shard_shape[2])
    return (N_DEV,) + tuple(shard_shape)


def _extent(shard_shape, mode):
    return {"rows": shard_shape[0], "cols": shard_shape[1], "mid": shard_shape[1], "stack": 1}[mode]


def _window(ref, mode, extent, d):
    if mode == "stack":
        return ref.at[d]
    start = pl.multiple_of(d * extent, extent)
    if mode == "rows":
        return ref.at[pl.ds(start, extent)]
    if mode == "cols":
        return ref.at[:, pl.ds(start, extent)]
    return ref.at[:, pl.ds(start, extent), :]


class _Job:
    def __init__(self, inputs, out_shapes, sems, start, finish, aliases=None):
        self.inputs, self.out_shapes, self.sems = list(inputs), list(out_shapes), sems
        self.start, self.finish, self.aliases = start, finish, dict(aliases or {})


def _remote(src, dst, send_sem, recv_sem, to):
    return pltpu.make_async_remote_copy(src_ref=src, dst_ref=dst, send_sem=send_sem, recv_sem=recv_sem,
                                        device_id=to, device_id_type=MESH)


def _ag_first_job(shards, modes):
    n = len(shards)
    extents = [_extent(s.shape, m) for s, m in zip(shards, modes)]

    def copies(x_refs, out_refs, send, recv, local, arriving):
        x, y, c = _mesh_pos()
        peers = [(x, y, 1 - c)] + [(*chip, c) for chip in _other_chips(x, y)]
        win = lambda i, p: _window(out_refs[i], modes[i], extents[i], 4 * p[0] + 2 * p[1] + p[2])
        if arriving:
            return [_remote(x_refs[i], win(i, p), send.at[i, k], recv.at[i, k], p)
                    for i in range(n) for k, p in enumerate(peers)]
        mine = [pltpu.make_async_copy(x_refs[i], win(i, (x, y, c)), local.at[i]) for i in range(n)]
        sends = [_remote(x_refs[i], win(i, (x, y, c)), send.at[i, k], recv.at[i, k], p)
                 for i in range(n) for k, p in enumerate(peers)]
        return mine, sends

    def start(*refs):
        mine, sends = copies(*refs, False)
        for cp in mine + sends:
            cp.start()

    def finish(*refs):
        for cp in copies(*refs, True):
            cp.wait_recv()
        mine, sends = copies(*refs, False)
        for cp in sends:
            cp.wait_send()
        for cp in mine:
            cp.wait()

    out_shapes = [jax.ShapeDtypeStruct(_full_shape(s.shape, m), s.dtype) for s, m in zip(shards, modes)]
    return _Job(shards, out_shapes, ((n, 4), (n, 4), (n,)), start, finish)


def _ag_second_job(fulls, modes, shard_shapes):
    n = len(fulls)
    extents = [_extent(s, m) for s, m in zip(shard_shapes, modes)]

    def copies(_, out_refs, send, recv, local, core_of_block):
        x, y, c = _mesh_pos()
        pc = c if core_of_block == "mine" else 1 - c
        win = lambda i, chip: _window(out_refs[i], modes[i], extents[i], 4 * chip[0] + 2 * chip[1] + pc)
        return [_remote(win(i, chip), win(i, chip), send.at[i, j], recv.at[i, j], (x, y, 1 - c))
                for i in range(n) for j, chip in enumerate(_other_chips(x, y))]

    def start(*refs):
        for cp in copies(*refs, "mine"):
            cp.start()

    def finish(*refs):
        for cp in copies(*refs, "sibling"):
            cp.wait_recv()
        for cp in copies(*refs, "mine"):
            cp.wait_send()

    out_shapes = [jax.ShapeDtypeStruct(f.shape, f.dtype) for f in fulls]
    return _Job(fulls, out_shapes, ((n, 3), (n, 3), (1,)), start, finish, aliases={i: i for i in range(n)})


def _rs_sibling_job(grads, modes, shard_shapes):
    n = len(grads)
    extents = [_extent(s, m) for s, m in zip(shard_shapes, modes)]

    def copies(g_refs, out_refs, send, recv, local):
        x, y, c = _mesh_pos()
        return [_remote(_window(g_refs[i], modes[i], extents[i], 2 * k + (1 - c)), out_refs[i].at[k],
                        send.at[i, k], recv.at[i, k], (x, y, 1 - c))
                for i in range(n) for k in range(4)]

    def start(*refs):
        for cp in copies(*refs):
            cp.start()

    def finish(*refs):
        cps = copies(*refs)
        for cp in cps:
            cp.wait_recv()
        for cp in cps:
            cp.wait_send()

    out_shapes = [jax.ShapeDtypeStruct((4,) + tuple(s), g.dtype) for s, g in zip(shard_shapes, grads)]
    return _Job(grads, out_shapes, ((n, 4), (n, 4), (1,)), start, finish)


def _rs_chip_job(partials):
    n = len(partials)

    def copies(p_refs, out_refs, send, recv, local):
        x, y, c = _mesh_pos()
        return [_remote(p_refs[i].at[2 * px + py], out_refs[i].at[j], send.at[i, j], recv.at[i, j], (px, py, c))
                for i in range(n) for j, (px, py) in enumerate(_other_chips(x, y))]

    def start(*refs):
        for cp in copies(*refs):
            cp.start()

    def finish(*refs):
        cps = copies(*refs)
        for cp in cps:
            cp.wait_recv()
        for cp in cps:
            cp.wait_send()

    out_shapes = [jax.ShapeDtypeStruct((3,) + tuple(p.shape[1:]), p.dtype) for p in partials]
    return _Job(partials, out_shapes, ((n, 3), (n, 3), (1,)), start, finish)


def _all_true(conds):
    out = conds[0]
    for c in conds[1:]:
        out = jnp.logical_and(out, c)
    return out


def _pcall(body, *, name, grid, in_specs, out_specs, out_shape, args, sem, scratch=(), jobs=(), alias_in_out=None):
    n_in, n_out, n_scr = len(args), len(out_shape), len(scratch)
    job_in = [a for j in jobs for a in j.inputs]
    job_out = [s for j in jobs for s in j.out_shapes]
    job_sems = [pltpu.SemaphoreType.DMA(shape) for j in jobs for shape in j.sems]
    aliases, in_off, out_off = dict(alias_in_out or {}), n_in, n_out
    for j in jobs:
        for a, b in j.aliases.items():
            aliases[in_off + a] = out_off + b
        in_off += len(j.inputs)
        out_off += len(j.out_shapes)

    def wrapped(*refs):
        ins = refs[:n_in]
        jins = refs[n_in:n_in + len(job_in)]
        o0 = n_in + len(job_in)
        outs = refs[o0:o0 + n_out]
        jouts = refs[o0 + n_out:o0 + n_out + len(job_out)]
        s0 = o0 + n_out + len(job_out)
        scr = refs[s0:s0 + n_scr]
        jsems = refs[s0 + n_scr:]
        if jobs:
            ids = [pl.program_id(a) for a in range(len(grid))]
            first = _all_true([i == 0 for i in ids])
            last = _all_true([i == g - 1 for i, g in zip(ids, grid)])

            def per_job(which):
                i0 = o0_ = 0
                for k, j in enumerate(jobs):
                    fn = j.start if which == "start" else j.finish
                    fn(jins[i0:i0 + len(j.inputs)], jouts[o0_:o0_ + len(j.out_shapes)], *jsems[3 * k:3 * k + 3])
                    i0 += len(j.inputs)
                    o0_ += len(j.out_shapes)

            @pl.when(first)
            def _():
                per_job("start")

        body(*ins, *outs, *scr)
        if jobs:
            @pl.when(last)
            def _():
                per_job("finish")

    semantics = tuple("arbitrary" for _ in grid) if jobs else sem
    return pl.pallas_call(
        wrapped, name=name, grid=grid,
        in_specs=list(in_specs) + [HBM_SPEC] * len(job_in),
        out_specs=list(out_specs) + [HBM_SPEC] * len(job_out),
        out_shape=list(out_shape) + job_out,
        scratch_shapes=list(scratch) + job_sems,
        input_output_aliases=aliases,
        compiler_params=pltpu.CompilerParams(dimension_semantics=semantics, vmem_limit_bytes=VMEM_LIMIT),
    )(*args, *job_in)


def _row_tile(rows, cap):
    if rows <= cap:
        return rows
    best = None
    for t in range(16, cap + 1, 16):
        if rows % t == 0:
            best = t
    assert best is not None
    return best


def _matmul(a, b, mode, out_dtype, tm, tn, tk, name, add=None, jobs=()):
    if mode == "tn":
        K, M = a.shape
    else:
        M, K = a.shape
    N = b.shape[0] if mode == "nt" else b.shape[1]
    tm, tn, tk = min(tm, M), min(tn, N), min(tk, K)
    assert M % tm == 0 and N % tn == 0 and K % tk == 0
    nk = K // tk
    dims = {"nn": NN, "nt": NT, "tn": TN}[mode]

    def body(*refs):
        if add is None:
            a_ref, b_ref, o_ref, acc = refs
            add_ref = None
        else:
            a_ref, b_ref, add_ref, o_ref, acc = refs
        k = pl.program_id(2)
        p = _dot(a_ref[...], b_ref[...], dims)

        def finish(r):
            if add_ref is not None:
                r = r + add_ref[...].astype(F32)
            o_ref[...] = r.astype(out_dtype)

        if nk == 1:
            finish(p)
        else:
            @pl.when(k == 0)
            def _():
                acc[...] = p

            @pl.when(k > 0)
            def _():
                acc[...] += p

            @pl.when(k == nk - 1)
            def _():
                finish(acc[...])

    if mode == "tn":
        a_spec = pl.BlockSpec((tk, tm), lambda i, j, k: (k, i))
    else:
        a_spec = pl.BlockSpec((tm, tk), lambda i, j, k: (i, k))
    if mode == "nt":
        b_spec = pl.BlockSpec((tn, tk), lambda i, j, k: (j, k))
    else:
        b_spec = pl.BlockSpec((tk, tn), lambda i, j, k: (k, j))
    in_specs = [a_spec, b_spec]
    args = [a, b]
    if add is not None:
        in_specs.append(pl.BlockSpec((tm, tn), lambda i, j, k: (i, j)))
        args.append(add)
    return _pcall(
        body, name=name, grid=(M // tm, N // tn, nk), in_specs=in_specs,
        out_specs=[pl.BlockSpec((tm, tn), lambda i, j, k: (i, j))],
        out_shape=[jax.ShapeDtypeStruct((M, N), out_dtype)], args=args,
        scratch=[pltpu.VMEM((tm, tn) if nk > 1 else (8, LANES), F32)],
        sem=("parallel", "parallel", "arbitrary"), jobs=jobs)


def _rope_and_norm(pos_col, inv2, x, w, tm, name, jobs=()):
    T, D = x.shape

    def body(p_ref, f_ref, x_ref, w_ref, c_ref, s_ref, h_ref):
        ang = p_ref[...] * f_ref[...]
        lane = lax.broadcasted_iota(jnp.int32, ang.shape, 1)
        c_ref[...] = jnp.cos(ang)
        s_ref[...] = jnp.where(lane < RET_DK // 2, -1.0, 1.0) * jnp.sin(ang)
        xv = x_ref[...]
        r = lax.rsqrt(jnp.mean(xv * xv, axis=-1, keepdims=True) + RMS_EPS)
        h_ref[...] = (xv * r * w_ref[...]).astype(BF16)

    table = pl.BlockSpec((tm, RET_DK), lambda i: (i, 0))
    tile = pl.BlockSpec((tm, D), lambda i: (i, 0))
    return _pcall(
        body, name=name, grid=(T // tm,),
        in_specs=[pl.BlockSpec((tm, 1), lambda i: (i, 0)), pl.BlockSpec((1, RET_DK), lambda i: (0, 0)),
                  tile, pl.BlockSpec((1, D), lambda i: (0, 0))],
        out_specs=[table, table, tile],
        out_shape=[jax.ShapeDtypeStruct((T, RET_DK), F32)] * 2 + [jax.ShapeDtypeStruct((T, D), BF16)],
        args=[pos_col, inv2, x, w], sem=("parallel",), jobs=jobs)


def _rmsnorm_bwd_add(dres, dh, x, w, tm, name, want_bf16, jobs=()):
    T, D = x.shape

    def body(dres_ref, dh_ref, x_ref, w_ref, *outs):
        if want_bf16:
            dx_ref, dxb_ref, dw_ref = outs
        else:
            dx_ref, dw_ref = outs
        i = pl.program_id(0)
        xv = x_ref[...]
        r = lax.rsqrt(jnp.mean(xv * xv, axis=-1, keepdims=True) + RMS_EPS)
        xh = xv * r
        dh_v = dh_ref[...].astype(F32)
        dxh = dh_v * w_ref[...]
        dx = dres_ref[...] + r * (dxh - xh * jnp.mean(dxh * xh, axis=-1, keepdims=True))
        dx_ref[...] = dx
        if want_bf16:
            dxb_ref[...] = dx.astype(BF16)
        part = jnp.sum(dh_v * xh, axis=0, keepdims=True)

        @pl.when(i == 0)
        def _():
            dw_ref[...] = part

        @pl.when(i > 0)
        def _():
            dw_ref[...] += part

    tile = pl.BlockSpec((tm, D), lambda i: (i, 0))
    row = pl.BlockSpec((1, D), lambda i: (0, 0))
    out_specs = [tile] + ([tile] if want_bf16 else []) + [row]
    out_shape = ([jax.ShapeDtypeStruct((T, D), F32)] + ([jax.ShapeDtypeStruct((T, D), BF16)] if want_bf16 else [])
                 + [jax.ShapeDtypeStruct((1, D), F32)])
    return _pcall(body, name=name, grid=(T // tm,), in_specs=[tile, tile, tile, row], out_specs=out_specs,
                  out_shape=out_shape, args=[dres, dh, x, w], sem=("arbitrary",), jobs=jobs)


def _loss_head(x2, target, wf, tm, name):
    T, D = x2.shape

    def body(x_ref, t_ref, w_ref, dx_ref, dxb_ref, loss_ref, dw_ref):
        i = pl.program_id(0)
        xv = x_ref[...]
        r = lax.rsqrt(jnp.mean(xv * xv, axis=-1, keepdims=True) + RMS_EPS)
        xh = xv * r
        wv = w_ref[...]
        e = xh * wv - t_ref[...]
        lpart = 0.5 * jnp.sum(jnp.sum(e * e, axis=-1, keepdims=True), axis=0, keepdims=True) * (1.0 / D)
        dy = e * (1.0 / D)
        dxh = dy * wv
        dx = r * (dxh - xh * jnp.mean(dxh * xh, axis=-1, keepdims=True))
        dx_ref[...] = dx
        dxb_ref[...] = dx.astype(BF16)
        wpart = jnp.sum(dy * xh, axis=0, keepdims=True)
        lfull = jnp.broadcast_to(lpart, (8, LANES))

        @pl.when(i == 0)
        def _():
            loss_ref[...] = lfull
            dw_ref[...] = wpart

        @pl.when(i > 0)
        def _():
            loss_ref[...] += lfull
            dw_ref[...] += wpart

    tile = pl.BlockSpec((tm, D), lambda i: (i, 0))
    row = pl.BlockSpec((1, D), lambda i: (0, 0))
    return _pcall(
        body, name=name, grid=(T // tm,), in_specs=[tile, tile, row],
        out_specs=[tile, tile, pl.BlockSpec((8, LANES), lambda i: (0, 0)), row],
        out_shape=[jax.ShapeDtypeStruct((T, D), F32), jax.ShapeDtypeStruct((T, D), BF16),
                   jax.ShapeDtypeStruct((8, LANES), F32), jax.ShapeDtypeStruct((1, D), F32)],
        args=[x2, target, wf], sem=("arbitrary",))


def _mix_fwd(a_in, b_in, proj, x, w_ro, w_lo, w_out, mb, w2, tm, name):
    T, D = x.shape

    def body(a_ref, b_ref, gr_ref, gl_ref, x_ref, wro_ref, wlo_ref, wout_ref, mb_ref, w2_ref,
             x1_ref, mix_ref, h2_ref, ya_ref, yb_ref):
        ya = _dot(a_ref[...], wro_ref[...], NN)
        yb = _dot(b_ref[...], wlo_ref[...], NN)
        ya_ref[...] = ya.astype(BF16)
        yb_ref[...] = yb.astype(BF16)
        sa = _sigmoid(gr_ref[...].astype(F32) + mb_ref[0:1, :])
        sb = _sigmoid(gl_ref[...].astype(F32) + mb_ref[1:2, :])
        mix = (sa * ya + sb * yb).astype(BF16)
        mix_ref[...] = mix
        x1 = x_ref[...] + _dot(mix, wout_ref[...], NN)
        x1_ref[...] = x1
        r = lax.rsqrt(jnp.mean(x1 * x1, axis=-1, keepdims=True) + RMS_EPS)
        h2_ref[...] = (x1 * r * w2_ref[...]).astype(BF16)

    tile = pl.BlockSpec((tm, D), lambda i: (i, 0))
    wspec = pl.BlockSpec((D, D), lambda i: (0, 0))
    return _pcall(
        body, name=name, grid=(T // tm,),
        in_specs=[tile, tile,
                  pl.BlockSpec((tm, D), lambda i: (i, COL_GR)), pl.BlockSpec((tm, D), lambda i: (i, COL_GL)),
                  tile, wspec, wspec, wspec,
                  pl.BlockSpec((2, D), lambda i: (0, 0)), pl.BlockSpec((1, D), lambda i: (0, 0))],
        out_specs=[tile] * 5,
        out_shape=[jax.ShapeDtypeStruct((T, D), F32)] + [jax.ShapeDtypeStruct((T, D), BF16)] * 4,
        args=[a_in, b_in, proj, proj, x, w_ro, w_lo, w_out, mb, w2], sem=("parallel",))


def _write_pieces(dst_ref, sems, stashes, row0, col0s, ids, grid, compute):
    def aligned(v, m):
        return v if isinstance(v, int) else pl.multiple_of(v, m)

    def copies(slot):
        return [pltpu.make_async_copy(
                    st.at[slot],
                    dst_ref.at[pl.ds(aligned(row0, 16), st.shape[1]), pl.ds(aligned(c0, LANES), st.shape[2])],
                    sems.at[slot, k])
                for k, (st, c0) in enumerate(zip(stashes, col0s))]

    step = ids[0]
    for i, g in zip(ids[1:], grid[1:]):
        step = step * g + i
    slot = step % 2
    last = _all_true([i == g - 1 for i, g in zip(ids, grid)])
    compute(slot)

    @pl.when(step > 0)
    def _():
        for cp in copies(1 - slot):
            cp.wait()

    for cp in copies(slot):
        cp.start()

    @pl.when(last)
    def _():
        for cp in copies(slot):
            cp.wait()


def _mix_bwd(dx1b, ya, yb, proj, w_ro, w_lo, w_out, mb, tm, name, jobs=()):
    T, D = ya.shape
    grid = (T // tm,)

    def body(dx_ref, ya_ref, yb_ref, gr_ref, gl_ref, wro_ref, wlo_ref, wout_ref, mb_ref,
             da_ref, db_ref, dya_ref, dyb_ref, dp_ref, dmb_ref, dgr_s, dgl_s, wsem):
        i = pl.program_id(0)

        def compute(slot):
            dmix = _dot(dx_ref[...], wout_ref[...], NT)
            ya = ya_ref[...].astype(F32)
            yb = yb_ref[...].astype(F32)
            sa = _sigmoid(gr_ref[...].astype(F32) + mb_ref[0:1, :])
            sb = _sigmoid(gl_ref[...].astype(F32) + mb_ref[1:2, :])
            dya = (dmix * sa).astype(BF16)
            dyb = (dmix * sb).astype(BF16)
            dgr = dmix * ya * sa * (1.0 - sa)
            dgl = dmix * yb * sb * (1.0 - sb)
            dya_ref[...] = dya
            dyb_ref[...] = dyb
            dgr_s[slot] = dgr.astype(BF16)
            dgl_s[slot] = dgl.astype(BF16)
            da_ref[...] = _dot(dya, wro_ref[...], NT).astype(BF16)
            db_ref[...] = _dot(dyb, wlo_ref[...], NT).astype(BF16)

            @pl.when(i == 0)
            def _():
                dmb_ref[...] = jnp.zeros_like(dmb_ref)

            dmb_ref[0:1, :] += jnp.sum(dgr, axis=0, keepdims=True)
            dmb_ref[1:2, :] += jnp.sum(dgl, axis=0, keepdims=True)

        _write_pieces(dp_ref, wsem, [dgr_s, dgl_s], i * tm, [COL_GR * D, COL_GL * D], [i], grid, compute)

    tile = pl.BlockSpec((tm, D), lambda i: (i, 0))
    wspec = pl.BlockSpec((D, D), lambda i: (0, 0))
    two = pl.BlockSpec((2, D), lambda i: (0, 0))
    return _pcall(
        body, name=name, grid=grid,
        in_specs=[tile, tile, tile,
                  pl.BlockSpec((tm, D), lambda i: (i, COL_GR)), pl.BlockSpec((tm, D), lambda i: (i, COL_GL)),
                  wspec, wspec, wspec, two],
        out_specs=[tile] * 4 + [HBM_SPEC, two],
        out_shape=[jax.ShapeDtypeStruct((T, D), BF16)] * 4
                  + [jax.ShapeDtypeStruct((T, D_IN), BF16), jax.ShapeDtypeStruct((2, D), F32)],
        args=[dx1b, ya, yb, proj, proj, w_ro, w_lo, w_out, mb],
        scratch=[pltpu.VMEM((2, tm, D), BF16), pltpu.VMEM((2, tm, D), BF16), pltpu.SemaphoreType.DMA((2, 2))],
        sem=("arbitrary",), jobs=jobs)


def _ret_decay_consts(lg):
    L = RET_BLOCK
    n = lax.broadcasted_iota(jnp.int32, (L, L), 0)
    m = lax.broadcasted_iota(jnp.int32, (L, L), 1)
    cn, cm = n // CHUNK, m // CHUNK
    expo = jnp.where(cn == cm, jnp.abs(n - m), n - m).astype(F32)
    wm = jnp.where(cm <= cn, jnp.exp(lg * expo), 0.0)
    idx = lax.broadcasted_iota(jnp.int32, (L, 1), 0).astype(F32)
    qd = jnp.exp(lg * (idx + 1.0))
    kd = jnp.exp(lg * (L - 1.0 - idx))
    bd = jnp.exp(lg * float(L))
    return wm, qd, kd, bd


def _rotate(v, cos2, sin2s):
    return v * cos2 + pltpu.roll(v, RET_DK // 2, 1) * sin2s


def _rotate_t(d, cos2, sin2s):
    return d * cos2 - pltpu.roll(d, RET_DK // 2, 1) * sin2s


def _retention_fwd(proj, cos2, sin2s, lgam, gn_w, B, S, name, jobs=()):
    T = B * S
    nb = S // RET_BLOCK
    scale = RET_DK ** -0.5

    def body(q_ref, k_ref, v_ref, g_ref, c_ref, s_ref, lg_ref, gw_ref, o_ref, a_ref, qr, kr, st):
        wm, qd, kd, bd = _ret_decay_consts(lg_ref[0:1, 0:1])
        cos2, sin2s = c_ref[...], s_ref[...]
        qr[...] = _rotate(q_ref[...].astype(F32), cos2, sin2s)
        kr[...] = _rotate(k_ref[...].astype(F32), cos2, sin2s) * scale
        st[...] = jnp.zeros_like(st)
        gw = gw_ref[...]
        for j in range(nb):
            rows = pl.ds(j * RET_BLOCK, RET_BLOCK)
            qb = qr[rows, :]
            kb = kr[rows, :]
            vb = v_ref[rows, :].astype(BF16)
            sc = _dot(qb.astype(BF16), kb.astype(BF16), NT) * wm
            o = _dot(sc.astype(BF16), vb, NN) + _dot((qb * qd).astype(BF16), st[...].astype(BF16), NN)
            st[...] = st[...] * bd + _dot((kb * kd).astype(BF16), vb, TN)
            o_ref[rows, :] = o
            mu = jnp.mean(o, axis=-1, keepdims=True)
            oc = o - mu
            var = jnp.mean(oc * oc, axis=-1, keepdims=True)
            y = oc * lax.rsqrt(var + GN_EPS) * gw
            g = g_ref[rows, :].astype(F32)
            a_ref[rows, :] = (y * (g * _sigmoid(g))).astype(BF16)

    blk = lambda w, off: pl.BlockSpec((S, w), lambda b, h: (b, off + h))
    return _pcall(
        body, name=name, grid=(B, RET_HEADS),
        in_specs=[blk(RET_DK, COL_Q), blk(RET_DK, COL_K), blk(RET_DV, COL_V), blk(RET_DV, COL_G),
                  pl.BlockSpec((S, RET_DK), lambda b, h: (b, 0)), pl.BlockSpec((S, RET_DK), lambda b, h: (b, 0)),
                  pl.BlockSpec((None, 8, LANES), lambda b, h: (h, 0, 0)),
                  pl.BlockSpec((1, RET_DV), lambda b, h: (0, h))],
        out_specs=[blk(RET_DV, 0), blk(RET_DV, 0)],
        out_shape=[jax.ShapeDtypeStruct((T, RET_HEADS * RET_DV), F32),
                   jax.ShapeDtypeStruct((T, RET_HEADS * RET_DV), BF16)],
        args=[proj, proj, proj, proj, cos2, sin2s, lgam, gn_w],
        scratch=[pltpu.VMEM((S, RET_DK), F32), pltpu.VMEM((S, RET_DK), F32), pltpu.VMEM((RET_DK, RET_DV), F32)],
        sem=("parallel", "parallel"), jobs=jobs)


def _retention_bwd(da_in, o, proj, dproj, cos2, sin2s, lgam, gn_w, B, S, name, jobs=()):
    T = B * S
    nb = S // RET_BLOCK
    scale = RET_DK ** -0.5
    grid = (RET_HEADS, B)

    def body(da_ref, o_ref, q_ref, k_ref, v_ref, g_ref, c_ref, s_ref, lg_ref, gw_ref, _, dp_ref, dgw_ref,
             qr, kr, do_s, sts, rst, dq_s, dk_s, dv_s, dg_s, wsem):
        h, b = pl.program_id(0), pl.program_id(1)

        def compute(slot):
            wm, qd, kd, bd = _ret_decay_consts(lg_ref[0:1, 0:1])
            cos2, sin2s = c_ref[...], s_ref[...]
            qr[...] = _rotate(q_ref[...].astype(F32), cos2, sin2s)
            kr[...] = _rotate(k_ref[...].astype(F32), cos2, sin2s) * scale
            gw = gw_ref[...]
            st = jnp.zeros((RET_DK, RET_DV), F32)
            dgw = jnp.zeros((1, RET_DV), F32)
            for j in range(nb):
                rows = pl.ds(j * RET_BLOCK, RET_BLOCK)
                ov = o_ref[rows, :]
                mu = jnp.mean(ov, axis=-1, keepdims=True)
                oc = ov - mu
                rstd = lax.rsqrt(jnp.mean(oc * oc, axis=-1, keepdims=True) + GN_EPS)
                y = oc * rstd
                g = g_ref[rows, :].astype(F32)
                sg = _sigmoid(g)
                da = da_ref[rows, :].astype(F32)
                dg_s[slot, rows, :] = (da * (y * gw) * (sg * (1.0 + g * (1.0 - sg)))).astype(BF16)
                dyw = da * (g * sg)
                dgw = dgw + jnp.sum(dyw * y, axis=0, keepdims=True)
                dy = dyw * gw
                do_s[rows, :] = rstd * (dy - jnp.mean(dy, axis=-1, keepdims=True)
                                        - y * jnp.mean(dy * y, axis=-1, keepdims=True))
                sts[j] = st
                st = st * bd + _dot((kr[rows, :] * kd).astype(BF16), v_ref[rows, :].astype(BF16), TN)

            @pl.when(b == 0)
            def _():
                dgw_ref[...] = dgw

            @pl.when(b > 0)
            def _():
                dgw_ref[...] += dgw

            rst[...] = jnp.zeros_like(rst)
            for j in reversed(range(nb)):
                rows = pl.ds(j * RET_BLOCK, RET_BLOCK)
                qb = qr[rows, :]
                kb = kr[rows, :]
                qbb, kbb = qb.astype(BF16), kb.astype(BF16)
                vb = v_ref[rows, :].astype(BF16)
                dob = do_s[rows, :]
                dobb = dob.astype(BF16)
                a_m = (_dot(qbb, kbb, NT) * wm).astype(BF16)
                b_m = (_dot(dobb, vb, NT) * wm).astype(BF16)
                rb = rst[...].astype(BF16)
                dq = _dot(b_m, kbb, NN) + _dot((dob * qd).astype(BF16), sts[j].astype(BF16), NT)
                dk = _dot(b_m, qbb, TN) + kd * _dot(vb, rb, NT)
                dv = _dot(a_m, dobb, TN) + kd * _dot(kbb, rb, NN)
                rst[...] = rst[...] * bd + _dot((qb * qd).astype(BF16), dobb, TN)
                cb, sb = c_ref[rows, :], s_ref[rows, :]
                dq_s[slot, rows, :] = _rotate_t(dq, cb, sb).astype(BF16)
                dk_s[slot, rows, :] = _rotate_t(dk * scale, cb, sb).astype(BF16)
                dv_s[slot, rows, :] = dv.astype(BF16)

        cols = [(COL_Q + h) * RET_DK, (COL_K + h) * RET_DK, (COL_V + h) * RET_DV, (COL_G + h) * RET_DV]
        _write_pieces(dp_ref, wsem, [dq_s, dk_s, dv_s, dg_s], b * S, cols, [h, b], grid, compute)

    blk = lambda w, off: pl.BlockSpec((S, w), lambda h, b: (b, off + h))
    return _pcall(
        body, name=name, grid=grid,
        in_specs=[blk(RET_DV, 0), blk(RET_DV, 0),
                  blk(RET_DK, COL_Q), blk(RET_DK, COL_K), blk(RET_DV, COL_V), blk(RET_DV, COL_G),
                  pl.BlockSpec((S, RET_DK), lambda h, b: (b, 0)), pl.BlockSpec((S, RET_DK), lambda h, b: (b, 0)),
                  pl.BlockSpec((None, 8, LANES), lambda h, b: (h, 0, 0)),
                  pl.BlockSpec((1, RET_DV), lambda h, b: (0, h)), HBM_SPEC],
        out_specs=[HBM_SPEC, pl.BlockSpec((1, RET_DV), lambda h, b: (0, h))],
        out_shape=[jax.ShapeDtypeStruct(dproj.shape, dproj.dtype),
                   jax.ShapeDtypeStruct((1, RET_HEADS * RET_DV), F32)],
        args=[da_in, o, proj, proj, proj, proj, cos2, sin2s, lgam, gn_w, dproj],
        scratch=[pltpu.VMEM((S, RET_DK), F32), pltpu.VMEM((S, RET_DK), F32),
                 pltpu.VMEM((S, RET_DV), F32), pltpu.VMEM((nb, RET_DK, RET_DV), F32),
                 pltpu.VMEM((RET_DK, RET_DV), F32),
                 pltpu.VMEM((2, S, RET_DK), BF16), pltpu.VMEM((2, S, RET_DK), BF16),
                 pltpu.VMEM((2, S, RET_DV), BF16), pltpu.VMEM((2, S, RET_DV), BF16), pltpu.SemaphoreType.DMA((2, 4))],
        sem=("arbitrary", "arbitrary"), jobs=jobs, alias_in_out={10: 0})


def _lru_gates(x, cw, cb, wr, wi, br, bi, lam):
    xc = cb + cw[LRU_CONV - 1:LRU_CONV, :] * x
    for j in range(LRU_CONV - 1):
        xc = xc + cw[j:j + 1, :] * _shift_down(x, LRU_CONV - 1 - j, 0.0)
    xcb = xc.astype(BF16)
    r = 1.0 / (1.0 + jnp.exp(-(_dot(xcb, wr, NN) + br)))
    ig = _sigmoid(_dot(xcb, wi, NN) + bi)
    z = -lam
    sp = jnp.maximum(z, 0.0) + jnp.log1p(jnp.exp(-jnp.abs(z)))
    log_a = (-LRU_C) * r * sp
    a = jnp.exp(log_a)
    om = -jnp.tanh(log_a) * (a * a + 1.0)
    sq = jnp.sqrt(om)
    return xc, xcb, r, ig, sp, a, sq


def _lru_fwd(proj, cw, cb, wr, wi, br, bi, lam, B, S, name):
    T = B * S
    W = LRU_BLOCKS * LRU_BLOCK

    def body(x_ref, y_ref, cw_ref, cb_ref, wr_ref, wi_ref, br_ref, bi_ref, lam_ref, h_ref, bin_ref, a_s, b_s):
        xc, _, _, ig, _, a, sq = _lru_gates(x_ref[...].astype(F32), cw_ref[...], cb_ref[...], wr_ref[...], wi_ref[...],
                                           br_ref[...], bi_ref[...], lam_ref[...])
        a_s[...] = a
        b_s[...] = sq * ig * xc
        _scan_forward_ref(a_s, b_s, h_ref)
        bin_ref[...] = (h_ref[...] * _gelu(y_ref[...].astype(F32))).astype(BF16)

    blk = lambda off: pl.BlockSpec((S, LRU_BLOCK), lambda b, n: (b, off + n))
    vec = lambda rows: pl.BlockSpec((rows, LRU_BLOCK), lambda b, n: (0, n))
    wspec = pl.BlockSpec((None, LRU_BLOCK, LRU_BLOCK), lambda b, n: (n, 0, 0))
    return _pcall(
        body, name=name, grid=(B, LRU_BLOCKS),
        in_specs=[blk(COL_XL), blk(COL_YL), vec(LRU_CONV), vec(1), wspec, wspec, vec(1), vec(1), vec(1)],
        out_specs=[blk(0), blk(0)],
        out_shape=[jax.ShapeDtypeStruct((T, W), F32), jax.ShapeDtypeStruct((T, W), BF16)],
        args=[proj, proj, cw, cb, wr, wi, br, bi, lam],
        scratch=[pltpu.VMEM((S, LRU_BLOCK), F32), pltpu.VMEM((S, LRU_BLOCK), F32)], sem=("parallel", "parallel"))


def _lru_bwd(db_in, h, proj, dproj, cw, cb, wr, wi, br, bi, lam, B, S, name, jobs=()):
    T = B * S
    W = LRU_BLOCKS * LRU_BLOCK

    grid = (LRU_BLOCKS, B)

    def body(dbin_ref, h_ref, x_ref, y_ref, cw_ref, cb_ref, wr_ref, wi_ref, br_ref, bi_ref, lam_ref, _,
             dp_ref, dcw_ref, dcb_ref, dwr_ref, dwi_ref, dbr_ref, dbi_ref, dlam_ref, dx_s, dy_s, wsem,
             an_s, u_s, dh_s):
        n, b = pl.program_id(0), pl.program_id(1)

        def compute(slot):
            x = x_ref[...].astype(F32)
            cw = cw_ref[...]
            wr, wi = wr_ref[...], wi_ref[...]
            lam = lam_ref[...]
            xc, xcb, r, ig, sp, a, sq = _lru_gates(x, cw, cb_ref[...], wr, wi, br_ref[...], bi_ref[...], lam)
            hv = h_ref[...]
            gel, dgel = _gelu_and_grad(y_ref[...].astype(F32))
            dbin = dbin_ref[...].astype(F32)
            dy_s[slot] = (dbin * hv * dgel).astype(BF16)
            an_s[...] = _shift_up(a, 1, 0.0)
            u_s[...] = dbin * gel
            _scan_backward_ref(an_s, u_s, dh_s)
            dh = dh_s[...]
            hprev = _shift_down(hv, 1, 0.0)
            dhs = dh * sq
            d_ig = dhs * xc
            d_xc = dhs * ig
            d_loga = (dh * a) * (hprev - (ig * xc) * (a / sq))
            d_r = d_loga * ((-LRU_C) * sp)
            d_sp = jnp.sum(d_loga * ((-LRU_C) * r), axis=0, keepdims=True)
            dlam = -d_sp * _sigmoid(-lam)
            d_pr = d_r * r * (1.0 - r)
            d_pi = d_ig * ig * (1.0 - ig)
            d_prb, d_pib = d_pr.astype(BF16), d_pi.astype(BF16)
            d_xc = d_xc + _dot(d_prb, wr, NT) + _dot(d_pib, wi, NT)

            @pl.when(b == 0)
            def _():
                for ref in (dcw_ref, dcb_ref, dwr_ref, dwi_ref, dbr_ref, dbi_ref, dlam_ref):
                    ref[...] = jnp.zeros_like(ref)

            dx = cw[LRU_CONV - 1:LRU_CONV, :] * d_xc
            for j in range(LRU_CONV - 1):
                sft = LRU_CONV - 1 - j
                dx = dx + cw[j:j + 1, :] * _shift_up(d_xc, sft, 0.0)
                dcw_ref[j:j + 1, :] += jnp.sum(d_xc * _shift_down(x, sft, 0.0), axis=0, keepdims=True)
            dcw_ref[LRU_CONV - 1:LRU_CONV, :] += jnp.sum(d_xc * x, axis=0, keepdims=True)
            dx_s[slot] = dx.astype(BF16)
            dcb_ref[...] += jnp.sum(d_xc, axis=0, keepdims=True)
            dwr_ref[...] += _dot(xcb, d_prb, TN)
            dwi_ref[...] += _dot(xcb, d_pib, TN)
            dbr_ref[...] += jnp.sum(d_pr, axis=0, keepdims=True)
            dbi_ref[...] += jnp.sum(d_pi, axis=0, keepdims=True)
            dlam_ref[...] += dlam

        cols = [(COL_XL + n) * LRU_BLOCK, (COL_YL + n) * LRU_BLOCK]
        _write_pieces(dp_ref, wsem, [dx_s, dy_s], b * S, cols, [n, b], grid, compute)

    blk = lambda off: pl.BlockSpec((S, LRU_BLOCK), lambda n, b: (b, off + n))
    vec = lambda rows: pl.BlockSpec((rows, LRU_BLOCK), lambda n, b: (0, n))
    wspec = pl.BlockSpec((None, LRU_BLOCK, LRU_BLOCK), lambda n, b: (n, 0, 0))
    vshape = lambda rows: jax.ShapeDtypeStruct((rows, W), F32)
    wshape = jax.ShapeDtypeStruct((LRU_BLOCKS, LRU_BLOCK, LRU_BLOCK), F32)
    return _pcall(
        body, name=name, grid=grid,
        in_specs=[blk(0), blk(0), blk(COL_XL), blk(COL_YL), vec(LRU_CONV), vec(1), wspec, wspec, vec(1), vec(1),
                  vec(1), HBM_SPEC],
        out_specs=[HBM_SPEC, vec(LRU_CONV), vec(1), wspec, wspec, vec(1), vec(1), vec(1)],
        out_shape=[jax.ShapeDtypeStruct(dproj.shape, dproj.dtype),
                   vshape(LRU_CONV), vshape(1), wshape, wshape, vshape(1), vshape(1), vshape(1)],
        args=[db_in, h, proj, proj, cw, cb, wr, wi, br, bi, lam, dproj],
        scratch=[pltpu.VMEM((2, S, LRU_BLOCK), BF16), pltpu.VMEM((2, S, LRU_BLOCK), BF16), pltpu.SemaphoreType.DMA((2, 2)),
                 pltpu.VMEM((S, LRU_BLOCK), F32), pltpu.VMEM((S, LRU_BLOCK), F32), pltpu.VMEM((S, LRU_BLOCK), F32)],
        sem=("arbitrary", "arbitrary"), jobs=jobs, alias_in_out={11: 0})


FFN_CT = 256


def _ffn_conv(gate, cw, cb):
    gc = cb + cw[FFN_CONV - 1:FFN_CONV, :] * gate
    for j in range(FFN_CONV - 1):
        gc = gc + cw[j:j + 1, :] * _shift_down(gate, FFN_CONV - 1 - j, 0.0)
    return gc


def _ffn_act_fwd(up, cw, cb, B, S, name):
    T = B * S
    nct = D_FF // FFN_CT

    def body(g_ref, v_ref, cw_ref, cb_ref, f_ref):
        gc = _ffn_conv(g_ref[...].astype(F32), cw_ref[...], cb_ref[...])
        f_ref[...] = (_gelu(gc) * v_ref[...].astype(F32)).astype(BF16)

    return _pcall(
        body, name=name, grid=(B, nct),
        in_specs=[pl.BlockSpec((S, FFN_CT), lambda b, c: (b, c)), pl.BlockSpec((S, FFN_CT), lambda b, c: (b, nct + c)),
                  pl.BlockSpec((FFN_CONV, FFN_CT), lambda b, c: (0, c)), pl.BlockSpec((1, FFN_CT), lambda b, c: (0, c))],
        out_specs=[pl.BlockSpec((S, FFN_CT), lambda b, c: (b, c))],
        out_shape=[jax.ShapeDtypeStruct((T, D_FF), BF16)], args=[up, up, cw, cb], sem=("parallel", "parallel"))[0]


def _ffn_act_bwd(df, up, cw, cb, B, S, name, jobs=()):
    T = B * S
    nct = D_FF // FFN_CT

    grid = (nct, B)

    def body(df_ref, g_ref, v_ref, cw_ref, cb_ref, du_ref, dcw_ref, dcb_ref, dg_s, dv_s, wsem):
        c, b = pl.program_id(0), pl.program_id(1)

        def compute(slot):
            gate = g_ref[...].astype(F32)
            cw = cw_ref[...]
            gc = _ffn_conv(gate, cw, cb_ref[...])
            gel, dgel = _gelu_and_grad(gc)
            dfv = df_ref[...].astype(F32)
            dv_s[slot] = (dfv * gel).astype(BF16)
            dgc = dfv * v_ref[...].astype(F32) * dgel

            @pl.when(b == 0)
            def _():
                dcw_ref[...] = jnp.zeros_like(dcw_ref)
                dcb_ref[...] = jnp.zeros_like(dcb_ref)

            dgate = cw[FFN_CONV - 1:FFN_CONV, :] * dgc
            for j in range(FFN_CONV - 1):
                sft = FFN_CONV - 1 - j
                dgate = dgate + cw[j:j + 1, :] * _shift_up(dgc, sft, 0.0)
                dcw_ref[j:j + 1, :] += jnp.sum(dgc * _shift_down(gate, sft, 0.0), axis=0, keepdims=True)
            dcw_ref[FFN_CONV - 1:FFN_CONV, :] += jnp.sum(dgc * gate, axis=0, keepdims=True)
            dg_s[slot] = dgate.astype(BF16)
            dcb_ref[...] += jnp.sum(dgc, axis=0, keepdims=True)

        _write_pieces(du_ref, wsem, [dg_s, dv_s], b * S, [c * FFN_CT, (nct + c) * FFN_CT], [c, b], grid, compute)

    blk = pl.BlockSpec((S, FFN_CT), lambda c, b: (b, c))
    return _pcall(
        body, name=name, grid=grid,
        in_specs=[blk, blk, pl.BlockSpec((S, FFN_CT), lambda c, b: (b, nct + c)),
                  pl.BlockSpec((FFN_CONV, FFN_CT), lambda c, b: (0, c)),
                  pl.BlockSpec((1, FFN_CT), lambda c, b: (0, c))],
        out_specs=[HBM_SPEC, pl.BlockSpec((FFN_CONV, FFN_CT), lambda c, b: (0, c)),
                   pl.BlockSpec((1, FFN_CT), lambda c, b: (0, c))],
        out_shape=[jax.ShapeDtypeStruct((T, 2 * D_FF), BF16),
                   jax.ShapeDtypeStruct((FFN_CONV, D_FF), F32), jax.ShapeDtypeStruct((1, D_FF), F32)],
        args=[df, up, up, cw, cb],
        scratch=[pltpu.VMEM((2, S, FFN_CT), BF16), pltpu.VMEM((2, S, FFN_CT), BF16), pltpu.SemaphoreType.DMA((2, 2))],
        sem=("arbitrary", "arbitrary"), jobs=jobs)


def _rs_add(g, recv, mode, core, name, also_bf16=False):
    shard = tuple(recv.shape[1:])
    if mode == "mid":
        a, e, c2 = shard
        g_in = g.reshape(a, N_DEV, e, c2)
        grid = (4, 1)
        g_spec = pl.BlockSpec((a, None, e, c2), lambda k, i, c_ref: (0, 2 * k + c_ref[0], 0, 0))
        r_spec = pl.BlockSpec((None, a, e, c2), lambda k, i, c_ref: (k, 0, 0, 0))
    else:
        R, C = shard
        tr = _row_tile(R, 512)
        grid = (4, R // tr)
        if mode == "rows":
            g_in = g.reshape(N_DEV, R, C)
            g_spec = pl.BlockSpec((None, tr, C), lambda k, i, c_ref: (2 * k + c_ref[0], i, 0))
        else:
            g_in = g
            g_spec = pl.BlockSpec((tr, C), lambda k, i, c_ref: (i, 2 * k + c_ref[0]))
        r_spec = pl.BlockSpec((None, tr, C), lambda k, i, c_ref: (k, i, 0))

    def body(c_ref, g_ref, r_ref, o_ref, *ob_ref):
        s = g_ref[...] + r_ref[...]
        o_ref[...] = s
        if also_bf16:
            ob_ref[0][...] = s.astype(BF16)

    out_shape = jax.ShapeDtypeStruct(recv.shape, recv.dtype)
    return pl.pallas_call(
        body, name=name,
        grid_spec=pltpu.PrefetchScalarGridSpec(
            num_scalar_prefetch=1, grid=grid, in_specs=[g_spec, r_spec],
            out_specs=[r_spec, r_spec] if also_bf16 else r_spec),
        out_shape=[out_shape, jax.ShapeDtypeStruct(recv.shape, BF16)] if also_bf16 else out_shape,
        compiler_params=pltpu.CompilerParams(dimension_semantics=("parallel", "parallel"),
                                             vmem_limit_bytes=VMEM_LIMIT),
    )(core, g_in, recv)


def _adam_update(gv, w, m, v):
    nm = ADAM_B1 * m + (1.0 - ADAM_B1) * gv
    nv = ADAM_B2 * v + (1.0 - ADAM_B2) * (gv * gv)
    m_hat = nm / (1.0 - ADAM_B1 ** ADAM_STEP)
    v_hat = nv / (1.0 - ADAM_B2 ** ADAM_STEP)
    delta = -ADAM_LR * (m_hat / (jnp.sqrt(v_hat) + ADAM_EPS) + ADAM_WD * w)
    return delta, nm, nv


def _adamw_shard(partial, recv, w, m, v, chip, name):
    shape = tuple(w.shape)
    tr = _row_tile(shape[0], 256)
    rest = shape[1:]
    zeros = (0,) * len(rest)
    tile = pl.BlockSpec((tr,) + rest, lambda i, s: (i,) + zeros)

    def body(_, p_ref, r_ref, w_ref, m_ref, v_ref, g_ref, d_ref, nm_ref, nv_ref):
        gv = p_ref[...] + r_ref[0].astype(F32) + r_ref[1].astype(F32) + r_ref[2].astype(F32)
        g_ref[...] = gv
        d_ref[...], nm_ref[...], nv_ref[...] = _adam_update(gv, w_ref[...], m_ref[...], v_ref[...])

    grid_spec = pltpu.PrefetchScalarGridSpec(
        num_scalar_prefetch=1, grid=(shape[0] // tr,),
        in_specs=[pl.BlockSpec((None, tr) + rest, lambda i, s: (s[0], i) + zeros),
                  pl.BlockSpec((3, tr) + rest, lambda i, s: (0, i) + zeros), tile, tile, tile],
        out_specs=[tile] * 4)
    return pl.pallas_call(
        body, name=name, grid_spec=grid_spec, out_shape=[jax.ShapeDtypeStruct(shape, F32)] * 4,
        compiler_params=pltpu.CompilerParams(dimension_semantics=("parallel",), vmem_limit_bytes=VMEM_LIMIT),
    )(chip, partial, recv, w, m, v)


SMALL_LANES = 1024


def _small_rows(shape):
    r, w = shape
    return r * max(1, w // SMALL_LANES)


def _small_allreduce(parts, name):
    n = len(parts)
    shapes = [tuple(p.shape) for p in parts]
    offs, total = [], 0
    for s in shapes:
        offs.append(total)
        total += _small_rows(s)
    rows = -(-total // 8) * 8

    def body(*refs):
        p_refs, o_refs = refs[:n], refs[n:2 * n]
        buf, tot, send_sems, recv_sems = refs[2 * n:]
        x, y, c = _mesh_pos()
        me, sibling = (x, y, c), (x, y, 1 - c)
        chips = _other_chips(x, y)

        def slot(px, py, pc):
            return buf.at[4 * px + 2 * py + pc]

        def copy(k, block, to):
            return _remote(slot(*block), slot(*block), send_sems.at[k], recv_sems.at[k], to)

        tot[...] = jnp.zeros_like(tot)
        for p_ref, (r, w), off in zip(p_refs, shapes, offs):
            wl = min(w, SMALL_LANES)
            for part in range(max(1, w // SMALL_LANES)):
                tot[pl.ds(off + part * r, r), pl.ds(0, wl)] = p_ref[:, pl.ds(part * SMALL_LANES, wl)]
        buf[4 * x + 2 * y + c] = tot[...]
        first = [copy(0, me, sibling)] + [copy(1 + j, me, (*chip, c)) for j, chip in enumerate(chips)]
        for cp in first:
            cp.start()
        passed = [copy(4 + j, (*chip, c), sibling) for j, chip in enumerate(chips)]
        for j, chip in enumerate(chips):
            copy(1 + j, (*chip, c), me).wait_recv()
            passed[j].start()
        copy(0, sibling, me).wait_recv()
        for j, chip in enumerate(chips):
            copy(4 + j, (*chip, 1 - c), me).wait_recv()
        for cp in first + passed:
            cp.wait_send()
        acc = buf[0]
        for d in range(1, N_DEV):
            acc = acc + buf[d]
        tot[...] = acc
        for o_ref, (r, w), off in zip(o_refs, shapes, offs):
            wl = min(w, SMALL_LANES)
            for part in range(max(1, w // SMALL_LANES)):
                o_ref[:, pl.ds(part * SMALL_LANES, wl)] = tot[pl.ds(off + part * r, r), pl.ds(0, wl)]

    vm = pl.BlockSpec(memory_space=pltpu.VMEM)
    return pl.pallas_call(
        body, name=name,
        in_specs=[vm] * n, out_specs=[vm] * n,
        out_shape=[jax.ShapeDtypeStruct(s, F32) for s in shapes],
        scratch_shapes=[pltpu.VMEM((N_DEV, rows, SMALL_LANES), F32), pltpu.VMEM((rows, SMALL_LANES), F32),
                        pltpu.SemaphoreType.DMA((7,)), pltpu.SemaphoreType.DMA((7,))],
    )(*parts)


def _adamw_small(gs, ws, ms, vs, name):
    n = len(gs)

    def body(*refs):
        g_r, w_r, m_r, v_r = refs[:n], refs[n:2 * n], refs[2 * n:3 * n], refs[3 * n:4 * n]
        d_r, nm_r, nv_r = refs[4 * n:5 * n], refs[5 * n:6 * n], refs[6 * n:7 * n]
        for i in range(n):
            d_r[i][...], nm_r[i][...], nv_r[i][...] = _adam_update(g_r[i][...], w_r[i][...], m_r[i][...], v_r[i][...])

    vm = pl.BlockSpec(memory_space=pltpu.VMEM)
    shapes = [jax.ShapeDtypeStruct(w.shape, F32) for w in ws]
    outs = pl.pallas_call(body, name=name, in_specs=[vm] * (4 * n), out_specs=[vm] * (3 * n),
                          out_shape=shapes * 3)(*gs, *ws, *ms, *vs)
    return outs[:n], outs[n:2 * n], outs[2 * n:]


FIRST = [("w_in", (1024, 896), "cols"), ("lru_w_r", (4, 32, 256), "mid"), ("lru_w_i", (4, 32, 256), "mid")]
LATE = [("w_ret_o", (128, 1024), "rows"), ("w_lru_o", (128, 1024), "rows"), ("w_out", (128, 1024), "rows"),
        ("ffn_w_up", (1024, 768), "cols"), ("ffn_w_down", (384, 1024), "rows")]
BIG = FIRST + LATE
SMALL_SHARDED = [("merge_gate_b", (2, 128), "cols"), ("lru_conv_w", (4, 128), "cols"), ("lru_b_r", (4, 32), "stack"),
                 ("lru_b_i", (4, 32), "stack"), ("ffn_conv_w", (3, 384), "cols")]
REPLICATED = [("norm1_w", (1, 1024)), ("ret_gn_w", (1, 1024)), ("lru_conv_b", (1, 1024)), ("lru_lambda", (1, 1024)),
              ("norm2_w", (1, 1024)), ("ffn_conv_b", (1, 3072)), ("norm_f_w", (1, 1024))]
MODE = {n: m for n, _, m in BIG}
SHARD = {n: s for n, s, _ in BIG}


def _local_step(x3, positions, target3, first_shards, ws, late_shards, core):
    B, S, D = x3.shape
    T = B * S
    x = x3.reshape(T, D)
    target = target3.reshape(T, D)
    tm = min(512, T)
    big = min(1024, T)
    big2 = min(2048, T)
    big4 = min(4096, T)

    half = RET_DK // 2
    inv_freq = ROPE_BASE ** (-jnp.arange(half, dtype=F32) / half)
    inv2 = jnp.concatenate([inv_freq, inv_freq]).reshape(1, RET_DK)
    log_gamma = jnp.log1p(-jnp.power(2.0, -5.0 - jnp.arange(RET_HEADS, dtype=F32)))
    lgam = jnp.broadcast_to(log_gamma[:, None, None], (RET_HEADS, 8, LANES))
    pos_col = positions.astype(F32).reshape(T, 1)
    late_names = [n for n, _, _ in LATE]
    late_modes = [m for _, _, m in LATE]
    late_shapes = [s for _, s, _ in LATE]
    first = FIRST + SMALL_SHARDED
    first_modes = [m for _, _, m in first]

    cos2, sin2s, h1, *first_part = _rope_and_norm(pos_col, inv2, x, ws["norm1_w"], tm, "rope_norm1_fwd",
                                                  jobs=[_ag_first_job(first_shards, first_modes)])
    first_full = _pcall(lambda: None, name="gather_first_pass_on", grid=(1,), in_specs=[], out_specs=[], out_shape=[],
                        args=[], sem=("arbitrary",),
                        jobs=[_ag_second_job(first_part, first_modes, [s for _, s, _ in first])])
    gathered = dict(zip([n for n, _, _ in first], first_full))
    wb = {n: gathered[n] for n, _, _ in FIRST}
    ws = dict(ws, **{n: gathered[n] for n, _, _ in SMALL_SHARDED})
    for n in ("lru_b_r", "lru_b_i"):
        ws[n] = jnp.transpose(ws[n], (1, 0, 2)).reshape(1, LRU_BLOCKS * LRU_BLOCK)
    proj, *late_part = _matmul(h1, wb["w_in"], "nn", BF16, big2, 1024, 1024, "proj_fwd",
                               jobs=[_ag_first_job(late_shards, late_modes)])
    o, a_in, *late_full = _retention_fwd(proj, cos2, sin2s, lgam, ws["ret_gn_w"], B, S, "retention_fwd",
                                         jobs=[_ag_second_job(late_part, late_modes, late_shapes)])
    wb = dict(wb, **dict(zip(late_names, late_full)))
    hl, b_in = _lru_fwd(proj, ws["lru_conv_w"], ws["lru_conv_b"], wb["lru_w_r"], wb["lru_w_i"],
                        ws["lru_b_r"], ws["lru_b_i"], ws["lru_lambda"], B, S, "lru_fwd")
    x1, mix, h2, ya, yb = _mix_fwd(a_in, b_in, proj, x, wb["w_ret_o"], wb["w_lru_o"], wb["w_out"],
                                   ws["merge_gate_b"], ws["norm2_w"], tm, "mix_fwd")
    up = _matmul(h2, wb["ffn_w_up"], "nn", BF16, big2, 1024, 1024, "ffn_up_fwd")[0]
    f = _ffn_act_fwd(up, ws["ffn_conv_w"], ws["ffn_conv_b"], B, S, "ffn_act_fwd")
    x2 = _matmul(f, wb["ffn_w_down"], "nn", F32, big, 1024, D_FF, "ffn_down_fwd", add=x1)[0]
    dx2, dx2b, loss_acc, d_norm_f = _loss_head(x2, target, ws["norm_f_w"], tm, "loss_head")

    g, rs = {}, {}

    def stage1(names, grads):
        return _rs_sibling_job(grads, [MODE[n] for n in names], [SHARD[n] for n in names])

    def add(names, grads, recvs):
        return [_rs_add(gr, r, MODE[n], core, "rs_add_" + n) for n, gr, r in zip(names, grads, recvs)]

    g["norm_f_w"] = d_norm_f
    g_down = _matmul(f, dx2b, "tn", F32, 1024, 1024, big4, "ffn_down_bwd_w")[0]
    df, s1_down = _matmul(dx2b, wb["ffn_w_down"], "nt", BF16, big2, 1024, 1024, "ffn_down_bwd_x",
                          jobs=[stage1(["ffn_w_down"], [g_down])])
    p_down = add(["ffn_w_down"], [g_down], [s1_down])
    dup, g["ffn_conv_w"], g["ffn_conv_b"], s2_down = _ffn_act_bwd(
        df, up, ws["ffn_conv_w"], ws["ffn_conv_b"], B, S, "ffn_act_bwd", jobs=[_rs_chip_job(p_down)])
    rs["ffn_w_down"] = (p_down[0], s2_down)

    g_up = _matmul(h2, dup, "tn", F32, 1024, 1024, big4, "ffn_up_bwd_w")[0]
    dh2, s1_up = _matmul(dup, wb["ffn_w_up"], "nt", BF16, big, 1024, D_FF, "ffn_up_bwd_x",
                         jobs=[stage1(["ffn_w_up"], [g_up])])
    p_up = add(["ffn_w_up"], [g_up], [s1_up])
    dx1, dx1b, g["norm2_w"] = _rmsnorm_bwd_add(dx2, dh2, x1, ws["norm2_w"], tm, "norm2_bwd", True)
    da_in, db_in, dya, dyb, dproj, g["merge_gate_b"] = _mix_bwd(
        dx1b, ya, yb, proj, wb["w_ret_o"], wb["w_lru_o"], wb["w_out"], ws["merge_gate_b"], tm, "mix_bwd")

    mid_names = ["w_out", "w_ret_o", "w_lru_o"]
    g_mid = [_matmul(mix, dx1b, "tn", F32, 1024, 1024, big4, "w_out_bwd_w")[0],
             _matmul(a_in, dya, "tn", F32, 1024, 1024, big4, "w_ret_o_bwd_w")[0],
             _matmul(b_in, dyb, "tn", F32, 1024, 1024, big4, "w_lru_o_bwd_w")[0]]
    (dproj, g["lru_conv_w"], g["lru_conv_b"], g_wr, g_wi, g["lru_b_r"], g["lru_b_i"], g["lru_lambda"], s2_up,
     *s1_mid) = _lru_bwd(db_in, hl, proj, dproj, ws["lru_conv_w"], ws["lru_conv_b"], wb["lru_w_r"], wb["lru_w_i"],
                         ws["lru_b_r"], ws["lru_b_i"], ws["lru_lambda"], B, S, "lru_bwd",
                         jobs=[_rs_chip_job(p_up), stage1(mid_names, g_mid)])
    rs["ffn_w_up"] = (p_up[0], s2_up)
    p_mid = add(mid_names, g_mid, s1_mid)
    lru_names = ["lru_w_r", "lru_w_i"]
    dproj, g["ret_gn_w"], *rest = _retention_bwd(
        da_in, o, proj, dproj, cos2, sin2s, lgam, ws["ret_gn_w"], B, S, "retention_bwd",
        jobs=[_rs_chip_job(p_mid), stage1(lru_names, [g_wr, g_wi])])
    s2_mid, s1_lru = rest[:3], rest[3:]
    for n, p, r in zip(mid_names, p_mid, s2_mid):
        rs[n] = (p, r)
    p_lru = add(lru_names, [g_wr, g_wi], s1_lru)

    g_in, *s2_lru = _matmul(h1, dproj, "tn", F32, 1024, 1024, big4, "proj_bwd_w", jobs=[_rs_chip_job(p_lru)])
    for n, p, r in zip(lru_names, p_lru, s2_lru):
        rs[n] = (p, r)
    s1_in = _pcall(lambda: None, name="rs_sibling_w_in", grid=(1,), in_specs=[], out_specs=[], out_shape=[], args=[],
                   sem=("arbitrary",), jobs=[stage1(["w_in"], [g_in])])
    p_in, p_in_bf16 = _rs_add(g_in, s1_in[0], MODE["w_in"], core, "rs_add_w_in", also_bf16=True)
    dh1, s2_in = _matmul(dproj, wb["w_in"], "nt", BF16, big, 1024, D_IN // 2, "proj_bwd_x",
                         jobs=[_rs_chip_job([p_in_bf16])])
    grad_x, g["norm1_w"] = _rmsnorm_bwd_add(dx1, dh1, x, ws["norm1_w"], tm, "norm1_bwd", False)
    rs["w_in"] = (p_in, s2_in)
    return loss_acc, grad_x.reshape(B, S, D), g, rs


def kernel(x, positions, norm1_w, w_in, merge_gate_b, ret_gn_w, w_ret_o, lru_conv_w, lru_conv_b, lru_w_r, lru_b_r, lru_w_i, lru_b_i, lru_lambda, w_lru_o, w_out, norm2_w, ffn_w_up, ffn_conv_w, ffn_conv_b, ffn_w_down, norm_f_w, loss_target, m_norm1_w, m_w_in, m_merge_gate_b, m_ret_gn_w, m_w_ret_o, m_lru_conv_w, m_lru_conv_b, m_lru_w_r, m_lru_b_r, m_lru_w_i, m_lru_b_i, m_lru_lambda, m_w_lru_o, m_w_out, m_norm2_w, m_ffn_w_up, m_ffn_conv_w, m_ffn_conv_b, m_ffn_w_down, m_norm_f_w, v_norm1_w, v_w_in, v_merge_gate_b, v_ret_gn_w, v_w_ret_o, v_lru_conv_w, v_lru_conv_b, v_lru_w_r, v_lru_b_r, v_lru_w_i, v_lru_b_i, v_lru_lambda, v_w_lru_o, v_w_out, v_norm2_w, v_ffn_w_up, v_ffn_conv_w, v_ffn_conv_b, v_ffn_w_down, v_norm_f_w):
    names = ["norm1_w", "w_in", "merge_gate_b", "ret_gn_w", "w_ret_o", "lru_conv_w", "lru_conv_b", "lru_w_r", "lru_b_r",
             "lru_w_i", "lru_b_i", "lru_lambda", "w_lru_o", "w_out", "norm2_w", "ffn_w_up", "ffn_conv_w", "ffn_conv_b",
             "ffn_w_down", "norm_f_w"]
    w_args = [norm1_w, w_in, merge_gate_b, ret_gn_w, w_ret_o, lru_conv_w, lru_conv_b, lru_w_r, lru_b_r, lru_w_i, lru_b_i,
              lru_lambda, w_lru_o, w_out, norm2_w, ffn_w_up, ffn_conv_w, ffn_conv_b, ffn_w_down, norm_f_w]
    m_args = [m_norm1_w, m_w_in, m_merge_gate_b, m_ret_gn_w, m_w_ret_o, m_lru_conv_w, m_lru_conv_b, m_lru_w_r, m_lru_b_r,
              m_lru_w_i, m_lru_b_i, m_lru_lambda, m_w_lru_o, m_w_out, m_norm2_w, m_ffn_w_up, m_ffn_conv_w, m_ffn_conv_b,
              m_ffn_w_down, m_norm_f_w]
    v_args = [v_norm1_w, v_w_in, v_merge_gate_b, v_ret_gn_w, v_w_ret_o, v_lru_conv_w, v_lru_conv_b, v_lru_w_r, v_lru_b_r,
              v_lru_w_i, v_lru_b_i, v_lru_lambda, v_w_lru_o, v_w_out, v_norm2_w, v_ffn_w_up, v_ffn_conv_w, v_ffn_conv_b,
              v_ffn_w_down, v_norm_f_w]
    orig_shape = {n: a.shape for n, a in zip(names, w_args)}
    local_shape = {n: s for n, s, _ in BIG + SMALL_SHARDED}
    local_shape.update({n: s for n, s in REPLICATED})
    W = {n: a.reshape(local_shape[n]) for n, a in zip(names, w_args)}
    M = {n: a.reshape(local_shape[n]) for n, a in zip(names, m_args)}
    V = {n: a.reshape(local_shape[n]) for n, a in zip(names, v_args)}

    xi, yi, ci = _mesh_pos()
    dev = 4 * xi + 2 * yi + ci
    chip = (2 * xi + yi).astype(jnp.int32).reshape(1)
    core = ci.astype(jnp.int32).reshape(1)

    small_names = [n for n, _, _ in SMALL_SHARDED]
    first_shards = [W[n].astype(BF16) for n, _, _ in FIRST] + [W[n] for n in small_names]
    late_shards = [W[n].astype(BF16) for n, _, _ in LATE]
    rep = {n: W[n] for n, _ in REPLICATED}
    loss_acc, grad_x, g, rs = _local_step(x, positions, loss_target, first_shards, rep, late_shards, core)

    G_out, D_out, M_out, V_out = {}, {}, {}, {}
    for n, _, _ in BIG:
        G_out[n], D_out[n], M_out[n], V_out[n] = _adamw_shard(rs[n][0], rs[n][1], W[n], M[n], V[n], chip, "adamw_" + n)

    rep_names = [n for n, _ in REPLICATED]
    red_names = rep_names + small_names
    red = _small_allreduce([g[n] for n in red_names] + [loss_acc[0:1, :]], "allreduce_small_grads")
    loss = red[-1][0, 0]
    gs = dict(zip(red_names, red[:-1]))
    for n, s, mode in SMALL_SHARDED:
        if mode == "cols":
            gs[n] = lax.dynamic_slice_in_dim(gs[n], dev * s[1], s[1], axis=1)
        else:
            full = gs[n].reshape(LRU_BLOCKS, LRU_BLOCK)
            gs[n] = lax.dynamic_slice_in_dim(full, dev * s[1], s[1], axis=1)
    d2, m2, v2 = _adamw_small([gs[n] for n in red_names], [W[n] for n in red_names], [M[n] for n in red_names],
                              [V[n] for n in red_names], "adamw_small")
    for i, n in enumerate(red_names):
        G_out[n], D_out[n], M_out[n], V_out[n] = gs[n], d2[i], m2[i], v2[i]

    outs = [loss, grad_x]
    for group in (G_out, D_out, M_out, V_out):
        outs += [group[n].reshape(orig_shape[n]) for n in names]
    return tuple(outs)
```

```python
import math

import jax
import jax.numpy as jnp
from jax import lax
from jax.experimental import pallas as pl
from jax.experimental.pallas import tpu as pltpu

F32 = jnp.float32
BF16 = jnp.bfloat16
MESH = pl.DeviceIdType.MESH

D_MODEL = 1024
CHUNK = 64
RET_HEADS = 4
RET_DK = 128
RET_DV = 256
LRU_BLOCKS = 4
LRU_BLOCK = 256
LRU_CONV = 4
LRU_C = 8.0
D_FF = 3072
FFN_CONV = 3
ROPE_BASE = 10000.0
RMS_EPS = 1e-6
GN_EPS = 1e-6
D_IN = 7168
ADAM_LR, ADAM_B1, ADAM_B2, ADAM_EPS, ADAM_WD, ADAM_STEP = 0.001, 0.9, 0.999, 1e-08, 0.01, 10

N_DEV = 8
V7X_VMEM_BYTES = 64 * 1024 * 1024
VMEM_LIMIT = V7X_VMEM_BYTES - 4 * 1024 * 1024
RET_BLOCK = 256
LANES = 128

COL_Q, COL_K = 0, 4
COL_V, COL_G, COL_XL, COL_YL = 4, 8, 12, 16
COL_GR, COL_GL = 5, 6

HBM_SPEC = pl.BlockSpec(memory_space=pl.ANY)


def _gelu(x):
    c = math.sqrt(2.0 / math.pi)
    t = jnp.tanh(x * (c + (c * 0.044715) * (x * x)))
    return x * (0.5 * t + 0.5)


def _gelu_and_grad(x):
    c = math.sqrt(2.0 / math.pi)
    x2 = x * x
    t = jnp.tanh(x * (c + (c * 0.044715) * x2))
    h = 0.5 * t + 0.5
    g = x * h
    dg = h + g * (1.0 - h) * ((2.0 * c) + (6.0 * c * 0.044715) * x2)
    return g, dg


def _sigmoid(x):
    return 0.5 * jnp.tanh(0.5 * x) + 0.5


SUBLANES = 8


def _shift_down(x, s, fill):
    r = pltpu.roll(x, s, 0)
    rows = lax.broadcasted_iota(jnp.int32, (SUBLANES,) + x.shape[1:], 0)
    top = jnp.where(rows >= s, r[:SUBLANES], fill)
    return jnp.concatenate([top, r[SUBLANES:]], axis=0)


def _shift_up(x, s, fill):
    n = x.shape[0]
    r = pltpu.roll(x, n - s, 0)
    rows = lax.broadcasted_iota(jnp.int32, (SUBLANES,) + x.shape[1:], 0)
    bottom = jnp.where(rows < SUBLANES - s, r[n - SUBLANES:], fill)
    return jnp.concatenate([r[:n - SUBLANES], bottom], axis=0)


SCAN_CHUNK = 64


def _scan_forward(a, b):
    n = a.shape[0]
    s = 1
    while s < n:
        if s % SUBLANES:
            b = a * _shift_down(b, s, 0.0) + b
            a = a * _shift_down(a, s, 1.0)
        else:
            b = jnp.concatenate([b[:s], a[s:] * b[:n - s] + b[s:]], axis=0)
            a = jnp.concatenate([a[:s], a[s:] * a[:n - s]], axis=0)
        s *= 2
    return a, b


def _scan_backward(a_next, u):
    n = u.shape[0]
    s = 1
    while s < n:
        if s % SUBLANES:
            u = u + a_next * _shift_up(u, s, 0.0)
            a_next = a_next * _shift_up(a_next, s, 1.0)
        else:
            u = jnp.concatenate([u[:n - s] + a_next[:n - s] * u[s:], u[n - s:]], axis=0)
            a_next = jnp.concatenate([a_next[:n - s] * a_next[s:], a_next[n - s:]], axis=0)
        s *= 2
    return a_next, u


def _scan_forward_ref(a_ref, b_ref, h_ref):
    S, W = a_ref.shape
    for strip in range(W // LANES):
        cols = pl.ds(strip * LANES, LANES)

        def body(k, carry, cols=cols):
            rows = pl.ds(pl.multiple_of(k * SCAN_CHUNK, SCAN_CHUNK), SCAN_CHUNK)
            a_cum, h_loc = _scan_forward(a_ref[rows, cols], b_ref[rows, cols])
            h = h_loc + a_cum * carry
            h_ref[rows, cols] = h
            return h[SCAN_CHUNK - 1:, :]

        lax.fori_loop(0, S // SCAN_CHUNK, body, jnp.zeros((1, LANES), F32))


def _scan_backward_ref(an_ref, u_ref, d_ref):
    S, W = an_ref.shape
    n_chunks = S // SCAN_CHUNK
    for strip in range(W // LANES):
        cols = pl.ds(strip * LANES, LANES)

        def body(i, carry, cols=cols):
            rows = pl.ds(pl.multiple_of((n_chunks - 1 - i) * SCAN_CHUNK, SCAN_CHUNK), SCAN_CHUNK)
            an_cum, d_loc = _scan_backward(an_ref[rows, cols], u_ref[rows, cols])
            d = d_loc + an_cum * carry
            d_ref[rows, cols] = d
            return d[:1, :]

        lax.fori_loop(0, n_chunks, body, jnp.zeros((1, LANES), F32))


def _dot(a, b, dims):
    return lax.dot_general(a, b, (dims, ((), ())), preferred_element_type=F32)


NN = ((1,), (0,))
NT = ((1,), (1,))
TN = ((0,), (0,))


def _mesh_pos():
    return lax.axis_index("x"), lax.axis_index("y"), lax.axis_index("c")


def _other_chips(x, y):
    return [(1 - x, y), (x, 1 - y), (1 - x, 1 - y)]


def _full_shape(shard_shape, mode):
    if mode == "rows":
        return (N_DEV * shard_shape[0],) + tuple(shard_shape[1:])
    if mode == "cols":
        return (shard_shape[0], N_DEV * shard_shape[1])
    if mode == "mid":
        return (shard_shape[0], N_DEV * shard_shape[1], shard_shape[2])
    return (N_DEV,) + tuple(shard_shape)


def _extent(shard_shape, mode):
    return {"rows": shard_shape[0], "cols": shard_shape[1], "mid": shard_shape[1], "stack": 1}[mode]


def _window(ref, mode, extent, d):
    if mode == "stack":
        return ref.at[d]
    start = pl.multiple_of(d * extent, extent)
    if mode == "rows":
        return ref.at[pl.ds(start, extent)]
    if mode == "cols":
        return ref.at[:, pl.ds(start, extent)]
    return ref.at[:, pl.ds(start, extent), :]


class _Job:
    def __init__(self, inputs, out_shapes, sems, start, finish, aliases=None):
        self.inputs, self.out_shapes, self.sems = list(inputs), list(out_shapes), sems
        self.start, self.finish, self.aliases = start, finish, dict(aliases or {})


def _remote(src, dst, send_sem, recv_sem, to):
    return pltpu.make_async_remote_copy(src_ref=src, dst_ref=dst, send_sem=send_sem, recv_sem=recv_sem,
                                        device_id=to, device_id_type=MESH)


def _ag_first_job(shards, modes):
    n = len(shards)
    extents = [_extent(s.shape, m) for s, m in zip(shards, modes)]

    def copies(x_refs, out_refs, send, recv, local, arriving):
        x, y, c = _mesh_pos()
        peers = [(x, y, 1 - c)] + [(*chip, c) for chip in _other_chips(x, y)]
        win = lambda i, p: _window(out_refs[i], modes[i], extents[i], 4 * p[0] + 2 * p[1] + p[2])
        if arriving:
            return [_remote(x_refs[i], win(i, p), send.at[i, k], recv.at[i, k], p)
                    for i in range(n) for k, p in enumerate(peers)]
        mine = [pltpu.make_async_copy(x_refs[i], win(i, (x, y, c)), local.at[i]) for i in range(n)]
        sends = [_remote(x_refs[i], win(i, (x, y, c)), send.at[i, k], recv.at[i, k], p)
                 for i in range(n) for k, p in enumerate(peers)]
        return mine, sends

    def start(*refs):
        mine, sends = copies(*refs, False)
        for cp in mine + sends:
            cp.start()

    def finish(*refs):
        for cp in copies(*refs, True):
            cp.wait_recv()
        mine, sends = copies(*refs, False)
        for cp in sends:
            cp.wait_send()
        for cp in mine:
            cp.wait()

    out_shapes = [jax.ShapeDtypeStruct(_full_shape(s.shape, m), s.dtype) for s, m in zip(shards, modes)]
    return _Job(shards, out_shapes, ((n, 4), (n, 4), (n,)), start, finish)


def _ag_second_job(fulls, modes, shard_shapes):
    n = len(fulls)
    extents = [_extent(s, m) for s, m in zip(shard_shapes, modes)]

    def copies(_, out_refs, send, recv, local, core_of_block):
        x, y, c = _mesh_pos()
        pc = c if core_of_block == "mine" else 1 - c
        win = lambda i, chip: _window(out_refs[i], modes[i], extents[i], 4 * chip[0] + 2 * chip[1] + pc)
        return [_remote(win(i, chip), win(i, chip), send.at[i, j], recv.at[i, j], (x, y, 1 - c))
                for i in range(n) for j, chip in enumerate(_other_chips(x, y))]

    def start(*refs):
        for cp in copies(*refs, "mine"):
            cp.start()

    def finish(*refs):
        for cp in copies(*refs, "sibling"):
            cp.wait_recv()
        for cp in copies(*refs, "mine"):
            cp.wait_send()

    out_shapes = [jax.ShapeDtypeStruct(f.shape, f.dtype) for f in fulls]
    return _Job(fulls, out_shapes, ((n, 3), (n, 3), (1,)), start, finish, aliases={i: i for i in range(n)})


def _rs_sibling_job(grads, modes, shard_shapes):
    n = len(grads)
    extents = [_extent(s, m) for s, m in zip(shard_shapes, modes)]

    def copies(g_refs, out_refs, send, recv, local):
        x, y, c = _mesh_pos()
        return [_remote(_window(g_refs[i], modes[i], extents[i], 2 * k + (1 - c)), out_refs[i].at[k],
                        send.at[i, k], recv.at[i, k], (x, y, 1 - c))
                for i in range(n) for k in range(4)]

    def start(*refs):
        for cp in copies(*refs):
            cp.start()

    def finish(*refs):
        cps = copies(*refs)
        for cp in cps:
            cp.wait_recv()
        for cp in cps:
            cp.wait_send()

    out_shapes = [jax.ShapeDtypeStruct((4,) + tuple(s), g.dtype) for s, g in zip(shard_shapes, grads)]
    return _Job(grads, out_shapes, ((n, 4), (n, 4), (1,)), start, finish)


def _rs_chip_job(partials):
    n = len(partials)

    def copies(p_refs, out_refs, send, recv, local):
        x, y, c = _mesh_pos()
        return [_remote(p_refs[i].at[2 * px + py], out_refs[i].at[j], send.at[i, j], recv.at[i, j], (px, py, c))
                for i in range(n) for j, (px, py) in enumerate(_other_chips(x, y))]

    def start(*refs):
        for cp in copies(*refs):
            cp.start()

    def finish(*refs):
        cps = copies(*refs)
        for cp in cps:
            cp.wait_recv()
        for cp in cps:
            cp.wait_send()

    out_shapes = [jax.ShapeDtypeStruct((3,) + tuple(p.shape[1:]), p.dtype) for p in partials]
    return _Job(partials, out_shapes, ((n, 3), (n, 3), (1,)), start, finish)


def _all_true(conds):
    out = conds[0]
    for c in conds[1:]:
        out = jnp.logical_and(out, c)
    return out


def _pcall(body, *, name, grid, in_specs, out_specs, out_shape, args, sem, scratch=(), jobs=(), alias_in_out=None):
    n_in, n_out, n_scr = len(args), len(out_shape), len(scratch)
    job_in = [a for j in jobs for a in j.inputs]
    job_out = [s for j in jobs for s in j.out_shapes]
    job_sems = [pltpu.SemaphoreType.DMA(shape) for j in jobs for shape in j.sems]
    aliases, in_off, out_off = dict(alias_in_out or {}), n_in, n_out
    for j in jobs:
        for a, b in j.aliases.items():
            aliases[in_off + a] = out_off + b
        in_off += len(j.inputs)
        out_off += len(j.out_shapes)

    def wrapped(*refs):
        ins = refs[:n_in]
        jins = refs[n_in:n_in + len(job_in)]
        o0 = n_in + len(job_in)
        outs = refs[o0:o0 + n_out]
        jouts = refs[o0 + n_out:o0 + n_out + len(job_out)]
        s0 = o0 + n_out + len(job_out)
        scr = refs[s0:s0 + n_scr]
        jsems = refs[s0 + n_scr:]
        if jobs:
            ids = [pl.program_id(a) for a in range(len(grid))]
            first = _all_true([i == 0 for i in ids])
            last = _all_true([i == g - 1 for i, g in zip(ids, grid)])

            def per_job(which):
                i0 = o0_ = 0
                for k, j in enumerate(jobs):
                    fn = j.start if which == "start" else j.finish
                    fn(jins[i0:i0 + len(j.inputs)], jouts[o0_:o0_ + len(j.out_shapes)], *jsems[3 * k:3 * k + 3])
                    i0 += len(j.inputs)
                    o0_ += len(j.out_shapes)

            @pl.when(first)
            def _():
                per_job("start")

        body(*ins, *outs, *scr)
        if jobs:
            @pl.when(last)
            def _():
                per_job("finish")

    semantics = tuple("arbitrary" for _ in grid) if jobs else sem
    return pl.pallas_call(
        wrapped, name=name, grid=grid,
        in_specs=list(in_specs) + [HBM_SPEC] * len(job_in),
        out_specs=list(out_specs) + [HBM_SPEC] * len(job_out),
        out_shape=list(out_shape) + job_out,
        scratch_shapes=list(scratch) + job_sems,
        input_output_aliases=aliases,
        compiler_params=pltpu.CompilerParams(dimension_semantics=semantics, vmem_limit_bytes=VMEM_LIMIT),
    )(*args, *job_in)


def _row_tile(rows, cap):
    if rows <= cap:
        return rows
    best = None
    for t in range(16, cap + 1, 16):
        if rows % t == 0:
            best = t
    assert best is not None
    return best


def _matmul(a, b, mode, out_dtype, tm, tn, tk, name, add=None, jobs=(), also_bf16=False):
    if mode == "tn":
        K, M = a.shape
    else:
        M, K = a.shape
    N = b.shape[0] if mode == "nt" else b.shape[1]
    tm, tn, tk = min(tm, M), min(tn, N), min(tk, K)
    assert M % tm == 0 and N % tn == 0 and K % tk == 0
    nk = K // tk
    dims = {"nn": NN, "nt": NT, "tn": TN}[mode]

    def body(*refs):
        a_ref, b_ref = refs[:2]
        add_ref = refs[2] if add is not None else None
        outs, acc = refs[(3 if add is not None else 2):-1], refs[-1]
        k = pl.program_id(2)
        p = _dot(a_ref[...], b_ref[...], dims)

        def finish(r):
            if add_ref is not None:
                r = r + add_ref[...].astype(F32)
            outs[0][...] = r.astype(out_dtype)
            if also_bf16:
                outs[1][...] = r.astype(BF16)

        if nk == 1:
            finish(p)
        else:
            @pl.when(k == 0)
            def _():
                acc[...] = p

            @pl.when(k > 0)
            def _():
                acc[...] += p

            @pl.when(k == nk - 1)
            def _():
                finish(acc[...])

    if mode == "tn":
        a_spec = pl.BlockSpec((tk, tm), lambda i, j, k: (k, i))
    else:
        a_spec = pl.BlockSpec((tm, tk), lambda i, j, k: (i, k))
    if mode == "nt":
        b_spec = pl.BlockSpec((tn, tk), lambda i, j, k: (j, k))
    else:
        b_spec = pl.BlockSpec((tk, tn), lambda i, j, k: (k, j))
    in_specs = [a_spec, b_spec]
    args = [a, b]
    if add is not None:
        in_specs.append(pl.BlockSpec((tm, tn), lambda i, j, k: (i, j)))
        args.append(add)
    n_out = 2 if also_bf16 else 1
    return _pcall(
        body, name=name, grid=(M // tm, N // tn, nk), in_specs=in_specs,
        out_specs=[pl.BlockSpec((tm, tn), lambda i, j, k: (i, j))] * n_out,
        out_shape=[jax.ShapeDtypeStruct((M, N), out_dtype), jax.ShapeDtypeStruct((M, N), BF16)][:n_out], args=args,
        scratch=[pltpu.VMEM((tm, tn) if nk > 1 else (8, LANES), F32)],
        sem=("parallel", "parallel", "arbitrary"), jobs=jobs)


def _rope_and_norm(pos_col, inv2, x, w, tm, name, jobs=()):
    T, D = x.shape

    def body(p_ref, f_ref, x_ref, w_ref, c_ref, s_ref, h_ref):
        ang = p_ref[...] * f_ref[...]
        lane = lax.broadcasted_iota(jnp.int32, ang.shape, 1)
        c_ref[...] = jnp.cos(ang)
        s_ref[...] = jnp.where(lane < RET_DK // 2, -1.0, 1.0) * jnp.sin(ang)
        xv = x_ref[...]
        r = lax.rsqrt(jnp.mean(xv * xv, axis=-1, keepdims=True) + RMS_EPS)
        h_ref[...] = (xv * r * w_ref[...]).astype(BF16)

    table = pl.BlockSpec((tm, RET_DK), lambda i: (i, 0))
    tile = pl.BlockSpec((tm, D), lambda i: (i, 0))
    return _pcall(
        body, name=name, grid=(T // tm,),
        in_specs=[pl.BlockSpec((tm, 1), lambda i: (i, 0)), pl.BlockSpec((1, RET_DK), lambda i: (0, 0)),
                  tile, pl.BlockSpec((1, D), lambda i: (0, 0))],
        out_specs=[table, table, tile],
        out_shape=[jax.ShapeDtypeStruct((T, RET_DK), F32)] * 2 + [jax.ShapeDtypeStruct((T, D), BF16)],
        args=[pos_col, inv2, x, w], sem=("parallel",), jobs=jobs)


def _rmsnorm_bwd_add(dres, dh, x, w, tm, name, want_bf16, jobs=()):
    T, D = x.shape

    def body(dres_ref, dh_ref, x_ref, w_ref, *outs):
        if want_bf16:
            dx_ref, dxb_ref, dw_ref = outs
        else:
            dx_ref, dw_ref = outs
        i = pl.program_id(0)
        xv = x_ref[...]
        r = lax.rsqrt(jnp.mean(xv * xv, axis=-1, keepdims=True) + RMS_EPS)
        xh = xv * r
        dh_v = dh_ref[...].astype(F32)
        dxh = dh_v * w_ref[...]
        dx = dres_ref[...] + r * (dxh - xh * jnp.mean(dxh * xh, axis=-1, keepdims=True))
        dx_ref[...] = dx
        if want_bf16:
            dxb_ref[...] = dx.astype(BF16)
        part = jnp.sum(dh_v * xh, axis=0, keepdims=True)

        @pl.when(i == 0)
        def _():
            dw_ref[...] = part

        @pl.when(i > 0)
        def _():
            dw_ref[...] += part

    tile = pl.BlockSpec((tm, D), lambda i: (i, 0))
    row = pl.BlockSpec((1, D), lambda i: (0, 0))
    out_specs = [tile] + ([tile] if want_bf16 else []) + [row]
    out_shape = ([jax.ShapeDtypeStruct((T, D), F32)] + ([jax.ShapeDtypeStruct((T, D), BF16)] if want_bf16 else [])
                 + [jax.ShapeDtypeStruct((1, D), F32)])
    return _pcall(body, name=name, grid=(T // tm,), in_specs=[tile, tile, tile, row], out_specs=out_specs,
                  out_shape=out_shape, args=[dres, dh, x, w], sem=("arbitrary",), jobs=jobs)


def _loss_head(x2, target, wf, tm, name):
    T, D = x2.shape

    def body(x_ref, t_ref, w_ref, dx_ref, dxb_ref, loss_ref, dw_ref):
        i = pl.program_id(0)
        xv = x_ref[...]
        r = lax.rsqrt(jnp.mean(xv * xv, axis=-1, keepdims=True) + RMS_EPS)
        xh = xv * r
        wv = w_ref[...]
        e = xh * wv - t_ref[...]
        lpart = 0.5 * jnp.sum(jnp.sum(e * e, axis=-1, keepdims=True), axis=0, keepdims=True) * (1.0 / D)
        dy = e * (1.0 / D)
        dxh = dy * wv
        dx = r * (dxh - xh * jnp.mean(dxh * xh, axis=-1, keepdims=True))
        dx_ref[...] = dx
        dxb_ref[...] = dx.astype(BF16)
        wpart = jnp.sum(dy * xh, axis=0, keepdims=True)
        lfull = jnp.broadcast_to(lpart, (8, LANES))

        @pl.when(i == 0)
        def _():
            loss_ref[...] = lfull
            dw_ref[...] = wpart

        @pl.when(i > 0)
        def _():
            loss_ref[...] += lfull
            dw_ref[...] += wpart

    tile = pl.BlockSpec((tm, D), lambda i: (i, 0))
    row = pl.BlockSpec((1, D), lambda i: (0, 0))
    return _pcall(
        body, name=name, grid=(T // tm,), in_specs=[tile, tile, row],
        out_specs=[tile, tile, pl.BlockSpec((8, LANES), lambda i: (0, 0)), row],
        out_shape=[jax.ShapeDtypeStruct((T, D), F32), jax.ShapeDtypeStruct((T, D), BF16),
                   jax.ShapeDtypeStruct((8, LANES), F32), jax.ShapeDtypeStruct((1, D), F32)],
        args=[x2, target, wf], sem=("arbitrary",))


def _mix_fwd(a_in, b_in, proj, x, w_ro, w_lo, w_out, mb, w2, tm, name):
    T, D = x.shape

    def body(a_ref, b_ref, gr_ref, gl_ref, x_ref, wro_ref, wlo_ref, wout_ref, mb_ref, w2_ref,
             x1_ref, mix_ref, h2_ref, ya_ref, yb_ref):
        ya = _dot(a_ref[...], wro_ref[...], NN)
        yb = _dot(b_ref[...], wlo_ref[...], NN)
        ya_ref[...] = ya.astype(BF16)
        yb_ref[...] = yb.astype(BF16)
        sa = _sigmoid(gr_ref[...].astype(F32) + mb_ref[0:1, :])
        sb = _sigmoid(gl_ref[...].astype(F32) + mb_ref[1:2, :])
        mix = (sa * ya + sb * yb).astype(BF16)
        mix_ref[...] = mix
        x1 = x_ref[...] + _dot(mix, wout_ref[...], NN)
        x1_ref[...] = x1
        r = lax.rsqrt(jnp.mean(x1 * x1, axis=-1, keepdims=True) + RMS_EPS)
        h2_ref[...] = (x1 * r * w2_ref[...]).astype(BF16)

    tile = pl.BlockSpec((tm, D), lambda i: (i, 0))
    wspec = pl.BlockSpec((D, D), lambda i: (0, 0))
    return _pcall(
        body, name=name, grid=(T // tm,),
        in_specs=[tile, tile,
                  pl.BlockSpec((tm, D), lambda i: (i, COL_GR)), pl.BlockSpec((tm, D), lambda i: (i, COL_GL)),
                  tile, wspec, wspec, wspec,
                  pl.BlockSpec((2, D), lambda i: (0, 0)), pl.BlockSpec((1, D), lambda i: (0, 0))],
        out_specs=[tile] * 5,
        out_shape=[jax.ShapeDtypeStruct((T, D), F32)] + [jax.ShapeDtypeStruct((T, D), BF16)] * 4,
        args=[a_in, b_in, proj, proj, x, w_ro, w_lo, w_out, mb, w2], sem=("parallel",))


def _write_pieces(dst_ref, sems, stashes, row0, col0s, ids, grid, compute):
    def aligned(v, m):
        return v if isinstance(v, int) else pl.multiple_of(v, m)

    def copies(slot):
        return [pltpu.make_async_copy(
                    st.at[slot],
                    dst_ref.at[pl.ds(aligned(row0, 16), st.shape[1]), pl.ds(aligned(c0, LANES), st.shape[2])],
                    sems.at[slot, k])
                for k, (st, c0) in enumerate(zip(stashes, col0s))]

    step = ids[0]
    for i, g in zip(ids[1:], grid[1:]):
        step = step * g + i
    slot = step % 2
    last = _all_true([i == g - 1 for i, g in zip(ids, grid)])
    compute(slot)

    @pl.when(step > 0)
    def _():
        for cp in copies(1 - slot):
            cp.wait()

    for cp in copies(slot):
        cp.start()

    @pl.when(last)
    def _():
        for cp in copies(slot):
            cp.wait()


def _mix_bwd(dx1b, ya, yb, proj, w_ro, w_lo, w_out, mb, tm, name, jobs=()):
    T, D = ya.shape
    grid = (T // tm,)

    def body(dx_ref, ya_ref, yb_ref, gr_ref, gl_ref, wro_ref, wlo_ref, wout_ref, mb_ref,
             da_ref, db_ref, dya_ref, dyb_ref, dp_ref, dmb_ref, dgr_s, dgl_s, wsem):
        i = pl.program_id(0)

        def compute(slot):
            dmix = _dot(dx_ref[...], wout_ref[...], NT)
            ya = ya_ref[...].astype(F32)
            yb = yb_ref[...].astype(F32)
            sa = _sigmoid(gr_ref[...].astype(F32) + mb_ref[0:1, :])
            sb = _sigmoid(gl_ref[...].astype(F32) + mb_ref[1:2, :])
            dya = (dmix * sa).astype(BF16)
            dyb = (dmix * sb).astype(BF16)
            dgr = dmix * ya * sa * (1.0 - sa)
            dgl = dmix * yb * sb * (1.0 - sb)
            dya_ref[...] = dya
            dyb_ref[...] = dyb
            dgr_s[slot] = dgr.astype(BF16)
            dgl_s[slot] = dgl.astype(BF16)
            da_ref[...] = _dot(dya, wro_ref[...], NT).astype(BF16)
            db_ref[...] = _dot(dyb, wlo_ref[...], NT).astype(BF16)

            @pl.when(i == 0)
            def _():
                dmb_ref[...] = jnp.zeros_like(dmb_ref)

            dmb_ref[0:1, :] += jnp.sum(dgr, axis=0, keepdims=True)
            dmb_ref[1:2, :] += jnp.sum(dgl, axis=0, keepdims=True)

        _write_pieces(dp_ref, wsem, [dgr_s, dgl_s], i * tm, [COL_GR * D, COL_GL * D], [i], grid, compute)

    tile = pl.BlockSpec((tm, D), lambda i: (i, 0))
    wspec = pl.BlockSpec((D, D), lambda i: (0, 0))
    two = pl.BlockSpec((2, D), lambda i: (0, 0))
    return _pcall(
        body, name=name, grid=grid,
        in_specs=[tile, tile, tile,
                  pl.BlockSpec((tm, D), lambda i: (i, COL_GR)), pl.BlockSpec((tm, D), lambda i: (i, COL_GL)),
                  wspec, wspec, wspec, two],
        out_specs=[tile] * 4 + [HBM_SPEC, two],
        out_shape=[jax.ShapeDtypeStruct((T, D), BF16)] * 4
                  + [jax.ShapeDtypeStruct((T, D_IN), BF16), jax.ShapeDtypeStruct((2, D), F32)],
        args=[dx1b, ya, yb, proj, proj, w_ro, w_lo, w_out, mb],
        scratch=[pltpu.VMEM((2, tm, D), BF16), pltpu.VMEM((2, tm, D), BF16), pltpu.SemaphoreType.DMA((2, 2))],
        sem=("arbitrary",), jobs=jobs)


def _ret_decay_consts(lg):
    L = RET_BLOCK
    n = lax.broadcasted_iota(jnp.int32, (L, L), 0)
    m = lax.broadcasted_iota(jnp.int32, (L, L), 1)
    cn, cm = n // CHUNK, m // CHUNK
    expo = jnp.where(cn == cm, jnp.abs(n - m), n - m).astype(F32)
    wm = jnp.where(cm <= cn, jnp.exp(lg * expo), 0.0)
    idx = lax.broadcasted_iota(jnp.int32, (L, 1), 0).astype(F32)
    qd = jnp.exp(lg * (idx + 1.0))
    kd = jnp.exp(lg * (L - 1.0 - idx))
    bd = jnp.exp(lg * float(L))
    return wm, qd, kd, bd


def _rotate(v, cos2, sin2s):
    return v * cos2 + pltpu.roll(v, RET_DK // 2, 1) * sin2s


def _rotate_t(d, cos2, sin2s):
    return d * cos2 - pltpu.roll(d, RET_DK // 2, 1) * sin2s


def _retention_fwd(proj, cos2, sin2s, lgam, gn_w, B, S, name, jobs=()):
    T = B * S
    nb = S // RET_BLOCK
    scale = RET_DK ** -0.5

    def body(q_ref, k_ref, v_ref, g_ref, c_ref, s_ref, lg_ref, gw_ref, o_ref, a_ref, qr, kr, st):
        wm, qd, kd, bd = _ret_decay_consts(lg_ref[0:1, 0:1])
        cos2, sin2s = c_ref[...], s_ref[...]
        qr[...] = _rotate(q_ref[...].astype(F32), cos2, sin2s)
        kr[...] = _rotate(k_ref[...].astype(F32), cos2, sin2s) * scale
        st[...] = jnp.zeros_like(st)
        gw = gw_ref[...]
        for j in range(nb):
            rows = pl.ds(j * RET_BLOCK, RET_BLOCK)
            qb = qr[rows, :]
            kb = kr[rows, :]
            vb = v_ref[rows, :].astype(BF16)
            sc = _dot(qb.astype(BF16), kb.astype(BF16), NT) * wm
            o = _dot(sc.astype(BF16), vb, NN) + _dot((qb * qd).astype(BF16), st[...].astype(BF16), NN)
            st[...] = st[...] * bd + _dot((kb * kd).astype(BF16), vb, TN)
            o_ref[rows, :] = o
            mu = jnp.mean(o, axis=-1, keepdims=True)
            oc = o - mu
            var = jnp.mean(oc * oc, axis=-1, keepdims=True)
            y = oc * lax.rsqrt(var + GN_EPS) * gw
            g = g_ref[rows, :].astype(F32)
            a_ref[rows, :] = (y * (g * _sigmoid(g))).astype(BF16)

    blk = lambda w, off: pl.BlockSpec((S, w), lambda b, h: (b, off + h))
    return _pcall(
        body, name=name, grid=(B, RET_HEADS),
        in_specs=[blk(RET_DK, COL_Q), blk(RET_DK, COL_K), blk(RET_DV, COL_V), blk(RET_DV, COL_G),
                  pl.BlockSpec((S, RET_DK), lambda b, h: (b, 0)), pl.BlockSpec((S, RET_DK), lambda b, h: (b, 0)),
                  pl.BlockSpec((None, 8, LANES), lambda b, h: (h, 0, 0)),
                  pl.BlockSpec((1, RET_DV), lambda b, h: (0, h))],
        out_specs=[blk(RET_DV, 0), blk(RET_DV, 0)],
        out_shape=[jax.ShapeDtypeStruct((T, RET_HEADS * RET_DV), F32),
                   jax.ShapeDtypeStruct((T, RET_HEADS * RET_DV), BF16)],
        args=[proj, proj, proj, proj, cos2, sin2s, lgam, gn_w],
        scratch=[pltpu.VMEM((S, RET_DK), F32), pltpu.VMEM((S, RET_DK), F32), pltpu.VMEM((RET_DK, RET_DV), F32)],
        sem=("parallel", "parallel"), jobs=jobs)


def _retention_bwd(da_in, o, proj, dproj, cos2, sin2s, lgam, gn_w, B, S, name, jobs=()):
    T = B * S
    nb = S // RET_BLOCK
    scale = RET_DK ** -0.5
    grid = (RET_HEADS, B)

    def body(da_ref, o_ref, q_ref, k_ref, v_ref, g_ref, c_ref, s_ref, lg_ref, gw_ref, _, dp_ref, dgw_ref,
             qr, kr, do_s, sts, rst, dq_s, dk_s, dv_s, dg_s, wsem):
        h, b = pl.program_id(0), pl.program_id(1)

        def compute(slot):
            wm, qd, kd, bd = _ret_decay_consts(lg_ref[0:1, 0:1])
            cos2, sin2s = c_ref[...], s_ref[...]
            qr[...] = _rotate(q_ref[...].astype(F32), cos2, sin2s)
            kr[...] = _rotate(k_ref[...].astype(F32), cos2, sin2s) * scale
            gw = gw_ref[...]
            st = jnp.zeros((RET_DK, RET_DV), F32)
            dgw = jnp.zeros((1, RET_DV), F32)
            for j in range(nb):
                rows = pl.ds(j * RET_BLOCK, RET_BLOCK)
                ov = o_ref[rows, :]
                mu = jnp.mean(ov, axis=-1, keepdims=True)
                oc = ov - mu
                rstd = lax.rsqrt(jnp.mean(oc * oc, axis=-1, keepdims=True) + GN_EPS)
                y = oc * rstd
                g = g_ref[rows, :].astype(F32)
                sg = _sigmoid(g)
                da = da_ref[rows, :].astype(F32)
                dg_s[slot, rows, :] = (da * (y * gw) * (sg * (1.0 + g * (1.0 - sg)))).astype(BF16)
                dyw = da * (g * sg)
                dgw = dgw + jnp.sum(dyw * y, axis=0, keepdims=True)
                dy = dyw * gw
                do_s[rows, :] = rstd * (dy - jnp.mean(dy, axis=-1, keepdims=True)
                                        - y * jnp.mean(dy * y, axis=-1, keepdims=True))
                sts[j] = st
                st = st * bd + _dot((kr[rows, :] * kd).astype(BF16), v_ref[rows, :].astype(BF16), TN)

            @pl.when(b == 0)
            def _():
                dgw_ref[...] = dgw

            @pl.when(b > 0)
            def _():
                dgw_ref[...] += dgw

            rst[...] = jnp.zeros_like(rst)
            for j in reversed(range(nb)):
                rows = pl.ds(j * RET_BLOCK, RET_BLOCK)
                qb = qr[rows, :]
                kb = kr[rows, :]
                qbb, kbb = qb.astype(BF16), kb.astype(BF16)
                vb = v_ref[rows, :].astype(BF16)
                dob = do_s[rows, :]
                dobb = dob.astype(BF16)
                a_m = (_dot(qbb, kbb, NT) * wm).astype(BF16)
                b_m = (_dot(dobb, vb, NT) * wm).astype(BF16)
                rb = rst[...].astype(BF16)
                dq = _dot(b_m, kbb, NN) + _dot((dob * qd).astype(BF16), sts[j].astype(BF16), NT)
                dk = _dot(b_m, qbb, TN) + kd * _dot(vb, rb, NT)
                dv = _dot(a_m, dobb, TN) + kd * _dot(kbb, rb, NN)
                rst[...] = rst[...] * bd + _dot((qb * qd).astype(BF16), dobb, TN)
                cb, sb = c_ref[rows, :], s_ref[rows, :]
                dq_s[slot, rows, :] = _rotate_t(dq, cb, sb).astype(BF16)
                dk_s[slot, rows, :] = _rotate_t(dk * scale, cb, sb).astype(BF16)
                dv_s[slot, rows, :] = dv.astype(BF16)

        cols = [(COL_Q + h) * RET_DK, (COL_K + h) * RET_DK, (COL_V + h) * RET_DV, (COL_G + h) * RET_DV]
        _write_pieces(dp_ref, wsem, [dq_s, dk_s, dv_s, dg_s], b * S, cols, [h, b], grid, compute)

    blk = lambda w, off: pl.BlockSpec((S, w), lambda h, b: (b, off + h))
    return _pcall(
        body, name=name, grid=grid,
        in_specs=[blk(RET_DV, 0), blk(RET_DV, 0),
                  blk(RET_DK, COL_Q), blk(RET_DK, COL_K), blk(RET_DV, COL_V), blk(RET_DV, COL_G),
                  pl.BlockSpec((S, RET_DK), lambda h, b: (b, 0)), pl.BlockSpec((S, RET_DK), lambda h, b: (b, 0)),
                  pl.BlockSpec((None, 8, LANES), lambda h, b: (h, 0, 0)),
                  pl.BlockSpec((1, RET_DV), lambda h, b: (0, h)), HBM_SPEC],
        out_specs=[HBM_SPEC, pl.BlockSpec((1, RET_DV), lambda h, b: (0, h))],
        out_shape=[jax.ShapeDtypeStruct(dproj.shape, dproj.dtype),
                   jax.ShapeDtypeStruct((1, RET_HEADS * RET_DV), F32)],
        args=[da_in, o, proj, proj, proj, proj, cos2, sin2s, lgam, gn_w, dproj],
        scratch=[pltpu.VMEM((S, RET_DK), F32), pltpu.VMEM((S, RET_DK), F32),
                 pltpu.VMEM((S, RET_DV), F32), pltpu.VMEM((nb, RET_DK, RET_DV), F32),
                 pltpu.VMEM((RET_DK, RET_DV), F32),
                 pltpu.VMEM((2, S, RET_DK), BF16), pltpu.VMEM((2, S, RET_DK), BF16),
                 pltpu.VMEM((2, S, RET_DV), BF16), pltpu.VMEM((2, S, RET_DV), BF16), pltpu.SemaphoreType.DMA((2, 4))],
        sem=("arbitrary", "arbitrary"), jobs=jobs, alias_in_out={10: 0})


def _lru_gates(x, cw, cb, wr, wi, br, bi, lam):
    xc = cb + cw[LRU_CONV - 1:LRU_CONV, :] * x
    for j in range(LRU_CONV - 1):
        xc = xc + cw[j:j + 1, :] * _shift_down(x, LRU_CONV - 1 - j, 0.0)
    xcb = xc.astype(BF16)
    r = 1.0 / (1.0 + jnp.exp(-(_dot(xcb, wr, NN) + br)))
    ig = _sigmoid(_dot(xcb, wi, NN) + bi)
    z = -lam
    sp = jnp.maximum(z, 0.0) + jnp.log1p(jnp.exp(-jnp.abs(z)))
    log_a = (-LRU_C) * r * sp
    a = jnp.exp(log_a)
    om = -jnp.tanh(log_a) * (a * a + 1.0)
    sq = jnp.sqrt(om)
    return xc, xcb, r, ig, sp, a, sq


def _lru_fwd(proj, cw, cb, wr, wi, br, bi, lam, B, S, name):
    T = B * S
    W = LRU_BLOCKS * LRU_BLOCK

    def body(x_ref, y_ref, cw_ref, cb_ref, wr_ref, wi_ref, br_ref, bi_ref, lam_ref, h_ref, bin_ref, a_s, b_s):
        xc, _, _, ig, _, a, sq = _lru_gates(x_ref[...].astype(F32), cw_ref[...], cb_ref[...], wr_ref[...], wi_ref[...],
                                           br_ref[...], bi_ref[...], lam_ref[...])
        a_s[...] = a
        b_s[...] = sq * ig * xc
        _scan_forward_ref(a_s, b_s, h_ref)
        bin_ref[...] = (h_ref[...] * _gelu(y_ref[...].astype(F32))).astype(BF16)

    blk = lambda off: pl.BlockSpec((S, LRU_BLOCK), lambda b, n: (b, off + n))
    vec = lambda rows: pl.BlockSpec((rows, LRU_BLOCK), lambda b, n: (0, n))
    wspec = pl.BlockSpec((None, LRU_BLOCK, LRU_BLOCK), lambda b, n: (n, 0, 0))
    return _pcall(
        body, name=name, grid=(B, LRU_BLOCKS),
        in_specs=[blk(COL_XL), blk(COL_YL), vec(LRU_CONV), vec(1), wspec, wspec, vec(1), vec(1), vec(1)],
        out_specs=[blk(0), blk(0)],
        out_shape=[jax.ShapeDtypeStruct((T, W), F32), jax.ShapeDtypeStruct((T, W), BF16)],
        args=[proj, proj, cw, cb, wr, wi, br, bi, lam],
        scratch=[pltpu.VMEM((S, LRU_BLOCK), F32), pltpu.VMEM((S, LRU_BLOCK), F32)], sem=("parallel", "parallel"))


def _lru_bwd(db_in, h, proj, dproj, cw, cb, wr, wi, br, bi, lam, B, S, name, jobs=()):
    T = B * S
    W = LRU_BLOCKS * LRU_BLOCK

    grid = (LRU_BLOCKS, B)

    def body(dbin_ref, h_ref, x_ref, y_ref, cw_ref, cb_ref, wr_ref, wi_ref, br_ref, bi_ref, lam_ref, _,
             dp_ref, dcw_ref, dcb_ref, dwr_ref, dwi_ref, dbr_ref, dbi_ref, dlam_ref, dx_s, dy_s, wsem,
             an_s, u_s, dh_s):
        n, b = pl.program_id(0), pl.program_id(1)

        def compute(slot):
            x = x_ref[...].astype(F32)
            cw = cw_ref[...]
            wr, wi = wr_ref[...], wi_ref[...]
            lam = lam_ref[...]
            xc, xcb, r, ig, sp, a, sq = _lru_gates(x, cw, cb_ref[...], wr, wi, br_ref[...], bi_ref[...], lam)
            hv = h_ref[...]
            gel, dgel = _gelu_and_grad(y_ref[...].astype(F32))
            dbin = dbin_ref[...].astype(F32)
            dy_s[slot] = (dbin * hv * dgel).astype(BF16)
            an_s[...] = _shift_up(a, 1, 0.0)
            u_s[...] = dbin * gel
            _scan_backward_ref(an_s, u_s, dh_s)
            dh = dh_s[...]
            hprev = _shift_down(hv, 1, 0.0)
            dhs = dh * sq
            d_ig = dhs * xc
            d_xc = dhs * ig
            d_loga = (dh * a) * (hprev - (ig * xc) * (a / sq))
            d_r = d_loga * ((-LRU_C) * sp)
            d_sp = jnp.sum(d_loga * ((-LRU_C) * r), axis=0, keepdims=True)
            dlam = -d_sp * _sigmoid(-lam)
            d_pr = d_r * r * (1.0 - r)
            d_pi = d_ig * ig * (1.0 - ig)
            d_prb, d_pib = d_pr.astype(BF16), d_pi.astype(BF16)
            d_xc = d_xc + _dot(d_prb, wr, NT) + _dot(d_pib, wi, NT)

            @pl.when(b == 0)
            def _():
                for ref in (dcw_ref, dcb_ref, dwr_ref, dwi_ref, dbr_ref, dbi_ref, dlam_ref):
                    ref[...] = jnp.zeros_like(ref)

            dx = cw[LRU_CONV - 1:LRU_CONV, :] * d_xc
            for j in range(LRU_CONV - 1):
                sft = LRU_CONV - 1 - j
                dx = dx + cw[j:j + 1, :] * _shift_up(d_xc, sft, 0.0)
                dcw_ref[j:j + 1, :] += jnp.sum(d_xc * _shift_down(x, sft, 0.0), axis=0, keepdims=True)
            dcw_ref[LRU_CONV - 1:LRU_CONV, :] += jnp.sum(d_xc * x, axis=0, keepdims=True)
            dx_s[slot] = dx.astype(BF16)
            dcb_ref[...] += jnp.sum(d_xc, axis=0, keepdims=True)
            dwr_ref[...] += _dot(xcb, d_prb, TN)
            dwi_ref[...] += _dot(xcb, d_pib, TN)
            dbr_ref[...] += jnp.sum(d_pr, axis=0, keepdims=True)
            dbi_ref[...] += jnp.sum(d_pi, axis=0, keepdims=True)
            dlam_ref[...] += dlam

        cols = [(COL_XL + n) * LRU_BLOCK, (COL_YL + n) * LRU_BLOCK]
        _write_pieces(dp_ref, wsem, [dx_s, dy_s], b * S, cols, [n, b], grid, compute)

    blk = lambda off: pl.BlockSpec((S, LRU_BLOCK), lambda n, b: (b, off + n))
    vec = lambda rows: pl.BlockSpec((rows, LRU_BLOCK), lambda n, b: (0, n))
    wspec = pl.BlockSpec((None, LRU_BLOCK, LRU_BLOCK), lambda n, b: (n, 0, 0))
    vshape = lambda rows: jax.ShapeDtypeStruct((rows, W), F32)
    wshape = jax.ShapeDtypeStruct((LRU_BLOCKS, LRU_BLOCK, LRU_BLOCK), F32)
    return _pcall(
        body, name=name, grid=grid,
        in_specs=[blk(0), blk(0), blk(COL_XL), blk(COL_YL), vec(LRU_CONV), vec(1), wspec, wspec, vec(1), vec(1),
                  vec(1), HBM_SPEC],
        out_specs=[HBM_SPEC, vec(LRU_CONV), vec(1), wspec, wspec, vec(1), vec(1), vec(1)],
        out_shape=[jax.ShapeDtypeStruct(dproj.shape, dproj.dtype),
                   vshape(LRU_CONV), vshape(1), wshape, wshape, vshape(1), vshape(1), vshape(1)],
        args=[db_in, h, proj, proj, cw, cb, wr, wi, br, bi, lam, dproj],
        scratch=[pltpu.VMEM((2, S, LRU_BLOCK), BF16), pltpu.VMEM((2, S, LRU_BLOCK), BF16), pltpu.SemaphoreType.DMA((2, 2)),
                 pltpu.VMEM((S, LRU_BLOCK), F32), pltpu.VMEM((S, LRU_BLOCK), F32), pltpu.VMEM((S, LRU_BLOCK), F32)],
        sem=("arbitrary", "arbitrary"), jobs=jobs, alias_in_out={11: 0})


FFN_CT = 256


def _ffn_conv(gate, cw, cb):
    gc = cb + cw[FFN_CONV - 1:FFN_CONV, :] * gate
    for j in range(FFN_CONV - 1):
        gc = gc + cw[j:j + 1, :] * _shift_down(gate, FFN_CONV - 1 - j, 0.0)
    return gc


def _ffn_act_fwd(up, cw, cb, B, S, name):
    T = B * S
    nct = D_FF // FFN_CT

    def body(g_ref, v_ref, cw_ref, cb_ref, f_ref):
        gc = _ffn_conv(g_ref[...].astype(F32), cw_ref[...], cb_ref[...])
        f_ref[...] = (_gelu(gc) * v_ref[...].astype(F32)).astype(BF16)

    return _pcall(
        body, name=name, grid=(B, nct),
        in_specs=[pl.BlockSpec((S, FFN_CT), lambda b, c: (b, c)), pl.BlockSpec((S, FFN_CT), lambda b, c: (b, nct + c)),
                  pl.BlockSpec((FFN_CONV, FFN_CT), lambda b, c: (0, c)), pl.BlockSpec((1, FFN_CT), lambda b, c: (0, c))],
        out_specs=[pl.BlockSpec((S, FFN_CT), lambda b, c: (b, c))],
        out_shape=[jax.ShapeDtypeStruct((T, D_FF), BF16)], args=[up, up, cw, cb], sem=("parallel", "parallel"))[0]


def _ffn_act_bwd(df, up, cw, cb, B, S, name, jobs=()):
    T = B * S
    nct = D_FF // FFN_CT

    grid = (nct, B)

    def body(df_ref, g_ref, v_ref, cw_ref, cb_ref, du_ref, dcw_ref, dcb_ref, dg_s, dv_s, wsem):
        c, b = pl.program_id(0), pl.program_id(1)

        def compute(slot):
            gate = g_ref[...].astype(F32)
            cw = cw_ref[...]
            gc = _ffn_conv(gate, cw, cb_ref[...])
            gel, dgel = _gelu_and_grad(gc)
            dfv = df_ref[...].astype(F32)
            dv_s[slot] = (dfv * gel).astype(BF16)
            dgc = dfv * v_ref[...].astype(F32) * dgel

            @pl.when(b == 0)
            def _():
                dcw_ref[...] = jnp.zeros_like(dcw_ref)
                dcb_ref[...] = jnp.zeros_like(dcb_ref)

            dgate = cw[FFN_CONV - 1:FFN_CONV, :] * dgc
            for j in range(FFN_CONV - 1):
                sft = FFN_CONV - 1 - j
                dgate = dgate + cw[j:j + 1, :] * _shift_up(dgc, sft, 0.0)
                dcw_ref[j:j + 1, :] += jnp.sum(dgc * _shift_down(gate, sft, 0.0), axis=0, keepdims=True)
            dcw_ref[FFN_CONV - 1:FFN_CONV, :] += jnp.sum(dgc * gate, axis=0, keepdims=True)
            dg_s[slot] = dgate.astype(BF16)
            dcb_ref[...] += jnp.sum(dgc, axis=0, keepdims=True)

        _write_pieces(du_ref, wsem, [dg_s, dv_s], b * S, [c * FFN_CT, (nct + c) * FFN_CT], [c, b], grid, compute)

    blk = pl.BlockSpec((S, FFN_CT), lambda c, b: (b, c))
    return _pcall(
        body, name=name, grid=grid,
        in_specs=[blk, blk, pl.BlockSpec((S, FFN_CT), lambda c, b: (b, nct + c)),
                  pl.BlockSpec((FFN_CONV, FFN_CT), lambda c, b: (0, c)),
                  pl.BlockSpec((1, FFN_CT), lambda c, b: (0, c))],
        out_specs=[HBM_SPEC, pl.BlockSpec((FFN_CONV, FFN_CT), lambda c, b: (0, c)),
                   pl.BlockSpec((1, FFN_CT), lambda c, b: (0, c))],
        out_shape=[jax.ShapeDtypeStruct((T, 2 * D_FF), BF16),
                   jax.ShapeDtypeStruct((FFN_CONV, D_FF), F32), jax.ShapeDtypeStruct((1, D_FF), F32)],
        args=[df, up, up, cw, cb],
        scratch=[pltpu.VMEM((2, S, FFN_CT), BF16), pltpu.VMEM((2, S, FFN_CT), BF16), pltpu.SemaphoreType.DMA((2, 2))],
        sem=("arbitrary", "arbitrary"), jobs=jobs)


def _rs_add(g, recv, mode, core, name, also_bf16=False):
    shard = tuple(recv.shape[1:])
    if mode == "mid":
        a, e, c2 = shard
        g_in = g.reshape(a, N_DEV, e, c2)
        grid = (4, 1)
        g_spec = pl.BlockSpec((a, None, e, c2), lambda k, i, c_ref: (0, 2 * k + c_ref[0], 0, 0))
        r_spec = pl.BlockSpec((None, a, e, c2), lambda k, i, c_ref: (k, 0, 0, 0))
    else:
        R, C = shard
        tr = _row_tile(R, 512)
        grid = (4, R // tr)
        if mode == "rows":
            g_in = g.reshape(N_DEV, R, C)
            g_spec = pl.BlockSpec((None, tr, C), lambda k, i, c_ref: (2 * k + c_ref[0], i, 0))
        else:
            g_in = g
            g_spec = pl.BlockSpec((tr, C), lambda k, i, c_ref: (i, 2 * k + c_ref[0]))
        r_spec = pl.BlockSpec((None, tr, C), lambda k, i, c_ref: (k, i, 0))

    def body(c_ref, g_ref, r_ref, o_ref, *ob_ref):
        s = g_ref[...] + r_ref[...].astype(F32)
        o_ref[...] = s
        if also_bf16:
            ob_ref[0][...] = s.astype(BF16)

    out_shape = jax.ShapeDtypeStruct(recv.shape, F32)
    return pl.pallas_call(
        body, name=name,
        grid_spec=pltpu.PrefetchScalarGridSpec(
            num_scalar_prefetch=1, grid=grid, in_specs=[g_spec, r_spec],
            out_specs=[r_spec, r_spec] if also_bf16 else r_spec),
        out_shape=[out_shape, jax.ShapeDtypeStruct(recv.shape, BF16)] if also_bf16 else out_shape,
        compiler_params=pltpu.CompilerParams(dimension_semantics=("parallel", "parallel"),
                                             vmem_limit_bytes=VMEM_LIMIT),
    )(core, g_in, recv)


def _adam_update(gv, w, m, v):
    nm = ADAM_B1 * m + (1.0 - ADAM_B1) * gv
    nv = ADAM_B2 * v + (1.0 - ADAM_B2) * (gv * gv)
    m_hat = nm / (1.0 - ADAM_B1 ** ADAM_STEP)
    v_hat = nv / (1.0 - ADAM_B2 ** ADAM_STEP)
    delta = -ADAM_LR * (m_hat / (jnp.sqrt(v_hat) + ADAM_EPS) + ADAM_WD * w)
    return delta, nm, nv


def _adamw_shard(partial, recv, w, m, v, chip, name):
    shape = tuple(w.shape)
    tr = _row_tile(shape[0], 256)
    rest = shape[1:]
    zeros = (0,) * len(rest)
    tile = pl.BlockSpec((tr,) + rest, lambda i, s: (i,) + zeros)

    def body(_, p_ref, r_ref, w_ref, m_ref, v_ref, g_ref, d_ref, nm_ref, nv_ref):
        gv = p_ref[...] + r_ref[0].astype(F32) + r_ref[1].astype(F32) + r_ref[2].astype(F32)
        g_ref[...] = gv
        d_ref[...], nm_ref[...], nv_ref[...] = _adam_update(gv, w_ref[...], m_ref[...], v_ref[...])

    grid_spec = pltpu.PrefetchScalarGridSpec(
        num_scalar_prefetch=1, grid=(shape[0] // tr,),
        in_specs=[pl.BlockSpec((None, tr) + rest, lambda i, s: (s[0], i) + zeros),
                  pl.BlockSpec((3, tr) + rest, lambda i, s: (0, i) + zeros), tile, tile, tile],
        out_specs=[tile] * 4)
    return pl.pallas_call(
        body, name=name, grid_spec=grid_spec, out_shape=[jax.ShapeDtypeStruct(shape, F32)] * 4,
        compiler_params=pltpu.CompilerParams(dimension_semantics=("parallel",), vmem_limit_bytes=VMEM_LIMIT),
    )(chip, partial, recv, w, m, v)


SMALL_LANES = 1024


def _small_rows(shape):
    r, w = shape
    return r * max(1, w // SMALL_LANES)


def _small_allreduce(parts, name):
    n = len(parts)
    shapes = [tuple(p.shape) for p in parts]
    offs, total = [], 0
    for s in shapes:
        offs.append(total)
        total += _small_rows(s)
    rows = -(-total // 8) * 8

    def body(*refs):
        p_refs, o_refs = refs[:n], refs[n:2 * n]
        buf, tot, send_sems, recv_sems = refs[2 * n:]
        x, y, c = _mesh_pos()
        me, sibling = (x, y, c), (x, y, 1 - c)
        chips = _other_chips(x, y)

        def slot(px, py, pc):
            return buf.at[4 * px + 2 * py + pc]

        def copy(k, block, to):
            return _remote(slot(*block), slot(*block), send_sems.at[k], recv_sems.at[k], to)

        tot[...] = jnp.zeros_like(tot)
        for p_ref, (r, w), off in zip(p_refs, shapes, offs):
            wl = min(w, SMALL_LANES)
            for part in range(max(1, w // SMALL_LANES)):
                tot[pl.ds(off + part * r, r), pl.ds(0, wl)] = p_ref[:, pl.ds(part * SMALL_LANES, wl)]
        buf[4 * x + 2 * y + c] = tot[...]
        first = [copy(0, me, sibling)] + [copy(1 + j, me, (*chip, c)) for j, chip in enumerate(chips)]
        for cp in first:
            cp.start()
        passed = [copy(4 + j, (*chip, c), sibling) for j, chip in enumerate(chips)]
        for j, chip in enumerate(chips):
            copy(1 + j, (*chip, c), me).wait_recv()
            passed[j].start()
        copy(0, sibling, me).wait_recv()
        for j, chip in enumerate(chips):
            copy(4 + j, (*chip, 1 - c), me).wait_recv()
        for cp in first + passed:
            cp.wait_send()
        acc = buf[0]
        for d in range(1, N_DEV):
            acc = acc + buf[d]
        tot[...] = acc
        for o_ref, (r, w), off in zip(o_refs, shapes, offs):
            wl = min(w, SMALL_LANES)
            for part in range(max(1, w // SMALL_LANES)):
                o_ref[:, pl.ds(part * SMALL_LANES, wl)] = tot[pl.ds(off + part * r, r), pl.ds(0, wl)]

    vm = pl.BlockSpec(memory_space=pltpu.VMEM)
    return pl.pallas_call(
        body, name=name,
        in_specs=[vm] * n, out_specs=[vm] * n,
        out_shape=[jax.ShapeDtypeStruct(s, F32) for s in shapes],
        scratch_shapes=[pltpu.VMEM((N_DEV, rows, SMALL_LANES), F32), pltpu.VMEM((rows, SMALL_LANES), F32),
                        pltpu.SemaphoreType.DMA((7,)), pltpu.SemaphoreType.DMA((7,))],
    )(*parts)


def _adamw_small(gs, ws, ms, vs, name):
    n = len(gs)

    def body(*refs):
        g_r, w_r, m_r, v_r = refs[:n], refs[n:2 * n], refs[2 * n:3 * n], refs[3 * n:4 * n]
        d_r, nm_r, nv_r = refs[4 * n:5 * n], refs[5 * n:6 * n], refs[6 * n:7 * n]
        for i in range(n):
            d_r[i][...], nm_r[i][...], nv_r[i][...] = _adam_update(g_r[i][...], w_r[i][...], m_r[i][...], v_r[i][...])

    vm = pl.BlockSpec(memory_space=pltpu.VMEM)
    shapes = [jax.ShapeDtypeStruct(w.shape, F32) for w in ws]
    outs = pl.pallas_call(body, name=name, in_specs=[vm] * (4 * n), out_specs=[vm] * (3 * n),
                          out_shape=shapes * 3)(*gs, *ws, *ms, *vs)
    return outs[:n], outs[n:2 * n], outs[2 * n:]


FIRST = [("w_in", (1024, 896), "cols"), ("lru_w_r", (4, 32, 256), "mid"), ("lru_w_i", (4, 32, 256), "mid")]
LATE = [("w_ret_o", (128, 1024), "rows"), ("w_lru_o", (128, 1024), "rows"), ("w_out", (128, 1024), "rows"),
        ("ffn_w_up", (1024, 768), "cols"), ("ffn_w_down", (384, 1024), "rows")]
BIG = FIRST + LATE
SMALL_SHARDED = [("merge_gate_b", (2, 128), "cols"), ("lru_conv_w", (4, 128), "cols"), ("lru_b_r", (4, 32), "stack"),
                 ("lru_b_i", (4, 32), "stack"), ("ffn_conv_w", (3, 384), "cols")]
REPLICATED = [("norm1_w", (1, 1024)), ("ret_gn_w", (1, 1024)), ("lru_conv_b", (1, 1024)), ("lru_lambda", (1, 1024)),
              ("norm2_w", (1, 1024)), ("ffn_conv_b", (1, 3072)), ("norm_f_w", (1, 1024))]
MODE = {n: m for n, _, m in BIG}
SHARD = {n: s for n, s, _ in BIG}


def _local_step(x3, positions, target3, first_shards, ws, late_shards, core):
    B, S, D = x3.shape
    T = B * S
    x = x3.reshape(T, D)
    target = target3.reshape(T, D)
    tm = min(512, T)
    big = min(1024, T)
    big2 = min(2048, T)
    big4 = min(4096, T)

    half = RET_DK // 2
    inv_freq = ROPE_BASE ** (-jnp.arange(half, dtype=F32) / half)
    inv2 = jnp.concatenate([inv_freq, inv_freq]).reshape(1, RET_DK)
    log_gamma = jnp.log1p(-jnp.power(2.0, -5.0 - jnp.arange(RET_HEADS, dtype=F32)))
    lgam = jnp.broadcast_to(log_gamma[:, None, None], (RET_HEADS, 8, LANES))
    pos_col = positions.astype(F32).reshape(T, 1)
    late_names = [n for n, _, _ in LATE]
    late_modes = [m for _, _, m in LATE]
    late_shapes = [s for _, s, _ in LATE]
    first = FIRST + SMALL_SHARDED
    first_modes = [m for _, _, m in first]

    cos2, sin2s, h1, *first_part = _rope_and_norm(pos_col, inv2, x, ws["norm1_w"], tm, "rope_norm1_fwd",
                                                  jobs=[_ag_first_job(first_shards, first_modes)])
    first_full = _pcall(lambda: None, name="gather_first_pass_on", grid=(1,), in_specs=[], out_specs=[], out_shape=[],
                        args=[], sem=("arbitrary",),
                        jobs=[_ag_second_job(first_part, first_modes, [s for _, s, _ in first])])
    gathered = dict(zip([n for n, _, _ in first], first_full))
    wb = {n: gathered[n] for n, _, _ in FIRST}
    ws = dict(ws, **{n: gathered[n] for n, _, _ in SMALL_SHARDED})
    for n in ("lru_b_r", "lru_b_i"):
        ws[n] = jnp.transpose(ws[n], (1, 0, 2)).reshape(1, LRU_BLOCKS * LRU_BLOCK)
    proj, *late_part = _matmul(h1, wb["w_in"], "nn", BF16, big2, 1024, 1024, "proj_fwd",
                               jobs=[_ag_first_job(late_shards, late_modes)])
    o, a_in, *late_full = _retention_fwd(proj, cos2, sin2s, lgam, ws["ret_gn_w"], B, S, "retention_fwd",
                                         jobs=[_ag_second_job(late_part, late_modes, late_shapes)])
    wb = dict(wb, **dict(zip(late_names, late_full)))
    hl, b_in = _lru_fwd(proj, ws["lru_conv_w"], ws["lru_conv_b"], wb["lru_w_r"], wb["lru_w_i"],
                        ws["lru_b_r"], ws["lru_b_i"], ws["lru_lambda"], B, S, "lru_fwd")
    x1, mix, h2, ya, yb = _mix_fwd(a_in, b_in, proj, x, wb["w_ret_o"], wb["w_lru_o"], wb["w_out"],
                                   ws["merge_gate_b"], ws["norm2_w"], tm, "mix_fwd")
    up = _matmul(h2, wb["ffn_w_up"], "nn", BF16, big2, 1024, 1024, "ffn_up_fwd")[0]
    f = _ffn_act_fwd(up, ws["ffn_conv_w"], ws["ffn_conv_b"], B, S, "ffn_act_fwd")
    x2 = _matmul(f, wb["ffn_w_down"], "nn", F32, big, 1024, D_FF, "ffn_down_fwd", add=x1)[0]
    dx2, dx2b, loss_acc, d_norm_f = _loss_head(x2, target, ws["norm_f_w"], tm, "loss_head")

    g, rs = {}, {}

    def stage1(names, grads):
        return _rs_sibling_job(grads, [MODE[n] for n in names], [SHARD[n] for n in names])

    def add(names, grads, recvs):
        return [_rs_add(gr, r, MODE[n], core, "rs_add_" + n) for n, gr, r in zip(names, grads, recvs)]

    g["norm_f_w"] = d_norm_f
    g_down = _matmul(f, dx2b, "tn", F32, 1024, 1024, big4, "ffn_down_bwd_w")[0]
    df, s1_down = _matmul(dx2b, wb["ffn_w_down"], "nt", BF16, big2, 1024, 1024, "ffn_down_bwd_x",
                          jobs=[stage1(["ffn_w_down"], [g_down])])
    p_down = add(["ffn_w_down"], [g_down], [s1_down])
    dup, g["ffn_conv_w"], g["ffn_conv_b"], s2_down = _ffn_act_bwd(
        df, up, ws["ffn_conv_w"], ws["ffn_conv_b"], B, S, "ffn_act_bwd", jobs=[_rs_chip_job(p_down)])
    rs["ffn_w_down"] = (p_down[0], s2_down)

    g_up = _matmul(h2, dup, "tn", F32, 1024, 1024, big4, "ffn_up_bwd_w")[0]
    dh2, s1_up = _matmul(dup, wb["ffn_w_up"], "nt", BF16, big, 1024, D_FF, "ffn_up_bwd_x",
                         jobs=[stage1(["ffn_w_up"], [g_up])])
    p_up = add(["ffn_w_up"], [g_up], [s1_up])
    dx1, dx1b, g["norm2_w"] = _rmsnorm_bwd_add(dx2, dh2, x1, ws["norm2_w"], tm, "norm2_bwd", True)
    da_in, db_in, dya, dyb, dproj, g["merge_gate_b"] = _mix_bwd(
        dx1b, ya, yb, proj, wb["w_ret_o"], wb["w_lru_o"], wb["w_out"], ws["merge_gate_b"], tm, "mix_bwd")

    mid_names = ["w_out", "w_ret_o", "w_lru_o"]
    g_mid = [_matmul(mix, dx1b, "tn", F32, 1024, 1024, big4, "w_out_bwd_w")[0],
             _matmul(a_in, dya, "tn", F32, 1024, 1024, big4, "w_ret_o_bwd_w")[0],
             _matmul(b_in, dyb, "tn", F32, 1024, 1024, big4, "w_lru_o_bwd_w")[0]]
    (dproj, g["lru_conv_w"], g["lru_conv_b"], g_wr, g_wi, g["lru_b_r"], g["lru_b_i"], g["lru_lambda"], s2_up,
     *s1_mid) = _lru_bwd(db_in, hl, proj, dproj, ws["lru_conv_w"], ws["lru_conv_b"], wb["lru_w_r"], wb["lru_w_i"],
                         ws["lru_b_r"], ws["lru_b_i"], ws["lru_lambda"], B, S, "lru_bwd",
                         jobs=[_rs_chip_job(p_up), stage1(mid_names, g_mid)])
    rs["ffn_w_up"] = (p_up[0], s2_up)
    p_mid = add(mid_names, g_mid, s1_mid)
    lru_names = ["lru_w_r", "lru_w_i"]
    dproj, g["ret_gn_w"], *rest = _retention_bwd(
        da_in, o, proj, dproj, cos2, sin2s, lgam, ws["ret_gn_w"], B, S, "retention_bwd",
        jobs=[_rs_chip_job(p_mid), stage1(lru_names, [g_wr, g_wi])])
    s2_mid, s1_lru = rest[:3], rest[3:]
    for n, p, r in zip(mid_names, p_mid, s2_mid):
        rs[n] = (p, r)
    p_lru = add(lru_names, [g_wr, g_wi], s1_lru)

    g_in, g_in_bf16, *s2_lru = _matmul(h1, dproj, "tn", F32, 1024, 1024, big4, "proj_bwd_w",
                                       jobs=[_rs_chip_job(p_lru)], also_bf16=True)
    for n, p, r in zip(lru_names, p_lru, s2_lru):
        rs[n] = (p, r)
    s1_in = _pcall(lambda: None, name="rs_sibling_w_in", grid=(1,), in_specs=[], out_specs=[], out_shape=[], args=[],
                   sem=("arbitrary",), jobs=[stage1(["w_in"], [g_in_bf16])])
    p_in, p_in_bf16 = _rs_add(g_in, s1_in[0], MODE["w_in"], core, "rs_add_w_in", also_bf16=True)
    dh1, s2_in = _matmul(dproj, wb["w_in"], "nt", BF16, big, 1024, D_IN // 2, "proj_bwd_x",
                         jobs=[_rs_chip_job([p_in_bf16])])
    grad_x, g["norm1_w"] = _rmsnorm_bwd_add(dx1, dh1, x, ws["norm1_w"], tm, "norm1_bwd", False)
    rs["w_in"] = (p_in, s2_in)
    return loss_acc, grad_x.reshape(B, S, D), g, rs


def kernel(x, positions, norm1_w, w_in, merge_gate_b, ret_gn_w, w_ret_o, lru_conv_w, lru_conv_b, lru_w_r, lru_b_r, lru_w_i, lru_b_i, lru_lambda, w_lru_o, w_out, norm2_w, ffn_w_up, ffn_conv_w, ffn_conv_b, ffn_w_down, norm_f_w, loss_target, m_norm1_w, m_w_in, m_merge_gate_b, m_ret_gn_w, m_w_ret_o, m_lru_conv_w, m_lru_conv_b, m_lru_w_r, m_lru_b_r, m_lru_w_i, m_lru_b_i, m_lru_lambda, m_w_lru_o, m_w_out, m_norm2_w, m_ffn_w_up, m_ffn_conv_w, m_ffn_conv_b, m_ffn_w_down, m_norm_f_w, v_norm1_w, v_w_in, v_merge_gate_b, v_ret_gn_w, v_w_ret_o, v_lru_conv_w, v_lru_conv_b, v_lru_w_r, v_lru_b_r, v_lru_w_i, v_lru_b_i, v_lru_lambda, v_w_lru_o, v_w_out, v_norm2_w, v_ffn_w_up, v_ffn_conv_w, v_ffn_conv_b, v_ffn_w_down, v_norm_f_w):
    names = ["norm1_w", "w_in", "merge_gate_b", "ret_gn_w", "w_ret_o", "lru_conv_w", "lru_conv_b", "lru_w_r", "lru_b_r",
             "lru_w_i", "lru_b_i", "lru_lambda", "w_lru_o", "w_out", "norm2_w", "ffn_w_up", "ffn_conv_w", "ffn_conv_b",
             "ffn_w_down", "norm_f_w"]
    w_args = [norm1_w, w_in, merge_gate_b, ret_gn_w, w_ret_o, lru_conv_w, lru_conv_b, lru_w_r, lru_b_r, lru_w_i, lru_b_i,
              lru_lambda, w_lru_o, w_out, norm2_w, ffn_w_up, ffn_conv_w, ffn_conv_b, ffn_w_down, norm_f_w]
    m_args = [m_norm1_w, m_w_in, m_merge_gate_b, m_ret_gn_w, m_w_ret_o, m_lru_conv_w, m_lru_conv_b, m_lru_w_r, m_lru_b_r,
              m_lru_w_i, m_lru_b_i, m_lru_lambda, m_w_lru_o, m_w_out, m_norm2_w, m_ffn_w_up, m_ffn_conv_w, m_ffn_conv_b,
              m_ffn_w_down, m_norm_f_w]
    v_args = [v_norm1_w, v_w_in, v_merge_gate_b, v_ret_gn_w, v_w_ret_o, v_lru_conv_w, v_lru_conv_b, v_lru_w_r, v_lru_b_r,
              v_lru_w_i, v_lru_b_i, v_lru_lambda, v_w_lru_o, v_w_out, v_norm2_w, v_ffn_w_up, v_ffn_conv_w, v_ffn_conv_b,
              v_ffn_w_down, v_norm_f_w]
    orig_shape = {n: a.shape for n, a in zip(names, w_args)}
    local_shape = {n: s for n, s, _ in BIG + SMALL_SHARDED}
    local_shape.update({n: s for n, s in REPLICATED})
    W = {n: a.reshape(local_shape[n]) for n, a in zip(names, w_args)}
    M = {n: a.reshape(local_shape[n]) for n, a in zip(names, m_args)}
    V = {n: a.reshape(local_shape[n]) for n, a in zip(names, v_args)}

    xi, yi, ci = _mesh_pos()
    dev = 4 * xi + 2 * yi + ci
    chip = (2 * xi + yi).astype(jnp.int32).reshape(1)
    core = ci.astype(jnp.int32).reshape(1)

    small_names = [n for n, _, _ in SMALL_SHARDED]
    first_shards = [W[n].astype(BF16) for n, _, _ in FIRST] + [W[n] for n in small_names]
    late_shards = [W[n].astype(BF16) for n, _, _ in LATE]
    rep = {n: W[n] for n, _ in REPLICATED}
    loss_acc, grad_x, g, rs = _local_step(x, positions, loss_target, first_shards, rep, late_shards, core)

    G_out, D_out, M_out, V_out = {}, {}, {}, {}
    for n, _, _ in BIG:
        G_out[n], D_out[n], M_out[n], V_out[n] = _adamw_shard(rs[n][0], rs[n][1], W[n], M[n], V[n], chip, "adamw_" + n)

    rep_names = [n for n, _ in REPLICATED]
    red_names = rep_names + small_names
    red = _small_allreduce([g[n] for n in red_names] + [loss_acc[0:1, :]], "allreduce_small_grads")
    loss = red[-1][0, 0]
    gs = dict(zip(red_names, red[:-1]))
    for n, s, mode in SMALL_SHARDED:
        if mode == "cols":
            gs[n] = lax.dynamic_slice_in_dim(gs[n], dev * s[1], s[1], axis=1)
        else:
            full = gs[n].reshape(LRU_BLOCKS, LRU_BLOCK)
            gs[n] = lax.dynamic_slice_in_dim(full, dev * s[1], s[1], axis=1)
    d2, m2, v2 = _adamw_small([gs[n] for n in red_names], [W[n] for n in red_names], [M[n] for n in red_names],
                              [V[n] for n in red_names], "adamw_small")
    for i, n in enumerate(red_names):
        G_out[n], D_out[n], M_out[n], V_out[n] = gs[n], d2[i], m2[i], v2[i]

    outs = [loss, grad_x]
    for group in (G_out, D_out, M_out, V_out):
        outs += [group[n].reshape(orig_shape[n]) for n in names]
    return tuple(outs)
```

```python
import math

import jax
import jax.numpy as jnp
from jax import lax
from jax.experimental import pallas as pl
from jax.experimental.pallas import tpu as pltpu

F32 = jnp.float32
BF16 = jnp.bfloat16
MESH = pl.DeviceIdType.MESH

D_MODEL = 1024
CHUNK = 64
RET_HEADS = 4
RET_DK = 128
RET_DV = 256
LRU_BLOCKS = 4
LRU_BLOCK = 256
LRU_CONV = 4
LRU_C = 8.0
D_FF = 3072
FFN_CONV = 3
ROPE_BASE = 10000.0
RMS_EPS = 1e-6
GN_EPS = 1e-6
D_IN = 7168
ADAM_LR, ADAM_B1, ADAM_B2, ADAM_EPS, ADAM_WD, ADAM_STEP = 0.001, 0.9, 0.999, 1e-08, 0.01, 10

N_DEV = 8
V7X_VMEM_BYTES = 64 * 1024 * 1024
VMEM_LIMIT = V7X_VMEM_BYTES - 8 * 1024 * 1024
RET_BLOCK = 256
LANES = 128

COL_Q, COL_K = 0, 4
COL_V, COL_G, COL_XL, COL_YL = 4, 8, 12, 16
COL_GR, COL_GL = 5, 6

HBM_SPEC = pl.BlockSpec(memory_space=pl.ANY)


def _gelu(x):
    c = math.sqrt(2.0 / math.pi)
    t = jnp.tanh(x * (c + (c * 0.044715) * (x * x)))
    return x * (0.5 * t + 0.5)


def _gelu_and_grad(x):
    c = math.sqrt(2.0 / math.pi)
    x2 = x * x
    t = jnp.tanh(x * (c + (c * 0.044715) * x2))
    h = 0.5 * t + 0.5
    g = x * h
    dg = h + g * (1.0 - h) * ((2.0 * c) + (6.0 * c * 0.044715) * x2)
    return g, dg


def _sigmoid(x):
    return 0.5 * jnp.tanh(0.5 * x) + 0.5


SUBLANES = 8


def _shift_down(x, s, fill):
    r = pltpu.roll(x, s, 0)
    rows = lax.broadcasted_iota(jnp.int32, (SUBLANES,) + x.shape[1:], 0)
    top = jnp.where(rows >= s, r[:SUBLANES], fill)
    return jnp.concatenate([top, r[SUBLANES:]], axis=0)


def _shift_up(x, s, fill):
    n = x.shape[0]
    r = pltpu.roll(x, n - s, 0)
    rows = lax.broadcasted_iota(jnp.int32, (SUBLANES,) + x.shape[1:], 0)
    bottom = jnp.where(rows < SUBLANES - s, r[n - SUBLANES:], fill)
    return jnp.concatenate([r[:n - SUBLANES], bottom], axis=0)


SCAN_CHUNK = 64


def _scan_forward(a, b):
    n = a.shape[0]
    s = 1
    while s < n:
        if s % SUBLANES:
            b = a * _shift_down(b, s, 0.0) + b
            a = a * _shift_down(a, s, 1.0)
        else:
            b = jnp.concatenate([b[:s], a[s:] * b[:n - s] + b[s:]], axis=0)
            a = jnp.concatenate([a[:s], a[s:] * a[:n - s]], axis=0)
        s *= 2
    return a, b


def _scan_backward(a_next, u):
    n = u.shape[0]
    s = 1
    while s < n:
        if s % SUBLANES:
            u = u + a_next * _shift_up(u, s, 0.0)
            a_next = a_next * _shift_up(a_next, s, 1.0)
        else:
            u = jnp.concatenate([u[:n - s] + a_next[:n - s] * u[s:], u[n - s:]], axis=0)
            a_next = jnp.concatenate([a_next[:n - s] * a_next[s:], a_next[n - s:]], axis=0)
        s *= 2
    return a_next, u


def _scan_forward_ref(a_ref, b_ref, h_ref):
    S, W = a_ref.shape
    for strip in range(W // LANES):
        cols = pl.ds(strip * LANES, LANES)

        def body(k, carry, cols=cols):
            rows = pl.ds(pl.multiple_of(k * SCAN_CHUNK, SCAN_CHUNK), SCAN_CHUNK)
            a_cum, h_loc = _scan_forward(a_ref[rows, cols], b_ref[rows, cols])
            h = h_loc + a_cum * carry
            h_ref[rows, cols] = h
            return h[SCAN_CHUNK - 1:, :]

        lax.fori_loop(0, S // SCAN_CHUNK, body, jnp.zeros((1, LANES), F32))


def _scan_backward_ref(an_ref, u_ref, d_ref):
    S, W = an_ref.shape
    n_chunks = S // SCAN_CHUNK
    for strip in range(W // LANES):
        cols = pl.ds(strip * LANES, LANES)

        def body(i, carry, cols=cols):
            rows = pl.ds(pl.multiple_of((n_chunks - 1 - i) * SCAN_CHUNK, SCAN_CHUNK), SCAN_CHUNK)
            an_cum, d_loc = _scan_backward(an_ref[rows, cols], u_ref[rows, cols])
            d = d_loc + an_cum * carry
            d_ref[rows, cols] = d
            return d[:1, :]

        lax.fori_loop(0, n_chunks, body, jnp.zeros((1, LANES), F32))


def _dot(a, b, dims):
    return lax.dot_general(a, b, (dims, ((), ())), preferred_element_type=F32)


NN = ((1,), (0,))
NT = ((1,), (1,))
TN = ((0,), (0,))


def _mesh_pos():
    return lax.axis_index("x"), lax.axis_index("y"), lax.axis_index("c")


def _other_chips(x, y):
    return [(1 - x, y), (x, 1 - y), (1 - x, 1 - y)]


def _full_shape(shard_shape, mode):
    if mode == "rows":
        return (N_DEV * shard_shape[0],) + tuple(shard_shape[1:])
    if mode == "cols":
        return (shard_shape[0], N_DEV * shard_shape[1])
    if mode == "mid":
        return (shard_shape[0], N_DEV * shard_shape[1], shard_shape[2])
    return (N_DEV,) + tuple(shard_shape)


def _extent(shard_shape, mode):
    return {"rows": shard_shape[0], "cols": shard_shape[1], "mid": shard_shape[1], "stack": 1}[mode]


def _window(ref, mode, extent, d):
    if mode == "stack":
        return ref.at[d]
    start = pl.multiple_of(d * extent, extent)
    if mode == "rows":
        return ref.at[pl.ds(start, extent)]
    if mode == "cols":
        return ref.at[:, pl.ds(start, extent)]
    return ref.at[:, pl.ds(start, extent), :]


class _Job:
    def __init__(self, inputs, out_shapes, sems, start, finish, aliases=None):
        self.inputs, self.out_shapes, self.sems = list(inputs), list(out_shapes), sems
        self.start, self.finish, self.aliases = start, finish, dict(aliases or {})


def _remote(src, dst, send_sem, recv_sem, to):
    return pltpu.make_async_remote_copy(src_ref=src, dst_ref=dst, send_sem=send_sem, recv_sem=recv_sem,
                                        device_id=to, device_id_type=MESH)


def _ag_first_job(shards, modes):
    n = len(shards)
    extents = [_extent(s.shape, m) for s, m in zip(shards, modes)]

    def copies(x_refs, out_refs, send, recv, local, arriving):
        x, y, c = _mesh_pos()
        peers = [(x, y, 1 - c)] + [(*chip, c) for chip in _other_chips(x, y)]
        win = lambda i, p: _window(out_refs[i], modes[i], extents[i], 4 * p[0] + 2 * p[1] + p[2])
        if arriving:
            return [_remote(x_refs[i], win(i, p), send.at[i, k], recv.at[i, k], p)
                    for i in range(n) for k, p in enumerate(peers)]
        mine = [pltpu.make_async_copy(x_refs[i], win(i, (x, y, c)), local.at[i]) for i in range(n)]
        sends = [_remote(x_refs[i], win(i, (x, y, c)), send.at[i, k], recv.at[i, k], p)
                 for i in range(n) for k, p in enumerate(peers)]
        return mine, sends

    def start(*refs):
        mine, sends = copies(*refs, False)
        for cp in mine + sends:
            cp.start()

    def finish(*refs):
        for cp in copies(*refs, True):
            cp.wait_recv()
        mine, sends = copies(*refs, False)
        for cp in sends:
            cp.wait_send()
        for cp in mine:
            cp.wait()

    out_shapes = [jax.ShapeDtypeStruct(_full_shape(s.shape, m), s.dtype) for s, m in zip(shards, modes)]
    return _Job(shards, out_shapes, ((n, 4), (n, 4), (n,)), start, finish)


def _ag_second_job(fulls, modes, shard_shapes):
    n = len(fulls)
    extents = [_extent(s, m) for s, m in zip(shard_shapes, modes)]

    def copies(_, out_refs, send, recv, local, core_of_block):
        x, y, c = _mesh_pos()
        pc = c if core_of_block == "mine" else 1 - c
        win = lambda i, chip: _window(out_refs[i], modes[i], extents[i], 4 * chip[0] + 2 * chip[1] + pc)
        return [_remote(win(i, chip), win(i, chip), send.at[i, j], recv.at[i, j], (x, y, 1 - c))
                for i in range(n) for j, chip in enumerate(_other_chips(x, y))]

    def start(*refs):
        for cp in copies(*refs, "mine"):
            cp.start()

    def finish(*refs):
        for cp in copies(*refs, "sibling"):
            cp.wait_recv()
        for cp in copies(*refs, "mine"):
            cp.wait_send()

    out_shapes = [jax.ShapeDtypeStruct(f.shape, f.dtype) for f in fulls]
    return _Job(fulls, out_shapes, ((n, 3), (n, 3), (1,)), start, finish, aliases={i: i for i in range(n)})


def _rs_sibling_job(grads, modes, shard_shapes):
    n = len(grads)
    extents = [_extent(s, m) for s, m in zip(shard_shapes, modes)]

    def copies(g_refs, out_refs, send, recv, local):
        x, y, c = _mesh_pos()
        return [_remote(_window(g_refs[i], modes[i], extents[i], 2 * k + (1 - c)), out_refs[i].at[k],
                        send.at[i, k], recv.at[i, k], (x, y, 1 - c))
                for i in range(n) for k in range(4)]

    def start(*refs):
        for cp in copies(*refs):
            cp.start()

    def finish(*refs):
        cps = copies(*refs)
        for cp in cps:
            cp.wait_recv()
        for cp in cps:
            cp.wait_send()

    out_shapes = [jax.ShapeDtypeStruct((4,) + tuple(s), g.dtype) for s, g in zip(shard_shapes, grads)]
    return _Job(grads, out_shapes, ((n, 4), (n, 4), (1,)), start, finish)


def _rs_chip_job(partials):
    n = len(partials)

    def copies(p_refs, out_refs, send, recv, local):
        x, y, c = _mesh_pos()
        return [_remote(p_refs[i].at[2 * px + py], out_refs[i].at[j], send.at[i, j], recv.at[i, j], (px, py, c))
                for i in range(n) for j, (px, py) in enumerate(_other_chips(x, y))]

    def start(*refs):
        for cp in copies(*refs):
            cp.start()

    def finish(*refs):
        cps = copies(*refs)
        for cp in cps:
            cp.wait_recv()
        for cp in cps:
            cp.wait_send()

    out_shapes = [jax.ShapeDtypeStruct((3,) + tuple(p.shape[1:]), p.dtype) for p in partials]
    return _Job(partials, out_shapes, ((n, 3), (n, 3), (1,)), start, finish)


def _all_true(conds):
    out = conds[0]
    for c in conds[1:]:
        out = jnp.logical_and(out, c)
    return out


def _pcall(body, *, name, grid, in_specs, out_specs, out_shape, args, sem, scratch=(), jobs=(), alias_in_out=None):
    n_in, n_out, n_scr = len(args), len(out_shape), len(scratch)
    job_in = [a for j in jobs for a in j.inputs]
    job_out = [s for j in jobs for s in j.out_shapes]
    job_sems = [pltpu.SemaphoreType.DMA(shape) for j in jobs for shape in j.sems]
    aliases, in_off, out_off = dict(alias_in_out or {}), n_in, n_out
    for j in jobs:
        for a, b in j.aliases.items():
            aliases[in_off + a] = out_off + b
        in_off += len(j.inputs)
        out_off += len(j.out_shapes)

    def wrapped(*refs):
        ins = refs[:n_in]
        jins = refs[n_in:n_in + len(job_in)]
        o0 = n_in + len(job_in)
        outs = refs[o0:o0 + n_out]
        jouts = refs[o0 + n_out:o0 + n_out + len(job_out)]
        s0 = o0 + n_out + len(job_out)
        scr = refs[s0:s0 + n_scr]
        jsems = refs[s0 + n_scr:]
        if jobs:
            ids = [pl.program_id(a) for a in range(len(grid))]
            first = _all_true([i == 0 for i in ids])
            last = _all_true([i == g - 1 for i, g in zip(ids, grid)])

            def per_job(which):
                i0 = o0_ = 0
                for k, j in enumerate(jobs):
                    fn = j.start if which == "start" else j.finish
                    fn(jins[i0:i0 + len(j.inputs)], jouts[o0_:o0_ + len(j.out_shapes)], *jsems[3 * k:3 * k + 3])
                    i0 += len(j.inputs)
                    o0_ += len(j.out_shapes)

            @pl.when(first)
            def _():
                per_job("start")

        body(*ins, *outs, *scr)
        if jobs:
            @pl.when(last)
            def _():
                per_job("finish")

    semantics = tuple("arbitrary" for _ in grid) if jobs else sem
    return pl.pallas_call(
        wrapped, name=name, grid=grid,
        in_specs=list(in_specs) + [HBM_SPEC] * len(job_in),
        out_specs=list(out_specs) + [HBM_SPEC] * len(job_out),
        out_shape=list(out_shape) + job_out,
        scratch_shapes=list(scratch) + job_sems,
        input_output_aliases=aliases,
        compiler_params=pltpu.CompilerParams(dimension_semantics=semantics, vmem_limit_bytes=VMEM_LIMIT),
    )(*args, *job_in)


def _row_tile(rows, cap):
    if rows <= cap:
        return rows
    best = None
    for t in range(16, cap + 1, 16):
        if rows % t == 0:
            best = t
    assert best is not None
    return best


def _matmul(a, b, mode, out_dtype, tm, tn, tk, name, add=None, jobs=()):
    if mode == "tn":
        K, M = a.shape
    else:
        M, K = a.shape
    N = b.shape[0] if mode == "nt" else b.shape[1]
    tm, tn, tk = min(tm, M), min(tn, N), min(tk, K)
    assert M % tm == 0 and N % tn == 0 and K % tk == 0
    nk = K // tk
    dims = {"nn": NN, "nt": NT, "tn": TN}[mode]

    def body(*refs):
        if add is None:
            a_ref, b_ref, o_ref, acc = refs
            add_ref = None
        else:
            a_ref, b_ref, add_ref, o_ref, acc = refs
        k = pl.program_id(2)
        p = _dot(a_ref[...], b_ref[...], dims)

        def finish(r):
            if add_ref is not None:
                r = r + add_ref[...].astype(F32)
            o_ref[...] = r.astype(out_dtype)

        if nk == 1:
            finish(p)
        else:
            @pl.when(k == 0)
            def _():
                acc[...] = p

            @pl.when(k > 0)
            def _():
                acc[...] += p

            @pl.when(k == nk - 1)
            def _():
                finish(acc[...])

    if mode == "tn":
        a_spec = pl.BlockSpec((tk, tm), lambda i, j, k: (k, i))
    else:
        a_spec = pl.BlockSpec((tm, tk), lambda i, j, k: (i, k))
    if mode == "nt":
        b_spec = pl.BlockSpec((tn, tk), lambda i, j, k: (j, k))
    else:
        b_spec = pl.BlockSpec((tk, tn), lambda i, j, k: (k, j))
    in_specs = [a_spec, b_spec]
    args = [a, b]
    if add is not None:
        in_specs.append(pl.BlockSpec((tm, tn), lambda i, j, k: (i, j)))
        args.append(add)
    return _pcall(
        body, name=name, grid=(M // tm, N // tn, nk), in_specs=in_specs,
        out_specs=[pl.BlockSpec((tm, tn), lambda i, j, k: (i, j))],
        out_shape=[jax.ShapeDtypeStruct((M, N), out_dtype)], args=args,
        scratch=[pltpu.VMEM((tm, tn) if nk > 1 else (8, LANES), F32)],
        sem=("parallel", "parallel", "arbitrary"), jobs=jobs)


def _rope_and_norm(pos_col, inv2, x, w, tm, name, jobs=()):
    T, D = x.shape

    def body(p_ref, f_ref, x_ref, w_ref, c_ref, s_ref, h_ref):
        ang = p_ref[...] * f_ref[...]
        lane = lax.broadcasted_iota(jnp.int32, ang.shape, 1)
        c_ref[...] = jnp.cos(ang)
        s_ref[...] = jnp.where(lane < RET_DK // 2, -1.0, 1.0) * jnp.sin(ang)
        xv = x_ref[...]
        r = lax.rsqrt(jnp.mean(xv * xv, axis=-1, keepdims=True) + RMS_EPS)
        h_ref[...] = (xv * r * w_ref[...]).astype(BF16)

    table = pl.BlockSpec((tm, RET_DK), lambda i: (i, 0))
    tile = pl.BlockSpec((tm, D), lambda i: (i, 0))
    return _pcall(
        body, name=name, grid=(T // tm,),
        in_specs=[pl.BlockSpec((tm, 1), lambda i: (i, 0)), pl.BlockSpec((1, RET_DK), lambda i: (0, 0)),
                  tile, pl.BlockSpec((1, D), lambda i: (0, 0))],
        out_specs=[table, table, tile],
        out_shape=[jax.ShapeDtypeStruct((T, RET_DK), F32)] * 2 + [jax.ShapeDtypeStruct((T, D), BF16)],
        args=[pos_col, inv2, x, w], sem=("parallel",), jobs=jobs)


def _rmsnorm_bwd_add(dres, dh, x, w, tm, name, want_bf16, jobs=()):
    T, D = x.shape

    def body(dres_ref, dh_ref, x_ref, w_ref, *outs):
        if want_bf16:
            dx_ref, dxb_ref, dw_ref = outs
        else:
            dx_ref, dw_ref = outs
        i = pl.program_id(0)
        xv = x_ref[...]
        r = lax.rsqrt(jnp.mean(xv * xv, axis=-1, keepdims=True) + RMS_EPS)
        xh = xv * r
        dh_v = dh_ref[...].astype(F32)
        dxh = dh_v * w_ref[...]
        dx = dres_ref[...] + r * (dxh - xh * jnp.mean(dxh * xh, axis=-1, keepdims=True))
        dx_ref[...] = dx
        if want_bf16:
            dxb_ref[...] = dx.astype(BF16)
        part = jnp.sum(dh_v * xh, axis=0, keepdims=True)

        @pl.when(i == 0)
        def _():
            dw_ref[...] = part

        @pl.when(i > 0)
        def _():
            dw_ref[...] += part

    tile = pl.BlockSpec((tm, D), lambda i: (i, 0))
    row = pl.BlockSpec((1, D), lambda i: (0, 0))
    out_specs = [tile] + ([tile] if want_bf16 else []) + [row]
    out_shape = ([jax.ShapeDtypeStruct((T, D), F32)] + ([jax.ShapeDtypeStruct((T, D), BF16)] if want_bf16 else [])
                 + [jax.ShapeDtypeStruct((1, D), F32)])
    return _pcall(body, name=name, grid=(T // tm,), in_specs=[tile, tile, tile, row], out_specs=out_specs,
                  out_shape=out_shape, args=[dres, dh, x, w], sem=("arbitrary",), jobs=jobs)


def _loss_head(x2, target, wf, tm, name):
    T, D = x2.shape

    def body(x_ref, t_ref, w_ref, dx_ref, dxb_ref, loss_ref, dw_ref):
        i = pl.program_id(0)
        xv = x_ref[...]
        r = lax.rsqrt(jnp.mean(xv * xv, axis=-1, keepdims=True) + RMS_EPS)
        xh = xv * r
        wv = w_ref[...]
        e = xh * wv - t_ref[...]
        lpart = 0.5 * jnp.sum(jnp.sum(e * e, axis=-1, keepdims=True), axis=0, keepdims=True) * (1.0 / D)
        dy = e * (1.0 / D)
        dxh = dy * wv
        dx = r * (dxh - xh * jnp.mean(dxh * xh, axis=-1, keepdims=True))
        dx_ref[...] = dx
        dxb_ref[...] = dx.astype(BF16)
        wpart = jnp.sum(dy * xh, axis=0, keepdims=True)
        lfull = jnp.broadcast_to(lpart, (8, LANES))

        @pl.when(i == 0)
        def _():
            loss_ref[...] = lfull
            dw_ref[...] = wpart

        @pl.when(i > 0)
        def _():
            loss_ref[...] += lfull
            dw_ref[...] += wpart

    tile = pl.BlockSpec((tm, D), lambda i: (i, 0))
    row = pl.BlockSpec((1, D), lambda i: (0, 0))
    return _pcall(
        body, name=name, grid=(T // tm,), in_specs=[tile, tile, row],
        out_specs=[tile, tile, pl.BlockSpec((8, LANES), lambda i: (0, 0)), row],
        out_shape=[jax.ShapeDtypeStruct((T, D), F32), jax.ShapeDtypeStruct((T, D), BF16),
                   jax.ShapeDtypeStruct((8, LANES), F32), jax.ShapeDtypeStruct((1, D), F32)],
        args=[x2, target, wf], sem=("arbitrary",))


def _mix_fwd(a_in, b_in, proj, x, w_ro, w_lo, w_out, mb, w2, tm, name):
    T, D = x.shape

    def body(a_ref, b_ref, gr_ref, gl_ref, x_ref, wro_ref, wlo_ref, wout_ref, mb_ref, w2_ref,
             x1_ref, mix_ref, h2_ref, ya_ref, yb_ref):
        ya = _dot(a_ref[...], wro_ref[...], NN)
        yb = _dot(b_ref[...], wlo_ref[...], NN)
        ya_ref[...] = ya.astype(BF16)
        yb_ref[...] = yb.astype(BF16)
        sa = _sigmoid(gr_ref[...].astype(F32) + mb_ref[0:1, :])
        sb = _sigmoid(gl_ref[...].astype(F32) + mb_ref[1:2, :])
        mix = (sa * ya + sb * yb).astype(BF16)
        mix_ref[...] = mix
        x1 = x_ref[...] + _dot(mix, wout_ref[...], NN)
        x1_ref[...] = x1
        r = lax.rsqrt(jnp.mean(x1 * x1, axis=-1, keepdims=True) + RMS_EPS)
        h2_ref[...] = (x1 * r * w2_ref[...]).astype(BF16)

    tile = pl.BlockSpec((tm, D), lambda i: (i, 0))
    wspec = pl.BlockSpec((D, D), lambda i: (0, 0))
    return _pcall(
        body, name=name, grid=(T // tm,),
        in_specs=[tile, tile,
                  pl.BlockSpec((tm, D), lambda i: (i, COL_GR)), pl.BlockSpec((tm, D), lambda i: (i, COL_GL)),
                  tile, wspec, wspec, wspec,
                  pl.BlockSpec((2, D), lambda i: (0, 0)), pl.BlockSpec((1, D), lambda i: (0, 0))],
        out_specs=[tile] * 5,
        out_shape=[jax.ShapeDtypeStruct((T, D), F32)] + [jax.ShapeDtypeStruct((T, D), BF16)] * 4,
        args=[a_in, b_in, proj, proj, x, w_ro, w_lo, w_out, mb, w2], sem=("parallel",))


def _write_pieces(dst_ref, sems, stashes, row0, col0s, ids, grid, compute):
    def aligned(v, m):
        return v if isinstance(v, int) else pl.multiple_of(v, m)

    def copies(slot):
        return [pltpu.make_async_copy(
                    st.at[slot],
                    dst_ref.at[pl.ds(aligned(row0, 16), st.shape[1]), pl.ds(aligned(c0, LANES), st.shape[2])],
                    sems.at[slot, k])
                for k, (st, c0) in enumerate(zip(stashes, col0s))]

    step = ids[0]
    for i, g in zip(ids[1:], grid[1:]):
        step = step * g + i
    slot = step % 2
    last = _all_true([i == g - 1 for i, g in zip(ids, grid)])
    compute(slot)

    @pl.when(step > 0)
    def _():
        for cp in copies(1 - slot):
            cp.wait()

    for cp in copies(slot):
        cp.start()

    @pl.when(last)
    def _():
        for cp in copies(slot):
            cp.wait()


def _mix_bwd(dx1b, ya, yb, proj, w_ro, w_lo, w_out, mb, tm, name, jobs=()):
    T, D = ya.shape
    grid = (T // tm,)

    def body(dx_ref, ya_ref, yb_ref, gr_ref, gl_ref, wro_ref, wlo_ref, wout_ref, mb_ref,
             da_ref, db_ref, dya_ref, dyb_ref, dp_ref, dmb_ref, dgr_s, dgl_s, wsem):
        i = pl.program_id(0)

        def compute(slot):
            dmix = _dot(dx_ref[...], wout_ref[...], NT)
            ya = ya_ref[...].astype(F32)
            yb = yb_ref[...].astype(F32)
            sa = _sigmoid(gr_ref[...].astype(F32) + mb_ref[0:1, :])
            sb = _sigmoid(gl_ref[...].astype(F32) + mb_ref[1:2, :])
            dya = (dmix * sa).astype(BF16)
            dyb = (dmix * sb).astype(BF16)
            dgr = dmix * ya * sa * (1.0 - sa)
            dgl = dmix * yb * sb * (1.0 - sb)
            dya_ref[...] = dya
            dyb_ref[...] = dyb
            dgr_s[slot] = dgr.astype(BF16)
            dgl_s[slot] = dgl.astype(BF16)
            da_ref[...] = _dot(dya, wro_ref[...], NT).astype(BF16)
            db_ref[...] = _dot(dyb, wlo_ref[...], NT).astype(BF16)

            @pl.when(i == 0)
            def _():
                dmb_ref[...] = jnp.zeros_like(dmb_ref)

            dmb_ref[0:1, :] += jnp.sum(dgr, axis=0, keepdims=True)
            dmb_ref[1:2, :] += jnp.sum(dgl, axis=0, keepdims=True)

        _write_pieces(dp_ref, wsem, [dgr_s, dgl_s], i * tm, [COL_GR * D, COL_GL * D], [i], grid, compute)

    tile = pl.BlockSpec((tm, D), lambda i: (i, 0))
    wspec = pl.BlockSpec((D, D), lambda i: (0, 0))
    two = pl.BlockSpec((2, D), lambda i: (0, 0))
    return _pcall(
        body, name=name, grid=grid,
        in_specs=[tile, tile, tile,
                  pl.BlockSpec((tm, D), lambda i: (i, COL_GR)), pl.BlockSpec((tm, D), lambda i: (i, COL_GL)),
                  wspec, wspec, wspec, two],
        out_specs=[tile] * 4 + [HBM_SPEC, two],
        out_shape=[jax.ShapeDtypeStruct((T, D), BF16)] * 4
                  + [jax.ShapeDtypeStruct((T, D_IN), BF16), jax.ShapeDtypeStruct((2, D), F32)],
        args=[dx1b, ya, yb, proj, proj, w_ro, w_lo, w_out, mb],
        scratch=[pltpu.VMEM((2, tm, D), BF16), pltpu.VMEM((2, tm, D), BF16), pltpu.SemaphoreType.DMA((2, 2))],
        sem=("arbitrary",), jobs=jobs)


def _ret_decay_consts(lg):
    L = RET_BLOCK
    n = lax.broadcasted_iota(jnp.int32, (L, L), 0)
    m = lax.broadcasted_iota(jnp.int32, (L, L), 1)
    cn, cm = n // CHUNK, m // CHUNK
    expo = jnp.where(cn == cm, jnp.abs(n - m), n - m).astype(F32)
    wm = jnp.where(cm <= cn, jnp.exp(lg * expo), 0.0)
    idx = lax.broadcasted_iota(jnp.int32, (L, 1), 0).astype(F32)
    qd = jnp.exp(lg * (idx + 1.0))
    kd = jnp.exp(lg * (L - 1.0 - idx))
    bd = jnp.exp(lg * float(L))
    return wm, qd, kd, bd


def _rotate(v, cos2, sin2s):
    return v * cos2 + pltpu.roll(v, RET_DK // 2, 1) * sin2s


def _rotate_t(d, cos2, sin2s):
    return d * cos2 - pltpu.roll(d, RET_DK // 2, 1) * sin2s


def _retention_fwd(proj, cos2, sin2s, lgam, gn_w, B, S, name, jobs=()):
    T = B * S
    nb = S // RET_BLOCK
    scale = RET_DK ** -0.5

    def body(q_ref, k_ref, v_ref, g_ref, c_ref, s_ref, lg_ref, gw_ref, o_ref, a_ref, qr, kr, st):
        wm, qd, kd, bd = _ret_decay_consts(lg_ref[0:1, 0:1])
        cos2, sin2s = c_ref[...], s_ref[...]
        qr[...] = _rotate(q_ref[...].astype(F32), cos2, sin2s)
        kr[...] = _rotate(k_ref[...].astype(F32), cos2, sin2s) * scale
        st[...] = jnp.zeros_like(st)
        gw = gw_ref[...]
        for j in range(nb):
            rows = pl.ds(j * RET_BLOCK, RET_BLOCK)
            qb = qr[rows, :]
            kb = kr[rows, :]
            vb = v_ref[rows, :].astype(BF16)
            sc = _dot(qb.astype(BF16), kb.astype(BF16), NT) * wm
            o = _dot(sc.astype(BF16), vb, NN) + _dot((qb * qd).astype(BF16), st[...].astype(BF16), NN)
            st[...] = st[...] * bd + _dot((kb * kd).astype(BF16), vb, TN)
            o_ref[rows, :] = o
            mu = jnp.mean(o, axis=-1, keepdims=True)
            oc = o - mu
            var = jnp.mean(oc * oc, axis=-1, keepdims=True)
            y = oc * lax.rsqrt(var + GN_EPS) * gw
            g = g_ref[rows, :].astype(F32)
            a_ref[rows, :] = (y * (g * _sigmoid(g))).astype(BF16)

    blk = lambda w, off: pl.BlockSpec((S, w), lambda b, h: (b, off + h))
    return _pcall(
        body, name=name, grid=(B, RET_HEADS),
        in_specs=[blk(RET_DK, COL_Q), blk(RET_DK, COL_K), blk(RET_DV, COL_V), blk(RET_DV, COL_G),
                  pl.BlockSpec((S, RET_DK), lambda b, h: (b, 0)), pl.BlockSpec((S, RET_DK), lambda b, h: (b, 0)),
                  pl.BlockSpec((None, 8, LANES), lambda b, h: (h, 0, 0)),
                  pl.BlockSpec((1, RET_DV), lambda b, h: (0, h))],
        out_specs=[blk(RET_DV, 0), blk(RET_DV, 0)],
        out_shape=[jax.ShapeDtypeStruct((T, RET_HEADS * RET_DV), F32),
                   jax.ShapeDtypeStruct((T, RET_HEADS * RET_DV), BF16)],
        args=[proj, proj, proj, proj, cos2, sin2s, lgam, gn_w],
        scratch=[pltpu.VMEM((S, RET_DK), F32), pltpu.VMEM((S, RET_DK), F32), pltpu.VMEM((RET_DK, RET_DV), F32)],
        sem=("parallel", "parallel"), jobs=jobs)


def _retention_bwd(da_in, o, proj, dproj, cos2, sin2s, lgam, gn_w, B, S, name, jobs=()):
    T = B * S
    nb = S // RET_BLOCK
    scale = RET_DK ** -0.5
    grid = (RET_HEADS, B)

    def body(da_ref, o_ref, q_ref, k_ref, v_ref, g_ref, c_ref, s_ref, lg_ref, gw_ref, _, dp_ref, dgw_ref,
             qr, kr, do_s, sts, rst, dq_s, dk_s, dv_s, dg_s, wsem):
        h, b = pl.program_id(0), pl.program_id(1)

        def compute(slot):
            wm, qd, kd, bd = _ret_decay_consts(lg_ref[0:1, 0:1])
            cos2, sin2s = c_ref[...], s_ref[...]
            qr[...] = _rotate(q_ref[...].astype(F32), cos2, sin2s)
            kr[...] = _rotate(k_ref[...].astype(F32), cos2, sin2s) * scale
            gw = gw_ref[...]
            st = jnp.zeros((RET_DK, RET_DV), F32)
            dgw = jnp.zeros((1, RET_DV), F32)
            for j in range(nb):
                rows = pl.ds(j * RET_BLOCK, RET_BLOCK)
                ov = o_ref[rows, :]
                mu = jnp.mean(ov, axis=-1, keepdims=True)
                oc = ov - mu
                rstd = lax.rsqrt(jnp.mean(oc * oc, axis=-1, keepdims=True) + GN_EPS)
                y = oc * rstd
                g = g_ref[rows, :].astype(F32)
                sg = _sigmoid(g)
                da = da_ref[rows, :].astype(F32)
                dg_s[slot, rows, :] = (da * (y * gw) * (sg * (1.0 + g * (1.0 - sg)))).astype(BF16)
                dyw = da * (g * sg)
                dgw = dgw + jnp.sum(dyw * y, axis=0, keepdims=True)
                dy = dyw * gw
                do_s[rows, :] = rstd * (dy - jnp.mean(dy, axis=-1, keepdims=True)
                                        - y * jnp.mean(dy * y, axis=-1, keepdims=True))
                sts[j] = st
                st = st * bd + _dot((kr[rows, :] * kd).astype(BF16), v_ref[rows, :].astype(BF16), TN)

            @pl.when(b == 0)
            def _():
                dgw_ref[...] = dgw

            @pl.when(b > 0)
            def _():
                dgw_ref[...] += dgw

            rst[...] = jnp.zeros_like(rst)
            for j in reversed(range(nb)):
                rows = pl.ds(j * RET_BLOCK, RET_BLOCK)
                qb = qr[rows, :]
                kb = kr[rows, :]
                qbb, kbb = qb.astype(BF16), kb.astype(BF16)
                vb = v_ref[rows, :].astype(BF16)
                dob = do_s[rows, :]
                dobb = dob.astype(BF16)
                a_m = (_dot(qbb, kbb, NT) * wm).astype(BF16)
                b_m = (_dot(dobb, vb, NT) * wm).astype(BF16)
                rb = rst[...].astype(BF16)
                dq = _dot(b_m, kbb, NN) + _dot((dob * qd).astype(BF16), sts[j].astype(BF16), NT)
                dk = _dot(b_m, qbb, TN) + kd * _dot(vb, rb, NT)
                dv = _dot(a_m, dobb, TN) + kd * _dot(kbb, rb, NN)
                rst[...] = rst[...] * bd + _dot((qb * qd).astype(BF16), dobb, TN)
                cb, sb = c_ref[rows, :], s_ref[rows, :]
                dq_s[slot, rows, :] = _rotate_t(dq, cb, sb).astype(BF16)
                dk_s[slot, rows, :] = _rotate_t(dk * scale, cb, sb).astype(BF16)
                dv_s[slot, rows, :] = dv.astype(BF16)

        cols = [(COL_Q + h) * RET_DK, (COL_K + h) * RET_DK, (COL_V + h) * RET_DV, (COL_G + h) * RET_DV]
        _write_pieces(dp_ref, wsem, [dq_s, dk_s, dv_s, dg_s], b * S, cols, [h, b], grid, compute)

    blk = lambda w, off: pl.BlockSpec((S, w), lambda h, b: (b, off + h))
    return _pcall(
        body, name=name, grid=grid,
        in_specs=[blk(RET_DV, 0), blk(RET_DV, 0),
                  blk(RET_DK, COL_Q), blk(RET_DK, COL_K), blk(RET_DV, COL_V), blk(RET_DV, COL_G),
                  pl.BlockSpec((S, RET_DK), lambda h, b: (b, 0)), pl.BlockSpec((S, RET_DK), lambda h, b: (b, 0)),
                  pl.BlockSpec((None, 8, LANES), lambda h, b: (h, 0, 0)),
                  pl.BlockSpec((1, RET_DV), lambda h, b: (0, h)), HBM_SPEC],
        out_specs=[HBM_SPEC, pl.BlockSpec((1, RET_DV), lambda h, b: (0, h))],
        out_shape=[jax.ShapeDtypeStruct(dproj.shape, dproj.dtype),
                   jax.ShapeDtypeStruct((1, RET_HEADS * RET_DV), F32)],
        args=[da_in, o, proj, proj, proj, proj, cos2, sin2s, lgam, gn_w, dproj],
        scratch=[pltpu.VMEM((S, RET_DK), F32), pltpu.VMEM((S, RET_DK), F32),
                 pltpu.VMEM((S, RET_DV), F32), pltpu.VMEM((nb, RET_DK, RET_DV), F32),
                 pltpu.VMEM((RET_DK, RET_DV), F32),
                 pltpu.VMEM((2, S, RET_DK), BF16), pltpu.VMEM((2, S, RET_DK), BF16),
                 pltpu.VMEM((2, S, RET_DV), BF16), pltpu.VMEM((2, S, RET_DV), BF16), pltpu.SemaphoreType.DMA((2, 4))],
        sem=("arbitrary", "arbitrary"), jobs=jobs, alias_in_out={10: 0})


def _lru_gates(x, cw, cb, wr, wi, br, bi, lam):
    xc = cb + cw[LRU_CONV - 1:LRU_CONV, :] * x
    for j in range(LRU_CONV - 1):
        xc = xc + cw[j:j + 1, :] * _shift_down(x, LRU_CONV - 1 - j, 0.0)
    xcb = xc.astype(BF16)
    r = 1.0 / (1.0 + jnp.exp(-(_dot(xcb, wr, NN) + br)))
    ig = _sigmoid(_dot(xcb, wi, NN) + bi)
    z = -lam
    sp = jnp.maximum(z, 0.0) + jnp.log1p(jnp.exp(-jnp.abs(z)))
    log_a = (-LRU_C) * r * sp
    a = jnp.exp(log_a)
    om = -jnp.tanh(log_a) * (a * a + 1.0)
    sq = jnp.sqrt(om)
    return xc, xcb, r, ig, sp, a, sq


def _lru_fwd(proj, cw, cb, wr, wi, br, bi, lam, B, S, name):
    T = B * S
    W = LRU_BLOCKS * LRU_BLOCK

    def body(x_ref, y_ref, cw_ref, cb_ref, wr_ref, wi_ref, br_ref, bi_ref, lam_ref, h_ref, bin_ref, a_s, b_s):
        xc, _, _, ig, _, a, sq = _lru_gates(x_ref[...].astype(F32), cw_ref[...], cb_ref[...], wr_ref[...], wi_ref[...],
                                           br_ref[...], bi_ref[...], lam_ref[...])
        a_s[...] = a
        b_s[...] = sq * ig * xc
        _scan_forward_ref(a_s, b_s, h_ref)
        bin_ref[...] = (h_ref[...] * _gelu(y_ref[...].astype(F32))).astype(BF16)

    blk = lambda off: pl.BlockSpec((S, LRU_BLOCK), lambda b, n: (b, off + n))
    vec = lambda rows: pl.BlockSpec((rows, LRU_BLOCK), lambda b, n: (0, n))
    wspec = pl.BlockSpec((None, LRU_BLOCK, LRU_BLOCK), lambda b, n: (n, 0, 0))
    return _pcall(
        body, name=name, grid=(B, LRU_BLOCKS),
        in_specs=[blk(COL_XL), blk(COL_YL), vec(LRU_CONV), vec(1), wspec, wspec, vec(1), vec(1), vec(1)],
        out_specs=[blk(0), blk(0)],
        out_shape=[jax.ShapeDtypeStruct((T, W), F32), jax.ShapeDtypeStruct((T, W), BF16)],
        args=[proj, proj, cw, cb, wr, wi, br, bi, lam],
        scratch=[pltpu.VMEM((S, LRU_BLOCK), F32), pltpu.VMEM((S, LRU_BLOCK), F32)], sem=("parallel", "parallel"))


def _lru_bwd(db_in, h, proj, dproj, cw, cb, wr, wi, br, bi, lam, B, S, name, jobs=()):
    T = B * S
    W = LRU_BLOCKS * LRU_BLOCK

    grid = (LRU_BLOCKS, B)

    def body(dbin_ref, h_ref, x_ref, y_ref, cw_ref, cb_ref, wr_ref, wi_ref, br_ref, bi_ref, lam_ref, _,
             dp_ref, dcw_ref, dcb_ref, dwr_ref, dwi_ref, dbr_ref, dbi_ref, dlam_ref, dx_s, dy_s, wsem,
             an_s, u_s, dh_s):
        n, b = pl.program_id(0), pl.program_id(1)

        def compute(slot):
            x = x_ref[...].astype(F32)
            cw = cw_ref[...]
            wr, wi = wr_ref[...], wi_ref[...]
            lam = lam_ref[...]
            xc, xcb, r, ig, sp, a, sq = _lru_gates(x, cw, cb_ref[...], wr, wi, br_ref[...], bi_ref[...], lam)
            hv = h_ref[...]
            gel, dgel = _gelu_and_grad(y_ref[...].astype(F32))
            dbin = dbin_ref[...].astype(F32)
            dy_s[slot] = (dbin * hv * dgel).astype(BF16)
            an_s[...] = _shift_up(a, 1, 0.0)
            u_s[...] = dbin * gel
            _scan_backward_ref(an_s, u_s, dh_s)
            dh = dh_s[...]
            hprev = _shift_down(hv, 1, 0.0)
            dhs = dh * sq
            d_ig = dhs * xc
            d_xc = dhs * ig
            d_loga = (dh * a) * (hprev - (ig * xc) * (a / sq))
            d_r = d_loga * ((-LRU_C) * sp)
            d_sp = jnp.sum(d_loga * ((-LRU_C) * r), axis=0, keepdims=True)
            dlam = -d_sp * _sigmoid(-lam)
            d_pr = d_r * r * (1.0 - r)
            d_pi = d_ig * ig * (1.0 - ig)
            d_prb, d_pib = d_pr.astype(BF16), d_pi.astype(BF16)
            d_xc = d_xc + _dot(d_prb, wr, NT) + _dot(d_pib, wi, NT)

            @pl.when(b == 0)
            def _():
                for ref in (dcw_ref, dcb_ref, dwr_ref, dwi_ref, dbr_ref, dbi_ref, dlam_ref):
                    ref[...] = jnp.zeros_like(ref)

            dx = cw[LRU_CONV - 1:LRU_CONV, :] * d_xc
            for j in range(LRU_CONV - 1):
                sft = LRU_CONV - 1 - j
                dx = dx + cw[j:j + 1, :] * _shift_up(d_xc, sft, 0.0)
                dcw_ref[j:j + 1, :] += jnp.sum(d_xc * _shift_down(x, sft, 0.0), axis=0, keepdims=True)
            dcw_ref[LRU_CONV - 1:LRU_CONV, :] += jnp.sum(d_xc * x, axis=0, keepdims=True)
            dx_s[slot] = dx.astype(BF16)
            dcb_ref[...] += jnp.sum(d_xc, axis=0, keepdims=True)
            dwr_ref[...] += _dot(xcb, d_prb, TN)
            dwi_ref[...] += _dot(xcb, d_pib, TN)
            dbr_ref[...] += jnp.sum(d_pr, axis=0, keepdims=True)
            dbi_ref[...] += jnp.sum(d_pi, axis=0, keepdims=True)
            dlam_ref[...] += dlam

        cols = [(COL_XL + n) * LRU_BLOCK, (COL_YL + n) * LRU_BLOCK]
        _write_pieces(dp_ref, wsem, [dx_s, dy_s], b * S, cols, [n, b], grid, compute)

    blk = lambda off: pl.BlockSpec((S, LRU_BLOCK), lambda n, b: (b, off + n))
    vec = lambda rows: pl.BlockSpec((rows, LRU_BLOCK), lambda n, b: (0, n))
    wspec = pl.BlockSpec((None, LRU_BLOCK, LRU_BLOCK), lambda n, b: (n, 0, 0))
    vshape = lambda rows: jax.ShapeDtypeStruct((rows, W), F32)
    wshape = jax.ShapeDtypeStruct((LRU_BLOCKS, LRU_BLOCK, LRU_BLOCK), F32)
    return _pcall(
        body, name=name, grid=grid,
        in_specs=[blk(0), blk(0), blk(COL_XL), blk(COL_YL), vec(LRU_CONV), vec(1), wspec, wspec, vec(1), vec(1),
                  vec(1), HBM_SPEC],
        out_specs=[HBM_SPEC, vec(LRU_CONV), vec(1), wspec, wspec, vec(1), vec(1), vec(1)],
        out_shape=[jax.ShapeDtypeStruct(dproj.shape, dproj.dtype),
                   vshape(LRU_CONV), vshape(1), wshape, wshape, vshape(1), vshape(1), vshape(1)],
        args=[db_in, h, proj, proj, cw, cb, wr, wi, br, bi, lam, dproj],
        scratch=[pltpu.VMEM((2, S, LRU_BLOCK), BF16), pltpu.VMEM((2, S, LRU_BLOCK), BF16), pltpu.SemaphoreType.DMA((2, 2)),
                 pltpu.VMEM((S, LRU_BLOCK), F32), pltpu.VMEM((S, LRU_BLOCK), F32), pltpu.VMEM((S, LRU_BLOCK), F32)],
        sem=("arbitrary", "arbitrary"), jobs=jobs, alias_in_out={11: 0})


FFN_CT = 256


def _ffn_conv(gate, cw, cb):
    gc = cb + cw[FFN_CONV - 1:FFN_CONV, :] * gate
    for j in range(FFN_CONV - 1):
        gc = gc + cw[j:j + 1, :] * _shift_down(gate, FFN_CONV - 1 - j, 0.0)
    return gc


def _ffn_act_fwd(up, cw, cb, B, S, name):
    T = B * S
    nct = D_FF // FFN_CT

    def body(g_ref, v_ref, cw_ref, cb_ref, f_ref):
        gc = _ffn_conv(g_ref[...].astype(F32), cw_ref[...], cb_ref[...])
        f_ref[...] = (_gelu(gc) * v_ref[...].astype(F32)).astype(BF16)

    return _pcall(
        body, name=name, grid=(B, nct),
        in_specs=[pl.BlockSpec((S, FFN_CT), lambda b, c: (b, c)), pl.BlockSpec((S, FFN_CT), lambda b, c: (b, nct + c)),
                  pl.BlockSpec((FFN_CONV, FFN_CT), lambda b, c: (0, c)), pl.BlockSpec((1, FFN_CT), lambda b, c: (0, c))],
        out_specs=[pl.BlockSpec((S, FFN_CT), lambda b, c: (b, c))],
        out_shape=[jax.ShapeDtypeStruct((T, D_FF), BF16)], args=[up, up, cw, cb], sem=("parallel", "parallel"))[0]


def _ffn_act_bwd(df, up, cw, cb, B, S, name, jobs=()):
    T = B * S
    nct = D_FF // FFN_CT

    grid = (nct, B)

    def body(df_ref, g_ref, v_ref, cw_ref, cb_ref, du_ref, dcw_ref, dcb_ref, dg_s, dv_s, wsem):
        c, b = pl.program_id(0), pl.program_id(1)

        def compute(slot):
            gate = g_ref[...].astype(F32)
            cw = cw_ref[...]
            gc = _ffn_conv(gate, cw, cb_ref[...])
            gel, dgel = _gelu_and_grad(gc)
            dfv = df_ref[...].astype(F32)
            dv_s[slot] = (dfv * gel).astype(BF16)
            dgc = dfv * v_ref[...].astype(F32) * dgel

            @pl.when(b == 0)
            def _():
                dcw_ref[...] = jnp.zeros_like(dcw_ref)
                dcb_ref[...] = jnp.zeros_like(dcb_ref)

            dgate = cw[FFN_CONV - 1:FFN_CONV, :] * dgc
            for j in range(FFN_CONV - 1):
                sft = FFN_CONV - 1 - j
                dgate = dgate + cw[j:j + 1, :] * _shift_up(dgc, sft, 0.0)
                dcw_ref[j:j + 1, :] += jnp.sum(dgc * _shift_down(gate, sft, 0.0), axis=0, keepdims=True)
            dcw_ref[FFN_CONV - 1:FFN_CONV, :] += jnp.sum(dgc * gate, axis=0, keepdims=True)
            dg_s[slot] = dgate.astype(BF16)
            dcb_ref[...] += jnp.sum(dgc, axis=0, keepdims=True)

        _write_pieces(du_ref, wsem, [dg_s, dv_s], b * S, [c * FFN_CT, (nct + c) * FFN_CT], [c, b], grid, compute)

    blk = pl.BlockSpec((S, FFN_CT), lambda c, b: (b, c))
    return _pcall(
        body, name=name, grid=grid,
        in_specs=[blk, blk, pl.BlockSpec((S, FFN_CT), lambda c, b: (b, nct + c)),
                  pl.BlockSpec((FFN_CONV, FFN_CT), lambda c, b: (0, c)),
                  pl.BlockSpec((1, FFN_CT), lambda c, b: (0, c))],
        out_specs=[HBM_SPEC, pl.BlockSpec((FFN_CONV, FFN_CT), lambda c, b: (0, c)),
                   pl.BlockSpec((1, FFN_CT), lambda c, b: (0, c))],
        out_shape=[jax.ShapeDtypeStruct((T, 2 * D_FF), BF16),
                   jax.ShapeDtypeStruct((FFN_CONV, D_FF), F32), jax.ShapeDtypeStruct((1, D_FF), F32)],
        args=[df, up, up, cw, cb],
        scratch=[pltpu.VMEM((2, S, FFN_CT), BF16), pltpu.VMEM((2, S, FFN_CT), BF16), pltpu.SemaphoreType.DMA((2, 2))],
        sem=("arbitrary", "arbitrary"), jobs=jobs)


def _rs_add(g, recv, mode, core, name, also_bf16=False):
    shard = tuple(recv.shape[1:])
    if mode == "mid":
        a, e, c2 = shard
        g_in = g.reshape(a, N_DEV, e, c2)
        grid = (4, 1)
        g_spec = pl.BlockSpec((a, None, e, c2), lambda k, i, c_ref: (0, 2 * k + c_ref[0], 0, 0))
        r_spec = pl.BlockSpec((None, a, e, c2), lambda k, i, c_ref: (k, 0, 0, 0))
    else:
        R, C = shard
        tr = _row_tile(R, 512)
        grid = (4, R // tr)
        if mode == "rows":
            g_in = g.reshape(N_DEV, R, C)
            g_spec = pl.BlockSpec((None, tr, C), lambda k, i, c_ref: (2 * k + c_ref[0], i, 0))
        else:
            g_in = g
            g_spec = pl.BlockSpec((tr, C), lambda k, i, c_ref: (i, 2 * k + c_ref[0]))
        r_spec = pl.BlockSpec((None, tr, C), lambda k, i, c_ref: (k, i, 0))

    def body(c_ref, g_ref, r_ref, o_ref, *ob_ref):
        s = g_ref[...] + r_ref[...]
        o_ref[...] = s
        if also_bf16:
            ob_ref[0][...] = s.astype(BF16)

    out_shape = jax.ShapeDtypeStruct(recv.shape, recv.dtype)
    return pl.pallas_call(
        body, name=name,
        grid_spec=pltpu.PrefetchScalarGridSpec(
            num_scalar_prefetch=1, grid=grid, in_specs=[g_spec, r_spec],
            out_specs=[r_spec, r_spec] if also_bf16 else r_spec),
        out_shape=[out_shape, jax.ShapeDtypeStruct(recv.shape, BF16)] if also_bf16 else out_shape,
        compiler_params=pltpu.CompilerParams(dimension_semantics=("parallel", "parallel"),
                                             vmem_limit_bytes=VMEM_LIMIT),
    )(core, g_in, recv)


def _adam_update(gv, w, m, v):
    nm = ADAM_B1 * m + (1.0 - ADAM_B1) * gv
    nv = ADAM_B2 * v + (1.0 - ADAM_B2) * (gv * gv)
    m_hat = nm / (1.0 - ADAM_B1 ** ADAM_STEP)
    v_hat = nv / (1.0 - ADAM_B2 ** ADAM_STEP)
    delta = -ADAM_LR * (m_hat / (jnp.sqrt(v_hat) + ADAM_EPS) + ADAM_WD * w)
    return delta, nm, nv


def _adamw_shard(partial, recv, w, m, v, chip, name):
    shape = tuple(w.shape)
    tr = _row_tile(shape[0], 256)
    rest = shape[1:]
    zeros = (0,) * len(rest)
    tile = pl.BlockSpec((tr,) + rest, lambda i, s: (i,) + zeros)

    def body(_, p_ref, r_ref, w_ref, m_ref, v_ref, g_ref, d_ref, nm_ref, nv_ref):
        gv = p_ref[...] + r_ref[0].astype(F32) + r_ref[1].astype(F32) + r_ref[2].astype(F32)
        g_ref[...] = gv
        d_ref[...], nm_ref[...], nv_ref[...] = _adam_update(gv, w_ref[...], m_ref[...], v_ref[...])

    grid_spec = pltpu.PrefetchScalarGridSpec(
        num_scalar_prefetch=1, grid=(shape[0] // tr,),
        in_specs=[pl.BlockSpec((None, tr) + rest, lambda i, s: (s[0], i) + zeros),
                  pl.BlockSpec((3, tr) + rest, lambda i, s: (0, i) + zeros), tile, tile, tile],
        out_specs=[tile] * 4)
    return pl.pallas_call(
        body, name=name, grid_spec=grid_spec, out_shape=[jax.ShapeDtypeStruct(shape, F32)] * 4,
        compiler_params=pltpu.CompilerParams(dimension_semantics=("parallel",), vmem_limit_bytes=VMEM_LIMIT),
    )(chip, partial, recv, w, m, v)


SMALL_LANES = 1024


def _small_rows(shape):
    r, w = shape
    return r * max(1, w // SMALL_LANES)


def _small_allreduce(parts, name):
    n = len(parts)
    shapes = [tuple(p.shape) for p in parts]
    offs, total = [], 0
    for s in shapes:
        offs.append(total)
        total += _small_rows(s)
    rows = -(-total // 8) * 8

    def body(*refs):
        p_refs, o_refs = refs[:n], refs[n:2 * n]
        buf, tot, send_sems, recv_sems = refs[2 * n:]
        x, y, c = _mesh_pos()
        me, sibling = (x, y, c), (x, y, 1 - c)
        chips = _other_chips(x, y)

        def slot(px, py, pc):
            return buf.at[4 * px + 2 * py + pc]

        def copy(k, block, to):
            return _remote(slot(*block), slot(*block), send_sems.at[k], recv_sems.at[k], to)

        tot[...] = jnp.zeros_like(tot)
        for p_ref, (r, w), off in zip(p_refs, shapes, offs):
            wl = min(w, SMALL_LANES)
            for part in range(max(1, w // SMALL_LANES)):
                tot[pl.ds(off + part * r, r), pl.ds(0, wl)] = p_ref[:, pl.ds(part * SMALL_LANES, wl)]
        buf[4 * x + 2 * y + c] = tot[...]
        first = [copy(0, me, sibling)] + [copy(1 + j, me, (*chip, c)) for j, chip in enumerate(chips)]
        for cp in first:
            cp.start()
        passed = [copy(4 + j, (*chip, c), sibling) for j, chip in enumerate(chips)]
        for j, chip in enumerate(chips):
            copy(1 + j, (*chip, c), me).wait_recv()
            passed[j].start()
        copy(0, sibling, me).wait_recv()
        for j, chip in enumerate(chips):
            copy(4 + j, (*chip, 1 - c), me).wait_recv()
        for cp in first + passed:
            cp.wait_send()
        acc = buf[0]
        for d in range(1, N_DEV):
            acc = acc + buf[d]
        tot[...] = acc
        for o_ref, (r, w), off in zip(o_refs, shapes, offs):
            wl = min(w, SMALL_LANES)
            for part in range(max(1, w // SMALL_LANES)):
                o_ref[:, pl.ds(part * SMALL_LANES, wl)] = tot[pl.ds(off + part * r, r), pl.ds(0, wl)]

    vm = pl.BlockSpec(memory_space=pltpu.VMEM)
    return pl.pallas_call(
        body, name=name,
        in_specs=[vm] * n, out_specs=[vm] * n,
        out_shape=[jax.ShapeDtypeStruct(s, F32) for s in shapes],
        scratch_shapes=[pltpu.VMEM((N_DEV, rows, SMALL_LANES), F32), pltpu.VMEM((rows, SMALL_LANES), F32),
                        pltpu.SemaphoreType.DMA((7,)), pltpu.SemaphoreType.DMA((7,))],
    )(*parts)


def _adamw_small(gs, ws, ms, vs, name):
    n = len(gs)

    def body(*refs):
        g_r, w_r, m_r, v_r = refs[:n], refs[n:2 * n], refs[2 * n:3 * n], refs[3 * n:4 * n]
        d_r, nm_r, nv_r = refs[4 * n:5 * n], refs[5 * n:6 * n], refs[6 * n:7 * n]
        for i in range(n):
            d_r[i][...], nm_r[i][...], nv_r[i][...] = _adam_update(g_r[i][...], w_r[i][...], m_r[i][...], v_r[i][...])

    vm = pl.BlockSpec(memory_space=pltpu.VMEM)
    shapes = [jax.ShapeDtypeStruct(w.shape, F32) for w in ws]
    outs = pl.pallas_call(body, name=name, in_specs=[vm] * (4 * n), out_specs=[vm] * (3 * n),
                          out_shape=shapes * 3)(*gs, *ws, *ms, *vs)
    return outs[:n], outs[n:2 * n], outs[2 * n:]


FIRST = [("w_in", (1024, 896), "cols")]
LATE = [("lru_w_r", (4, 32, 256), "mid"), ("lru_w_i", (4, 32, 256), "mid"),
        ("w_ret_o", (128, 1024), "rows"), ("w_lru_o", (128, 1024), "rows"), ("w_out", (128, 1024), "rows"),
        ("ffn_w_up", (1024, 768), "cols"), ("ffn_w_down", (384, 1024), "rows")]
BIG = FIRST + LATE
SMALL_SHARDED = [("merge_gate_b", (2, 128), "cols"), ("lru_conv_w", (4, 128), "cols"), ("lru_b_r", (4, 32), "stack"),
                 ("lru_b_i", (4, 32), "stack"), ("ffn_conv_w", (3, 384), "cols")]
REPLICATED = [("norm1_w", (1, 1024)), ("ret_gn_w", (1, 1024)), ("lru_conv_b", (1, 1024)), ("lru_lambda", (1, 1024)),
              ("norm2_w", (1, 1024)), ("ffn_conv_b", (1, 3072)), ("norm_f_w", (1, 1024))]
MODE = {n: m for n, _, m in BIG}
SHARD = {n: s for n, s, _ in BIG}


def _local_step(x3, positions, target3, first_shards, ws, late_shards, core):
    B, S, D = x3.shape
    T = B * S
    x = x3.reshape(T, D)
    target = target3.reshape(T, D)
    tm = min(512, T)
    big = min(1024, T)
    big2 = min(2048, T)
    big4 = min(4096, T)

    half = RET_DK // 2
    inv_freq = ROPE_BASE ** (-jnp.arange(half, dtype=F32) / half)
    inv2 = jnp.concatenate([inv_freq, inv_freq]).reshape(1, RET_DK)
    log_gamma = jnp.log1p(-jnp.power(2.0, -5.0 - jnp.arange(RET_HEADS, dtype=F32)))
    lgam = jnp.broadcast_to(log_gamma[:, None, None], (RET_HEADS, 8, LANES))
    pos_col = positions.astype(F32).reshape(T, 1)
    late = LATE + SMALL_SHARDED
    late_names = [n for n, _, _ in late]
    late_modes = [m for _, _, m in late]
    late_shapes = [s for _, s, _ in late]
    first = FIRST
    first_modes = [m for _, _, m in first]

    cos2, sin2s, h1, *first_part = _rope_and_norm(pos_col, inv2, x, ws["norm1_w"], tm, "rope_norm1_fwd",
                                                  jobs=[_ag_first_job(first_shards, first_modes)])
    first_full = _pcall(lambda: None, name="gather_first_pass_on", grid=(1,), in_specs=[], out_specs=[], out_shape=[],
                        args=[], sem=("arbitrary",),
                        jobs=[_ag_second_job(first_part, first_modes, [s for _, s, _ in first])])
    gathered = dict(zip([n for n, _, _ in first], first_full))
    wb = {n: gathered[n] for n, _, _ in FIRST}
    proj, *late_part = _matmul(h1, wb["w_in"], "nn", BF16, big2, 1024, 1024, "proj_fwd",
                               jobs=[_ag_first_job(late_shards, late_modes)])
    o, a_in, *late_full = _retention_fwd(proj, cos2, sin2s, lgam, ws["ret_gn_w"], B, S, "retention_fwd",
                                         jobs=[_ag_second_job(late_part, late_modes, late_shapes)])
    gathered = dict(zip(late_names, late_full))
    wb = dict(wb, **{n: gathered[n] for n, _, _ in LATE})
    ws = dict(ws, **{n: gathered[n] for n, _, _ in SMALL_SHARDED})
    for n in ("lru_b_r", "lru_b_i"):
        ws[n] = jnp.transpose(ws[n], (1, 0, 2)).reshape(1, LRU_BLOCKS * LRU_BLOCK)
    hl, b_in = _lru_fwd(proj, ws["lru_conv_w"], ws["lru_conv_b"], wb["lru_w_r"], wb["lru_w_i"],
                        ws["lru_b_r"], ws["lru_b_i"], ws["lru_lambda"], B, S, "lru_fwd")
    x1, mix, h2, ya, yb = _mix_fwd(a_in, b_in, proj, x, wb["w_ret_o"], wb["w_lru_o"], wb["w_out"],
                                   ws["merge_gate_b"], ws["norm2_w"], tm, "mix_fwd")
    up = _matmul(h2, wb["ffn_w_up"], "nn", BF16, big2, 1024, 1024, "ffn_up_fwd")[0]
    f = _ffn_act_fwd(up, ws["ffn_conv_w"], ws["ffn_conv_b"], B, S, "ffn_act_fwd")
    x2 = _matmul(f, wb["ffn_w_down"], "nn", F32, big, 1024, D_FF, "ffn_down_fwd", add=x1)[0]
    dx2, dx2b, loss_acc, d_norm_f = _loss_head(x2, target, ws["norm_f_w"], tm, "loss_head")

    g, rs = {}, {}

    def stage1(names, grads):
        return _rs_sibling_job(grads, [MODE[n] for n in names], [SHARD[n] for n in names])

    def add(names, grads, recvs):
        return [_rs_add(gr, r, MODE[n], core, "rs_add_" + n) for n, gr, r in zip(names, grads, recvs)]

    g["norm_f_w"] = d_norm_f
    g_down = _matmul(f, dx2b, "tn", F32, 1024, 1024, big4, "ffn_down_bwd_w")[0]
    df, s1_down = _matmul(dx2b, wb["ffn_w_down"], "nt", BF16, big2, 1024, 1024, "ffn_down_bwd_x",
                          jobs=[stage1(["ffn_w_down"], [g_down])])
    p_down = add(["ffn_w_down"], [g_down], [s1_down])
    dup, g["ffn_conv_w"], g["ffn_conv_b"], s2_down = _ffn_act_bwd(
        df, up, ws["ffn_conv_w"], ws["ffn_conv_b"], B, S, "ffn_act_bwd", jobs=[_rs_chip_job(p_down)])
    rs["ffn_w_down"] = (p_down[0], s2_down)

    g_up = _matmul(h2, dup, "tn", F32, 1024, 1024, big4, "ffn_up_bwd_w")[0]
    dh2, s1_up = _matmul(dup, wb["ffn_w_up"], "nt", BF16, big, 1024, D_FF, "ffn_up_bwd_x",
                         jobs=[stage1(["ffn_w_up"], [g_up])])
    p_up = add(["ffn_w_up"], [g_up], [s1_up])
    dx1, dx1b, g["norm2_w"] = _rmsnorm_bwd_add(dx2, dh2, x1, ws["norm2_w"], tm, "norm2_bwd", True)
    da_in, db_in, dya, dyb, dproj, g["merge_gate_b"] = _mix_bwd(
        dx1b, ya, yb, proj, wb["w_ret_o"], wb["w_lru_o"], wb["w_out"], ws["merge_gate_b"], tm, "mix_bwd")

    mid_names = ["w_out", "w_ret_o", "w_lru_o"]
    g_mid = [_matmul(mix, dx1b, "tn", F32, 1024, 1024, big4, "w_out_bwd_w")[0],
             _matmul(a_in, dya, "tn", F32, 1024, 1024, big4, "w_ret_o_bwd_w")[0],
             _matmul(b_in, dyb, "tn", F32, 1024, 1024, big4, "w_lru_o_bwd_w")[0]]
    (dproj, g["lru_conv_w"], g["lru_conv_b"], g_wr, g_wi, g["lru_b_r"], g["lru_b_i"], g["lru_lambda"], s2_up,
     *s1_mid) = _lru_bwd(db_in, hl, proj, dproj, ws["lru_conv_w"], ws["lru_conv_b"], wb["lru_w_r"], wb["lru_w_i"],
                         ws["lru_b_r"], ws["lru_b_i"], ws["lru_lambda"], B, S, "lru_bwd",
                         jobs=[_rs_chip_job(p_up), stage1(mid_names, g_mid)])
    rs["ffn_w_up"] = (p_up[0], s2_up)
    p_mid = add(mid_names, g_mid, s1_mid)
    lru_names = ["lru_w_r", "lru_w_i"]
    dproj, g["ret_gn_w"], *rest = _retention_bwd(
        da_in, o, proj, dproj, cos2, sin2s, lgam, ws["ret_gn_w"], B, S, "retention_bwd",
        jobs=[_rs_chip_job(p_mid), stage1(lru_names, [g_wr, g_wi])])
    s2_mid, s1_lru = rest[:3], rest[3:]
    for n, p, r in zip(mid_names, p_mid, s2_mid):
        rs[n] = (p, r)
    p_lru = add(lru_names, [g_wr, g_wi], s1_lru)

    g_in, *s2_lru = _matmul(h1, dproj, "tn", F32, 1024, 1024, big4, "proj_bwd_w", jobs=[_rs_chip_job(p_lru)])
    for n, p, r in zip(lru_names, p_lru, s2_lru):
        rs[n] = (p, r)
    s1_in = _pcall(lambda: None, name="rs_sibling_w_in", grid=(1,), in_specs=[], out_specs=[], out_shape=[], args=[],
                   sem=("arbitrary",), jobs=[stage1(["w_in"], [g_in])])
    p_in, p_in_bf16 = _rs_add(g_in, s1_in[0], MODE["w_in"], core, "rs_add_w_in", also_bf16=True)
    dh1, s2_in = _matmul(dproj, wb["w_in"], "nt", BF16, big, 1024, D_IN // 2, "proj_bwd_x",
                         jobs=[_rs_chip_job([p_in_bf16])])
    grad_x, g["norm1_w"] = _rmsnorm_bwd_add(dx1, dh1, x, ws["norm1_w"], tm, "norm1_bwd", False)
    rs["w_in"] = (p_in, s2_in)
    return loss_acc, grad_x.reshape(B, S, D), g, rs


def kernel(x, positions, norm1_w, w_in, merge_gate_b, ret_gn_w, w_ret_o, lru_conv_w, lru_conv_b, lru_w_r, lru_b_r, lru_w_i, lru_b_i, lru_lambda, w_lru_o, w_out, norm2_w, ffn_w_up, ffn_conv_w, ffn_conv_b, ffn_w_down, norm_f_w, loss_target, m_norm1_w, m_w_in, m_merge_gate_b, m_ret_gn_w, m_w_ret_o, m_lru_conv_w, m_lru_conv_b, m_lru_w_r, m_lru_b_r, m_lru_w_i, m_lru_b_i, m_lru_lambda, m_w_lru_o, m_w_out, m_norm2_w, m_ffn_w_up, m_ffn_conv_w, m_ffn_conv_b, m_ffn_w_down, m_norm_f_w, v_norm1_w, v_w_in, v_merge_gate_b, v_ret_gn_w, v_w_ret_o, v_lru_conv_w, v_lru_conv_b, v_lru_w_r, v_lru_b_r, v_lru_w_i, v_lru_b_i, v_lru_lambda, v_w_lru_o, v_w_out, v_norm2_w, v_ffn_w_up, v_ffn_conv_w, v_ffn_conv_b, v_ffn_w_down, v_norm_f_w):
    names = ["norm1_w", "w_in", "merge_gate_b", "ret_gn_w", "w_ret_o", "lru_conv_w", "lru_conv_b", "lru_w_r", "lru_b_r",
             "lru_w_i", "lru_b_i", "lru_lambda", "w_lru_o", "w_out", "norm2_w", "ffn_w_up", "ffn_conv_w", "ffn_conv_b",
             "ffn_w_down", "norm_f_w"]
    w_args = [norm1_w, w_in, merge_gate_b, ret_gn_w, w_ret_o, lru_conv_w, lru_conv_b, lru_w_r, lru_b_r, lru_w_i, lru_b_i,
              lru_lambda, w_lru_o, w_out, norm2_w, ffn_w_up, ffn_conv_w, ffn_conv_b, ffn_w_down, norm_f_w]
    m_args = [m_norm1_w, m_w_in, m_merge_gate_b, m_ret_gn_w, m_w_ret_o, m_lru_conv_w, m_lru_conv_b, m_lru_w_r, m_lru_b_r,
              m_lru_w_i, m_lru_b_i, m_lru_lambda, m_w_lru_o, m_w_out, m_norm2_w, m_ffn_w_up, m_ffn_conv_w, m_ffn_conv_b,
              m_ffn_w_down, m_norm_f_w]
    v_args = [v_norm1_w, v_w_in, v_merge_gate_b, v_ret_gn_w, v_w_ret_o, v_lru_conv_w, v_lru_conv_b, v_lru_w_r, v_lru_b_r,
              v_lru_w_i, v_lru_b_i, v_lru_lambda, v_w_lru_o, v_w_out, v_norm2_w, v_ffn_w_up, v_ffn_conv_w, v_ffn_conv_b,
              v_ffn_w_down, v_norm_f_w]
    orig_shape = {n: a.shape for n, a in zip(names, w_args)}
    local_shape = {n: s for n, s, _ in BIG + SMALL_SHARDED}
    local_shape.update({n: s for n, s in REPLICATED})
    W = {n: a.reshape(local_shape[n]) for n, a in zip(names, w_args)}
    M = {n: a.reshape(local_shape[n]) for n, a in zip(names, m_args)}
    V = {n: a.reshape(local_shape[n]) for n, a in zip(names, v_args)}

    xi, yi, ci = _mesh_pos()
    dev = 4 * xi + 2 * yi + ci
    chip = (2 * xi + yi).astype(jnp.int32).reshape(1)
    core = ci.astype(jnp.int32).reshape(1)

    small_names = [n for n, _, _ in SMALL_SHARDED]
    first_shards = [W[n].astype(BF16) for n, _, _ in FIRST]
    late_shards = [W[n].astype(BF16) for n, _, _ in LATE] + [W[n] for n in small_names]
    rep = {n: W[n] for n, _ in REPLICATED}
    loss_acc, grad_x, g, rs = _local_step(x, positions, loss_target, first_shards, rep, late_shards, core)

    G_out, D_out, M_out, V_out = {}, {}, {}, {}
    for n, _, _ in BIG:
        G_out[n], D_out[n], M_out[n], V_out[n] = _adamw_shard(rs[n][0], rs[n][1], W[n], M[n], V[n], chip, "adamw_" + n)

    rep_names = [n for n, _ in REPLICATED]
    red_names = rep_names + small_names
    red = _small_allreduce([g[n] for n in red_names] + [loss_acc[0:1, :]], "allreduce_small_grads")
    loss = red[-1][0, 0]
    gs = dict(zip(red_names, red[:-1]))
    for n, s, mode in SMALL_SHARDED:
        if mode == "cols":
            gs[n] = lax.dynamic_slice_in_dim(gs[n], dev * s[1], s[1], axis=1)
        else:
            full = gs[n].reshape(LRU_BLOCKS, LRU_BLOCK)
            gs[n] = lax.dynamic_slice_in_dim(full, dev * s[1], s[1], axis=1)
    d2, m2, v2 = _adamw_small([gs[n] for n in red_names], [W[n] for n in red_names], [M[n] for n in red_names],
                              [V[n] for n in red_names], "adamw_small")
    for i, n in enumerate(red_names):
        G_out[n], D_out[n], M_out[n], V_out[n] = gs[n], d2[i], m2[i], v2[i]

    outs = [loss, grad_x]
    for group in (G_out, D_out, M_out, V_out):
        outs += [group[n].reshape(orig_shape[n]) for n in names]
    return tuple(outs)
```
